```python
import math
import jax
import jax.numpy as jnp
from jax import lax
import numpy as np

D_MODEL = 2048
BATCH = 2
SEQ = 4096
DEPTH = 1
DEC_BATCH = 8
DEC_SEQ = 1
PAST_LEN = 16384
PAGE_SIZE = 128

RET_HEADS = 8
RET_DK = 256
RET_DV = 256
RET_CHUNK = 128
ROPE_BASE = 10000.0
NSA_HEADS = 16
NSA_KV_HEADS = 4
NSA_GROUP = NSA_HEADS // NSA_KV_HEADS
NSA_DK = 128
CMP_BLOCK = 32
CMP_STRIDE = 16
SEL_BLOCK = 64
SEL_TOPK = 16
SEL_QBLOCK = 64
WINDOW = 512
WIN_QBLOCK = 128
MEM_TOKENS = 256
MEM_HEADS = 4
MEM_DH = 384
REL_BUCKETS = 32
REL_MAX_EXACT = 16
REL_MAX_DIST = 128
N_BRANCHES = 3
EPS = 1e-6
NEG_INF = -1e30
FORCE_SCORE = 1e4

RET_W = RET_HEADS * RET_DV
NSA_W = NSA_HEADS * NSA_DK
KV_W = NSA_KV_HEADS * NSA_DK
MEM_W = MEM_HEADS * MEM_DH
IN_SPLITS = (RET_HEADS * RET_DK, RET_HEADS * RET_DK, RET_W, RET_W, NSA_W, 2 * KV_W, 2 * KV_W, 2 * KV_W,
             3 * NSA_HEADS, NSA_W, MEM_W, N_BRANCHES * D_MODEL)
IN_WIDTH = sum(IN_SPLITS)

kernel_name = 'hybrid_retention_nsa_memory_step'


def rmsnorm(x, g):
    xf = x.astype(jnp.float32)
    y = xf * lax.rsqrt(jnp.mean(xf * xf, -1, keepdims=True) + EPS)
    return (y * g.astype(jnp.float32)).astype(x.dtype)


def masked_softmax(s, mask):
    s = jnp.where(mask, s.astype(jnp.float32), NEG_INF)
    m = jnp.max(s, -1, keepdims=True)
    p = jnp.where(mask, jnp.exp(s - m), 0.0)
    return p / jnp.maximum(jnp.sum(p, -1, keepdims=True), 1e-30)


def rel_bucket(dist):
    n = jnp.maximum(dist, 0)
    nf = jnp.maximum(n, 1).astype(jnp.float32)
    scale = (REL_BUCKETS - REL_MAX_EXACT) / math.log(REL_MAX_DIST / REL_MAX_EXACT)
    large = REL_MAX_EXACT + (jnp.log(nf / REL_MAX_EXACT) * scale).astype(jnp.int32)
    large = jnp.minimum(large, REL_BUCKETS - 1)
    return jnp.where(n < REL_MAX_EXACT, n, large)


def rope(x, pos):
    half = x.shape[-1] // 2
    freq = jnp.power(ROPE_BASE, -jnp.arange(half, dtype=jnp.float32) / half)
    ang = pos.astype(jnp.float32)[:, None] * freq[None, :]
    cos, sin = jnp.cos(ang)[:, None, :], jnp.sin(ang)[:, None, :]
    x1 = x[..., :half].astype(jnp.float32)
    x2 = x[..., half:].astype(jnp.float32)
    return jnp.concatenate([x1 * cos - x2 * sin, x1 * sin + x2 * cos], -1).astype(x.dtype)


def pad_rows(rows, m):
    t = rows.shape[1]
    tp = -(-t // m) * m
    return jnp.pad(rows, ((0, 0), (0, tp - t)) + ((0, 0),) * (rows.ndim - 2))


def split_columns(z):
    pieces, start = [], 0
    for w in IN_SPLITS:
        pieces.append(z[..., start:start + w])
        start += w
    return pieces


def retention(q, k, v, state0, chunk):
    B, L, H, dk = q.shape
    dv = v.shape[-1]
    n = L // chunk
    log_g = jnp.log1p(-jnp.exp2(-5.0 - jnp.arange(H, dtype=jnp.float32)))
    i = jnp.arange(chunk, dtype=jnp.float32)
    diff = i[:, None] - i[None, :]
    dmat = jnp.where(diff >= 0, jnp.exp(log_g[:, None, None] * jnp.maximum(diff, 0.0)), 0.0)
    xi = jnp.exp(log_g[None, :] * (i[:, None] + 1.0))[None, :, :, None]
    zeta = jnp.exp(log_g[None, :] * (chunk - 1.0 - i[:, None]))[None, :, :, None]
    g_chunk = jnp.exp(log_g * chunk)[None, :, None, None]

    def to_chunks(t):
        return jnp.moveaxis(t.astype(jnp.float32).reshape(B, n, chunk, H, t.shape[-1]), 1, 0)

    def step(s, inp):
        qc, kc, vc = inp
        inner = jnp.einsum('bihd,bjhd->bhij', qc, kc) * dmat
        o = jnp.einsum('bhij,bjhe->bihe', inner, vc) + jnp.einsum('bihd,bhde->bihe', qc, s) * xi
        s = s * g_chunk + jnp.einsum('bjhd,bjhe->bhde', kc * zeta, vc)
        return s, o

    s_fin, o = lax.scan(step, state0.astype(jnp.float32), (to_chunks(q), to_chunks(k), to_chunks(v)))
    return jnp.moveaxis(o, 0, 1).reshape(B, L, H, dv), s_fin


def head_norm(o, g):
    B, L, H, dv = o.shape
    oc = o - jnp.mean(o, -1, keepdims=True)
    y = oc * lax.rsqrt(jnp.mean(oc * oc, -1, keepdims=True) + EPS)
    return y.reshape(B, L, H * dv) * g.astype(jnp.float32)


def compress(rows, pos_emb, w1, w2):
    B, T, Hk, d = rows.shape
    halves = rows.reshape(B, T // CMP_STRIDE, CMP_STRIDE, Hk, d)
    a = jnp.einsum('bnphd,pde->bnhe', halves, w1[:CMP_STRIDE])
    c = jnp.einsum('bnphd,pde->bnhe', halves, w1[CMP_STRIDE:])
    pre = a[:, :-1] + c[:, 1:] + jnp.einsum('pd,pde->e', pos_emb, w1)
    return jnp.einsum('bnhe,ef->bnhf', jax.nn.gelu(pre), w2)


def selected_attention(qg, q_pos, idx, kv_sel, tab):
    B, Q, Hk, G, d = qg.shape
    ns = kv_sel.shape[1] // SEL_BLOCK
    kvb = jnp.moveaxis(kv_sel.reshape(B, ns, SEL_BLOCK, 2, Hk, d), 4, 1)
    tab_h = jnp.transpose(tab, (1, 2, 0))
    qb = SEL_QBLOCK if Q % SEL_QBLOCK == 0 else Q
    nq = Q // qb
    scale = d ** -0.5

    def one(args):
        qc, pc, ic = args
        kv = jax.vmap(jax.vmap(lambda t, i: t[i]))(kvb, ic)
        kv = kv.reshape(B, Hk, qb, -1, 2, d)
        kpos = (ic[..., None] * SEL_BLOCK + jnp.arange(SEL_BLOCK)).reshape(B, Hk, qb, -1)
        dist = pc[None, None, :, None] - kpos
        bias = jax.vmap(lambda th, bh: th[:, bh], in_axes=(0, 1), out_axes=0)(tab_h, rel_bucket(dist))
        bias = jnp.transpose(bias, (2, 0, 1, 3, 4))
        s = jnp.einsum('bqhgd,bhqkd->bhgqk', qc, kv[..., 0, :]).astype(jnp.float32) * scale + bias
        p = masked_softmax(s, (dist >= 0)[:, :, None])
        return jnp.einsum('bhgqk,bhqkd->bqhgd', p.astype(kv.dtype), kv[..., 1, :])

    qs = jnp.moveaxis(qg.reshape(B, nq, qb, Hk, G, d), 1, 0)
    ps = q_pos.reshape(nq, qb)
    ids = jnp.moveaxis(idx.reshape(B, Hk, nq, qb, idx.shape[-1]), 2, 0)
    out = lax.map(one, (qs, ps, ids))
    return jnp.moveaxis(out, 0, 1).reshape(B, Q, Hk, G, d)


def nsa_cmp_sel(qg, q_pos, kv_cmp, kv_sel, rel_table, cmp_pos, w_cmp1, w_cmp2):
    B, Q, Hk, G, d = qg.shape
    T = kv_cmp.shape[1]
    tab = rel_table.reshape(REL_BUCKETS, Hk, G)
    k_c = compress(kv_cmp[:, :, 0], cmp_pos[0], w_cmp1[0], w_cmp2[0])
    v_c = compress(kv_cmp[:, :, 1], cmp_pos[1], w_cmp1[1], w_cmp2[1])
    nc = k_c.shape[1]
    c_end = jnp.arange(nc) * CMP_STRIDE + CMP_BLOCK - 1
    dist_c = q_pos[:, None] - c_end[None, :]
    bias_c = jnp.moveaxis(tab[rel_bucket(dist_c)], (2, 3), (0, 1))
    s_c = jnp.einsum('bqhgd,bnhd->bhgqn', qg, k_c).astype(jnp.float32) * (d ** -0.5) + bias_c
    p_c = masked_softmax(s_c, dist_c >= 0)
    out_c = jnp.einsum('bhgqn,bnhd->bqhgd', p_c.astype(v_c.dtype), v_c)
    ns = T // SEL_BLOCK
    ci = jnp.arange(nc)[:, None] * CMP_STRIDE
    sj = jnp.arange(ns)[None, :] * SEL_BLOCK
    overlap = ((ci < sj + SEL_BLOCK) & (ci + CMP_BLOCK > sj)).astype(jnp.float32)
    imp = jnp.einsum('bhgqn,ns->bhqs', p_c, overlap)
    blk = jnp.arange(ns)[None, :]
    cur = q_pos[:, None] // SEL_BLOCK
    valid = blk * SEL_BLOCK <= q_pos[:, None]
    forced = (blk == 0) | (blk == cur) | (blk == cur - 1)
    imp = jnp.where(valid, jnp.where(forced, FORCE_SCORE, imp), NEG_INF)
    _, idx = lax.top_k(imp, min(SEL_TOPK, ns))
    out_s = selected_attention(qg, q_pos, idx, kv_sel, tab)
    return out_c, out_s


def window_attention(qg, q_pos, k, v, k_pos, rel_table):
    B, Q, Hk, G, d = qg.shape
    tab = rel_table.reshape(REL_BUCKETS, Hk, G)
    dist = q_pos[:, None] - k_pos[None, :]
    mask = (dist >= 0) & (dist < WINDOW) & (k_pos[None, :] >= 0)
    bias = jnp.moveaxis(tab[rel_bucket(dist)], (2, 3), (0, 1))
    s = jnp.einsum('bqhgd,bkhd->bhgqk', qg, k).astype(jnp.float32) * (d ** -0.5) + bias
    p = masked_softmax(s, mask)
    return jnp.einsum('bhgqk,bkhd->bqhgd', p.astype(v.dtype), v)


def window_prompt(qg, kv, rel_table):
    B, L, Hk, G, d = qg.shape
    nb = L // WIN_QBLOCK
    kp = jnp.pad(kv, ((0, 0), (WINDOW, 0), (0, 0), (0, 0), (0, 0)))
    k_pos = jnp.arange(nb)[:, None] * WIN_QBLOCK + jnp.arange(WINDOW + WIN_QBLOCK)[None, :] - WINDOW
    kvb = kp[:, k_pos + WINDOW]
    q_pos = jnp.arange(L).reshape(nb, WIN_QBLOCK)
    qb = qg.reshape(B, nb, WIN_QBLOCK, Hk, G, d)
    f = jax.vmap(window_attention, in_axes=(1, 0, 1, 1, 0, None), out_axes=1)
    out = f(qb, q_pos, kvb[:, :, :, 0], kvb[:, :, :, 1], k_pos, rel_table)
    return out.reshape(B, L, Hk, G, d)


def memory_attention(q, mem_kv):
    s = jnp.einsum('bqhd,bmhd->bhqm', q, mem_kv[:, :, 0]).astype(jnp.float32) * (MEM_DH ** -0.5)
    p = jax.nn.softmax(s, -1)
    return jnp.einsum('bhqm,bmhd->bqhd', p.astype(mem_kv.dtype), mem_kv[:, :, 1])


def project(x, pos, norm_pre, w_in):
    B, L, _ = x.shape
    h = rmsnorm(x, norm_pre)
    z = jnp.einsum('bld,de->ble', h, w_in)
    rq, rk, rv, rg, nq, kvc, kvs, kvw, ng, nsl, mq, mg = split_columns(z)
    rq = rope(rq.reshape(B, L, RET_HEADS, RET_DK), pos)
    rk = rope(rk.reshape(B, L, RET_HEADS, RET_DK), pos) * (RET_DK ** -0.5)
    rv = rv.reshape(B, L, RET_HEADS, RET_DV)
    qg = nq.reshape(B, L, NSA_KV_HEADS, NSA_GROUP, NSA_DK)
    kvc = kvc.reshape(B, L, 2, NSA_KV_HEADS, NSA_DK)
    kvs = kvs.reshape(B, L, 2, NSA_KV_HEADS, NSA_DK)
    kvw = kvw.reshape(B, L, 2, NSA_KV_HEADS, NSA_DK)
    mq = mq.reshape(B, L, MEM_HEADS, MEM_DH)
    return rq, rk, rv, rg, qg, kvc, kvs, kvw, ng, nsl, mq, mg


def finish(x, o_ret, rg, o_cmp, o_sel, o_win, ng, nsl, o_mem, mg, ret_norm, w_ret_up, w_nsa_up, w_mem_up, w_out, norm_post):
    B, L, D = x.shape
    dt = x.dtype
    y_ret = (head_norm(o_ret, ret_norm) * jax.nn.silu(rg.astype(jnp.float32))).astype(dt) @ w_ret_up
    g = jax.nn.sigmoid(ng.reshape(B, L, NSA_HEADS, 3).astype(jnp.float32))
    o_nsa = (g[..., 0:1] * o_cmp.reshape(B, L, NSA_HEADS, NSA_DK)
             + g[..., 1:2] * o_sel.reshape(B, L, NSA_HEADS, NSA_DK)
             + g[..., 2:3] * o_win.reshape(B, L, NSA_HEADS, NSA_DK))
    y_nsa = (o_nsa.reshape(B, L, NSA_W) * jax.nn.silu(nsl.astype(jnp.float32))).astype(dt) @ w_nsa_up
    y_mem = o_mem.reshape(B, L, MEM_W) @ w_mem_up
    gm = jax.nn.sigmoid(mg.reshape(B, L, N_BRANCHES, D).astype(jnp.float32))
    merged = gm[:, :, 0] * y_ret + gm[:, :, 1] * y_nsa + gm[:, :, 2] * y_mem
    out = merged.astype(dt) @ w_out
    return x + rmsnorm(out, norm_post)


def prompt_layer(x, mem, rel_table, lw):
    (norm_pre, norm_post, norm_mem, w_in, ret_norm, w_ret_up, cmp_pos, w_cmp1, w_cmp2,
     w_nsa_up, w_mem_kv, w_mem_up, w_out) = lw
    B, L, _ = x.shape
    pos = jnp.arange(L)
    rq, rk, rv, rg, qg, kvc, kvs, kvw, ng, nsl, mq, mg = project(x, pos, norm_pre, w_in)
    s0 = jnp.zeros((B, RET_HEADS, RET_DK, RET_DV), jnp.float32)
    o_ret, s_new = retention(rq, rk, rv, s0, RET_CHUNK if L % RET_CHUNK == 0 else L)
    o_cmp, o_sel = nsa_cmp_sel(qg, pos, pad_rows(kvc, SEL_BLOCK), pad_rows(kvs, SEL_BLOCK),
                               rel_table, cmp_pos, w_cmp1, w_cmp2)
    o_win = window_prompt(qg, kvw, rel_table)
    mem_kv = jnp.einsum('bmd,de->bme', rmsnorm(mem, norm_mem), w_mem_kv).reshape(
        B, mem.shape[1], 2, MEM_HEADS, MEM_DH)
    o_mem = memory_attention(mq, mem_kv)
    y = finish(x, o_ret, rg, o_cmp, o_sel, o_win, ng, nsl, o_mem, mg,
               ret_norm, w_ret_up, w_nsa_up, w_mem_up, w_out, norm_post)
    return y, (kvc, kvs, kvw[:, -min(WINDOW, L):], s_new.astype(x.dtype), mem_kv)


def sample_layer(x, layer, cache_cmp_kv, cache_sel_kv, page_table, win_buf, ret_state, mem_kv, rel_table, lw):
    (norm_pre, norm_post, norm_mem, w_in, ret_norm, w_ret_up, cmp_pos, w_cmp1, w_cmp2,
     w_nsa_up, w_mem_kv, w_mem_up, w_out) = lw
    B, L, _ = x.shape
    past = page_table.shape[1] * cache_cmp_kv.shape[2]
    pos = past + jnp.arange(L)
    rq, rk, rv, rg, qg, kvc, kvs, kvw, ng, nsl, mq, mg = project(x, pos, norm_pre, w_in)
    o_ret, s_new = retention(rq, rk, rv, ret_state, L)

    def past_rows(pool, new):
        old = pool[layer, page_table]
        old = old.reshape((B, past) + old.shape[3:])
        return pad_rows(jnp.concatenate([old, new.astype(old.dtype)], 1), SEL_BLOCK)

    o_cmp, o_sel = nsa_cmp_sel(qg, pos, past_rows(cache_cmp_kv, kvc), past_rows(cache_sel_kv, kvs),
                               rel_table, cmp_pos, w_cmp1, w_cmp2)
    nbuf = win_buf.shape[1]
    kv_all = jnp.concatenate([win_buf, kvw.astype(win_buf.dtype)], 1)
    k_pos = past - nbuf + jnp.arange(nbuf + L)
    o_win = window_attention(qg, pos, kv_all[:, :, 0], kv_all[:, :, 1], k_pos, rel_table)
    o_mem = memory_attention(mq, mem_kv)
    y = finish(x, o_ret, rg, o_cmp, o_sel, o_win, ng, nsl, o_mem, mg,
               ret_norm, w_ret_up, w_nsa_up, w_mem_up, w_out, norm_post)
    return y, (kvc, kvs, kv_all[:, L:], s_new.astype(ret_state.dtype))


def setup_inputs(seed: int = 0) -> dict:
    key = jax.random.key(seed)
    ks = jax.random.split(key, 24)

    def nrm(k, shape, scale):
        return jax.random.normal(k, shape, jnp.float32) * scale

    n_pages = PAST_LEN // PAGE_SIZE
    n_used = DEC_BATCH * n_pages
    n_pool = n_used + max(1, n_used // 4)
    page_table = jax.random.permutation(ks[0], n_pool)[:n_used].reshape(DEC_BATCH, n_pages).astype(jnp.int32)
    win_len = min(WINDOW, PAST_LEN)
    return {
        'x_prompt': nrm(ks[1], (BATCH, SEQ, D_MODEL), 1.0),
        'x_sample': nrm(ks[2], (DEC_BATCH, DEC_SEQ, D_MODEL), 1.0),
        'cache_cmp_kv': nrm(ks[3], (DEPTH, n_pool, PAGE_SIZE, 2, NSA_KV_HEADS, NSA_DK), 1.0),
        'cache_sel_kv': nrm(ks[4], (DEPTH, n_pool, PAGE_SIZE, 2, NSA_KV_HEADS, NSA_DK), 1.0),
        'cache_win_kv': nrm(ks[5], (DEPTH, DEC_BATCH, win_len, 2, NSA_KV_HEADS, NSA_DK), 1.0),
        'state_ret': nrm(ks[6], (DEPTH, DEC_BATCH, RET_HEADS, RET_DK, RET_DV), 0.1),
        'cache_mem_kv': nrm(ks[7], (DEPTH, DEC_BATCH, MEM_TOKENS, 2, MEM_HEADS, MEM_DH), 1.0),
        'page_table': page_table,
        'mem_prompt': nrm(ks[8], (BATCH, MEM_TOKENS, D_MODEL), 1.0),
        'rel_table': nrm(ks[9], (REL_BUCKETS, NSA_HEADS), 0.5),
        'norm_pre': 1.0 + nrm(ks[10], (DEPTH, D_MODEL), 0.02),
        'norm_post': 1.0 + nrm(ks[11], (DEPTH, D_MODEL), 0.02),
        'norm_mem': 1.0 + nrm(ks[12], (DEPTH, D_MODEL), 0.02),
        'w_in': nrm(ks[13], (DEPTH, D_MODEL, IN_WIDTH), D_MODEL ** -0.5),
        'ret_norm': 1.0 + nrm(ks[14], (DEPTH, RET_W), 0.02),
        'w_ret_up': nrm(ks[15], (DEPTH, RET_W, D_MODEL), RET_W ** -0.5),
        'cmp_pos': nrm(ks[16], (DEPTH, 2, CMP_BLOCK, NSA_DK), 0.1),
        'w_cmp1': nrm(ks[17], (DEPTH, 2, CMP_BLOCK, NSA_DK, NSA_DK), (CMP_BLOCK * NSA_DK) ** -0.5),
        'w_cmp2': nrm(ks[18], (DEPTH, 2, NSA_DK, NSA_DK), NSA_DK ** -0.5),
        'w_nsa_up': nrm(ks[19], (DEPTH, NSA_W, D_MODEL), NSA_W ** -0.5),
        'w_mem_kv': nrm(ks[20], (DEPTH, D_MODEL, 2 * MEM_W), D_MODEL ** -0.5),
        'w_mem_up': nrm(ks[21], (DEPTH, MEM_W, D_MODEL), MEM_W ** -0.5),
        'w_out': nrm(ks[22], (DEPTH, D_MODEL, D_MODEL), D_MODEL ** -0.5),
    }


def reference(x_prompt, x_sample, cache_cmp_kv, cache_sel_kv, cache_win_kv, state_ret, cache_mem_kv, page_table,
              mem_prompt, rel_table, norm_pre, norm_post, norm_mem, w_in, ret_norm, w_ret_up, cmp_pos, w_cmp1,
              w_cmp2, w_nsa_up, w_mem_kv, w_mem_up, w_out):
    xp, xs = x_prompt, x_sample
    cmp_p, sel_p, win_p, ret_p, mem_p = [], [], [], [], []
    cmp_s, sel_s, win_s, ret_s = [], [], [], []
    for layer in range(DEPTH):
        lw = (norm_pre[layer], norm_post[layer], norm_mem[layer], w_in[layer], ret_norm[layer], w_ret_up[layer],
              cmp_pos[layer], w_cmp1[layer], w_cmp2[layer], w_nsa_up[layer], w_mem_kv[layer], w_mem_up[layer],
              w_out[layer])
        xp, (kc, kse, kw, rs, mk) = prompt_layer(xp, mem_prompt, rel_table, lw)
        cmp_p.append(kc)
        sel_p.append(kse)
        win_p.append(kw)
        ret_p.append(rs)
        mem_p.append(mk)
        xs, (kc, kse, kw, rs) = sample_layer(xs, layer, cache_cmp_kv, cache_sel_kv, page_table, cache_win_kv[layer],
                                             state_ret[layer], cache_mem_kv[layer], rel_table, lw)
        cmp_s.append(kc)
        sel_s.append(kse)
        win_s.append(kw)
        ret_s.append(rs)
    return (xp, xs, jnp.stack(cmp_p), jnp.stack(sel_p), jnp.stack(win_p), jnp.stack(ret_p), jnp.stack(mem_p),
            jnp.stack(cmp_s), jnp.stack(sel_s), jnp.stack(win_s), jnp.stack(ret_s))
```

```python
import functools
import math

import jax
import jax.numpy as jnp
from jax import lax
from jax.experimental import pallas as pl
from jax.experimental.pallas import tpu as pltpu

F32 = jnp.float32
BF16 = jnp.bfloat16

D_MODEL = 2048
PAGE_SIZE = 128
RET_HEADS = 8
RET_DK = 256
RET_DV = 256
RET_CHUNK = 128
ROPE_BASE = 10000.0
NSA_HEADS = 16
NSA_KV_HEADS = 4
NSA_GROUP = NSA_HEADS // NSA_KV_HEADS
NSA_DK = 128
CMP_BLOCK = 32
CMP_STRIDE = 16
SEL_BLOCK = 64
SEL_TOPK = 16
WINDOW = 512
MEM_HEADS = 4
MEM_DH = 384
REL_BUCKETS = 32
REL_MAX_EXACT = 16
REL_MAX_DIST = 128
N_BRANCHES = 3
EPS = 1e-6
NEG_INF = -1e30
FORCE_SCORE = 1e4

RET_W = RET_HEADS * RET_DV
NSA_W = NSA_HEADS * NSA_DK
KV_W = NSA_KV_HEADS * NSA_DK
MEM_W = MEM_HEADS * MEM_DH

COL_RQ = 0
COL_RK = COL_RQ + RET_HEADS * RET_DK
COL_RV = COL_RK + RET_HEADS * RET_DK
COL_RG = COL_RV + RET_W
COL_NQ = COL_RG + RET_W
COL_KVC = COL_NQ + NSA_W
COL_KVS = COL_KVC + 2 * KV_W
COL_KVW = COL_KVS + 2 * KV_W
COL_NSL = COL_KVW + 2 * KV_W
COL_MQ = COL_NSL + NSA_W
COL_MG = COL_MQ + MEM_W
COL_NG = COL_MG + N_BRANCHES * D_MODEL
NG_SLOT = NSA_KV_HEADS * 128
PROJ_W = COL_NG + NG_SLOT

LANE = 128
TQ = 128
NSA_ROWS = NSA_GROUP * TQ
VMEM_LIMIT = 48 * 1024 * 1024


def _cparams(n_axes):
    return pltpu.CompilerParams(dimension_semantics=("arbitrary",) * n_axes, vmem_limit_bytes=VMEM_LIMIT)


def _nt(a, b):
    return lax.dot_general(a, b, (((1,), (1,)), ((), ())), preferred_element_type=F32)


def _dot(a, b):
    return jnp.dot(a, b, preferred_element_type=F32)


def _sigmoid(x):
    return 1.0 / (1.0 + jnp.exp(-x))


def _iota(shape, dim):
    return lax.broadcasted_iota(jnp.int32, shape, dim)


def _norm_matmul_kernel(x_ref, g_ref, w_ref, o_ref, h_ref):
    @pl.when(pl.program_id(1) == 0)
    def _():
        x = x_ref[...]
        ms = jnp.mean(x * x, axis=-1, keepdims=True)
        h_ref[...] = ((x * lax.rsqrt(ms + EPS)) * g_ref[...]).astype(BF16)

    o_ref[...] = _dot(h_ref[...], w_ref[...])


def _norm_matmul(x, g, w, tm, tn):
    m, k = x.shape
    n = w.shape[1]
    return pl.pallas_call(
        _norm_matmul_kernel,
        grid=(m // tm, n // tn),
        in_specs=[pl.BlockSpec((tm, k), lambda i, j: (i, 0)),
                  pl.BlockSpec((1, k), lambda i, j: (0, 0)),
                  pl.BlockSpec((k, tn), lambda i, j: (0, j))],
        out_specs=pl.BlockSpec((tm, tn), lambda i, j: (i, j)),
        out_shape=jax.ShapeDtypeStruct((m, n), F32),
        scratch_shapes=[pltpu.VMEM((tm, k), BF16)],
        compiler_params=_cparams(2),
        name="norm_matmul",
    )(x, g.reshape(1, k), w)


def _rope_rows(x, cos, sin):
    half = x.shape[-1] // 2
    x1, x2 = x[:, :half], x[:, half:]
    return jnp.concatenate([x1 * cos - x2 * sin, x1 * sin + x2 * cos], axis=-1)


def _head_norm_gate(o, gnorm, rg):
    oc = o - jnp.mean(o, axis=-1, keepdims=True)
    y = oc * lax.rsqrt(jnp.mean(oc * oc, axis=-1, keepdims=True) + EPS) * gnorm
    return y * (rg * _sigmoid(rg))


def _ret_prompt_kernel(q_ref, k_ref, v_ref, rg_ref, cos_ref, sin_ref, dmat_ref, xi_ref, zeta_ref, gc_ref,
                       gn_ref, a_ref, s_ref):
    @pl.when(pl.program_id(2) == 0)
    def _():
        s_ref[...] = jnp.zeros_like(s_ref)

    cos, sin = cos_ref[...], sin_ref[...]
    q = _rope_rows(q_ref[...], cos, sin)
    k = _rope_rows(k_ref[...], cos, sin) * (RET_DK ** -0.5)
    qb, vb = q.astype(BF16), v_ref[...].astype(BF16)
    state = s_ref[0, 0]
    inner = _nt(qb, k.astype(BF16)) * dmat_ref[0]
    o = _dot(inner.astype(BF16), vb) + _dot(qb, state.astype(BF16)) * xi_ref[0]
    kz_t = (k * zeta_ref[0]).T.astype(BF16)
    s_ref[0, 0] = state * gc_ref[0] + _dot(kz_t, vb)
    a_ref[...] = _head_norm_gate(o, gn_ref[...], rg_ref[...]).astype(BF16)


def _decay_tables(chunk):
    log_g = jnp.log1p(-jnp.exp2(-5.0 - jnp.arange(RET_HEADS, dtype=F32)))
    i = jnp.arange(chunk, dtype=F32)
    diff = i[:, None] - i[None, :]
    dmat = jnp.where(diff >= 0, jnp.exp(log_g[:, None, None] * jnp.maximum(diff, 0.0)), 0.0)
    xi = jnp.exp(log_g[:, None] * (i[None, :] + 1.0))[:, :, None]
    zeta = jnp.exp(log_g[:, None] * (chunk - 1.0 - i[None, :]))[:, :, None]
    g_chunk = jnp.exp(log_g * chunk)[:, None, None]
    return dmat, xi, zeta, g_chunk


def _rope_tables(pos):
    half = RET_DK // 2
    freq = jnp.power(ROPE_BASE, -jnp.arange(half, dtype=F32) / half)
    ang = pos.astype(F32)[:, None] * freq[None, :]
    return jnp.cos(ang), jnp.sin(ang)


def _retention_prompt(z, ret_norm, batch, seq):
    c = RET_CHUNK
    nc = seq // c
    dmat, xi, zeta, g_chunk = _decay_tables(c)
    cos, sin = _rope_tables(jnp.arange(seq))
    hb = RET_DK

    def zspec(col0):
        return pl.BlockSpec((c, hb), lambda b, h, t, col0=col0: (b * nc + t, col0 // hb + h))

    per_head = lambda shape: pl.BlockSpec((1,) + shape, lambda b, h, t: (h, 0, 0))
    return pl.pallas_call(
        _ret_prompt_kernel,
        grid=(batch, RET_HEADS, nc),
        in_specs=[zspec(COL_RQ), zspec(COL_RK), zspec(COL_RV), zspec(COL_RG),
                  pl.BlockSpec((c, hb // 2), lambda b, h, t: (t, 0)),
                  pl.BlockSpec((c, hb // 2), lambda b, h, t: (t, 0)),
                  per_head((c, c)), per_head((c, 1)), per_head((c, 1)), per_head((1, 1)),
                  pl.BlockSpec((1, hb), lambda b, h, t: (0, h))],
        out_specs=[pl.BlockSpec((c, hb), lambda b, h, t: (b * nc + t, h)),
                   pl.BlockSpec((1, 1, RET_DK, RET_DV), lambda b, h, t: (b, h, 0, 0))],
        out_shape=[jax.ShapeDtypeStruct((batch * seq, RET_W), BF16),
                   jax.ShapeDtypeStruct((batch, RET_HEADS, RET_DK, RET_DV), F32)],
        compiler_params=_cparams(3),
        name="retention_prompt",
    )(z, z, z, z, cos, sin, dmat, xi, zeta, g_chunk, ret_norm.reshape(1, RET_W))


def _column_of(row):
    n = row.shape[1]
    eye = _iota((n, n), 0) == _iota((n, n), 1)
    return jnp.sum(jnp.where(eye, jnp.broadcast_to(row, (n, n)), 0.0), axis=1, keepdims=True)


def _ret_sample_kernel(q_ref, k_ref, v_ref, rg_ref, cos_ref, sin_ref, gam_ref, gn_ref, s_ref, a_ref, so_ref):
    cos, sin = cos_ref[...], sin_ref[...]
    q = _rope_rows(q_ref[0], cos, sin)
    k = _rope_rows(k_ref[0], cos, sin) * (RET_DK ** -0.5)
    v = v_ref[0]
    state = s_ref[0, 0]
    gamma = gam_ref[0]
    qk = jnp.sum(q * k, axis=-1, keepdims=True)
    o = qk * v + jnp.sum(_column_of(q) * state, axis=0, keepdims=True) * gamma
    so_ref[0, 0] = state * gamma + _column_of(k) * v
    a_ref[0] = _head_norm_gate(o, gn_ref[...], rg_ref[0]).astype(BF16)


def _retention_sample(z3, state, ret_norm, pos):
    nb = z3.shape[0]
    cos, sin = _rope_tables(jnp.full((1,), pos))
    gamma = jnp.exp(jnp.log1p(-jnp.exp2(-5.0 - jnp.arange(RET_HEADS, dtype=F32))))[:, None, None]
    hb = RET_DK

    def zspec(col0):
        return pl.BlockSpec((1, 1, hb), lambda b, h, col0=col0: (b, 0, col0 // hb + h))

    st_spec = pl.BlockSpec((1, 1, RET_DK, RET_DV), lambda b, h: (b, h, 0, 0))
    return pl.pallas_call(
        _ret_sample_kernel,
        grid=(nb, RET_HEADS),
        in_specs=[zspec(COL_RQ), zspec(COL_RK), zspec(COL_RV), zspec(COL_RG),
                  pl.BlockSpec((1, hb // 2), lambda b, h: (0, 0)),
                  pl.BlockSpec((1, hb // 2), lambda b, h: (0, 0)),
                  pl.BlockSpec((1, 1, 1), lambda b, h: (h, 0, 0)),
                  pl.BlockSpec((1, hb), lambda b, h: (0, h)),
                  st_spec],
        out_specs=[pl.BlockSpec((1, 1, hb), lambda b, h: (b, 0, h)), st_spec],
        out_shape=[jax.ShapeDtypeStruct((nb, 1, RET_W), BF16),
                   jax.ShapeDtypeStruct(state.shape, F32)],
        compiler_params=_cparams(2),
        name="retention_sample",
    )(z3, z3, z3, z3, cos, sin, gamma, ret_norm.reshape(1, RET_W), state)


def _half_rows(ref_slice_fn, n_half):
    return jnp.concatenate([ref_slice_fn(p) for p in range(CMP_STRIDE)], axis=1)


def _cmp_stage1_dense_kernel(x_ref, w1a_ref, w1b_ref, o_ref):
    nh = o_ref.shape[3]
    x = _half_rows(lambda p: x_ref[pl.ds(p, nh, stride=CMP_STRIDE), :], nh).astype(BF16)
    o_ref[0, 0, 0] = jnp.concatenate([_dot(x, w1a_ref[0]), _dot(x, w1b_ref[0])], axis=1)


def _cmp_stage1_dense(z, w1a, w1b, batch, seq):
    nh = seq // CMP_STRIDE
    return pl.pallas_call(
        _cmp_stage1_dense_kernel,
        grid=(batch, 2, NSA_KV_HEADS),
        in_specs=[pl.BlockSpec((seq, NSA_DK), lambda b, kv, h: (b, COL_KVC // NSA_DK + kv * NSA_KV_HEADS + h)),
                  pl.BlockSpec((1, CMP_STRIDE * NSA_DK, NSA_DK), lambda b, kv, h: (kv, 0, 0)),
                  pl.BlockSpec((1, CMP_STRIDE * NSA_DK, NSA_DK), lambda b, kv, h: (kv, 0, 0))],
        out_specs=pl.BlockSpec((1, 1, 1, nh, 2 * NSA_DK), lambda b, kv, h: (b, kv, h, 0, 0)),
        out_shape=jax.ShapeDtypeStruct((batch, 2, NSA_KV_HEADS, nh, 2 * NSA_DK), F32),
        compiler_params=_cparams(3),
        name="cmp_stage1_dense",
    )(z, w1a, w1b)


PAGES_PER_STEP = 8


def _cmp_stage1_paged_kernel(pt_ref, *refs):
    pages, (w1a_ref, w1b_ref, o_ref) = refs[:PAGES_PER_STEP], refs[PAGES_PER_STEP:]
    hp = PAGE_SIZE // CMP_STRIDE
    top = _iota((2 * NSA_KV_HEADS, NSA_DK), 0) < NSA_KV_HEADS
    cols = [[], []]
    for p in range(CMP_STRIDE):
        tiles = [[], []]
        for pg in pages:
            xp = pg[0, :, p]
            for n in range(0, hp, 2):
                a, b = xp[n], xp[n + 1]
                tiles[0].append(jnp.where(top, a, pltpu.roll(b, NSA_KV_HEADS, 0)))
                tiles[1].append(jnp.where(top, pltpu.roll(a, NSA_KV_HEADS, 0), b))
        for kv in range(2):
            cols[kv].append(jnp.concatenate(tiles[kv], axis=0))
    for kv in range(2):
        x = jnp.concatenate(cols[kv], axis=1).astype(BF16)
        o_ref[0, kv, 0] = jnp.concatenate([_dot(x, w1a_ref[kv]), _dot(x, w1b_ref[kv])], axis=1)


def _cmp_stage1_paged(cache, page_table, w1a, w1b):
    nb, n_pages = page_table.shape
    hp = PAGE_SIZE // CMP_STRIDE
    steps = n_pages // PAGES_PER_STEP
    rows = PAGES_PER_STEP * hp * NSA_KV_HEADS

    def page_spec(j):
        return pl.BlockSpec((1, hp, CMP_STRIDE, 2 * NSA_KV_HEADS, NSA_DK),
                            lambda b, s, pt, j=j: (pt[b, s * PAGES_PER_STEP + j], 0, 0, 0, 0))

    wspec = pl.BlockSpec((2, CMP_STRIDE * NSA_DK, NSA_DK), lambda b, s, pt: (0, 0, 0))
    return pl.pallas_call(
        _cmp_stage1_paged_kernel,
        grid_spec=pltpu.PrefetchScalarGridSpec(
            num_scalar_prefetch=1,
            grid=(nb, steps),
            in_specs=[page_spec(j) for j in range(PAGES_PER_STEP)] + [wspec, wspec],
            out_specs=pl.BlockSpec((1, 2, 1, rows, 2 * NSA_DK), lambda b, s, pt: (b, 0, 0, s, 0))),
        out_shape=jax.ShapeDtypeStruct((nb, 2, 1, steps * rows, 2 * NSA_DK), F32),
        compiler_params=_cparams(2),
        name="cmp_stage1_paged",
    )(page_table, *([cache] * PAGES_PER_STEP), w1a, w1b)


def _cmp_stage2_kernel(ac_ref, pos_ref, w1a_ref, w1b_ref, w2_ref, o_ref, *, shift):
    ac = ac_ref[0, 0, 0]
    nh = ac.shape[0]
    pos = pos_ref[0].astype(BF16)
    kw = CMP_STRIDE * NSA_DK
    pe = _dot(pos[:, :kw], w1a_ref[0]) + _dot(pos[:, kw:], w1b_ref[0])
    pre = ac[:, :NSA_DK] + pltpu.roll(ac[:, NSA_DK:], nh - shift, 0) + pe[0:1]
    gelu = 0.5 * pre * (1.0 + jnp.tanh(math.sqrt(2.0 / math.pi) * (pre + 0.044715 * (pre * pre * pre))))
    o_ref[0, 0, 0] = _dot(gelu.astype(BF16), w2_ref[0]).astype(BF16)


def _cmp_stage2(ac, pos8, w1a, w1b, w2, shift):
    nb, _, groups, nh, _ = ac.shape
    kw = CMP_STRIDE * NSA_DK
    return pl.pallas_call(
        functools.partial(_cmp_stage2_kernel, shift=shift),
        grid=(nb, 2, groups),
        in_specs=[pl.BlockSpec((1, 1, 1, nh, 2 * NSA_DK), lambda b, kv, h: (b, kv, h, 0, 0)),
                  pl.BlockSpec((1, 8, 2 * kw), lambda b, kv, h: (kv, 0, 0)),
                  pl.BlockSpec((1, kw, NSA_DK), lambda b, kv, h: (kv, 0, 0)),
                  pl.BlockSpec((1, kw, NSA_DK), lambda b, kv, h: (kv, 0, 0)),
                  pl.BlockSpec((1, NSA_DK, NSA_DK), lambda b, kv, h: (kv, 0, 0))],
        out_specs=pl.BlockSpec((1, 1, 1, nh, NSA_DK), lambda b, kv, h: (b, kv, h, 0, 0)),
        out_shape=jax.ShapeDtypeStruct((nb, 2, groups, nh, NSA_DK), BF16),
        compiler_params=_cparams(3),
        name="cmp_stage2",
    )(ac, pos8, w1a, w1b, w2)


def _rel_bucket(dist):
    n = jnp.maximum(dist, 0)
    nf = jnp.maximum(n, 1).astype(F32)
    scale = (REL_BUCKETS - REL_MAX_EXACT) / math.log(REL_MAX_DIST / REL_MAX_EXACT)
    large = REL_MAX_EXACT + (jnp.log(nf / REL_MAX_EXACT) * scale).astype(jnp.int32)
    large = jnp.minimum(large, REL_BUCKETS - 1)
    return jnp.where(n < REL_MAX_EXACT, n, large)


def _bias_of_dist(rel_table, dist, masked):
    tab = rel_table[_rel_bucket(dist)]
    tab = jnp.moveaxis(tab, -1, 0).reshape((NSA_KV_HEADS, NSA_GROUP) + dist.shape)
    return jnp.where(masked, NEG_INF, tab)


def _flash_init(m_ref, l_ref, acc_ref):
    m_ref[...] = jnp.full(m_ref.shape, NEG_INF, F32)
    l_ref[...] = jnp.zeros(l_ref.shape, F32)
    acc_ref[...] = jnp.zeros(acc_ref.shape, F32)


def _flash_step(s, v, m_ref, l_ref, acc_ref):
    m_old = m_ref[...]
    m_new = jnp.maximum(m_old, jnp.max(s, axis=1, keepdims=True))
    alpha = jnp.exp(m_old - m_new)
    p = jnp.exp(s - m_new)
    l_ref[...] = alpha * l_ref[...] + jnp.sum(p, axis=1, keepdims=True)
    acc_ref[...] = alpha * acc_ref[...] + _dot(p.astype(BF16), v)
    m_ref[...] = m_new


def _flash_result(l_ref, acc_ref):
    return acc_ref[...] / jnp.maximum(l_ref[...], 1e-30)


def _select_blocks(imp_t, q0, ns):
    shape = imp_t.shape
    blk = _iota(shape, 0)
    qpos = q0 + _iota(shape, 1)
    cur = qpos >> 6
    valid = blk * SEL_BLOCK <= qpos
    forced = (blk == 0) | (blk == cur) | (blk == cur - 1)
    imp_t = jnp.where(valid, jnp.where(forced, FORCE_SCORE, imp_t), NEG_INF)
    rank = jnp.zeros(shape, F32)
    for other in range(ns):
        row = imp_t[other:other + 1, :]
        ahead = (row > imp_t) | ((row == imp_t) & (blk > other))
        rank = rank + jnp.where(ahead, 1.0, 0.0)
    return jnp.where((rank < SEL_TOPK) & valid, 1.0, 0.0)


def _nsa_prompt_kernel(q_ref, ks_ref, vs_ref, kw_ref, vw_ref, kc_ref, vc_ref, tz_ref, basec_ref, ovt_ref,
                       ng_ref, nsl_ref, o_ref, m_ref, l_ref, acc_ref, *, ns):
    i = pl.program_id(2)
    q0 = i * TQ
    scale = NSA_DK ** -0.5
    qall = q_ref[...]
    qs = jnp.concatenate([qall[:, g * NSA_DK:(g + 1) * NSA_DK] for g in range(NSA_GROUP)], axis=0).astype(BF16)
    r_loc = _iota((NSA_ROWS, TQ), 0) & (TQ - 1)
    c_loc = _iota((NSA_ROWS, TQ), 1)

    ncp = kc_ref.shape[3]
    bias_c = pltpu.roll(basec_ref[0], (TQ // CMP_STRIDE) * i, 1)[:, ncp:]
    s = _nt(qs, kc_ref[0, 0, 0]) * scale + bias_c
    m = jnp.max(s, axis=1, keepdims=True)
    e = jnp.exp(s - m)
    inv = jnp.where(m > 0.5 * NEG_INF, 1.0 / jnp.maximum(jnp.sum(e, axis=1, keepdims=True), 1e-30), 0.0)
    p = e * inv
    o_cmp = _dot(p.astype(BF16), vc_ref[0, 0, 0])

    psum = p[0:TQ] + p[TQ:2 * TQ] + p[2 * TQ:3 * TQ] + p[3 * TQ:4 * TQ]
    hi = psum.astype(BF16)
    lo = (psum - hi.astype(F32)).astype(BF16)
    ovt = ovt_ref[...]
    imp_t = _nt(ovt, hi) + _nt(ovt, lo)
    ns8 = -(-ns // 8) * 8
    sel_t = _select_blocks(imp_t[:ns8], q0, ns)
    nsp = ovt.shape[0]
    if ns8 < nsp:
        sel_t = jnp.concatenate([sel_t, jnp.zeros((nsp - ns8, TQ), F32)], axis=0)
    sel = sel_t.T.astype(BF16)

    def sel_tile(kt, bias, causal):
        rows = pl.ds(pl.multiple_of(kt * TQ, TQ), TQ)
        k = ks_ref[rows, :].astype(BF16)
        v = vs_ref[rows, :].astype(BF16)
        expand = jnp.where(_iota((nsp, TQ), 0) == (TQ // SEL_BLOCK) * kt + (_iota((nsp, TQ), 1) >> 6), 1.0, 0.0)
        mk = _dot(sel, expand.astype(BF16))
        mk4 = jnp.concatenate([mk] * NSA_GROUP, axis=0) > 0.5
        if causal:
            mk4 = mk4 & (c_loc <= r_loc)
        sc = jnp.where(mk4, _nt(qs, k) * scale + bias, NEG_INF)
        _flash_step(sc, v, m_ref, l_ref, acc_ref)

    _flash_init(m_ref, l_ref, acc_ref)
    sel_tile(i, tz_ref[0, 0], True)

    @pl.when(i >= 1)
    def _():
        sel_tile(i - 1, tz_ref[0, 1], False)

    def far_tile(t, carry):
        sel_tile(i - t, tz_ref[0, 2], False)
        return carry

    lax.fori_loop(2, i + 1, far_tile, 0)
    o_sel = _flash_result(l_ref, acc_ref)

    def win_tile(off):
        rows = pl.ds(pl.multiple_of((i - off) * TQ, TQ), TQ)
        k = kw_ref[rows, :].astype(BF16)
        v = vw_ref[rows, :].astype(BF16)
        sc = _nt(qs, k) * scale + tz_ref[0, min(off, 2)]
        if off == 0:
            sc = jnp.where(c_loc <= r_loc, sc, NEG_INF)
        if off * TQ == WINDOW:
            sc = jnp.where(c_loc > r_loc, sc, NEG_INF)
        _flash_step(sc, v, m_ref, l_ref, acc_ref)

    _flash_init(m_ref, l_ref, acc_ref)
    win_tile(0)
    for off in range(1, WINDOW // TQ + 1):
        pl.when(i >= off)(functools.partial(win_tile, off))
    o_win = _flash_result(l_ref, acc_ref)

    gates = _sigmoid(ng_ref[...])
    nsl = nsl_ref[...]
    outs = []
    for g in range(NSA_GROUP):
        rows = slice(g * TQ, (g + 1) * TQ)
        o = (gates[:, 3 * g:3 * g + 1] * o_cmp[rows] + gates[:, 3 * g + 1:3 * g + 2] * o_sel[rows]
             + gates[:, 3 * g + 2:3 * g + 3] * o_win[rows])
        x = nsl[:, g * NSA_DK:(g + 1) * NSA_DK]
        outs.append(o * (x * _sigmoid(x)))
    o_ref[...] = jnp.concatenate(outs, axis=1).astype(BF16)


def _nsa_prompt(z, kcvc, rel_table, batch, seq):
    nq = seq // TQ
    ns = seq // SEL_BLOCK
    ncp = kcvc.shape[3]
    nsp = LANE
    assert ns <= nsp and ncp >= seq // CMP_STRIDE and ncp % LANE == 0
    gw = NSA_GROUP * NSA_DK

    r = jnp.arange(TQ)[:, None]
    c = jnp.arange(TQ)[None, :]
    no_mask = jnp.zeros((TQ, TQ), bool)
    tz = jnp.stack([_bias_of_dist(rel_table, r - c, no_mask),
                    _bias_of_dist(rel_table, TQ + r - c, no_mask),
                    _bias_of_dist(rel_table, jnp.full((TQ, TQ), REL_MAX_DIST), no_mask)], axis=1)
    tz = tz.reshape(NSA_KV_HEADS, 3, NSA_ROWS, TQ)
    dist_c = r - CMP_STRIDE * (jnp.arange(2 * ncp)[None, :] - ncp) - (CMP_BLOCK - 1)
    basec = _bias_of_dist(rel_table, dist_c, dist_c < 0).reshape(NSA_KV_HEADS, NSA_ROWS, 2 * ncp)
    sblk = jnp.arange(nsp)[:, None]
    nblk = jnp.arange(ncp)[None, :]
    ovt = ((nblk >= 4 * sblk - 1) & (nblk <= 4 * sblk + 3)).astype(BF16)

    def kvspec(col0, which):
        return pl.BlockSpec((seq, NSA_DK),
                            lambda b, h, i: (b, col0 // NSA_DK + which * NSA_KV_HEADS + h))

    def cspec(which):
        return pl.BlockSpec((1, 1, 1, ncp, NSA_DK), lambda b, h, i: (b, which, h, 0, 0))

    return pl.pallas_call(
        functools.partial(_nsa_prompt_kernel, ns=ns),
        grid=(batch, NSA_KV_HEADS, nq),
        in_specs=[pl.BlockSpec((TQ, gw), lambda b, h, i: (b * nq + i, COL_NQ // gw + h)),
                  kvspec(COL_KVS, 0), kvspec(COL_KVS, 1), kvspec(COL_KVW, 0), kvspec(COL_KVW, 1),
                  cspec(0), cspec(1),
                  pl.BlockSpec((1, 3, NSA_ROWS, TQ), lambda b, h, i: (h, 0, 0, 0)),
                  pl.BlockSpec((1, NSA_ROWS, 2 * ncp), lambda b, h, i: (h, 0, 0)),
                  pl.BlockSpec((nsp, ncp), lambda b, h, i: (0, 0)),
                  pl.BlockSpec((TQ, LANE), lambda b, h, i: (b * nq + i, COL_NG // LANE + h)),
                  pl.BlockSpec((TQ, gw), lambda b, h, i: (b * nq + i, COL_NSL // gw + h))],
        out_specs=pl.BlockSpec((TQ, gw), lambda b, h, i: (b * nq + i, h)),
        out_shape=jax.ShapeDtypeStruct((batch * seq, NSA_W), BF16),
        scratch_shapes=[pltpu.VMEM((NSA_ROWS, 1), F32), pltpu.VMEM((NSA_ROWS, 1), F32),
                        pltpu.VMEM((NSA_ROWS, NSA_DK), F32)],
        compiler_params=_cparams(3),
        name="nsa_prompt",
    )(z, z, z, z, z, kcvc, kcvc, tz, basec, ovt, z, z)


SROWS = 8


def _stack_group_q(q_row):
    heads = [q_row[:, g * NSA_DK:(g + 1) * NSA_DK] for g in range(NSA_GROUP)]
    return jnp.concatenate(heads + [heads[0]] * (SROWS - NSA_GROUP), axis=0)


def _nsa_sample_cmp_kernel(q_ref, kc_ref, vc_ref, bias_ref, ov_ref, o_ref, idx_ref, *, ns):
    scale = NSA_DK ** -0.5
    qs = _stack_group_q(q_ref[0]).astype(BF16)
    s = _nt(qs, kc_ref[0, 0]) * scale + bias_ref[0]
    m = jnp.max(s, axis=1, keepdims=True)
    e = jnp.exp(s - m)
    inv = jnp.where(m > 0.5 * NEG_INF, 1.0 / jnp.maximum(jnp.sum(e, axis=1, keepdims=True), 1e-30), 0.0)
    p = e * inv
    o_ref[0, 0] = _dot(p.astype(BF16), vc_ref[0, 0])
    psum = jnp.broadcast_to(jnp.sum(p[0:NSA_GROUP], axis=0, keepdims=True), p.shape)
    hi = psum.astype(BF16)
    lo = (psum - hi.astype(F32)).astype(BF16)
    imp = (_dot(hi, ov_ref[...]) + _dot(lo, ov_ref[...]))[0:1]
    nsp = imp.shape[1]
    blk_r = _iota((1, nsp), 1)
    cur = ns - 1
    forced = (blk_r == 0) | (blk_r == cur) | (blk_r == cur - 1)
    imp = jnp.where(blk_r < ns, jnp.where(forced, FORCE_SCORE, imp), 2.0 * NEG_INF)
    imp_c = _column_of(imp)
    i_r = _iota((nsp, nsp), 1)
    j_c = _iota((nsp, nsp), 0)
    ahead = (imp > imp_c) | ((imp == imp_c) & (i_r < j_c))
    rank_c = jnp.sum(jnp.where(ahead, 1.0, 0.0), axis=1, keepdims=True)
    slot = _iota((nsp, LANE), 1).astype(F32)
    picks = jnp.where(rank_c == slot, _iota((nsp, LANE), 0).astype(F32), 0.0)
    idx_ref[0, 0] = jnp.broadcast_to(jnp.sum(picks, axis=0, keepdims=True), (SROWS, LANE)).astype(jnp.int32)


def _nsa_sample_cmp(z3, kcvc, bias_c, ov, ns):
    nb = z3.shape[0]
    ncp = kcvc.shape[2]
    nsp = ov.shape[1]
    gw = NSA_GROUP * NSA_DK

    def cspec(which):
        return pl.BlockSpec((1, 1, ncp, NSA_DK), lambda b, h: (b, which, 0, h))

    return pl.pallas_call(
        functools.partial(_nsa_sample_cmp_kernel, ns=ns),
        grid=(nb, NSA_KV_HEADS),
        in_specs=[pl.BlockSpec((1, 1, gw), lambda b, h: (b, 0, COL_NQ // gw + h)),
                  cspec(0), cspec(1),
                  pl.BlockSpec((1, SROWS, ncp), lambda b, h: (h, 0, 0)),
                  pl.BlockSpec((ncp, nsp), lambda b, h: (0, 0))],
        out_specs=[pl.BlockSpec((1, 1, SROWS, NSA_DK), lambda b, h: (b, h, 0, 0)),
                   pl.BlockSpec((1, 1, SROWS, LANE), lambda b, h: (b, h, 0, 0))],
        out_shape=[jax.ShapeDtypeStruct((nb, NSA_KV_HEADS, SROWS, NSA_DK), F32),
                   jax.ShapeDtypeStruct((nb, NSA_KV_HEADS, SROWS, LANE), jnp.int32)],
        compiler_params=_cparams(2),
        name="nsa_sample_cmp",
    )(z3, kcvc, kcvc, bias_c, ov)


def _nsa_sample_sel_kernel(pt_ref, idx_ref, q_ref, k_ref, v_ref, kn_ref, vn_ref, bias_ref, o_ref,
                           m_ref, l_ref, acc_ref, *, ns):
    b, h, t = pl.program_id(0), pl.program_id(1), pl.program_id(2)

    @pl.when(t == 0)
    def _():
        _flash_init(m_ref, l_ref, acc_ref)

    is_new = idx_ref[b, h, t] == ns - 1
    k = jnp.where(is_new, jnp.broadcast_to(kn_ref[0], (SEL_BLOCK, NSA_DK)), k_ref[0])
    v = jnp.where(is_new, jnp.broadcast_to(vn_ref[0], (SEL_BLOCK, NSA_DK)), v_ref[0])
    qs = _stack_group_q(q_ref[0]).astype(BF16)
    sc = _nt(qs, k.astype(BF16)) * (NSA_DK ** -0.5) + bias_ref[0, 0]
    _flash_step(sc, v.astype(BF16), m_ref, l_ref, acc_ref)

    @pl.when(t == pl.num_programs(2) - 1)
    def _():
        o_ref[0, 0] = _flash_result(l_ref, acc_ref)


def _nsa_sample_sel(z3, cache, page_table, idx, bias_sel, ns):
    nb, n_pages = page_table.shape
    n_sel = idx.shape[2]
    gw = NSA_GROUP * NSA_DK
    halves = PAGE_SIZE // SEL_BLOCK

    def kvspec(which):
        def index(b, h, t, pt, ix):
            blk = ix[b, h, t]
            return (pt[b, jnp.minimum(blk // halves, n_pages - 1)], blk % halves, which * NSA_KV_HEADS + h)
        return pl.BlockSpec((1, SEL_BLOCK, NSA_DK), index)

    def newspec(which):
        return pl.BlockSpec((1, 1, NSA_DK),
                            lambda b, h, t, pt, ix: (b, 0, COL_KVS // NSA_DK + which * NSA_KV_HEADS + h))

    return pl.pallas_call(
        functools.partial(_nsa_sample_sel_kernel, ns=ns),
        grid_spec=pltpu.PrefetchScalarGridSpec(
            num_scalar_prefetch=2,
            grid=(nb, NSA_KV_HEADS, n_sel),
            in_specs=[pl.BlockSpec((1, 1, gw), lambda b, h, t, pt, ix: (b, 0, COL_NQ // gw + h)),
                      kvspec(0), kvspec(1), newspec(0), newspec(1),
                      pl.BlockSpec((1, 1, SROWS, SEL_BLOCK), lambda b, h, t, pt, ix: (h, ix[b, h, t], 0, 0))],
            out_specs=pl.BlockSpec((1, 1, SROWS, NSA_DK), lambda b, h, t, pt, ix: (b, h, 0, 0)),
            scratch_shapes=[pltpu.VMEM((SROWS, 1), F32), pltpu.VMEM((SROWS, 1), F32),
                            pltpu.VMEM((SROWS, NSA_DK), F32)]),
        out_shape=jax.ShapeDtypeStruct((nb, NSA_KV_HEADS, SROWS, NSA_DK), F32),
        compiler_params=_cparams(3),
        name="nsa_sample_sel",
    )(page_table, idx, z3, cache, cache, z3, z3, bias_sel)


def _nsa_sample_win_kernel(q_ref, k_ref, v_ref, kn_ref, vn_ref, bias_ref, bnew_ref, o_ref):
    scale = NSA_DK ** -0.5
    q = _stack_group_q(q_ref[0])
    s_buf = _nt(q.astype(BF16), k_ref[0].astype(BF16)) * scale + bias_ref[0]
    s_new = jnp.sum(q * kn_ref[0], axis=1, keepdims=True) * scale + bnew_ref[0]
    m = jnp.maximum(jnp.max(s_buf, axis=1, keepdims=True), s_new)
    p_buf = jnp.exp(s_buf - m)
    p_new = jnp.exp(s_new - m)
    l = jnp.sum(p_buf, axis=1, keepdims=True) + p_new
    acc = _dot(p_buf.astype(BF16), v_ref[0].astype(BF16)) + p_new * vn_ref[0]
    o_ref[0, 0] = acc / jnp.maximum(l, 1e-30)


def _nsa_sample_win(z3, win_buf, bias_win, bias_new):
    nb, nbuf, _ = win_buf.shape
    gw = NSA_GROUP * NSA_DK

    def bufspec(which):
        return pl.BlockSpec((1, nbuf, NSA_DK), lambda b, h: (b, 0, which * NSA_KV_HEADS + h))

    def newspec(which):
        return pl.BlockSpec((1, 1, NSA_DK), lambda b, h: (b, 0, COL_KVW // NSA_DK + which * NSA_KV_HEADS + h))

    return pl.pallas_call(
        _nsa_sample_win_kernel,
        grid=(nb, NSA_KV_HEADS),
        in_specs=[pl.BlockSpec((1, 1, gw), lambda b, h: (b, 0, COL_NQ // gw + h)),
                  bufspec(0), bufspec(1), newspec(0), newspec(1),
                  pl.BlockSpec((1, SROWS, nbuf), lambda b, h: (h, 0, 0)),
                  pl.BlockSpec((1, SROWS, 1), lambda b, h: (h, 0, 0))],
        out_specs=pl.BlockSpec((1, 1, SROWS, NSA_DK), lambda b, h: (b, h, 0, 0)),
        out_shape=jax.ShapeDtypeStruct((nb, NSA_KV_HEADS, SROWS, NSA_DK), F32),
        compiler_params=_cparams(2),
        name="nsa_sample_win",
    )(z3, win_buf, win_buf, z3, z3, bias_win, bias_new)


def _nsa_sample_gate_kernel(oc_ref, os_ref, ow_ref, ng_ref, nsl_ref, o_ref):
    gates = _sigmoid(ng_ref[0])
    nsl = nsl_ref[0]
    outs = []
    for g in range(NSA_GROUP):
        o = (gates[:, 3 * g:3 * g + 1] * oc_ref[0, 0, g:g + 1] + gates[:, 3 * g + 1:3 * g + 2] * os_ref[0, 0, g:g + 1]
             + gates[:, 3 * g + 2:3 * g + 3] * ow_ref[0, 0, g:g + 1])
        x = nsl[:, g * NSA_DK:(g + 1) * NSA_DK]
        outs.append(o * (x * _sigmoid(x)))
    o_ref[0] = jnp.concatenate(outs, axis=1).astype(BF16)


def _nsa_sample_gate(o_cmp, o_sel, o_win, z3):
    nb = z3.shape[0]
    gw = NSA_GROUP * NSA_DK
    ospec = pl.BlockSpec((1, 1, SROWS, NSA_DK), lambda b, h: (b, h, 0, 0))
    return pl.pallas_call(
        _nsa_sample_gate_kernel,
        grid=(nb, NSA_KV_HEADS),
        in_specs=[ospec, ospec, ospec,
                  pl.BlockSpec((1, 1, LANE), lambda b, h: (b, 0, COL_NG // LANE + h)),
                  pl.BlockSpec((1, 1, gw), lambda b, h: (b, 0, COL_NSL // gw + h))],
        out_specs=pl.BlockSpec((1, 1, gw), lambda b, h: (b, 0, h)),
        out_shape=jax.ShapeDtypeStruct((nb, 1, NSA_W), BF16),
        compiler_params=_cparams(2),
        name="nsa_sample_gate",
    )(o_cmp, o_sel, o_win, z3, z3)


def _mem_heads(q, kv):
    outs = []
    for h in range(MEM_HEADS):
        k = kv[:, h * MEM_DH:(h + 1) * MEM_DH].astype(BF16)
        v = kv[:, MEM_W + h * MEM_DH:MEM_W + (h + 1) * MEM_DH].astype(BF16)
        s = _nt(q[:, h * MEM_DH:(h + 1) * MEM_DH].astype(BF16), k) * (MEM_DH ** -0.5)
        e = jnp.exp(s - jnp.max(s, axis=1, keepdims=True))
        p = e / jnp.sum(e, axis=1, keepdims=True)
        outs.append(_dot(p.astype(BF16), v))
    return jnp.concatenate(outs, axis=1)


def _mem_prompt_kernel(q_ref, kv_ref, o_ref):
    o_ref[...] = _mem_heads(q_ref[...], kv_ref[...]).astype(BF16)


def _mem_prompt(z, mem_kv, batch, seq, tq):
    nq = seq // tq
    n_mem = mem_kv.shape[0] // batch
    return pl.pallas_call(
        _mem_prompt_kernel,
        grid=(batch, nq),
        in_specs=[pl.BlockSpec((tq, MEM_W), lambda b, i: (b * nq + i, COL_MQ // MEM_W)),
                  pl.BlockSpec((n_mem, 2 * MEM_W), lambda b, i: (b, 0))],
        out_specs=pl.BlockSpec((tq, MEM_W), lambda b, i: (b * nq + i, 0)),
        out_shape=jax.ShapeDtypeStruct((batch * seq, MEM_W), BF16),
        compiler_params=_cparams(2),
        name="mem_prompt",
    )(z, mem_kv)


def _mem_sample_kernel(q_ref, kv_ref, o_ref):
    q = jnp.broadcast_to(q_ref[0], (SROWS, MEM_W))
    o_ref[0] = _mem_heads(q, kv_ref[0])[0:1].astype(BF16)


def _mem_sample(z3, mem_kv):
    nb, n_mem, _ = mem_kv.shape
    return pl.pallas_call(
        _mem_sample_kernel,
        grid=(nb,),
        in_specs=[pl.BlockSpec((1, 1, MEM_W), lambda b: (b, 0, COL_MQ // MEM_W)),
                  pl.BlockSpec((1, n_mem, 2 * MEM_W), lambda b: (b, 0, 0))],
        out_specs=pl.BlockSpec((1, 1, MEM_W), lambda b: (b, 0, 0)),
        out_shape=jax.ShapeDtypeStruct((nb, 1, MEM_W), BF16),
        compiler_params=_cparams(1),
        name="mem_sample",
    )(z3, mem_kv)


def _merge_kernel(ar_ref, an_ref, am_ref, wr_ref, wn_ref, wm_ref, g0_ref, g1_ref, g2_ref, o_ref):
    merged = (_sigmoid(g0_ref[...]) * _dot(ar_ref[...], wr_ref[...])
              + _sigmoid(g1_ref[...]) * _dot(an_ref[...], wn_ref[...])
              + _sigmoid(g2_ref[...]) * _dot(am_ref[...], wm_ref[...]))
    o_ref[...] = merged.astype(BF16)


def _merge(a_ret, a_nsa, a_mem, w_ret, w_nsa, w_mem, z, tm, tn):
    m = a_ret.shape[0]
    nt = D_MODEL // tn

    def aspec(width):
        return pl.BlockSpec((tm, width), lambda i, j: (i, 0))

    def wspec(width):
        return pl.BlockSpec((width, tn), lambda i, j: (0, j))

    def gspec(branch):
        return pl.BlockSpec((tm, tn), lambda i, j: (i, COL_MG // tn + branch * nt + j))

    return pl.pallas_call(
        _merge_kernel,
        grid=(m // tm, nt),
        in_specs=[aspec(RET_W), aspec(NSA_W), aspec(MEM_W), wspec(RET_W), wspec(NSA_W), wspec(MEM_W),
                  gspec(0), gspec(1), gspec(2)],
        out_specs=pl.BlockSpec((tm, tn), lambda i, j: (i, j)),
        out_shape=jax.ShapeDtypeStruct((m, D_MODEL), BF16),
        compiler_params=_cparams(2),
        name="merge",
    )(a_ret, a_nsa, a_mem, w_ret, w_nsa, w_mem, z, z, z)


def _out_kernel(a_ref, w_ref, x_ref, g_ref, o_ref):
    out = _dot(a_ref[...], w_ref[...])
    y = out * lax.rsqrt(jnp.mean(out * out, axis=-1, keepdims=True) + EPS)
    o_ref[...] = x_ref[...] + y * g_ref[...]


def _out_proj(merged, w_out, x, norm_post, tm):
    m = merged.shape[0]
    return pl.pallas_call(
        _out_kernel,
        grid=(m // tm,),
        in_specs=[pl.BlockSpec((tm, D_MODEL), lambda i: (i, 0)),
                  pl.BlockSpec((D_MODEL, D_MODEL), lambda i: (0, 0)),
                  pl.BlockSpec((tm, D_MODEL), lambda i: (i, 0)),
                  pl.BlockSpec((1, D_MODEL), lambda i: (0, 0))],
        out_specs=pl.BlockSpec((tm, D_MODEL), lambda i: (i, 0)),
        out_shape=jax.ShapeDtypeStruct((m, D_MODEL), F32),
        compiler_params=_cparams(1),
        name="out_proj",
    )(merged, w_out, x, norm_post.reshape(1, D_MODEL))


def _layout_w_in(w_in):
    splits = (RET_HEADS * RET_DK, RET_HEADS * RET_DK, RET_W, RET_W, NSA_W, 2 * KV_W, 2 * KV_W, 2 * KV_W,
              N_BRANCHES * NSA_HEADS, NSA_W, MEM_W, N_BRANCHES * D_MODEL)
    pieces, start = [], 0
    for w in splits:
        pieces.append(w_in[:, start:start + w])
        start += w
    rq, rk, rv, rg, nq, kvc, kvs, kvw, ng, nsl, mq, mg = pieces
    per_group = N_BRANCHES * NSA_GROUP
    ng = ng.reshape(D_MODEL, NSA_KV_HEADS, per_group)
    ng = jnp.pad(ng, ((0, 0), (0, 0), (0, LANE - per_group))).reshape(D_MODEL, NG_SLOT)
    return jnp.concatenate([rq, rk, rv, rg, nq, kvc, kvs, kvw, nsl, mq, mg, ng], axis=1).astype(BF16)


def _pick_tile(m, cap):
    t = min(m, cap)
    while m % t:
        t //= 2
    return t


def kernel(x_prompt, x_sample, cache_cmp_kv, cache_sel_kv, cache_win_kv, state_ret, cache_mem_kv, page_table,
           mem_prompt, rel_table, norm_pre, norm_post, norm_mem, w_in, ret_norm, w_ret_up, cmp_pos, w_cmp1,
           w_cmp2, w_nsa_up, w_mem_kv, w_mem_up, w_out):
    batch, seq, _ = x_prompt.shape
    nb = x_sample.shape[0]
    assert x_sample.shape[1] == 1 and norm_pre.shape[0] == 1
    assert seq % TQ == 0 and seq >= WINDOW
    n_pool = cache_cmp_kv.shape[1]
    n_pages = page_table.shape[1]
    past = n_pages * PAGE_SIZE
    n_mem = mem_prompt.shape[1]
    assert n_pages % PAGES_PER_STEP == 0 and cache_win_kv.shape[2] == WINDOW

    w_proj = _layout_w_in(w_in[0])
    kw = CMP_STRIDE * NSA_DK
    w1 = w_cmp1[0].reshape(2, CMP_BLOCK * NSA_DK, NSA_DK).astype(BF16)
    w1a, w1b = w1[:, :kw], w1[:, kw:]
    w2 = w_cmp2[0].astype(BF16)
    pos8 = jnp.pad(cmp_pos[0].reshape(2, 1, CMP_BLOCK * NSA_DK), ((0, 0), (0, 7), (0, 0)))
    w_ret = w_ret_up[0].astype(BF16)
    w_nsa = w_nsa_up[0].astype(BF16)
    w_mem = w_mem_up[0].astype(BF16)
    w_o = w_out[0].astype(BF16)
    w_mkv = w_mem_kv[0].astype(BF16)

    m_p = batch * seq
    xp = x_prompt.reshape(m_p, D_MODEL)
    z = _norm_matmul(xp, norm_pre[0], w_proj, _pick_tile(m_p, 1024), 512)

    a_ret, ret_state_p = _retention_prompt(z, ret_norm[0], batch, seq)

    ncp = max(LANE, -(-(seq // CMP_STRIDE) // LANE) * LANE)
    ac = _cmp_stage1_dense(z, w1a, w1b, batch, seq)
    kcvc = _cmp_stage2(ac, pos8, w1a, w1b, w2, 1)
    if ncp > kcvc.shape[3]:
        kcvc = jnp.pad(kcvc, ((0, 0), (0, 0), (0, 0), (0, ncp - kcvc.shape[3]), (0, 0)))
    a_nsa = _nsa_prompt(z, kcvc, rel_table, batch, seq)

    mem_kv_p = _norm_matmul(mem_prompt.reshape(batch * n_mem, D_MODEL), norm_mem[0], w_mkv,
                            _pick_tile(batch * n_mem, 512), 512)
    a_mem = _mem_prompt(z, mem_kv_p, batch, seq, _pick_tile(seq, 512))

    merged = _merge(a_ret, a_nsa, a_mem, w_ret, w_nsa, w_mem, z, _pick_tile(m_p, 512), 512)
    y_p = _out_proj(merged, w_o, xp, norm_post[0], _pick_tile(m_p, 256)).reshape(batch, seq, D_MODEL)

    kv_shape = (1, batch, seq, 2, NSA_KV_HEADS, NSA_DK)
    new_cmp_p = z[:, COL_KVC:COL_KVC + 2 * KV_W].reshape(kv_shape)
    new_sel_p = z[:, COL_KVS:COL_KVS + 2 * KV_W].reshape(kv_shape)
    new_win_p = z[:, COL_KVW:COL_KVW + 2 * KV_W].reshape(kv_shape)[:, :, seq - WINDOW:]
    new_ret_p = ret_state_p[None]
    new_mem_p = mem_kv_p.reshape(1, batch, n_mem, 2, MEM_HEADS, MEM_DH)

    xs = x_sample.reshape(nb, D_MODEL)
    zs = _norm_matmul(xs, norm_pre[0], w_proj, nb, 512)
    z3 = zs.reshape(nb, 1, PROJ_W)

    a_ret_s, ret_state_s = _retention_sample(z3, state_ret[0], ret_norm[0], past)

    cache_c = cache_cmp_kv[0].reshape(n_pool, PAGE_SIZE // CMP_STRIDE, CMP_STRIDE, 2 * NSA_KV_HEADS, NSA_DK)
    cache_s = cache_sel_kv[0].reshape(n_pool, PAGE_SIZE, 2 * KV_W)
    ac_s = _cmp_stage1_paged(cache_c, page_table, w1a, w1b)
    kcvc_s = _cmp_stage2(ac_s, pos8, w1a, w1b, w2, NSA_KV_HEADS)
    ncs = past // CMP_STRIDE
    kcvc_s = kcvc_s.reshape(nb, 2, ncs, KV_W)
    ns_s = past // SEL_BLOCK + 1
    nsp_s = -(-ns_s // LANE) * LANE
    dist_cs = past - (CMP_STRIDE * jnp.arange(ncs) + CMP_BLOCK - 1)
    bias_cs = _bias_of_dist(rel_table, dist_cs, dist_cs < 0)
    bias_cs = jnp.concatenate([bias_cs, jnp.broadcast_to(bias_cs[:, :1], (NSA_KV_HEADS, SROWS - NSA_GROUP, ncs))], 1)
    nblk = jnp.arange(ncs)[:, None]
    sblk = jnp.arange(nsp_s)[None, :]
    ov_s = ((nblk >= 4 * sblk - 1) & (nblk <= 4 * sblk + 3)).astype(BF16)
    o_cmp_s, idx_s = _nsa_sample_cmp(z3, kcvc_s, bias_cs, ov_s, ns_s)
    n_sel = min(SEL_TOPK, ns_s)
    idx = idx_s[:, :, 0, :n_sel]

    kpos = jnp.arange(ns_s * SEL_BLOCK)
    bias_sel = _bias_of_dist(rel_table, past - kpos, kpos > past)
    bias_sel = bias_sel.reshape(NSA_KV_HEADS, NSA_GROUP, ns_s, SEL_BLOCK).transpose(0, 2, 1, 3)
    bias_sel = jnp.concatenate([bias_sel, jnp.broadcast_to(bias_sel[:, :, :1], (NSA_KV_HEADS, ns_s, SROWS - NSA_GROUP, SEL_BLOCK))], 2)
    o_sel_s = _nsa_sample_sel(z3, cache_s, page_table, idx, bias_sel, ns_s)

    win_buf = cache_win_kv[0].reshape(nb, WINDOW, 2 * KV_W)
    dist_w = WINDOW - jnp.arange(WINDOW)
    bias_w = _bias_of_dist(rel_table, dist_w, (dist_w >= WINDOW) | (past - dist_w < 0))
    bias_w = jnp.concatenate([bias_w, jnp.broadcast_to(bias_w[:, :1], (NSA_KV_HEADS, SROWS - NSA_GROUP, WINDOW))], 1)
    bias_n = _bias_of_dist(rel_table, jnp.zeros((1,), jnp.int32), jnp.zeros((1,), bool))
    bias_n = jnp.concatenate([bias_n, jnp.broadcast_to(bias_n[:, :1], (NSA_KV_HEADS, SROWS - NSA_GROUP, 1))], 1)
    o_win_s = _nsa_sample_win(z3, win_buf, bias_w, bias_n)
    a_nsa_s = _nsa_sample_gate(o_cmp_s, o_sel_s, o_win_s, z3)

    mem_kv_s = cache_mem_kv[0].reshape(nb, n_mem, 2 * MEM_W)
    a_mem_s = _mem_sample(z3, mem_kv_s)

    merged_s = _merge(a_ret_s.reshape(nb, RET_W), a_nsa_s.reshape(nb, NSA_W), a_mem_s.reshape(nb, MEM_W),
                      w_ret, w_nsa, w_mem, zs, nb, 512)
    y_s = _out_proj(merged_s, w_o, xs, norm_post[0], nb).reshape(nb, 1, D_MODEL)

    kvs_shape = (1, nb, 1, 2, NSA_KV_HEADS, NSA_DK)
    new_cmp_s = zs[:, COL_KVC:COL_KVC + 2 * KV_W].reshape(kvs_shape)
    new_sel_s = zs[:, COL_KVS:COL_KVS + 2 * KV_W].reshape(kvs_shape)
    kvw_s = zs[:, COL_KVW:COL_KVW + 2 * KV_W].reshape(nb, 1, 2, NSA_KV_HEADS, NSA_DK)
    new_win_s = jnp.concatenate([cache_win_kv[0][:, 1:], kvw_s], axis=1)[None]
    new_ret_s = ret_state_s[None]

    return (y_p, y_s, new_cmp_p, new_sel_p, new_win_p, new_ret_p, new_mem_p,
            new_cmp_s, new_sel_s, new_win_s, new_ret_s)
```

```python
import functools
import math

import jax
import jax.numpy as jnp
from jax import lax
from jax.experimental import pallas as pl
from jax.experimental.pallas import tpu as pltpu

F32 = jnp.float32
BF16 = jnp.bfloat16

D_MODEL = 2048
PAGE_SIZE = 128
RET_HEADS = 8
RET_DK = 256
RET_DV = 256
RET_CHUNK = 128
ROPE_BASE = 10000.0
NSA_HEADS = 16
NSA_KV_HEADS = 4
NSA_GROUP = NSA_HEADS // NSA_KV_HEADS
NSA_DK = 128
CMP_BLOCK = 32
CMP_STRIDE = 16
SEL_BLOCK = 64
SEL_TOPK = 16
WINDOW = 512
MEM_HEADS = 4
MEM_DH = 384
REL_BUCKETS = 32
REL_MAX_EXACT = 16
REL_MAX_DIST = 128
N_BRANCHES = 3
EPS = 1e-6
NEG_INF = -1e30
FORCE_SCORE = 1e4

RET_W = RET_HEADS * RET_DV
NSA_W = NSA_HEADS * NSA_DK
KV_W = NSA_KV_HEADS * NSA_DK
MEM_W = MEM_HEADS * MEM_DH

COL_RQ = 0
COL_RK = COL_RQ + RET_HEADS * RET_DK
COL_RV = COL_RK + RET_HEADS * RET_DK
COL_RG = COL_RV + RET_W
COL_NQ = COL_RG + RET_W
COL_KVC = COL_NQ + NSA_W
COL_KVS = COL_KVC + 2 * KV_W
COL_KVW = COL_KVS + 2 * KV_W
COL_NSL = COL_KVW + 2 * KV_W
COL_MQ = COL_NSL + NSA_W
COL_MG = COL_MQ + MEM_W
COL_NG = COL_MG + N_BRANCHES * D_MODEL
NG_SLOT = NSA_KV_HEADS * 128
PROJ_W = COL_NG + NG_SLOT

LANE = 128
TQ = 256
TK = 256
NSA_ROWS = NSA_GROUP * TQ
BIAS_DISTS = 1024
VMEM_LIMIT = 48 * 1024 * 1024


def _cparams(n_axes):
    return pltpu.CompilerParams(dimension_semantics=("arbitrary",) * n_axes, vmem_limit_bytes=VMEM_LIMIT)


def _nt(a, b):
    return lax.dot_general(a, b, (((1,), (1,)), ((), ())), preferred_element_type=F32)


def _dot(a, b):
    return jnp.dot(a, b, preferred_element_type=F32)


def _sigmoid(x):
    return 1.0 / (1.0 + jnp.exp(-x))


def _iota(shape, dim):
    return lax.broadcasted_iota(jnp.int32, shape, dim)


def _norm_matmul_kernel(x_ref, g_ref, w_ref, o_ref, h_ref):
    @pl.when(pl.program_id(1) == 0)
    def _():
        x = x_ref[...]
        ms = jnp.mean(x * x, axis=-1, keepdims=True)
        h_ref[...] = ((x * lax.rsqrt(ms + EPS)) * g_ref[...]).astype(BF16)

    o_ref[...] = _dot(h_ref[...], w_ref[...])


def _norm_matmul(x, g, w, tm, tn):
    m, k = x.shape
    n = w.shape[1]
    return pl.pallas_call(
        _norm_matmul_kernel,
        grid=(m // tm, n // tn),
        in_specs=[pl.BlockSpec((tm, k), lambda i, j: (i, 0)),
                  pl.BlockSpec((1, k), lambda i, j: (0, 0)),
                  pl.BlockSpec((k, tn), lambda i, j: (0, j))],
        out_specs=pl.BlockSpec((tm, tn), lambda i, j: (i, j)),
        out_shape=jax.ShapeDtypeStruct((m, n), F32),
        scratch_shapes=[pltpu.VMEM((tm, k), BF16)],
        compiler_params=_cparams(2),
        name="norm_matmul",
    )(x, g.reshape(1, k), w)


def _rope_rows(x, cos, sin):
    half = x.shape[-1] // 2
    x1, x2 = x[:, :half], x[:, half:]
    return jnp.concatenate([x1 * cos - x2 * sin, x1 * sin + x2 * cos], axis=-1)


def _head_norm_gate(o, gnorm, rg):
    oc = o - jnp.mean(o, axis=-1, keepdims=True)
    y = oc * lax.rsqrt(jnp.mean(oc * oc, axis=-1, keepdims=True) + EPS) * gnorm
    return y * (rg * _sigmoid(rg))


def _ret_prompt_kernel(q_ref, k_ref, v_ref, rg_ref, cos_ref, sin_ref, dmat_ref, xi_ref, zeta_ref, gc_ref,
                       gn_ref, a_ref, s_ref):
    @pl.when(pl.program_id(2) == 0)
    def _():
        s_ref[...] = jnp.zeros_like(s_ref)

    cos, sin = cos_ref[...], sin_ref[...]
    q = _rope_rows(q_ref[...], cos, sin)
    k = _rope_rows(k_ref[...], cos, sin) * (RET_DK ** -0.5)
    qb, vb = q.astype(BF16), v_ref[...].astype(BF16)
    state = s_ref[0, 0]
    inner = _nt(qb, k.astype(BF16)) * dmat_ref[0]
    o = _dot(inner.astype(BF16), vb) + _dot(qb, state.astype(BF16)) * xi_ref[0]
    kz_t = (k * zeta_ref[0]).T.astype(BF16)
    s_ref[0, 0] = state * gc_ref[0] + _dot(kz_t, vb)
    a_ref[...] = _head_norm_gate(o, gn_ref[...], rg_ref[...]).astype(BF16)


def _decay_tables(chunk):
    log_g = jnp.log1p(-jnp.exp2(-5.0 - jnp.arange(RET_HEADS, dtype=F32)))
    i = jnp.arange(chunk, dtype=F32)
    diff = i[:, None] - i[None, :]
    dmat = jnp.where(diff >= 0, jnp.exp(log_g[:, None, None] * jnp.maximum(diff, 0.0)), 0.0)
    xi = jnp.exp(log_g[:, None] * (i[None, :] + 1.0))[:, :, None]
    zeta = jnp.exp(log_g[:, None] * (chunk - 1.0 - i[None, :]))[:, :, None]
    g_chunk = jnp.exp(log_g * chunk)[:, None, None]
    return dmat, xi, zeta, g_chunk


def _rope_tables(pos):
    half = RET_DK // 2
    freq = jnp.power(ROPE_BASE, -jnp.arange(half, dtype=F32) / half)
    ang = pos.astype(F32)[:, None] * freq[None, :]
    return jnp.cos(ang), jnp.sin(ang)


def _retention_prompt(z, ret_norm, batch, seq):
    c = RET_CHUNK
    nc = seq // c
    dmat, xi, zeta, g_chunk = _decay_tables(c)
    cos, sin = _rope_tables(jnp.arange(seq))
    hb = RET_DK

    def zspec(col0):
        return pl.BlockSpec((c, hb), lambda b, h, t, col0=col0: (b * nc + t, col0 // hb + h))

    per_head = lambda shape: pl.BlockSpec((1,) + shape, lambda b, h, t: (h, 0, 0))
    return pl.pallas_call(
        _ret_prompt_kernel,
        grid=(batch, RET_HEADS, nc),
        in_specs=[zspec(COL_RQ), zspec(COL_RK), zspec(COL_RV), zspec(COL_RG),
                  pl.BlockSpec((c, hb // 2), lambda b, h, t: (t, 0)),
                  pl.BlockSpec((c, hb // 2), lambda b, h, t: (t, 0)),
                  per_head((c, c)), per_head((c, 1)), per_head((c, 1)), per_head((1, 1)),
                  pl.BlockSpec((1, hb), lambda b, h, t: (0, h))],
        out_specs=[pl.BlockSpec((c, hb), lambda b, h, t: (b * nc + t, h)),
                   pl.BlockSpec((1, 1, RET_DK, RET_DV), lambda b, h, t: (b, h, 0, 0))],
        out_shape=[jax.ShapeDtypeStruct((batch * seq, RET_W), BF16),
                   jax.ShapeDtypeStruct((batch, RET_HEADS, RET_DK, RET_DV), F32)],
        compiler_params=_cparams(3),
        name="retention_prompt",
    )(z, z, z, z, cos, sin, dmat, xi, zeta, g_chunk, ret_norm.reshape(1, RET_W))


def _column_of(row):
    n = row.shape[1]
    eye = _iota((n, n), 0) == _iota((n, n), 1)
    return jnp.sum(jnp.where(eye, jnp.broadcast_to(row, (n, n)), 0.0), axis=1, keepdims=True)


def _ret_sample_kernel(q_ref, k_ref, v_ref, rg_ref, cos_ref, sin_ref, gam_ref, gn_ref, s_ref, a_ref, so_ref):
    cos, sin = cos_ref[...], sin_ref[...]
    q = _rope_rows(q_ref[0], cos, sin)
    k = _rope_rows(k_ref[0], cos, sin) * (RET_DK ** -0.5)
    v = v_ref[0]
    state = s_ref[0, 0]
    gamma = gam_ref[0]
    qk = jnp.sum(q * k, axis=-1, keepdims=True)
    o = qk * v + jnp.sum(_column_of(q) * state, axis=0, keepdims=True) * gamma
    so_ref[0, 0] = state * gamma + _column_of(k) * v
    a_ref[0] = _head_norm_gate(o, gn_ref[...], rg_ref[0]).astype(BF16)


def _retention_sample(z3, state, ret_norm, pos):
    nb = z3.shape[0]
    cos, sin = _rope_tables(jnp.full((1,), pos))
    gamma = jnp.exp(jnp.log1p(-jnp.exp2(-5.0 - jnp.arange(RET_HEADS, dtype=F32))))[:, None, None]
    hb = RET_DK

    def zspec(col0):
        return pl.BlockSpec((1, 1, hb), lambda b, h, col0=col0: (b, 0, col0 // hb + h))

    st_spec = pl.BlockSpec((1, 1, RET_DK, RET_DV), lambda b, h: (b, h, 0, 0))
    return pl.pallas_call(
        _ret_sample_kernel,
        grid=(nb, RET_HEADS),
        in_specs=[zspec(COL_RQ), zspec(COL_RK), zspec(COL_RV), zspec(COL_RG),
                  pl.BlockSpec((1, hb // 2), lambda b, h: (0, 0)),
                  pl.BlockSpec((1, hb // 2), lambda b, h: (0, 0)),
                  pl.BlockSpec((1, 1, 1), lambda b, h: (h, 0, 0)),
                  pl.BlockSpec((1, hb), lambda b, h: (0, h)),
                  st_spec],
        out_specs=[pl.BlockSpec((1, 1, hb), lambda b, h: (b, 0, h)), st_spec],
        out_shape=[jax.ShapeDtypeStruct((nb, 1, RET_W), BF16),
                   jax.ShapeDtypeStruct(state.shape, F32)],
        compiler_params=_cparams(2),
        name="retention_sample",
    )(z3, z3, z3, z3, cos, sin, gamma, ret_norm.reshape(1, RET_W), state)


def _half_rows(ref_slice_fn, n_half):
    return jnp.concatenate([ref_slice_fn(p) for p in range(CMP_STRIDE)], axis=1)


def _cmp_stage1_dense_kernel(x_ref, w1a_ref, w1b_ref, o_ref):
    nh = o_ref.shape[3]
    x = _half_rows(lambda p: x_ref[pl.ds(p, nh, stride=CMP_STRIDE), :], nh).astype(BF16)
    o_ref[0, 0, 0] = jnp.concatenate([_dot(x, w1a_ref[0]), _dot(x, w1b_ref[0])], axis=1)


def _cmp_stage1_dense(z, w1a, w1b, batch, seq):
    nh = seq // CMP_STRIDE
    return pl.pallas_call(
        _cmp_stage1_dense_kernel,
        grid=(batch, 2, NSA_KV_HEADS),
        in_specs=[pl.BlockSpec((seq, NSA_DK), lambda b, kv, h: (b, COL_KVC // NSA_DK + kv * NSA_KV_HEADS + h)),
                  pl.BlockSpec((1, CMP_STRIDE * NSA_DK, NSA_DK), lambda b, kv, h: (kv, 0, 0)),
                  pl.BlockSpec((1, CMP_STRIDE * NSA_DK, NSA_DK), lambda b, kv, h: (kv, 0, 0))],
        out_specs=pl.BlockSpec((1, 1, 1, nh, 2 * NSA_DK), lambda b, kv, h: (b, kv, h, 0, 0)),
        out_shape=jax.ShapeDtypeStruct((batch, 2, NSA_KV_HEADS, nh, 2 * NSA_DK), F32),
        compiler_params=_cparams(3),
        name="cmp_stage1_dense",
    )(z, w1a, w1b)


PAGES_PER_STEP = 8


def _cmp_stage1_paged_kernel(pt_ref, *refs):
    pages, (w1a_ref, w1b_ref, o_ref) = refs[:PAGES_PER_STEP], refs[PAGES_PER_STEP:]
    hp = PAGE_SIZE // CMP_STRIDE
    top = _iota((2 * NSA_KV_HEADS, NSA_DK), 0) < NSA_KV_HEADS
    cols = [[], []]
    for p in range(CMP_STRIDE):
        tiles = [[], []]
        for pg in pages:
            xp = pg[0, :, p]
            for n in range(0, hp, 2):
                a, b = xp[n], xp[n + 1]
                tiles[0].append(jnp.where(top, a, pltpu.roll(b, NSA_KV_HEADS, 0)))
                tiles[1].append(jnp.where(top, pltpu.roll(a, NSA_KV_HEADS, 0), b))
        for kv in range(2):
            cols[kv].append(jnp.concatenate(tiles[kv], axis=0))
    for kv in range(2):
        x = jnp.concatenate(cols[kv], axis=1).astype(BF16)
        o_ref[0, kv, 0] = jnp.concatenate([_dot(x, w1a_ref[kv]), _dot(x, w1b_ref[kv])], axis=1)


def _cmp_stage1_paged(cache, page_table, w1a, w1b):
    nb, n_pages = page_table.shape
    hp = PAGE_SIZE // CMP_STRIDE
    steps = n_pages // PAGES_PER_STEP
    rows = PAGES_PER_STEP * hp * NSA_KV_HEADS

    def page_spec(j):
        return pl.BlockSpec((1, hp, CMP_STRIDE, 2 * NSA_KV_HEADS, NSA_DK),
                            lambda b, s, pt, j=j: (pt[b, s * PAGES_PER_STEP + j], 0, 0, 0, 0))

    wspec = pl.BlockSpec((2, CMP_STRIDE * NSA_DK, NSA_DK), lambda b, s, pt: (0, 0, 0))
    return pl.pallas_call(
        _cmp_stage1_paged_kernel,
        grid_spec=pltpu.PrefetchScalarGridSpec(
            num_scalar_prefetch=1,
            grid=(nb, steps),
            in_specs=[page_spec(j) for j in range(PAGES_PER_STEP)] + [wspec, wspec],
            out_specs=pl.BlockSpec((1, 2, 1, rows, 2 * NSA_DK), lambda b, s, pt: (b, 0, 0, s, 0))),
        out_shape=jax.ShapeDtypeStruct((nb, 2, 1, steps * rows, 2 * NSA_DK), F32),
        compiler_params=_cparams(2),
        name="cmp_stage1_paged",
    )(page_table, *([cache] * PAGES_PER_STEP), w1a, w1b)


def _cmp_stage2_kernel(ac_ref, pos_ref, w1a_ref, w1b_ref, w2_ref, o_ref, *, shift):
    ac = ac_ref[0, 0, 0]
    nh = ac.shape[0]
    pos = pos_ref[0].astype(BF16)
    kw = CMP_STRIDE * NSA_DK
    pe = _dot(pos[:, :kw], w1a_ref[0]) + _dot(pos[:, kw:], w1b_ref[0])
    pre = ac[:, :NSA_DK] + pltpu.roll(ac[:, NSA_DK:], nh - shift, 0) + pe[0:1]
    gelu = 0.5 * pre * (1.0 + jnp.tanh(math.sqrt(2.0 / math.pi) * (pre + 0.044715 * (pre * pre * pre))))
    o_ref[0, 0, 0] = _dot(gelu.astype(BF16), w2_ref[0]).astype(BF16)


def _cmp_stage2(ac, pos8, w1a, w1b, w2, shift):
    nb, _, groups, nh, _ = ac.shape
    kw = CMP_STRIDE * NSA_DK
    return pl.pallas_call(
        functools.partial(_cmp_stage2_kernel, shift=shift),
        grid=(nb, 2, groups),
        in_specs=[pl.BlockSpec((1, 1, 1, nh, 2 * NSA_DK), lambda b, kv, h: (b, kv, h, 0, 0)),
                  pl.BlockSpec((1, 8, 2 * kw), lambda b, kv, h: (kv, 0, 0)),
                  pl.BlockSpec((1, kw, NSA_DK), lambda b, kv, h: (kv, 0, 0)),
                  pl.BlockSpec((1, kw, NSA_DK), lambda b, kv, h: (kv, 0, 0)),
                  pl.BlockSpec((1, NSA_DK, NSA_DK), lambda b, kv, h: (kv, 0, 0))],
        out_specs=pl.BlockSpec((1, 1, 1, nh, NSA_DK), lambda b, kv, h: (b, kv, h, 0, 0)),
        out_shape=jax.ShapeDtypeStruct((nb, 2, groups, nh, NSA_DK), BF16),
        compiler_params=_cparams(3),
        name="cmp_stage2",
    )(ac, pos8, w1a, w1b, w2)


def _rel_bucket(dist):
    n = jnp.maximum(dist, 0)
    nf = jnp.maximum(n, 1).astype(F32)
    scale = (REL_BUCKETS - REL_MAX_EXACT) / math.log(REL_MAX_DIST / REL_MAX_EXACT)
    large = REL_MAX_EXACT + (jnp.log(nf / REL_MAX_EXACT) * scale).astype(jnp.int32)
    large = jnp.minimum(large, REL_BUCKETS - 1)
    return jnp.where(n < REL_MAX_EXACT, n, large)


def _bias_by_dist(rel_table, n):
    tab = rel_table[_rel_bucket(jnp.arange(n))]
    return tab.T.reshape(NSA_KV_HEADS, NSA_GROUP, n)


def _pad_group_rows(t, axis):
    first = lax.slice_in_dim(t, 0, 1, axis=axis)
    return jnp.concatenate([t] + [first] * (SROWS - NSA_GROUP), axis=axis)


def _flash_init(m_ref, l_ref, acc_ref):
    m_ref[...] = jnp.full(m_ref.shape, NEG_INF, F32)
    l_ref[...] = jnp.zeros(l_ref.shape, F32)
    acc_ref[...] = jnp.zeros(acc_ref.shape, F32)


def _flash_step(s, v, m_ref, l_ref, acc_ref):
    m_old = m_ref[...]
    m_new = jnp.maximum(m_old, jnp.max(s, axis=1, keepdims=True))
    alpha = jnp.exp(m_old - m_new)
    p = jnp.exp(s - m_new)
    l_ref[...] = alpha * l_ref[...] + jnp.sum(p, axis=1, keepdims=True)
    acc_ref[...] = alpha * acc_ref[...] + _dot(p.astype(BF16), v)
    m_ref[...] = m_new


def _flash_result(l_ref, acc_ref):
    return acc_ref[...] / jnp.maximum(l_ref[...], 1e-30)


def _select_blocks(imp_t, q0, ns):
    shape = imp_t.shape
    blk = _iota(shape, 0)
    qpos = q0 + _iota(shape, 1)
    cur = qpos >> 6
    valid = blk * SEL_BLOCK <= qpos
    forced = (blk == 0) | (blk == cur) | (blk == cur - 1)
    imp_t = jnp.where(valid, jnp.where(forced, FORCE_SCORE, imp_t), NEG_INF)
    rank = jnp.zeros(shape, F32)
    for other in range(ns):
        row = imp_t[other:other + 1, :]
        ahead = (row > imp_t) | ((row == imp_t) & (blk > other))
        rank = rank + jnp.where(ahead, 1.0, 0.0)
    return jnp.where((rank < SEL_TOPK) & valid, 1.0, 0.0)


def _nsa_prompt_kernel(q_ref, ks_ref, vst_ref, kw_ref, vwt_ref, kc_ref, vct_ref, tz_ref, cfar_ref, basec_ref,
                       ovt_ref, ng_ref, nsl_ref, o_ref, m_ref, l_ref, acc_ref, *, ns):
    i = pl.program_id(2)
    q0 = i * TQ
    scale = NSA_DK ** -0.5
    qall = q_ref[...]
    qt = jnp.concatenate([qall[:, g * NSA_DK:(g + 1) * NSA_DK].T for g in range(NSA_GROUP)], axis=1).astype(BF16)
    c_loc = _iota((TK, NSA_ROWS), 0)
    r_loc = _iota((TK, NSA_ROWS), 1) & (TQ - 1)

    ncp = kc_ref.shape[3]
    shift = (TQ // CMP_STRIDE) * i
    bias_c = basec_ref[0, pl.ds(pl.multiple_of(ncp - shift, TQ // CMP_STRIDE), ncp), :]
    s = _dot(kc_ref[0, 0, 0], qt) * scale + bias_c
    m = jnp.max(s, axis=0, keepdims=True)
    e = jnp.exp(s - m)
    inv = jnp.where(m > 0.5 * NEG_INF, 1.0 / jnp.maximum(jnp.sum(e, axis=0, keepdims=True), 1e-30), 0.0)
    p = e * inv
    o_cmp = _dot(vct_ref[0, 0], p.astype(BF16))

    psum = p[:, 0:TQ] + p[:, TQ:2 * TQ] + p[:, 2 * TQ:3 * TQ] + p[:, 3 * TQ:4 * TQ]
    hi = psum.astype(BF16)
    lo = (psum - hi.astype(F32)).astype(BF16)
    ovt = ovt_ref[...]
    imp_t = _dot(ovt, hi) + _dot(ovt, lo)
    ns8 = -(-ns // 8) * 8
    sel_t = _select_blocks(imp_t[:ns8], q0, ns)
    nsp = ovt.shape[0]
    if ns8 < nsp:
        sel_t = jnp.concatenate([sel_t, jnp.zeros((nsp - ns8, TQ), F32)], axis=0)
    sel_t = sel_t.astype(BF16)

    def flash_step(sc, vt):
        m_old = m_ref[...]
        m_new = jnp.maximum(m_old, jnp.max(sc, axis=0, keepdims=True))
        alpha = jnp.exp(m_old - m_new)
        pt = jnp.exp(sc - m_new)
        l_ref[...] = alpha * l_ref[...] + jnp.sum(pt, axis=0, keepdims=True)
        acc_ref[...] = alpha * acc_ref[...] + _dot(vt, pt.astype(BF16))
        m_ref[...] = m_new

    def sel_tile(kt, bias, causal):
        k = ks_ref[pl.ds(pl.multiple_of(kt * TK, TK), TK), :].astype(BF16)
        blk_of_key = (TK // SEL_BLOCK) * kt + (_iota((TK, nsp), 0) >> 6)
        expand = jnp.where(_iota((TK, nsp), 1) == blk_of_key, 1.0, 0.0).astype(BF16)
        mk = _dot(expand, sel_t)
        ok = jnp.concatenate([mk] * NSA_GROUP, axis=1) > 0.5
        if causal:
            ok = ok & (c_loc <= r_loc)
        sc = jnp.where(ok, _dot(k, qt) * scale + bias, NEG_INF)
        flash_step(sc, vst_ref[0, 0, kt])

    _flash_init(m_ref, l_ref, acc_ref)
    sel_tile(i, tz_ref[0, 0], True)

    @pl.when(i >= 1)
    def _():
        sel_tile(i - 1, tz_ref[0, 1], False)

    def far_tile(t, carry):
        sel_tile(i - t, cfar_ref[0], False)
        return carry

    lax.fori_loop(2, i + 1, far_tile, 0)
    o_sel = _flash_result(l_ref, acc_ref)

    def win_tile(off):
        kt = i - off
        k = kw_ref[pl.ds(pl.multiple_of(kt * TK, TK), TK), :].astype(BF16)
        sc = _dot(k, qt) * scale + (tz_ref[0, off] if off < 2 else cfar_ref[0])
        if off == 0:
            sc = jnp.where(c_loc <= r_loc, sc, NEG_INF)
        if off * TK == WINDOW:
            sc = jnp.where(c_loc > r_loc, sc, NEG_INF)
        flash_step(sc, vwt_ref[0, 0, kt])

    _flash_init(m_ref, l_ref, acc_ref)
    win_tile(0)
    for off in range(1, WINDOW // TK + 1):
        pl.when(i >= off)(functools.partial(win_tile, off))
    o_win = _flash_result(l_ref, acc_ref)

    gates_t = _sigmoid(ng_ref[...]).T
    nsl = nsl_ref[...]
    outs = []
    for g in range(NSA_GROUP):
        cols = slice(g * TQ, (g + 1) * TQ)
        o_t = (gates_t[3 * g:3 * g + 1] * o_cmp[:, cols] + gates_t[3 * g + 1:3 * g + 2] * o_sel[:, cols]
               + gates_t[3 * g + 2:3 * g + 3] * o_win[:, cols])
        x = nsl[:, g * NSA_DK:(g + 1) * NSA_DK]
        outs.append(o_t.T * (x * _sigmoid(x)))
    o_ref[...] = jnp.concatenate(outs, axis=1).astype(BF16)


def _lanes_by_head(t):
    hk, g, rows, tq = t.shape
    return t.transpose(0, 2, 1, 3).reshape(hk, rows, g * tq)


def _nsa_prompt(z, kcvc, bias_d, batch, seq):
    nq = seq // TQ
    ns = seq // SEL_BLOCK
    ncp = kcvc.shape[3]
    nsp = LANE
    wn = TQ // CMP_STRIDE
    assert TQ == TK and ns <= nsp and ncp >= seq // CMP_STRIDE and ncp % LANE == 0 and ncp > wn
    gw = NSA_GROUP * NSA_DK
    n_dist = bias_d.shape[-1]
    assert n_dist >= 2 * TK + TQ

    rep = jnp.tile(bias_d, (1, 1, TK + 1))[..., :TK * (n_dist - 1)].reshape(NSA_KV_HEADS, NSA_GROUP, TK, n_dist - 1)
    tz = jnp.stack([_lanes_by_head(rep[..., 0:TQ]), _lanes_by_head(rep[..., TK:TK + TQ])], axis=1)
    far = jnp.broadcast_to(bias_d[..., REL_MAX_DIST][:, :, None, None], (NSA_KV_HEADS, NSA_GROUP, 1, TQ))
    cfar = _lanes_by_head(far)
    n_rel = jnp.arange(-wn, wn)[:, None]
    dist_c = jnp.arange(TQ)[None, :] - CMP_STRIDE * n_rel - (CMP_BLOCK - 1)
    near = jnp.where(dist_c >= 0, jnp.take(bias_d, jnp.clip(dist_c, 0, n_dist - 1), axis=-1), NEG_INF)
    basec = jnp.concatenate([jnp.broadcast_to(far, (NSA_KV_HEADS, NSA_GROUP, ncp - wn, TQ)), near,
                             jnp.full((NSA_KV_HEADS, NSA_GROUP, ncp - wn, TQ), NEG_INF, F32)], axis=2)
    basec = _lanes_by_head(basec)
    sblk = jnp.arange(nsp)[:, None]
    nblk = jnp.arange(ncp)[None, :]
    ovt = ((nblk >= 4 * sblk - 1) & (nblk <= 4 * sblk + 3)).astype(BF16)

    def v_tiles(col0):
        v = z[:, col0 + KV_W:col0 + 2 * KV_W].reshape(batch, seq // TK, TK, NSA_KV_HEADS, NSA_DK)
        return v.transpose(0, 3, 1, 4, 2).astype(BF16)

    vct = kcvc[:, 1].transpose(0, 1, 3, 2)

    def kspec(col0):
        return pl.BlockSpec((seq, NSA_DK), lambda b, h, i: (b, col0 // NSA_DK + h))

    vspec = pl.BlockSpec((1, 1, seq // TK, NSA_DK, TK), lambda b, h, i: (b, h, 0, 0, 0))
    return pl.pallas_call(
        functools.partial(_nsa_prompt_kernel, ns=ns),
        grid=(batch, NSA_KV_HEADS, nq),
        in_specs=[pl.BlockSpec((TQ, gw), lambda b, h, i: (b * nq + i, COL_NQ // gw + h)),
                  kspec(COL_KVS), vspec, kspec(COL_KVW), vspec,
                  pl.BlockSpec((1, 1, 1, ncp, NSA_DK), lambda b, h, i: (b, 0, h, 0, 0)),
                  pl.BlockSpec((1, 1, NSA_DK, ncp), lambda b, h, i: (b, h, 0, 0)),
                  pl.BlockSpec((1, 2, TK, NSA_ROWS), lambda b, h, i: (h, 0, 0, 0)),
                  pl.BlockSpec((1, 1, NSA_ROWS), lambda b, h, i: (h, 0, 0)),
                  pl.BlockSpec((1, 2 * ncp, NSA_ROWS), lambda b, h, i: (h, 0, 0)),
                  pl.BlockSpec((nsp, ncp), lambda b, h, i: (0, 0)),
                  pl.BlockSpec((TQ, LANE), lambda b, h, i: (b * nq + i, COL_NG // LANE + h)),
                  pl.BlockSpec((TQ, gw), lambda b, h, i: (b * nq + i, COL_NSL // gw + h))],
        out_specs=pl.BlockSpec((TQ, gw), lambda b, h, i: (b * nq + i, h)),
        out_shape=jax.ShapeDtypeStruct((batch * seq, NSA_W), BF16),
        scratch_shapes=[pltpu.VMEM((1, NSA_ROWS), F32), pltpu.VMEM((1, NSA_ROWS), F32),
                        pltpu.VMEM((NSA_DK, NSA_ROWS), F32)],
        compiler_params=_cparams(3),
        name="nsa_prompt",
    )(z, z, v_tiles(COL_KVS), z, v_tiles(COL_KVW), kcvc, vct, tz, cfar, basec, ovt, z, z)


SROWS = 8


def _stack_group_q(q_row):
    heads = [q_row[:, g * NSA_DK:(g + 1) * NSA_DK] for g in range(NSA_GROUP)]
    return jnp.concatenate(heads + [heads[0]] * (SROWS - NSA_GROUP), axis=0)


def _nsa_sample_cmp_kernel(q_ref, kc_ref, vc_ref, bias_ref, ov_ref, o_ref, idx_ref, *, ns):
    scale = NSA_DK ** -0.5
    qs = _stack_group_q(q_ref[0]).astype(BF16)
    s = _nt(qs, kc_ref[0, 0]) * scale + bias_ref[0]
    m = jnp.max(s, axis=1, keepdims=True)
    e = jnp.exp(s - m)
    inv = jnp.where(m > 0.5 * NEG_INF, 1.0 / jnp.maximum(jnp.sum(e, axis=1, keepdims=True), 1e-30), 0.0)
    p = e * inv
    o_ref[0, 0] = _dot(p.astype(BF16), vc_ref[0, 0])
    psum = jnp.broadcast_to(jnp.sum(p[0:NSA_GROUP], axis=0, keepdims=True), p.shape)
    hi = psum.astype(BF16)
    lo = (psum - hi.astype(F32)).astype(BF16)
    imp = (_dot(hi, ov_ref[...]) + _dot(lo, ov_ref[...]))[0:1]
    nsp = imp.shape[1]
    blk_r = _iota((1, nsp), 1)
    cur = ns - 1
    forced = (blk_r == 0) | (blk_r == cur) | (blk_r == cur - 1)
    imp = jnp.where(blk_r < ns, jnp.where(forced, FORCE_SCORE, imp), 2.0 * NEG_INF)
    imp_c = _column_of(imp)
    i_r = _iota((nsp, nsp), 1)
    j_c = _iota((nsp, nsp), 0)
    ahead = (imp > imp_c) | ((imp == imp_c) & (i_r < j_c))
    rank_c = jnp.sum(jnp.where(ahead, 1.0, 0.0), axis=1, keepdims=True)
    slot = _iota((nsp, LANE), 1).astype(F32)
    picks = jnp.where(rank_c == slot, _iota((nsp, LANE), 0).astype(F32), 0.0)
    idx_ref[0, 0] = jnp.broadcast_to(jnp.sum(picks, axis=0, keepdims=True), (SROWS, LANE)).astype(jnp.int32)


def _nsa_sample_cmp(z3, kcvc, bias_c, ov, ns):
    nb = z3.shape[0]
    ncp = kcvc.shape[2]
    nsp = ov.shape[1]
    gw = NSA_GROUP * NSA_DK

    def cspec(which):
        return pl.BlockSpec((1, 1, ncp, NSA_DK), lambda b, h: (b, which, 0, h))

    return pl.pallas_call(
        functools.partial(_nsa_sample_cmp_kernel, ns=ns),
        grid=(nb, NSA_KV_HEADS),
        in_specs=[pl.BlockSpec((1, 1, gw), lambda b, h: (b, 0, COL_NQ // gw + h)),
                  cspec(0), cspec(1),
                  pl.BlockSpec((1, SROWS, ncp), lambda b, h: (h, 0, 0)),
                  pl.BlockSpec((ncp, nsp), lambda b, h: (0, 0))],
        out_specs=[pl.BlockSpec((1, 1, SROWS, NSA_DK), lambda b, h: (b, h, 0, 0)),
                   pl.BlockSpec((1, 1, SROWS, LANE), lambda b, h: (b, h, 0, 0))],
        out_shape=[jax.ShapeDtypeStruct((nb, NSA_KV_HEADS, SROWS, NSA_DK), F32),
                   jax.ShapeDtypeStruct((nb, NSA_KV_HEADS, SROWS, LANE), jnp.int32)],
        compiler_params=_cparams(2),
        name="nsa_sample_cmp",
    )(z3, kcvc, kcvc, bias_c, ov)


def _nsa_sample_sel_kernel(pt_ref, idx_ref, q_ref, k_ref, v_ref, kn_ref, vn_ref, bias_ref, o_ref,
                           m_ref, l_ref, acc_ref, *, ns):
    b, h, t = pl.program_id(0), pl.program_id(1), pl.program_id(2)

    @pl.when(t == 0)
    def _():
        _flash_init(m_ref, l_ref, acc_ref)

    is_new = idx_ref[b, h, t] == ns - 1
    k = jnp.where(is_new, jnp.broadcast_to(kn_ref[0], (SEL_BLOCK, NSA_DK)), k_ref[0])
    v = jnp.where(is_new, jnp.broadcast_to(vn_ref[0], (SEL_BLOCK, NSA_DK)), v_ref[0])
    qs = _stack_group_q(q_ref[0]).astype(BF16)
    sc = _nt(qs, k.astype(BF16)) * (NSA_DK ** -0.5) + bias_ref[0, 0]
    _flash_step(sc, v.astype(BF16), m_ref, l_ref, acc_ref)

    @pl.when(t == pl.num_programs(2) - 1)
    def _():
        o_ref[0, 0] = _flash_result(l_ref, acc_ref)


def _nsa_sample_sel(z3, cache, page_table, idx, bias_sel, ns):
    nb, n_pages = page_table.shape
    n_sel = idx.shape[2]
    gw = NSA_GROUP * NSA_DK
    halves = PAGE_SIZE // SEL_BLOCK

    def kvspec(which):
        def index(b, h, t, pt, ix):
            blk = ix[b, h, t]
            return (pt[b, jnp.minimum(blk // halves, n_pages - 1)], blk % halves, which * NSA_KV_HEADS + h)
        return pl.BlockSpec((1, SEL_BLOCK, NSA_DK), index)

    def newspec(which):
        return pl.BlockSpec((1, 1, NSA_DK),
                            lambda b, h, t, pt, ix: (b, 0, COL_KVS // NSA_DK + which * NSA_KV_HEADS + h))

    return pl.pallas_call(
        functools.partial(_nsa_sample_sel_kernel, ns=ns),
        grid_spec=pltpu.PrefetchScalarGridSpec(
            num_scalar_prefetch=2,
            grid=(nb, NSA_KV_HEADS, n_sel),
            in_specs=[pl.BlockSpec((1, 1, gw), lambda b, h, t, pt, ix: (b, 0, COL_NQ // gw + h)),
                      kvspec(0), kvspec(1), newspec(0), newspec(1),
                      pl.BlockSpec((1, 1, SROWS, SEL_BLOCK), lambda b, h, t, pt, ix: (h, ix[b, h, t], 0, 0))],
            out_specs=pl.BlockSpec((1, 1, SROWS, NSA_DK), lambda b, h, t, pt, ix: (b, h, 0, 0)),
            scratch_shapes=[pltpu.VMEM((SROWS, 1), F32), pltpu.VMEM((SROWS, 1), F32),
                            pltpu.VMEM((SROWS, NSA_DK), F32)]),
        out_shape=jax.ShapeDtypeStruct((nb, NSA_KV_HEADS, SROWS, NSA_DK), F32),
        compiler_params=_cparams(3),
        name="nsa_sample_sel",
    )(page_table, idx, z3, cache, cache, z3, z3, bias_sel)


def _nsa_sample_win_kernel(q_ref, k_ref, v_ref, kn_ref, vn_ref, bias_ref, bnew_ref, o_ref):
    scale = NSA_DK ** -0.5
    q = _stack_group_q(q_ref[0])
    s_buf = _nt(q.astype(BF16), k_ref[0].astype(BF16)) * scale + bias_ref[0]
    s_new = jnp.sum(q * kn_ref[0], axis=1, keepdims=True) * scale + bnew_ref[0]
    m = jnp.maximum(jnp.max(s_buf, axis=1, keepdims=True), s_new)
    p_buf = jnp.exp(s_buf - m)
    p_new = jnp.exp(s_new - m)
    l = jnp.sum(p_buf, axis=1, keepdims=True) + p_new
    acc = _dot(p_buf.astype(BF16), v_ref[0].astype(BF16)) + p_new * vn_ref[0]
    o_ref[0, 0] = acc / jnp.maximum(l, 1e-30)


def _nsa_sample_win(z3, win_buf, bias_win, bias_new):
    nb, nbuf, _ = win_buf.shape
    gw = NSA_GROUP * NSA_DK

    def bufspec(which):
        return pl.BlockSpec((1, nbuf, NSA_DK), lambda b, h: (b, 0, which * NSA_KV_HEADS + h))

    def newspec(which):
        return pl.BlockSpec((1, 1, NSA_DK), lambda b, h: (b, 0, COL_KVW // NSA_DK + which * NSA_KV_HEADS + h))

    return pl.pallas_call(
        _nsa_sample_win_kernel,
        grid=(nb, NSA_KV_HEADS),
        in_specs=[pl.BlockSpec((1, 1, gw), lambda b, h: (b, 0, COL_NQ // gw + h)),
                  bufspec(0), bufspec(1), newspec(0), newspec(1),
                  pl.BlockSpec((1, SROWS, nbuf), lambda b, h: (h, 0, 0)),
                  pl.BlockSpec((1, SROWS, 1), lambda b, h: (h, 0, 0))],
        out_specs=pl.BlockSpec((1, 1, SROWS, NSA_DK), lambda b, h: (b, h, 0, 0)),
        out_shape=jax.ShapeDtypeStruct((nb, NSA_KV_HEADS, SROWS, NSA_DK), F32),
        compiler_params=_cparams(2),
        name="nsa_sample_win",
    )(z3, win_buf, win_buf, z3, z3, bias_win, bias_new)


def _nsa_sample_gate_kernel(oc_ref, os_ref, ow_ref, ng_ref, nsl_ref, o_ref):
    gates = _sigmoid(ng_ref[0])
    nsl = nsl_ref[0]
    outs = []
    for g in range(NSA_GROUP):
        o = (gates[:, 3 * g:3 * g + 1] * oc_ref[0, 0, g:g + 1] + gates[:, 3 * g + 1:3 * g + 2] * os_ref[0, 0, g:g + 1]
             + gates[:, 3 * g + 2:3 * g + 3] * ow_ref[0, 0, g:g + 1])
        x = nsl[:, g * NSA_DK:(g + 1) * NSA_DK]
        outs.append(o * (x * _sigmoid(x)))
    o_ref[0] = jnp.concatenate(outs, axis=1).astype(BF16)


def _nsa_sample_gate(o_cmp, o_sel, o_win, z3):
    nb = z3.shape[0]
    gw = NSA_GROUP * NSA_DK
    ospec = pl.BlockSpec((1, 1, SROWS, NSA_DK), lambda b, h: (b, h, 0, 0))
    return pl.pallas_call(
        _nsa_sample_gate_kernel,
        grid=(nb, NSA_KV_HEADS),
        in_specs=[ospec, ospec, ospec,
                  pl.BlockSpec((1, 1, LANE), lambda b, h: (b, 0, COL_NG // LANE + h)),
                  pl.BlockSpec((1, 1, gw), lambda b, h: (b, 0, COL_NSL // gw + h))],
        out_specs=pl.BlockSpec((1, 1, gw), lambda b, h: (b, 0, h)),
        out_shape=jax.ShapeDtypeStruct((nb, 1, NSA_W), BF16),
        compiler_params=_cparams(2),
        name="nsa_sample_gate",
    )(o_cmp, o_sel, o_win, z3, z3)


def _mem_heads(q, kv):
    outs = []
    for h in range(MEM_HEADS):
        k = kv[:, h * MEM_DH:(h + 1) * MEM_DH].astype(BF16)
        v = kv[:, MEM_W + h * MEM_DH:MEM_W + (h + 1) * MEM_DH].astype(BF16)
        s = _nt(q[:, h * MEM_DH:(h + 1) * MEM_DH].astype(BF16), k) * (MEM_DH ** -0.5)
        e = jnp.exp(s - jnp.max(s, axis=1, keepdims=True))
        p = e / jnp.sum(e, axis=1, keepdims=True)
        outs.append(_dot(p.astype(BF16), v))
    return jnp.concatenate(outs, axis=1)


def _mem_prompt_kernel(q_ref, kv_ref, o_ref):
    o_ref[...] = _mem_heads(q_ref[...], kv_ref[...]).astype(BF16)


def _mem_prompt(z, mem_kv, batch, seq, tq):
    nq = seq // tq
    n_mem = mem_kv.shape[0] // batch
    return pl.pallas_call(
        _mem_prompt_kernel,
        grid=(batch, nq),
        in_specs=[pl.BlockSpec((tq, MEM_W), lambda b, i: (b * nq + i, COL_MQ // MEM_W)),
                  pl.BlockSpec((n_mem, 2 * MEM_W), lambda b, i: (b, 0))],
        out_specs=pl.BlockSpec((tq, MEM_W), lambda b, i: (b * nq + i, 0)),
        out_shape=jax.ShapeDtypeStruct((batch * seq, MEM_W), BF16),
        compiler_params=_cparams(2),
        name="mem_prompt",
    )(z, mem_kv)


def _mem_sample_kernel(q_ref, kv_ref, o_ref):
    q = jnp.broadcast_to(q_ref[0], (SROWS, MEM_W))
    o_ref[0] = _mem_heads(q, kv_ref[0])[0:1].astype(BF16)


def _mem_sample(z3, mem_kv):
    nb, n_mem, _ = mem_kv.shape
    return pl.pallas_call(
        _mem_sample_kernel,
        grid=(nb,),
        in_specs=[pl.BlockSpec((1, 1, MEM_W), lambda b: (b, 0, COL_MQ // MEM_W)),
                  pl.BlockSpec((1, n_mem, 2 * MEM_W), lambda b: (b, 0, 0))],
        out_specs=pl.BlockSpec((1, 1, MEM_W), lambda b: (b, 0, 0)),
        out_shape=jax.ShapeDtypeStruct((nb, 1, MEM_W), BF16),
        compiler_params=_cparams(1),
        name="mem_sample",
    )(z3, mem_kv)


def _merge_kernel(ar_ref, an_ref, am_ref, wr_ref, wn_ref, wm_ref, g0_ref, g1_ref, g2_ref, o_ref):
    merged = (_sigmoid(g0_ref[...]) * _dot(ar_ref[...], wr_ref[...])
              + _sigmoid(g1_ref[...]) * _dot(an_ref[...], wn_ref[...])
              + _sigmoid(g2_ref[...]) * _dot(am_ref[...], wm_ref[...]))
    o_ref[...] = merged.astype(BF16)


def _merge(a_ret, a_nsa, a_mem, w_ret, w_nsa, w_mem, z, tm, tn):
    m = a_ret.shape[0]
    nt = D_MODEL // tn

    def aspec(width):
        return pl.BlockSpec((tm, width), lambda i, j: (i, 0))

    def wspec(width):
        return pl.BlockSpec((width, tn), lambda i, j: (0, j))

    def gspec(branch):
        return pl.BlockSpec((tm, tn), lambda i, j: (i, COL_MG // tn + branch * nt + j))

    return pl.pallas_call(
        _merge_kernel,
        grid=(m // tm, nt),
        in_specs=[aspec(RET_W), aspec(NSA_W), aspec(MEM_W), wspec(RET_W), wspec(NSA_W), wspec(MEM_W),
                  gspec(0), gspec(1), gspec(2)],
        out_specs=pl.BlockSpec((tm, tn), lambda i, j: (i, j)),
        out_shape=jax.ShapeDtypeStruct((m, D_MODEL), BF16),
        compiler_params=_cparams(2),
        name="merge",
    )(a_ret, a_nsa, a_mem, w_ret, w_nsa, w_mem, z, z, z)


def _out_kernel(a_ref, w_ref, x_ref, g_ref, o_ref):
    out = _dot(a_ref[...], w_ref[...])
    y = out * lax.rsqrt(jnp.mean(out * out, axis=-1, keepdims=True) + EPS)
    o_ref[...] = x_ref[...] + y * g_ref[...]


def _out_proj(merged, w_out, x, norm_post, tm):
    m = merged.shape[0]
    return pl.pallas_call(
        _out_kernel,
        grid=(m // tm,),
        in_specs=[pl.BlockSpec((tm, D_MODEL), lambda i: (i, 0)),
                  pl.BlockSpec((D_MODEL, D_MODEL), lambda i: (0, 0)),
                  pl.BlockSpec((tm, D_MODEL), lambda i: (i, 0)),
                  pl.BlockSpec((1, D_MODEL), lambda i: (0, 0))],
        out_specs=pl.BlockSpec((tm, D_MODEL), lambda i: (i, 0)),
        out_shape=jax.ShapeDtypeStruct((m, D_MODEL), F32),
        compiler_params=_cparams(1),
        name="out_proj",
    )(merged, w_out, x, norm_post.reshape(1, D_MODEL))


def _layout_w_in(w_in):
    splits = (RET_HEADS * RET_DK, RET_HEADS * RET_DK, RET_W, RET_W, NSA_W, 2 * KV_W, 2 * KV_W, 2 * KV_W,
              N_BRANCHES * NSA_HEADS, NSA_W, MEM_W, N_BRANCHES * D_MODEL)
    pieces, start = [], 0
    for w in splits:
        pieces.append(w_in[:, start:start + w])
        start += w
    rq, rk, rv, rg, nq, kvc, kvs, kvw, ng, nsl, mq, mg = pieces
    per_group = N_BRANCHES * NSA_GROUP
    ng = ng.reshape(D_MODEL, NSA_KV_HEADS, per_group)
    ng = jnp.pad(ng, ((0, 0), (0, 0), (0, LANE - per_group))).reshape(D_MODEL, NG_SLOT)
    return jnp.concatenate([rq, rk, rv, rg, nq, kvc, kvs, kvw, nsl, mq, mg, ng], axis=1).astype(BF16)


def _pick_tile(m, cap):
    t = min(m, cap)
    while m % t:
        t //= 2
    return t


def kernel(x_prompt, x_sample, cache_cmp_kv, cache_sel_kv, cache_win_kv, state_ret, cache_mem_kv, page_table,
           mem_prompt, rel_table, norm_pre, norm_post, norm_mem, w_in, ret_norm, w_ret_up, cmp_pos, w_cmp1,
           w_cmp2, w_nsa_up, w_mem_kv, w_mem_up, w_out):
    batch, seq, _ = x_prompt.shape
    nb = x_sample.shape[0]
    assert x_sample.shape[1] == 1 and norm_pre.shape[0] == 1
    assert seq % TQ == 0 and seq >= WINDOW
    n_pool = cache_cmp_kv.shape[1]
    n_pages = page_table.shape[1]
    past = n_pages * PAGE_SIZE
    n_mem = mem_prompt.shape[1]
    assert n_pages % PAGES_PER_STEP == 0 and cache_win_kv.shape[2] == WINDOW

    w_proj = _layout_w_in(w_in[0])
    kw = CMP_STRIDE * NSA_DK
    w1 = w_cmp1[0].reshape(2, CMP_BLOCK * NSA_DK, NSA_DK).astype(BF16)
    w1a, w1b = w1[:, :kw], w1[:, kw:]
    w2 = w_cmp2[0].astype(BF16)
    pos8 = jnp.pad(cmp_pos[0].reshape(2, 1, CMP_BLOCK * NSA_DK), ((0, 0), (0, 7), (0, 0)))
    w_ret = w_ret_up[0].astype(BF16)
    w_nsa = w_nsa_up[0].astype(BF16)
    w_mem = w_mem_up[0].astype(BF16)
    w_o = w_out[0].astype(BF16)
    w_mkv = w_mem_kv[0].astype(BF16)

    m_p = batch * seq
    xp = x_prompt.reshape(m_p, D_MODEL)
    z = _norm_matmul(xp, norm_pre[0], w_proj, _pick_tile(m_p, 1024), 512)

    a_ret, ret_state_p = _retention_prompt(z, ret_norm[0], batch, seq)

    ncp = max(LANE, -(-(seq // CMP_STRIDE) // LANE) * LANE)
    ac = _cmp_stage1_dense(z, w1a, w1b, batch, seq)
    kcvc = _cmp_stage2(ac, pos8, w1a, w1b, w2, 1)
    if ncp > kcvc.shape[3]:
        kcvc = jnp.pad(kcvc, ((0, 0), (0, 0), (0, 0), (0, ncp - kcvc.shape[3]), (0, 0)))
    bias_d = _bias_by_dist(rel_table, BIAS_DISTS)
    a_nsa = _nsa_prompt(z, kcvc, bias_d, batch, seq)

    mem_kv_p = _norm_matmul(mem_prompt.reshape(batch * n_mem, D_MODEL), norm_mem[0], w_mkv,
                            _pick_tile(batch * n_mem, 512), 512)
    a_mem = _mem_prompt(z, mem_kv_p, batch, seq, _pick_tile(seq, 512))

    merged = _merge(a_ret, a_nsa, a_mem, w_ret, w_nsa, w_mem, z, _pick_tile(m_p, 512), 512)
    y_p = _out_proj(merged, w_o, xp, norm_post[0], _pick_tile(m_p, 256)).reshape(batch, seq, D_MODEL)

    kv_shape = (1, batch, seq, 2, NSA_KV_HEADS, NSA_DK)
    new_cmp_p = z[:, COL_KVC:COL_KVC + 2 * KV_W].reshape(kv_shape)
    new_sel_p = z[:, COL_KVS:COL_KVS + 2 * KV_W].reshape(kv_shape)
    new_win_p = z[:, COL_KVW:COL_KVW + 2 * KV_W].reshape(kv_shape)[:, :, seq - WINDOW:]
    new_ret_p = ret_state_p[None]
    new_mem_p = mem_kv_p.reshape(1, batch, n_mem, 2, MEM_HEADS, MEM_DH)

    xs = x_sample.reshape(nb, D_MODEL)
    zs = _norm_matmul(xs, norm_pre[0], w_proj, nb, 512)
    z3 = zs.reshape(nb, 1, PROJ_W)

    a_ret_s, ret_state_s = _retention_sample(z3, state_ret[0], ret_norm[0], past)

    cache_c = cache_cmp_kv[0].reshape(n_pool, PAGE_SIZE // CMP_STRIDE, CMP_STRIDE, 2 * NSA_KV_HEADS, NSA_DK)
    cache_s = cache_sel_kv[0].reshape(n_pool, PAGE_SIZE, 2 * KV_W)
    ac_s = _cmp_stage1_paged(cache_c, page_table, w1a, w1b)
    kcvc_s = _cmp_stage2(ac_s, pos8, w1a, w1b, w2, NSA_KV_HEADS)
    ncs = past // CMP_STRIDE
    kcvc_s = kcvc_s.reshape(nb, 2, ncs, KV_W)
    ns_s = past // SEL_BLOCK + 1
    nsp_s = -(-ns_s // LANE) * LANE
    assert past >= WINDOW and past >= REL_MAX_DIST and BIAS_DISTS > WINDOW
    far_s = bias_d[..., REL_MAX_DIST:REL_MAX_DIST + 1]
    dist_cs = past - (CMP_STRIDE * jnp.arange(ncs) + CMP_BLOCK - 1)
    bias_cs = jnp.where(dist_cs >= 0, jnp.take(bias_d, jnp.clip(dist_cs, 0, BIAS_DISTS - 1), axis=-1), NEG_INF)
    bias_cs = _pad_group_rows(bias_cs, 1)
    nblk = jnp.arange(ncs)[:, None]
    sblk = jnp.arange(nsp_s)[None, :]
    ov_s = ((nblk >= 4 * sblk - 1) & (nblk <= 4 * sblk + 3)).astype(BF16)
    o_cmp_s, idx_s = _nsa_sample_cmp(z3, kcvc_s, bias_cs, ov_s, ns_s)
    n_sel = min(SEL_TOPK, ns_s)
    idx = idx_s[:, :, 0, :n_sel]

    hg = (NSA_KV_HEADS, NSA_GROUP)
    n_key = ns_s * SEL_BLOCK
    bias_sel = jnp.concatenate([jnp.broadcast_to(far_s, hg + (past + 1 - REL_MAX_DIST,)),
                                bias_d[..., :REL_MAX_DIST][..., ::-1],
                                jnp.full(hg + (n_key - past - 1,), NEG_INF, F32)], axis=-1)
    bias_sel = _pad_group_rows(bias_sel.reshape(hg + (ns_s, SEL_BLOCK)).transpose(0, 2, 1, 3), 2)
    o_sel_s = _nsa_sample_sel(z3, cache_s, page_table, idx, bias_sel, ns_s)

    win_buf = cache_win_kv[0].reshape(nb, WINDOW, 2 * KV_W)
    bias_w = jnp.concatenate([jnp.full(hg + (1,), NEG_INF, F32), bias_d[..., 1:WINDOW][..., ::-1]], axis=-1)
    o_win_s = _nsa_sample_win(z3, win_buf, _pad_group_rows(bias_w, 1), _pad_group_rows(bias_d[..., 0:1], 1))
    a_nsa_s = _nsa_sample_gate(o_cmp_s, o_sel_s, o_win_s, z3)

    mem_kv_s = cache_mem_kv[0].reshape(nb, n_mem, 2 * MEM_W)
    a_mem_s = _mem_sample(z3, mem_kv_s)

    merged_s = _merge(a_ret_s.reshape(nb, RET_W), a_nsa_s.reshape(nb, NSA_W), a_mem_s.reshape(nb, MEM_W),
                      w_ret, w_nsa, w_mem, zs, nb, 512)
    y_s = _out_proj(merged_s, w_o, xs, norm_post[0], nb).reshape(nb, 1, D_MODEL)

    kvs_shape = (1, nb, 1, 2, NSA_KV_HEADS, NSA_DK)
    new_cmp_s = zs[:, COL_KVC:COL_KVC + 2 * KV_W].reshape(kvs_shape)
    new_sel_s = zs[:, COL_KVS:COL_KVS + 2 * KV_W].reshape(kvs_shape)
    kvw_s = zs[:, COL_KVW:COL_KVW + 2 * KV_W].reshape(nb, 1, 2, NSA_KV_HEADS, NSA_DK)
    new_win_s = jnp.concatenate([cache_win_kv[0][:, 1:], kvw_s], axis=1)[None]
    new_ret_s = ret_state_s[None]

    return (y_p, y_s, new_cmp_p, new_sel_p, new_win_p, new_ret_p, new_mem_p,
            new_cmp_s, new_sel_s, new_win_s, new_ret_s)
```

```python
import functools
import math

import jax
import jax.numpy as jnp
from jax import lax
from jax.experimental import pallas as pl
from jax.experimental.pallas import tpu as pltpu

F32 = jnp.float32
BF16 = jnp.bfloat16

D_MODEL = 2048
PAGE_SIZE = 128
RET_HEADS = 8
RET_DK = 256
RET_DV = 256
RET_CHUNK = 128
ROPE_BASE = 10000.0
NSA_HEADS = 16
NSA_KV_HEADS = 4
NSA_GROUP = NSA_HEADS // NSA_KV_HEADS
NSA_DK = 128
CMP_BLOCK = 32
CMP_STRIDE = 16
SEL_BLOCK = 64
SEL_TOPK = 16
WINDOW = 512
MEM_HEADS = 4
MEM_DH = 384
REL_BUCKETS = 32
REL_MAX_EXACT = 16
REL_MAX_DIST = 128
N_BRANCHES = 3
EPS = 1e-6
NEG_INF = -1e30
FORCE_SCORE = 1e4

RET_W = RET_HEADS * RET_DV
NSA_W = NSA_HEADS * NSA_DK
KV_W = NSA_KV_HEADS * NSA_DK
MEM_W = MEM_HEADS * MEM_DH

COL_RQ = 0
COL_RK = COL_RQ + RET_HEADS * RET_DK
COL_RV = COL_RK + RET_HEADS * RET_DK
COL_RG = COL_RV + RET_W
COL_NQ = COL_RG + RET_W
COL_KVC = COL_NQ + NSA_W
COL_KVS = COL_KVC + 2 * KV_W
COL_KVW = COL_KVS + 2 * KV_W
COL_NSL = COL_KVW + 2 * KV_W
COL_MQ = COL_NSL + NSA_W
COL_MG = COL_MQ + MEM_W
COL_NG = COL_MG + N_BRANCHES * D_MODEL
NG_SLOT = NSA_KV_HEADS * 128
PROJ_W = COL_NG + NG_SLOT

LANE = 128
TQ = 256
TK = 256
NSA_ROWS = NSA_GROUP * TQ
BIAS_DISTS = 1024
RET_STEP_CHUNKS = 8
VMEM_LIMIT = 48 * 1024 * 1024


def _cparams(n_axes):
    return pltpu.CompilerParams(dimension_semantics=("arbitrary",) * n_axes, vmem_limit_bytes=VMEM_LIMIT)


def _nt(a, b):
    return lax.dot_general(a, b, (((1,), (1,)), ((), ())), preferred_element_type=F32)


def _dot(a, b):
    return jnp.dot(a, b, preferred_element_type=F32)


def _sigmoid(x):
    return 1.0 / (1.0 + jnp.exp(-x))


def _iota(shape, dim):
    return lax.broadcasted_iota(jnp.int32, shape, dim)


def _norm_matmul_kernel(x_ref, g_ref, w_ref, o_ref, h_ref):
    @pl.when(pl.program_id(1) == 0)
    def _():
        x = x_ref[...]
        ms = jnp.mean(x * x, axis=-1, keepdims=True)
        h_ref[...] = ((x * lax.rsqrt(ms + EPS)) * g_ref[...]).astype(BF16)

    o_ref[...] = _dot(h_ref[...], w_ref[...])


def _norm_matmul(x, g, w, tm, tn):
    m, k = x.shape
    n = w.shape[1]
    return pl.pallas_call(
        _norm_matmul_kernel,
        grid=(m // tm, n // tn),
        in_specs=[pl.BlockSpec((tm, k), lambda i, j: (i, 0)),
                  pl.BlockSpec((1, k), lambda i, j: (0, 0)),
                  pl.BlockSpec((k, tn), lambda i, j: (0, j))],
        out_specs=pl.BlockSpec((tm, tn), lambda i, j: (i, j)),
        out_shape=jax.ShapeDtypeStruct((m, n), F32),
        scratch_shapes=[pltpu.VMEM((tm, k), BF16)],
        compiler_params=_cparams(2),
        name="norm_matmul",
    )(x, g.reshape(1, k), w)


def _rope_rows(x, cos, sin):
    half = x.shape[-1] // 2
    x1, x2 = x[:, :half], x[:, half:]
    return jnp.concatenate([x1 * cos - x2 * sin, x1 * sin + x2 * cos], axis=-1)


def _head_norm_gate(o, gnorm, rg):
    oc = o - jnp.mean(o, axis=-1, keepdims=True)
    y = oc * lax.rsqrt(jnp.mean(oc * oc, axis=-1, keepdims=True) + EPS) * gnorm
    return y * (rg * _sigmoid(rg))


def _ret_prompt_kernel(q_ref, k_ref, v_ref, rg_ref, cos_ref, sin_ref, dmat_ref, xi_ref, zeta_ref, gc_ref,
                       gn_ref, a_ref, s_ref):
    @pl.when(pl.program_id(2) == 0)
    def _():
        s_ref[...] = jnp.zeros_like(s_ref)

    c = RET_CHUNK
    for t in range(q_ref.shape[0] // c):
        rows = slice(t * c, (t + 1) * c)
        cos, sin = cos_ref[rows, :], sin_ref[rows, :]
        q = _rope_rows(q_ref[rows, :], cos, sin)
        k = _rope_rows(k_ref[rows, :], cos, sin) * (RET_DK ** -0.5)
        qb, vb = q.astype(BF16), v_ref[rows, :].astype(BF16)
        state = s_ref[0, 0]
        inner = _nt(qb, k.astype(BF16)) * dmat_ref[0]
        o = _dot(inner.astype(BF16), vb) + _dot(qb, state.astype(BF16)) * xi_ref[0]
        kz_t = (k * zeta_ref[0]).T.astype(BF16)
        s_ref[0, 0] = state * gc_ref[0] + _dot(kz_t, vb)
        a_ref[rows, :] = _head_norm_gate(o, gn_ref[...], rg_ref[rows, :]).astype(BF16)


def _decay_tables(chunk):
    log_g = jnp.log1p(-jnp.exp2(-5.0 - jnp.arange(RET_HEADS, dtype=F32)))
    i = jnp.arange(chunk, dtype=F32)
    diff = i[:, None] - i[None, :]
    dmat = jnp.where(diff >= 0, jnp.exp(log_g[:, None, None] * jnp.maximum(diff, 0.0)), 0.0)
    xi = jnp.exp(log_g[:, None] * (i[None, :] + 1.0))[:, :, None]
    zeta = jnp.exp(log_g[:, None] * (chunk - 1.0 - i[None, :]))[:, :, None]
    g_chunk = jnp.exp(log_g * chunk)[:, None, None]
    return dmat, xi, zeta, g_chunk


def _rope_tables(pos):
    half = RET_DK // 2
    freq = jnp.power(ROPE_BASE, -jnp.arange(half, dtype=F32) / half)
    ang = pos.astype(F32)[:, None] * freq[None, :]
    return jnp.cos(ang), jnp.sin(ang)


def _retention_prompt(z, ret_norm, batch, seq):
    c = RET_CHUNK
    rows = _pick_tile(seq, RET_STEP_CHUNKS * c)
    nc = seq // rows
    dmat, xi, zeta, g_chunk = _decay_tables(c)
    cos, sin = _rope_tables(jnp.arange(seq))
    hb = RET_DK

    def zspec(col0):
        return pl.BlockSpec((rows, hb), lambda b, h, t, col0=col0: (b * nc + t, col0 // hb + h))

    per_head = lambda shape: pl.BlockSpec((1,) + shape, lambda b, h, t: (h, 0, 0))
    return pl.pallas_call(
        _ret_prompt_kernel,
        grid=(batch, RET_HEADS, nc),
        in_specs=[zspec(COL_RQ), zspec(COL_RK), zspec(COL_RV), zspec(COL_RG),
                  pl.BlockSpec((rows, hb // 2), lambda b, h, t: (t, 0)),
                  pl.BlockSpec((rows, hb // 2), lambda b, h, t: (t, 0)),
                  per_head((c, c)), per_head((c, 1)), per_head((c, 1)), per_head((1, 1)),
                  pl.BlockSpec((1, hb), lambda b, h, t: (0, h))],
        out_specs=[pl.BlockSpec((rows, hb), lambda b, h, t: (b * nc + t, h)),
                   pl.BlockSpec((1, 1, RET_DK, RET_DV), lambda b, h, t: (b, h, 0, 0))],
        out_shape=[jax.ShapeDtypeStruct((batch * seq, RET_W), BF16),
                   jax.ShapeDtypeStruct((batch, RET_HEADS, RET_DK, RET_DV), F32)],
        compiler_params=_cparams(3),
        name="retention_prompt",
    )(z, z, z, z, cos, sin, dmat, xi, zeta, g_chunk, ret_norm.reshape(1, RET_W))


def _column_of(row):
    n = row.shape[1]
    eye = _iota((n, n), 0) == _iota((n, n), 1)
    return jnp.sum(jnp.where(eye, jnp.broadcast_to(row, (n, n)), 0.0), axis=1, keepdims=True)


def _ret_sample_kernel(q_ref, k_ref, v_ref, rg_ref, cos_ref, sin_ref, gam_ref, gn_ref, s_ref, a_ref, so_ref):
    cos, sin = cos_ref[...], sin_ref[...]
    q = _rope_rows(q_ref[0], cos, sin)
    k = _rope_rows(k_ref[0], cos, sin) * (RET_DK ** -0.5)
    v = v_ref[0]
    state = s_ref[0, 0]
    gamma = gam_ref[0]
    qk = jnp.sum(q * k, axis=-1, keepdims=True)
    o = qk * v + jnp.sum(_column_of(q) * state, axis=0, keepdims=True) * gamma
    so_ref[0, 0] = state * gamma + _column_of(k) * v
    a_ref[0] = _head_norm_gate(o, gn_ref[...], rg_ref[0]).astype(BF16)


def _retention_sample(z3, state, ret_norm, pos):
    nb = z3.shape[0]
    cos, sin = _rope_tables(jnp.full((1,), pos))
    gamma = jnp.exp(jnp.log1p(-jnp.exp2(-5.0 - jnp.arange(RET_HEADS, dtype=F32))))[:, None, None]
    hb = RET_DK

    def zspec(col0):
        return pl.BlockSpec((1, 1, hb), lambda b, h, col0=col0: (b, 0, col0 // hb + h))

    st_spec = pl.BlockSpec((1, 1, RET_DK, RET_DV), lambda b, h: (b, h, 0, 0))
    return pl.pallas_call(
        _ret_sample_kernel,
        grid=(nb, RET_HEADS),
        in_specs=[zspec(COL_RQ), zspec(COL_RK), zspec(COL_RV), zspec(COL_RG),
                  pl.BlockSpec((1, hb // 2), lambda b, h: (0, 0)),
                  pl.BlockSpec((1, hb // 2), lambda b, h: (0, 0)),
                  pl.BlockSpec((1, 1, 1), lambda b, h: (h, 0, 0)),
                  pl.BlockSpec((1, hb), lambda b, h: (0, h)),
                  st_spec],
        out_specs=[pl.BlockSpec((1, 1, hb), lambda b, h: (b, 0, h)), st_spec],
        out_shape=[jax.ShapeDtypeStruct((nb, 1, RET_W), BF16),
                   jax.ShapeDtypeStruct(state.shape, F32)],
        compiler_params=_cparams(2),
        name="retention_sample",
    )(z3, z3, z3, z3, cos, sin, gamma, ret_norm.reshape(1, RET_W), state)


def _half_rows(ref_slice_fn, n_half):
    return jnp.concatenate([ref_slice_fn(p) for p in range(CMP_STRIDE)], axis=1)


def _cmp_stage1_dense_kernel(x_ref, w_ref, o_ref):
    nh = o_ref.shape[3]
    x = _half_rows(lambda p: x_ref[pl.ds(p, nh, stride=CMP_STRIDE), :], nh).astype(BF16)
    o_ref[0, 0, 0] = _dot(x, w_ref[0])


def _cmp_stage1_dense(z, w1ab, batch, seq):
    nh = seq // CMP_STRIDE
    return pl.pallas_call(
        _cmp_stage1_dense_kernel,
        grid=(batch, 2, NSA_KV_HEADS),
        in_specs=[pl.BlockSpec((seq, NSA_DK), lambda b, kv, h: (b, COL_KVC // NSA_DK + kv * NSA_KV_HEADS + h)),
                  pl.BlockSpec((1, CMP_STRIDE * NSA_DK, 2 * NSA_DK), lambda b, kv, h: (kv, 0, 0))],
        out_specs=pl.BlockSpec((1, 1, 1, nh, 2 * NSA_DK), lambda b, kv, h: (b, kv, h, 0, 0)),
        out_shape=jax.ShapeDtypeStruct((batch, 2, NSA_KV_HEADS, nh, 2 * NSA_DK), F32),
        compiler_params=_cparams(3),
        name="cmp_stage1_dense",
    )(z, w1ab)


PAGES_PER_STEP = 8


def _cmp_stage1_paged_kernel(pt_ref, *refs):
    pages, (w_ref, o_ref) = refs[:PAGES_PER_STEP], refs[PAGES_PER_STEP:]
    hp = PAGE_SIZE // CMP_STRIDE
    top = _iota((2 * NSA_KV_HEADS, NSA_DK), 0) < NSA_KV_HEADS
    cols = [[], []]
    for p in range(CMP_STRIDE):
        tiles = [[], []]
        for pg in pages:
            xp = pg[0, :, p]
            for n in range(0, hp, 2):
                a, b = xp[n], xp[n + 1]
                tiles[0].append(jnp.where(top, a, pltpu.roll(b, NSA_KV_HEADS, 0)))
                tiles[1].append(jnp.where(top, pltpu.roll(a, NSA_KV_HEADS, 0), b))
        for kv in range(2):
            cols[kv].append(jnp.concatenate(tiles[kv], axis=0))
    for kv in range(2):
        x = jnp.concatenate(cols[kv], axis=1).astype(BF16)
        o_ref[0, kv, 0] = _dot(x, w_ref[kv])


def _cmp_stage1_paged(cache, page_table, w1ab):
    nb, n_pages = page_table.shape
    hp = PAGE_SIZE // CMP_STRIDE
    steps = n_pages // PAGES_PER_STEP
    rows = PAGES_PER_STEP * hp * NSA_KV_HEADS

    def page_spec(j):
        return pl.BlockSpec((1, hp, CMP_STRIDE, 2 * NSA_KV_HEADS, NSA_DK),
                            lambda b, s, pt, j=j: (pt[b, s * PAGES_PER_STEP + j], 0, 0, 0, 0))

    wspec = pl.BlockSpec((2, CMP_STRIDE * NSA_DK, 2 * NSA_DK), lambda b, s, pt: (0, 0, 0))
    return pl.pallas_call(
        _cmp_stage1_paged_kernel,
        grid_spec=pltpu.PrefetchScalarGridSpec(
            num_scalar_prefetch=1,
            grid=(nb, steps),
            in_specs=[page_spec(j) for j in range(PAGES_PER_STEP)] + [wspec],
            out_specs=pl.BlockSpec((1, 2, 1, rows, 2 * NSA_DK), lambda b, s, pt: (b, 0, 0, s, 0))),
        out_shape=jax.ShapeDtypeStruct((nb, 2, 1, steps * rows, 2 * NSA_DK), F32),
        compiler_params=_cparams(2),
        name="cmp_stage1_paged",
    )(page_table, *([cache] * PAGES_PER_STEP), w1ab)


def _cmp_stage2_kernel(ac_ref, pos_ref, w1a_ref, w1b_ref, w2_ref, o_ref, *, shift):
    ac = ac_ref[0, 0, 0]
    nh = ac.shape[0]
    pos = pos_ref[0].astype(BF16)
    kw = CMP_STRIDE * NSA_DK
    pe = _dot(pos[:, :kw], w1a_ref[0]) + _dot(pos[:, kw:], w1b_ref[0])
    pre = ac[:, :NSA_DK] + pltpu.roll(ac[:, NSA_DK:], nh - shift, 0) + pe[0:1]
    gelu = 0.5 * pre * (1.0 + jnp.tanh(math.sqrt(2.0 / math.pi) * (pre + 0.044715 * (pre * pre * pre))))
    o_ref[0, 0, 0] = _dot(gelu.astype(BF16), w2_ref[0]).astype(BF16)


def _cmp_stage2(ac, pos8, w1a, w1b, w2, shift):
    nb, _, groups, nh, _ = ac.shape
    kw = CMP_STRIDE * NSA_DK
    return pl.pallas_call(
        functools.partial(_cmp_stage2_kernel, shift=shift),
        grid=(nb, 2, groups),
        in_specs=[pl.BlockSpec((1, 1, 1, nh, 2 * NSA_DK), lambda b, kv, h: (b, kv, h, 0, 0)),
                  pl.BlockSpec((1, 8, 2 * kw), lambda b, kv, h: (kv, 0, 0)),
                  pl.BlockSpec((1, kw, NSA_DK), lambda b, kv, h: (kv, 0, 0)),
                  pl.BlockSpec((1, kw, NSA_DK), lambda b, kv, h: (kv, 0, 0)),
                  pl.BlockSpec((1, NSA_DK, NSA_DK), lambda b, kv, h: (kv, 0, 0))],
        out_specs=pl.BlockSpec((1, 1, 1, nh, NSA_DK), lambda b, kv, h: (b, kv, h, 0, 0)),
        out_shape=jax.ShapeDtypeStruct((nb, 2, groups, nh, NSA_DK), BF16),
        compiler_params=_cparams(3),
        name="cmp_stage2",
    )(ac, pos8, w1a, w1b, w2)


def _rel_bucket(dist):
    n = jnp.maximum(dist, 0)
    nf = jnp.maximum(n, 1).astype(F32)
    scale = (REL_BUCKETS - REL_MAX_EXACT) / math.log(REL_MAX_DIST / REL_MAX_EXACT)
    large = REL_MAX_EXACT + (jnp.log(nf / REL_MAX_EXACT) * scale).astype(jnp.int32)
    large = jnp.minimum(large, REL_BUCKETS - 1)
    return jnp.where(n < REL_MAX_EXACT, n, large)


def _bias_by_dist(rel_table, n):
    tab = rel_table[_rel_bucket(jnp.arange(n))]
    return tab.T.reshape(NSA_KV_HEADS, NSA_GROUP, n)


def _pad_group_rows(t, axis):
    first = lax.slice_in_dim(t, 0, 1, axis=axis)
    return jnp.concatenate([t] + [first] * (SROWS - NSA_GROUP), axis=axis)


def _flash_init(m_ref, l_ref, acc_ref):
    m_ref[...] = jnp.full(m_ref.shape, NEG_INF, F32)
    l_ref[...] = jnp.zeros(l_ref.shape, F32)
    acc_ref[...] = jnp.zeros(acc_ref.shape, F32)


def _flash_step(s, v, m_ref, l_ref, acc_ref):
    m_old = m_ref[...]
    m_new = jnp.maximum(m_old, jnp.max(s, axis=1, keepdims=True))
    alpha = jnp.exp(m_old - m_new)
    p = jnp.exp(s - m_new)
    l_ref[...] = alpha * l_ref[...] + jnp.sum(p, axis=1, keepdims=True)
    acc_ref[...] = alpha * acc_ref[...] + _dot(p.astype(BF16), v)
    m_ref[...] = m_new


def _flash_result(l_ref, acc_ref):
    return acc_ref[...] / jnp.maximum(l_ref[...], 1e-30)


def _select_blocks(imp_t, q0, ns):
    shape = imp_t.shape
    blk = _iota(shape, 0)
    qpos = q0 + _iota(shape, 1)
    cur = qpos >> 6
    valid = blk * SEL_BLOCK <= qpos
    forced = (blk == 0) | (blk == cur) | (blk == cur - 1)
    imp_t = jnp.where(valid, jnp.where(forced, FORCE_SCORE, imp_t), NEG_INF)
    rank = jnp.zeros(shape, F32)
    for other in range(ns):
        row = imp_t[other:other + 1, :]
        ahead = (row > imp_t) | ((row == imp_t) & (blk > other))
        rank = rank + jnp.where(ahead, 1.0, 0.0)
    return jnp.where((rank < SEL_TOPK) & valid, 1.0, 0.0)


def _nsa_prompt_kernel(q_ref, ks_ref, vs_ref, kw_ref, vw_ref, kc_ref, vct_ref, tz_ref, cfar_ref, basec_ref,
                       ovt_ref, ng_ref, nsl_ref, o_ref, m_ref, l_ref, acc_ref, vst_ref, vwt_ref, *, ns):
    i = pl.program_id(2)
    q0 = i * TQ
    scale = NSA_DK ** -0.5
    qall = q_ref[...]
    qt = jnp.concatenate([qall[:, g * NSA_DK:(g + 1) * NSA_DK].T for g in range(NSA_GROUP)], axis=1).astype(BF16)
    c_loc = _iota((TK, NSA_ROWS), 0)
    r_loc = _iota((TK, NSA_ROWS), 1) & (TQ - 1)

    @pl.when(i == 0)
    def _():
        for kt in range(vst_ref.shape[0]):
            vst_ref[kt] = vs_ref[kt * TK:(kt + 1) * TK, :].T.astype(BF16)
            vwt_ref[kt] = vw_ref[kt * TK:(kt + 1) * TK, :].T.astype(BF16)

    ncp = kc_ref.shape[3]
    shift = (TQ // CMP_STRIDE) * i
    bias_c = basec_ref[0, pl.ds(pl.multiple_of(ncp - shift, TQ // CMP_STRIDE), ncp), :]
    s = _dot(kc_ref[0, 0, 0], qt) * scale + bias_c
    m = jnp.max(s, axis=0, keepdims=True)
    e = jnp.exp(s - m)
    inv = jnp.where(m > 0.5 * NEG_INF, 1.0 / jnp.maximum(jnp.sum(e, axis=0, keepdims=True), 1e-30), 0.0)
    p = e * inv
    o_cmp = _dot(vct_ref[0, 0], p.astype(BF16))

    psum = p[:, 0:TQ] + p[:, TQ:2 * TQ] + p[:, 2 * TQ:3 * TQ] + p[:, 3 * TQ:4 * TQ]
    hi = psum.astype(BF16)
    lo = (psum - hi.astype(F32)).astype(BF16)
    ovt = ovt_ref[...]
    imp_t = _dot(ovt, hi) + _dot(ovt, lo)
    ns8 = -(-ns // 8) * 8
    sel_t = _select_blocks(imp_t[:ns8], q0, ns)
    nsp = ovt.shape[0]
    if ns8 < nsp:
        sel_t = jnp.concatenate([sel_t, jnp.zeros((nsp - ns8, TQ), F32)], axis=0)
    sel_t = sel_t.astype(BF16)

    def flash_step(sc, vt):
        m_old = m_ref[...]
        m_new = jnp.maximum(m_old, jnp.max(sc, axis=0, keepdims=True))
        alpha = jnp.exp(m_old - m_new)
        pt = jnp.exp(sc - m_new)
        l_ref[...] = alpha * l_ref[...] + jnp.sum(pt, axis=0, keepdims=True)
        acc_ref[...] = alpha * acc_ref[...] + _dot(vt, pt.astype(BF16))
        m_ref[...] = m_new

    def sel_tile(kt, bias, causal):
        k = ks_ref[pl.ds(pl.multiple_of(kt * TK, TK), TK), :].astype(BF16)
        blk_of_key = (TK // SEL_BLOCK) * kt + (_iota((TK, nsp), 0) >> 6)
        expand = jnp.where(_iota((TK, nsp), 1) == blk_of_key, 1.0, 0.0).astype(BF16)
        mk = _dot(expand, sel_t)
        ok = jnp.concatenate([mk] * NSA_GROUP, axis=1) > 0.5
        if causal:
            ok = ok & (c_loc <= r_loc)
        sc = jnp.where(ok, _dot(k, qt) * scale + bias, NEG_INF)
        flash_step(sc, vst_ref[kt])

    _flash_init(m_ref, l_ref, acc_ref)
    sel_tile(i, tz_ref[0, 0], True)

    @pl.when(i >= 1)
    def _():
        sel_tile(i - 1, tz_ref[0, 1], False)

    def far_tile(t, carry):
        sel_tile(i - t, cfar_ref[0], False)
        return carry

    lax.fori_loop(2, i + 1, far_tile, 0)
    o_sel = _flash_result(l_ref, acc_ref)

    def win_tile(off):
        kt = i - off
        k = kw_ref[pl.ds(pl.multiple_of(kt * TK, TK), TK), :].astype(BF16)
        sc = _dot(k, qt) * scale + (tz_ref[0, off] if off < 2 else cfar_ref[0])
        if off == 0:
            sc = jnp.where(c_loc <= r_loc, sc, NEG_INF)
        if off * TK == WINDOW:
            sc = jnp.where(c_loc > r_loc, sc, NEG_INF)
        flash_step(sc, vwt_ref[kt])

    _flash_init(m_ref, l_ref, acc_ref)
    win_tile(0)
    for off in range(1, WINDOW // TK + 1):
        pl.when(i >= off)(functools.partial(win_tile, off))
    o_win = _flash_result(l_ref, acc_ref)

    gates_t = _sigmoid(ng_ref[...]).T
    nsl = nsl_ref[...]
    outs = []
    for g in range(NSA_GROUP):
        cols = slice(g * TQ, (g + 1) * TQ)
        o_t = (gates_t[3 * g:3 * g + 1] * o_cmp[:, cols] + gates_t[3 * g + 1:3 * g + 2] * o_sel[:, cols]
               + gates_t[3 * g + 2:3 * g + 3] * o_win[:, cols])
        x = nsl[:, g * NSA_DK:(g + 1) * NSA_DK]
        outs.append(o_t.T * (x * _sigmoid(x)))
    o_ref[...] = jnp.concatenate(outs, axis=1).astype(BF16)


def _lanes_by_head(t):
    hk, g, rows, tq = t.shape
    return t.transpose(0, 2, 1, 3).reshape(hk, rows, g * tq)


def _nsa_prompt(z, kcvc, bias_d, batch, seq):
    nq = seq // TQ
    ns = seq // SEL_BLOCK
    ncp = kcvc.shape[3]
    nsp = LANE
    wn = TQ // CMP_STRIDE
    assert TQ == TK and ns <= nsp and ncp >= seq // CMP_STRIDE and ncp % LANE == 0 and ncp > wn
    gw = NSA_GROUP * NSA_DK
    n_dist = bias_d.shape[-1]
    assert n_dist >= 2 * TK + TQ

    rep = jnp.tile(bias_d, (1, 1, TK + 1))[..., :TK * (n_dist - 1)].reshape(NSA_KV_HEADS, NSA_GROUP, TK, n_dist - 1)
    tz = jnp.stack([_lanes_by_head(rep[..., 0:TQ]), _lanes_by_head(rep[..., TK:TK + TQ])], axis=1)
    far = jnp.broadcast_to(bias_d[..., REL_MAX_DIST][:, :, None, None], (NSA_KV_HEADS, NSA_GROUP, 1, TQ))
    cfar = _lanes_by_head(far)
    half = n_dist // 2
    start = CMP_STRIDE * wn - (CMP_BLOCK - 1)
    assert start + TQ <= half and 2 * wn * CMP_STRIDE - start <= half
    w_ext = jnp.concatenate([bias_d[..., :half], jnp.full(bias_d.shape[:-1] + (n_dist - half,), NEG_INF, F32)], -1)
    near = jnp.tile(w_ext, (1, 1, 2 * wn + 1))[..., :2 * wn * (n_dist - CMP_STRIDE)]
    near = near.reshape(NSA_KV_HEADS, NSA_GROUP, 2 * wn, n_dist - CMP_STRIDE)[..., start:start + TQ]
    basec = jnp.concatenate([jnp.broadcast_to(far, (NSA_KV_HEADS, NSA_GROUP, ncp - wn, TQ)), near,
                             jnp.full((NSA_KV_HEADS, NSA_GROUP, ncp - wn, TQ), NEG_INF, F32)], axis=2)
    basec = _lanes_by_head(basec)
    sblk = jnp.arange(nsp)[:, None]
    nblk = jnp.arange(ncp)[None, :]
    ovt = ((nblk >= 4 * sblk - 1) & (nblk <= 4 * sblk + 3)).astype(BF16)

    vct = kcvc[:, 1].transpose(0, 1, 3, 2)

    def kvspec(col0, which):
        return pl.BlockSpec((seq, NSA_DK), lambda b, h, i: (b, col0 // NSA_DK + which * NSA_KV_HEADS + h))

    vt_scratch = pltpu.VMEM((seq // TK, NSA_DK, TK), BF16)
    return pl.pallas_call(
        functools.partial(_nsa_prompt_kernel, ns=ns),
        grid=(batch, NSA_KV_HEADS, nq),
        in_specs=[pl.BlockSpec((TQ, gw), lambda b, h, i: (b * nq + i, COL_NQ // gw + h)),
                  kvspec(COL_KVS, 0), kvspec(COL_KVS, 1), kvspec(COL_KVW, 0), kvspec(COL_KVW, 1),
                  pl.BlockSpec((1, 1, 1, ncp, NSA_DK), lambda b, h, i: (b, 0, h, 0, 0)),
                  pl.BlockSpec((1, 1, NSA_DK, ncp), lambda b, h, i: (b, h, 0, 0)),
                  pl.BlockSpec((1, 2, TK, NSA_ROWS), lambda b, h, i: (h, 0, 0, 0)),
                  pl.BlockSpec((1, 1, NSA_ROWS), lambda b, h, i: (h, 0, 0)),
                  pl.BlockSpec((1, 2 * ncp, NSA_ROWS), lambda b, h, i: (h, 0, 0)),
                  pl.BlockSpec((nsp, ncp), lambda b, h, i: (0, 0)),
                  pl.BlockSpec((TQ, LANE), lambda b, h, i: (b * nq + i, COL_NG // LANE + h)),
                  pl.BlockSpec((TQ, gw), lambda b, h, i: (b * nq + i, COL_NSL // gw + h))],
        out_specs=pl.BlockSpec((TQ, gw), lambda b, h, i: (b * nq + i, h)),
        out_shape=jax.ShapeDtypeStruct((batch * seq, NSA_W), BF16),
        scratch_shapes=[pltpu.VMEM((1, NSA_ROWS), F32), pltpu.VMEM((1, NSA_ROWS), F32),
                        pltpu.VMEM((NSA_DK, NSA_ROWS), F32), vt_scratch, vt_scratch],
        compiler_params=_cparams(3),
        name="nsa_prompt",
    )(z, z, z, z, z, kcvc, vct, tz, cfar, basec, ovt, z, z)


SROWS = 8


def _stack_group_q(q_row):
    heads = [q_row[:, g * NSA_DK:(g + 1) * NSA_DK] for g in range(NSA_GROUP)]
    return jnp.concatenate(heads + [heads[0]] * (SROWS - NSA_GROUP), axis=0)


def _nsa_sample_cmp_kernel(q_ref, kc_ref, vc_ref, bias_ref, ov_ref, o_ref, idx_ref, *, ns):
    scale = NSA_DK ** -0.5
    qs = _stack_group_q(q_ref[0]).astype(BF16)
    s = _nt(qs, kc_ref[0, 0]) * scale + bias_ref[0]
    m = jnp.max(s, axis=1, keepdims=True)
    e = jnp.exp(s - m)
    inv = jnp.where(m > 0.5 * NEG_INF, 1.0 / jnp.maximum(jnp.sum(e, axis=1, keepdims=True), 1e-30), 0.0)
    p = e * inv
    o_ref[0, 0] = _dot(p.astype(BF16), vc_ref[0, 0])
    psum = jnp.broadcast_to(jnp.sum(p[0:NSA_GROUP], axis=0, keepdims=True), p.shape)
    hi = psum.astype(BF16)
    lo = (psum - hi.astype(F32)).astype(BF16)
    imp = (_dot(hi, ov_ref[...]) + _dot(lo, ov_ref[...]))[0:1]
    nsp = imp.shape[1]
    blk_r = _iota((1, nsp), 1)
    cur = ns - 1
    forced = (blk_r == 0) | (blk_r == cur) | (blk_r == cur - 1)
    imp = jnp.where(blk_r < ns, jnp.where(forced, FORCE_SCORE, imp), 2.0 * NEG_INF)
    imp_c = _column_of(imp)
    i_r = _iota((nsp, nsp), 1)
    j_c = _iota((nsp, nsp), 0)
    ahead = (imp > imp_c) | ((imp == imp_c) & (i_r < j_c))
    rank_c = jnp.sum(jnp.where(ahead, 1.0, 0.0), axis=1, keepdims=True)
    slot = _iota((nsp, LANE), 1).astype(F32)
    picks = jnp.where(rank_c == slot, _iota((nsp, LANE), 0).astype(F32), 0.0)
    idx_ref[0, 0] = jnp.broadcast_to(jnp.sum(picks, axis=0, keepdims=True), (SROWS, LANE)).astype(jnp.int32)


def _nsa_sample_cmp(z3, kcvc, bias_c, ov, ns):
    nb = z3.shape[0]
    ncp = kcvc.shape[2]
    nsp = ov.shape[1]
    gw = NSA_GROUP * NSA_DK

    def cspec(which):
        return pl.BlockSpec((1, 1, ncp, NSA_DK), lambda b, h: (b, which, 0, h))

    return pl.pallas_call(
        functools.partial(_nsa_sample_cmp_kernel, ns=ns),
        grid=(nb, NSA_KV_HEADS),
        in_specs=[pl.BlockSpec((1, 1, gw), lambda b, h: (b, 0, COL_NQ // gw + h)),
                  cspec(0), cspec(1),
                  pl.BlockSpec((1, SROWS, ncp), lambda b, h: (h, 0, 0)),
                  pl.BlockSpec((ncp, nsp), lambda b, h: (0, 0))],
        out_specs=[pl.BlockSpec((1, 1, SROWS, NSA_DK), lambda b, h: (b, h, 0, 0)),
                   pl.BlockSpec((1, 1, SROWS, LANE), lambda b, h: (b, h, 0, 0))],
        out_shape=[jax.ShapeDtypeStruct((nb, NSA_KV_HEADS, SROWS, NSA_DK), F32),
                   jax.ShapeDtypeStruct((nb, NSA_KV_HEADS, SROWS, LANE), jnp.int32)],
        compiler_params=_cparams(2),
        name="nsa_sample_cmp",
    )(z3, kcvc, kcvc, bias_c, ov)


def _nsa_sample_sel_kernel(pt_ref, idx_ref, q_ref, *refs, ns):
    blocks, (new_ref,), biases = refs[:NSA_KV_HEADS], refs[NSA_KV_HEADS:NSA_KV_HEADS + 1], refs[NSA_KV_HEADS + 1:2 * NSA_KV_HEADS + 1]
    o_ref, m_ref, l_ref, acc_ref = refs[2 * NSA_KV_HEADS + 1:]
    b, t = pl.program_id(0), pl.program_id(1)
    gw = NSA_GROUP * NSA_DK
    rows_kv = 2 * NSA_KV_HEADS

    @pl.when(t == 0)
    def _():
        _flash_init(m_ref, l_ref, acc_ref)

    new_rows = jnp.concatenate([new_ref[0]] * SEL_BLOCK, axis=0)
    for h in range(NSA_KV_HEADS):
        is_new = idx_ref[b, h, t] == ns - 1
        x = blocks[h][0].reshape(SEL_BLOCK * rows_kv, NSA_DK)
        x = jnp.where(is_new, new_rows, x).astype(BF16)
        qs = _stack_group_q(q_ref[0][:, h * gw:(h + 1) * gw]).astype(BF16)
        sc = _nt(qs, x) * (NSA_DK ** -0.5) + biases[h][0, 0]
        m_old = m_ref[h]
        m_new = jnp.maximum(m_old, jnp.max(sc, axis=1, keepdims=True))
        alpha = jnp.exp(m_old - m_new)
        p = jnp.exp(sc - m_new)
        l_ref[h] = alpha * l_ref[h] + jnp.sum(p, axis=1, keepdims=True)
        acc_ref[h] = alpha * acc_ref[h] + _dot(pltpu.roll(p, NSA_KV_HEADS, 1).astype(BF16), x)
        m_ref[h] = m_new

    @pl.when(t == pl.num_programs(1) - 1)
    def _():
        o_ref[0] = acc_ref[...] / jnp.maximum(l_ref[...], 1e-30)


def _nsa_sample_sel(z3, cache, kv_new, page_table, idx, bias_sel, ns):
    nb, n_pages = page_table.shape
    n_sel = idx.shape[2]
    halves = PAGE_SIZE // SEL_BLOCK
    rows_kv = 2 * NSA_KV_HEADS

    def blockspec(h):
        def index(b, t, pt, ix):
            blk = ix[b, h, t]
            return (pt[b, jnp.minimum(blk // halves, n_pages - 1)], blk % halves, 0, 0)
        return pl.BlockSpec((1, SEL_BLOCK, rows_kv, NSA_DK), index)

    def biasspec(h):
        return pl.BlockSpec((1, 1, SROWS, SEL_BLOCK * rows_kv), lambda b, t, pt, ix: (h, ix[b, h, t], 0, 0))

    heads = range(NSA_KV_HEADS)
    return pl.pallas_call(
        functools.partial(_nsa_sample_sel_kernel, ns=ns),
        grid_spec=pltpu.PrefetchScalarGridSpec(
            num_scalar_prefetch=2,
            grid=(nb, n_sel),
            in_specs=[pl.BlockSpec((1, 1, NSA_W), lambda b, t, pt, ix: (b, 0, COL_NQ // NSA_W))]
                     + [blockspec(h) for h in heads]
                     + [pl.BlockSpec((1, rows_kv, NSA_DK), lambda b, t, pt, ix: (b, 0, 0))]
                     + [biasspec(h) for h in heads],
            out_specs=pl.BlockSpec((1, NSA_KV_HEADS, SROWS, NSA_DK), lambda b, t, pt, ix: (b, 0, 0, 0)),
            scratch_shapes=[pltpu.VMEM((NSA_KV_HEADS, SROWS, 1), F32), pltpu.VMEM((NSA_KV_HEADS, SROWS, 1), F32),
                            pltpu.VMEM((NSA_KV_HEADS, SROWS, NSA_DK), F32)]),
        out_shape=jax.ShapeDtypeStruct((nb, NSA_KV_HEADS, SROWS, NSA_DK), F32),
        compiler_params=_cparams(2),
        name="nsa_sample_sel",
    )(page_table, idx, z3, *([cache] * NSA_KV_HEADS), kv_new, *([bias_sel] * NSA_KV_HEADS))


def _nsa_sample_win_kernel(q_ref, k_ref, v_ref, kn_ref, vn_ref, bias_ref, bnew_ref, o_ref):
    scale = NSA_DK ** -0.5
    q = _stack_group_q(q_ref[0])
    s_buf = _nt(q.astype(BF16), k_ref[0].astype(BF16)) * scale + bias_ref[0]
    s_new = jnp.sum(q * kn_ref[0], axis=1, keepdims=True) * scale + bnew_ref[0]
    m = jnp.maximum(jnp.max(s_buf, axis=1, keepdims=True), s_new)
    p_buf = jnp.exp(s_buf - m)
    p_new = jnp.exp(s_new - m)
    l = jnp.sum(p_buf, axis=1, keepdims=True) + p_new
    acc = _dot(p_buf.astype(BF16), v_ref[0].astype(BF16)) + p_new * vn_ref[0]
    o_ref[0, 0] = acc / jnp.maximum(l, 1e-30)


def _nsa_sample_win(z3, win_buf, bias_win, bias_new):
    nb, nbuf, _ = win_buf.shape
    gw = NSA_GROUP * NSA_DK

    def bufspec(which):
        return pl.BlockSpec((1, nbuf, NSA_DK), lambda b, h: (b, 0, which * NSA_KV_HEADS + h))

    def newspec(which):
        return pl.BlockSpec((1, 1, NSA_DK), lambda b, h: (b, 0, COL_KVW // NSA_DK + which * NSA_KV_HEADS + h))

    return pl.pallas_call(
        _nsa_sample_win_kernel,
        grid=(nb, NSA_KV_HEADS),
        in_specs=[pl.BlockSpec((1, 1, gw), lambda b, h: (b, 0, COL_NQ // gw + h)),
                  bufspec(0), bufspec(1), newspec(0), newspec(1),
                  pl.BlockSpec((1, SROWS, nbuf), lambda b, h: (h, 0, 0)),
                  pl.BlockSpec((1, SROWS, 1), lambda b, h: (h, 0, 0))],
        out_specs=pl.BlockSpec((1, 1, SROWS, NSA_DK), lambda b, h: (b, h, 0, 0)),
        out_shape=jax.ShapeDtypeStruct((nb, NSA_KV_HEADS, SROWS, NSA_DK), F32),
        compiler_params=_cparams(2),
        name="nsa_sample_win",
    )(z3, win_buf, win_buf, z3, z3, bias_win, bias_new)


def _nsa_sample_gate_kernel(oc_ref, os_ref, ow_ref, ng_ref, nsl_ref, o_ref):
    gates = _sigmoid(ng_ref[0])
    nsl = nsl_ref[0]
    outs = []
    for g in range(NSA_GROUP):
        o = (gates[:, 3 * g:3 * g + 1] * oc_ref[0, 0, g:g + 1] + gates[:, 3 * g + 1:3 * g + 2] * os_ref[0, 0, g:g + 1]
             + gates[:, 3 * g + 2:3 * g + 3] * ow_ref[0, 0, g:g + 1])
        x = nsl[:, g * NSA_DK:(g + 1) * NSA_DK]
        outs.append(o * (x * _sigmoid(x)))
    o_ref[0] = jnp.concatenate(outs, axis=1).astype(BF16)


def _nsa_sample_gate(o_cmp, o_sel, o_win, z3):
    nb = z3.shape[0]
    gw = NSA_GROUP * NSA_DK
    ospec = pl.BlockSpec((1, 1, SROWS, NSA_DK), lambda b, h: (b, h, 0, 0))
    return pl.pallas_call(
        _nsa_sample_gate_kernel,
        grid=(nb, NSA_KV_HEADS),
        in_specs=[ospec, ospec, ospec,
                  pl.BlockSpec((1, 1, LANE), lambda b, h: (b, 0, COL_NG // LANE + h)),
                  pl.BlockSpec((1, 1, gw), lambda b, h: (b, 0, COL_NSL // gw + h))],
        out_specs=pl.BlockSpec((1, 1, gw), lambda b, h: (b, 0, h)),
        out_shape=jax.ShapeDtypeStruct((nb, 1, NSA_W), BF16),
        compiler_params=_cparams(2),
        name="nsa_sample_gate",
    )(o_cmp, o_sel, o_win, z3, z3)


def _mem_heads(q, kv):
    outs = []
    for h in range(MEM_HEADS):
        k = kv[:, h * MEM_DH:(h + 1) * MEM_DH].astype(BF16)
        v = kv[:, MEM_W + h * MEM_DH:MEM_W + (h + 1) * MEM_DH].astype(BF16)
        s = _nt(q[:, h * MEM_DH:(h + 1) * MEM_DH].astype(BF16), k) * (MEM_DH ** -0.5)
        e = jnp.exp(s - jnp.max(s, axis=1, keepdims=True))
        p = e / jnp.sum(e, axis=1, keepdims=True)
        outs.append(_dot(p.astype(BF16), v))
    return jnp.concatenate(outs, axis=1)


def _mem_prompt_kernel(q_ref, kv_ref, o_ref):
    o_ref[...] = _mem_heads(q_ref[...], kv_ref[...]).astype(BF16)


def _mem_prompt(z, mem_kv, batch, seq, tq):
    nq = seq // tq
    n_mem = mem_kv.shape[0] // batch
    return pl.pallas_call(
        _mem_prompt_kernel,
        grid=(batch, nq),
        in_specs=[pl.BlockSpec((tq, MEM_W), lambda b, i: (b * nq + i, COL_MQ // MEM_W)),
                  pl.BlockSpec((n_mem, 2 * MEM_W), lambda b, i: (b, 0))],
        out_specs=pl.BlockSpec((tq, MEM_W), lambda b, i: (b * nq + i, 0)),
        out_shape=jax.ShapeDtypeStruct((batch * seq, MEM_W), BF16),
        compiler_params=_cparams(2),
        name="mem_prompt",
    )(z, mem_kv)


def _mem_sample_kernel(q_ref, kv_ref, o_ref):
    q = jnp.broadcast_to(q_ref[0], (SROWS, MEM_W))
    o_ref[0] = _mem_heads(q, kv_ref[0])[0:1].astype(BF16)


def _mem_sample(z3, mem_kv):
    nb, n_mem, _ = mem_kv.shape
    return pl.pallas_call(
        _mem_sample_kernel,
        grid=(nb,),
        in_specs=[pl.BlockSpec((1, 1, MEM_W), lambda b: (b, 0, COL_MQ // MEM_W)),
                  pl.BlockSpec((1, n_mem, 2 * MEM_W), lambda b: (b, 0, 0))],
        out_specs=pl.BlockSpec((1, 1, MEM_W), lambda b: (b, 0, 0)),
        out_shape=jax.ShapeDtypeStruct((nb, 1, MEM_W), BF16),
        compiler_params=_cparams(1),
        name="mem_sample",
    )(z3, mem_kv)


def _merge_kernel(ar_ref, an_ref, am_ref, wr_ref, wn_ref, wm_ref, g0_ref, g1_ref, g2_ref, o_ref):
    merged = (_sigmoid(g0_ref[...]) * _dot(ar_ref[...], wr_ref[...])
              + _sigmoid(g1_ref[...]) * _dot(an_ref[...], wn_ref[...])
              + _sigmoid(g2_ref[...]) * _dot(am_ref[...], wm_ref[...]))
    o_ref[...] = merged.astype(BF16)


def _merge(a_ret, a_nsa, a_mem, w_ret, w_nsa, w_mem, z, tm, tn):
    m = a_ret.shape[0]
    nt = D_MODEL // tn

    def aspec(width):
        return pl.BlockSpec((tm, width), lambda i, j: (i, 0))

    def wspec(width):
        return pl.BlockSpec((width, tn), lambda i, j: (0, j))

    def gspec(branch):
        return pl.BlockSpec((tm, tn), lambda i, j: (i, COL_MG // tn + branch * nt + j))

    return pl.pallas_call(
        _merge_kernel,
        grid=(m // tm, nt),
        in_specs=[aspec(RET_W), aspec(NSA_W), aspec(MEM_W), wspec(RET_W), wspec(NSA_W), wspec(MEM_W),
                  gspec(0), gspec(1), gspec(2)],
        out_specs=pl.BlockSpec((tm, tn), lambda i, j: (i, j)),
        out_shape=jax.ShapeDtypeStruct((m, D_MODEL), BF16),
        compiler_params=_cparams(2),
        name="merge",
    )(a_ret, a_nsa, a_mem, w_ret, w_nsa, w_mem, z, z, z)


def _out_kernel(a_ref, w_ref, x_ref, g_ref, o_ref):
    out = _dot(a_ref[...], w_ref[...])
    y = out * lax.rsqrt(jnp.mean(out * out, axis=-1, keepdims=True) + EPS)
    o_ref[...] = x_ref[...] + y * g_ref[...]


def _out_proj(merged, w_out, x, norm_post, tm):
    m = merged.shape[0]
    return pl.pallas_call(
        _out_kernel,
        grid=(m // tm,),
        in_specs=[pl.BlockSpec((tm, D_MODEL), lambda i: (i, 0)),
                  pl.BlockSpec((D_MODEL, D_MODEL), lambda i: (0, 0)),
                  pl.BlockSpec((tm, D_MODEL), lambda i: (i, 0)),
                  pl.BlockSpec((1, D_MODEL), lambda i: (0, 0))],
        out_specs=pl.BlockSpec((tm, D_MODEL), lambda i: (i, 0)),
        out_shape=jax.ShapeDtypeStruct((m, D_MODEL), F32),
        compiler_params=_cparams(1),
        name="out_proj",
    )(merged, w_out, x, norm_post.reshape(1, D_MODEL))


def _layout_w_in(w_in):
    splits = (RET_HEADS * RET_DK, RET_HEADS * RET_DK, RET_W, RET_W, NSA_W, 2 * KV_W, 2 * KV_W, 2 * KV_W,
              N_BRANCHES * NSA_HEADS, NSA_W, MEM_W, N_BRANCHES * D_MODEL)
    pieces, start = [], 0
    for w in splits:
        pieces.append(w_in[:, start:start + w])
        start += w
    rq, rk, rv, rg, nq, kvc, kvs, kvw, ng, nsl, mq, mg = pieces
    per_group = N_BRANCHES * NSA_GROUP
    ng = ng.reshape(D_MODEL, NSA_KV_HEADS, per_group)
    ng = jnp.pad(ng, ((0, 0), (0, 0), (0, LANE - per_group))).reshape(D_MODEL, NG_SLOT)
    return jnp.concatenate([rq, rk, rv, rg, nq, kvc, kvs, kvw, nsl, mq, mg, ng], axis=1).astype(BF16)


def _pick_tile(m, cap):
    t = min(m, cap)
    while m % t:
        t //= 2
    return t


def kernel(x_prompt, x_sample, cache_cmp_kv, cache_sel_kv, cache_win_kv, state_ret, cache_mem_kv, page_table,
           mem_prompt, rel_table, norm_pre, norm_post, norm_mem, w_in, ret_norm, w_ret_up, cmp_pos, w_cmp1,
           w_cmp2, w_nsa_up, w_mem_kv, w_mem_up, w_out):
    batch, seq, _ = x_prompt.shape
    nb = x_sample.shape[0]
    assert x_sample.shape[1] == 1 and norm_pre.shape[0] == 1
    assert seq % TQ == 0 and seq >= WINDOW
    n_pool = cache_cmp_kv.shape[1]
    n_pages = page_table.shape[1]
    past = n_pages * PAGE_SIZE
    n_mem = mem_prompt.shape[1]
    assert n_pages % PAGES_PER_STEP == 0 and cache_win_kv.shape[2] == WINDOW

    w_proj = _layout_w_in(w_in[0])
    kw = CMP_STRIDE * NSA_DK
    w1 = w_cmp1[0].reshape(2, CMP_BLOCK * NSA_DK, NSA_DK).astype(BF16)
    w1a, w1b = w1[:, :kw], w1[:, kw:]
    w1ab = jnp.concatenate([w1a, w1b], axis=2)
    w2 = w_cmp2[0].astype(BF16)
    pos8 = jnp.pad(cmp_pos[0].reshape(2, 1, CMP_BLOCK * NSA_DK), ((0, 0), (0, 7), (0, 0)))
    w_ret = w_ret_up[0].astype(BF16)
    w_nsa = w_nsa_up[0].astype(BF16)
    w_mem = w_mem_up[0].astype(BF16)
    w_o = w_out[0].astype(BF16)
    w_mkv = w_mem_kv[0].astype(BF16)

    m_p = batch * seq
    xp = x_prompt.reshape(m_p, D_MODEL)
    z = _norm_matmul(xp, norm_pre[0], w_proj, _pick_tile(m_p, 1024), 512)

    a_ret, ret_state_p = _retention_prompt(z, ret_norm[0], batch, seq)

    ncp = max(LANE, -(-(seq // CMP_STRIDE) // LANE) * LANE)
    ac = _cmp_stage1_dense(z, w1ab, batch, seq)
    kcvc = _cmp_stage2(ac, pos8, w1a, w1b, w2, 1)
    if ncp > kcvc.shape[3]:
        kcvc = jnp.pad(kcvc, ((0, 0), (0, 0), (0, 0), (0, ncp - kcvc.shape[3]), (0, 0)))
    bias_d = _bias_by_dist(rel_table, BIAS_DISTS)
    a_nsa = _nsa_prompt(z, kcvc, bias_d, batch, seq)

    mem_kv_p = _norm_matmul(mem_prompt.reshape(batch * n_mem, D_MODEL), norm_mem[0], w_mkv,
                            _pick_tile(batch * n_mem, 512), 512)
    a_mem = _mem_prompt(z, mem_kv_p, batch, seq, _pick_tile(seq, 512))

    merged = _merge(a_ret, a_nsa, a_mem, w_ret, w_nsa, w_mem, z, _pick_tile(m_p, 512), 512)
    y_p = _out_proj(merged, w_o, xp, norm_post[0], _pick_tile(m_p, 256)).reshape(batch, seq, D_MODEL)

    kv_shape = (1, batch, seq, 2, NSA_KV_HEADS, NSA_DK)
    new_cmp_p = z[:, COL_KVC:COL_KVC + 2 * KV_W].reshape(kv_shape)
    new_sel_p = z[:, COL_KVS:COL_KVS + 2 * KV_W].reshape(kv_shape)
    new_win_p = z[:, COL_KVW:COL_KVW + 2 * KV_W].reshape(kv_shape)[:, :, seq - WINDOW:]
    new_ret_p = ret_state_p[None]
    new_mem_p = mem_kv_p.reshape(1, batch, n_mem, 2, MEM_HEADS, MEM_DH)

    xs = x_sample.reshape(nb, D_MODEL)
    zs = _norm_matmul(xs, norm_pre[0], w_proj, nb, 512)
    z3 = zs.reshape(nb, 1, PROJ_W)

    a_ret_s, ret_state_s = _retention_sample(z3, state_ret[0], ret_norm[0], past)

    cache_c = cache_cmp_kv[0].reshape(n_pool, PAGE_SIZE // CMP_STRIDE, CMP_STRIDE, 2 * NSA_KV_HEADS, NSA_DK)
    cache_s = cache_sel_kv[0].reshape(n_pool, PAGE_SIZE, 2 * NSA_KV_HEADS, NSA_DK)
    ac_s = _cmp_stage1_paged(cache_c, page_table, w1ab)
    kcvc_s = _cmp_stage2(ac_s, pos8, w1a, w1b, w2, NSA_KV_HEADS)
    ncs = past // CMP_STRIDE
    kcvc_s = kcvc_s.reshape(nb, 2, ncs, KV_W)
    ns_s = past // SEL_BLOCK + 1
    nsp_s = -(-ns_s // LANE) * LANE
    assert past >= WINDOW and past >= REL_MAX_DIST and BIAS_DISTS > WINDOW
    far_s = bias_d[..., REL_MAX_DIST:REL_MAX_DIST + 1]
    hg = (NSA_KV_HEADS, NSA_GROUP)
    n_valid = (past - (CMP_BLOCK - 1)) // CMP_STRIDE + 1
    strided = bias_d[..., (past - (CMP_BLOCK - 1)) % CMP_STRIDE::CMP_STRIDE]
    n_tab = strided.shape[-1]
    assert n_valid >= n_tab and n_tab * CMP_STRIDE > REL_MAX_DIST + CMP_STRIDE and ncs >= n_valid
    bias_cs = jnp.concatenate([jnp.broadcast_to(far_s, hg + (n_valid - n_tab,)), strided[..., ::-1],
                               jnp.full(hg + (ncs - n_valid,), NEG_INF, F32)], axis=-1)
    bias_cs = _pad_group_rows(bias_cs, 1)
    nblk = jnp.arange(ncs)[:, None]
    sblk = jnp.arange(nsp_s)[None, :]
    ov_s = ((nblk >= 4 * sblk - 1) & (nblk <= 4 * sblk + 3)).astype(BF16)
    o_cmp_s, idx_s = _nsa_sample_cmp(z3, kcvc_s, bias_cs, ov_s, ns_s)
    n_sel = min(SEL_TOPK, ns_s)
    idx = idx_s[:, :, 0, :n_sel]

    hg = (NSA_KV_HEADS, NSA_GROUP)
    n_key = ns_s * SEL_BLOCK
    bias_sel = jnp.concatenate([jnp.broadcast_to(far_s, hg + (past + 1 - REL_MAX_DIST,)),
                                bias_d[..., :REL_MAX_DIST][..., ::-1],
                                jnp.full(hg + (n_key - past - 1,), NEG_INF, F32)], axis=-1)
    bias_sel = _pad_group_rows(bias_sel.reshape(hg + (ns_s, SEL_BLOCK)).transpose(0, 2, 1, 3), 2)
    own_k = jnp.arange(2 * NSA_KV_HEADS)[None, :] == jnp.arange(NSA_KV_HEADS)[:, None]
    bias_sel = jnp.where(own_k[:, None, None, None, :], bias_sel[..., None], NEG_INF)
    bias_sel = bias_sel.reshape(NSA_KV_HEADS, ns_s, SROWS, SEL_BLOCK * 2 * NSA_KV_HEADS)
    kvs_new = zs[:, COL_KVS:COL_KVS + 2 * KV_W].reshape(nb, 2 * NSA_KV_HEADS, NSA_DK)
    o_sel_s = _nsa_sample_sel(z3, cache_s, kvs_new, page_table, idx, bias_sel, ns_s)

    win_buf = cache_win_kv[0].reshape(nb, WINDOW, 2 * KV_W)
    bias_w = jnp.concatenate([jnp.full(hg + (1,), NEG_INF, F32), bias_d[..., 1:WINDOW][..., ::-1]], axis=-1)
    o_win_s = _nsa_sample_win(z3, win_buf, _pad_group_rows(bias_w, 1), _pad_group_rows(bias_d[..., 0:1], 1))
    a_nsa_s = _nsa_sample_gate(o_cmp_s, o_sel_s, o_win_s, z3)

    mem_kv_s = cache_mem_kv[0].reshape(nb, n_mem, 2 * MEM_W)
    a_mem_s = _mem_sample(z3, mem_kv_s)

    merged_s = _merge(a_ret_s.reshape(nb, RET_W), a_nsa_s.reshape(nb, NSA_W), a_mem_s.reshape(nb, MEM_W),
                      w_ret, w_nsa, w_mem, zs, nb, 512)
    y_s = _out_proj(merged_s, w_o, xs, norm_post[0], nb).reshape(nb, 1, D_MODEL)

    kvs_shape = (1, nb, 1, 2, NSA_KV_HEADS, NSA_DK)
    new_cmp_s = zs[:, COL_KVC:COL_KVC + 2 * KV_W].reshape(kvs_shape)
    new_sel_s = zs[:, COL_KVS:COL_KVS + 2 * KV_W].reshape(kvs_shape)
    kvw_s = zs[:, COL_KVW:COL_KVW + 2 * KV_W].reshape(nb, 1, 2, NSA_KV_HEADS, NSA_DK)
    new_win_s = jnp.concatenate([cache_win_kv[0][:, 1:], kvw_s], axis=1)[None]
    new_ret_s = ret_state_s[None]

    return (y_p, y_s, new_cmp_p, new_sel_p, new_win_p, new_ret_p, new_mem_p,
            new_cmp_s, new_sel_s, new_win_s, new_ret_s)
```

```python
import functools
import math

import jax
import jax.numpy as jnp
from jax import lax
from jax.experimental import pallas as pl
from jax.experimental.pallas import tpu as pltpu

F32 = jnp.float32
BF16 = jnp.bfloat16

D_MODEL = 2048
PAGE_SIZE = 128
RET_HEADS = 8
RET_DK = 256
RET_DV = 256
RET_CHUNK = 128
ROPE_BASE = 10000.0
NSA_HEADS = 16
NSA_KV_HEADS = 4
NSA_GROUP = NSA_HEADS // NSA_KV_HEADS
NSA_DK = 128
CMP_BLOCK = 32
CMP_STRIDE = 16
SEL_BLOCK = 64
SEL_TOPK = 16
WINDOW = 512
MEM_HEADS = 4
MEM_DH = 384
REL_BUCKETS = 32
REL_MAX_EXACT = 16
REL_MAX_DIST = 128
N_BRANCHES = 3
EPS = 1e-6
NEG_INF = -1e30
FORCE_SCORE = 1e4

RET_W = RET_HEADS * RET_DV
NSA_W = NSA_HEADS * NSA_DK
KV_W = NSA_KV_HEADS * NSA_DK
MEM_W = MEM_HEADS * MEM_DH

COL_RQ = 0
COL_RK = COL_RQ + RET_HEADS * RET_DK
COL_RV = COL_RK + RET_HEADS * RET_DK
COL_RG = COL_RV + RET_W
COL_NQ = COL_RG + RET_W
COL_KVC = COL_NQ + NSA_W
COL_KVS = COL_KVC + 2 * KV_W
COL_KVW = COL_KVS + 2 * KV_W
PROJ_A = COL_KVW + 2 * KV_W
COL_MQ = 0
COL_NSL = COL_MQ + MEM_W
COL_MG = COL_NSL + NSA_W
COL_NG = COL_MG + N_BRANCHES * D_MODEL
NG_SLOT = NSA_KV_HEADS * 128
PROJ_B = COL_NG + NG_SLOT

LOG2E = math.log2(math.e)
LANE = 128
TQ = 256
TK = 256
NSA_ROWS = NSA_GROUP * TQ
BIAS_DISTS = 1024
RET_STEP_CHUNKS = 8
VMEM_LIMIT = 48 * 1024 * 1024


def _cparams(n_axes):
    return pltpu.CompilerParams(dimension_semantics=("arbitrary",) * n_axes, vmem_limit_bytes=VMEM_LIMIT)


def _nt(a, b):
    return lax.dot_general(a, b, (((1,), (1,)), ((), ())), preferred_element_type=F32)


def _dot(a, b):
    return jnp.dot(a, b, preferred_element_type=F32)


def _sigmoid(x):
    return 1.0 / (1.0 + jnp.exp(-x))


def _iota(shape, dim):
    return lax.broadcasted_iota(jnp.int32, shape, dim)


def _norm_matmul_kernel(x_ref, g_ref, w_ref, o_ref, h_ref):
    @pl.when(pl.program_id(1) == 0)
    def _():
        x = x_ref[...]
        ms = jnp.mean(x * x, axis=-1, keepdims=True)
        h_ref[...] = ((x * lax.rsqrt(ms + EPS)) * g_ref[...]).astype(BF16)

    o_ref[...] = _dot(h_ref[...], w_ref[...].astype(BF16))


def _norm_matmul(x, g, w, tm, tn, n=None):
    m, k = x.shape
    n = w.shape[1] if n is None else n
    return pl.pallas_call(
        _norm_matmul_kernel,
        grid=(m // tm, n // tn),
        in_specs=[pl.BlockSpec((tm, k), lambda i, j: (i, 0)),
                  pl.BlockSpec((1, k), lambda i, j: (0, 0)),
                  pl.BlockSpec((k, tn), lambda i, j: (0, j))],
        out_specs=pl.BlockSpec((tm, tn), lambda i, j: (i, j)),
        out_shape=jax.ShapeDtypeStruct((m, n), F32),
        scratch_shapes=[pltpu.VMEM((tm, k), BF16)],
        compiler_params=_cparams(2),
        name="norm_matmul",
    )(x, g.reshape(1, k), w)


def _rope_rows(x, cos, sin):
    half = x.shape[-1] // 2
    x1, x2 = x[:, :half], x[:, half:]
    return jnp.concatenate([x1 * cos - x2 * sin, x1 * sin + x2 * cos], axis=-1)


def _head_norm_gate(o, gnorm, rg):
    oc = o - jnp.mean(o, axis=-1, keepdims=True)
    y = oc * lax.rsqrt(jnp.mean(oc * oc, axis=-1, keepdims=True) + EPS) * gnorm
    return y * (rg * _sigmoid(rg))


def _ret_prompt_kernel(q_ref, k_ref, v_ref, rg_ref, cos_ref, sin_ref, dmat_ref, xi_ref, zeta_ref, gc_ref,
                       gn_ref, a_ref, s_ref):
    @pl.when(pl.program_id(2) == 0)
    def _():
        s_ref[...] = jnp.zeros_like(s_ref)

    c = RET_CHUNK
    for t in range(q_ref.shape[0] // c):
        rows = slice(t * c, (t + 1) * c)
        cos, sin = cos_ref[rows, :], sin_ref[rows, :]
        q = _rope_rows(q_ref[rows, :], cos, sin)
        k = _rope_rows(k_ref[rows, :], cos, sin) * (RET_DK ** -0.5)
        qb, vb = q.astype(BF16), v_ref[rows, :].astype(BF16)
        state = s_ref[0, 0]
        inner = _nt(qb, k.astype(BF16)) * dmat_ref[0]
        o = _dot(inner.astype(BF16), vb) + _dot(qb, state.astype(BF16)) * xi_ref[0]
        kz_t = (k * zeta_ref[0]).T.astype(BF16)
        s_ref[0, 0] = state * gc_ref[0] + _dot(kz_t, vb)
        a_ref[rows, :] = _head_norm_gate(o, gn_ref[...], rg_ref[rows, :]).astype(BF16)


def _decay_tables(chunk):
    log_g = jnp.log1p(-jnp.exp2(-5.0 - jnp.arange(RET_HEADS, dtype=F32)))
    i = jnp.arange(chunk, dtype=F32)
    diff = i[:, None] - i[None, :]
    dmat = jnp.where(diff >= 0, jnp.exp(log_g[:, None, None] * jnp.maximum(diff, 0.0)), 0.0)
    xi = jnp.exp(log_g[:, None] * (i[None, :] + 1.0))[:, :, None]
    zeta = jnp.exp(log_g[:, None] * (chunk - 1.0 - i[None, :]))[:, :, None]
    g_chunk = jnp.exp(log_g * chunk)[:, None, None]
    return dmat, xi, zeta, g_chunk


def _rope_tables(pos):
    half = RET_DK // 2
    freq = jnp.power(ROPE_BASE, -jnp.arange(half, dtype=F32) / half)
    ang = pos.astype(F32)[:, None] * freq[None, :]
    return jnp.cos(ang), jnp.sin(ang)


def _retention_prompt(z, ret_norm, batch, seq):
    c = RET_CHUNK
    rows = _pick_tile(seq, RET_STEP_CHUNKS * c)
    nc = seq // rows
    dmat, xi, zeta, g_chunk = _decay_tables(c)
    cos, sin = _rope_tables(jnp.arange(seq))
    hb = RET_DK

    def zspec(col0):
        return pl.BlockSpec((rows, hb), lambda b, h, t, col0=col0: (b * nc + t, col0 // hb + h))

    per_head = lambda shape: pl.BlockSpec((1,) + shape, lambda b, h, t: (h, 0, 0))
    return pl.pallas_call(
        _ret_prompt_kernel,
        grid=(batch, RET_HEADS, nc),
        in_specs=[zspec(COL_RQ), zspec(COL_RK), zspec(COL_RV), zspec(COL_RG),
                  pl.BlockSpec((rows, hb // 2), lambda b, h, t: (t, 0)),
                  pl.BlockSpec((rows, hb // 2), lambda b, h, t: (t, 0)),
                  per_head((c, c)), per_head((c, 1)), per_head((c, 1)), per_head((1, 1)),
                  pl.BlockSpec((1, hb), lambda b, h, t: (0, h))],
        out_specs=[pl.BlockSpec((rows, hb), lambda b, h, t: (b * nc + t, h)),
                   pl.BlockSpec((1, 1, RET_DK, RET_DV), lambda b, h, t: (b, h, 0, 0))],
        out_shape=[jax.ShapeDtypeStruct((batch * seq, RET_W), BF16),
                   jax.ShapeDtypeStruct((batch, RET_HEADS, RET_DK, RET_DV), F32)],
        compiler_params=_cparams(3),
        name="retention_prompt",
    )(z, z, z, z, cos, sin, dmat, xi, zeta, g_chunk, ret_norm.reshape(1, RET_W))


def _column_of(row):
    n = row.shape[1]
    eye = _iota((n, n), 0) == _iota((n, n), 1)
    return jnp.sum(jnp.where(eye, jnp.broadcast_to(row, (n, n)), 0.0), axis=1, keepdims=True)


def _ret_sample_kernel(q_ref, k_ref, v_ref, rg_ref, cos_ref, sin_ref, gam_ref, gn_ref, s_ref, a_ref, so_ref):
    cos, sin = cos_ref[...], sin_ref[...]
    q = _rope_rows(q_ref[0], cos, sin)
    k = _rope_rows(k_ref[0], cos, sin) * (RET_DK ** -0.5)
    v = v_ref[0]
    state = s_ref[0, 0]
    gamma = gam_ref[0]
    qk = jnp.sum(q * k, axis=-1, keepdims=True)
    o = qk * v + jnp.sum(_column_of(q) * state, axis=0, keepdims=True) * gamma
    so_ref[0, 0] = state * gamma + _column_of(k) * v
    a_ref[0] = _head_norm_gate(o, gn_ref[...], rg_ref[0]).astype(BF16)


def _retention_sample(z3, state, ret_norm, pos):
    nb = z3.shape[0]
    cos, sin = _rope_tables(jnp.full((1,), pos))
    gamma = jnp.exp(jnp.log1p(-jnp.exp2(-5.0 - jnp.arange(RET_HEADS, dtype=F32))))[:, None, None]
    hb = RET_DK

    def zspec(col0):
        return pl.BlockSpec((1, 1, hb), lambda b, h, col0=col0: (b, 0, col0 // hb + h))

    st_spec = pl.BlockSpec((1, 1, RET_DK, RET_DV), lambda b, h: (b, h, 0, 0))
    return pl.pallas_call(
        _ret_sample_kernel,
        grid=(nb, RET_HEADS),
        in_specs=[zspec(COL_RQ), zspec(COL_RK), zspec(COL_RV), zspec(COL_RG),
                  pl.BlockSpec((1, hb // 2), lambda b, h: (0, 0)),
                  pl.BlockSpec((1, hb // 2), lambda b, h: (0, 0)),
                  pl.BlockSpec((1, 1, 1), lambda b, h: (h, 0, 0)),
                  pl.BlockSpec((1, hb), lambda b, h: (0, h)),
                  st_spec],
        out_specs=[pl.BlockSpec((1, 1, hb), lambda b, h: (b, 0, h)), st_spec],
        out_shape=[jax.ShapeDtypeStruct((nb, 1, RET_W), BF16),
                   jax.ShapeDtypeStruct(state.shape, F32)],
        compiler_params=_cparams(2),
        name="retention_sample",
    )(z3, z3, z3, z3, cos, sin, gamma, ret_norm.reshape(1, RET_W), state)


def _half_rows(ref_slice_fn, n_half):
    return jnp.concatenate([ref_slice_fn(p) for p in range(CMP_STRIDE)], axis=1)


def _cmp_stage1_dense_kernel(x_ref, w_ref, o_ref):
    nh = o_ref.shape[3]
    x = _half_rows(lambda p: x_ref[pl.ds(p, nh, stride=CMP_STRIDE), :], nh).astype(BF16)
    o_ref[0, 0, 0] = _dot(x, w_ref[0])


def _cmp_stage1_dense(z, w1ab, batch, seq):
    nh = seq // CMP_STRIDE
    return pl.pallas_call(
        _cmp_stage1_dense_kernel,
        grid=(batch, 2, NSA_KV_HEADS),
        in_specs=[pl.BlockSpec((seq, NSA_DK), lambda b, kv, h: (b, COL_KVC // NSA_DK + kv * NSA_KV_HEADS + h)),
                  pl.BlockSpec((1, CMP_STRIDE * NSA_DK, 2 * NSA_DK), lambda b, kv, h: (kv, 0, 0))],
        out_specs=pl.BlockSpec((1, 1, 1, nh, 2 * NSA_DK), lambda b, kv, h: (b, kv, h, 0, 0)),
        out_shape=jax.ShapeDtypeStruct((batch, 2, NSA_KV_HEADS, nh, 2 * NSA_DK), F32),
        compiler_params=_cparams(3),
        name="cmp_stage1_dense",
    )(z, w1ab)


PAGES_PER_STEP = 8


def _cmp_stage1_paged_kernel(pt_ref, *refs):
    pages, (w_ref, o_ref) = refs[:PAGES_PER_STEP], refs[PAGES_PER_STEP:]
    hp = PAGE_SIZE // CMP_STRIDE
    top = _iota((2 * NSA_KV_HEADS, NSA_DK), 0) < NSA_KV_HEADS
    cols = [[], []]
    for p in range(CMP_STRIDE):
        tiles = [[], []]
        for pg in pages:
            xp = pg[0, :, p]
            for n in range(0, hp, 2):
                a, b = xp[n], xp[n + 1]
                tiles[0].append(jnp.where(top, a, pltpu.roll(b, NSA_KV_HEADS, 0)))
                tiles[1].append(jnp.where(top, pltpu.roll(a, NSA_KV_HEADS, 0), b))
        for kv in range(2):
            cols[kv].append(jnp.concatenate(tiles[kv], axis=0))
    for kv in range(2):
        x = jnp.concatenate(cols[kv], axis=1).astype(BF16)
        o_ref[0, kv, 0] = _dot(x, w_ref[kv])


def _cmp_stage1_paged(cache, page_table, w1ab):
    nb, n_pages = page_table.shape
    hp = PAGE_SIZE // CMP_STRIDE
    steps = n_pages // PAGES_PER_STEP
    rows = PAGES_PER_STEP * hp * NSA_KV_HEADS

    def page_spec(j):
        return pl.BlockSpec((1, hp, CMP_STRIDE, 2 * NSA_KV_HEADS, NSA_DK),
                            lambda b, s, pt, j=j: (pt[b, s * PAGES_PER_STEP + j], 0, 0, 0, 0))

    wspec = pl.BlockSpec((2, CMP_STRIDE * NSA_DK, 2 * NSA_DK), lambda b, s, pt: (0, 0, 0))
    return pl.pallas_call(
        _cmp_stage1_paged_kernel,
        grid_spec=pltpu.PrefetchScalarGridSpec(
            num_scalar_prefetch=1,
            grid=(nb, steps),
            in_specs=[page_spec(j) for j in range(PAGES_PER_STEP)] + [wspec],
            out_specs=pl.BlockSpec((1, 2, 1, rows, 2 * NSA_DK), lambda b, s, pt: (b, 0, 0, s, 0))),
        out_shape=jax.ShapeDtypeStruct((nb, 2, 1, steps * rows, 2 * NSA_DK), F32),
        compiler_params=_cparams(2),
        name="cmp_stage1_paged",
    )(page_table, *([cache] * PAGES_PER_STEP), w1ab)


def _cmp_stage2_kernel(ac_ref, pos_ref, w1a_ref, w1b_ref, w2_ref, o_ref, *, shift):
    ac = ac_ref[0, 0, 0]
    nh = ac.shape[0]
    pos = pos_ref[0].astype(BF16)
    kw = CMP_STRIDE * NSA_DK
    pe = _dot(pos[:, :kw], w1a_ref[0]) + _dot(pos[:, kw:], w1b_ref[0])
    pre = ac[:, :NSA_DK] + pltpu.roll(ac[:, NSA_DK:], nh - shift, 0) + pe[0:1]
    gelu = 0.5 * pre * (1.0 + jnp.tanh(math.sqrt(2.0 / math.pi) * (pre + 0.044715 * (pre * pre * pre))))
    o_ref[0, 0, 0] = _dot(gelu.astype(BF16), w2_ref[0]).astype(BF16)


def _cmp_stage2(ac, pos8, w1a, w1b, w2, shift):
    nb, _, groups, nh, _ = ac.shape
    kw = CMP_STRIDE * NSA_DK
    return pl.pallas_call(
        functools.partial(_cmp_stage2_kernel, shift=shift),
        grid=(nb, 2, groups),
        in_specs=[pl.BlockSpec((1, 1, 1, nh, 2 * NSA_DK), lambda b, kv, h: (b, kv, h, 0, 0)),
                  pl.BlockSpec((1, 8, 2 * kw), lambda b, kv, h: (kv, 0, 0)),
                  pl.BlockSpec((1, kw, NSA_DK), lambda b, kv, h: (kv, 0, 0)),
                  pl.BlockSpec((1, kw, NSA_DK), lambda b, kv, h: (kv, 0, 0)),
                  pl.BlockSpec((1, NSA_DK, NSA_DK), lambda b, kv, h: (kv, 0, 0))],
        out_specs=pl.BlockSpec((1, 1, 1, nh, NSA_DK), lambda b, kv, h: (b, kv, h, 0, 0)),
        out_shape=jax.ShapeDtypeStruct((nb, 2, groups, nh, NSA_DK), BF16),
        compiler_params=_cparams(3),
        name="cmp_stage2",
    )(ac, pos8, w1a, w1b, w2)


def _rel_bucket(dist):
    n = jnp.maximum(dist, 0)
    nf = jnp.maximum(n, 1).astype(F32)
    scale = (REL_BUCKETS - REL_MAX_EXACT) / math.log(REL_MAX_DIST / REL_MAX_EXACT)
    large = REL_MAX_EXACT + (jnp.log(nf / REL_MAX_EXACT) * scale).astype(jnp.int32)
    large = jnp.minimum(large, REL_BUCKETS - 1)
    return jnp.where(n < REL_MAX_EXACT, n, large)


def _bias_by_dist(rel_table, n):
    tab = rel_table[_rel_bucket(jnp.arange(n))]
    return tab.T.reshape(NSA_KV_HEADS, NSA_GROUP, n)


def _pad_group_rows(t, axis):
    first = lax.slice_in_dim(t, 0, 1, axis=axis)
    return jnp.concatenate([t] + [first] * (SROWS - NSA_GROUP), axis=axis)


def _flash_init(m_ref, l_ref, acc_ref):
    m_ref[...] = jnp.full(m_ref.shape, NEG_INF, F32)
    l_ref[...] = jnp.zeros(l_ref.shape, F32)
    acc_ref[...] = jnp.zeros(acc_ref.shape, F32)


def _flash_step(s, v, m_ref, l_ref, acc_ref):
    m_old = m_ref[...]
    m_new = jnp.maximum(m_old, jnp.max(s, axis=1, keepdims=True))
    alpha = jnp.exp(m_old - m_new)
    p = jnp.exp(s - m_new)
    l_ref[...] = alpha * l_ref[...] + jnp.sum(p, axis=1, keepdims=True)
    acc_ref[...] = alpha * acc_ref[...] + _dot(p.astype(BF16), v)
    m_ref[...] = m_new


def _flash_result(l_ref, acc_ref):
    return acc_ref[...] / jnp.maximum(l_ref[...], 1e-30)


def _select_blocks(imp_t, q0, ns):
    shape = imp_t.shape
    blk = _iota(shape, 0)
    qpos = q0 + _iota(shape, 1)
    cur = qpos >> 6
    valid = blk * SEL_BLOCK <= qpos
    forced = (blk == 0) | (blk == cur) | (blk == cur - 1)
    imp_t = jnp.where(valid, jnp.where(forced, FORCE_SCORE, imp_t), NEG_INF)
    rank = jnp.zeros(shape, F32)
    for other in range(ns):
        row = imp_t[other:other + 1, :]
        ahead = (row > imp_t) | ((row == imp_t) & (blk > other))
        rank = rank + jnp.where(ahead, 1.0, 0.0)
    return jnp.where((rank < SEL_TOPK) & valid, 1.0, 0.0)


def _nsa_prompt_kernel(q_ref, ks_ref, vs_ref, kw_ref, vw_ref, kc_ref, vct_ref, tz_ref, cfar_ref, basec_ref,
                       ovt_ref, ng_ref, nsl_ref, o_ref, m_ref, l_ref, acc_ref, vst_ref, vwt_ref, *, ns):
    i = pl.program_id(2)
    q0 = i * TQ
    qall = q_ref[...] * ((NSA_DK ** -0.5) * LOG2E)
    qt = jnp.concatenate([qall[:, g * NSA_DK:(g + 1) * NSA_DK].T for g in range(NSA_GROUP)], axis=1).astype(BF16)
    c_loc = _iota((TK, NSA_ROWS), 0)
    r_loc = _iota((TK, NSA_ROWS), 1) & (TQ - 1)

    @pl.when(i == 0)
    def _():
        for kt in range(vst_ref.shape[0]):
            vst_ref[kt] = vs_ref[kt * TK:(kt + 1) * TK, :].T.astype(BF16)
            vwt_ref[kt] = vw_ref[kt * TK:(kt + 1) * TK, :].T.astype(BF16)

    ncp = kc_ref.shape[3]
    shift = (TQ // CMP_STRIDE) * i
    bias_c = basec_ref[0, pl.ds(pl.multiple_of(ncp - shift, TQ // CMP_STRIDE), ncp), :]
    s = _dot(kc_ref[0, 0, 0], qt) + bias_c
    m = jnp.max(s, axis=0, keepdims=True)
    e = jnp.exp2(s - m)
    inv = jnp.where(m > 0.5 * NEG_INF, 1.0 / jnp.maximum(jnp.sum(e, axis=0, keepdims=True), 1e-30), 0.0)
    p = e * inv
    o_cmp = _dot(vct_ref[0, 0], p.astype(BF16))

    psum = p[:, 0:TQ] + p[:, TQ:2 * TQ] + p[:, 2 * TQ:3 * TQ] + p[:, 3 * TQ:4 * TQ]
    hi = psum.astype(BF16)
    lo = (psum - hi.astype(F32)).astype(BF16)
    ovt = ovt_ref[...]
    imp_t = _dot(ovt, hi) + _dot(ovt, lo)
    ns8 = -(-ns // 8) * 8
    sel_t = _select_blocks(imp_t[:ns8], q0, ns)
    nsp = ovt.shape[0]
    sel_neg = jnp.where(sel_t > 0.5, 0.0, NEG_INF)
    if ns8 < nsp:
        sel_neg = jnp.concatenate([sel_neg, jnp.full((nsp - ns8, TQ), NEG_INF, F32)], axis=0)
    sel_neg = sel_neg.astype(BF16)

    def flash_step(scs, vts):
        m_old = m_ref[...]
        m_new = m_old
        for sc in scs:
            m_new = jnp.maximum(m_new, jnp.max(sc, axis=0, keepdims=True))
        alpha = jnp.exp2(m_old - m_new)
        l_new = alpha * l_ref[...]
        acc = alpha * acc_ref[...]
        for sc, vt in zip(scs, vts):
            pt = jnp.exp2(sc - m_new)
            l_new = l_new + jnp.sum(pt, axis=0, keepdims=True)
            acc = acc + _dot(vt, pt.astype(BF16))
        l_ref[...] = l_new
        acc_ref[...] = acc
        m_ref[...] = m_new

    def sel_scores(kt, bias, causal):
        k = ks_ref[pl.ds(pl.multiple_of(kt * TK, TK), TK), :].astype(BF16)
        blk_of_key = (TK // SEL_BLOCK) * kt + (_iota((TK, nsp), 0) >> 6)
        expand = jnp.where(_iota((TK, nsp), 1) == blk_of_key, 1.0, 0.0).astype(BF16)
        mk = _dot(expand, sel_neg)
        sc = _dot(k, qt) + bias + jnp.concatenate([mk] * NSA_GROUP, axis=1)
        if causal:
            sc = jnp.where(c_loc <= r_loc, sc, NEG_INF)
        return sc

    _flash_init(m_ref, l_ref, acc_ref)
    flash_step([sel_scores(i, tz_ref[0, 0], True)], [vst_ref[i]])

    @pl.when(i >= 1)
    def _():
        flash_step([sel_scores(i - 1, tz_ref[0, 1], False)], [vst_ref[i - 1]])

    def far_pair(j, carry):
        kt = i - 2 - 2 * j
        flash_step([sel_scores(kt, cfar_ref[0], False), sel_scores(kt - 1, cfar_ref[0], False)],
                   [vst_ref[kt], vst_ref[kt - 1]])
        return carry

    n_far = jnp.maximum(i - 1, 0)
    lax.fori_loop(0, n_far >> 1, far_pair, 0)

    @pl.when((n_far & 1) == 1)
    def _():
        flash_step([sel_scores(0, cfar_ref[0], False)], [vst_ref[0]])

    o_sel = _flash_result(l_ref, acc_ref)

    def win_tile(off):
        kt = i - off
        k = kw_ref[pl.ds(pl.multiple_of(kt * TK, TK), TK), :].astype(BF16)
        sc = _dot(k, qt) + (tz_ref[0, off] if off < 2 else cfar_ref[0])
        if off == 0:
            sc = jnp.where(c_loc <= r_loc, sc, NEG_INF)
        if off * TK == WINDOW:
            sc = jnp.where(c_loc > r_loc, sc, NEG_INF)
        flash_step([sc], [vwt_ref[kt]])

    _flash_init(m_ref, l_ref, acc_ref)
    win_tile(0)
    for off in range(1, WINDOW // TK + 1):
        pl.when(i >= off)(functools.partial(win_tile, off))
    o_win = _flash_result(l_ref, acc_ref)

    gates_t = _sigmoid(ng_ref[...]).T
    nsl = nsl_ref[...]
    outs = []
    for g in range(NSA_GROUP):
        cols = slice(g * TQ, (g + 1) * TQ)
        o_t = (gates_t[3 * g:3 * g + 1] * o_cmp[:, cols] + gates_t[3 * g + 1:3 * g + 2] * o_sel[:, cols]
               + gates_t[3 * g + 2:3 * g + 3] * o_win[:, cols])
        x = nsl[:, g * NSA_DK:(g + 1) * NSA_DK]
        outs.append(o_t.T * (x * _sigmoid(x)))
    o_ref[...] = jnp.concatenate(outs, axis=1).astype(BF16)


def _lanes_by_head(t):
    hk, g, rows, tq = t.shape
    return t.transpose(0, 2, 1, 3).reshape(hk, rows, g * tq)


def _nsa_prompt(z, zb, kcvc, bias_d, batch, seq):
    nq = seq // TQ
    ns = seq // SEL_BLOCK
    ncp = kcvc.shape[3]
    nsp = LANE
    wn = TQ // CMP_STRIDE
    assert TQ == TK and ns <= nsp and ncp >= seq // CMP_STRIDE and ncp % LANE == 0 and ncp > wn
    gw = NSA_GROUP * NSA_DK
    n_dist = bias_d.shape[-1]
    assert n_dist >= 2 * TK + TQ
    bias_d = bias_d * LOG2E

    rep = jnp.tile(bias_d, (1, 1, TK + 1))[..., :TK * (n_dist - 1)].reshape(NSA_KV_HEADS, NSA_GROUP, TK, n_dist - 1)
    tz = jnp.stack([_lanes_by_head(rep[..., 0:TQ]), _lanes_by_head(rep[..., TK:TK + TQ])], axis=1)
    far = jnp.broadcast_to(bias_d[..., REL_MAX_DIST][:, :, None, None], (NSA_KV_HEADS, NSA_GROUP, 1, TQ))
    cfar = _lanes_by_head(far)
    half = n_dist // 2
    start = CMP_STRIDE * wn - (CMP_BLOCK - 1)
    assert start + TQ <= half and 2 * wn * CMP_STRIDE - start <= half
    w_ext = jnp.concatenate([bias_d[..., :half], jnp.full(bias_d.shape[:-1] + (n_dist - half,), NEG_INF, F32)], -1)
    near = jnp.tile(w_ext, (1, 1, 2 * wn + 1))[..., :2 * wn * (n_dist - CMP_STRIDE)]
    near = near.reshape(NSA_KV_HEADS, NSA_GROUP, 2 * wn, n_dist - CMP_STRIDE)[..., start:start + TQ]
    basec = jnp.concatenate([jnp.broadcast_to(far, (NSA_KV_HEADS, NSA_GROUP, ncp - wn, TQ)), near,
                             jnp.full((NSA_KV_HEADS, NSA_GROUP, ncp - wn, TQ), NEG_INF, F32)], axis=2)
    basec = _lanes_by_head(basec)
    sblk = jnp.arange(nsp)[:, None]
    nblk = jnp.arange(ncp)[None, :]
    ovt = ((nblk >= 4 * sblk - 1) & (nblk <= 4 * sblk + 3)).astype(BF16)

    vct = kcvc[:, 1].transpose(0, 1, 3, 2)

    def kvspec(col0, which):
        return pl.BlockSpec((seq, NSA_DK), lambda b, h, i: (b, col0 // NSA_DK + which * NSA_KV_HEADS + h))

    vt_scratch = pltpu.VMEM((seq // TK, NSA_DK, TK), BF16)
    return pl.pallas_call(
        functools.partial(_nsa_prompt_kernel, ns=ns),
        grid=(batch, NSA_KV_HEADS, nq),
        in_specs=[pl.BlockSpec((TQ, gw), lambda b, h, i: (b * nq + i, COL_NQ // gw + h)),
                  kvspec(COL_KVS, 0), kvspec(COL_KVS, 1), kvspec(COL_KVW, 0), kvspec(COL_KVW, 1),
                  pl.BlockSpec((1, 1, 1, ncp, NSA_DK), lambda b, h, i: (b, 0, h, 0, 0)),
                  pl.BlockSpec((1, 1, NSA_DK, ncp), lambda b, h, i: (b, h, 0, 0)),
                  pl.BlockSpec((1, 2, TK, NSA_ROWS), lambda b, h, i: (h, 0, 0, 0)),
                  pl.BlockSpec((1, 1, NSA_ROWS), lambda b, h, i: (h, 0, 0)),
                  pl.BlockSpec((1, 2 * ncp, NSA_ROWS), lambda b, h, i: (h, 0, 0)),
                  pl.BlockSpec((nsp, ncp), lambda b, h, i: (0, 0)),
                  pl.BlockSpec((TQ, LANE), lambda b, h, i: (b * nq + i, COL_NG // LANE + h)),
                  pl.BlockSpec((TQ, gw), lambda b, h, i: (b * nq + i, COL_NSL // gw + h))],
        out_specs=pl.BlockSpec((TQ, gw), lambda b, h, i: (b * nq + i, h)),
        out_shape=jax.ShapeDtypeStruct((batch * seq, NSA_W), BF16),
        scratch_shapes=[pltpu.VMEM((1, NSA_ROWS), F32), pltpu.VMEM((1, NSA_ROWS), F32),
                        pltpu.VMEM((NSA_DK, NSA_ROWS), F32), vt_scratch, vt_scratch],
        compiler_params=_cparams(3),
        name="nsa_prompt",
    )(z, z, z, z, z, kcvc, vct, tz, cfar, basec, ovt, zb, zb)


SROWS = 8


def _stack_group_q(q_row):
    heads = [q_row[:, g * NSA_DK:(g + 1) * NSA_DK] for g in range(NSA_GROUP)]
    return jnp.concatenate(heads + [heads[0]] * (SROWS - NSA_GROUP), axis=0)


def _nsa_sample_cmp_kernel(q_ref, kc_ref, vc_ref, bias_ref, ov_ref, o_ref, idx_ref, *, ns):
    scale = NSA_DK ** -0.5
    qs = _stack_group_q(q_ref[0]).astype(BF16)
    s = _nt(qs, kc_ref[0, 0]) * scale + bias_ref[0]
    m = jnp.max(s, axis=1, keepdims=True)
    e = jnp.exp(s - m)
    inv = jnp.where(m > 0.5 * NEG_INF, 1.0 / jnp.maximum(jnp.sum(e, axis=1, keepdims=True), 1e-30), 0.0)
    p = e * inv
    o_ref[0, 0] = _dot(p.astype(BF16), vc_ref[0, 0])
    psum = jnp.broadcast_to(jnp.sum(p[0:NSA_GROUP], axis=0, keepdims=True), p.shape)
    hi = psum.astype(BF16)
    lo = (psum - hi.astype(F32)).astype(BF16)
    imp = (_dot(hi, ov_ref[...]) + _dot(lo, ov_ref[...]))[0:1]
    nsp = imp.shape[1]
    blk_r = _iota((1, nsp), 1)
    cur = ns - 1
    forced = (blk_r == 0) | (blk_r == cur) | (blk_r == cur - 1)
    imp = jnp.where(blk_r < ns, jnp.where(forced, FORCE_SCORE, imp), 2.0 * NEG_INF)
    imp_c = _column_of(imp)
    i_r = _iota((nsp, nsp), 1)
    j_c = _iota((nsp, nsp), 0)
    ahead = (imp > imp_c) | ((imp == imp_c) & (i_r < j_c))
    rank_c = jnp.sum(jnp.where(ahead, 1.0, 0.0), axis=1, keepdims=True)
    slot = _iota((nsp, LANE), 1).astype(F32)
    picks = jnp.where(rank_c == slot, _iota((nsp, LANE), 0).astype(F32), 0.0)
    idx_ref[0, 0] = jnp.broadcast_to(jnp.sum(picks, axis=0, keepdims=True), (SROWS, LANE)).astype(jnp.int32)


def _nsa_sample_cmp(z3, kcvc, bias_c, ov, ns):
    nb = z3.shape[0]
    ncp = kcvc.shape[2]
    nsp = ov.shape[1]
    gw = NSA_GROUP * NSA_DK

    def cspec(which):
        return pl.BlockSpec((1, 1, ncp, NSA_DK), lambda b, h: (b, which, 0, h))

    return pl.pallas_call(
        functools.partial(_nsa_sample_cmp_kernel, ns=ns),
        grid=(nb, NSA_KV_HEADS),
        in_specs=[pl.BlockSpec((1, 1, gw), lambda b, h: (b, 0, COL_NQ // gw + h)),
                  cspec(0), cspec(1),
                  pl.BlockSpec((1, SROWS, ncp), lambda b, h: (h, 0, 0)),
                  pl.BlockSpec((ncp, nsp), lambda b, h: (0, 0))],
        out_specs=[pl.BlockSpec((1, 1, SROWS, NSA_DK), lambda b, h: (b, h, 0, 0)),
                   pl.BlockSpec((1, 1, SROWS, LANE), lambda b, h: (b, h, 0, 0))],
        out_shape=[jax.ShapeDtypeStruct((nb, NSA_KV_HEADS, SROWS, NSA_DK), F32),
                   jax.ShapeDtypeStruct((nb, NSA_KV_HEADS, SROWS, LANE), jnp.int32)],
        compiler_params=_cparams(2),
        name="nsa_sample_cmp",
    )(z3, kcvc, kcvc, bias_c, ov)


def _nsa_sample_sel_kernel(pt_ref, idx_ref, q_ref, *refs, ns):
    blocks, (new_ref,), biases = refs[:NSA_KV_HEADS], refs[NSA_KV_HEADS:NSA_KV_HEADS + 1], refs[NSA_KV_HEADS + 1:2 * NSA_KV_HEADS + 1]
    o_ref, m_ref, l_ref, acc_ref = refs[2 * NSA_KV_HEADS + 1:]
    b, t = pl.program_id(0), pl.program_id(1)
    gw = NSA_GROUP * NSA_DK
    rows_kv = 2 * NSA_KV_HEADS

    @pl.when(t == 0)
    def _():
        _flash_init(m_ref, l_ref, acc_ref)

    new_rows = jnp.concatenate([new_ref[0]] * SEL_BLOCK, axis=0)
    xs, scs = [], []
    for h in range(NSA_KV_HEADS):
        is_new = idx_ref[b, h, t] == ns - 1
        x = blocks[h][0].reshape(SEL_BLOCK * rows_kv, NSA_DK)
        x = jnp.where(is_new, new_rows, x).astype(BF16)
        qs = _stack_group_q(q_ref[0][:, h * gw:(h + 1) * gw]).astype(BF16)
        xs.append(x)
        scs.append(_nt(qs, x) * (NSA_DK ** -0.5) + biases[h][0, 0])
    sc = jnp.concatenate(scs, axis=0)
    m_old = m_ref[...]
    m_new = jnp.maximum(m_old, jnp.max(sc, axis=1, keepdims=True))
    alpha = jnp.exp(m_old - m_new)
    p = jnp.exp(sc - m_new)
    l_ref[...] = alpha * l_ref[...] + jnp.sum(p, axis=1, keepdims=True)
    pv = pltpu.roll(p, NSA_KV_HEADS, 1).astype(BF16)
    acc_ref[...] = alpha * acc_ref[...] + jnp.concatenate(
        [_dot(pv[h * SROWS:(h + 1) * SROWS], xs[h]) for h in range(NSA_KV_HEADS)], axis=0)
    m_ref[...] = m_new

    @pl.when(t == pl.num_programs(1) - 1)
    def _():
        o_ref[0] = (acc_ref[...] / jnp.maximum(l_ref[...], 1e-30)).reshape(NSA_KV_HEADS, SROWS, NSA_DK)


def _nsa_sample_sel(z3, cache, kv_new, page_table, idx, bias_sel, ns):
    nb, n_pages = page_table.shape
    n_sel = idx.shape[2]
    halves = PAGE_SIZE // SEL_BLOCK
    rows_kv = 2 * NSA_KV_HEADS

    def blockspec(h):
        def index(b, t, pt, ix):
            blk = ix[b, h, t]
            return (pt[b, jnp.minimum(blk // halves, n_pages - 1)], blk % halves, 0, 0)
        return pl.BlockSpec((1, SEL_BLOCK, rows_kv, NSA_DK), index)

    def biasspec(h):
        return pl.BlockSpec((1, 1, SROWS, SEL_BLOCK * rows_kv), lambda b, t, pt, ix: (h, ix[b, h, t], 0, 0))

    heads = range(NSA_KV_HEADS)
    return pl.pallas_call(
        functools.partial(_nsa_sample_sel_kernel, ns=ns),
        grid_spec=pltpu.PrefetchScalarGridSpec(
            num_scalar_prefetch=2,
            grid=(nb, n_sel),
            in_specs=[pl.BlockSpec((1, 1, NSA_W), lambda b, t, pt, ix: (b, 0, COL_NQ // NSA_W))]
                     + [blockspec(h) for h in heads]
                     + [pl.BlockSpec((1, rows_kv, NSA_DK), lambda b, t, pt, ix: (b, 0, 0))]
                     + [biasspec(h) for h in heads],
            out_specs=pl.BlockSpec((1, NSA_KV_HEADS, SROWS, NSA_DK), lambda b, t, pt, ix: (b, 0, 0, 0)),
            scratch_shapes=[pltpu.VMEM((NSA_KV_HEADS * SROWS, 1), F32), pltpu.VMEM((NSA_KV_HEADS * SROWS, 1), F32),
                            pltpu.VMEM((NSA_KV_HEADS * SROWS, NSA_DK), F32)]),
        out_shape=jax.ShapeDtypeStruct((nb, NSA_KV_HEADS, SROWS, NSA_DK), F32),
        compiler_params=_cparams(2),
        name="nsa_sample_sel",
    )(page_table, idx, z3, *([cache] * NSA_KV_HEADS), kv_new, *([bias_sel] * NSA_KV_HEADS))


def _nsa_sample_win_kernel(q_ref, k_ref, v_ref, kn_ref, vn_ref, bias_ref, bnew_ref, o_ref):
    scale = NSA_DK ** -0.5
    q = _stack_group_q(q_ref[0])
    s_buf = _nt(q.astype(BF16), k_ref[0].astype(BF16)) * scale + bias_ref[0]
    s_new = jnp.sum(q * kn_ref[0], axis=1, keepdims=True) * scale + bnew_ref[0]
    m = jnp.maximum(jnp.max(s_buf, axis=1, keepdims=True), s_new)
    p_buf = jnp.exp(s_buf - m)
    p_new = jnp.exp(s_new - m)
    l = jnp.sum(p_buf, axis=1, keepdims=True) + p_new
    acc = _dot(p_buf.astype(BF16), v_ref[0].astype(BF16)) + p_new * vn_ref[0]
    o_ref[0, 0] = acc / jnp.maximum(l, 1e-30)


def _nsa_sample_win(z3, win_buf, bias_win, bias_new):
    nb, nbuf, _ = win_buf.shape
    gw = NSA_GROUP * NSA_DK

    def bufspec(which):
        return pl.BlockSpec((1, nbuf, NSA_DK), lambda b, h: (b, 0, which * NSA_KV_HEADS + h))

    def newspec(which):
        return pl.BlockSpec((1, 1, NSA_DK), lambda b, h: (b, 0, COL_KVW // NSA_DK + which * NSA_KV_HEADS + h))

    return pl.pallas_call(
        _nsa_sample_win_kernel,
        grid=(nb, NSA_KV_HEADS),
        in_specs=[pl.BlockSpec((1, 1, gw), lambda b, h: (b, 0, COL_NQ // gw + h)),
                  bufspec(0), bufspec(1), newspec(0), newspec(1),
                  pl.BlockSpec((1, SROWS, nbuf), lambda b, h: (h, 0, 0)),
                  pl.BlockSpec((1, SROWS, 1), lambda b, h: (h, 0, 0))],
        out_specs=pl.BlockSpec((1, 1, SROWS, NSA_DK), lambda b, h: (b, h, 0, 0)),
        out_shape=jax.ShapeDtypeStruct((nb, NSA_KV_HEADS, SROWS, NSA_DK), F32),
        compiler_params=_cparams(2),
        name="nsa_sample_win",
    )(z3, win_buf, win_buf, z3, z3, bias_win, bias_new)


def _nsa_sample_gate_kernel(oc_ref, os_ref, ow_ref, ng_ref, nsl_ref, o_ref):
    gates = _sigmoid(ng_ref[0])
    nsl = nsl_ref[0]
    outs = []
    for g in range(NSA_GROUP):
        o = (gates[:, 3 * g:3 * g + 1] * oc_ref[0, 0, g:g + 1] + gates[:, 3 * g + 1:3 * g + 2] * os_ref[0, 0, g:g + 1]
             + gates[:, 3 * g + 2:3 * g + 3] * ow_ref[0, 0, g:g + 1])
        x = nsl[:, g * NSA_DK:(g + 1) * NSA_DK]
        outs.append(o * (x * _sigmoid(x)))
    o_ref[0] = jnp.concatenate(outs, axis=1).astype(BF16)


def _nsa_sample_gate(o_cmp, o_sel, o_win, z3b):
    nb = z3b.shape[0]
    gw = NSA_GROUP * NSA_DK
    ospec = pl.BlockSpec((1, 1, SROWS, NSA_DK), lambda b, h: (b, h, 0, 0))
    return pl.pallas_call(
        _nsa_sample_gate_kernel,
        grid=(nb, NSA_KV_HEADS),
        in_specs=[ospec, ospec, ospec,
                  pl.BlockSpec((1, 1, LANE), lambda b, h: (b, 0, COL_NG // LANE + h)),
                  pl.BlockSpec((1, 1, gw), lambda b, h: (b, 0, COL_NSL // gw + h))],
        out_specs=pl.BlockSpec((1, 1, gw), lambda b, h: (b, 0, h)),
        out_shape=jax.ShapeDtypeStruct((nb, 1, NSA_W), BF16),
        compiler_params=_cparams(2),
        name="nsa_sample_gate",
    )(o_cmp, o_sel, o_win, z3b, z3b)


def _mem_heads(q, kv):
    outs = []
    for h in range(MEM_HEADS):
        k = kv[:, h * MEM_DH:(h + 1) * MEM_DH].astype(BF16)
        v = kv[:, MEM_W + h * MEM_DH:MEM_W + (h + 1) * MEM_DH].astype(BF16)
        s = _nt(q[:, h * MEM_DH:(h + 1) * MEM_DH].astype(BF16), k) * (MEM_DH ** -0.5)
        e = jnp.exp(s - jnp.max(s, axis=1, keepdims=True))
        p = e / jnp.sum(e, axis=1, keepdims=True)
        outs.append(_dot(p.astype(BF16), v))
    return jnp.concatenate(outs, axis=1)


def _mem_prompt_kernel(q_ref, kv_ref, o_ref):
    o_ref[...] = _mem_heads(q_ref[...], kv_ref[...]).astype(BF16)


def _mem_prompt(z, mem_kv, batch, seq, tq):
    nq = seq // tq
    n_mem = mem_kv.shape[0] // batch
    return pl.pallas_call(
        _mem_prompt_kernel,
        grid=(batch, nq),
        in_specs=[pl.BlockSpec((tq, MEM_W), lambda b, i: (b * nq + i, COL_MQ // MEM_W)),
                  pl.BlockSpec((n_mem, 2 * MEM_W), lambda b, i: (b, 0))],
        out_specs=pl.BlockSpec((tq, MEM_W), lambda b, i: (b * nq + i, 0)),
        out_shape=jax.ShapeDtypeStruct((batch * seq, MEM_W), BF16),
        compiler_params=_cparams(2),
        name="mem_prompt",
    )(z, mem_kv)


def _mem_sample_kernel(q_ref, kv_ref, o_ref):
    q = jnp.broadcast_to(q_ref[0], (SROWS, MEM_W))
    o_ref[0] = _mem_heads(q, kv_ref[0])[0:1].astype(BF16)


def _mem_sample(z3, mem_kv):
    nb, n_mem, _ = mem_kv.shape
    return pl.pallas_call(
        _mem_sample_kernel,
        grid=(nb,),
        in_specs=[pl.BlockSpec((1, 1, MEM_W), lambda b: (b, 0, COL_MQ // MEM_W)),
                  pl.BlockSpec((1, n_mem, 2 * MEM_W), lambda b: (b, 0, 0))],
        out_specs=pl.BlockSpec((1, 1, MEM_W), lambda b: (b, 0, 0)),
        out_shape=jax.ShapeDtypeStruct((nb, 1, MEM_W), BF16),
        compiler_params=_cparams(1),
        name="mem_sample",
    )(z3, mem_kv)


def _merge_kernel(ar_ref, an_ref, am_ref, wr_ref, wn_ref, wm_ref, g0_ref, g1_ref, g2_ref, o_ref):
    merged = (_sigmoid(g0_ref[...]) * _dot(ar_ref[...], wr_ref[...])
              + _sigmoid(g1_ref[...]) * _dot(an_ref[...], wn_ref[...])
              + _sigmoid(g2_ref[...]) * _dot(am_ref[...], wm_ref[...]))
    o_ref[...] = merged.astype(BF16)


def _merge(a_ret, a_nsa, a_mem, w_ret, w_nsa, w_mem, z, tm, tn):
    m = a_ret.shape[0]
    nt = D_MODEL // tn

    def aspec(width):
        return pl.BlockSpec((tm, width), lambda i, j: (i, 0))

    def wspec(width):
        return pl.BlockSpec((width, tn), lambda i, j: (0, j))

    def gspec(branch):
        return pl.BlockSpec((tm, tn), lambda i, j: (i, COL_MG // tn + branch * nt + j))

    return pl.pallas_call(
        _merge_kernel,
        grid=(m // tm, nt),
        in_specs=[aspec(RET_W), aspec(NSA_W), aspec(MEM_W), wspec(RET_W), wspec(NSA_W), wspec(MEM_W),
                  gspec(0), gspec(1), gspec(2)],
        out_specs=pl.BlockSpec((tm, tn), lambda i, j: (i, j)),
        out_shape=jax.ShapeDtypeStruct((m, D_MODEL), BF16),
        compiler_params=_cparams(2),
        name="merge",
    )(a_ret, a_nsa, a_mem, w_ret, w_nsa, w_mem, z, z, z)


def _out_kernel(a_ref, w_ref, x_ref, g_ref, o_ref):
    out = _dot(a_ref[...], w_ref[...])
    y = out * lax.rsqrt(jnp.mean(out * out, axis=-1, keepdims=True) + EPS)
    o_ref[...] = x_ref[...] + y * g_ref[...]


def _out_proj(merged, w_out, x, norm_post, tm):
    m = merged.shape[0]
    return pl.pallas_call(
        _out_kernel,
        grid=(m // tm,),
        in_specs=[pl.BlockSpec((tm, D_MODEL), lambda i: (i, 0)),
                  pl.BlockSpec((D_MODEL, D_MODEL), lambda i: (0, 0)),
                  pl.BlockSpec((tm, D_MODEL), lambda i: (i, 0)),
                  pl.BlockSpec((1, D_MODEL), lambda i: (0, 0))],
        out_specs=pl.BlockSpec((tm, D_MODEL), lambda i: (i, 0)),
        out_shape=jax.ShapeDtypeStruct((m, D_MODEL), F32),
        compiler_params=_cparams(1),
        name="out_proj",
    )(merged, w_out, x, norm_post.reshape(1, D_MODEL))


def _layout_w_tail(w_in):
    splits = (N_BRANCHES * NSA_HEADS, NSA_W, MEM_W, N_BRANCHES * D_MODEL)
    pieces, start = [], PROJ_A
    for w in splits:
        pieces.append(w_in[:, start:start + w].astype(BF16))
        start += w
    assert start == w_in.shape[1]
    ng, nsl, mq, mg = pieces
    per_group = N_BRANCHES * NSA_GROUP
    ng = ng.reshape(D_MODEL, NSA_KV_HEADS, per_group)
    ng = jnp.pad(ng, ((0, 0), (0, 0), (0, LANE - per_group))).reshape(D_MODEL, NG_SLOT)
    return jnp.concatenate([mq, nsl, mg, ng], axis=1)


def _pick_tile(m, cap):
    t = min(m, cap)
    while m % t:
        t //= 2
    return t


def kernel(x_prompt, x_sample, cache_cmp_kv, cache_sel_kv, cache_win_kv, state_ret, cache_mem_kv, page_table,
           mem_prompt, rel_table, norm_pre, norm_post, norm_mem, w_in, ret_norm, w_ret_up, cmp_pos, w_cmp1,
           w_cmp2, w_nsa_up, w_mem_kv, w_mem_up, w_out):
    batch, seq, _ = x_prompt.shape
    nb = x_sample.shape[0]
    assert x_sample.shape[1] == 1 and norm_pre.shape[0] == 1
    assert seq % TQ == 0 and seq >= WINDOW
    n_pool = cache_cmp_kv.shape[1]
    n_pages = page_table.shape[1]
    past = n_pages * PAGE_SIZE
    n_mem = mem_prompt.shape[1]
    assert n_pages % PAGES_PER_STEP == 0 and cache_win_kv.shape[2] == WINDOW

    w_a = w_in[0]
    w_b = _layout_w_tail(w_in[0])
    kw = CMP_STRIDE * NSA_DK
    w1 = w_cmp1[0].reshape(2, CMP_BLOCK * NSA_DK, NSA_DK).astype(BF16)
    w1a, w1b = w1[:, :kw], w1[:, kw:]
    w1ab = jnp.concatenate([w1a, w1b], axis=2)
    w2 = w_cmp2[0].astype(BF16)
    pos8 = jnp.pad(cmp_pos[0].reshape(2, 1, CMP_BLOCK * NSA_DK), ((0, 0), (0, 7), (0, 0)))
    w_ret = w_ret_up[0].astype(BF16)
    w_nsa = w_nsa_up[0].astype(BF16)
    w_mem = w_mem_up[0].astype(BF16)
    w_o = w_out[0].astype(BF16)
    w_mkv = w_mem_kv[0].astype(BF16)

    m_p = batch * seq
    xp = x_prompt.reshape(m_p, D_MODEL)
    z = _norm_matmul(xp, norm_pre[0], w_a, _pick_tile(m_p, 1024), 512, PROJ_A)
    zb = _norm_matmul(xp, norm_pre[0], w_b, _pick_tile(m_p, 1024), 512)

    a_ret, ret_state_p = _retention_prompt(z, ret_norm[0], batch, seq)

    ncp = max(LANE, -(-(seq // CMP_STRIDE) // LANE) * LANE)
    ac = _cmp_stage1_dense(z, w1ab, batch, seq)
    kcvc = _cmp_stage2(ac, pos8, w1a, w1b, w2, 1)
    if ncp > kcvc.shape[3]:
        kcvc = jnp.pad(kcvc, ((0, 0), (0, 0), (0, 0), (0, ncp - kcvc.shape[3]), (0, 0)))
    bias_d = _bias_by_dist(rel_table, BIAS_DISTS)
    a_nsa = _nsa_prompt(z, zb, kcvc, bias_d, batch, seq)

    mem_kv_p = _norm_matmul(mem_prompt.reshape(batch * n_mem, D_MODEL), norm_mem[0], w_mkv,
                            _pick_tile(batch * n_mem, 512), 512)
    a_mem = _mem_prompt(zb, mem_kv_p, batch, seq, _pick_tile(seq, 512))

    merged = _merge(a_ret, a_nsa, a_mem, w_ret, w_nsa, w_mem, zb, _pick_tile(m_p, 512), 512)
    y_p = _out_proj(merged, w_o, xp, norm_post[0], _pick_tile(m_p, 256)).reshape(batch, seq, D_MODEL)

    kv_shape = (1, batch, seq, 2, NSA_KV_HEADS, NSA_DK)
    new_cmp_p = z[:, COL_KVC:COL_KVC + 2 * KV_W].reshape(kv_shape)
    new_sel_p = z[:, COL_KVS:COL_KVS + 2 * KV_W].reshape(kv_shape)
    new_win_p = z[:, COL_KVW:COL_KVW + 2 * KV_W].reshape(kv_shape)[:, :, seq - WINDOW:]
    new_ret_p = ret_state_p[None]
    new_mem_p = mem_kv_p.reshape(1, batch, n_mem, 2, MEM_HEADS, MEM_DH)

    xs = x_sample.reshape(nb, D_MODEL)
    zs = _norm_matmul(xs, norm_pre[0], w_a, nb, 512, PROJ_A)
    zsb = _norm_matmul(xs, norm_pre[0], w_b, nb, 512)
    z3 = zs.reshape(nb, 1, PROJ_A)
    z3b = zsb.reshape(nb, 1, PROJ_B)

    a_ret_s, ret_state_s = _retention_sample(z3, state_ret[0], ret_norm[0], past)

    cache_c = cache_cmp_kv[0].reshape(n_pool, PAGE_SIZE // CMP_STRIDE, CMP_STRIDE, 2 * NSA_KV_HEADS, NSA_DK)
    cache_s = cache_sel_kv[0].reshape(n_pool, PAGE_SIZE, 2 * NSA_KV_HEADS, NSA_DK)
    ac_s = _cmp_stage1_paged(cache_c, page_table, w1ab)
    kcvc_s = _cmp_stage2(ac_s, pos8, w1a, w1b, w2, NSA_KV_HEADS)
    ncs = past // CMP_STRIDE
    kcvc_s = kcvc_s.reshape(nb, 2, ncs, KV_W)
    ns_s = past // SEL_BLOCK + 1
    nsp_s = -(-ns_s // LANE) * LANE
    assert past >= WINDOW and past >= REL_MAX_DIST and BIAS_DISTS > WINDOW
    far_s = bias_d[..., REL_MAX_DIST:REL_MAX_DIST + 1]
    hg = (NSA_KV_HEADS, NSA_GROUP)
    n_valid = (past - (CMP_BLOCK - 1)) // CMP_STRIDE + 1
    strided = bias_d[..., (past - (CMP_BLOCK - 1)) % CMP_STRIDE::CMP_STRIDE]
    n_tab = strided.shape[-1]
    assert n_valid >= n_tab and n_tab * CMP_STRIDE > REL_MAX_DIST + CMP_STRIDE and ncs >= n_valid
    bias_cs = jnp.concatenate([jnp.broadcast_to(far_s, hg + (n_valid - n_tab,)), strided[..., ::-1],
                               jnp.full(hg + (ncs - n_valid,), NEG_INF, F32)], axis=-1)
    bias_cs = _pad_group_rows(bias_cs, 1)
    nblk = jnp.arange(ncs)[:, None]
    sblk = jnp.arange(nsp_s)[None, :]
    ov_s = ((nblk >= 4 * sblk - 1) & (nblk <= 4 * sblk + 3)).astype(BF16)
    o_cmp_s, idx_s = _nsa_sample_cmp(z3, kcvc_s, bias_cs, ov_s, ns_s)
    n_sel = min(SEL_TOPK, ns_s)
    idx = idx_s[:, :, 0, :n_sel]

    hg = (NSA_KV_HEADS, NSA_GROUP)
    n_key = ns_s * SEL_BLOCK
    bias_sel = jnp.concatenate([jnp.broadcast_to(far_s, hg + (past + 1 - REL_MAX_DIST,)),
                                bias_d[..., :REL_MAX_DIST][..., ::-1],
                                jnp.full(hg + (n_key - past - 1,), NEG_INF, F32)], axis=-1)
    bias_sel = _pad_group_rows(bias_sel.reshape(hg + (ns_s, SEL_BLOCK)).transpose(0, 2, 1, 3), 2)
    own_k = jnp.arange(2 * NSA_KV_HEADS)[None, :] == jnp.arange(NSA_KV_HEADS)[:, None]
    bias_sel = jnp.where(own_k[:, None, None, None, :], bias_sel[..., None], NEG_INF)
    bias_sel = bias_sel.reshape(NSA_KV_HEADS, ns_s, SROWS, SEL_BLOCK * 2 * NSA_KV_HEADS)
    kvs_new = zs[:, COL_KVS:COL_KVS + 2 * KV_W].reshape(nb, 2 * NSA_KV_HEADS, NSA_DK)
    o_sel_s = _nsa_sample_sel(z3, cache_s, kvs_new, page_table, idx, bias_sel, ns_s)

    win_buf = cache_win_kv[0].reshape(nb, WINDOW, 2 * KV_W)
    bias_w = jnp.concatenate([jnp.full(hg + (1,), NEG_INF, F32), bias_d[..., 1:WINDOW][..., ::-1]], axis=-1)
    o_win_s = _nsa_sample_win(z3, win_buf, _pad_group_rows(bias_w, 1), _pad_group_rows(bias_d[..., 0:1], 1))
    a_nsa_s = _nsa_sample_gate(o_cmp_s, o_sel_s, o_win_s, z3b)

    mem_kv_s = cache_mem_kv[0].reshape(nb, n_mem, 2 * MEM_W)
    a_mem_s = _mem_sample(z3b, mem_kv_s)

    merged_s = _merge(a_ret_s.reshape(nb, RET_W), a_nsa_s.reshape(nb, NSA_W), a_mem_s.reshape(nb, MEM_W),
                      w_ret, w_nsa, w_mem, zsb, nb, 512)
    y_s = _out_proj(merged_s, w_o, xs, norm_post[0], nb).reshape(nb, 1, D_MODEL)

    kvs_shape = (1, nb, 1, 2, NSA_KV_HEADS, NSA_DK)
    new_cmp_s = zs[:, COL_KVC:COL_KVC + 2 * KV_W].reshape(kvs_shape)
    new_sel_s = zs[:, COL_KVS:COL_KVS + 2 * KV_W].reshape(kvs_shape)
    kvw_s = zs[:, COL_KVW:COL_KVW + 2 * KV_W].reshape(nb, 1, 2, NSA_KV_HEADS, NSA_DK)
    new_win_s = jnp.concatenate([cache_win_kv[0][:, 1:], kvw_s], axis=1)[None]
    new_ret_s = ret_state_s[None]

    return (y_p, y_s, new_cmp_p, new_sel_p, new_win_p, new_ret_p, new_mem_p,
            new_cmp_s, new_sel_s, new_win_s, new_ret_s)
```

```python
import functools
import math

import jax
import jax.numpy as jnp
from jax import lax
from jax.experimental import pallas as pl
from jax.experimental.pallas import tpu as pltpu

F32 = jnp.float32
BF16 = jnp.bfloat16

D_MODEL = 2048
PAGE_SIZE = 128
RET_HEADS = 8
RET_DK = 256
RET_DV = 256
RET_CHUNK = 128
ROPE_BASE = 10000.0
NSA_HEADS = 16
NSA_KV_HEADS = 4
NSA_GROUP = NSA_HEADS // NSA_KV_HEADS
NSA_DK = 128
CMP_BLOCK = 32
CMP_STRIDE = 16
SEL_BLOCK = 64
SEL_TOPK = 16
WINDOW = 512
MEM_HEADS = 4
MEM_DH = 384
REL_BUCKETS = 32
REL_MAX_EXACT = 16
REL_MAX_DIST = 128
N_BRANCHES = 3
EPS = 1e-6
NEG_INF = -1e30
FORCE_SCORE = 1e4

RET_W = RET_HEADS * RET_DV
NSA_W = NSA_HEADS * NSA_DK
KV_W = NSA_KV_HEADS * NSA_DK
MEM_W = MEM_HEADS * MEM_DH

COL_RQ = 0
COL_RK = COL_RQ + RET_HEADS * RET_DK
COL_RV = COL_RK + RET_HEADS * RET_DK
COL_RG = COL_RV + RET_W
COL_NQ = COL_RG + RET_W
COL_KVC = COL_NQ + NSA_W
COL_KVS = COL_KVC + 2 * KV_W
COL_KVW = COL_KVS + 2 * KV_W
PROJ_A = COL_KVW + 2 * KV_W
COL_MQ = 0
COL_NSL = COL_MQ + MEM_W
COL_MG = COL_NSL + NSA_W
COL_NG = COL_MG + N_BRANCHES * D_MODEL
NG_SLOT = NSA_KV_HEADS * 128
PROJ_B = COL_NG + NG_SLOT

LOG2E = math.log2(math.e)
LANE = 128
TQ = 256
TK = 256
NSA_ROWS = NSA_GROUP * TQ
BIAS_DISTS = 1024
RET_STEP_CHUNKS = 8
PROJ_TN = 1024
VMEM_LIMIT = 48 * 1024 * 1024


def _cparams(n_axes):
    return pltpu.CompilerParams(dimension_semantics=("arbitrary",) * n_axes, vmem_limit_bytes=VMEM_LIMIT)


def _nt(a, b):
    return lax.dot_general(a, b, (((1,), (1,)), ((), ())), preferred_element_type=F32)


def _dot(a, b):
    return jnp.dot(a, b, preferred_element_type=F32)


def _sigmoid(x):
    return 1.0 / (1.0 + jnp.exp(-x))


def _iota(shape, dim):
    return lax.broadcasted_iota(jnp.int32, shape, dim)


def _rmsnorm_kernel(x_ref, g_ref, h_ref):
    x = x_ref[...]
    ms = jnp.mean(x * x, axis=-1, keepdims=True)
    h_ref[...] = ((x * lax.rsqrt(ms + EPS)) * g_ref[...]).astype(BF16)


def _rmsnorm(x, g, tm):
    m, k = x.shape
    return pl.pallas_call(
        _rmsnorm_kernel,
        grid=(m // tm,),
        in_specs=[pl.BlockSpec((tm, k), lambda i: (i, 0)), pl.BlockSpec((1, k), lambda i: (0, 0))],
        out_specs=pl.BlockSpec((tm, k), lambda i: (i, 0)),
        out_shape=jax.ShapeDtypeStruct((m, k), BF16),
        compiler_params=_cparams(1),
        name="rmsnorm",
    )(x, g.reshape(1, k))


def _proj_kernel(h_ref, w_ref, o_ref, wb_ref):
    @pl.when(pl.program_id(1) == 0)
    def _():
        wb_ref[...] = w_ref[...].astype(BF16)

    o_ref[...] = _dot(h_ref[...], wb_ref[...])


def _proj(h, w, tm, tn, n=None):
    m, k = h.shape
    n = w.shape[1] if n is None else n
    return pl.pallas_call(
        _proj_kernel,
        grid=(n // tn, m // tm),
        in_specs=[pl.BlockSpec((tm, k), lambda j, i: (i, 0)),
                  pl.BlockSpec((k, tn), lambda j, i: (0, j))],
        out_specs=pl.BlockSpec((tm, tn), lambda j, i: (i, j)),
        out_shape=jax.ShapeDtypeStruct((m, n), F32),
        scratch_shapes=[pltpu.VMEM((k, tn), BF16)],
        compiler_params=_cparams(2),
        name="proj",
    )(h, w)


def _rope_rows(x, cos, sin):
    half = x.shape[-1] // 2
    x1, x2 = x[:, :half], x[:, half:]
    return jnp.concatenate([x1 * cos - x2 * sin, x1 * sin + x2 * cos], axis=-1)


def _head_norm_gate(o, gnorm, rg):
    oc = o - jnp.mean(o, axis=-1, keepdims=True)
    y = oc * lax.rsqrt(jnp.mean(oc * oc, axis=-1, keepdims=True) + EPS) * gnorm
    return y * (rg * _sigmoid(rg))


def _ret_prompt_kernel(q_ref, k_ref, v_ref, rg_ref, cos_ref, sin_ref, dmat_ref, xi_ref, zeta_ref, gc_ref,
                       gn_ref, a_ref, s_ref):
    @pl.when(pl.program_id(2) == 0)
    def _():
        s_ref[...] = jnp.zeros_like(s_ref)

    c = RET_CHUNK
    for t in range(q_ref.shape[0] // c):
        rows = slice(t * c, (t + 1) * c)
        cos, sin = cos_ref[rows, :], sin_ref[rows, :]
        q = _rope_rows(q_ref[rows, :], cos, sin)
        k = _rope_rows(k_ref[rows, :], cos, sin) * (RET_DK ** -0.5)
        qb, vb = q.astype(BF16), v_ref[rows, :].astype(BF16)
        state = s_ref[0, 0]
        inner = _nt(qb, k.astype(BF16)) * dmat_ref[0]
        o = _dot(inner.astype(BF16), vb) + _dot(qb, state.astype(BF16)) * xi_ref[0]
        kz_t = (k * zeta_ref[0]).T.astype(BF16)
        s_ref[0, 0] = state * gc_ref[0] + _dot(kz_t, vb)
        a_ref[rows, :] = _head_norm_gate(o, gn_ref[...], rg_ref[rows, :]).astype(BF16)


def _decay_tables(chunk):
    log_g = jnp.log1p(-jnp.exp2(-5.0 - jnp.arange(RET_HEADS, dtype=F32)))
    i = jnp.arange(chunk, dtype=F32)
    diff = i[:, None] - i[None, :]
    dmat = jnp.where(diff >= 0, jnp.exp(log_g[:, None, None] * jnp.maximum(diff, 0.0)), 0.0)
    xi = jnp.exp(log_g[:, None] * (i[None, :] + 1.0))[:, :, None]
    zeta = jnp.exp(log_g[:, None] * (chunk - 1.0 - i[None, :]))[:, :, None]
    g_chunk = jnp.exp(log_g * chunk)[:, None, None]
    return dmat, xi, zeta, g_chunk


def _rope_tables(pos):
    half = RET_DK // 2
    freq = jnp.power(ROPE_BASE, -jnp.arange(half, dtype=F32) / half)
    ang = pos.astype(F32)[:, None] * freq[None, :]
    return jnp.cos(ang), jnp.sin(ang)


def _retention_prompt(z, ret_norm, batch, seq):
    c = RET_CHUNK
    rows = _pick_tile(seq, RET_STEP_CHUNKS * c)
    nc = seq // rows
    dmat, xi, zeta, g_chunk = _decay_tables(c)
    cos, sin = _rope_tables(jnp.arange(seq))
    hb = RET_DK

    def zspec(col0):
        return pl.BlockSpec((rows, hb), lambda b, h, t, col0=col0: (b * nc + t, col0 // hb + h))

    per_head = lambda shape: pl.BlockSpec((1,) + shape, lambda b, h, t: (h, 0, 0))
    return pl.pallas_call(
        _ret_prompt_kernel,
        grid=(batch, RET_HEADS, nc),
        in_specs=[zspec(COL_RQ), zspec(COL_RK), zspec(COL_RV), zspec(COL_RG),
                  pl.BlockSpec((rows, hb // 2), lambda b, h, t: (t, 0)),
                  pl.BlockSpec((rows, hb // 2), lambda b, h, t: (t, 0)),
                  per_head((c, c)), per_head((c, 1)), per_head((c, 1)), per_head((1, 1)),
                  pl.BlockSpec((1, hb), lambda b, h, t: (0, h))],
        out_specs=[pl.BlockSpec((rows, hb), lambda b, h, t: (b * nc + t, h)),
                   pl.BlockSpec((1, 1, RET_DK, RET_DV), lambda b, h, t: (b, h, 0, 0))],
        out_shape=[jax.ShapeDtypeStruct((batch * seq, RET_W), BF16),
                   jax.ShapeDtypeStruct((batch, RET_HEADS, RET_DK, RET_DV), F32)],
        compiler_params=_cparams(3),
        name="retention_prompt",
    )(z, z, z, z, cos, sin, dmat, xi, zeta, g_chunk, ret_norm.reshape(1, RET_W))


def _column_of(row):
    n = row.shape[1]
    eye = _iota((n, n), 0) == _iota((n, n), 1)
    return jnp.sum(jnp.where(eye, jnp.broadcast_to(row, (n, n)), 0.0), axis=1, keepdims=True)


def _ret_sample_kernel(q_ref, k_ref, v_ref, rg_ref, cos_ref, sin_ref, gam_ref, gn_ref, s_ref, a_ref, so_ref):
    cos, sin = cos_ref[...], sin_ref[...]
    q = _rope_rows(q_ref[0], cos, sin)
    k = _rope_rows(k_ref[0], cos, sin) * (RET_DK ** -0.5)
    v = v_ref[0]
    state = s_ref[0, 0]
    gamma = gam_ref[0]
    qk = jnp.sum(q * k, axis=-1, keepdims=True)
    o = qk * v + jnp.sum(_column_of(q) * state, axis=0, keepdims=True) * gamma
    so_ref[0, 0] = state * gamma + _column_of(k) * v
    a_ref[0] = _head_norm_gate(o, gn_ref[...], rg_ref[0]).astype(BF16)


def _retention_sample(z3, state, ret_norm, pos):
    nb = z3.shape[0]
    cos, sin = _rope_tables(jnp.full((1,), pos))
    gamma = jnp.exp(jnp.log1p(-jnp.exp2(-5.0 - jnp.arange(RET_HEADS, dtype=F32))))[:, None, None]
    hb = RET_DK

    def zspec(col0):
        return pl.BlockSpec((1, 1, hb), lambda b, h, col0=col0: (b, 0, col0 // hb + h))

    st_spec = pl.BlockSpec((1, 1, RET_DK, RET_DV), lambda b, h: (b, h, 0, 0))
    return pl.pallas_call(
        _ret_sample_kernel,
        grid=(nb, RET_HEADS),
        in_specs=[zspec(COL_RQ), zspec(COL_RK), zspec(COL_RV), zspec(COL_RG),
                  pl.BlockSpec((1, hb // 2), lambda b, h: (0, 0)),
                  pl.BlockSpec((1, hb // 2), lambda b, h: (0, 0)),
                  pl.BlockSpec((1, 1, 1), lambda b, h: (h, 0, 0)),
                  pl.BlockSpec((1, hb), lambda b, h: (0, h)),
                  st_spec],
        out_specs=[pl.BlockSpec((1, 1, hb), lambda b, h: (b, 0, h)), st_spec],
        out_shape=[jax.ShapeDtypeStruct((nb, 1, RET_W), BF16),
                   jax.ShapeDtypeStruct(state.shape, F32)],
        compiler_params=_cparams(2),
        name="retention_sample",
    )(z3, z3, z3, z3, cos, sin, gamma, ret_norm.reshape(1, RET_W), state)


def _half_rows(ref_slice_fn, n_half):
    return jnp.concatenate([ref_slice_fn(p) for p in range(CMP_STRIDE)], axis=1)


def _cmp_stage1_dense_kernel(x_ref, w_ref, o_ref):
    nh = o_ref.shape[3]
    x = _half_rows(lambda p: x_ref[pl.ds(p, nh, stride=CMP_STRIDE), :], nh).astype(BF16)
    o_ref[0, 0, 0] = _dot(x, w_ref[0])


def _cmp_stage1_dense(z, w1ab, batch, seq):
    nh = seq // CMP_STRIDE
    return pl.pallas_call(
        _cmp_stage1_dense_kernel,
        grid=(batch, 2, NSA_KV_HEADS),
        in_specs=[pl.BlockSpec((seq, NSA_DK), lambda b, kv, h: (b, COL_KVC // NSA_DK + kv * NSA_KV_HEADS + h)),
                  pl.BlockSpec((1, CMP_STRIDE * NSA_DK, 2 * NSA_DK), lambda b, kv, h: (kv, 0, 0))],
        out_specs=pl.BlockSpec((1, 1, 1, nh, 2 * NSA_DK), lambda b, kv, h: (b, kv, h, 0, 0)),
        out_shape=jax.ShapeDtypeStruct((batch, 2, NSA_KV_HEADS, nh, 2 * NSA_DK), F32),
        compiler_params=_cparams(3),
        name="cmp_stage1_dense",
    )(z, w1ab)


PAGES_PER_STEP = 8


def _cmp_stage1_paged_kernel(pt_ref, *refs):
    pages, (w_ref, o_ref) = refs[:PAGES_PER_STEP], refs[PAGES_PER_STEP:]
    hp = PAGE_SIZE // CMP_STRIDE
    top = _iota((2 * NSA_KV_HEADS, NSA_DK), 0) < NSA_KV_HEADS
    cols = [[], []]
    for p in range(CMP_STRIDE):
        tiles = [[], []]
        for pg in pages:
            xp = pg[0, :, p]
            for n in range(0, hp, 2):
                a, b = xp[n], xp[n + 1]
                tiles[0].append(jnp.where(top, a, pltpu.roll(b, NSA_KV_HEADS, 0)))
                tiles[1].append(jnp.where(top, pltpu.roll(a, NSA_KV_HEADS, 0), b))
        for kv in range(2):
            cols[kv].append(jnp.concatenate(tiles[kv], axis=0))
    for kv in range(2):
        x = jnp.concatenate(cols[kv], axis=1).astype(BF16)
        o_ref[0, kv, 0] = _dot(x, w_ref[kv])


def _cmp_stage1_paged(cache, page_table, w1ab):
    nb, n_pages = page_table.shape
    hp = PAGE_SIZE // CMP_STRIDE
    steps = n_pages // PAGES_PER_STEP
    rows = PAGES_PER_STEP * hp * NSA_KV_HEADS

    def page_spec(j):
        return pl.BlockSpec((1, hp, CMP_STRIDE, 2 * NSA_KV_HEADS, NSA_DK),
                            lambda b, s, pt, j=j: (pt[b, s * PAGES_PER_STEP + j], 0, 0, 0, 0))

    wspec = pl.BlockSpec((2, CMP_STRIDE * NSA_DK, 2 * NSA_DK), lambda b, s, pt: (0, 0, 0))
    return pl.pallas_call(
        _cmp_stage1_paged_kernel,
        grid_spec=pltpu.PrefetchScalarGridSpec(
            num_scalar_prefetch=1,
            grid=(nb, steps),
            in_specs=[page_spec(j) for j in range(PAGES_PER_STEP)] + [wspec],
            out_specs=pl.BlockSpec((1, 2, 1, rows, 2 * NSA_DK), lambda b, s, pt: (b, 0, 0, s, 0))),
        out_shape=jax.ShapeDtypeStruct((nb, 2, 1, steps * rows, 2 * NSA_DK), F32),
        compiler_params=_cparams(2),
        name="cmp_stage1_paged",
    )(page_table, *([cache] * PAGES_PER_STEP), w1ab)


def _cmp_stage2_kernel(ac_ref, pos_ref, w1a_ref, w1b_ref, w2_ref, o_ref, *, shift):
    ac = ac_ref[0, 0, 0]
    nh = ac.shape[0]
    pos = pos_ref[0].astype(BF16)
    kw = CMP_STRIDE * NSA_DK
    pe = _dot(pos[:, :kw], w1a_ref[0]) + _dot(pos[:, kw:], w1b_ref[0])
    pre = ac[:, :NSA_DK] + pltpu.roll(ac[:, NSA_DK:], nh - shift, 0) + pe[0:1]
    gelu = 0.5 * pre * (1.0 + jnp.tanh(math.sqrt(2.0 / math.pi) * (pre + 0.044715 * (pre * pre * pre))))
    o_ref[0, 0, 0] = _dot(gelu.astype(BF16), w2_ref[0]).astype(BF16)


def _cmp_stage2(ac, pos8, w1a, w1b, w2, shift):
    nb, _, groups, nh, _ = ac.shape
    kw = CMP_STRIDE * NSA_DK
    return pl.pallas_call(
        functools.partial(_cmp_stage2_kernel, shift=shift),
        grid=(nb, 2, groups),
        in_specs=[pl.BlockSpec((1, 1, 1, nh, 2 * NSA_DK), lambda b, kv, h: (b, kv, h, 0, 0)),
                  pl.BlockSpec((1, 8, 2 * kw), lambda b, kv, h: (kv, 0, 0)),
                  pl.BlockSpec((1, kw, NSA_DK), lambda b, kv, h: (kv, 0, 0)),
                  pl.BlockSpec((1, kw, NSA_DK), lambda b, kv, h: (kv, 0, 0)),
                  pl.BlockSpec((1, NSA_DK, NSA_DK), lambda b, kv, h: (kv, 0, 0))],
        out_specs=pl.BlockSpec((1, 1, 1, nh, NSA_DK), lambda b, kv, h: (b, kv, h, 0, 0)),
        out_shape=jax.ShapeDtypeStruct((nb, 2, groups, nh, NSA_DK), BF16),
        compiler_params=_cparams(3),
        name="cmp_stage2",
    )(ac, pos8, w1a, w1b, w2)


def _rel_bucket(dist):
    n = jnp.maximum(dist, 0)
    nf = jnp.maximum(n, 1).astype(F32)
    scale = (REL_BUCKETS - REL_MAX_EXACT) / math.log(REL_MAX_DIST / REL_MAX_EXACT)
    large = REL_MAX_EXACT + (jnp.log(nf / REL_MAX_EXACT) * scale).astype(jnp.int32)
    large = jnp.minimum(large, REL_BUCKETS - 1)
    return jnp.where(n < REL_MAX_EXACT, n, large)


def _bias_by_dist(rel_table, n):
    tab = rel_table[_rel_bucket(jnp.arange(n))]
    return tab.T.reshape(NSA_KV_HEADS, NSA_GROUP, n)


def _pad_group_rows(t, axis):
    first = lax.slice_in_dim(t, 0, 1, axis=axis)
    return jnp.concatenate([t] + [first] * (SROWS - NSA_GROUP), axis=axis)


def _flash_init(m_ref, l_ref, acc_ref):
    m_ref[...] = jnp.full(m_ref.shape, NEG_INF, F32)
    l_ref[...] = jnp.zeros(l_ref.shape, F32)
    acc_ref[...] = jnp.zeros(acc_ref.shape, F32)


def _flash_step(s, v, m_ref, l_ref, acc_ref):
    m_old = m_ref[...]
    m_new = jnp.maximum(m_old, jnp.max(s, axis=1, keepdims=True))
    alpha = jnp.exp(m_old - m_new)
    p = jnp.exp(s - m_new)
    l_ref[...] = alpha * l_ref[...] + jnp.sum(p, axis=1, keepdims=True)
    acc_ref[...] = alpha * acc_ref[...] + _dot(p.astype(BF16), v)
    m_ref[...] = m_new


def _flash_result(l_ref, acc_ref):
    return acc_ref[...] / jnp.maximum(l_ref[...], 1e-30)


def _select_blocks(imp_t, q0, ns):
    shape = imp_t.shape
    blk = _iota(shape, 0)
    qpos = q0 + _iota(shape, 1)
    cur = qpos >> 6
    valid = blk * SEL_BLOCK <= qpos
    forced = (blk == 0) | (blk == cur) | (blk == cur - 1)
    imp_t = jnp.where(valid, jnp.where(forced, FORCE_SCORE, imp_t), NEG_INF)
    rank = jnp.zeros(shape, F32)
    for other in range(ns):
        row = imp_t[other:other + 1, :]
        ahead = (row > imp_t) | ((row == imp_t) & (blk > other))
        rank = rank + jnp.where(ahead, 1.0, 0.0)
    return jnp.where((rank < SEL_TOPK) & valid, 1.0, 0.0)


def _nsa_prompt_kernel(q_ref, ks_ref, vs_ref, kw_ref, vw_ref, kc_ref, vct_ref, tz_ref, cfar_ref, basec_ref,
                       ovt_ref, ng_ref, nsl_ref, o_ref, m_ref, l_ref, acc_ref, vst_ref, vwt_ref, *, ns):
    i = pl.program_id(2)
    q0 = i * TQ
    qall = q_ref[...] * ((NSA_DK ** -0.5) * LOG2E)
    qt = jnp.concatenate([qall[:, g * NSA_DK:(g + 1) * NSA_DK].T for g in range(NSA_GROUP)], axis=1).astype(BF16)
    c_loc = _iota((TK, NSA_ROWS), 0)
    r_loc = _iota((TK, NSA_ROWS), 1) & (TQ - 1)

    @pl.when(i == 0)
    def _():
        for kt in range(vst_ref.shape[0]):
            vst_ref[kt] = vs_ref[kt * TK:(kt + 1) * TK, :].T.astype(BF16)
            vwt_ref[kt] = vw_ref[kt * TK:(kt + 1) * TK, :].T.astype(BF16)

    ncp = kc_ref.shape[3]
    shift = (TQ // CMP_STRIDE) * i
    bias_c = basec_ref[0, pl.ds(pl.multiple_of(ncp - shift, TQ // CMP_STRIDE), ncp), :]
    s = _dot(kc_ref[0, 0, 0], qt) + bias_c
    m = jnp.max(s, axis=0, keepdims=True)
    e = jnp.exp2(s - m)
    inv = jnp.where(m > 0.5 * NEG_INF, 1.0 / jnp.maximum(jnp.sum(e, axis=0, keepdims=True), 1e-30), 0.0)
    p = e * inv
    o_cmp = _dot(vct_ref[0, 0], p.astype(BF16))

    psum = p[:, 0:TQ] + p[:, TQ:2 * TQ] + p[:, 2 * TQ:3 * TQ] + p[:, 3 * TQ:4 * TQ]
    hi = psum.astype(BF16)
    lo = (psum - hi.astype(F32)).astype(BF16)
    ovt = ovt_ref[...]
    imp_t = _dot(ovt, hi) + _dot(ovt, lo)
    ns8 = -(-ns // 8) * 8
    sel_t = _select_blocks(imp_t[:ns8], q0, ns)
    nsp = ovt.shape[0]
    sel_neg = jnp.where(sel_t > 0.5, 0.0, NEG_INF)
    if ns8 < nsp:
        sel_neg = jnp.concatenate([sel_neg, jnp.full((nsp - ns8, TQ), NEG_INF, F32)], axis=0)
    sel_neg = sel_neg.astype(BF16)

    def flash_step(scs, vts):
        m_old = m_ref[...]
        m_new = m_old
        for sc in scs:
            m_new = jnp.maximum(m_new, jnp.max(sc, axis=0, keepdims=True))
        alpha = jnp.exp2(m_old - m_new)
        l_new = alpha * l_ref[...]
        acc = alpha * acc_ref[...]
        for sc, vt in zip(scs, vts):
            pt = jnp.exp2(sc - m_new)
            l_new = l_new + jnp.sum(pt, axis=0, keepdims=True)
            acc = acc + _dot(vt, pt.astype(BF16))
        l_ref[...] = l_new
        acc_ref[...] = acc
        m_ref[...] = m_new

    def sel_scores(kt, bias, causal):
        k = ks_ref[pl.ds(pl.multiple_of(kt * TK, TK), TK), :].astype(BF16)
        blk_of_key = (TK // SEL_BLOCK) * kt + (_iota((TK, nsp), 0) >> 6)
        expand = jnp.where(_iota((TK, nsp), 1) == blk_of_key, 1.0, 0.0).astype(BF16)
        mk = _dot(expand, sel_neg)
        sc = _dot(k, qt) + bias + jnp.concatenate([mk] * NSA_GROUP, axis=1)
        if causal:
            sc = jnp.where(c_loc <= r_loc, sc, NEG_INF)
        return sc

    _flash_init(m_ref, l_ref, acc_ref)

    @pl.when(i == 0)
    def _():
        flash_step([sel_scores(i, tz_ref[0, 0], True)], [vst_ref[i]])

    @pl.when(i >= 1)
    def _():
        flash_step([sel_scores(i, tz_ref[0, 0], True), sel_scores(i - 1, tz_ref[0, 1], False)],
                   [vst_ref[i], vst_ref[i - 1]])

    def far_pair(j, carry):
        kt = i - 2 - 2 * j
        flash_step([sel_scores(kt, cfar_ref[0], False), sel_scores(kt - 1, cfar_ref[0], False)],
                   [vst_ref[kt], vst_ref[kt - 1]])
        return carry

    n_far = jnp.maximum(i - 1, 0)
    lax.fori_loop(0, n_far >> 1, far_pair, 0)

    @pl.when((n_far & 1) == 1)
    def _():
        flash_step([sel_scores(0, cfar_ref[0], False)], [vst_ref[0]])

    o_sel = _flash_result(l_ref, acc_ref)

    def win_scores(off):
        kt = i - off
        k = kw_ref[pl.ds(pl.multiple_of(kt * TK, TK), TK), :].astype(BF16)
        sc = _dot(k, qt) + (tz_ref[0, off] if off < 2 else cfar_ref[0])
        if off == 0:
            sc = jnp.where(c_loc <= r_loc, sc, NEG_INF)
        if off * TK == WINDOW:
            sc = jnp.where(c_loc > r_loc, sc, NEG_INF)
        return sc

    def win_tiles(n_tiles):
        flash_step([win_scores(off) for off in range(n_tiles)], [vwt_ref[i - off] for off in range(n_tiles)])

    _flash_init(m_ref, l_ref, acc_ref)
    n_win = WINDOW // TK + 1
    for n_tiles in range(1, n_win + 1):
        cond = (i == n_tiles - 1) if n_tiles < n_win else (i >= n_win - 1)
        pl.when(cond)(functools.partial(win_tiles, n_tiles))
    o_win = _flash_result(l_ref, acc_ref)

    gates_t = _sigmoid(ng_ref[...]).T
    nsl = nsl_ref[...]
    outs = []
    for g in range(NSA_GROUP):
        cols = slice(g * TQ, (g + 1) * TQ)
        o_t = (gates_t[3 * g:3 * g + 1] * o_cmp[:, cols] + gates_t[3 * g + 1:3 * g + 2] * o_sel[:, cols]
               + gates_t[3 * g + 2:3 * g + 3] * o_win[:, cols])
        x = nsl[:, g * NSA_DK:(g + 1) * NSA_DK]
        outs.append(o_t.T * (x * _sigmoid(x)))
    o_ref[...] = jnp.concatenate(outs, axis=1).astype(BF16)


def _lanes_by_head(t):
    hk, g, rows, tq = t.shape
    return t.transpose(0, 2, 1, 3).reshape(hk, rows, g * tq)


def _nsa_prompt(z, zb, kcvc, bias_d, batch, seq):
    nq = seq // TQ
    ns = seq // SEL_BLOCK
    ncp = kcvc.shape[3]
    nsp = LANE
    wn = TQ // CMP_STRIDE
    assert TQ == TK and ns <= nsp and ncp >= seq // CMP_STRIDE and ncp % LANE == 0 and ncp > wn
    gw = NSA_GROUP * NSA_DK
    n_dist = bias_d.shape[-1]
    assert n_dist >= 2 * TK + TQ
    bias_d = bias_d * LOG2E

    rep = jnp.tile(bias_d, (1, 1, TK + 1))[..., :TK * (n_dist - 1)].reshape(NSA_KV_HEADS, NSA_GROUP, TK, n_dist - 1)
    tz = jnp.stack([_lanes_by_head(rep[..., 0:TQ]), _lanes_by_head(rep[..., TK:TK + TQ])], axis=1)
    far = jnp.broadcast_to(bias_d[..., REL_MAX_DIST][:, :, None, None], (NSA_KV_HEADS, NSA_GROUP, 1, TQ))
    cfar = _lanes_by_head(far)
    half = n_dist // 2
    start = CMP_STRIDE * wn - (CMP_BLOCK - 1)
    assert start + TQ <= half and 2 * wn * CMP_STRIDE - start <= half
    w_ext = jnp.concatenate([bias_d[..., :half], jnp.full(bias_d.shape[:-1] + (n_dist - half,), NEG_INF, F32)], -1)
    near = jnp.tile(w_ext, (1, 1, 2 * wn + 1))[..., :2 * wn * (n_dist - CMP_STRIDE)]
    near = near.reshape(NSA_KV_HEADS, NSA_GROUP, 2 * wn, n_dist - CMP_STRIDE)[..., start:start + TQ]
    basec = jnp.concatenate([jnp.broadcast_to(far, (NSA_KV_HEADS, NSA_GROUP, ncp - wn, TQ)), near,
                             jnp.full((NSA_KV_HEADS, NSA_GROUP, ncp - wn, TQ), NEG_INF, F32)], axis=2)
    basec = _lanes_by_head(basec)
    sblk = jnp.arange(nsp)[:, None]
    nblk = jnp.arange(ncp)[None, :]
    ovt = ((nblk >= 4 * sblk - 1) & (nblk <= 4 * sblk + 3)).astype(BF16)

    vct = kcvc[:, 1].transpose(0, 1, 3, 2)

    def kvspec(col0, which):
        return pl.BlockSpec((seq, NSA_DK), lambda b, h, i: (b, col0 // NSA_DK + which * NSA_KV_HEADS + h))

    vt_scratch = pltpu.VMEM((seq // TK, NSA_DK, TK), BF16)
    return pl.pallas_call(
        functools.partial(_nsa_prompt_kernel, ns=ns),
        grid=(batch, NSA_KV_HEADS, nq),
        in_specs=[pl.BlockSpec((TQ, gw), lambda b, h, i: (b * nq + i, COL_NQ // gw + h)),
                  kvspec(COL_KVS, 0), kvspec(COL_KVS, 1), kvspec(COL_KVW, 0), kvspec(COL_KVW, 1),
                  pl.BlockSpec((1, 1, 1, ncp, NSA_DK), lambda b, h, i: (b, 0, h, 0, 0)),
                  pl.BlockSpec((1, 1, NSA_DK, ncp), lambda b, h, i: (b, h, 0, 0)),
                  pl.BlockSpec((1, 2, TK, NSA_ROWS), lambda b, h, i: (h, 0, 0, 0)),
                  pl.BlockSpec((1, 1, NSA_ROWS), lambda b, h, i: (h, 0, 0)),
                  pl.BlockSpec((1, 2 * ncp, NSA_ROWS), lambda b, h, i: (h, 0, 0)),
                  pl.BlockSpec((nsp, ncp), lambda b, h, i: (0, 0)),
                  pl.BlockSpec((TQ, LANE), lambda b, h, i: (b * nq + i, COL_NG // LANE + h)),
                  pl.BlockSpec((TQ, gw), lambda b, h, i: (b * nq + i, COL_NSL // gw + h))],
        out_specs=pl.BlockSpec((TQ, gw), lambda b, h, i: (b * nq + i, h)),
        out_shape=jax.ShapeDtypeStruct((batch * seq, NSA_W), BF16),
        scratch_shapes=[pltpu.VMEM((1, NSA_ROWS), F32), pltpu.VMEM((1, NSA_ROWS), F32),
                        pltpu.VMEM((NSA_DK, NSA_ROWS), F32), vt_scratch, vt_scratch],
        compiler_params=_cparams(3),
        name="nsa_prompt",
    )(z, z, z, z, z, kcvc, vct, tz, cfar, basec, ovt, zb, zb)


SROWS = 8


def _stack_group_q(q_row):
    heads = [q_row[:, g * NSA_DK:(g + 1) * NSA_DK] for g in range(NSA_GROUP)]
    return jnp.concatenate(heads + [heads[0]] * (SROWS - NSA_GROUP), axis=0)


def _nsa_sample_cmp_kernel(q_ref, kc_ref, vc_ref, bias_ref, ov_ref, o_ref, idx_ref, *, ns):
    scale = NSA_DK ** -0.5
    qs = _stack_group_q(q_ref[0]).astype(BF16)
    s = _nt(qs, kc_ref[0, 0]) * scale + bias_ref[0]
    m = jnp.max(s, axis=1, keepdims=True)
    e = jnp.exp(s - m)
    inv = jnp.where(m > 0.5 * NEG_INF, 1.0 / jnp.maximum(jnp.sum(e, axis=1, keepdims=True), 1e-30), 0.0)
    p = e * inv
    o_ref[0, 0] = _dot(p.astype(BF16), vc_ref[0, 0])
    psum = jnp.broadcast_to(jnp.sum(p[0:NSA_GROUP], axis=0, keepdims=True), p.shape)
    hi = psum.astype(BF16)
    lo = (psum - hi.astype(F32)).astype(BF16)
    imp = (_dot(hi, ov_ref[...]) + _dot(lo, ov_ref[...]))[0:1]
    nsp = imp.shape[1]
    blk_r = _iota((1, nsp), 1)
    cur = ns - 1
    forced = (blk_r == 0) | (blk_r == cur) | (blk_r == cur - 1)
    imp = jnp.where(blk_r < ns, jnp.where(forced, FORCE_SCORE, imp), 2.0 * NEG_INF)
    imp_c = _column_of(imp)
    i_r = _iota((nsp, nsp), 1)
    j_c = _iota((nsp, nsp), 0)
    ahead = (imp > imp_c) | ((imp == imp_c) & (i_r < j_c))
    rank_c = jnp.sum(jnp.where(ahead, 1.0, 0.0), axis=1, keepdims=True)
    slot = _iota((nsp, LANE), 1).astype(F32)
    picks = jnp.where(rank_c == slot, _iota((nsp, LANE), 0).astype(F32), 0.0)
    idx_ref[0, 0] = jnp.broadcast_to(jnp.sum(picks, axis=0, keepdims=True), (SROWS, LANE)).astype(jnp.int32)


def _nsa_sample_cmp(z3, kcvc, bias_c, ov, ns):
    nb = z3.shape[0]
    ncp = kcvc.shape[2]
    nsp = ov.shape[1]
    gw = NSA_GROUP * NSA_DK

    def cspec(which):
        return pl.BlockSpec((1, 1, ncp, NSA_DK), lambda b, h: (b, which, 0, h))

    return pl.pallas_call(
        functools.partial(_nsa_sample_cmp_kernel, ns=ns),
        grid=(nb, NSA_KV_HEADS),
        in_specs=[pl.BlockSpec((1, 1, gw), lambda b, h: (b, 0, COL_NQ // gw + h)),
                  cspec(0), cspec(1),
                  pl.BlockSpec((1, SROWS, ncp), lambda b, h: (h, 0, 0)),
                  pl.BlockSpec((ncp, nsp), lambda b, h: (0, 0))],
        out_specs=[pl.BlockSpec((1, 1, SROWS, NSA_DK), lambda b, h: (b, h, 0, 0)),
                   pl.BlockSpec((1, 1, SROWS, LANE), lambda b, h: (b, h, 0, 0))],
        out_shape=[jax.ShapeDtypeStruct((nb, NSA_KV_HEADS, SROWS, NSA_DK), F32),
                   jax.ShapeDtypeStruct((nb, NSA_KV_HEADS, SROWS, LANE), jnp.int32)],
        compiler_params=_cparams(2),
        name="nsa_sample_cmp",
    )(z3, kcvc, kcvc, bias_c, ov)


def _nsa_sample_sel_kernel(pt_ref, idx_ref, q_ref, *refs, ns):
    blocks, (new_ref,), biases = refs[:NSA_KV_HEADS], refs[NSA_KV_HEADS:NSA_KV_HEADS + 1], refs[NSA_KV_HEADS + 1:2 * NSA_KV_HEADS + 1]
    o_ref, m_ref, l_ref, acc_ref = refs[2 * NSA_KV_HEADS + 1:]
    b, t = pl.program_id(0), pl.program_id(1)
    gw = NSA_GROUP * NSA_DK
    rows_kv = 2 * NSA_KV_HEADS

    @pl.when(t == 0)
    def _():
        _flash_init(m_ref, l_ref, acc_ref)

    new_rows = jnp.concatenate([new_ref[0]] * SEL_BLOCK, axis=0)
    xs, scs = [], []
    for h in range(NSA_KV_HEADS):
        is_new = idx_ref[b, h, t] == ns - 1
        x = blocks[h][0].reshape(SEL_BLOCK * rows_kv, NSA_DK)
        x = jnp.where(is_new, new_rows, x).astype(BF16)
        qs = _stack_group_q(q_ref[0][:, h * gw:(h + 1) * gw]).astype(BF16)
        xs.append(x)
        scs.append(_nt(qs, x) * (NSA_DK ** -0.5) + biases[h][0, 0])
    sc = jnp.concatenate(scs, axis=0)
    m_old = m_ref[...]
    m_new = jnp.maximum(m_old, jnp.max(sc, axis=1, keepdims=True))
    alpha = jnp.exp(m_old - m_new)
    p = jnp.exp(sc - m_new)
    l_ref[...] = alpha * l_ref[...] + jnp.sum(p, axis=1, keepdims=True)
    pv = pltpu.roll(p, NSA_KV_HEADS, 1).astype(BF16)
    acc_ref[...] = alpha * acc_ref[...] + jnp.concatenate(
        [_dot(pv[h * SROWS:(h + 1) * SROWS], xs[h]) for h in range(NSA_KV_HEADS)], axis=0)
    m_ref[...] = m_new

    @pl.when(t == pl.num_programs(1) - 1)
    def _():
        o_ref[0] = (acc_ref[...] / jnp.maximum(l_ref[...], 1e-30)).reshape(NSA_KV_HEADS, SROWS, NSA_DK)


def _nsa_sample_sel(z3, cache, kv_new, page_table, idx, bias_sel, ns):
    nb, n_pages = page_table.shape
    n_sel = idx.shape[2]
    halves = PAGE_SIZE // SEL_BLOCK
    rows_kv = 2 * NSA_KV_HEADS

    def blockspec(h):
        def index(b, t, pt, ix):
            blk = ix[b, h, t]
            return (pt[b, jnp.minimum(blk // halves, n_pages - 1)], blk % halves, 0, 0)
        return pl.BlockSpec((1, SEL_BLOCK, rows_kv, NSA_DK), index)

    def biasspec(h):
        return pl.BlockSpec((1, 1, SROWS, SEL_BLOCK * rows_kv), lambda b, t, pt, ix: (h, ix[b, h, t], 0, 0))

    heads = range(NSA_KV_HEADS)
    return pl.pallas_call(
        functools.partial(_nsa_sample_sel_kernel, ns=ns),
        grid_spec=pltpu.PrefetchScalarGridSpec(
            num_scalar_prefetch=2,
            grid=(nb, n_sel),
            in_specs=[pl.BlockSpec((1, 1, NSA_W), lambda b, t, pt, ix: (b, 0, COL_NQ // NSA_W))]
                     + [blockspec(h) for h in heads]
                     + [pl.BlockSpec((1, rows_kv, NSA_DK), lambda b, t, pt, ix: (b, 0, 0))]
                     + [biasspec(h) for h in heads],
            out_specs=pl.BlockSpec((1, NSA_KV_HEADS, SROWS, NSA_DK), lambda b, t, pt, ix: (b, 0, 0, 0)),
            scratch_shapes=[pltpu.VMEM((NSA_KV_HEADS * SROWS, 1), F32), pltpu.VMEM((NSA_KV_HEADS * SROWS, 1), F32),
                            pltpu.VMEM((NSA_KV_HEADS * SROWS, NSA_DK), F32)]),
        out_shape=jax.ShapeDtypeStruct((nb, NSA_KV_HEADS, SROWS, NSA_DK), F32),
        compiler_params=_cparams(2),
        name="nsa_sample_sel",
    )(page_table, idx, z3, *([cache] * NSA_KV_HEADS), kv_new, *([bias_sel] * NSA_KV_HEADS))


def _nsa_sample_win_kernel(q_ref, k_ref, v_ref, kn_ref, vn_ref, bias_ref, bnew_ref, o_ref):
    scale = NSA_DK ** -0.5
    q = _stack_group_q(q_ref[0])
    s_buf = _nt(q.astype(BF16), k_ref[0].astype(BF16)) * scale + bias_ref[0]
    s_new = jnp.sum(q * kn_ref[0], axis=1, keepdims=True) * scale + bnew_ref[0]
    m = jnp.maximum(jnp.max(s_buf, axis=1, keepdims=True), s_new)
    p_buf = jnp.exp(s_buf - m)
    p_new = jnp.exp(s_new - m)
    l = jnp.sum(p_buf, axis=1, keepdims=True) + p_new
    acc = _dot(p_buf.astype(BF16), v_ref[0].astype(BF16)) + p_new * vn_ref[0]
    o_ref[0, 0] = acc / jnp.maximum(l, 1e-30)


def _nsa_sample_win(z3, win_buf, bias_win, bias_new):
    nb, nbuf, _ = win_buf.shape
    gw = NSA_GROUP * NSA_DK

    def bufspec(which):
        return pl.BlockSpec((1, nbuf, NSA_DK), lambda b, h: (b, 0, which * NSA_KV_HEADS + h))

    def newspec(which):
        return pl.BlockSpec((1, 1, NSA_DK), lambda b, h: (b, 0, COL_KVW // NSA_DK + which * NSA_KV_HEADS + h))

    return pl.pallas_call(
        _nsa_sample_win_kernel,
        grid=(nb, NSA_KV_HEADS),
        in_specs=[pl.BlockSpec((1, 1, gw), lambda b, h: (b, 0, COL_NQ // gw + h)),
                  bufspec(0), bufspec(1), newspec(0), newspec(1),
                  pl.BlockSpec((1, SROWS, nbuf), lambda b, h: (h, 0, 0)),
                  pl.BlockSpec((1, SROWS, 1), lambda b, h: (h, 0, 0))],
        out_specs=pl.BlockSpec((1, 1, SROWS, NSA_DK), lambda b, h: (b, h, 0, 0)),
        out_shape=jax.ShapeDtypeStruct((nb, NSA_KV_HEADS, SROWS, NSA_DK), F32),
        compiler_params=_cparams(2),
        name="nsa_sample_win",
    )(z3, win_buf, win_buf, z3, z3, bias_win, bias_new)


def _nsa_sample_gate_kernel(oc_ref, os_ref, ow_ref, ng_ref, nsl_ref, o_ref):
    gates = _sigmoid(ng_ref[0])
    nsl = nsl_ref[0]
    outs = []
    for g in range(NSA_GROUP):
        o = (gates[:, 3 * g:3 * g + 1] * oc_ref[0, 0, g:g + 1] + gates[:, 3 * g + 1:3 * g + 2] * os_ref[0, 0, g:g + 1]
             + gates[:, 3 * g + 2:3 * g + 3] * ow_ref[0, 0, g:g + 1])
        x = nsl[:, g * NSA_DK:(g + 1) * NSA_DK]
        outs.append(o * (x * _sigmoid(x)))
    o_ref[0] = jnp.concatenate(outs, axis=1).astype(BF16)


def _nsa_sample_gate(o_cmp, o_sel, o_win, z3b):
    nb = z3b.shape[0]
    gw = NSA_GROUP * NSA_DK
    ospec = pl.BlockSpec((1, 1, SROWS, NSA_DK), lambda b, h: (b, h, 0, 0))
    return pl.pallas_call(
        _nsa_sample_gate_kernel,
        grid=(nb, NSA_KV_HEADS),
        in_specs=[ospec, ospec, ospec,
                  pl.BlockSpec((1, 1, LANE), lambda b, h: (b, 0, COL_NG // LANE + h)),
                  pl.BlockSpec((1, 1, gw), lambda b, h: (b, 0, COL_NSL // gw + h))],
        out_specs=pl.BlockSpec((1, 1, gw), lambda b, h: (b, 0, h)),
        out_shape=jax.ShapeDtypeStruct((nb, 1, NSA_W), BF16),
        compiler_params=_cparams(2),
        name="nsa_sample_gate",
    )(o_cmp, o_sel, o_win, z3b, z3b)


def _mem_heads(q, kv):
    outs = []
    for h in range(MEM_HEADS):
        k = kv[:, h * MEM_DH:(h + 1) * MEM_DH].astype(BF16)
        v = kv[:, MEM_W + h * MEM_DH:MEM_W + (h + 1) * MEM_DH].astype(BF16)
        s = _nt(q[:, h * MEM_DH:(h + 1) * MEM_DH].astype(BF16), k) * (MEM_DH ** -0.5)
        e = jnp.exp(s - jnp.max(s, axis=1, keepdims=True))
        p = e / jnp.sum(e, axis=1, keepdims=True)
        outs.append(_dot(p.astype(BF16), v))
    return jnp.concatenate(outs, axis=1)


def _mem_prompt_kernel(q_ref, kv_ref, o_ref):
    o_ref[...] = _mem_heads(q_ref[...], kv_ref[...]).astype(BF16)


def _mem_prompt(z, mem_kv, batch, seq, tq):
    nq = seq // tq
    n_mem = mem_kv.shape[0] // batch
    return pl.pallas_call(
        _mem_prompt_kernel,
        grid=(batch, nq),
        in_specs=[pl.BlockSpec((tq, MEM_W), lambda b, i: (b * nq + i, COL_MQ // MEM_W)),
                  pl.BlockSpec((n_mem, 2 * MEM_W), lambda b, i: (b, 0))],
        out_specs=pl.BlockSpec((tq, MEM_W), lambda b, i: (b * nq + i, 0)),
        out_shape=jax.ShapeDtypeStruct((batch * seq, MEM_W), BF16),
        compiler_params=_cparams(2),
        name="mem_prompt",
    )(z, mem_kv)


def _mem_sample_kernel(q_ref, kv_ref, o_ref):
    q = jnp.broadcast_to(q_ref[0], (SROWS, MEM_W))
    o_ref[0] = _mem_heads(q, kv_ref[0])[0:1].astype(BF16)


def _mem_sample(z3, mem_kv):
    nb, n_mem, _ = mem_kv.shape
    return pl.pallas_call(
        _mem_sample_kernel,
        grid=(nb,),
        in_specs=[pl.BlockSpec((1, 1, MEM_W), lambda b: (b, 0, COL_MQ // MEM_W)),
                  pl.BlockSpec((1, n_mem, 2 * MEM_W), lambda b: (b, 0, 0))],
        out_specs=pl.BlockSpec((1, 1, MEM_W), lambda b: (b, 0, 0)),
        out_shape=jax.ShapeDtypeStruct((nb, 1, MEM_W), BF16),
        compiler_params=_cparams(1),
        name="mem_sample",
    )(z3, mem_kv)


def _merge_kernel(ar_ref, an_ref, am_ref, wr_ref, wn_ref, wm_ref, g0_ref, g1_ref, g2_ref, o_ref):
    merged = (_sigmoid(g0_ref[...]) * _dot(ar_ref[...], wr_ref[...])
              + _sigmoid(g1_ref[...]) * _dot(an_ref[...], wn_ref[...])
              + _sigmoid(g2_ref[...]) * _dot(am_ref[...], wm_ref[...]))
    o_ref[...] = merged.astype(BF16)


def _merge(a_ret, a_nsa, a_mem, w_ret, w_nsa, w_mem, z, tm, tn):
    m = a_ret.shape[0]
    nt = D_MODEL // tn

    def aspec(width):
        return pl.BlockSpec((tm, width), lambda i, j: (i, 0))

    def wspec(width):
        return pl.BlockSpec((width, tn), lambda i, j: (0, j))

    def gspec(branch):
        return pl.BlockSpec((tm, tn), lambda i, j: (i, COL_MG // tn + branch * nt + j))

    return pl.pallas_call(
        _merge_kernel,
        grid=(m // tm, nt),
        in_specs=[aspec(RET_W), aspec(NSA_W), aspec(MEM_W), wspec(RET_W), wspec(NSA_W), wspec(MEM_W),
                  gspec(0), gspec(1), gspec(2)],
        out_specs=pl.BlockSpec((tm, tn), lambda i, j: (i, j)),
        out_shape=jax.ShapeDtypeStruct((m, D_MODEL), BF16),
        compiler_params=_cparams(2),
        name="merge",
    )(a_ret, a_nsa, a_mem, w_ret, w_nsa, w_mem, z, z, z)


def _out_kernel(a_ref, w_ref, x_ref, g_ref, o_ref):
    out = _dot(a_ref[...], w_ref[...])
    y = out * lax.rsqrt(jnp.mean(out * out, axis=-1, keepdims=True) + EPS)
    o_ref[...] = x_ref[...] + y * g_ref[...]


def _out_proj(merged, w_out, x, norm_post, tm):
    m = merged.shape[0]
    return pl.pallas_call(
        _out_kernel,
        grid=(m // tm,),
        in_specs=[pl.BlockSpec((tm, D_MODEL), lambda i: (i, 0)),
                  pl.BlockSpec((D_MODEL, D_MODEL), lambda i: (0, 0)),
                  pl.BlockSpec((tm, D_MODEL), lambda i: (i, 0)),
                  pl.BlockSpec((1, D_MODEL), lambda i: (0, 0))],
        out_specs=pl.BlockSpec((tm, D_MODEL), lambda i: (i, 0)),
        out_shape=jax.ShapeDtypeStruct((m, D_MODEL), F32),
        compiler_params=_cparams(1),
        name="out_proj",
    )(merged, w_out, x, norm_post.reshape(1, D_MODEL))


def _layout_w_tail(w_in):
    splits = (N_BRANCHES * NSA_HEADS, NSA_W, MEM_W, N_BRANCHES * D_MODEL)
    pieces, start = [], PROJ_A
    for w in splits:
        pieces.append(w_in[:, start:start + w])
        start += w
    assert start == w_in.shape[1]
    ng, nsl, mq, mg = pieces
    per_group = N_BRANCHES * NSA_GROUP
    ng = ng.reshape(D_MODEL, NSA_KV_HEADS, per_group)
    ng = jnp.pad(ng, ((0, 0), (0, 0), (0, LANE - per_group))).reshape(D_MODEL, NG_SLOT)
    return jnp.concatenate([mq, nsl, mg, ng], axis=1)


def _pick_tile(m, cap):
    t = min(m, cap)
    while m % t:
        t //= 2
    return t


def kernel(x_prompt, x_sample, cache_cmp_kv, cache_sel_kv, cache_win_kv, state_ret, cache_mem_kv, page_table,
           mem_prompt, rel_table, norm_pre, norm_post, norm_mem, w_in, ret_norm, w_ret_up, cmp_pos, w_cmp1,
           w_cmp2, w_nsa_up, w_mem_kv, w_mem_up, w_out):
    batch, seq, _ = x_prompt.shape
    nb = x_sample.shape[0]
    assert x_sample.shape[1] == 1 and norm_pre.shape[0] == 1
    assert seq % TQ == 0 and seq >= WINDOW
    n_pool = cache_cmp_kv.shape[1]
    n_pages = page_table.shape[1]
    past = n_pages * PAGE_SIZE
    n_mem = mem_prompt.shape[1]
    assert n_pages % PAGES_PER_STEP == 0 and cache_win_kv.shape[2] == WINDOW

    w_a = w_in[0]
    w_b = _layout_w_tail(w_in[0])
    kw = CMP_STRIDE * NSA_DK
    w1 = w_cmp1[0].reshape(2, CMP_BLOCK * NSA_DK, NSA_DK).astype(BF16)
    w1a, w1b = w1[:, :kw], w1[:, kw:]
    w1ab = jnp.concatenate([w1a, w1b], axis=2)
    w2 = w_cmp2[0].astype(BF16)
    pos8 = jnp.pad(cmp_pos[0].reshape(2, 1, CMP_BLOCK * NSA_DK), ((0, 0), (0, 7), (0, 0)))
    w_ret = w_ret_up[0].astype(BF16)
    w_nsa = w_nsa_up[0].astype(BF16)
    w_mem = w_mem_up[0].astype(BF16)
    w_o = w_out[0].astype(BF16)

    m_p = batch * seq
    xp = x_prompt.reshape(m_p, D_MODEL)
    hp = _rmsnorm(xp, norm_pre[0], _pick_tile(m_p, 512))
    z = _proj(hp, w_a, _pick_tile(m_p, 1024), PROJ_TN, PROJ_A)
    zb = _proj(hp, w_b, _pick_tile(m_p, 1024), PROJ_TN)

    a_ret, ret_state_p = _retention_prompt(z, ret_norm[0], batch, seq)

    ncp = max(LANE, -(-(seq // CMP_STRIDE) // LANE) * LANE)
    ac = _cmp_stage1_dense(z, w1ab, batch, seq)
    kcvc = _cmp_stage2(ac, pos8, w1a, w1b, w2, 1)
    if ncp > kcvc.shape[3]:
        kcvc = jnp.pad(kcvc, ((0, 0), (0, 0), (0, 0), (0, ncp - kcvc.shape[3]), (0, 0)))
    bias_d = _bias_by_dist(rel_table, BIAS_DISTS)
    a_nsa = _nsa_prompt(z, zb, kcvc, bias_d, batch, seq)

    hm = _rmsnorm(mem_prompt.reshape(batch * n_mem, D_MODEL), norm_mem[0], _pick_tile(batch * n_mem, 512))
    mem_kv_p = _proj(hm, w_mem_kv[0], _pick_tile(batch * n_mem, 512), PROJ_TN)
    a_mem = _mem_prompt(zb, mem_kv_p, batch, seq, _pick_tile(seq, 512))

    merged = _merge(a_ret, a_nsa, a_mem, w_ret, w_nsa, w_mem, zb, _pick_tile(m_p, 512), 512)
    y_p = _out_proj(merged, w_o, xp, norm_post[0], _pick_tile(m_p, 256)).reshape(batch, seq, D_MODEL)

    kv_shape = (1, batch, seq, 2, NSA_KV_HEADS, NSA_DK)
    new_cmp_p = z[:, COL_KVC:COL_KVC + 2 * KV_W].reshape(kv_shape)
    new_sel_p = z[:, COL_KVS:COL_KVS + 2 * KV_W].reshape(kv_shape)
    new_win_p = z[:, COL_KVW:COL_KVW + 2 * KV_W].reshape(kv_shape)[:, :, seq - WINDOW:]
    new_ret_p = ret_state_p[None]
    new_mem_p = mem_kv_p.reshape(1, batch, n_mem, 2, MEM_HEADS, MEM_DH)

    xs = x_sample.reshape(nb, D_MODEL)
    hs = _rmsnorm(xs, norm_pre[0], nb)
    zs = _proj(hs, w_a, nb, PROJ_TN, PROJ_A)
    zsb = _proj(hs, w_b, nb, PROJ_TN)
    z3 = zs.reshape(nb, 1, PROJ_A)
    z3b = zsb.reshape(nb, 1, PROJ_B)

    a_ret_s, ret_state_s = _retention_sample(z3, state_ret[0], ret_norm[0], past)

    cache_c = cache_cmp_kv[0].reshape(n_pool, PAGE_SIZE // CMP_STRIDE, CMP_STRIDE, 2 * NSA_KV_HEADS, NSA_DK)
    cache_s = cache_sel_kv[0].reshape(n_pool, PAGE_SIZE, 2 * NSA_KV_HEADS, NSA_DK)
    ac_s = _cmp_stage1_paged(cache_c, page_table, w1ab)
    kcvc_s = _cmp_stage2(ac_s, pos8, w1a, w1b, w2, NSA_KV_HEADS)
    ncs = past // CMP_STRIDE
    kcvc_s = kcvc_s.reshape(nb, 2, ncs, KV_W)
    ns_s = past // SEL_BLOCK + 1
    nsp_s = -(-ns_s // LANE) * LANE
    assert past >= WINDOW and past >= REL_MAX_DIST and BIAS_DISTS > WINDOW
    far_s = bias_d[..., REL_MAX_DIST:REL_MAX_DIST + 1]
    hg = (NSA_KV_HEADS, NSA_GROUP)
    n_valid = (past - (CMP_BLOCK - 1)) // CMP_STRIDE + 1
    strided = bias_d[..., (past - (CMP_BLOCK - 1)) % CMP_STRIDE::CMP_STRIDE]
    n_tab = strided.shape[-1]
    assert n_valid >= n_tab and n_tab * CMP_STRIDE > REL_MAX_DIST + CMP_STRIDE and ncs >= n_valid
    bias_cs = jnp.concatenate([jnp.broadcast_to(far_s, hg + (n_valid - n_tab,)), strided[..., ::-1],
                               jnp.full(hg + (ncs - n_valid,), NEG_INF, F32)], axis=-1)
    bias_cs = _pad_group_rows(bias_cs, 1)
    nblk = jnp.arange(ncs)[:, None]
    sblk = jnp.arange(nsp_s)[None, :]
    ov_s = ((nblk >= 4 * sblk - 1) & (nblk <= 4 * sblk + 3)).astype(BF16)
    o_cmp_s, idx_s = _nsa_sample_cmp(z3, kcvc_s, bias_cs, ov_s, ns_s)
    n_sel = min(SEL_TOPK, ns_s)
    idx = idx_s[:, :, 0, :n_sel]

    hg = (NSA_KV_HEADS, NSA_GROUP)
    n_key = ns_s * SEL_BLOCK
    bias_sel = jnp.concatenate([jnp.broadcast_to(far_s, hg + (past + 1 - REL_MAX_DIST,)),
                                bias_d[..., :REL_MAX_DIST][..., ::-1],
                                jnp.full(hg + (n_key - past - 1,), NEG_INF, F32)], axis=-1)
    bias_sel = _pad_group_rows(bias_sel.reshape(hg + (ns_s, SEL_BLOCK)).transpose(0, 2, 1, 3), 2)
    own_k = jnp.arange(2 * NSA_KV_HEADS)[None, :] == jnp.arange(NSA_KV_HEADS)[:, None]
    bias_sel = jnp.where(own_k[:, None, None, None, :], bias_sel[..., None], NEG_INF)
    bias_sel = bias_sel.reshape(NSA_KV_HEADS, ns_s, SROWS, SEL_BLOCK * 2 * NSA_KV_HEADS)
    kvs_new = zs[:, COL_KVS:COL_KVS + 2 * KV_W].reshape(nb, 2 * NSA_KV_HEADS, NSA_DK)
    o_sel_s = _nsa_sample_sel(z3, cache_s, kvs_new, page_table, idx, bias_sel, ns_s)

    win_buf = cache_win_kv[0].reshape(nb, WINDOW, 2 * KV_W)
    bias_w = jnp.concatenate([jnp.full(hg + (1,), NEG_INF, F32), bias_d[..., 1:WINDOW][..., ::-1]], axis=-1)
    o_win_s = _nsa_sample_win(z3, win_buf, _pad_group_rows(bias_w, 1), _pad_group_rows(bias_d[..., 0:1], 1))
    a_nsa_s = _nsa_sample_gate(o_cmp_s, o_sel_s, o_win_s, z3b)

    mem_kv_s = cache_mem_kv[0].reshape(nb, n_mem, 2 * MEM_W)
    a_mem_s = _mem_sample(z3b, mem_kv_s)

    merged_s = _merge(a_ret_s.reshape(nb, RET_W), a_nsa_s.reshape(nb, NSA_W), a_mem_s.reshape(nb, MEM_W),
                      w_ret, w_nsa, w_mem, zsb, nb, 512)
    y_s = _out_proj(merged_s, w_o, xs, norm_post[0], nb).reshape(nb, 1, D_MODEL)

    kvs_shape = (1, nb, 1, 2, NSA_KV_HEADS, NSA_DK)
    new_cmp_s = zs[:, COL_KVC:COL_KVC + 2 * KV_W].reshape(kvs_shape)
    new_sel_s = zs[:, COL_KVS:COL_KVS + 2 * KV_W].reshape(kvs_shape)
    kvw_s = zs[:, COL_KVW:COL_KVW + 2 * KV_W].reshape(nb, 1, 2, NSA_KV_HEADS, NSA_DK)
    new_win_s = jnp.concatenate([cache_win_kv[0][:, 1:], kvw_s], axis=1)[None]
    new_ret_s = ret_state_s[None]

    return (y_p, y_s, new_cmp_p, new_sel_p, new_win_p, new_ret_p, new_mem_p,
            new_cmp_s, new_sel_s, new_win_s, new_ret_s)
```

```python
import functools
import math

import jax
import jax.numpy as jnp
from jax import lax
from jax.experimental import pallas as pl
from jax.experimental.pallas import tpu as pltpu

F32 = jnp.float32
BF16 = jnp.bfloat16

D_MODEL = 2048
PAGE_SIZE = 128
RET_HEADS = 8
RET_DK = 256
RET_DV = 256
RET_CHUNK = 128
ROPE_BASE = 10000.0
NSA_HEADS = 16
NSA_KV_HEADS = 4
NSA_GROUP = NSA_HEADS // NSA_KV_HEADS
NSA_DK = 128
CMP_BLOCK = 32
CMP_STRIDE = 16
SEL_BLOCK = 64
SEL_TOPK = 16
WINDOW = 512
MEM_HEADS = 4
MEM_DH = 384
REL_BUCKETS = 32
REL_MAX_EXACT = 16
REL_MAX_DIST = 128
N_BRANCHES = 3
EPS = 1e-6
NEG_INF = -1e30
FORCE_SCORE = 1e4

RET_W = RET_HEADS * RET_DV
NSA_W = NSA_HEADS * NSA_DK
KV_W = NSA_KV_HEADS * NSA_DK
MEM_W = MEM_HEADS * MEM_DH

COL_RQ = 0
COL_RK = COL_RQ + RET_HEADS * RET_DK
COL_RV = COL_RK + RET_HEADS * RET_DK
COL_RG = COL_RV + RET_W
COL_NQ = COL_RG + RET_W
COL_KVC = COL_NQ + NSA_W
COL_KVS = COL_KVC + 2 * KV_W
COL_KVW = COL_KVS + 2 * KV_W
PROJ_A = COL_KVW + 2 * KV_W
COL_MQ = 0
COL_NSL = COL_MQ + MEM_W
COL_MG = COL_NSL + NSA_W
COL_NG = COL_MG + N_BRANCHES * D_MODEL
NG_SLOT = NSA_KV_HEADS * 128
PROJ_B = COL_NG + NG_SLOT

LOG2E = math.log2(math.e)
LANE = 128
TQ = 256
TK = 256
NSA_ROWS = NSA_GROUP * TQ
BIAS_DISTS = 1024
RET_STEP_CHUNKS = 8
PROJ_TN = 1024
VMEM_LIMIT = 56 * 1024 * 1024


def _cparams(n_axes):
    return pltpu.CompilerParams(dimension_semantics=("arbitrary",) * n_axes, vmem_limit_bytes=VMEM_LIMIT)


def _nt(a, b):
    return lax.dot_general(a, b, (((1,), (1,)), ((), ())), preferred_element_type=F32)


def _dot(a, b):
    return jnp.dot(a, b, preferred_element_type=F32)


def _sigmoid(x):
    return 1.0 / (1.0 + jnp.exp(-x))


def _iota(shape, dim):
    return lax.broadcasted_iota(jnp.int32, shape, dim)


def _rmsnorm_kernel(x_ref, g_ref, h_ref):
    x = x_ref[...]
    ms = jnp.mean(x * x, axis=-1, keepdims=True)
    h_ref[...] = ((x * lax.rsqrt(ms + EPS)) * g_ref[...]).astype(BF16)


def _rmsnorm(x, g, tm):
    m, k = x.shape
    return pl.pallas_call(
        _rmsnorm_kernel,
        grid=(m // tm,),
        in_specs=[pl.BlockSpec((tm, k), lambda i: (i, 0)), pl.BlockSpec((1, k), lambda i: (0, 0))],
        out_specs=pl.BlockSpec((tm, k), lambda i: (i, 0)),
        out_shape=jax.ShapeDtypeStruct((m, k), BF16),
        compiler_params=_cparams(1),
        name="rmsnorm",
    )(x, g.reshape(1, k))


def _proj_kernel(h_ref, w_ref, o_ref, wb_ref, *, transposed):
    @pl.when(pl.program_id(1) == 0)
    def _():
        w = w_ref[...]
        wb_ref[...] = (w.T if transposed else w).astype(BF16)

    o_ref[...] = _dot(h_ref[...], wb_ref[...])


def _proj(h, w, tm, tn, n=None, transposed=False):
    m, k = h.shape
    n = w.shape[0 if transposed else 1] if n is None else n
    wspec = pl.BlockSpec((tn, k), lambda j, i: (j, 0)) if transposed else pl.BlockSpec((k, tn), lambda j, i: (0, j))
    return pl.pallas_call(
        functools.partial(_proj_kernel, transposed=transposed),
        grid=(n // tn, m // tm),
        in_specs=[pl.BlockSpec((tm, k), lambda j, i: (i, 0)), wspec],
        out_specs=pl.BlockSpec((tm, tn), lambda j, i: (i, j)),
        out_shape=jax.ShapeDtypeStruct((m, n), F32),
        scratch_shapes=[pltpu.VMEM((k, tn), BF16)],
        compiler_params=_cparams(2),
        name="proj",
    )(h, w)


def _rope_rows(x, cos, sin):
    half = x.shape[-1] // 2
    x1, x2 = x[:, :half], x[:, half:]
    return jnp.concatenate([x1 * cos - x2 * sin, x1 * sin + x2 * cos], axis=-1)


def _head_norm_gate(o, gnorm, rg):
    oc = o - jnp.mean(o, axis=-1, keepdims=True)
    y = oc * lax.rsqrt(jnp.mean(oc * oc, axis=-1, keepdims=True) + EPS) * gnorm
    return y * (rg * _sigmoid(rg))


def _ret_prompt_kernel(q_ref, k_ref, v_ref, rg_ref, cos_ref, sin_ref, dmat_ref, xi_ref, zeta_ref, gc_ref,
                       gn_ref, a_ref, s_ref):
    @pl.when(pl.program_id(2) == 0)
    def _():
        s_ref[...] = jnp.zeros_like(s_ref)

    c = RET_CHUNK
    for t in range(q_ref.shape[0] // c):
        rows = slice(t * c, (t + 1) * c)
        cos, sin = cos_ref[rows, :], sin_ref[rows, :]
        q = _rope_rows(q_ref[rows, :], cos, sin)
        k = _rope_rows(k_ref[rows, :], cos, sin) * (RET_DK ** -0.5)
        qb, vb = q.astype(BF16), v_ref[rows, :].astype(BF16)
        state = s_ref[0, 0]
        inner = _nt(qb, k.astype(BF16)) * dmat_ref[0]
        o = _dot(inner.astype(BF16), vb) + _dot(qb, state.astype(BF16)) * xi_ref[0]
        kz_t = (k * zeta_ref[0]).T.astype(BF16)
        s_ref[0, 0] = state * gc_ref[0] + _dot(kz_t, vb)
        a_ref[rows, :] = _head_norm_gate(o, gn_ref[...], rg_ref[rows, :]).astype(BF16)


def _decay_tables(chunk):
    log_g = jnp.log1p(-jnp.exp2(-5.0 - jnp.arange(RET_HEADS, dtype=F32)))
    i = jnp.arange(chunk, dtype=F32)
    diff = i[:, None] - i[None, :]
    dmat = jnp.where(diff >= 0, jnp.exp(log_g[:, None, None] * jnp.maximum(diff, 0.0)), 0.0)
    xi = jnp.exp(log_g[:, None] * (i[None, :] + 1.0))[:, :, None]
    zeta = jnp.exp(log_g[:, None] * (chunk - 1.0 - i[None, :]))[:, :, None]
    g_chunk = jnp.exp(log_g * chunk)[:, None, None]
    return dmat, xi, zeta, g_chunk


def _rope_tables(pos):
    half = RET_DK // 2
    freq = jnp.power(ROPE_BASE, -jnp.arange(half, dtype=F32) / half)
    ang = pos.astype(F32)[:, None] * freq[None, :]
    return jnp.cos(ang), jnp.sin(ang)


def _retention_prompt(z, ret_norm, batch, seq):
    c = RET_CHUNK
    rows = _pick_tile(seq, RET_STEP_CHUNKS * c)
    nc = seq // rows
    dmat, xi, zeta, g_chunk = _decay_tables(c)
    cos, sin = _rope_tables(jnp.arange(seq))
    hb = RET_DK

    def zspec(col0):
        return pl.BlockSpec((rows, hb), lambda b, h, t, col0=col0: (b * nc + t, col0 // hb + h))

    per_head = lambda shape: pl.BlockSpec((1,) + shape, lambda b, h, t: (h, 0, 0))
    return pl.pallas_call(
        _ret_prompt_kernel,
        grid=(batch, RET_HEADS, nc),
        in_specs=[zspec(COL_RQ), zspec(COL_RK), zspec(COL_RV), zspec(COL_RG),
                  pl.BlockSpec((rows, hb // 2), lambda b, h, t: (t, 0)),
                  pl.BlockSpec((rows, hb // 2), lambda b, h, t: (t, 0)),
                  per_head((c, c)), per_head((c, 1)), per_head((c, 1)), per_head((1, 1)),
                  pl.BlockSpec((1, hb), lambda b, h, t: (0, h))],
        out_specs=[pl.BlockSpec((rows, hb), lambda b, h, t: (b * nc + t, h)),
                   pl.BlockSpec((1, 1, RET_DK, RET_DV), lambda b, h, t: (b, h, 0, 0))],
        out_shape=[jax.ShapeDtypeStruct((batch * seq, RET_W), BF16),
                   jax.ShapeDtypeStruct((batch, RET_HEADS, RET_DK, RET_DV), F32)],
        compiler_params=_cparams(3),
        name="retention_prompt",
    )(z, z, z, z, cos, sin, dmat, xi, zeta, g_chunk, ret_norm.reshape(1, RET_W))


def _column_of(row):
    n = row.shape[1]
    eye = _iota((n, n), 0) == _iota((n, n), 1)
    return jnp.sum(jnp.where(eye, jnp.broadcast_to(row, (n, n)), 0.0), axis=1, keepdims=True)


def _ret_sample_kernel(q_ref, k_ref, v_ref, rg_ref, cos_ref, sin_ref, gam_ref, gn_ref, s_ref, a_ref, so_ref):
    cos, sin = cos_ref[...], sin_ref[...]
    q = _rope_rows(q_ref[0], cos, sin)
    k = _rope_rows(k_ref[0], cos, sin) * (RET_DK ** -0.5)
    v = v_ref[0]
    state = s_ref[0, 0]
    gamma = gam_ref[0]
    qk = jnp.sum(q * k, axis=-1, keepdims=True)
    o = qk * v + jnp.sum(_column_of(q) * state, axis=0, keepdims=True) * gamma
    so_ref[0, 0] = state * gamma + _column_of(k) * v
    a_ref[0] = _head_norm_gate(o, gn_ref[...], rg_ref[0]).astype(BF16)


def _retention_sample(z3, state, ret_norm, pos):
    nb = z3.shape[0]
    cos, sin = _rope_tables(jnp.full((1,), pos))
    gamma = jnp.exp(jnp.log1p(-jnp.exp2(-5.0 - jnp.arange(RET_HEADS, dtype=F32))))[:, None, None]
    hb = RET_DK

    def zspec(col0):
        return pl.BlockSpec((1, 1, hb), lambda b, h, col0=col0: (b, 0, col0 // hb + h))

    st_spec = pl.BlockSpec((1, 1, RET_DK, RET_DV), lambda b, h: (b, h, 0, 0))
    return pl.pallas_call(
        _ret_sample_kernel,
        grid=(nb, RET_HEADS),
        in_specs=[zspec(COL_RQ), zspec(COL_RK), zspec(COL_RV), zspec(COL_RG),
                  pl.BlockSpec((1, hb // 2), lambda b, h: (0, 0)),
                  pl.BlockSpec((1, hb // 2), lambda b, h: (0, 0)),
                  pl.BlockSpec((1, 1, 1), lambda b, h: (h, 0, 0)),
                  pl.BlockSpec((1, hb), lambda b, h: (0, h)),
                  st_spec],
        out_specs=[pl.BlockSpec((1, 1, hb), lambda b, h: (b, 0, h)), st_spec],
        out_shape=[jax.ShapeDtypeStruct((nb, 1, RET_W), BF16),
                   jax.ShapeDtypeStruct(state.shape, F32)],
        compiler_params=_cparams(2),
        name="retention_sample",
    )(z3, z3, z3, z3, cos, sin, gamma, ret_norm.reshape(1, RET_W), state)


def _half_rows(ref_slice_fn, n_half):
    return jnp.concatenate([ref_slice_fn(p) for p in range(CMP_STRIDE)], axis=1)


def _cmp_stage1_dense_kernel(x_ref, w_ref, o_ref):
    nh = o_ref.shape[3]
    x = _half_rows(lambda p: x_ref[pl.ds(p, nh, stride=CMP_STRIDE), :], nh).astype(BF16)
    o_ref[0, 0, 0] = _dot(x, w_ref[0])


def _cmp_stage1_dense(z, w1ab, batch, seq):
    nh = seq // CMP_STRIDE
    return pl.pallas_call(
        _cmp_stage1_dense_kernel,
        grid=(batch, 2, NSA_KV_HEADS),
        in_specs=[pl.BlockSpec((seq, NSA_DK), lambda b, kv, h: (b, COL_KVC // NSA_DK + kv * NSA_KV_HEADS + h)),
                  pl.BlockSpec((1, CMP_STRIDE * NSA_DK, 2 * NSA_DK), lambda b, kv, h: (kv, 0, 0))],
        out_specs=pl.BlockSpec((1, 1, 1, nh, 2 * NSA_DK), lambda b, kv, h: (b, kv, h, 0, 0)),
        out_shape=jax.ShapeDtypeStruct((batch, 2, NSA_KV_HEADS, nh, 2 * NSA_DK), F32),
        compiler_params=_cparams(3),
        name="cmp_stage1_dense",
    )(z, w1ab)


PAGES_PER_STEP = 16


def _cmp_stage1_paged_kernel(pt_ref, *refs):
    pages, (w_ref, o_ref) = refs[:PAGES_PER_STEP], refs[PAGES_PER_STEP:]
    hp = PAGE_SIZE // CMP_STRIDE
    top = _iota((2 * NSA_KV_HEADS, NSA_DK), 0) < NSA_KV_HEADS
    cols = [[], []]
    for p in range(CMP_STRIDE):
        tiles = [[], []]
        for pg in pages:
            xp = pg[0, :, p]
            for n in range(0, hp, 2):
                a, b = xp[n], xp[n + 1]
                tiles[0].append(jnp.where(top, a, pltpu.roll(b, NSA_KV_HEADS, 0)))
                tiles[1].append(jnp.where(top, pltpu.roll(a, NSA_KV_HEADS, 0), b))
        for kv in range(2):
            cols[kv].append(jnp.concatenate(tiles[kv], axis=0))
    for kv in range(2):
        x = jnp.concatenate(cols[kv], axis=1).astype(BF16)
        o_ref[0, kv, 0] = _dot(x, w_ref[kv])


def _cmp_stage1_paged(cache, page_table, w1ab):
    nb, n_pages = page_table.shape
    hp = PAGE_SIZE // CMP_STRIDE
    steps = n_pages // PAGES_PER_STEP
    rows = PAGES_PER_STEP * hp * NSA_KV_HEADS

    def page_spec(j):
        return pl.BlockSpec((1, hp, CMP_STRIDE, 2 * NSA_KV_HEADS, NSA_DK),
                            lambda b, s, pt, j=j: (pt[b, s * PAGES_PER_STEP + j], 0, 0, 0, 0))

    wspec = pl.BlockSpec((2, CMP_STRIDE * NSA_DK, 2 * NSA_DK), lambda b, s, pt: (0, 0, 0))
    return pl.pallas_call(
        _cmp_stage1_paged_kernel,
        grid_spec=pltpu.PrefetchScalarGridSpec(
            num_scalar_prefetch=1,
            grid=(nb, steps),
            in_specs=[page_spec(j) for j in range(PAGES_PER_STEP)] + [wspec],
            out_specs=pl.BlockSpec((1, 2, 1, rows, 2 * NSA_DK), lambda b, s, pt: (b, 0, 0, s, 0))),
        out_shape=jax.ShapeDtypeStruct((nb, 2, 1, steps * rows, 2 * NSA_DK), F32),
        compiler_params=_cparams(2),
        name="cmp_stage1_paged",
    )(page_table, *([cache] * PAGES_PER_STEP), w1ab)


def _cmp_stage2_kernel(ac_ref, pos_ref, w1a_ref, w1b_ref, w2_ref, o_ref, *, shift):
    ac = ac_ref[0, 0, 0]
    nh = ac.shape[0]
    pos = pos_ref[0].astype(BF16)
    kw = CMP_STRIDE * NSA_DK
    pe = _dot(pos[:, :kw], w1a_ref[0]) + _dot(pos[:, kw:], w1b_ref[0])
    pre = ac[:, :NSA_DK] + pltpu.roll(ac[:, NSA_DK:], nh - shift, 0) + pe[0:1]
    gelu = 0.5 * pre * (1.0 + jnp.tanh(math.sqrt(2.0 / math.pi) * (pre + 0.044715 * (pre * pre * pre))))
    o_ref[0, 0, 0] = _dot(gelu.astype(BF16), w2_ref[0]).astype(BF16)


def _cmp_stage2(ac, pos8, w1a, w1b, w2, shift):
    nb, _, groups, nh, _ = ac.shape
    kw = CMP_STRIDE * NSA_DK
    return pl.pallas_call(
        functools.partial(_cmp_stage2_kernel, shift=shift),
        grid=(nb, 2, groups),
        in_specs=[pl.BlockSpec((1, 1, 1, nh, 2 * NSA_DK), lambda b, kv, h: (b, kv, h, 0, 0)),
                  pl.BlockSpec((1, 8, 2 * kw), lambda b, kv, h: (kv, 0, 0)),
                  pl.BlockSpec((1, kw, NSA_DK), lambda b, kv, h: (kv, 0, 0)),
                  pl.BlockSpec((1, kw, NSA_DK), lambda b, kv, h: (kv, 0, 0)),
                  pl.BlockSpec((1, NSA_DK, NSA_DK), lambda b, kv, h: (kv, 0, 0))],
        out_specs=pl.BlockSpec((1, 1, 1, nh, NSA_DK), lambda b, kv, h: (b, kv, h, 0, 0)),
        out_shape=jax.ShapeDtypeStruct((nb, 2, groups, nh, NSA_DK), BF16),
        compiler_params=_cparams(3),
        name="cmp_stage2",
    )(ac, pos8, w1a, w1b, w2)


def _rel_bucket(dist):
    n = jnp.maximum(dist, 0)
    nf = jnp.maximum(n, 1).astype(F32)
    scale = (REL_BUCKETS - REL_MAX_EXACT) / math.log(REL_MAX_DIST / REL_MAX_EXACT)
    large = REL_MAX_EXACT + (jnp.log(nf / REL_MAX_EXACT) * scale).astype(jnp.int32)
    large = jnp.minimum(large, REL_BUCKETS - 1)
    return jnp.where(n < REL_MAX_EXACT, n, large)


def _bias_by_dist(rel_table, n):
    tab = rel_table[_rel_bucket(jnp.arange(n))]
    return tab.T.reshape(NSA_KV_HEADS, NSA_GROUP, n)


def _pad_group_rows(t, axis):
    first = lax.slice_in_dim(t, 0, 1, axis=axis)
    return jnp.concatenate([t] + [first] * (SROWS - NSA_GROUP), axis=axis)


def _flash_init(m_ref, l_ref, acc_ref):
    m_ref[...] = jnp.full(m_ref.shape, NEG_INF, F32)
    l_ref[...] = jnp.zeros(l_ref.shape, F32)
    acc_ref[...] = jnp.zeros(acc_ref.shape, F32)


def _flash_step(s, v, m_ref, l_ref, acc_ref):
    m_old = m_ref[...]
    m_new = jnp.maximum(m_old, jnp.max(s, axis=1, keepdims=True))
    alpha = jnp.exp(m_old - m_new)
    p = jnp.exp(s - m_new)
    l_ref[...] = alpha * l_ref[...] + jnp.sum(p, axis=1, keepdims=True)
    acc_ref[...] = alpha * acc_ref[...] + _dot(p.astype(BF16), v)
    m_ref[...] = m_new


def _flash_result(l_ref, acc_ref):
    return acc_ref[...] / jnp.maximum(l_ref[...], 1e-30)


def _select_blocks(imp_t, q0, ns):
    shape = imp_t.shape
    blk = _iota(shape, 0)
    qpos = q0 + _iota(shape, 1)
    cur = qpos >> 6
    valid = blk * SEL_BLOCK <= qpos
    forced = (blk == 0) | (blk == cur) | (blk == cur - 1)
    imp_t = jnp.where(valid, jnp.where(forced, FORCE_SCORE, imp_t), NEG_INF)
    rank = jnp.zeros(shape, F32)
    for other in range(ns):
        row = imp_t[other:other + 1, :]
        ahead = (row > imp_t) | ((row == imp_t) & (blk > other))
        rank = rank + jnp.where(ahead, 1.0, 0.0)
    return jnp.where((rank < SEL_TOPK) & valid, 1.0, 0.0)


def _nsa_prompt_kernel(q_ref, ks_ref, vs_ref, kw_ref, vw_ref, kc_ref, vct_ref, tz_ref, cfar_ref, basec_ref,
                       ovt_ref, ng_ref, nsl_ref, o_ref, m_ref, l_ref, acc_ref, vst_ref, vwt_ref, *, ns):
    i = pl.program_id(2)
    q0 = i * TQ
    qall = q_ref[...] * ((NSA_DK ** -0.5) * LOG2E)
    qt = jnp.concatenate([qall[:, g * NSA_DK:(g + 1) * NSA_DK].T for g in range(NSA_GROUP)], axis=1).astype(BF16)
    c_loc = _iota((TK, NSA_ROWS), 0)
    r_loc = _iota((TK, NSA_ROWS), 1) & (TQ - 1)

    @pl.when(i == 0)
    def _():
        for kt in range(vst_ref.shape[0]):
            vst_ref[kt] = vs_ref[kt * TK:(kt + 1) * TK, :].T.astype(BF16)
            vwt_ref[kt] = vw_ref[kt * TK:(kt + 1) * TK, :].T.astype(BF16)

    ncp = kc_ref.shape[3]
    shift = (TQ // CMP_STRIDE) * i
    bias_c = basec_ref[0, pl.ds(pl.multiple_of(ncp - shift, TQ // CMP_STRIDE), ncp), :]
    s = _dot(kc_ref[0, 0, 0], qt) + bias_c
    m = jnp.max(s, axis=0, keepdims=True)
    e = jnp.exp2(s - m)
    inv = jnp.where(m > 0.5 * NEG_INF, 1.0 / jnp.maximum(jnp.sum(e, axis=0, keepdims=True), 1e-30), 0.0)
    p = e * inv
    o_cmp = _dot(vct_ref[0, 0], p.astype(BF16))

    psum = p[:, 0:TQ] + p[:, TQ:2 * TQ] + p[:, 2 * TQ:3 * TQ] + p[:, 3 * TQ:4 * TQ]
    hi = psum.astype(BF16)
    lo = (psum - hi.astype(F32)).astype(BF16)
    ovt = ovt_ref[...]
    imp_t = _dot(ovt, hi) + _dot(ovt, lo)
    ns8 = -(-ns // 8) * 8
    sel_t = _select_blocks(imp_t[:ns8], q0, ns)
    nsp = ovt.shape[0]
    sel_neg = jnp.where(sel_t > 0.5, 0.0, NEG_INF)
    if ns8 < nsp:
        sel_neg = jnp.concatenate([sel_neg, jnp.full((nsp - ns8, TQ), NEG_INF, F32)], axis=0)
    sel_neg = sel_neg.astype(BF16)

    def flash_step(scs, vts):
        m_old = m_ref[...]
        m_new = m_old
        for sc in scs:
            m_new = jnp.maximum(m_new, jnp.max(sc, axis=0, keepdims=True))
        alpha = jnp.exp2(m_old - m_new)
        l_new = alpha * l_ref[...]
        acc = alpha * acc_ref[...]
        for sc, vt in zip(scs, vts):
            pt = jnp.exp2(sc - m_new)
            l_new = l_new + jnp.sum(pt, axis=0, keepdims=True)
            acc = acc + _dot(vt, pt.astype(BF16))
        l_ref[...] = l_new
        acc_ref[...] = acc
        m_ref[...] = m_new

    def sel_scores(kt, bias, causal):
        k = ks_ref[pl.ds(pl.multiple_of(kt * TK, TK), TK), :].astype(BF16)
        blk_of_key = (TK // SEL_BLOCK) * kt + (_iota((TK, nsp), 0) >> 6)
        expand = jnp.where(_iota((TK, nsp), 1) == blk_of_key, 1.0, 0.0).astype(BF16)
        mk = _dot(expand, sel_neg)
        sc = _dot(k, qt) + bias + jnp.concatenate([mk] * NSA_GROUP, axis=1)
        if causal:
            sc = jnp.where(c_loc <= r_loc, sc, NEG_INF)
        return sc

    _flash_init(m_ref, l_ref, acc_ref)

    @pl.when(i == 0)
    def _():
        flash_step([sel_scores(i, tz_ref[0, 0], True)], [vst_ref[i]])

    @pl.when(i >= 1)
    def _():
        flash_step([sel_scores(i, tz_ref[0, 0], True), sel_scores(i - 1, tz_ref[0, 1], False)],
                   [vst_ref[i], vst_ref[i - 1]])

    def far_pair(j, carry):
        kt = i - 2 - 2 * j
        flash_step([sel_scores(kt, cfar_ref[0], False), sel_scores(kt - 1, cfar_ref[0], False)],
                   [vst_ref[kt], vst_ref[kt - 1]])
        return carry

    n_far = jnp.maximum(i - 1, 0)
    lax.fori_loop(0, n_far >> 1, far_pair, 0)

    @pl.when((n_far & 1) == 1)
    def _():
        flash_step([sel_scores(0, cfar_ref[0], False)], [vst_ref[0]])

    o_sel = _flash_result(l_ref, acc_ref)

    def win_scores(off):
        kt = i - off
        k = kw_ref[pl.ds(pl.multiple_of(kt * TK, TK), TK), :].astype(BF16)
        sc = _dot(k, qt) + (tz_ref[0, off] if off < 2 else cfar_ref[0])
        if off == 0:
            sc = jnp.where(c_loc <= r_loc, sc, NEG_INF)
        if off * TK == WINDOW:
            sc = jnp.where(c_loc > r_loc, sc, NEG_INF)
        return sc

    def win_tiles(n_tiles):
        flash_step([win_scores(off) for off in range(n_tiles)], [vwt_ref[i - off] for off in range(n_tiles)])

    _flash_init(m_ref, l_ref, acc_ref)
    n_win = WINDOW // TK + 1
    for n_tiles in range(1, n_win + 1):
        cond = (i == n_tiles - 1) if n_tiles < n_win else (i >= n_win - 1)
        pl.when(cond)(functools.partial(win_tiles, n_tiles))
    o_win = _flash_result(l_ref, acc_ref)

    gates_t = _sigmoid(ng_ref[...]).T
    nsl = nsl_ref[...]
    outs = []
    for g in range(NSA_GROUP):
        cols = slice(g * TQ, (g + 1) * TQ)
        o_t = (gates_t[3 * g:3 * g + 1] * o_cmp[:, cols] + gates_t[3 * g + 1:3 * g + 2] * o_sel[:, cols]
               + gates_t[3 * g + 2:3 * g + 3] * o_win[:, cols])
        x = nsl[:, g * NSA_DK:(g + 1) * NSA_DK]
        outs.append(o_t.T * (x * _sigmoid(x)))
    o_ref[...] = jnp.concatenate(outs, axis=1).astype(BF16)


def _lanes_by_head(t):
    hk, g, rows, tq = t.shape
    return t.transpose(0, 2, 1, 3).reshape(hk, rows, g * tq)


def _nsa_prompt(z, zb, kcvc, bias_d, batch, seq):
    nq = seq // TQ
    ns = seq // SEL_BLOCK
    ncp = kcvc.shape[3]
    nsp = LANE
    wn = TQ // CMP_STRIDE
    assert TQ == TK and ns <= nsp and ncp >= seq // CMP_STRIDE and ncp % LANE == 0 and ncp > wn
    gw = NSA_GROUP * NSA_DK
    n_dist = bias_d.shape[-1]
    assert n_dist >= 2 * TK + TQ
    bias_d = bias_d * LOG2E

    rep = jnp.tile(bias_d, (1, 1, TK + 1))[..., :TK * (n_dist - 1)].reshape(NSA_KV_HEADS, NSA_GROUP, TK, n_dist - 1)
    tz = jnp.stack([_lanes_by_head(rep[..., 0:TQ]), _lanes_by_head(rep[..., TK:TK + TQ])], axis=1)
    far = jnp.broadcast_to(bias_d[..., REL_MAX_DIST][:, :, None, None], (NSA_KV_HEADS, NSA_GROUP, 1, TQ))
    cfar = _lanes_by_head(far)
    half = n_dist // 2
    start = CMP_STRIDE * wn - (CMP_BLOCK - 1)
    assert start + TQ <= half and 2 * wn * CMP_STRIDE - start <= half
    w_ext = jnp.concatenate([bias_d[..., :half], jnp.full(bias_d.shape[:-1] + (n_dist - half,), NEG_INF, F32)], -1)
    near = jnp.tile(w_ext, (1, 1, 2 * wn + 1))[..., :2 * wn * (n_dist - CMP_STRIDE)]
    near = near.reshape(NSA_KV_HEADS, NSA_GROUP, 2 * wn, n_dist - CMP_STRIDE)[..., start:start + TQ]
    basec = jnp.concatenate([jnp.broadcast_to(far, (NSA_KV_HEADS, NSA_GROUP, ncp - wn, TQ)), near,
                             jnp.full((NSA_KV_HEADS, NSA_GROUP, ncp - wn, TQ), NEG_INF, F32)], axis=2)
    basec = _lanes_by_head(basec)
    sblk = jnp.arange(nsp)[:, None]
    nblk = jnp.arange(ncp)[None, :]
    ovt = ((nblk >= 4 * sblk - 1) & (nblk <= 4 * sblk + 3)).astype(BF16)

    vct = kcvc[:, 1].transpose(0, 1, 3, 2)

    def kvspec(col0, which):
        return pl.BlockSpec((seq, NSA_DK), lambda b, h, i: (b, col0 // NSA_DK + which * NSA_KV_HEADS + h))

    vt_scratch = pltpu.VMEM((seq // TK, NSA_DK, TK), BF16)
    return pl.pallas_call(
        functools.partial(_nsa_prompt_kernel, ns=ns),
        grid=(batch, NSA_KV_HEADS, nq),
        in_specs=[pl.BlockSpec((TQ, gw), lambda b, h, i: (b * nq + i, COL_NQ // gw + h)),
                  kvspec(COL_KVS, 0), kvspec(COL_KVS, 1), kvspec(COL_KVW, 0), kvspec(COL_KVW, 1),
                  pl.BlockSpec((1, 1, 1, ncp, NSA_DK), lambda b, h, i: (b, 0, h, 0, 0)),
                  pl.BlockSpec((1, 1, NSA_DK, ncp), lambda b, h, i: (b, h, 0, 0)),
                  pl.BlockSpec((1, 2, TK, NSA_ROWS), lambda b, h, i: (h, 0, 0, 0)),
                  pl.BlockSpec((1, 1, NSA_ROWS), lambda b, h, i: (h, 0, 0)),
                  pl.BlockSpec((1, 2 * ncp, NSA_ROWS), lambda b, h, i: (h, 0, 0)),
                  pl.BlockSpec((nsp, ncp), lambda b, h, i: (0, 0)),
                  pl.BlockSpec((TQ, LANE), lambda b, h, i: (b * nq + i, COL_NG // LANE + h)),
                  pl.BlockSpec((TQ, gw), lambda b, h, i: (b * nq + i, COL_NSL // gw + h))],
        out_specs=pl.BlockSpec((TQ, gw), lambda b, h, i: (b * nq + i, h)),
        out_shape=jax.ShapeDtypeStruct((batch * seq, NSA_W), BF16),
        scratch_shapes=[pltpu.VMEM((1, NSA_ROWS), F32), pltpu.VMEM((1, NSA_ROWS), F32),
                        pltpu.VMEM((NSA_DK, NSA_ROWS), F32), vt_scratch, vt_scratch],
        compiler_params=_cparams(3),
        name="nsa_prompt",
    )(z, z, z, z, z, kcvc, vct, tz, cfar, basec, ovt, zb, zb)


SROWS = 8


def _stack_group_q(q_row):
    heads = [q_row[:, g * NSA_DK:(g + 1) * NSA_DK] for g in range(NSA_GROUP)]
    return jnp.concatenate(heads + [heads[0]] * (SROWS - NSA_GROUP), axis=0)


def _nsa_sample_cmp_kernel(q_ref, kc_ref, vc_ref, bias_ref, ov_ref, o_ref, idx_ref, *, ns):
    scale = NSA_DK ** -0.5
    qs = _stack_group_q(q_ref[0]).astype(BF16)
    s = _nt(qs, kc_ref[0, 0]) * scale + bias_ref[0]
    m = jnp.max(s, axis=1, keepdims=True)
    e = jnp.exp(s - m)
    inv = jnp.where(m > 0.5 * NEG_INF, 1.0 / jnp.maximum(jnp.sum(e, axis=1, keepdims=True), 1e-30), 0.0)
    p = e * inv
    o_ref[0, 0] = _dot(p.astype(BF16), vc_ref[0, 0])
    psum = jnp.broadcast_to(jnp.sum(p[0:NSA_GROUP], axis=0, keepdims=True), p.shape)
    hi = psum.astype(BF16)
    lo = (psum - hi.astype(F32)).astype(BF16)
    imp = (_dot(hi, ov_ref[...]) + _dot(lo, ov_ref[...]))[0:1]
    nsp = imp.shape[1]
    blk_r = _iota((1, nsp), 1)
    cur = ns - 1
    forced = (blk_r == 0) | (blk_r == cur) | (blk_r == cur - 1)
    imp = jnp.where(blk_r < ns, jnp.where(forced, FORCE_SCORE, imp), 2.0 * NEG_INF)
    imp_c = _column_of(imp)
    i_r = _iota((nsp, nsp), 1)
    j_c = _iota((nsp, nsp), 0)
    ahead = (imp > imp_c) | ((imp == imp_c) & (i_r < j_c))
    rank_c = jnp.sum(jnp.where(ahead, 1.0, 0.0), axis=1, keepdims=True)
    slot = _iota((nsp, LANE), 1).astype(F32)
    picks = jnp.where(rank_c == slot, _iota((nsp, LANE), 0).astype(F32), 0.0)
    idx_ref[0, 0] = jnp.broadcast_to(jnp.sum(picks, axis=0, keepdims=True), (SROWS, LANE)).astype(jnp.int32)


def _nsa_sample_cmp(z3, kcvc, bias_c, ov, ns):
    nb = z3.shape[0]
    ncp = kcvc.shape[2]
    nsp = ov.shape[1]
    gw = NSA_GROUP * NSA_DK

    def cspec(which):
        return pl.BlockSpec((1, 1, ncp, NSA_DK), lambda b, h: (b, which, 0, h))

    return pl.pallas_call(
        functools.partial(_nsa_sample_cmp_kernel, ns=ns),
        grid=(nb, NSA_KV_HEADS),
        in_specs=[pl.BlockSpec((1, 1, gw), lambda b, h: (b, 0, COL_NQ // gw + h)),
                  cspec(0), cspec(1),
                  pl.BlockSpec((1, SROWS, ncp), lambda b, h: (h, 0, 0)),
                  pl.BlockSpec((ncp, nsp), lambda b, h: (0, 0))],
        out_specs=[pl.BlockSpec((1, 1, SROWS, NSA_DK), lambda b, h: (b, h, 0, 0)),
                   pl.BlockSpec((1, 1, SROWS, LANE), lambda b, h: (b, h, 0, 0))],
        out_shape=[jax.ShapeDtypeStruct((nb, NSA_KV_HEADS, SROWS, NSA_DK), F32),
                   jax.ShapeDtypeStruct((nb, NSA_KV_HEADS, SROWS, LANE), jnp.int32)],
        compiler_params=_cparams(2),
        name="nsa_sample_cmp",
    )(z3, kcvc, kcvc, bias_c, ov)


def _nsa_sample_sel_kernel(pt_ref, idx_ref, q_ref, *refs, ns):
    blocks, (new_ref,), biases = refs[:NSA_KV_HEADS], refs[NSA_KV_HEADS:NSA_KV_HEADS + 1], refs[NSA_KV_HEADS + 1:2 * NSA_KV_HEADS + 1]
    o_ref, m_ref, l_ref, acc_ref = refs[2 * NSA_KV_HEADS + 1:]
    b, t = pl.program_id(0), pl.program_id(1)
    gw = NSA_GROUP * NSA_DK
    rows_kv = 2 * NSA_KV_HEADS

    @pl.when(t == 0)
    def _():
        _flash_init(m_ref, l_ref, acc_ref)

    new_rows = jnp.concatenate([new_ref[0]] * SEL_BLOCK, axis=0)
    xs, scs = [], []
    for h in range(NSA_KV_HEADS):
        is_new = idx_ref[b, h, t] == ns - 1
        x = blocks[h][0].reshape(SEL_BLOCK * rows_kv, NSA_DK)
        x = jnp.where(is_new, new_rows, x).astype(BF16)
        qs = _stack_group_q(q_ref[0][:, h * gw:(h + 1) * gw]).astype(BF16)
        xs.append(x)
        scs.append(_nt(qs, x) * (NSA_DK ** -0.5) + biases[h][0, 0])
    sc = jnp.concatenate(scs, axis=0)
    m_old = m_ref[...]
    m_new = jnp.maximum(m_old, jnp.max(sc, axis=1, keepdims=True))
    alpha = jnp.exp(m_old - m_new)
    p = jnp.exp(sc - m_new)
    l_ref[...] = alpha * l_ref[...] + jnp.sum(p, axis=1, keepdims=True)
    pv = pltpu.roll(p, NSA_KV_HEADS, 1).astype(BF16)
    acc_ref[...] = alpha * acc_ref[...] + jnp.concatenate(
        [_dot(pv[h * SROWS:(h + 1) * SROWS], xs[h]) for h in range(NSA_KV_HEADS)], axis=0)
    m_ref[...] = m_new

    @pl.when(t == pl.num_programs(1) - 1)
    def _():
        o_ref[0] = (acc_ref[...] / jnp.maximum(l_ref[...], 1e-30)).reshape(NSA_KV_HEADS, SROWS, NSA_DK)


def _nsa_sample_sel(z3, cache, kv_new, page_table, idx, bias_sel, ns):
    nb, n_pages = page_table.shape
    n_sel = idx.shape[2]
    halves = PAGE_SIZE // SEL_BLOCK
    rows_kv = 2 * NSA_KV_HEADS

    def blockspec(h):
        def index(b, t, pt, ix):
            blk = ix[b, h, t]
            return (pt[b, jnp.minimum(blk // halves, n_pages - 1)], blk % halves, 0, 0)
        return pl.BlockSpec((1, SEL_BLOCK, rows_kv, NSA_DK), index)

    def biasspec(h):
        return pl.BlockSpec((1, 1, SROWS, SEL_BLOCK * rows_kv), lambda b, t, pt, ix: (h, ix[b, h, t], 0, 0))

    heads = range(NSA_KV_HEADS)
    return pl.pallas_call(
        functools.partial(_nsa_sample_sel_kernel, ns=ns),
        grid_spec=pltpu.PrefetchScalarGridSpec(
            num_scalar_prefetch=2,
            grid=(nb, n_sel),
            in_specs=[pl.BlockSpec((1, 1, NSA_W), lambda b, t, pt, ix: (b, 0, COL_NQ // NSA_W))]
                     + [blockspec(h) for h in heads]
                     + [pl.BlockSpec((1, rows_kv, NSA_DK), lambda b, t, pt, ix: (b, 0, 0))]
                     + [biasspec(h) for h in heads],
            out_specs=pl.BlockSpec((1, NSA_KV_HEADS, SROWS, NSA_DK), lambda b, t, pt, ix: (b, 0, 0, 0)),
            scratch_shapes=[pltpu.VMEM((NSA_KV_HEADS * SROWS, 1), F32), pltpu.VMEM((NSA_KV_HEADS * SROWS, 1), F32),
                            pltpu.VMEM((NSA_KV_HEADS * SROWS, NSA_DK), F32)]),
        out_shape=jax.ShapeDtypeStruct((nb, NSA_KV_HEADS, SROWS, NSA_DK), F32),
        compiler_params=_cparams(2),
        name="nsa_sample_sel",
    )(page_table, idx, z3, *([cache] * NSA_KV_HEADS), kv_new, *([bias_sel] * NSA_KV_HEADS))


def _nsa_sample_win_kernel(q_ref, k_ref, v_ref, kn_ref, vn_ref, bias_ref, bnew_ref, o_ref):
    scale = NSA_DK ** -0.5
    q = _stack_group_q(q_ref[0])
    s_buf = _nt(q.astype(BF16), k_ref[0].astype(BF16)) * scale + bias_ref[0]
    s_new = jnp.sum(q * kn_ref[0], axis=1, keepdims=True) * scale + bnew_ref[0]
    m = jnp.maximum(jnp.max(s_buf, axis=1, keepdims=True), s_new)
    p_buf = jnp.exp(s_buf - m)
    p_new = jnp.exp(s_new - m)
    l = jnp.sum(p_buf, axis=1, keepdims=True) + p_new
    acc = _dot(p_buf.astype(BF16), v_ref[0].astype(BF16)) + p_new * vn_ref[0]
    o_ref[0, 0] = acc / jnp.maximum(l, 1e-30)


def _nsa_sample_win(z3, win_buf, bias_win, bias_new):
    nb, nbuf, _ = win_buf.shape
    gw = NSA_GROUP * NSA_DK

    def bufspec(which):
        return pl.BlockSpec((1, nbuf, NSA_DK), lambda b, h: (b, 0, which * NSA_KV_HEADS + h))

    def newspec(which):
        return pl.BlockSpec((1, 1, NSA_DK), lambda b, h: (b, 0, COL_KVW // NSA_DK + which * NSA_KV_HEADS + h))

    return pl.pallas_call(
        _nsa_sample_win_kernel,
        grid=(nb, NSA_KV_HEADS),
        in_specs=[pl.BlockSpec((1, 1, gw), lambda b, h: (b, 0, COL_NQ // gw + h)),
                  bufspec(0), bufspec(1), newspec(0), newspec(1),
                  pl.BlockSpec((1, SROWS, nbuf), lambda b, h: (h, 0, 0)),
                  pl.BlockSpec((1, SROWS, 1), lambda b, h: (h, 0, 0))],
        out_specs=pl.BlockSpec((1, 1, SROWS, NSA_DK), lambda b, h: (b, h, 0, 0)),
        out_shape=jax.ShapeDtypeStruct((nb, NSA_KV_HEADS, SROWS, NSA_DK), F32),
        compiler_params=_cparams(2),
        name="nsa_sample_win",
    )(z3, win_buf, win_buf, z3, z3, bias_win, bias_new)


def _nsa_sample_gate_kernel(oc_ref, os_ref, ow_ref, ng_ref, nsl_ref, o_ref):
    gates = _sigmoid(ng_ref[0])
    nsl = nsl_ref[0]
    outs = []
    for g in range(NSA_GROUP):
        o = (gates[:, 3 * g:3 * g + 1] * oc_ref[0, 0, g:g + 1] + gates[:, 3 * g + 1:3 * g + 2] * os_ref[0, 0, g:g + 1]
             + gates[:, 3 * g + 2:3 * g + 3] * ow_ref[0, 0, g:g + 1])
        x = nsl[:, g * NSA_DK:(g + 1) * NSA_DK]
        outs.append(o * (x * _sigmoid(x)))
    o_ref[0] = jnp.concatenate(outs, axis=1).astype(BF16)


def _nsa_sample_gate(o_cmp, o_sel, o_win, z3b):
    nb = z3b.shape[0]
    gw = NSA_GROUP * NSA_DK
    ospec = pl.BlockSpec((1, 1, SROWS, NSA_DK), lambda b, h: (b, h, 0, 0))
    return pl.pallas_call(
        _nsa_sample_gate_kernel,
        grid=(nb, NSA_KV_HEADS),
        in_specs=[ospec, ospec, ospec,
                  pl.BlockSpec((1, 1, LANE), lambda b, h: (b, 0, COL_NG // LANE + h)),
                  pl.BlockSpec((1, 1, gw), lambda b, h: (b, 0, COL_NSL // gw + h))],
        out_specs=pl.BlockSpec((1, 1, gw), lambda b, h: (b, 0, h)),
        out_shape=jax.ShapeDtypeStruct((nb, 1, NSA_W), BF16),
        compiler_params=_cparams(2),
        name="nsa_sample_gate",
    )(o_cmp, o_sel, o_win, z3b, z3b)


def _mem_heads(q, kv):
    outs = []
    for h in range(MEM_HEADS):
        k = kv[:, h * MEM_DH:(h + 1) * MEM_DH].astype(BF16)
        v = kv[:, MEM_W + h * MEM_DH:MEM_W + (h + 1) * MEM_DH].astype(BF16)
        s = _nt(q[:, h * MEM_DH:(h + 1) * MEM_DH].astype(BF16), k) * (MEM_DH ** -0.5)
        e = jnp.exp(s - jnp.max(s, axis=1, keepdims=True))
        p = e / jnp.sum(e, axis=1, keepdims=True)
        outs.append(_dot(p.astype(BF16), v))
    return jnp.concatenate(outs, axis=1)


def _mem_prompt_kernel(q_ref, kv_ref, o_ref):
    o_ref[...] = _mem_heads(q_ref[...], kv_ref[...]).astype(BF16)


def _mem_prompt(z, mem_kv, batch, seq, tq):
    nq = seq // tq
    n_mem = mem_kv.shape[0] // batch
    return pl.pallas_call(
        _mem_prompt_kernel,
        grid=(batch, nq),
        in_specs=[pl.BlockSpec((tq, MEM_W), lambda b, i: (b * nq + i, COL_MQ // MEM_W)),
                  pl.BlockSpec((n_mem, 2 * MEM_W), lambda b, i: (b, 0))],
        out_specs=pl.BlockSpec((tq, MEM_W), lambda b, i: (b * nq + i, 0)),
        out_shape=jax.ShapeDtypeStruct((batch * seq, MEM_W), BF16),
        compiler_params=_cparams(2),
        name="mem_prompt",
    )(z, mem_kv)


def _mem_sample_kernel(q_ref, kv_ref, o_ref):
    q = jnp.broadcast_to(q_ref[0], (SROWS, MEM_W))
    o_ref[0] = _mem_heads(q, kv_ref[0])[0:1].astype(BF16)


def _mem_sample(z3, mem_kv):
    nb, n_mem, _ = mem_kv.shape
    return pl.pallas_call(
        _mem_sample_kernel,
        grid=(nb,),
        in_specs=[pl.BlockSpec((1, 1, MEM_W), lambda b: (b, 0, COL_MQ // MEM_W)),
                  pl.BlockSpec((1, n_mem, 2 * MEM_W), lambda b: (b, 0, 0))],
        out_specs=pl.BlockSpec((1, 1, MEM_W), lambda b: (b, 0, 0)),
        out_shape=jax.ShapeDtypeStruct((nb, 1, MEM_W), BF16),
        compiler_params=_cparams(1),
        name="mem_sample",
    )(z3, mem_kv)


def _merge_kernel(ar_ref, an_ref, am_ref, wr_ref, wn_ref, wm_ref, g0_ref, g1_ref, g2_ref, o_ref):
    merged = (_sigmoid(g0_ref[...]) * _dot(ar_ref[...], wr_ref[...])
              + _sigmoid(g1_ref[...]) * _dot(an_ref[...], wn_ref[...])
              + _sigmoid(g2_ref[...]) * _dot(am_ref[...], wm_ref[...]))
    o_ref[...] = merged.astype(BF16)


def _merge(a_ret, a_nsa, a_mem, w_ret, w_nsa, w_mem, z, tm, tn):
    m = a_ret.shape[0]
    nt = D_MODEL // tn

    def aspec(width):
        return pl.BlockSpec((tm, width), lambda i, j: (i, 0))

    def wspec(width):
        return pl.BlockSpec((width, tn), lambda i, j: (0, j))

    def gspec(branch):
        return pl.BlockSpec((tm, tn), lambda i, j: (i, COL_MG // tn + branch * nt + j))

    return pl.pallas_call(
        _merge_kernel,
        grid=(m // tm, nt),
        in_specs=[aspec(RET_W), aspec(NSA_W), aspec(MEM_W), wspec(RET_W), wspec(NSA_W), wspec(MEM_W),
                  gspec(0), gspec(1), gspec(2)],
        out_specs=pl.BlockSpec((tm, tn), lambda i, j: (i, j)),
        out_shape=jax.ShapeDtypeStruct((m, D_MODEL), BF16),
        compiler_params=_cparams(2),
        name="merge",
    )(a_ret, a_nsa, a_mem, w_ret, w_nsa, w_mem, z, z, z)


def _out_kernel(a_ref, w_ref, x_ref, g_ref, o_ref):
    out = _dot(a_ref[...], w_ref[...])
    y = out * lax.rsqrt(jnp.mean(out * out, axis=-1, keepdims=True) + EPS)
    o_ref[...] = x_ref[...] + y * g_ref[...]


def _out_proj(merged, w_out, x, norm_post, tm):
    m = merged.shape[0]
    return pl.pallas_call(
        _out_kernel,
        grid=(m // tm,),
        in_specs=[pl.BlockSpec((tm, D_MODEL), lambda i: (i, 0)),
                  pl.BlockSpec((D_MODEL, D_MODEL), lambda i: (0, 0)),
                  pl.BlockSpec((tm, D_MODEL), lambda i: (i, 0)),
                  pl.BlockSpec((1, D_MODEL), lambda i: (0, 0))],
        out_specs=pl.BlockSpec((tm, D_MODEL), lambda i: (i, 0)),
        out_shape=jax.ShapeDtypeStruct((m, D_MODEL), F32),
        compiler_params=_cparams(1),
        name="out_proj",
    )(merged, w_out, x, norm_post.reshape(1, D_MODEL))


def _layout_w_tail(w_t):
    splits = (N_BRANCHES * NSA_HEADS, NSA_W, MEM_W, N_BRANCHES * D_MODEL)
    pieces, start = [], PROJ_A
    for w in splits:
        pieces.append(w_t[start:start + w])
        start += w
    assert start == w_t.shape[0]
    ng, nsl, mq, mg = pieces
    per_group = N_BRANCHES * NSA_GROUP
    ng = ng.reshape(NSA_KV_HEADS, per_group, D_MODEL)
    ng = jnp.pad(ng, ((0, 0), (0, LANE - per_group), (0, 0))).reshape(NG_SLOT, D_MODEL)
    return jnp.concatenate([mq, nsl, mg, ng], axis=0)


def _pick_tile(m, cap):
    t = min(m, cap)
    while m % t:
        t //= 2
    return t


def kernel(x_prompt, x_sample, cache_cmp_kv, cache_sel_kv, cache_win_kv, state_ret, cache_mem_kv, page_table,
           mem_prompt, rel_table, norm_pre, norm_post, norm_mem, w_in, ret_norm, w_ret_up, cmp_pos, w_cmp1,
           w_cmp2, w_nsa_up, w_mem_kv, w_mem_up, w_out):
    batch, seq, _ = x_prompt.shape
    nb = x_sample.shape[0]
    assert x_sample.shape[1] == 1 and norm_pre.shape[0] == 1
    assert seq % TQ == 0 and seq >= WINDOW
    n_pool = cache_cmp_kv.shape[1]
    n_pages = page_table.shape[1]
    past = n_pages * PAGE_SIZE
    n_mem = mem_prompt.shape[1]
    assert n_pages % PAGES_PER_STEP == 0 and cache_win_kv.shape[2] == WINDOW

    w_a = w_in[0].T
    w_b = _layout_w_tail(w_a)
    kw = CMP_STRIDE * NSA_DK
    w1 = w_cmp1[0].reshape(2, CMP_BLOCK * NSA_DK, NSA_DK).astype(BF16)
    w1a, w1b = w1[:, :kw], w1[:, kw:]
    w1ab = jnp.concatenate([w1a, w1b], axis=2)
    w2 = w_cmp2[0].astype(BF16)
    pos8 = jnp.pad(cmp_pos[0].reshape(2, 1, CMP_BLOCK * NSA_DK), ((0, 0), (0, 7), (0, 0)))
    w_ret = w_ret_up[0].astype(BF16)
    w_nsa = w_nsa_up[0].astype(BF16)
    w_mem = w_mem_up[0].astype(BF16)
    w_o = w_out[0].astype(BF16)

    m_p = batch * seq
    xp = x_prompt.reshape(m_p, D_MODEL)
    hp = _rmsnorm(xp, norm_pre[0], _pick_tile(m_p, 512))
    z = _proj(hp, w_a, _pick_tile(m_p, 1024), PROJ_TN, PROJ_A, transposed=True)
    zb = _proj(hp, w_b, _pick_tile(m_p, 1024), PROJ_TN, transposed=True)

    a_ret, ret_state_p = _retention_prompt(z, ret_norm[0], batch, seq)

    ncp = max(LANE, -(-(seq // CMP_STRIDE) // LANE) * LANE)
    ac = _cmp_stage1_dense(z, w1ab, batch, seq)
    kcvc = _cmp_stage2(ac, pos8, w1a, w1b, w2, 1)
    if ncp > kcvc.shape[3]:
        kcvc = jnp.pad(kcvc, ((0, 0), (0, 0), (0, 0), (0, ncp - kcvc.shape[3]), (0, 0)))
    bias_d = _bias_by_dist(rel_table, BIAS_DISTS)
    a_nsa = _nsa_prompt(z, zb, kcvc, bias_d, batch, seq)

    hm = _rmsnorm(mem_prompt.reshape(batch * n_mem, D_MODEL), norm_mem[0], _pick_tile(batch * n_mem, 512))
    mem_kv_p = _proj(hm, w_mem_kv[0], _pick_tile(batch * n_mem, 512), PROJ_TN)
    a_mem = _mem_prompt(zb, mem_kv_p, batch, seq, _pick_tile(seq, 512))

    merged = _merge(a_ret, a_nsa, a_mem, w_ret, w_nsa, w_mem, zb, _pick_tile(m_p, 512), 512)
    y_p = _out_proj(merged, w_o, xp, norm_post[0], _pick_tile(m_p, 256)).reshape(batch, seq, D_MODEL)

    kv_shape = (1, batch, seq, 2, NSA_KV_HEADS, NSA_DK)
    new_cmp_p = z[:, COL_KVC:COL_KVC + 2 * KV_W].reshape(kv_shape)
    new_sel_p = z[:, COL_KVS:COL_KVS + 2 * KV_W].reshape(kv_shape)
    new_win_p = z[:, COL_KVW:COL_KVW + 2 * KV_W].reshape(kv_shape)[:, :, seq - WINDOW:]
    new_ret_p = ret_state_p[None]
    new_mem_p = mem_kv_p.reshape(1, batch, n_mem, 2, MEM_HEADS, MEM_DH)

    xs = x_sample.reshape(nb, D_MODEL)
    hs = _rmsnorm(xs, norm_pre[0], nb)
    zs = _proj(hs, w_a, nb, PROJ_TN, PROJ_A, transposed=True)
    zsb = _proj(hs, w_b, nb, PROJ_TN, transposed=True)
    z3 = zs.reshape(nb, 1, PROJ_A)
    z3b = zsb.reshape(nb, 1, PROJ_B)

    a_ret_s, ret_state_s = _retention_sample(z3, state_ret[0], ret_norm[0], past)

    cache_c = cache_cmp_kv[0].reshape(n_pool, PAGE_SIZE // CMP_STRIDE, CMP_STRIDE, 2 * NSA_KV_HEADS, NSA_DK)
    cache_s = cache_sel_kv[0].reshape(n_pool, PAGE_SIZE, 2 * NSA_KV_HEADS, NSA_DK)
    ac_s = _cmp_stage1_paged(cache_c, page_table, w1ab)
    kcvc_s = _cmp_stage2(ac_s, pos8, w1a, w1b, w2, NSA_KV_HEADS)
    ncs = past // CMP_STRIDE
    kcvc_s = kcvc_s.reshape(nb, 2, ncs, KV_W)
    ns_s = past // SEL_BLOCK + 1
    nsp_s = -(-ns_s // LANE) * LANE
    assert past >= WINDOW and past >= REL_MAX_DIST and BIAS_DISTS > WINDOW
    far_s = bias_d[..., REL_MAX_DIST:REL_MAX_DIST + 1]
    hg = (NSA_KV_HEADS, NSA_GROUP)
    n_valid = (past - (CMP_BLOCK - 1)) // CMP_STRIDE + 1
    strided = bias_d[..., (past - (CMP_BLOCK - 1)) % CMP_STRIDE::CMP_STRIDE]
    n_tab = strided.shape[-1]
    assert n_valid >= n_tab and n_tab * CMP_STRIDE > REL_MAX_DIST + CMP_STRIDE and ncs >= n_valid
    bias_cs = jnp.concatenate([jnp.broadcast_to(far_s, hg + (n_valid - n_tab,)), strided[..., ::-1],
                               jnp.full(hg + (ncs - n_valid,), NEG_INF, F32)], axis=-1)
    bias_cs = _pad_group_rows(bias_cs, 1)
    nblk = jnp.arange(ncs)[:, None]
    sblk = jnp.arange(nsp_s)[None, :]
    ov_s = ((nblk >= 4 * sblk - 1) & (nblk <= 4 * sblk + 3)).astype(BF16)
    o_cmp_s, idx_s = _nsa_sample_cmp(z3, kcvc_s, bias_cs, ov_s, ns_s)
    n_sel = min(SEL_TOPK, ns_s)
    idx = idx_s[:, :, 0, :n_sel]

    hg = (NSA_KV_HEADS, NSA_GROUP)
    n_key = ns_s * SEL_BLOCK
    bias_sel = jnp.concatenate([jnp.broadcast_to(far_s, hg + (past + 1 - REL_MAX_DIST,)),
                                bias_d[..., :REL_MAX_DIST][..., ::-1],
                                jnp.full(hg + (n_key - past - 1,), NEG_INF, F32)], axis=-1)
    bias_sel = _pad_group_rows(bias_sel.reshape(hg + (ns_s, SEL_BLOCK)).transpose(0, 2, 1, 3), 2)
    own_k = jnp.arange(2 * NSA_KV_HEADS)[None, :] == jnp.arange(NSA_KV_HEADS)[:, None]
    bias_sel = jnp.where(own_k[:, None, None, None, :], bias_sel[..., None], NEG_INF)
    bias_sel = bias_sel.reshape(NSA_KV_HEADS, ns_s, SROWS, SEL_BLOCK * 2 * NSA_KV_HEADS)
    kvs_new = zs[:, COL_KVS:COL_KVS + 2 * KV_W].reshape(nb, 2 * NSA_KV_HEADS, NSA_DK)
    o_sel_s = _nsa_sample_sel(z3, cache_s, kvs_new, page_table, idx, bias_sel, ns_s)

    win_buf = cache_win_kv[0].reshape(nb, WINDOW, 2 * KV_W)
    bias_w = jnp.concatenate([jnp.full(hg + (1,), NEG_INF, F32), bias_d[..., 1:WINDOW][..., ::-1]], axis=-1)
    o_win_s = _nsa_sample_win(z3, win_buf, _pad_group_rows(bias_w, 1), _pad_group_rows(bias_d[..., 0:1], 1))
    a_nsa_s = _nsa_sample_gate(o_cmp_s, o_sel_s, o_win_s, z3b)

    mem_kv_s = cache_mem_kv[0].reshape(nb, n_mem, 2 * MEM_W)
    a_mem_s = _mem_sample(z3b, mem_kv_s)

    merged_s = _merge(a_ret_s.reshape(nb, RET_W), a_nsa_s.reshape(nb, NSA_W), a_mem_s.reshape(nb, MEM_W),
                      w_ret, w_nsa, w_mem, zsb, nb, 512)
    y_s = _out_proj(merged_s, w_o, xs, norm_post[0], nb).reshape(nb, 1, D_MODEL)

    kvs_shape = (1, nb, 1, 2, NSA_KV_HEADS, NSA_DK)
    new_cmp_s = zs[:, COL_KVC:COL_KVC + 2 * KV_W].reshape(kvs_shape)
    new_sel_s = zs[:, COL_KVS:COL_KVS + 2 * KV_W].reshape(kvs_shape)
    kvw_s = zs[:, COL_KVW:COL_KVW + 2 * KV_W].reshape(nb, 1, 2, NSA_KV_HEADS, NSA_DK)
    new_win_s = jnp.concatenate([cache_win_kv[0][:, 1:], kvw_s], axis=1)[None]
    new_ret_s = ret_state_s[None]

    return (y_p, y_s, new_cmp_p, new_sel_p, new_win_p, new_ret_p, new_mem_p,
            new_cmp_s, new_sel_s, new_win_s, new_ret_s)
```

```python
import functools
import math

import jax
import jax.numpy as jnp
from jax import lax
from jax.experimental import pallas as pl
from jax.experimental.pallas import tpu as pltpu

F32 = jnp.float32
BF16 = jnp.bfloat16

D_MODEL = 2048
PAGE_SIZE = 128
RET_HEADS = 8
RET_DK = 256
RET_DV = 256
RET_CHUNK = 128
ROPE_BASE = 10000.0
NSA_HEADS = 16
NSA_KV_HEADS = 4
NSA_GROUP = NSA_HEADS // NSA_KV_HEADS
NSA_DK = 128
CMP_BLOCK = 32
CMP_STRIDE = 16
SEL_BLOCK = 64
SEL_TOPK = 16
WINDOW = 512
MEM_HEADS = 4
MEM_DH = 384
REL_BUCKETS = 32
REL_MAX_EXACT = 16
REL_MAX_DIST = 128
N_BRANCHES = 3
EPS = 1e-6
NEG_INF = -1e30
FORCE_SCORE = 1e4

RET_W = RET_HEADS * RET_DV
NSA_W = NSA_HEADS * NSA_DK
KV_W = NSA_KV_HEADS * NSA_DK
MEM_W = MEM_HEADS * MEM_DH

COL_RQ = 0
COL_RK = COL_RQ + RET_HEADS * RET_DK
COL_RV = COL_RK + RET_HEADS * RET_DK
COL_RG = COL_RV + RET_W
COL_NQ = COL_RG + RET_W
COL_KVC = COL_NQ + NSA_W
COL_KVS = COL_KVC + 2 * KV_W
COL_KVW = COL_KVS + 2 * KV_W
PROJ_A = COL_KVW + 2 * KV_W
COL_MQ = 0
COL_NSL = COL_MQ + MEM_W
COL_MG = COL_NSL + NSA_W
COL_NG = COL_MG + N_BRANCHES * D_MODEL
NG_SLOT = NSA_KV_HEADS * 128
PROJ_B = COL_NG + NG_SLOT

LOG2E = math.log2(math.e)
LANE = 128
TQ = 256
TK = 256
NSA_ROWS = NSA_GROUP * TQ
BIAS_DISTS = 1024
FAR_GROUP = 4
RET_STEP_CHUNKS = 8
PROJ_TN = 1024
VMEM_LIMIT = 56 * 1024 * 1024


def _cparams(n_axes):
    return pltpu.CompilerParams(dimension_semantics=("arbitrary",) * n_axes, vmem_limit_bytes=VMEM_LIMIT)


def _nt(a, b):
    return lax.dot_general(a, b, (((1,), (1,)), ((), ())), preferred_element_type=F32)


def _dot(a, b):
    return jnp.dot(a, b, preferred_element_type=F32)


def _sigmoid(x):
    return 1.0 / (1.0 + jnp.exp(-x))


def _iota(shape, dim):
    return lax.broadcasted_iota(jnp.int32, shape, dim)


def _rmsnorm_kernel(x_ref, g_ref, h_ref):
    x = x_ref[...]
    ms = jnp.mean(x * x, axis=-1, keepdims=True)
    h_ref[...] = ((x * lax.rsqrt(ms + EPS)) * g_ref[...]).astype(BF16)


def _rmsnorm(x, g, tm):
    m, k = x.shape
    return pl.pallas_call(
        _rmsnorm_kernel,
        grid=(m // tm,),
        in_specs=[pl.BlockSpec((tm, k), lambda i: (i, 0)), pl.BlockSpec((1, k), lambda i: (0, 0))],
        out_specs=pl.BlockSpec((tm, k), lambda i: (i, 0)),
        out_shape=jax.ShapeDtypeStruct((m, k), BF16),
        compiler_params=_cparams(1),
        name="rmsnorm",
    )(x, g.reshape(1, k))


def _proj_kernel(h_ref, w_ref, o_ref, wb_ref, *, transposed):
    @pl.when(pl.program_id(1) == 0)
    def _():
        w = w_ref[...]
        wb_ref[...] = (w.T if transposed else w).astype(BF16)

    o_ref[...] = _dot(h_ref[...], wb_ref[...])


def _proj(h, w, tm, tn, n=None, transposed=False):
    m, k = h.shape
    n = w.shape[0 if transposed else 1] if n is None else n
    wspec = pl.BlockSpec((tn, k), lambda j, i: (j, 0)) if transposed else pl.BlockSpec((k, tn), lambda j, i: (0, j))
    return pl.pallas_call(
        functools.partial(_proj_kernel, transposed=transposed),
        grid=(n // tn, m // tm),
        in_specs=[pl.BlockSpec((tm, k), lambda j, i: (i, 0)), wspec],
        out_specs=pl.BlockSpec((tm, tn), lambda j, i: (i, j)),
        out_shape=jax.ShapeDtypeStruct((m, n), F32),
        scratch_shapes=[pltpu.VMEM((k, tn), BF16)],
        compiler_params=_cparams(2),
        name="proj",
    )(h, w)


TAIL_TN = 512


def _proj_tail_kernel(h_ref, w_ref, wng_ref, o_ref, wb_ref, *, ng_tile):
    j = pl.program_id(0)

    @pl.when((pl.program_id(1) == 0) & (j != ng_tile))
    def _():
        wb_ref[...] = w_ref[...].T.astype(BF16)

    @pl.when((pl.program_id(1) == 0) & (j == ng_tile))
    def _():
        wb_ref[...] = wng_ref[...].T.astype(BF16)

    o_ref[...] = _dot(h_ref[...], wb_ref[...])


def _proj_tail(h, w_t, w_ng, tm):
    m, k = h.shape
    tn = TAIL_TN
    row_ng = PROJ_A
    row_nsl = row_ng + N_BRANCHES * NSA_HEADS
    row_mq = row_nsl + NSA_W
    row_mg = row_mq + MEM_W
    assert row_mg + N_BRANCHES * D_MODEL == w_t.shape[0] and w_ng.shape == (NG_SLOT, k) and NG_SLOT == tn
    t_nsl, t_mg, t_ng = COL_NSL // tn, COL_MG // tn, COL_NG // tn

    def w_row(j, i):
        row = jnp.where(j < t_nsl, row_mq + j * tn,
                        jnp.where(j < t_mg, row_nsl + (j - t_nsl) * tn, row_mg + (jnp.minimum(j, t_ng - 1) - t_mg) * tn))
        return (pl.multiple_of(row, 16), 0)

    assert row_mq % 16 == 0 and row_nsl % 16 == 0 and row_mg % 16 == 0
    return pl.pallas_call(
        functools.partial(_proj_tail_kernel, ng_tile=t_ng),
        grid=(PROJ_B // tn, m // tm),
        in_specs=[pl.BlockSpec((tm, k), lambda j, i: (i, 0)),
                  pl.BlockSpec((pl.Element(tn), pl.Element(k)), w_row),
                  pl.BlockSpec((tn, k), lambda j, i: (0, 0))],
        out_specs=pl.BlockSpec((tm, tn), lambda j, i: (i, j)),
        out_shape=jax.ShapeDtypeStruct((m, PROJ_B), F32),
        scratch_shapes=[pltpu.VMEM((k, tn), BF16)],
        compiler_params=_cparams(2),
        name="proj_tail",
    )(h, w_t, w_ng)


def _rope_rows(x, cos, sin):
    half = x.shape[-1] // 2
    x1, x2 = x[:, :half], x[:, half:]
    return jnp.concatenate([x1 * cos - x2 * sin, x1 * sin + x2 * cos], axis=-1)


def _head_norm_gate(o, gnorm, rg):
    oc = o - jnp.mean(o, axis=-1, keepdims=True)
    y = oc * lax.rsqrt(jnp.mean(oc * oc, axis=-1, keepdims=True) + EPS) * gnorm
    return y * (rg * _sigmoid(rg))


def _ret_prompt_kernel(q_ref, k_ref, v_ref, rg_ref, cos_ref, sin_ref, dmat_ref, xi_ref, zeta_ref, gc_ref,
                       gn_ref, a_ref, s_ref):
    @pl.when(pl.program_id(2) == 0)
    def _():
        s_ref[...] = jnp.zeros_like(s_ref)

    c = RET_CHUNK
    for t in range(q_ref.shape[0] // c):
        rows = slice(t * c, (t + 1) * c)
        cos, sin = cos_ref[rows, :], sin_ref[rows, :]
        q = _rope_rows(q_ref[rows, :], cos, sin)
        k = _rope_rows(k_ref[rows, :], cos, sin) * (RET_DK ** -0.5)
        qb, vb = q.astype(BF16), v_ref[rows, :].astype(BF16)
        state = s_ref[0, 0]
        inner = _nt(qb, k.astype(BF16)) * dmat_ref[0]
        o = _dot(inner.astype(BF16), vb) + _dot(qb, state.astype(BF16)) * xi_ref[0]
        kz_t = (k * zeta_ref[0]).T.astype(BF16)
        s_ref[0, 0] = state * gc_ref[0] + _dot(kz_t, vb)
        a_ref[rows, :] = _head_norm_gate(o, gn_ref[...], rg_ref[rows, :]).astype(BF16)


def _decay_tables(chunk):
    log_g = jnp.log1p(-jnp.exp2(-5.0 - jnp.arange(RET_HEADS, dtype=F32)))
    i = jnp.arange(chunk, dtype=F32)
    diff = i[:, None] - i[None, :]
    dmat = jnp.where(diff >= 0, jnp.exp(log_g[:, None, None] * jnp.maximum(diff, 0.0)), 0.0)
    xi = jnp.exp(log_g[:, None] * (i[None, :] + 1.0))[:, :, None]
    zeta = jnp.exp(log_g[:, None] * (chunk - 1.0 - i[None, :]))[:, :, None]
    g_chunk = jnp.exp(log_g * chunk)[:, None, None]
    return dmat, xi, zeta, g_chunk


def _rope_tables(pos):
    half = RET_DK // 2
    freq = jnp.power(ROPE_BASE, -jnp.arange(half, dtype=F32) / half)
    ang = pos.astype(F32)[:, None] * freq[None, :]
    return jnp.cos(ang), jnp.sin(ang)


def _retention_prompt(z, ret_norm, batch, seq):
    c = RET_CHUNK
    rows = _pick_tile(seq, RET_STEP_CHUNKS * c)
    nc = seq // rows
    dmat, xi, zeta, g_chunk = _decay_tables(c)
    cos, sin = _rope_tables(jnp.arange(seq))
    hb = RET_DK

    def zspec(col0):
        return pl.BlockSpec((rows, hb), lambda b, h, t, col0=col0: (b * nc + t, col0 // hb + h))

    per_head = lambda shape: pl.BlockSpec((1,) + shape, lambda b, h, t: (h, 0, 0))
    return pl.pallas_call(
        _ret_prompt_kernel,
        grid=(batch, RET_HEADS, nc),
        in_specs=[zspec(COL_RQ), zspec(COL_RK), zspec(COL_RV), zspec(COL_RG),
                  pl.BlockSpec((rows, hb // 2), lambda b, h, t: (t, 0)),
                  pl.BlockSpec((rows, hb // 2), lambda b, h, t: (t, 0)),
                  per_head((c, c)), per_head((c, 1)), per_head((c, 1)), per_head((1, 1)),
                  pl.BlockSpec((1, hb), lambda b, h, t: (0, h))],
        out_specs=[pl.BlockSpec((rows, hb), lambda b, h, t: (b * nc + t, h)),
                   pl.BlockSpec((1, 1, RET_DK, RET_DV), lambda b, h, t: (b, h, 0, 0))],
        out_shape=[jax.ShapeDtypeStruct((batch * seq, RET_W), BF16),
                   jax.ShapeDtypeStruct((batch, RET_HEADS, RET_DK, RET_DV), F32)],
        compiler_params=_cparams(3),
        name="retention_prompt",
    )(z, z, z, z, cos, sin, dmat, xi, zeta, g_chunk, ret_norm.reshape(1, RET_W))


def _column_of(row):
    n = row.shape[1]
    eye = _iota((n, n), 0) == _iota((n, n), 1)
    return jnp.sum(jnp.where(eye, jnp.broadcast_to(row, (n, n)), 0.0), axis=1, keepdims=True)


def _ret_sample_kernel(q_ref, k_ref, v_ref, rg_ref, cos_ref, sin_ref, gam_ref, gn_ref, s_ref, a_ref, so_ref):
    cos, sin = cos_ref[...], sin_ref[...]
    outs = []
    for h in range(RET_HEADS):
        cols = slice(h * RET_DK, (h + 1) * RET_DK)
        q = _rope_rows(q_ref[0][:, cols], cos, sin)
        k = _rope_rows(k_ref[0][:, cols], cos, sin) * (RET_DK ** -0.5)
        v = v_ref[0][:, cols]
        state = s_ref[0, h]
        gamma = gam_ref[h]
        qk = jnp.sum(q * k, axis=-1, keepdims=True)
        o = qk * v + jnp.sum(_column_of(q) * state, axis=0, keepdims=True) * gamma
        so_ref[0, h] = state * gamma + _column_of(k) * v
        outs.append(_head_norm_gate(o, gn_ref[:, cols], rg_ref[0][:, cols]))
    a_ref[0] = jnp.concatenate(outs, axis=1).astype(BF16)


def _retention_sample(z3, state, ret_norm, pos):
    nb = z3.shape[0]
    cos, sin = _rope_tables(jnp.full((1,), pos))
    gamma = jnp.exp(jnp.log1p(-jnp.exp2(-5.0 - jnp.arange(RET_HEADS, dtype=F32))))[:, None, None]
    hb = RET_DK
    assert RET_HEADS * RET_DK == RET_W

    def zspec(col0):
        return pl.BlockSpec((1, 1, RET_W), lambda b, col0=col0: (b, 0, col0 // RET_W))

    st_spec = pl.BlockSpec((1, RET_HEADS, RET_DK, RET_DV), lambda b: (b, 0, 0, 0))
    return pl.pallas_call(
        _ret_sample_kernel,
        grid=(nb,),
        in_specs=[zspec(COL_RQ), zspec(COL_RK), zspec(COL_RV), zspec(COL_RG),
                  pl.BlockSpec((1, hb // 2), lambda b: (0, 0)),
                  pl.BlockSpec((1, hb // 2), lambda b: (0, 0)),
                  pl.BlockSpec((RET_HEADS, 1, 1), lambda b: (0, 0, 0)),
                  pl.BlockSpec((1, RET_W), lambda b: (0, 0)),
                  st_spec],
        out_specs=[pl.BlockSpec((1, 1, RET_W), lambda b: (b, 0, 0)), st_spec],
        out_shape=[jax.ShapeDtypeStruct((nb, 1, RET_W), BF16),
                   jax.ShapeDtypeStruct(state.shape, F32)],
        compiler_params=_cparams(1),
        name="retention_sample",
    )(z3, z3, z3, z3, cos, sin, gamma, ret_norm.reshape(1, RET_W), state)


def _half_rows(ref_slice_fn, n_half):
    return jnp.concatenate([ref_slice_fn(p) for p in range(CMP_STRIDE)], axis=1)


def _cmp_stage1_dense_kernel(x_ref, w_ref, o_ref):
    nh = o_ref.shape[3]
    x = _half_rows(lambda p: x_ref[pl.ds(p, nh, stride=CMP_STRIDE), :], nh).astype(BF16)
    o_ref[0, 0, 0] = _dot(x, w_ref[0])


def _cmp_stage1_dense(z, w1ab, batch, seq):
    nh = seq // CMP_STRIDE
    return pl.pallas_call(
        _cmp_stage1_dense_kernel,
        grid=(batch, 2, NSA_KV_HEADS),
        in_specs=[pl.BlockSpec((seq, NSA_DK), lambda b, kv, h: (b, COL_KVC // NSA_DK + kv * NSA_KV_HEADS + h)),
                  pl.BlockSpec((1, CMP_STRIDE * NSA_DK, 2 * NSA_DK), lambda b, kv, h: (kv, 0, 0))],
        out_specs=pl.BlockSpec((1, 1, 1, nh, 2 * NSA_DK), lambda b, kv, h: (b, kv, h, 0, 0)),
        out_shape=jax.ShapeDtypeStruct((batch, 2, NSA_KV_HEADS, nh, 2 * NSA_DK), F32),
        compiler_params=_cparams(3),
        name="cmp_stage1_dense",
    )(z, w1ab)


PAGES_PER_STEP = 16


def _cmp_stage1_paged_kernel(pt_ref, *refs):
    pages, (w_ref, o_ref) = refs[:PAGES_PER_STEP], refs[PAGES_PER_STEP:]
    hp = PAGE_SIZE // CMP_STRIDE
    top = _iota((2 * NSA_KV_HEADS, NSA_DK), 0) < NSA_KV_HEADS
    cols = [[], []]
    for p in range(CMP_STRIDE):
        tiles = [[], []]
        for pg in pages:
            xp = pg[0, :, p]
            for n in range(0, hp, 2):
                a, b = xp[n], xp[n + 1]
                tiles[0].append(jnp.where(top, a, pltpu.roll(b, NSA_KV_HEADS, 0)))
                tiles[1].append(jnp.where(top, pltpu.roll(a, NSA_KV_HEADS, 0), b))
        for kv in range(2):
            cols[kv].append(jnp.concatenate(tiles[kv], axis=0))
    for kv in range(2):
        x = jnp.concatenate(cols[kv], axis=1).astype(BF16)
        o_ref[0, kv, 0] = _dot(x, w_ref[kv])


def _cmp_stage1_paged(cache, page_table, w1ab):
    nb, n_pages = page_table.shape
    hp = PAGE_SIZE // CMP_STRIDE
    steps = n_pages // PAGES_PER_STEP
    rows = PAGES_PER_STEP * hp * NSA_KV_HEADS

    def page_spec(j):
        return pl.BlockSpec((1, hp, CMP_STRIDE, 2 * NSA_KV_HEADS, NSA_DK),
                            lambda b, s, pt, j=j: (pt[b, s * PAGES_PER_STEP + j], 0, 0, 0, 0))

    wspec = pl.BlockSpec((2, CMP_STRIDE * NSA_DK, 2 * NSA_DK), lambda b, s, pt: (0, 0, 0))
    return pl.pallas_call(
        _cmp_stage1_paged_kernel,
        grid_spec=pltpu.PrefetchScalarGridSpec(
            num_scalar_prefetch=1,
            grid=(nb, steps),
            in_specs=[page_spec(j) for j in range(PAGES_PER_STEP)] + [wspec],
            out_specs=pl.BlockSpec((1, 2, 1, rows, 2 * NSA_DK), lambda b, s, pt: (b, 0, 0, s, 0))),
        out_shape=jax.ShapeDtypeStruct((nb, 2, 1, steps * rows, 2 * NSA_DK), F32),
        compiler_params=_cparams(2),
        name="cmp_stage1_paged",
    )(page_table, *([cache] * PAGES_PER_STEP), w1ab)


def _cmp_stage2_kernel(ac_ref, pos_ref, w1a_ref, w1b_ref, w2_ref, o_ref, *, shift):
    ac = ac_ref[0, 0, 0]
    nh = ac.shape[0]
    pos = pos_ref[0].astype(BF16)
    kw = CMP_STRIDE * NSA_DK
    pe = _dot(pos[:, :kw], w1a_ref[0]) + _dot(pos[:, kw:], w1b_ref[0])
    pre = ac[:, :NSA_DK] + pltpu.roll(ac[:, NSA_DK:], nh - shift, 0) + pe[0:1]
    gelu = 0.5 * pre * (1.0 + jnp.tanh(math.sqrt(2.0 / math.pi) * (pre + 0.044715 * (pre * pre * pre))))
    o_ref[0, 0, 0] = _dot(gelu.astype(BF16), w2_ref[0]).astype(BF16)


def _cmp_stage2(ac, pos8, w1a, w1b, w2, shift):
    nb, _, groups, nh, _ = ac.shape
    kw = CMP_STRIDE * NSA_DK
    return pl.pallas_call(
        functools.partial(_cmp_stage2_kernel, shift=shift),
        grid=(nb, 2, groups),
        in_specs=[pl.BlockSpec((1, 1, 1, nh, 2 * NSA_DK), lambda b, kv, h: (b, kv, h, 0, 0)),
                  pl.BlockSpec((1, 8, 2 * kw), lambda b, kv, h: (kv, 0, 0)),
                  pl.BlockSpec((1, kw, NSA_DK), lambda b, kv, h: (kv, 0, 0)),
                  pl.BlockSpec((1, kw, NSA_DK), lambda b, kv, h: (kv, 0, 0)),
                  pl.BlockSpec((1, NSA_DK, NSA_DK), lambda b, kv, h: (kv, 0, 0))],
        out_specs=pl.BlockSpec((1, 1, 1, nh, NSA_DK), lambda b, kv, h: (b, kv, h, 0, 0)),
        out_shape=jax.ShapeDtypeStruct((nb, 2, groups, nh, NSA_DK), BF16),
        compiler_params=_cparams(3),
        name="cmp_stage2",
    )(ac, pos8, w1a, w1b, w2)


def _rel_bucket(dist):
    n = jnp.maximum(dist, 0)
    nf = jnp.maximum(n, 1).astype(F32)
    scale = (REL_BUCKETS - REL_MAX_EXACT) / math.log(REL_MAX_DIST / REL_MAX_EXACT)
    large = REL_MAX_EXACT + (jnp.log(nf / REL_MAX_EXACT) * scale).astype(jnp.int32)
    large = jnp.minimum(large, REL_BUCKETS - 1)
    return jnp.where(n < REL_MAX_EXACT, n, large)


def _bias_by_dist(rel_table, n):
    tab = rel_table[_rel_bucket(jnp.arange(n))]
    return tab.T.reshape(NSA_KV_HEADS, NSA_GROUP, n)


def _pad_group_rows(t, axis):
    first = lax.slice_in_dim(t, 0, 1, axis=axis)
    return jnp.concatenate([t] + [first] * (SROWS - NSA_GROUP), axis=axis)


def _flash_init(m_ref, l_ref, acc_ref):
    m_ref[...] = jnp.full(m_ref.shape, NEG_INF, F32)
    l_ref[...] = jnp.zeros(l_ref.shape, F32)
    acc_ref[...] = jnp.zeros(acc_ref.shape, F32)


def _flash_step(s, v, m_ref, l_ref, acc_ref):
    m_old = m_ref[...]
    m_new = jnp.maximum(m_old, jnp.max(s, axis=1, keepdims=True))
    alpha = jnp.exp(m_old - m_new)
    p = jnp.exp(s - m_new)
    l_ref[...] = alpha * l_ref[...] + jnp.sum(p, axis=1, keepdims=True)
    acc_ref[...] = alpha * acc_ref[...] + _dot(p.astype(BF16), v)
    m_ref[...] = m_new


def _flash_result(l_ref, acc_ref):
    return acc_ref[...] / jnp.maximum(l_ref[...], 1e-30)


def _select_blocks(imp_t, q0, ns):
    shape = imp_t.shape
    blk = _iota(shape, 0)
    qpos = q0 + _iota(shape, 1)
    cur = qpos >> 6
    valid = blk * SEL_BLOCK <= qpos
    forced = (blk == 0) | (blk == cur) | (blk == cur - 1)
    imp_t = jnp.where(valid, jnp.where(forced, FORCE_SCORE, imp_t), NEG_INF)
    rank = jnp.zeros(shape, F32)
    for other in range(ns):
        row = imp_t[other:other + 1, :]
        ahead = (row > imp_t) | ((row == imp_t) & (blk > other))
        rank = rank + jnp.where(ahead, 1.0, 0.0)
    return jnp.where((rank < SEL_TOPK) & valid, 1.0, 0.0)


def _nsa_prompt_kernel(q_ref, ks_ref, vs_ref, kw_ref, vw_ref, kc_ref, vct_ref, tz_ref, cfar_ref, basec_ref,
                       ovt_ref, ng_ref, nsl_ref, o_ref, m_ref, l_ref, acc_ref, vst_ref, vwt_ref, *, ns):
    i = pl.program_id(2)
    q0 = i * TQ
    qall = q_ref[...] * ((NSA_DK ** -0.5) * LOG2E)
    qt = jnp.concatenate([qall[:, g * NSA_DK:(g + 1) * NSA_DK].T for g in range(NSA_GROUP)], axis=1).astype(BF16)
    c_loc = _iota((TK, NSA_ROWS), 0)
    r_loc = _iota((TK, NSA_ROWS), 1) & (TQ - 1)

    @pl.when(i == 0)
    def _():
        for kt in range(vst_ref.shape[0]):
            vst_ref[kt] = vs_ref[kt * TK:(kt + 1) * TK, :].T.astype(BF16)
            vwt_ref[kt] = vw_ref[kt * TK:(kt + 1) * TK, :].T.astype(BF16)

    ncp = kc_ref.shape[3]
    shift = (TQ // CMP_STRIDE) * i
    bias_c = basec_ref[0, pl.ds(pl.multiple_of(ncp - shift, TQ // CMP_STRIDE), ncp), :]
    s = _dot(kc_ref[0, 0, 0], qt) + bias_c
    m = jnp.max(s, axis=0, keepdims=True)
    e = jnp.exp2(s - m)
    inv = jnp.where(m > 0.5 * NEG_INF, 1.0 / jnp.maximum(jnp.sum(e, axis=0, keepdims=True), 1e-30), 0.0)
    p = e * inv
    o_cmp = _dot(vct_ref[0, 0], p.astype(BF16))

    psum = p[:, 0:TQ] + p[:, TQ:2 * TQ] + p[:, 2 * TQ:3 * TQ] + p[:, 3 * TQ:4 * TQ]
    hi = psum.astype(BF16)
    lo = (psum - hi.astype(F32)).astype(BF16)
    ovt = ovt_ref[...]
    imp_t = _dot(ovt, hi) + _dot(ovt, lo)
    ns8 = -(-ns // 8) * 8
    sel_t = _select_blocks(imp_t[:ns8], q0, ns)
    nsp = ovt.shape[0]
    sel_neg = jnp.where(sel_t > 0.5, 0.0, NEG_INF)
    if ns8 < nsp:
        sel_neg = jnp.concatenate([sel_neg, jnp.full((nsp - ns8, TQ), NEG_INF, F32)], axis=0)
    sel_neg = sel_neg.astype(BF16)

    def flash_step(scs, vts):
        m_old = m_ref[...]
        m_new = m_old
        for sc in scs:
            m_new = jnp.maximum(m_new, jnp.max(sc, axis=0, keepdims=True))
        alpha = jnp.exp2(m_old - m_new)
        l_new = alpha * l_ref[...]
        acc = alpha * acc_ref[...]
        for sc, vt in zip(scs, vts):
            pt = jnp.exp2(sc - m_new)
            l_new = l_new + jnp.sum(pt, axis=0, keepdims=True)
            acc = acc + _dot(vt, pt.astype(BF16))
        l_ref[...] = l_new
        acc_ref[...] = acc
        m_ref[...] = m_new

    def sel_scores(kt, bias, causal):
        k = ks_ref[pl.ds(pl.multiple_of(kt * TK, TK), TK), :].astype(BF16)
        blk_of_key = (TK // SEL_BLOCK) * kt + (_iota((TK, nsp), 0) >> 6)
        expand = jnp.where(_iota((TK, nsp), 1) == blk_of_key, 1.0, 0.0).astype(BF16)
        mk = _dot(expand, sel_neg)
        sc = _dot(k, qt) + bias + jnp.concatenate([mk] * NSA_GROUP, axis=1)
        if causal:
            sc = jnp.where(c_loc <= r_loc, sc, NEG_INF)
        return sc

    _flash_init(m_ref, l_ref, acc_ref)

    @pl.when(i == 0)
    def _():
        flash_step([sel_scores(i, tz_ref[0, 0], True)], [vst_ref[i]])

    @pl.when(i >= 1)
    def _():
        flash_step([sel_scores(i, tz_ref[0, 0], True), sel_scores(i - 1, tz_ref[0, 1], False)],
                   [vst_ref[i], vst_ref[i - 1]])

    def far_tiles(kt_first, count):
        kts = [kt_first - t for t in range(count)]
        flash_step([sel_scores(kt, cfar_ref[0], False) for kt in kts], [vst_ref[kt] for kt in kts])

    def far_group(j, carry):
        far_tiles(i - 2 - FAR_GROUP * j, FAR_GROUP)
        return carry

    n_far = jnp.maximum(i - 1, 0)
    lax.fori_loop(0, n_far // FAR_GROUP, far_group, 0)
    for rest in range(1, FAR_GROUP):
        pl.when(n_far % FAR_GROUP == rest)(functools.partial(far_tiles, rest - 1, rest))

    o_sel = _flash_result(l_ref, acc_ref)

    def win_scores(off):
        kt = i - off
        k = kw_ref[pl.ds(pl.multiple_of(kt * TK, TK), TK), :].astype(BF16)
        sc = _dot(k, qt) + (tz_ref[0, off] if off < 2 else cfar_ref[0])
        if off == 0:
            sc = jnp.where(c_loc <= r_loc, sc, NEG_INF)
        if off * TK == WINDOW:
            sc = jnp.where(c_loc > r_loc, sc, NEG_INF)
        return sc

    def win_tiles(n_tiles):
        flash_step([win_scores(off) for off in range(n_tiles)], [vwt_ref[i - off] for off in range(n_tiles)])

    _flash_init(m_ref, l_ref, acc_ref)
    n_win = WINDOW // TK + 1
    for n_tiles in range(1, n_win + 1):
        cond = (i == n_tiles - 1) if n_tiles < n_win else (i >= n_win - 1)
        pl.when(cond)(functools.partial(win_tiles, n_tiles))
    o_win = _flash_result(l_ref, acc_ref)

    gates_t = _sigmoid(ng_ref[...]).T
    nsl = nsl_ref[...]
    outs = []
    for g in range(NSA_GROUP):
        cols = slice(g * TQ, (g + 1) * TQ)
        o_t = (gates_t[3 * g:3 * g + 1] * o_cmp[:, cols] + gates_t[3 * g + 1:3 * g + 2] * o_sel[:, cols]
               + gates_t[3 * g + 2:3 * g + 3] * o_win[:, cols])
        x = nsl[:, g * NSA_DK:(g + 1) * NSA_DK]
        outs.append(o_t.T * (x * _sigmoid(x)))
    o_ref[...] = jnp.concatenate(outs, axis=1).astype(BF16)


def _lanes_by_head(t):
    hk, g, rows, tq = t.shape
    return t.transpose(0, 2, 1, 3).reshape(hk, rows, g * tq)


def _nsa_prompt(z, zb, kcvc, bias_d, batch, seq):
    nq = seq // TQ
    ns = seq // SEL_BLOCK
    ncp = kcvc.shape[3]
    nsp = LANE
    wn = TQ // CMP_STRIDE
    assert TQ == TK and ns <= nsp and ncp >= seq // CMP_STRIDE and ncp % LANE == 0 and ncp > wn
    gw = NSA_GROUP * NSA_DK
    n_dist = bias_d.shape[-1]
    assert n_dist >= 2 * TK + TQ
    bias_d = bias_d * LOG2E

    rep = jnp.tile(bias_d, (1, 1, TK + 1))[..., :TK * (n_dist - 1)].reshape(NSA_KV_HEADS, NSA_GROUP, TK, n_dist - 1)
    tz = jnp.stack([_lanes_by_head(rep[..., 0:TQ]), _lanes_by_head(rep[..., TK:TK + TQ])], axis=1)
    far = jnp.broadcast_to(bias_d[..., REL_MAX_DIST][:, :, None, None], (NSA_KV_HEADS, NSA_GROUP, 1, TQ))
    cfar = _lanes_by_head(far)
    half = n_dist // 2
    start = CMP_STRIDE * wn - (CMP_BLOCK - 1)
    assert start + TQ <= half and 2 * wn * CMP_STRIDE - start <= half
    w_ext = jnp.concatenate([bias_d[..., :half], jnp.full(bias_d.shape[:-1] + (n_dist - half,), NEG_INF, F32)], -1)
    near = jnp.tile(w_ext, (1, 1, 2 * wn + 1))[..., :2 * wn * (n_dist - CMP_STRIDE)]
    near = near.reshape(NSA_KV_HEADS, NSA_GROUP, 2 * wn, n_dist - CMP_STRIDE)[..., start:start + TQ]
    basec = jnp.concatenate([jnp.broadcast_to(far, (NSA_KV_HEADS, NSA_GROUP, ncp - wn, TQ)), near,
                             jnp.full((NSA_KV_HEADS, NSA_GROUP, ncp - wn, TQ), NEG_INF, F32)], axis=2)
    basec = _lanes_by_head(basec)
    sblk = jnp.arange(nsp)[:, None]
    nblk = jnp.arange(ncp)[None, :]
    ovt = ((nblk >= 4 * sblk - 1) & (nblk <= 4 * sblk + 3)).astype(BF16)

    vct = kcvc[:, 1].transpose(0, 1, 3, 2)

    def kvspec(col0, which):
        return pl.BlockSpec((seq, NSA_DK), lambda b, h, i: (b, col0 // NSA_DK + which * NSA_KV_HEADS + h))

    vt_scratch = pltpu.VMEM((seq // TK, NSA_DK, TK), BF16)
    return pl.pallas_call(
        functools.partial(_nsa_prompt_kernel, ns=ns),
        grid=(batch, NSA_KV_HEADS, nq),
        in_specs=[pl.BlockSpec((TQ, gw), lambda b, h, i: (b * nq + i, COL_NQ // gw + h)),
                  kvspec(COL_KVS, 0), kvspec(COL_KVS, 1), kvspec(COL_KVW, 0), kvspec(COL_KVW, 1),
                  pl.BlockSpec((1, 1, 1, ncp, NSA_DK), lambda b, h, i: (b, 0, h, 0, 0)),
                  pl.BlockSpec((1, 1, NSA_DK, ncp), lambda b, h, i: (b, h, 0, 0)),
                  pl.BlockSpec((1, 2, TK, NSA_ROWS), lambda b, h, i: (h, 0, 0, 0)),
                  pl.BlockSpec((1, 1, NSA_ROWS), lambda b, h, i: (h, 0, 0)),
                  pl.BlockSpec((1, 2 * ncp, NSA_ROWS), lambda b, h, i: (h, 0, 0)),
                  pl.BlockSpec((nsp, ncp), lambda b, h, i: (0, 0)),
                  pl.BlockSpec((TQ, LANE), lambda b, h, i: (b * nq + i, COL_NG // LANE + h)),
                  pl.BlockSpec((TQ, gw), lambda b, h, i: (b * nq + i, COL_NSL // gw + h))],
        out_specs=pl.BlockSpec((TQ, gw), lambda b, h, i: (b * nq + i, h)),
        out_shape=jax.ShapeDtypeStruct((batch * seq, NSA_W), BF16),
        scratch_shapes=[pltpu.VMEM((1, NSA_ROWS), F32), pltpu.VMEM((1, NSA_ROWS), F32),
                        pltpu.VMEM((NSA_DK, NSA_ROWS), F32), vt_scratch, vt_scratch],
        compiler_params=_cparams(3),
        name="nsa_prompt",
    )(z, z, z, z, z, kcvc, vct, tz, cfar, basec, ovt, zb, zb)


SROWS = 8
SEL_PER_STEP = 2


def _stack_group_q(q_row):
    heads = [q_row[:, g * NSA_DK:(g + 1) * NSA_DK] for g in range(NSA_GROUP)]
    return jnp.concatenate(heads + [heads[0]] * (SROWS - NSA_GROUP), axis=0)


def _nsa_sample_cmp_kernel(q_ref, kc_ref, vc_ref, bias_ref, ov_ref, o_ref, idx_ref, *, ns):
    scale = NSA_DK ** -0.5
    qs = _stack_group_q(q_ref[0]).astype(BF16)
    s = _nt(qs, kc_ref[0, 0]) * scale + bias_ref[0]
    m = jnp.max(s, axis=1, keepdims=True)
    e = jnp.exp(s - m)
    inv = jnp.where(m > 0.5 * NEG_INF, 1.0 / jnp.maximum(jnp.sum(e, axis=1, keepdims=True), 1e-30), 0.0)
    p = e * inv
    o_ref[0, 0] = _dot(p.astype(BF16), vc_ref[0, 0])
    psum = jnp.broadcast_to(jnp.sum(p[0:NSA_GROUP], axis=0, keepdims=True), p.shape)
    hi = psum.astype(BF16)
    lo = (psum - hi.astype(F32)).astype(BF16)
    imp = (_dot(hi, ov_ref[...]) + _dot(lo, ov_ref[...]))[0:1]
    nsp = imp.shape[1]
    blk_r = _iota((1, nsp), 1)
    cur = ns - 1
    forced = (blk_r == 0) | (blk_r == cur) | (blk_r == cur - 1)
    imp = jnp.where(blk_r < ns, jnp.where(forced, FORCE_SCORE, imp), 2.0 * NEG_INF)
    imp_c = _column_of(imp)
    i_r = _iota((nsp, nsp), 1)
    j_c = _iota((nsp, nsp), 0)
    ahead = (imp > imp_c) | ((imp == imp_c) & (i_r < j_c))
    rank_c = jnp.sum(jnp.where(ahead, 1.0, 0.0), axis=1, keepdims=True)
    slot = _iota((nsp, LANE), 1).astype(F32)
    picks = jnp.where(rank_c == slot, _iota((nsp, LANE), 0).astype(F32), 0.0)
    idx_ref[0, 0] = jnp.broadcast_to(jnp.sum(picks, axis=0, keepdims=True), (SROWS, LANE)).astype(jnp.int32)


def _nsa_sample_cmp(z3, kcvc, bias_c, ov, ns):
    nb = z3.shape[0]
    ncp = kcvc.shape[2]
    nsp = ov.shape[1]
    gw = NSA_GROUP * NSA_DK

    def cspec(which):
        return pl.BlockSpec((1, 1, ncp, NSA_DK), lambda b, h: (b, which, 0, h))

    return pl.pallas_call(
        functools.partial(_nsa_sample_cmp_kernel, ns=ns),
        grid=(nb, NSA_KV_HEADS),
        in_specs=[pl.BlockSpec((1, 1, gw), lambda b, h: (b, 0, COL_NQ // gw + h)),
                  cspec(0), cspec(1),
                  pl.BlockSpec((1, SROWS, ncp), lambda b, h: (h, 0, 0)),
                  pl.BlockSpec((ncp, nsp), lambda b, h: (0, 0))],
        out_specs=[pl.BlockSpec((1, 1, SROWS, NSA_DK), lambda b, h: (b, h, 0, 0)),
                   pl.BlockSpec((1, 1, SROWS, LANE), lambda b, h: (b, h, 0, 0))],
        out_shape=[jax.ShapeDtypeStruct((nb, NSA_KV_HEADS, SROWS, NSA_DK), F32),
                   jax.ShapeDtypeStruct((nb, NSA_KV_HEADS, SROWS, LANE), jnp.int32)],
        compiler_params=_cparams(2),
        name="nsa_sample_cmp",
    )(z3, kcvc, kcvc, bias_c, ov)


def _nsa_sample_sel_kernel(pt_ref, idx_ref, q_ref, *refs, ns):
    n_blk = NSA_KV_HEADS * SEL_PER_STEP
    blocks, (new_ref,), biases = refs[:n_blk], refs[n_blk:n_blk + 1], refs[n_blk + 1:2 * n_blk + 1]
    o_ref, m_ref, l_ref, acc_ref = refs[2 * n_blk + 1:]
    b, t = pl.program_id(0), pl.program_id(1)
    gw = NSA_GROUP * NSA_DK
    rows_kv = 2 * NSA_KV_HEADS
    width = SEL_BLOCK * rows_kv

    @pl.when(t == 0)
    def _():
        _flash_init(m_ref, l_ref, acc_ref)

    new_rows = jnp.concatenate([new_ref[0]] * SEL_BLOCK, axis=0)
    xs, scs = [], []
    for h in range(NSA_KV_HEADS):
        qs = _stack_group_q(q_ref[0][:, h * gw:(h + 1) * gw]).astype(BF16)
        row = []
        for u in range(SEL_PER_STEP):
            r = h * SEL_PER_STEP + u
            is_new = idx_ref[b, h, t * SEL_PER_STEP + u] == ns - 1
            x = jnp.where(is_new, new_rows, blocks[r][0].reshape(width, NSA_DK)).astype(BF16)
            xs.append(x)
            row.append(_nt(qs, x) * (NSA_DK ** -0.5) + biases[r][0, 0])
        scs.append(jnp.concatenate(row, axis=1))
    sc = jnp.concatenate(scs, axis=0)
    m_old = m_ref[...]
    m_new = jnp.maximum(m_old, jnp.max(sc, axis=1, keepdims=True))
    alpha = jnp.exp(m_old - m_new)
    p = jnp.exp(sc - m_new)
    l_ref[...] = alpha * l_ref[...] + jnp.sum(p, axis=1, keepdims=True)
    pv = pltpu.roll(p, NSA_KV_HEADS, 1).astype(BF16)
    acc = alpha * acc_ref[...]
    upd = []
    for h in range(NSA_KV_HEADS):
        ph = pv[h * SROWS:(h + 1) * SROWS]
        upd.append(sum(_dot(ph[:, u * width:(u + 1) * width], xs[h * SEL_PER_STEP + u]) for u in range(SEL_PER_STEP)))
    acc_ref[...] = acc + jnp.concatenate(upd, axis=0)
    m_ref[...] = m_new

    @pl.when(t == pl.num_programs(1) - 1)
    def _():
        o_ref[0] = (acc_ref[...] / jnp.maximum(l_ref[...], 1e-30)).reshape(NSA_KV_HEADS, SROWS, NSA_DK)


def _nsa_sample_sel(z3, cache, kv_new, page_table, idx, bias_sel, ns):
    nb, n_pages = page_table.shape
    n_sel = idx.shape[2]
    halves = PAGE_SIZE // SEL_BLOCK
    rows_kv = 2 * NSA_KV_HEADS

    per_step = SEL_PER_STEP
    assert n_sel % per_step == 0

    def blockspec(h, u):
        def index(b, t, pt, ix):
            blk = ix[b, h, t * per_step + u]
            return (pt[b, jnp.minimum(blk // halves, n_pages - 1)], blk % halves, 0, 0)
        return pl.BlockSpec((1, SEL_BLOCK, rows_kv, NSA_DK), index)

    def biasspec(h, u):
        return pl.BlockSpec((1, 1, SROWS, SEL_BLOCK * rows_kv),
                            lambda b, t, pt, ix: (h, ix[b, h, t * per_step + u], 0, 0))

    slots = [(h, u) for h in range(NSA_KV_HEADS) for u in range(per_step)]
    return pl.pallas_call(
        functools.partial(_nsa_sample_sel_kernel, ns=ns),
        grid_spec=pltpu.PrefetchScalarGridSpec(
            num_scalar_prefetch=2,
            grid=(nb, n_sel // per_step),
            in_specs=[pl.BlockSpec((1, 1, NSA_W), lambda b, t, pt, ix: (b, 0, COL_NQ // NSA_W))]
                     + [blockspec(h, u) for h, u in slots]
                     + [pl.BlockSpec((1, rows_kv, NSA_DK), lambda b, t, pt, ix: (b, 0, 0))]
                     + [biasspec(h, u) for h, u in slots],
            out_specs=pl.BlockSpec((1, NSA_KV_HEADS, SROWS, NSA_DK), lambda b, t, pt, ix: (b, 0, 0, 0)),
            scratch_shapes=[pltpu.VMEM((NSA_KV_HEADS * SROWS, 1), F32), pltpu.VMEM((NSA_KV_HEADS * SROWS, 1), F32),
                            pltpu.VMEM((NSA_KV_HEADS * SROWS, NSA_DK), F32)]),
        out_shape=jax.ShapeDtypeStruct((nb, NSA_KV_HEADS, SROWS, NSA_DK), F32),
        compiler_params=_cparams(2),
        name="nsa_sample_sel",
    )(page_table, idx, z3, *([cache] * len(slots)), kv_new, *([bias_sel] * len(slots)))


def _nsa_sample_win_kernel(q_ref, k_ref, v_ref, kn_ref, vn_ref, bias_ref, bnew_ref, o_ref):
    scale = NSA_DK ** -0.5
    q = _stack_group_q(q_ref[0])
    s_buf = _nt(q.astype(BF16), k_ref[0].astype(BF16)) * scale + bias_ref[0]
    s_new = jnp.sum(q * kn_ref[0], axis=1, keepdims=True) * scale + bnew_ref[0]
    m = jnp.maximum(jnp.max(s_buf, axis=1, keepdims=True), s_new)
    p_buf = jnp.exp(s_buf - m)
    p_new = jnp.exp(s_new - m)
    l = jnp.sum(p_buf, axis=1, keepdims=True) + p_new
    acc = _dot(p_buf.astype(BF16), v_ref[0].astype(BF16)) + p_new * vn_ref[0]
    o_ref[0, 0] = acc / jnp.maximum(l, 1e-30)


def _nsa_sample_win(z3, win_buf, bias_win, bias_new):
    nb, nbuf, _ = win_buf.shape
    gw = NSA_GROUP * NSA_DK

    def bufspec(which):
        return pl.BlockSpec((1, nbuf, NSA_DK), lambda b, h: (b, 0, which * NSA_KV_HEADS + h))

    def newspec(which):
        return pl.BlockSpec((1, 1, NSA_DK), lambda b, h: (b, 0, COL_KVW // NSA_DK + which * NSA_KV_HEADS + h))

    return pl.pallas_call(
        _nsa_sample_win_kernel,
        grid=(nb, NSA_KV_HEADS),
        in_specs=[pl.BlockSpec((1, 1, gw), lambda b, h: (b, 0, COL_NQ // gw + h)),
                  bufspec(0), bufspec(1), newspec(0), newspec(1),
                  pl.BlockSpec((1, SROWS, nbuf), lambda b, h: (h, 0, 0)),
                  pl.BlockSpec((1, SROWS, 1), lambda b, h: (h, 0, 0))],
        out_specs=pl.BlockSpec((1, 1, SROWS, NSA_DK), lambda b, h: (b, h, 0, 0)),
        out_shape=jax.ShapeDtypeStruct((nb, NSA_KV_HEADS, SROWS, NSA_DK), F32),
        compiler_params=_cparams(2),
        name="nsa_sample_win",
    )(z3, win_buf, win_buf, z3, z3, bias_win, bias_new)


def _nsa_sample_gate_kernel(oc_ref, os_ref, ow_ref, ng_ref, nsl_ref, o_ref):
    gates = _sigmoid(ng_ref[0])
    nsl = nsl_ref[0]
    outs = []
    for g in range(NSA_GROUP):
        o = (gates[:, 3 * g:3 * g + 1] * oc_ref[0, 0, g:g + 1] + gates[:, 3 * g + 1:3 * g + 2] * os_ref[0, 0, g:g + 1]
             + gates[:, 3 * g + 2:3 * g + 3] * ow_ref[0, 0, g:g + 1])
        x = nsl[:, g * NSA_DK:(g + 1) * NSA_DK]
        outs.append(o * (x * _sigmoid(x)))
    o_ref[0] = jnp.concatenate(outs, axis=1).astype(BF16)


def _nsa_sample_gate(o_cmp, o_sel, o_win, z3b):
    nb = z3b.shape[0]
    gw = NSA_GROUP * NSA_DK
    ospec = pl.BlockSpec((1, 1, SROWS, NSA_DK), lambda b, h: (b, h, 0, 0))
    return pl.pallas_call(
        _nsa_sample_gate_kernel,
        grid=(nb, NSA_KV_HEADS),
        in_specs=[ospec, ospec, ospec,
                  pl.BlockSpec((1, 1, LANE), lambda b, h: (b, 0, COL_NG // LANE + h)),
                  pl.BlockSpec((1, 1, gw), lambda b, h: (b, 0, COL_NSL // gw + h))],
        out_specs=pl.BlockSpec((1, 1, gw), lambda b, h: (b, 0, h)),
        out_shape=jax.ShapeDtypeStruct((nb, 1, NSA_W), BF16),
        compiler_params=_cparams(2),
        name="nsa_sample_gate",
    )(o_cmp, o_sel, o_win, z3b, z3b)


def _mem_heads(q, kv):
    outs = []
    for h in range(MEM_HEADS):
        k = kv[:, h * MEM_DH:(h + 1) * MEM_DH].astype(BF16)
        v = kv[:, MEM_W + h * MEM_DH:MEM_W + (h + 1) * MEM_DH].astype(BF16)
        s = _nt(q[:, h * MEM_DH:(h + 1) * MEM_DH].astype(BF16), k) * (MEM_DH ** -0.5)
        e = jnp.exp(s - jnp.max(s, axis=1, keepdims=True))
        p = e / jnp.sum(e, axis=1, keepdims=True)
        outs.append(_dot(p.astype(BF16), v))
    return jnp.concatenate(outs, axis=1)


def _mem_prompt_kernel(q_ref, kv_ref, o_ref):
    o_ref[...] = _mem_heads(q_ref[...], kv_ref[...]).astype(BF16)


def _mem_prompt(z, mem_kv, batch, seq, tq):
    nq = seq // tq
    n_mem = mem_kv.shape[0] // batch
    return pl.pallas_call(
        _mem_prompt_kernel,
        grid=(batch, nq),
        in_specs=[pl.BlockSpec((tq, MEM_W), lambda b, i: (b * nq + i, COL_MQ // MEM_W)),
                  pl.BlockSpec((n_mem, 2 * MEM_W), lambda b, i: (b, 0))],
        out_specs=pl.BlockSpec((tq, MEM_W), lambda b, i: (b * nq + i, 0)),
        out_shape=jax.ShapeDtypeStruct((batch * seq, MEM_W), BF16),
        compiler_params=_cparams(2),
        name="mem_prompt",
    )(z, mem_kv)


def _mem_sample_kernel(q_ref, kv_ref, o_ref):
    q = jnp.broadcast_to(q_ref[0], (SROWS, MEM_W))
    o_ref[0] = _mem_heads(q, kv_ref[0])[0:1].astype(BF16)


def _mem_sample(z3, mem_kv):
    nb, n_mem, _ = mem_kv.shape
    return pl.pallas_call(
        _mem_sample_kernel,
        grid=(nb,),
        in_specs=[pl.BlockSpec((1, 1, MEM_W), lambda b: (b, 0, COL_MQ // MEM_W)),
                  pl.BlockSpec((1, n_mem, 2 * MEM_W), lambda b: (b, 0, 0))],
        out_specs=pl.BlockSpec((1, 1, MEM_W), lambda b: (b, 0, 0)),
        out_shape=jax.ShapeDtypeStruct((nb, 1, MEM_W), BF16),
        compiler_params=_cparams(1),
        name="mem_sample",
    )(z3, mem_kv)


def _merge_kernel(ar_ref, an_ref, am_ref, wr_ref, wn_ref, wm_ref, g0_ref, g1_ref, g2_ref, o_ref):
    merged = (_sigmoid(g0_ref[...]) * _dot(ar_ref[...], wr_ref[...])
              + _sigmoid(g1_ref[...]) * _dot(an_ref[...], wn_ref[...])
              + _sigmoid(g2_ref[...]) * _dot(am_ref[...], wm_ref[...]))
    o_ref[...] = merged.astype(BF16)


def _merge(a_ret, a_nsa, a_mem, w_ret, w_nsa, w_mem, z, tm, tn):
    m = a_ret.shape[0]
    nt = D_MODEL // tn

    def aspec(width):
        return pl.BlockSpec((tm, width), lambda i, j: (i, 0))

    def wspec(width):
        return pl.BlockSpec((width, tn), lambda i, j: (0, j))

    def gspec(branch):
        return pl.BlockSpec((tm, tn), lambda i, j: (i, COL_MG // tn + branch * nt + j))

    return pl.pallas_call(
        _merge_kernel,
        grid=(m // tm, nt),
        in_specs=[aspec(RET_W), aspec(NSA_W), aspec(MEM_W), wspec(RET_W), wspec(NSA_W), wspec(MEM_W),
                  gspec(0), gspec(1), gspec(2)],
        out_specs=pl.BlockSpec((tm, tn), lambda i, j: (i, j)),
        out_shape=jax.ShapeDtypeStruct((m, D_MODEL), BF16),
        compiler_params=_cparams(2),
        name="merge",
    )(a_ret, a_nsa, a_mem, w_ret, w_nsa, w_mem, z, z, z)


def _out_kernel(a_ref, w_ref, x_ref, g_ref, o_ref):
    out = _dot(a_ref[...], w_ref[...])
    y = out * lax.rsqrt(jnp.mean(out * out, axis=-1, keepdims=True) + EPS)
    o_ref[...] = x_ref[...] + y * g_ref[...]


def _out_proj(merged, w_out, x, norm_post, tm):
    m = merged.shape[0]
    return pl.pallas_call(
        _out_kernel,
        grid=(m // tm,),
        in_specs=[pl.BlockSpec((tm, D_MODEL), lambda i: (i, 0)),
                  pl.BlockSpec((D_MODEL, D_MODEL), lambda i: (0, 0)),
                  pl.BlockSpec((tm, D_MODEL), lambda i: (i, 0)),
                  pl.BlockSpec((1, D_MODEL), lambda i: (0, 0))],
        out_specs=pl.BlockSpec((tm, D_MODEL), lambda i: (i, 0)),
        out_shape=jax.ShapeDtypeStruct((m, D_MODEL), F32),
        compiler_params=_cparams(1),
        name="out_proj",
    )(merged, w_out, x, norm_post.reshape(1, D_MODEL))


def _layout_w_ng(w_t):
    per_group = N_BRANCHES * NSA_GROUP
    ng = w_t[PROJ_A:PROJ_A + N_BRANCHES * NSA_HEADS].reshape(NSA_KV_HEADS, per_group, D_MODEL)
    return jnp.pad(ng, ((0, 0), (0, LANE - per_group), (0, 0))).reshape(NG_SLOT, D_MODEL)


def _pick_tile(m, cap):
    t = min(m, cap)
    while m % t:
        t //= 2
    return t


def kernel(x_prompt, x_sample, cache_cmp_kv, cache_sel_kv, cache_win_kv, state_ret, cache_mem_kv, page_table,
           mem_prompt, rel_table, norm_pre, norm_post, norm_mem, w_in, ret_norm, w_ret_up, cmp_pos, w_cmp1,
           w_cmp2, w_nsa_up, w_mem_kv, w_mem_up, w_out):
    batch, seq, _ = x_prompt.shape
    nb = x_sample.shape[0]
    assert x_sample.shape[1] == 1 and norm_pre.shape[0] == 1
    assert seq % TQ == 0 and seq >= WINDOW
    n_pool = cache_cmp_kv.shape[1]
    n_pages = page_table.shape[1]
    past = n_pages * PAGE_SIZE
    n_mem = mem_prompt.shape[1]
    assert n_pages % PAGES_PER_STEP == 0 and cache_win_kv.shape[2] == WINDOW

    w_a = w_in[0].T
    w_ng = _layout_w_ng(w_a)
    kw = CMP_STRIDE * NSA_DK
    w1 = w_cmp1[0].reshape(2, CMP_BLOCK * NSA_DK, NSA_DK).astype(BF16)
    w1a, w1b = w1[:, :kw], w1[:, kw:]
    w1ab = jnp.concatenate([w1a, w1b], axis=2)
    w2 = w_cmp2[0].astype(BF16)
    pos8 = jnp.pad(cmp_pos[0].reshape(2, 1, CMP_BLOCK * NSA_DK), ((0, 0), (0, 7), (0, 0)))
    w_ret = w_ret_up[0].astype(BF16)
    w_nsa = w_nsa_up[0].astype(BF16)
    w_mem = w_mem_up[0].astype(BF16)
    w_o = w_out[0].astype(BF16)

    m_p = batch * seq
    xp = x_prompt.reshape(m_p, D_MODEL)
    hp = _rmsnorm(xp, norm_pre[0], _pick_tile(m_p, 512))
    z = _proj(hp, w_a, _pick_tile(m_p, 1024), PROJ_TN, PROJ_A, transposed=True)
    zb = _proj_tail(hp, w_a, w_ng, _pick_tile(m_p, 1024))

    a_ret, ret_state_p = _retention_prompt(z, ret_norm[0], batch, seq)

    ncp = max(LANE, -(-(seq // CMP_STRIDE) // LANE) * LANE)
    ac = _cmp_stage1_dense(z, w1ab, batch, seq)
    kcvc = _cmp_stage2(ac, pos8, w1a, w1b, w2, 1)
    if ncp > kcvc.shape[3]:
        kcvc = jnp.pad(kcvc, ((0, 0), (0, 0), (0, 0), (0, ncp - kcvc.shape[3]), (0, 0)))
    bias_d = _bias_by_dist(rel_table, BIAS_DISTS)
    a_nsa = _nsa_prompt(z, zb, kcvc, bias_d, batch, seq)

    hm = _rmsnorm(mem_prompt.reshape(batch * n_mem, D_MODEL), norm_mem[0], _pick_tile(batch * n_mem, 512))
    mem_kv_p = _proj(hm, w_mem_kv[0], _pick_tile(batch * n_mem, 512), PROJ_TN)
    a_mem = _mem_prompt(zb, mem_kv_p, batch, seq, _pick_tile(seq, 512))

    merged = _merge(a_ret, a_nsa, a_mem, w_ret, w_nsa, w_mem, zb, _pick_tile(m_p, 512), 512)
    y_p = _out_proj(merged, w_o, xp, norm_post[0], _pick_tile(m_p, 256)).reshape(batch, seq, D_MODEL)

    kv_shape = (1, batch, seq, 2, NSA_KV_HEADS, NSA_DK)
    new_cmp_p = z[:, COL_KVC:COL_KVC + 2 * KV_W].reshape(kv_shape)
    new_sel_p = z[:, COL_KVS:COL_KVS + 2 * KV_W].reshape(kv_shape)
    new_win_p = z[:, COL_KVW:COL_KVW + 2 * KV_W].reshape(kv_shape)[:, :, seq - WINDOW:]
    new_ret_p = ret_state_p[None]
    new_mem_p = mem_kv_p.reshape(1, batch, n_mem, 2, MEM_HEADS, MEM_DH)

    xs = x_sample.reshape(nb, D_MODEL)
    hs = _rmsnorm(xs, norm_pre[0], nb)
    zs = _proj(hs, w_a, nb, PROJ_TN, PROJ_A, transposed=True)
    zsb = _proj_tail(hs, w_a, w_ng, nb)
    z3 = zs.reshape(nb, 1, PROJ_A)
    z3b = zsb.reshape(nb, 1, PROJ_B)

    a_ret_s, ret_state_s = _retention_sample(z3, state_ret[0], ret_norm[0], past)

    cache_c = cache_cmp_kv[0].reshape(n_pool, PAGE_SIZE // CMP_STRIDE, CMP_STRIDE, 2 * NSA_KV_HEADS, NSA_DK)
    cache_s = cache_sel_kv[0].reshape(n_pool, PAGE_SIZE, 2 * NSA_KV_HEADS, NSA_DK)
    ac_s = _cmp_stage1_paged(cache_c, page_table, w1ab)
    kcvc_s = _cmp_stage2(ac_s, pos8, w1a, w1b, w2, NSA_KV_HEADS)
    ncs = past // CMP_STRIDE
    kcvc_s = kcvc_s.reshape(nb, 2, ncs, KV_W)
    ns_s = past // SEL_BLOCK + 1
    nsp_s = -(-ns_s // LANE) * LANE
    assert past >= WINDOW and past >= REL_MAX_DIST and BIAS_DISTS > WINDOW
    far_s = bias_d[..., REL_MAX_DIST:REL_MAX_DIST + 1]
    hg = (NSA_KV_HEADS, NSA_GROUP)
    n_valid = (past - (CMP_BLOCK - 1)) // CMP_STRIDE + 1
    strided = bias_d[..., (past - (CMP_BLOCK - 1)) % CMP_STRIDE::CMP_STRIDE]
    n_tab = strided.shape[-1]
    assert n_valid >= n_tab and n_tab * CMP_STRIDE > REL_MAX_DIST + CMP_STRIDE and ncs >= n_valid
    bias_cs = jnp.concatenate([jnp.broadcast_to(far_s, hg + (n_valid - n_tab,)), strided[..., ::-1],
                               jnp.full(hg + (ncs - n_valid,), NEG_INF, F32)], axis=-1)
    bias_cs = _pad_group_rows(bias_cs, 1)
    nblk = jnp.arange(ncs)[:, None]
    sblk = jnp.arange(nsp_s)[None, :]
    ov_s = ((nblk >= 4 * sblk - 1) & (nblk <= 4 * sblk + 3)).astype(BF16)
    o_cmp_s, idx_s = _nsa_sample_cmp(z3, kcvc_s, bias_cs, ov_s, ns_s)
    n_sel = min(SEL_TOPK, ns_s)
    idx = idx_s[:, :, 0, :n_sel]

    hg = (NSA_KV_HEADS, NSA_GROUP)
    n_key = ns_s * SEL_BLOCK
    bias_sel = jnp.concatenate([jnp.broadcast_to(far_s, hg + (past + 1 - REL_MAX_DIST,)),
                                bias_d[..., :REL_MAX_DIST][..., ::-1],
                                jnp.full(hg + (n_key - past - 1,), NEG_INF, F32)], axis=-1)
    bias_sel = _pad_group_rows(bias_sel.reshape(hg + (ns_s, SEL_BLOCK)).transpose(0, 2, 1, 3), 2)
    own_k = jnp.arange(2 * NSA_KV_HEADS)[None, :] == jnp.arange(NSA_KV_HEADS)[:, None]
    bias_sel = jnp.where(own_k[:, None, None, None, :], bias_sel[..., None], NEG_INF)
    bias_sel = bias_sel.reshape(NSA_KV_HEADS, ns_s, SROWS, SEL_BLOCK * 2 * NSA_KV_HEADS)
    kvs_new = zs[:, COL_KVS:COL_KVS + 2 * KV_W].reshape(nb, 2 * NSA_KV_HEADS, NSA_DK)
    o_sel_s = _nsa_sample_sel(z3, cache_s, kvs_new, page_table, idx, bias_sel, ns_s)

    win_buf = cache_win_kv[0].reshape(nb, WINDOW, 2 * KV_W)
    bias_w = jnp.concatenate([jnp.full(hg + (1,), NEG_INF, F32), bias_d[..., 1:WINDOW][..., ::-1]], axis=-1)
    o_win_s = _nsa_sample_win(z3, win_buf, _pad_group_rows(bias_w, 1), _pad_group_rows(bias_d[..., 0:1], 1))
    a_nsa_s = _nsa_sample_gate(o_cmp_s, o_sel_s, o_win_s, z3b)

    mem_kv_s = cache_mem_kv[0].reshape(nb, n_mem, 2 * MEM_W)
    a_mem_s = _mem_sample(z3b, mem_kv_s)

    merged_s = _merge(a_ret_s.reshape(nb, RET_W), a_nsa_s.reshape(nb, NSA_W), a_mem_s.reshape(nb, MEM_W),
                      w_ret, w_nsa, w_mem, zsb, nb, 512)
    y_s = _out_proj(merged_s, w_o, xs, norm_post[0], nb).reshape(nb, 1, D_MODEL)

    kvs_shape = (1, nb, 1, 2, NSA_KV_HEADS, NSA_DK)
    new_cmp_s = zs[:, COL_KVC:COL_KVC + 2 * KV_W].reshape(kvs_shape)
    new_sel_s = zs[:, COL_KVS:COL_KVS + 2 * KV_W].reshape(kvs_shape)
    kvw_s = zs[:, COL_KVW:COL_KVW + 2 * KV_W].reshape(nb, 1, 2, NSA_KV_HEADS, NSA_DK)
    new_win_s = jnp.concatenate([cache_win_kv[0][:, 1:], kvw_s], axis=1)[None]
    new_ret_s = ret_state_s[None]

    return (y_p, y_s, new_cmp_p, new_sel_p, new_win_p, new_ret_p, new_mem_p,
            new_cmp_s, new_sel_s, new_win_s, new_ret_s)
```

```python
import functools
import math

import jax
import jax.numpy as jnp
from jax import lax
from jax.experimental import pallas as pl
from jax.experimental.pallas import tpu as pltpu

F32 = jnp.float32
BF16 = jnp.bfloat16

D_MODEL = 2048
PAGE_SIZE = 128
RET_HEADS = 8
RET_DK = 256
RET_DV = 256
RET_CHUNK = 128
ROPE_BASE = 10000.0
NSA_HEADS = 16
NSA_KV_HEADS = 4
NSA_GROUP = NSA_HEADS // NSA_KV_HEADS
NSA_DK = 128
CMP_BLOCK = 32
CMP_STRIDE = 16
SEL_BLOCK = 64
SEL_TOPK = 16
WINDOW = 512
MEM_HEADS = 4
MEM_DH = 384
REL_BUCKETS = 32
REL_MAX_EXACT = 16
REL_MAX_DIST = 128
N_BRANCHES = 3
EPS = 1e-6
NEG_INF = -1e30
FORCE_SCORE = 1e4

RET_W = RET_HEADS * RET_DV
NSA_W = NSA_HEADS * NSA_DK
KV_W = NSA_KV_HEADS * NSA_DK
MEM_W = MEM_HEADS * MEM_DH

COL_RQ = 0
COL_RK = COL_RQ + RET_HEADS * RET_DK
COL_RV = COL_RK + RET_HEADS * RET_DK
COL_RG = COL_RV + RET_W
COL_NQ = COL_RG + RET_W
COL_KVC = COL_NQ + NSA_W
COL_KVS = COL_KVC + 2 * KV_W
COL_KVW = COL_KVS + 2 * KV_W
PROJ_A = COL_KVW + 2 * KV_W
COL_NSL = 0
COL_MG = COL_NSL + NSA_W
COL_MQ = COL_MG + N_BRANCHES * D_MODEL
COL_NG = COL_MQ + MEM_W
NG_SLOT = NSA_KV_HEADS * 128
PROJ_B = COL_NG + NG_SLOT
MQ_BLOCK = MEM_W + NG_SLOT

LOG2E = math.log2(math.e)
LANE = 128
TQ = 256
TK = 256
NSA_ROWS = NSA_GROUP * TQ
BIAS_DISTS = 1024
FAR_GROUP = 4
RET_STEP_CHUNKS = 8
PROJ_TN = 1024
VMEM_LIMIT = 56 * 1024 * 1024


def _cparams(n_axes):
    return pltpu.CompilerParams(dimension_semantics=("arbitrary",) * n_axes, vmem_limit_bytes=VMEM_LIMIT)


def _nt(a, b):
    return lax.dot_general(a, b, (((1,), (1,)), ((), ())), preferred_element_type=F32)


def _dot(a, b):
    return jnp.dot(a, b, preferred_element_type=F32)


def _sigmoid(x):
    return 1.0 / (1.0 + jnp.exp(-x))


def _iota(shape, dim):
    return lax.broadcasted_iota(jnp.int32, shape, dim)


def _rmsnorm_kernel(x_ref, g_ref, h_ref):
    x = x_ref[...]
    ms = jnp.mean(x * x, axis=-1, keepdims=True)
    h_ref[...] = ((x * lax.rsqrt(ms + EPS)) * g_ref[...]).astype(BF16)


def _rmsnorm(x, g, tm):
    m, k = x.shape
    return pl.pallas_call(
        _rmsnorm_kernel,
        grid=(m // tm,),
        in_specs=[pl.BlockSpec((tm, k), lambda i: (i, 0)), pl.BlockSpec((1, k), lambda i: (0, 0))],
        out_specs=pl.BlockSpec((tm, k), lambda i: (i, 0)),
        out_shape=jax.ShapeDtypeStruct((m, k), BF16),
        compiler_params=_cparams(1),
        name="rmsnorm",
    )(x, g.reshape(1, k))


def _proj_kernel(h_ref, w_ref, o_ref, wb_ref, *, transposed):
    @pl.when(pl.program_id(1) == 0)
    def _():
        if transposed:
            _store_transposed(wb_ref, w_ref, wb_ref.shape[1])
        else:
            wb_ref[...] = w_ref[...].astype(BF16)

    o_ref[...] = _dot(h_ref[...], wb_ref[...])


def _proj(h, w, tm, tn, n=None, transposed=False):
    m, k = h.shape
    n = w.shape[0 if transposed else 1] if n is None else n
    wspec = pl.BlockSpec((tn, k), lambda j, i: (j, 0)) if transposed else pl.BlockSpec((k, tn), lambda j, i: (0, j))
    return pl.pallas_call(
        functools.partial(_proj_kernel, transposed=transposed),
        grid=(n // tn, m // tm),
        in_specs=[pl.BlockSpec((tm, k), lambda j, i: (i, 0)), wspec],
        out_specs=pl.BlockSpec((tm, tn), lambda j, i: (i, j)),
        out_shape=jax.ShapeDtypeStruct((m, n), F32),
        scratch_shapes=[pltpu.VMEM((k, tn), BF16)],
        compiler_params=_cparams(2),
        name="proj",
    )(h, w)


def _proj_kv_kernel(h_ref, w_ref, o_ref, oc_ref, os_ref, ow_ref, wb_ref, *, tiles_per_batch):
    j, i = pl.program_id(0), pl.program_id(1)
    n_tiles = pl.num_programs(0)

    @pl.when(i == 0)
    def _():
        _store_transposed(wb_ref, w_ref, wb_ref.shape[1])

    res = _dot(h_ref[...], wb_ref[...])
    o_ref[...] = res
    tm, tn = res.shape
    rows_kv = tn // NSA_DK

    @pl.when(j == n_tiles - 3)
    def _():
        oc_ref[...] = res.reshape(tm * rows_kv, NSA_DK)

    @pl.when(j == n_tiles - 2)
    def _():
        os_ref[...] = res.reshape(tm * rows_kv, NSA_DK)

    @pl.when((j == n_tiles - 1) & (i % tiles_per_batch == tiles_per_batch - 1))
    def _():
        ow_ref[...] = res[tm - WINDOW:, :].reshape(WINDOW * rows_kv, NSA_DK)


def _proj_kv(h, w_t, tm, seq):
    m, k = h.shape
    tn = 2 * KV_W
    assert PROJ_A % tn == 0 and COL_KVC == PROJ_A - 3 * tn and seq % tm == 0 and tm >= WINDOW
    n_j, n_i = PROJ_A // tn, m // tm
    tiles_per_batch = seq // tm
    rows_kv = tn // NSA_DK

    def kv_rows(tile):
        return lambda j, i: (jnp.where(j < tile, 0, jnp.where(j == tile, i, n_i - 1)), 0)

    return pl.pallas_call(
        functools.partial(_proj_kv_kernel, tiles_per_batch=tiles_per_batch),
        grid=(n_j, n_i),
        in_specs=[pl.BlockSpec((tm, k), lambda j, i: (i, 0)),
                  pl.BlockSpec((tn, k), lambda j, i: (j, 0))],
        out_specs=[pl.BlockSpec((tm, tn), lambda j, i: (i, j)),
                   pl.BlockSpec((tm * rows_kv, NSA_DK), kv_rows(n_j - 3)),
                   pl.BlockSpec((tm * rows_kv, NSA_DK), kv_rows(n_j - 2)),
                   pl.BlockSpec((WINDOW * rows_kv, NSA_DK),
                                lambda j, i: (jnp.where(j < n_j - 1, 0, i // tiles_per_batch), 0))],
        out_shape=[jax.ShapeDtypeStruct((m, PROJ_A), F32),
                   jax.ShapeDtypeStruct((m * rows_kv, NSA_DK), F32),
                   jax.ShapeDtypeStruct((m * rows_kv, NSA_DK), F32),
                   jax.ShapeDtypeStruct((m // seq * WINDOW * rows_kv, NSA_DK), F32)],
        scratch_shapes=[pltpu.VMEM((k, tn), BF16)],
        compiler_params=_cparams(2),
        name="proj_kv",
    )(h, w_t)


TAIL_TN = 1024


def _store_transposed(dst_ref, src_ref, rows, chunk=256):
    for r0 in range(0, rows, chunk):
        dst_ref[:, r0:r0 + chunk] = src_ref[r0:r0 + chunk, :].T.astype(BF16)


def _proj_tail_kernel(h_ref, w_ref, wng_ref, o_ref, wb_ref, *, last_rows):
    tn = wb_ref.shape[1]
    is_last = pl.program_id(0) == pl.num_programs(0) - 1

    @pl.when((pl.program_id(1) == 0) & jnp.logical_not(is_last))
    def _():
        _store_transposed(wb_ref, w_ref, tn)

    @pl.when((pl.program_id(1) == 0) & is_last)
    def _():
        _store_transposed(wb_ref, w_ref, last_rows)
        wb_ref[:, last_rows:] = wng_ref[...].T.astype(BF16)

    o_ref[...] = _dot(h_ref[...], wb_ref[...])


def _proj_tail(h, w_t, w_ng, tm):
    m, k = h.shape
    tn = TAIL_TN
    row_ng = PROJ_A
    row_nsl = row_ng + N_BRANCHES * NSA_HEADS
    row_mq = row_nsl + NSA_W
    row_mg = row_mq + MEM_W
    assert row_mg + N_BRANCHES * D_MODEL == w_t.shape[0] and w_ng.shape == (NG_SLOT, k)
    assert COL_NSL == 0 and COL_MG % tn == 0 and COL_MQ % tn == 0 and COL_NG + NG_SLOT == PROJ_B == COL_MQ + 2 * tn
    assert row_mq % 16 == 0 and row_nsl % 16 == 0 and row_mg % 16 == 0
    t_mg, t_mq = COL_MG // tn, COL_MQ // tn

    def w_row(j, i):
        row = jnp.where(j < t_mg, row_nsl + j * tn,
                        jnp.where(j < t_mq, row_mg + (j - t_mg) * tn, row_mq + (j - t_mq) * tn))
        return (pl.multiple_of(row, 16), 0)

    return pl.pallas_call(
        functools.partial(_proj_tail_kernel, last_rows=COL_NG - COL_MQ - tn),
        grid=(PROJ_B // tn, m // tm),
        in_specs=[pl.BlockSpec((tm, k), lambda j, i: (i, 0)),
                  pl.BlockSpec((pl.Element(tn), pl.Element(k)), w_row),
                  pl.BlockSpec((NG_SLOT, k), lambda j, i: (0, 0))],
        out_specs=pl.BlockSpec((tm, tn), lambda j, i: (i, j)),
        out_shape=jax.ShapeDtypeStruct((m, PROJ_B), F32),
        scratch_shapes=[pltpu.VMEM((k, tn), BF16)],
        compiler_params=_cparams(2),
        name="proj_tail",
    )(h, w_t, w_ng)


def _rope_rows(x, cos, sin):
    half = x.shape[-1] // 2
    x1, x2 = x[:, :half], x[:, half:]
    return jnp.concatenate([x1 * cos - x2 * sin, x1 * sin + x2 * cos], axis=-1)


def _head_norm_gate(o, gnorm, rg):
    oc = o - jnp.mean(o, axis=-1, keepdims=True)
    y = oc * lax.rsqrt(jnp.mean(oc * oc, axis=-1, keepdims=True) + EPS) * gnorm
    return y * (rg * _sigmoid(rg))


def _ret_prompt_kernel(q_ref, k_ref, v_ref, rg_ref, cos_ref, sin_ref, dmat_ref, xi_ref, zeta_ref, gc_ref,
                       gn_ref, a_ref, s_ref):
    @pl.when(pl.program_id(2) == 0)
    def _():
        s_ref[...] = jnp.zeros_like(s_ref)

    c = RET_CHUNK
    for t in range(q_ref.shape[0] // c):
        rows = slice(t * c, (t + 1) * c)
        cos, sin = cos_ref[rows, :], sin_ref[rows, :]
        q = _rope_rows(q_ref[rows, :], cos, sin)
        k = _rope_rows(k_ref[rows, :], cos, sin) * (RET_DK ** -0.5)
        qb, vb = q.astype(BF16), v_ref[rows, :].astype(BF16)
        state = s_ref[0, 0]
        inner = _nt(qb, k.astype(BF16)) * dmat_ref[0]
        o = _dot(inner.astype(BF16), vb) + _dot(qb, state.astype(BF16)) * xi_ref[0]
        kz_t = (k * zeta_ref[0]).T.astype(BF16)
        s_ref[0, 0] = state * gc_ref[0] + _dot(kz_t, vb)
        a_ref[rows, :] = _head_norm_gate(o, gn_ref[...], rg_ref[rows, :]).astype(BF16)


def _decay_tables(chunk):
    log_g = jnp.log1p(-jnp.exp2(-5.0 - jnp.arange(RET_HEADS, dtype=F32)))
    i = jnp.arange(chunk, dtype=F32)
    diff = i[:, None] - i[None, :]
    dmat = jnp.where(diff >= 0, jnp.exp(log_g[:, None, None] * jnp.maximum(diff, 0.0)), 0.0)
    xi = jnp.exp(log_g[:, None] * (i[None, :] + 1.0))[:, :, None]
    zeta = jnp.exp(log_g[:, None] * (chunk - 1.0 - i[None, :]))[:, :, None]
    g_chunk = jnp.exp(log_g * chunk)[:, None, None]
    return dmat, xi, zeta, g_chunk


def _rope_tables(pos):
    half = RET_DK // 2
    freq = jnp.power(ROPE_BASE, -jnp.arange(half, dtype=F32) / half)
    ang = pos.astype(F32)[:, None] * freq[None, :]
    return jnp.cos(ang), jnp.sin(ang)


def _retention_prompt(z, ret_norm, batch, seq):
    c = RET_CHUNK
    rows = _pick_tile(seq, RET_STEP_CHUNKS * c)
    nc = seq // rows
    dmat, xi, zeta, g_chunk = _decay_tables(c)
    cos, sin = _rope_tables(jnp.arange(seq))
    hb = RET_DK

    def zspec(col0):
        return pl.BlockSpec((rows, hb), lambda b, h, t, col0=col0: (b * nc + t, col0 // hb + h))

    per_head = lambda shape: pl.BlockSpec((1,) + shape, lambda b, h, t: (h, 0, 0))
    return pl.pallas_call(
        _ret_prompt_kernel,
        grid=(batch, RET_HEADS, nc),
        in_specs=[zspec(COL_RQ), zspec(COL_RK), zspec(COL_RV), zspec(COL_RG),
                  pl.BlockSpec((rows, hb // 2), lambda b, h, t: (t, 0)),
                  pl.BlockSpec((rows, hb // 2), lambda b, h, t: (t, 0)),
                  per_head((c, c)), per_head((c, 1)), per_head((c, 1)), per_head((1, 1)),
                  pl.BlockSpec((1, hb), lambda b, h, t: (0, h))],
        out_specs=[pl.BlockSpec((rows, hb), lambda b, h, t: (b * nc + t, h)),
                   pl.BlockSpec((1, 1, RET_DK, RET_DV), lambda b, h, t: (b, h, 0, 0))],
        out_shape=[jax.ShapeDtypeStruct((batch * seq, RET_W), BF16),
                   jax.ShapeDtypeStruct((batch, RET_HEADS, RET_DK, RET_DV), F32)],
        compiler_params=_cparams(3),
        name="retention_prompt",
    )(z, z, z, z, cos, sin, dmat, xi, zeta, g_chunk, ret_norm.reshape(1, RET_W))


def _column_of(row):
    n = row.shape[1]
    eye = _iota((n, n), 0) == _iota((n, n), 1)
    return jnp.sum(jnp.where(eye, jnp.broadcast_to(row, (n, n)), 0.0), axis=1, keepdims=True)


def _ret_sample_kernel(q_ref, k_ref, v_ref, rg_ref, cos_ref, sin_ref, gam_ref, gn_ref, s_ref, a_ref, so_ref):
    cos, sin = cos_ref[...], sin_ref[...]
    outs = []
    for h in range(RET_HEADS):
        cols = slice(h * RET_DK, (h + 1) * RET_DK)
        q = _rope_rows(q_ref[0][:, cols], cos, sin)
        k = _rope_rows(k_ref[0][:, cols], cos, sin) * (RET_DK ** -0.5)
        v = v_ref[0][:, cols]
        state = s_ref[0, h]
        gamma = gam_ref[h]
        qk = jnp.sum(q * k, axis=-1, keepdims=True)
        o = qk * v + jnp.sum(_column_of(q) * state, axis=0, keepdims=True) * gamma
        so_ref[0, h] = state * gamma + _column_of(k) * v
        outs.append(_head_norm_gate(o, gn_ref[:, cols], rg_ref[0][:, cols]))
    a_ref[0] = jnp.concatenate(outs, axis=1).astype(BF16)


def _retention_sample(z3, state, ret_norm, pos):
    nb = z3.shape[0]
    cos, sin = _rope_tables(jnp.full((1,), pos))
    gamma = jnp.exp(jnp.log1p(-jnp.exp2(-5.0 - jnp.arange(RET_HEADS, dtype=F32))))[:, None, None]
    hb = RET_DK
    assert RET_HEADS * RET_DK == RET_W

    def zspec(col0):
        return pl.BlockSpec((1, 1, RET_W), lambda b, col0=col0: (b, 0, col0 // RET_W))

    st_spec = pl.BlockSpec((1, RET_HEADS, RET_DK, RET_DV), lambda b: (b, 0, 0, 0))
    return pl.pallas_call(
        _ret_sample_kernel,
        grid=(nb,),
        in_specs=[zspec(COL_RQ), zspec(COL_RK), zspec(COL_RV), zspec(COL_RG),
                  pl.BlockSpec((1, hb // 2), lambda b: (0, 0)),
                  pl.BlockSpec((1, hb // 2), lambda b: (0, 0)),
                  pl.BlockSpec((RET_HEADS, 1, 1), lambda b: (0, 0, 0)),
                  pl.BlockSpec((1, RET_W), lambda b: (0, 0)),
                  st_spec],
        out_specs=[pl.BlockSpec((1, 1, RET_W), lambda b: (b, 0, 0)), st_spec],
        out_shape=[jax.ShapeDtypeStruct((nb, 1, RET_W), BF16),
                   jax.ShapeDtypeStruct(state.shape, F32)],
        compiler_params=_cparams(1),
        name="retention_sample",
    )(z3, z3, z3, z3, cos, sin, gamma, ret_norm.reshape(1, RET_W), state)


def _half_rows(ref_slice_fn, n_half):
    return jnp.concatenate([ref_slice_fn(p) for p in range(CMP_STRIDE)], axis=1)


def _cmp_stage1_dense_kernel(x_ref, w_ref, o_ref):
    nh = o_ref.shape[3]
    x = _half_rows(lambda p: x_ref[pl.ds(p, nh, stride=CMP_STRIDE), :], nh).astype(BF16)
    o_ref[0, 0, 0] = _dot(x, w_ref[0])


def _cmp_stage1_dense(z, w1ab, batch, seq):
    nh = seq // CMP_STRIDE
    return pl.pallas_call(
        _cmp_stage1_dense_kernel,
        grid=(batch, 2, NSA_KV_HEADS),
        in_specs=[pl.BlockSpec((seq, NSA_DK), lambda b, kv, h: (b, COL_KVC // NSA_DK + kv * NSA_KV_HEADS + h)),
                  pl.BlockSpec((1, CMP_STRIDE * NSA_DK, 2 * NSA_DK), lambda b, kv, h: (kv, 0, 0))],
        out_specs=pl.BlockSpec((1, 1, 1, nh, 2 * NSA_DK), lambda b, kv, h: (b, kv, h, 0, 0)),
        out_shape=jax.ShapeDtypeStruct((batch, 2, NSA_KV_HEADS, nh, 2 * NSA_DK), F32),
        compiler_params=_cparams(3),
        name="cmp_stage1_dense",
    )(z, w1ab)


PAGES_PER_STEP = 16


def _cmp_stage1_paged_kernel(pt_ref, *refs):
    pages, (w_ref, o_ref) = refs[:PAGES_PER_STEP], refs[PAGES_PER_STEP:]
    hp = PAGE_SIZE // CMP_STRIDE
    top = _iota((2 * NSA_KV_HEADS, NSA_DK), 0) < NSA_KV_HEADS
    cols = [[], []]
    for p in range(CMP_STRIDE):
        tiles = [[], []]
        for pg in pages:
            xp = pg[0, :, p]
            for n in range(0, hp, 2):
                a, b = xp[n], xp[n + 1]
                tiles[0].append(jnp.where(top, a, pltpu.roll(b, NSA_KV_HEADS, 0)))
                tiles[1].append(jnp.where(top, pltpu.roll(a, NSA_KV_HEADS, 0), b))
        for kv in range(2):
            cols[kv].append(jnp.concatenate(tiles[kv], axis=0))
    for kv in range(2):
        x = jnp.concatenate(cols[kv], axis=1).astype(BF16)
        o_ref[0, kv, 0] = _dot(x, w_ref[kv])


def _cmp_stage1_paged(cache, page_table, w1ab):
    nb, n_pages = page_table.shape
    hp = PAGE_SIZE // CMP_STRIDE
    steps = n_pages // PAGES_PER_STEP
    rows = PAGES_PER_STEP * hp * NSA_KV_HEADS

    def page_spec(j):
        return pl.BlockSpec((1, hp, CMP_STRIDE, 2 * NSA_KV_HEADS, NSA_DK),
                            lambda b, s, pt, j=j: (pt[b, s * PAGES_PER_STEP + j], 0, 0, 0, 0))

    wspec = pl.BlockSpec((2, CMP_STRIDE * NSA_DK, 2 * NSA_DK), lambda b, s, pt: (0, 0, 0))
    return pl.pallas_call(
        _cmp_stage1_paged_kernel,
        grid_spec=pltpu.PrefetchScalarGridSpec(
            num_scalar_prefetch=1,
            grid=(nb, steps),
            in_specs=[page_spec(j) for j in range(PAGES_PER_STEP)] + [wspec],
            out_specs=pl.BlockSpec((1, 2, 1, rows, 2 * NSA_DK), lambda b, s, pt: (b, 0, 0, s, 0))),
        out_shape=jax.ShapeDtypeStruct((nb, 2, 1, steps * rows, 2 * NSA_DK), F32),
        compiler_params=_cparams(2),
        name="cmp_stage1_paged",
    )(page_table, *([cache] * PAGES_PER_STEP), w1ab)


def _cmp_stage2_kernel(ac_ref, pos_ref, w1a_ref, w1b_ref, w2_ref, o_ref, *, shift):
    ac = ac_ref[0, 0, 0]
    nh = ac.shape[0]
    pos = pos_ref[0].astype(BF16)
    kw = CMP_STRIDE * NSA_DK
    pe = _dot(pos[:, :kw], w1a_ref[0]) + _dot(pos[:, kw:], w1b_ref[0])
    pre = ac[:, :NSA_DK] + pltpu.roll(ac[:, NSA_DK:], nh - shift, 0) + pe[0:1]
    gelu = 0.5 * pre * (1.0 + jnp.tanh(math.sqrt(2.0 / math.pi) * (pre + 0.044715 * (pre * pre * pre))))
    o_ref[0, 0, 0] = _dot(gelu.astype(BF16), w2_ref[0]).astype(BF16)


def _cmp_stage2(ac, pos8, w1a, w1b, w2, shift):
    nb, _, groups, nh, _ = ac.shape
    kw = CMP_STRIDE * NSA_DK
    return pl.pallas_call(
        functools.partial(_cmp_stage2_kernel, shift=shift),
        grid=(nb, 2, groups),
        in_specs=[pl.BlockSpec((1, 1, 1, nh, 2 * NSA_DK), lambda b, kv, h: (b, kv, h, 0, 0)),
                  pl.BlockSpec((1, 8, 2 * kw), lambda b, kv, h: (kv, 0, 0)),
                  pl.BlockSpec((1, kw, NSA_DK), lambda b, kv, h: (kv, 0, 0)),
                  pl.BlockSpec((1, kw, NSA_DK), lambda b, kv, h: (kv, 0, 0)),
                  pl.BlockSpec((1, NSA_DK, NSA_DK), lambda b, kv, h: (kv, 0, 0))],
        out_specs=pl.BlockSpec((1, 1, 1, nh, NSA_DK), lambda b, kv, h: (b, kv, h, 0, 0)),
        out_shape=jax.ShapeDtypeStruct((nb, 2, groups, nh, NSA_DK), BF16),
        compiler_params=_cparams(3),
        name="cmp_stage2",
    )(ac, pos8, w1a, w1b, w2)


def _rel_bucket(dist):
    n = jnp.maximum(dist, 0)
    nf = jnp.maximum(n, 1).astype(F32)
    scale = (REL_BUCKETS - REL_MAX_EXACT) / math.log(REL_MAX_DIST / REL_MAX_EXACT)
    large = REL_MAX_EXACT + (jnp.log(nf / REL_MAX_EXACT) * scale).astype(jnp.int32)
    large = jnp.minimum(large, REL_BUCKETS - 1)
    return jnp.where(n < REL_MAX_EXACT, n, large)


def _bias_by_dist(rel_table, n):
    tab = rel_table[_rel_bucket(jnp.arange(n))]
    return tab.T.reshape(NSA_KV_HEADS, NSA_GROUP, n)


def _pad_group_rows(t, axis):
    first = lax.slice_in_dim(t, 0, 1, axis=axis)
    return jnp.concatenate([t] + [first] * (SROWS - NSA_GROUP), axis=axis)


def _flash_init(m_ref, l_ref, acc_ref):
    m_ref[...] = jnp.full(m_ref.shape, NEG_INF, F32)
    l_ref[...] = jnp.zeros(l_ref.shape, F32)
    acc_ref[...] = jnp.zeros(acc_ref.shape, F32)


def _flash_step(s, v, m_ref, l_ref, acc_ref):
    m_old = m_ref[...]
    m_new = jnp.maximum(m_old, jnp.max(s, axis=1, keepdims=True))
    alpha = jnp.exp(m_old - m_new)
    p = jnp.exp(s - m_new)
    l_ref[...] = alpha * l_ref[...] + jnp.sum(p, axis=1, keepdims=True)
    acc_ref[...] = alpha * acc_ref[...] + _dot(p.astype(BF16), v)
    m_ref[...] = m_new


def _flash_result(l_ref, acc_ref):
    return acc_ref[...] / jnp.maximum(l_ref[...], 1e-30)


def _select_blocks(imp_t, q0, ns):
    shape = imp_t.shape
    blk = _iota(shape, 0)
    qpos = q0 + _iota(shape, 1)
    cur = qpos >> 6
    valid = blk * SEL_BLOCK <= qpos
    forced = (blk == 0) | (blk == cur) | (blk == cur - 1)
    imp_t = jnp.where(valid, jnp.where(forced, FORCE_SCORE, imp_t), NEG_INF)
    rank = jnp.zeros(shape, F32)
    for other in range(ns):
        row = imp_t[other:other + 1, :]
        ahead = (row > imp_t) | ((row == imp_t) & (blk > other))
        rank = rank + jnp.where(ahead, 1.0, 0.0)
    return jnp.where((rank < SEL_TOPK) & valid, 1.0, 0.0)


def _nsa_prompt_kernel(q_ref, ks_ref, vs_ref, kw_ref, vw_ref, kc_ref, vct_ref, tz_ref, cfar_ref, basec_ref,
                       ovt_ref, ng_ref, nsl_ref, o_ref, m_ref, l_ref, acc_ref, vst_ref, vwt_ref, *, ns):
    i = pl.program_id(2)
    q0 = i * TQ
    qall = q_ref[...] * ((NSA_DK ** -0.5) * LOG2E)
    qt = jnp.concatenate([qall[:, g * NSA_DK:(g + 1) * NSA_DK].T for g in range(NSA_GROUP)], axis=1).astype(BF16)
    c_loc = _iota((TK, NSA_ROWS), 0)
    r_loc = _iota((TK, NSA_ROWS), 1) & (TQ - 1)

    @pl.when(i == 0)
    def _():
        for kt in range(vst_ref.shape[0]):
            vst_ref[kt] = vs_ref[kt * TK:(kt + 1) * TK, :].T.astype(BF16)
            vwt_ref[kt] = vw_ref[kt * TK:(kt + 1) * TK, :].T.astype(BF16)

    ncp = kc_ref.shape[3]
    shift = (TQ // CMP_STRIDE) * i
    bias_c = basec_ref[0, pl.ds(pl.multiple_of(ncp - shift, TQ // CMP_STRIDE), ncp), :]
    s = _dot(kc_ref[0, 0, 0], qt) + bias_c
    m = jnp.max(s, axis=0, keepdims=True)
    e = jnp.exp2(s - m)
    inv = jnp.where(m > 0.5 * NEG_INF, 1.0 / jnp.maximum(jnp.sum(e, axis=0, keepdims=True), 1e-30), 0.0)
    p = e * inv
    o_cmp = _dot(vct_ref[0, 0], p.astype(BF16))

    psum = p[:, 0:TQ] + p[:, TQ:2 * TQ] + p[:, 2 * TQ:3 * TQ] + p[:, 3 * TQ:4 * TQ]
    hi = psum.astype(BF16)
    lo = (psum - hi.astype(F32)).astype(BF16)
    ovt = ovt_ref[...]
    imp_t = _dot(ovt, hi) + _dot(ovt, lo)
    ns8 = -(-ns // 8) * 8
    sel_t = _select_blocks(imp_t[:ns8], q0, ns)
    nsp = ovt.shape[0]
    sel_neg = jnp.where(sel_t > 0.5, 0.0, NEG_INF)
    if ns8 < nsp:
        sel_neg = jnp.concatenate([sel_neg, jnp.full((nsp - ns8, TQ), NEG_INF, F32)], axis=0)
    sel_neg = sel_neg.astype(BF16)

    def flash_step(scs, vts):
        m_old = m_ref[...]
        m_new = m_old
        for sc in scs:
            m_new = jnp.maximum(m_new, jnp.max(sc, axis=0, keepdims=True))
        alpha = jnp.exp2(m_old - m_new)
        l_new = alpha * l_ref[...]
        acc = alpha * acc_ref[...]
        for sc, vt in zip(scs, vts):
            pt = jnp.exp2(sc - m_new)
            l_new = l_new + jnp.sum(pt, axis=0, keepdims=True)
            acc = acc + _dot(vt, pt.astype(BF16))
        l_ref[...] = l_new
        acc_ref[...] = acc
        m_ref[...] = m_new

    def sel_scores(kt, bias, causal):
        k = ks_ref[pl.ds(pl.multiple_of(kt * TK, TK), TK), :].astype(BF16)
        blk_of_key = (TK // SEL_BLOCK) * kt + (_iota((TK, nsp), 0) >> 6)
        expand = jnp.where(_iota((TK, nsp), 1) == blk_of_key, 1.0, 0.0).astype(BF16)
        mk = _dot(expand, sel_neg)
        sc = _dot(k, qt) + bias + jnp.concatenate([mk] * NSA_GROUP, axis=1)
        if causal:
            sc = jnp.where(c_loc <= r_loc, sc, NEG_INF)
        return sc

    _flash_init(m_ref, l_ref, acc_ref)

    @pl.when(i == 0)
    def _():
        flash_step([sel_scores(i, tz_ref[0, 0], True)], [vst_ref[i]])

    @pl.when(i >= 1)
    def _():
        flash_step([sel_scores(i, tz_ref[0, 0], True), sel_scores(i - 1, tz_ref[0, 1], False)],
                   [vst_ref[i], vst_ref[i - 1]])

    def far_tiles(kt_first, count):
        kts = [kt_first - t for t in range(count)]
        flash_step([sel_scores(kt, cfar_ref[0], False) for kt in kts], [vst_ref[kt] for kt in kts])

    def far_group(j, carry):
        far_tiles(i - 2 - FAR_GROUP * j, FAR_GROUP)
        return carry

    n_far = jnp.maximum(i - 1, 0)
    lax.fori_loop(0, n_far // FAR_GROUP, far_group, 0)
    for rest in range(1, FAR_GROUP):
        pl.when(n_far % FAR_GROUP == rest)(functools.partial(far_tiles, rest - 1, rest))

    o_sel = _flash_result(l_ref, acc_ref)

    def win_scores(off):
        kt = i - off
        k = kw_ref[pl.ds(pl.multiple_of(kt * TK, TK), TK), :].astype(BF16)
        sc = _dot(k, qt) + (tz_ref[0, off] if off < 2 else cfar_ref[0])
        if off == 0:
            sc = jnp.where(c_loc <= r_loc, sc, NEG_INF)
        if off * TK == WINDOW:
            sc = jnp.where(c_loc > r_loc, sc, NEG_INF)
        return sc

    def win_tiles(n_tiles):
        flash_step([win_scores(off) for off in range(n_tiles)], [vwt_ref[i - off] for off in range(n_tiles)])

    _flash_init(m_ref, l_ref, acc_ref)
    n_win = WINDOW // TK + 1
    for n_tiles in range(1, n_win + 1):
        cond = (i == n_tiles - 1) if n_tiles < n_win else (i >= n_win - 1)
        pl.when(cond)(functools.partial(win_tiles, n_tiles))
    o_win = _flash_result(l_ref, acc_ref)

    gates_t = _sigmoid(ng_ref[...]).T
    nsl = nsl_ref[...]
    outs = []
    for g in range(NSA_GROUP):
        cols = slice(g * TQ, (g + 1) * TQ)
        o_t = (gates_t[3 * g:3 * g + 1] * o_cmp[:, cols] + gates_t[3 * g + 1:3 * g + 2] * o_sel[:, cols]
               + gates_t[3 * g + 2:3 * g + 3] * o_win[:, cols])
        x = nsl[:, g * NSA_DK:(g + 1) * NSA_DK]
        outs.append(o_t.T * (x * _sigmoid(x)))
    o_ref[...] = jnp.concatenate(outs, axis=1).astype(BF16)


def _lanes_by_head(t):
    hk, g, rows, tq = t.shape
    return t.transpose(0, 2, 1, 3).reshape(hk, rows, g * tq)


def _nsa_prompt(z, zb, kcvc, bias_d, batch, seq):
    nq = seq // TQ
    ns = seq // SEL_BLOCK
    ncp = kcvc.shape[3]
    nsp = LANE
    wn = TQ // CMP_STRIDE
    assert TQ == TK and ns <= nsp and ncp >= seq // CMP_STRIDE and ncp % LANE == 0 and ncp > wn
    gw = NSA_GROUP * NSA_DK
    n_dist = bias_d.shape[-1]
    assert n_dist >= 2 * TK + TQ
    bias_d = bias_d * LOG2E

    rep = jnp.tile(bias_d, (1, 1, TK + 1))[..., :TK * (n_dist - 1)].reshape(NSA_KV_HEADS, NSA_GROUP, TK, n_dist - 1)
    tz = jnp.stack([_lanes_by_head(rep[..., 0:TQ]), _lanes_by_head(rep[..., TK:TK + TQ])], axis=1)
    far = jnp.broadcast_to(bias_d[..., REL_MAX_DIST][:, :, None, None], (NSA_KV_HEADS, NSA_GROUP, 1, TQ))
    cfar = _lanes_by_head(far)
    half = n_dist // 2
    start = CMP_STRIDE * wn - (CMP_BLOCK - 1)
    assert start + TQ <= half and 2 * wn * CMP_STRIDE - start <= half
    w_ext = jnp.concatenate([bias_d[..., :half], jnp.full(bias_d.shape[:-1] + (n_dist - half,), NEG_INF, F32)], -1)
    near = jnp.tile(w_ext, (1, 1, 2 * wn + 1))[..., :2 * wn * (n_dist - CMP_STRIDE)]
    near = near.reshape(NSA_KV_HEADS, NSA_GROUP, 2 * wn, n_dist - CMP_STRIDE)[..., start:start + TQ]
    basec = jnp.concatenate([jnp.broadcast_to(far, (NSA_KV_HEADS, NSA_GROUP, ncp - wn, TQ)), near,
                             jnp.full((NSA_KV_HEADS, NSA_GROUP, ncp - wn, TQ), NEG_INF, F32)], axis=2)
    basec = _lanes_by_head(basec)
    sblk = jnp.arange(nsp)[:, None]
    nblk = jnp.arange(ncp)[None, :]
    ovt = ((nblk >= 4 * sblk - 1) & (nblk <= 4 * sblk + 3)).astype(BF16)

    vct = kcvc[:, 1].transpose(0, 1, 3, 2)

    def kvspec(col0, which):
        return pl.BlockSpec((seq, NSA_DK), lambda b, h, i: (b, col0 // NSA_DK + which * NSA_KV_HEADS + h))

    vt_scratch = pltpu.VMEM((seq // TK, NSA_DK, TK), BF16)
    return pl.pallas_call(
        functools.partial(_nsa_prompt_kernel, ns=ns),
        grid=(batch, NSA_KV_HEADS, nq),
        in_specs=[pl.BlockSpec((TQ, gw), lambda b, h, i: (b * nq + i, COL_NQ // gw + h)),
                  kvspec(COL_KVS, 0), kvspec(COL_KVS, 1), kvspec(COL_KVW, 0), kvspec(COL_KVW, 1),
                  pl.BlockSpec((1, 1, 1, ncp, NSA_DK), lambda b, h, i: (b, 0, h, 0, 0)),
                  pl.BlockSpec((1, 1, NSA_DK, ncp), lambda b, h, i: (b, h, 0, 0)),
                  pl.BlockSpec((1, 2, TK, NSA_ROWS), lambda b, h, i: (h, 0, 0, 0)),
                  pl.BlockSpec((1, 1, NSA_ROWS), lambda b, h, i: (h, 0, 0)),
                  pl.BlockSpec((1, 2 * ncp, NSA_ROWS), lambda b, h, i: (h, 0, 0)),
                  pl.BlockSpec((nsp, ncp), lambda b, h, i: (0, 0)),
                  pl.BlockSpec((TQ, LANE), lambda b, h, i: (b * nq + i, COL_NG // LANE + h)),
                  pl.BlockSpec((TQ, gw), lambda b, h, i: (b * nq + i, COL_NSL // gw + h))],
        out_specs=pl.BlockSpec((TQ, gw), lambda b, h, i: (b * nq + i, h)),
        out_shape=jax.ShapeDtypeStruct((batch * seq, NSA_W), BF16),
        scratch_shapes=[pltpu.VMEM((1, NSA_ROWS), F32), pltpu.VMEM((1, NSA_ROWS), F32),
                        pltpu.VMEM((NSA_DK, NSA_ROWS), F32), vt_scratch, vt_scratch],
        compiler_params=_cparams(3),
        name="nsa_prompt",
    )(z, z, z, z, z, kcvc, vct, tz, cfar, basec, ovt, zb, zb)


SROWS = 8
SEL_PER_STEP = 2


def _stack_group_q(q_row):
    heads = [q_row[:, g * NSA_DK:(g + 1) * NSA_DK] for g in range(NSA_GROUP)]
    return jnp.concatenate(heads + [heads[0]] * (SROWS - NSA_GROUP), axis=0)


def _nsa_sample_cmp_kernel(q_ref, kc_ref, vc_ref, bias_ref, ov_ref, o_ref, idx_ref, *, ns):
    scale = NSA_DK ** -0.5
    qs = _stack_group_q(q_ref[0]).astype(BF16)
    s = _nt(qs, kc_ref[0, 0]) * scale + bias_ref[0]
    m = jnp.max(s, axis=1, keepdims=True)
    e = jnp.exp(s - m)
    inv = jnp.where(m > 0.5 * NEG_INF, 1.0 / jnp.maximum(jnp.sum(e, axis=1, keepdims=True), 1e-30), 0.0)
    p = e * inv
    o_ref[0, 0] = _dot(p.astype(BF16), vc_ref[0, 0])
    psum = jnp.broadcast_to(jnp.sum(p[0:NSA_GROUP], axis=0, keepdims=True), p.shape)
    hi = psum.astype(BF16)
    lo = (psum - hi.astype(F32)).astype(BF16)
    imp = (_dot(hi, ov_ref[...]) + _dot(lo, ov_ref[...]))[0:1]
    nsp = imp.shape[1]
    blk_r = _iota((1, nsp), 1)
    cur = ns - 1
    forced = (blk_r == 0) | (blk_r == cur) | (blk_r == cur - 1)
    imp = jnp.where(blk_r < ns, jnp.where(forced, FORCE_SCORE, imp), 2.0 * NEG_INF)
    imp_c = _column_of(imp)
    i_r = _iota((nsp, nsp), 1)
    j_c = _iota((nsp, nsp), 0)
    ahead = (imp > imp_c) | ((imp == imp_c) & (i_r < j_c))
    rank_c = jnp.sum(jnp.where(ahead, 1.0, 0.0), axis=1, keepdims=True)
    slot = _iota((nsp, LANE), 1).astype(F32)
    picks = jnp.where(rank_c == slot, _iota((nsp, LANE), 0).astype(F32), 0.0)
    idx_ref[0, 0] = jnp.broadcast_to(jnp.sum(picks, axis=0, keepdims=True), (SROWS, LANE)).astype(jnp.int32)


def _nsa_sample_cmp(z3, kcvc, bias_c, ov, ns):
    nb = z3.shape[0]
    ncp = kcvc.shape[2]
    nsp = ov.shape[1]
    gw = NSA_GROUP * NSA_DK

    def cspec(which):
        return pl.BlockSpec((1, 1, ncp, NSA_DK), lambda b, h: (b, which, 0, h))

    return pl.pallas_call(
        functools.partial(_nsa_sample_cmp_kernel, ns=ns),
        grid=(nb, NSA_KV_HEADS),
        in_specs=[pl.BlockSpec((1, 1, gw), lambda b, h: (b, 0, COL_NQ // gw + h)),
                  cspec(0), cspec(1),
                  pl.BlockSpec((1, SROWS, ncp), lambda b, h: (h, 0, 0)),
                  pl.BlockSpec((ncp, nsp), lambda b, h: (0, 0))],
        out_specs=[pl.BlockSpec((1, 1, SROWS, NSA_DK), lambda b, h: (b, h, 0, 0)),
                   pl.BlockSpec((1, 1, SROWS, LANE), lambda b, h: (b, h, 0, 0))],
        out_shape=[jax.ShapeDtypeStruct((nb, NSA_KV_HEADS, SROWS, NSA_DK), F32),
                   jax.ShapeDtypeStruct((nb, NSA_KV_HEADS, SROWS, LANE), jnp.int32)],
        compiler_params=_cparams(2),
        name="nsa_sample_cmp",
    )(z3, kcvc, kcvc, bias_c, ov)


def _nsa_sample_sel_kernel(pt_ref, idx_ref, q_ref, *refs, ns):
    n_blk = NSA_KV_HEADS * SEL_PER_STEP
    blocks, (new_ref,), biases = refs[:n_blk], refs[n_blk:n_blk + 1], refs[n_blk + 1:2 * n_blk + 1]
    o_ref, m_ref, l_ref, acc_ref = refs[2 * n_blk + 1:]
    b, t = pl.program_id(0), pl.program_id(1)
    gw = NSA_GROUP * NSA_DK
    rows_kv = 2 * NSA_KV_HEADS
    width = SEL_BLOCK * rows_kv

    @pl.when(t == 0)
    def _():
        _flash_init(m_ref, l_ref, acc_ref)

    new_rows = jnp.concatenate([new_ref[0]] * SEL_BLOCK, axis=0)
    xs, scs = [], []
    for h in range(NSA_KV_HEADS):
        qs = _stack_group_q(q_ref[0][:, h * gw:(h + 1) * gw]).astype(BF16)
        row = []
        for u in range(SEL_PER_STEP):
            r = h * SEL_PER_STEP + u
            is_new = idx_ref[b, h, t * SEL_PER_STEP + u] == ns - 1
            x = jnp.where(is_new, new_rows, blocks[r][0].reshape(width, NSA_DK)).astype(BF16)
            xs.append(x)
            row.append(_nt(qs, x) * (NSA_DK ** -0.5) + biases[r][0, 0])
        scs.append(jnp.concatenate(row, axis=1))
    sc = jnp.concatenate(scs, axis=0)
    m_old = m_ref[...]
    m_new = jnp.maximum(m_old, jnp.max(sc, axis=1, keepdims=True))
    alpha = jnp.exp(m_old - m_new)
    p = jnp.exp(sc - m_new)
    l_ref[...] = alpha * l_ref[...] + jnp.sum(p, axis=1, keepdims=True)
    pv = pltpu.roll(p, NSA_KV_HEADS, 1).astype(BF16)
    acc = alpha * acc_ref[...]
    upd = []
    for h in range(NSA_KV_HEADS):
        ph = pv[h * SROWS:(h + 1) * SROWS]
        upd.append(sum(_dot(ph[:, u * width:(u + 1) * width], xs[h * SEL_PER_STEP + u]) for u in range(SEL_PER_STEP)))
    acc_ref[...] = acc + jnp.concatenate(upd, axis=0)
    m_ref[...] = m_new

    @pl.when(t == pl.num_programs(1) - 1)
    def _():
        o_ref[0] = (acc_ref[...] / jnp.maximum(l_ref[...], 1e-30)).reshape(NSA_KV_HEADS, SROWS, NSA_DK)


def _nsa_sample_sel(z3, cache, kv_new, page_table, idx, bias_sel, ns):
    nb, n_pages = page_table.shape
    n_sel = idx.shape[2]
    halves = PAGE_SIZE // SEL_BLOCK
    rows_kv = 2 * NSA_KV_HEADS

    per_step = SEL_PER_STEP
    assert n_sel % per_step == 0

    def blockspec(h, u):
        def index(b, t, pt, ix):
            blk = ix[b, h, t * per_step + u]
            return (pt[b, jnp.minimum(blk // halves, n_pages - 1)], blk % halves, 0, 0)
        return pl.BlockSpec((1, SEL_BLOCK, rows_kv, NSA_DK), index)

    def biasspec(h, u):
        return pl.BlockSpec((1, 1, SROWS, SEL_BLOCK * rows_kv),
                            lambda b, t, pt, ix: (h, ix[b, h, t * per_step + u], 0, 0))

    slots = [(h, u) for h in range(NSA_KV_HEADS) for u in range(per_step)]
    return pl.pallas_call(
        functools.partial(_nsa_sample_sel_kernel, ns=ns),
        grid_spec=pltpu.PrefetchScalarGridSpec(
            num_scalar_prefetch=2,
            grid=(nb, n_sel // per_step),
            in_specs=[pl.BlockSpec((1, 1, NSA_W), lambda b, t, pt, ix: (b, 0, COL_NQ // NSA_W))]
                     + [blockspec(h, u) for h, u in slots]
                     + [pl.BlockSpec((1, rows_kv, NSA_DK), lambda b, t, pt, ix: (b, 0, 0))]
                     + [biasspec(h, u) for h, u in slots],
            out_specs=pl.BlockSpec((1, NSA_KV_HEADS, SROWS, NSA_DK), lambda b, t, pt, ix: (b, 0, 0, 0)),
            scratch_shapes=[pltpu.VMEM((NSA_KV_HEADS * SROWS, 1), F32), pltpu.VMEM((NSA_KV_HEADS * SROWS, 1), F32),
                            pltpu.VMEM((NSA_KV_HEADS * SROWS, NSA_DK), F32)]),
        out_shape=jax.ShapeDtypeStruct((nb, NSA_KV_HEADS, SROWS, NSA_DK), F32),
        compiler_params=_cparams(2),
        name="nsa_sample_sel",
    )(page_table, idx, z3, *([cache] * len(slots)), kv_new, *([bias_sel] * len(slots)))


def _nsa_sample_win_kernel(q_ref, k_ref, v_ref, kn_ref, vn_ref, bias_ref, bnew_ref, o_ref):
    scale = NSA_DK ** -0.5
    q = _stack_group_q(q_ref[0])
    s_buf = _nt(q.astype(BF16), k_ref[0].astype(BF16)) * scale + bias_ref[0]
    s_new = jnp.sum(q * kn_ref[0], axis=1, keepdims=True) * scale + bnew_ref[0]
    m = jnp.maximum(jnp.max(s_buf, axis=1, keepdims=True), s_new)
    p_buf = jnp.exp(s_buf - m)
    p_new = jnp.exp(s_new - m)
    l = jnp.sum(p_buf, axis=1, keepdims=True) + p_new
    acc = _dot(p_buf.astype(BF16), v_ref[0].astype(BF16)) + p_new * vn_ref[0]
    o_ref[0, 0] = acc / jnp.maximum(l, 1e-30)


def _nsa_sample_win(z3, win_buf, bias_win, bias_new):
    nb, nbuf, _ = win_buf.shape
    gw = NSA_GROUP * NSA_DK

    def bufspec(which):
        return pl.BlockSpec((1, nbuf, NSA_DK), lambda b, h: (b, 0, which * NSA_KV_HEADS + h))

    def newspec(which):
        return pl.BlockSpec((1, 1, NSA_DK), lambda b, h: (b, 0, COL_KVW // NSA_DK + which * NSA_KV_HEADS + h))

    return pl.pallas_call(
        _nsa_sample_win_kernel,
        grid=(nb, NSA_KV_HEADS),
        in_specs=[pl.BlockSpec((1, 1, gw), lambda b, h: (b, 0, COL_NQ // gw + h)),
                  bufspec(0), bufspec(1), newspec(0), newspec(1),
                  pl.BlockSpec((1, SROWS, nbuf), lambda b, h: (h, 0, 0)),
                  pl.BlockSpec((1, SROWS, 1), lambda b, h: (h, 0, 0))],
        out_specs=pl.BlockSpec((1, 1, SROWS, NSA_DK), lambda b, h: (b, h, 0, 0)),
        out_shape=jax.ShapeDtypeStruct((nb, NSA_KV_HEADS, SROWS, NSA_DK), F32),
        compiler_params=_cparams(2),
        name="nsa_sample_win",
    )(z3, win_buf, win_buf, z3, z3, bias_win, bias_new)


def _nsa_sample_gate_kernel(oc_ref, os_ref, ow_ref, ng_ref, nsl_ref, o_ref):
    gates = _sigmoid(ng_ref[0])
    nsl = nsl_ref[0]
    outs = []
    for g in range(NSA_GROUP):
        o = (gates[:, 3 * g:3 * g + 1] * oc_ref[0, 0, g:g + 1] + gates[:, 3 * g + 1:3 * g + 2] * os_ref[0, 0, g:g + 1]
             + gates[:, 3 * g + 2:3 * g + 3] * ow_ref[0, 0, g:g + 1])
        x = nsl[:, g * NSA_DK:(g + 1) * NSA_DK]
        outs.append(o * (x * _sigmoid(x)))
    o_ref[0] = jnp.concatenate(outs, axis=1).astype(BF16)


def _nsa_sample_gate(o_cmp, o_sel, o_win, z3b):
    nb = z3b.shape[0]
    gw = NSA_GROUP * NSA_DK
    ospec = pl.BlockSpec((1, 1, SROWS, NSA_DK), lambda b, h: (b, h, 0, 0))
    return pl.pallas_call(
        _nsa_sample_gate_kernel,
        grid=(nb, NSA_KV_HEADS),
        in_specs=[ospec, ospec, ospec,
                  pl.BlockSpec((1, 1, LANE), lambda b, h: (b, 0, COL_NG // LANE + h)),
                  pl.BlockSpec((1, 1, gw), lambda b, h: (b, 0, COL_NSL // gw + h))],
        out_specs=pl.BlockSpec((1, 1, gw), lambda b, h: (b, 0, h)),
        out_shape=jax.ShapeDtypeStruct((nb, 1, NSA_W), BF16),
        compiler_params=_cparams(2),
        name="nsa_sample_gate",
    )(o_cmp, o_sel, o_win, z3b, z3b)


def _mem_heads(q, kv):
    outs = []
    for h in range(MEM_HEADS):
        k = kv[:, h * MEM_DH:(h + 1) * MEM_DH].astype(BF16)
        v = kv[:, MEM_W + h * MEM_DH:MEM_W + (h + 1) * MEM_DH].astype(BF16)
        s = _nt(q[:, h * MEM_DH:(h + 1) * MEM_DH].astype(BF16), k) * (MEM_DH ** -0.5)
        e = jnp.exp(s - jnp.max(s, axis=1, keepdims=True))
        p = e / jnp.sum(e, axis=1, keepdims=True)
        outs.append(_dot(p.astype(BF16), v))
    return jnp.concatenate(outs, axis=1)


def _mem_prompt_kernel(q_ref, kv_ref, o_ref):
    o_ref[...] = _mem_heads(q_ref[:, :MEM_W], kv_ref[...]).astype(BF16)


def _mem_prompt(z, mem_kv, batch, seq, tq):
    nq = seq // tq
    n_mem = mem_kv.shape[0] // batch
    return pl.pallas_call(
        _mem_prompt_kernel,
        grid=(batch, nq),
        in_specs=[pl.BlockSpec((tq, MQ_BLOCK), lambda b, i: (b * nq + i, COL_MQ // MQ_BLOCK)),
                  pl.BlockSpec((n_mem, 2 * MEM_W), lambda b, i: (b, 0))],
        out_specs=pl.BlockSpec((tq, MEM_W), lambda b, i: (b * nq + i, 0)),
        out_shape=jax.ShapeDtypeStruct((batch * seq, MEM_W), BF16),
        compiler_params=_cparams(2),
        name="mem_prompt",
    )(z, mem_kv)


def _mem_sample_kernel(q_ref, kv_ref, o_ref):
    q = jnp.broadcast_to(q_ref[0][:, :MEM_W], (SROWS, MEM_W))
    o_ref[0] = _mem_heads(q, kv_ref[0])[0:1].astype(BF16)


def _mem_sample(z3, mem_kv):
    nb, n_mem, _ = mem_kv.shape
    return pl.pallas_call(
        _mem_sample_kernel,
        grid=(nb,),
        in_specs=[pl.BlockSpec((1, 1, MQ_BLOCK), lambda b: (b, 0, COL_MQ // MQ_BLOCK)),
                  pl.BlockSpec((1, n_mem, 2 * MEM_W), lambda b: (b, 0, 0))],
        out_specs=pl.BlockSpec((1, 1, MEM_W), lambda b: (b, 0, 0)),
        out_shape=jax.ShapeDtypeStruct((nb, 1, MEM_W), BF16),
        compiler_params=_cparams(1),
        name="mem_sample",
    )(z3, mem_kv)


def _merge_kernel(ar_ref, an_ref, am_ref, wr_ref, wn_ref, wm_ref, g0_ref, g1_ref, g2_ref, o_ref):
    merged = (_sigmoid(g0_ref[...]) * _dot(ar_ref[...], wr_ref[...])
              + _sigmoid(g1_ref[...]) * _dot(an_ref[...], wn_ref[...])
              + _sigmoid(g2_ref[...]) * _dot(am_ref[...], wm_ref[...]))
    o_ref[...] = merged.astype(BF16)


def _merge(a_ret, a_nsa, a_mem, w_ret, w_nsa, w_mem, z, tm, tn):
    m = a_ret.shape[0]
    nt = D_MODEL // tn

    def aspec(width):
        return pl.BlockSpec((tm, width), lambda i, j: (i, 0))

    def wspec(width):
        return pl.BlockSpec((width, tn), lambda i, j: (0, j))

    def gspec(branch):
        return pl.BlockSpec((tm, tn), lambda i, j: (i, COL_MG // tn + branch * nt + j))

    return pl.pallas_call(
        _merge_kernel,
        grid=(m // tm, nt),
        in_specs=[aspec(RET_W), aspec(NSA_W), aspec(MEM_W), wspec(RET_W), wspec(NSA_W), wspec(MEM_W),
                  gspec(0), gspec(1), gspec(2)],
        out_specs=pl.BlockSpec((tm, tn), lambda i, j: (i, j)),
        out_shape=jax.ShapeDtypeStruct((m, D_MODEL), BF16),
        compiler_params=_cparams(2),
        name="merge",
    )(a_ret, a_nsa, a_mem, w_ret, w_nsa, w_mem, z, z, z)


def _out_kernel(a_ref, w_ref, x_ref, g_ref, o_ref):
    out = _dot(a_ref[...], w_ref[...])
    y = out * lax.rsqrt(jnp.mean(out * out, axis=-1, keepdims=True) + EPS)
    o_ref[...] = x_ref[...] + y * g_ref[...]


def _out_proj(merged, w_out, x, norm_post, tm):
    m = merged.shape[0]
    return pl.pallas_call(
        _out_kernel,
        grid=(m // tm,),
        in_specs=[pl.BlockSpec((tm, D_MODEL), lambda i: (i, 0)),
                  pl.BlockSpec((D_MODEL, D_MODEL), lambda i: (0, 0)),
                  pl.BlockSpec((tm, D_MODEL), lambda i: (i, 0)),
                  pl.BlockSpec((1, D_MODEL), lambda i: (0, 0))],
        out_specs=pl.BlockSpec((tm, D_MODEL), lambda i: (i, 0)),
        out_shape=jax.ShapeDtypeStruct((m, D_MODEL), F32),
        compiler_params=_cparams(1),
        name="out_proj",
    )(merged, w_out, x, norm_post.reshape(1, D_MODEL))


def _layout_w_ng(w_t):
    per_group = N_BRANCHES * NSA_GROUP
    ng = w_t[PROJ_A:PROJ_A + N_BRANCHES * NSA_HEADS].reshape(NSA_KV_HEADS, per_group, D_MODEL)
    return jnp.pad(ng, ((0, 0), (0, LANE - per_group), (0, 0))).reshape(NG_SLOT, D_MODEL)


def _pick_tile(m, cap):
    t = min(m, cap)
    while m % t:
        t //= 2
    return t


def kernel(x_prompt, x_sample, cache_cmp_kv, cache_sel_kv, cache_win_kv, state_ret, cache_mem_kv, page_table,
           mem_prompt, rel_table, norm_pre, norm_post, norm_mem, w_in, ret_norm, w_ret_up, cmp_pos, w_cmp1,
           w_cmp2, w_nsa_up, w_mem_kv, w_mem_up, w_out):
    batch, seq, _ = x_prompt.shape
    nb = x_sample.shape[0]
    assert x_sample.shape[1] == 1 and norm_pre.shape[0] == 1
    assert seq % TQ == 0 and seq >= WINDOW
    n_pool = cache_cmp_kv.shape[1]
    n_pages = page_table.shape[1]
    past = n_pages * PAGE_SIZE
    n_mem = mem_prompt.shape[1]
    assert n_pages % PAGES_PER_STEP == 0 and cache_win_kv.shape[2] == WINDOW

    w_a = w_in[0].T
    w_ng = _layout_w_ng(w_a)
    kw = CMP_STRIDE * NSA_DK
    w1 = w_cmp1[0].reshape(2, CMP_BLOCK * NSA_DK, NSA_DK).astype(BF16)
    w1a, w1b = w1[:, :kw], w1[:, kw:]
    w1ab = jnp.concatenate([w1a, w1b], axis=2)
    w2 = w_cmp2[0].astype(BF16)
    pos8 = jnp.pad(cmp_pos[0].reshape(2, 1, CMP_BLOCK * NSA_DK), ((0, 0), (0, 7), (0, 0)))
    w_ret = w_ret_up[0].astype(BF16)
    w_nsa = w_nsa_up[0].astype(BF16)
    w_mem = w_mem_up[0].astype(BF16)
    w_o = w_out[0].astype(BF16)

    m_p = batch * seq
    xp = x_prompt.reshape(m_p, D_MODEL)
    hp = _rmsnorm(xp, norm_pre[0], _pick_tile(m_p, 512))
    z, kvc_rows, kvs_rows, kvw_rows = _proj_kv(hp, w_a, _pick_tile(seq, WINDOW), seq)
    zb = _proj_tail(hp, w_a, w_ng, _pick_tile(m_p, 1024))

    a_ret, ret_state_p = _retention_prompt(z, ret_norm[0], batch, seq)

    ncp = max(LANE, -(-(seq // CMP_STRIDE) // LANE) * LANE)
    ac = _cmp_stage1_dense(z, w1ab, batch, seq)
    kcvc = _cmp_stage2(ac, pos8, w1a, w1b, w2, 1)
    if ncp > kcvc.shape[3]:
        kcvc = jnp.pad(kcvc, ((0, 0), (0, 0), (0, 0), (0, ncp - kcvc.shape[3]), (0, 0)))
    bias_d = _bias_by_dist(rel_table, BIAS_DISTS)
    a_nsa = _nsa_prompt(z, zb, kcvc, bias_d, batch, seq)

    hm = _rmsnorm(mem_prompt.reshape(batch * n_mem, D_MODEL), norm_mem[0], _pick_tile(batch * n_mem, 512))
    mem_kv_p = _proj(hm, w_mem_kv[0], _pick_tile(batch * n_mem, 512), PROJ_TN)
    a_mem = _mem_prompt(zb, mem_kv_p, batch, seq, _pick_tile(seq, 512))

    merged = _merge(a_ret, a_nsa, a_mem, w_ret, w_nsa, w_mem, zb, _pick_tile(m_p, 512), 512)
    y_p = _out_proj(merged, w_o, xp, norm_post[0], _pick_tile(m_p, 256)).reshape(batch, seq, D_MODEL)

    kv_shape = (1, batch, seq, 2, NSA_KV_HEADS, NSA_DK)
    new_cmp_p = kvc_rows.reshape(kv_shape)
    new_sel_p = kvs_rows.reshape(kv_shape)
    new_win_p = kvw_rows.reshape(1, batch, WINDOW, 2, NSA_KV_HEADS, NSA_DK)
    new_ret_p = ret_state_p[None]
    new_mem_p = mem_kv_p.reshape(1, batch, n_mem, 2, MEM_HEADS, MEM_DH)

    xs = x_sample.reshape(nb, D_MODEL)
    hs = _rmsnorm(xs, norm_pre[0], nb)
    zs = _proj(hs, w_a, nb, PROJ_TN, PROJ_A, transposed=True)
    zsb = _proj_tail(hs, w_a, w_ng, nb)
    z3 = zs.reshape(nb, 1, PROJ_A)
    z3b = zsb.reshape(nb, 1, PROJ_B)

    a_ret_s, ret_state_s = _retention_sample(z3, state_ret[0], ret_norm[0], past)

    cache_c = cache_cmp_kv[0].reshape(n_pool, PAGE_SIZE // CMP_STRIDE, CMP_STRIDE, 2 * NSA_KV_HEADS, NSA_DK)
    cache_s = cache_sel_kv[0].reshape(n_pool, PAGE_SIZE, 2 * NSA_KV_HEADS, NSA_DK)
    ac_s = _cmp_stage1_paged(cache_c, page_table, w1ab)
    kcvc_s = _cmp_stage2(ac_s, pos8, w1a, w1b, w2, NSA_KV_HEADS)
    ncs = past // CMP_STRIDE
    kcvc_s = kcvc_s.reshape(nb, 2, ncs, KV_W)
    ns_s = past // SEL_BLOCK + 1
    nsp_s = -(-ns_s // LANE) * LANE
    assert past >= WINDOW and past >= REL_MAX_DIST and BIAS_DISTS > WINDOW
    far_s = bias_d[..., REL_MAX_DIST:REL_MAX_DIST + 1]
    hg = (NSA_KV_HEADS, NSA_GROUP)
    n_valid = (past - (CMP_BLOCK - 1)) // CMP_STRIDE + 1
    strided = bias_d[..., (past - (CMP_BLOCK - 1)) % CMP_STRIDE::CMP_STRIDE]
    n_tab = strided.shape[-1]
    assert n_valid >= n_tab and n_tab * CMP_STRIDE > REL_MAX_DIST + CMP_STRIDE and ncs >= n_valid
    bias_cs = jnp.concatenate([jnp.broadcast_to(far_s, hg + (n_valid - n_tab,)), strided[..., ::-1],
                               jnp.full(hg + (ncs - n_valid,), NEG_INF, F32)], axis=-1)
    bias_cs = _pad_group_rows(bias_cs, 1)
    nblk = jnp.arange(ncs)[:, None]
    sblk = jnp.arange(nsp_s)[None, :]
    ov_s = ((nblk >= 4 * sblk - 1) & (nblk <= 4 * sblk + 3)).astype(BF16)
    o_cmp_s, idx_s = _nsa_sample_cmp(z3, kcvc_s, bias_cs, ov_s, ns_s)
    n_sel = min(SEL_TOPK, ns_s)
    idx = idx_s[:, :, 0, :n_sel]

    hg = (NSA_KV_HEADS, NSA_GROUP)
    n_key = ns_s * SEL_BLOCK
    bias_sel = jnp.concatenate([jnp.broadcast_to(far_s, hg + (past + 1 - REL_MAX_DIST,)),
                                bias_d[..., :REL_MAX_DIST][..., ::-1],
                                jnp.full(hg + (n_key - past - 1,), NEG_INF, F32)], axis=-1)
    bias_sel = _pad_group_rows(bias_sel.reshape(hg + (ns_s, SEL_BLOCK)).transpose(0, 2, 1, 3), 2)
    own_k = jnp.arange(2 * NSA_KV_HEADS)[None, :] == jnp.arange(NSA_KV_HEADS)[:, None]
    bias_sel = jnp.where(own_k[:, None, None, None, :], bias_sel[..., None], NEG_INF)
    bias_sel = bias_sel.reshape(NSA_KV_HEADS, ns_s, SROWS, SEL_BLOCK * 2 * NSA_KV_HEADS)
    kvs_new = zs[:, COL_KVS:COL_KVS + 2 * KV_W].reshape(nb, 2 * NSA_KV_HEADS, NSA_DK)
    o_sel_s = _nsa_sample_sel(z3, cache_s, kvs_new, page_table, idx, bias_sel, ns_s)

    win_buf = cache_win_kv[0].reshape(nb, WINDOW, 2 * KV_W)
    bias_w = jnp.concatenate([jnp.full(hg + (1,), NEG_INF, F32), bias_d[..., 1:WINDOW][..., ::-1]], axis=-1)
    o_win_s = _nsa_sample_win(z3, win_buf, _pad_group_rows(bias_w, 1), _pad_group_rows(bias_d[..., 0:1], 1))
    a_nsa_s = _nsa_sample_gate(o_cmp_s, o_sel_s, o_win_s, z3b)

    mem_kv_s = cache_mem_kv[0].reshape(nb, n_mem, 2 * MEM_W)
    a_mem_s = _mem_sample(z3b, mem_kv_s)

    merged_s = _merge(a_ret_s.reshape(nb, RET_W), a_nsa_s.reshape(nb, NSA_W), a_mem_s.reshape(nb, MEM_W),
                      w_ret, w_nsa, w_mem, zsb, nb, 512)
    y_s = _out_proj(merged_s, w_o, xs, norm_post[0], nb).reshape(nb, 1, D_MODEL)

    kvs_shape = (1, nb, 1, 2, NSA_KV_HEADS, NSA_DK)
    new_cmp_s = zs[:, COL_KVC:COL_KVC + 2 * KV_W].reshape(kvs_shape)
    new_sel_s = zs[:, COL_KVS:COL_KVS + 2 * KV_W].reshape(kvs_shape)
    kvw_s = zs[:, COL_KVW:COL_KVW + 2 * KV_W].reshape(nb, 1, 2, NSA_KV_HEADS, NSA_DK)
    new_win_s = jnp.concatenate([cache_win_kv[0][:, 1:], kvw_s], axis=1)[None]
    new_ret_s = ret_state_s[None]

    return (y_p, y_s, new_cmp_p, new_sel_p, new_win_p, new_ret_p, new_mem_p,
            new_cmp_s, new_sel_s, new_win_s, new_ret_s)
```

```python
import functools
import math

import jax
import jax.numpy as jnp
from jax import lax
from jax.experimental import pallas as pl
from jax.experimental.pallas import tpu as pltpu

F32 = jnp.float32
BF16 = jnp.bfloat16

D_MODEL = 2048
PAGE_SIZE = 128
RET_HEADS = 8
RET_DK = 256
RET_DV = 256
RET_CHUNK = 128
ROPE_BASE = 10000.0
NSA_HEADS = 16
NSA_KV_HEADS = 4
NSA_GROUP = NSA_HEADS // NSA_KV_HEADS
NSA_DK = 128
CMP_BLOCK = 32
CMP_STRIDE = 16
SEL_BLOCK = 64
SEL_TOPK = 16
WINDOW = 512
MEM_HEADS = 4
MEM_DH = 384
REL_BUCKETS = 32
REL_MAX_EXACT = 16
REL_MAX_DIST = 128
N_BRANCHES = 3
EPS = 1e-6
NEG_INF = -1e30
FORCE_SCORE = 1e4

RET_W = RET_HEADS * RET_DV
NSA_W = NSA_HEADS * NSA_DK
KV_W = NSA_KV_HEADS * NSA_DK
MEM_W = MEM_HEADS * MEM_DH

COL_RQ = 0
COL_RK = COL_RQ + RET_HEADS * RET_DK
COL_RV = COL_RK + RET_HEADS * RET_DK
COL_RG = COL_RV + RET_W
COL_NQ = COL_RG + RET_W
COL_KVC = COL_NQ + NSA_W
COL_KVS = COL_KVC + 2 * KV_W
COL_KVW = COL_KVS + 2 * KV_W
PROJ_A = COL_KVW + 2 * KV_W
COL_NSL = 0
COL_MG = COL_NSL + NSA_W
COL_MQ = COL_MG + N_BRANCHES * D_MODEL
COL_NG = COL_MQ + MEM_W
NG_SLOT = NSA_KV_HEADS * 128
PROJ_B = COL_NG + NG_SLOT
MQ_BLOCK = MEM_W + NG_SLOT

LOG2E = math.log2(math.e)
LANE = 128
TQ = 256
TK = 256
NSA_ROWS = NSA_GROUP * TQ
BIAS_DISTS = 1024
FAR_GROUP = 4
RET_STEP_CHUNKS = 8
PROJ_TN = 1024
VMEM_LIMIT = 56 * 1024 * 1024


def _cparams(n_axes):
    return pltpu.CompilerParams(dimension_semantics=("arbitrary",) * n_axes, vmem_limit_bytes=VMEM_LIMIT)


def _nt(a, b):
    return lax.dot_general(a, b, (((1,), (1,)), ((), ())), preferred_element_type=F32)


def _dot(a, b):
    return jnp.dot(a, b, preferred_element_type=F32)


def _sigmoid(x):
    return 1.0 / (1.0 + jnp.exp(-x))


def _iota(shape, dim):
    return lax.broadcasted_iota(jnp.int32, shape, dim)


def _rmsnorm_kernel(x_ref, g_ref, h_ref):
    x = x_ref[...]
    ms = jnp.mean(x * x, axis=-1, keepdims=True)
    h_ref[...] = ((x * lax.rsqrt(ms + EPS)) * g_ref[...]).astype(BF16)


def _rmsnorm(x, g, tm):
    m, k = x.shape
    return pl.pallas_call(
        _rmsnorm_kernel,
        grid=(m // tm,),
        in_specs=[pl.BlockSpec((tm, k), lambda i: (i, 0)), pl.BlockSpec((1, k), lambda i: (0, 0))],
        out_specs=pl.BlockSpec((tm, k), lambda i: (i, 0)),
        out_shape=jax.ShapeDtypeStruct((m, k), BF16),
        compiler_params=_cparams(1),
        name="rmsnorm",
    )(x, g.reshape(1, k))


def _proj_kernel(h_ref, w_ref, o_ref, wb_ref, *, transposed):
    @pl.when(pl.program_id(1) == 0)
    def _():
        if transposed:
            _store_transposed(wb_ref, w_ref, wb_ref.shape[1])
        else:
            wb_ref[...] = w_ref[...].astype(BF16)

    o_ref[...] = _dot(h_ref[...], wb_ref[...])


def _proj(h, w, tm, tn, n=None, transposed=False):
    m, k = h.shape
    n = w.shape[0 if transposed else 1] if n is None else n
    wspec = pl.BlockSpec((tn, k), lambda j, i: (j, 0)) if transposed else pl.BlockSpec((k, tn), lambda j, i: (0, j))
    return pl.pallas_call(
        functools.partial(_proj_kernel, transposed=transposed),
        grid=(n // tn, m // tm),
        in_specs=[pl.BlockSpec((tm, k), lambda j, i: (i, 0)), wspec],
        out_specs=pl.BlockSpec((tm, tn), lambda j, i: (i, j)),
        out_shape=jax.ShapeDtypeStruct((m, n), F32),
        scratch_shapes=[pltpu.VMEM((k, tn), BF16)],
        compiler_params=_cparams(2),
        name="proj",
    )(h, w)


def _proj_kv_kernel(h_ref, w_ref, o_ref, oc_ref, os_ref, ow_ref, wb_ref, *, tiles_per_batch):
    j, i = pl.program_id(0), pl.program_id(1)
    n_tiles = pl.num_programs(0)

    @pl.when(i == 0)
    def _():
        _store_transposed(wb_ref, w_ref, wb_ref.shape[1])

    res = _dot(h_ref[...], wb_ref[...])
    o_ref[...] = res
    tm, tn = res.shape
    rows_kv = tn // NSA_DK

    @pl.when(j == n_tiles - 3)
    def _():
        oc_ref[...] = res.reshape(tm * rows_kv, NSA_DK)

    @pl.when(j == n_tiles - 2)
    def _():
        os_ref[...] = res.reshape(tm * rows_kv, NSA_DK)

    @pl.when((j == n_tiles - 1) & (i % tiles_per_batch == tiles_per_batch - 1))
    def _():
        ow_ref[...] = res[tm - WINDOW:, :].reshape(WINDOW * rows_kv, NSA_DK)


def _proj_kv(h, w_t, tm, seq):
    m, k = h.shape
    tn = 2 * KV_W
    assert PROJ_A % tn == 0 and COL_KVC == PROJ_A - 3 * tn and seq % tm == 0 and tm >= WINDOW
    n_j, n_i = PROJ_A // tn, m // tm
    tiles_per_batch = seq // tm
    rows_kv = tn // NSA_DK

    def kv_rows(tile):
        return lambda j, i: (jnp.where(j < tile, 0, jnp.where(j == tile, i, n_i - 1)), 0)

    return pl.pallas_call(
        functools.partial(_proj_kv_kernel, tiles_per_batch=tiles_per_batch),
        grid=(n_j, n_i),
        in_specs=[pl.BlockSpec((tm, k), lambda j, i: (i, 0)),
                  pl.BlockSpec((tn, k), lambda j, i: (j, 0))],
        out_specs=[pl.BlockSpec((tm, tn), lambda j, i: (i, j)),
                   pl.BlockSpec((tm * rows_kv, NSA_DK), kv_rows(n_j - 3)),
                   pl.BlockSpec((tm * rows_kv, NSA_DK), kv_rows(n_j - 2)),
                   pl.BlockSpec((WINDOW * rows_kv, NSA_DK),
                                lambda j, i: (jnp.where(j < n_j - 1, 0, i // tiles_per_batch), 0))],
        out_shape=[jax.ShapeDtypeStruct((m, PROJ_A), F32),
                   jax.ShapeDtypeStruct((m * rows_kv, NSA_DK), F32),
                   jax.ShapeDtypeStruct((m * rows_kv, NSA_DK), F32),
                   jax.ShapeDtypeStruct((m // seq * WINDOW * rows_kv, NSA_DK), F32)],
        scratch_shapes=[pltpu.VMEM((k, tn), BF16)],
        compiler_params=_cparams(2),
        name="proj_kv",
    )(h, w_t)


TAIL_TN = 1024


def _store_transposed(dst_ref, src_ref, rows, chunk=256):
    for r0 in range(0, rows, chunk):
        dst_ref[:, r0:r0 + chunk] = src_ref[r0:r0 + chunk, :].T.astype(BF16)


def _proj_tail_kernel(h_ref, w_ref, wng_ref, o_ref, wb_ref, *, last_rows):
    tn = wb_ref.shape[1]
    is_last = pl.program_id(0) == pl.num_programs(0) - 1

    @pl.when((pl.program_id(1) == 0) & jnp.logical_not(is_last))
    def _():
        _store_transposed(wb_ref, w_ref, tn)

    @pl.when((pl.program_id(1) == 0) & is_last)
    def _():
        _store_transposed(wb_ref, w_ref, last_rows)
        wb_ref[:, last_rows:] = wng_ref[...].T.astype(BF16)

    o_ref[...] = _dot(h_ref[...], wb_ref[...])


def _proj_tail(h, w_t, w_ng, tm):
    m, k = h.shape
    tn = TAIL_TN
    row_ng = PROJ_A
    row_nsl = row_ng + N_BRANCHES * NSA_HEADS
    row_mq = row_nsl + NSA_W
    row_mg = row_mq + MEM_W
    assert row_mg + N_BRANCHES * D_MODEL == w_t.shape[0] and w_ng.shape == (NG_SLOT, k)
    assert COL_NSL == 0 and COL_MG % tn == 0 and COL_MQ % tn == 0 and COL_NG + NG_SLOT == PROJ_B == COL_MQ + 2 * tn
    assert row_mq % 16 == 0 and row_nsl % 16 == 0 and row_mg % 16 == 0
    t_mg, t_mq = COL_MG // tn, COL_MQ // tn

    def w_row(j, i):
        row = jnp.where(j < t_mg, row_nsl + j * tn,
                        jnp.where(j < t_mq, row_mg + (j - t_mg) * tn, row_mq + (j - t_mq) * tn))
        return (pl.multiple_of(row, 16), 0)

    return pl.pallas_call(
        functools.partial(_proj_tail_kernel, last_rows=COL_NG - COL_MQ - tn),
        grid=(PROJ_B // tn, m // tm),
        in_specs=[pl.BlockSpec((tm, k), lambda j, i: (i, 0)),
                  pl.BlockSpec((pl.Element(tn), pl.Element(k)), w_row),
                  pl.BlockSpec((NG_SLOT, k), lambda j, i: (0, 0))],
        out_specs=pl.BlockSpec((tm, tn), lambda j, i: (i, j)),
        out_shape=jax.ShapeDtypeStruct((m, PROJ_B), F32),
        scratch_shapes=[pltpu.VMEM((k, tn), BF16)],
        compiler_params=_cparams(2),
        name="proj_tail",
    )(h, w_t, w_ng)


def _rope_rows(x, cos, sin):
    half = x.shape[-1] // 2
    x1, x2 = x[:, :half], x[:, half:]
    return jnp.concatenate([x1 * cos - x2 * sin, x1 * sin + x2 * cos], axis=-1)


def _head_norm_gate(o, gnorm, rg):
    oc = o - jnp.mean(o, axis=-1, keepdims=True)
    y = oc * lax.rsqrt(jnp.mean(oc * oc, axis=-1, keepdims=True) + EPS) * gnorm
    return y * (rg * _sigmoid(rg))


def _ret_prompt_kernel(q_ref, k_ref, v_ref, rg_ref, cos_ref, sin_ref, dmat_ref, xi_ref, zeta_ref, gc_ref,
                       gn_ref, a_ref, s_ref):
    @pl.when(pl.program_id(2) == 0)
    def _():
        s_ref[...] = jnp.zeros_like(s_ref)

    c = RET_CHUNK
    for t in range(q_ref.shape[0] // c):
        rows = slice(t * c, (t + 1) * c)
        cos, sin = cos_ref[rows, :], sin_ref[rows, :]
        q = _rope_rows(q_ref[rows, :], cos, sin)
        k = _rope_rows(k_ref[rows, :], cos, sin) * (RET_DK ** -0.5)
        qb, vb = q.astype(BF16), v_ref[rows, :].astype(BF16)
        state = s_ref[0, 0]
        inner = _nt(qb, k.astype(BF16)) * dmat_ref[0]
        o = _dot(inner.astype(BF16), vb) + _dot(qb, state.astype(BF16)) * xi_ref[0]
        kz_t = (k * zeta_ref[0]).T.astype(BF16)
        s_ref[0, 0] = state * gc_ref[0] + _dot(kz_t, vb)
        a_ref[rows, :] = _head_norm_gate(o, gn_ref[...], rg_ref[rows, :]).astype(BF16)


def _decay_tables(chunk):
    log_g = jnp.log1p(-jnp.exp2(-5.0 - jnp.arange(RET_HEADS, dtype=F32)))
    i = jnp.arange(chunk, dtype=F32)
    diff = i[:, None] - i[None, :]
    dmat = jnp.where(diff >= 0, jnp.exp(log_g[:, None, None] * jnp.maximum(diff, 0.0)), 0.0)
    xi = jnp.exp(log_g[:, None] * (i[None, :] + 1.0))[:, :, None]
    zeta = jnp.exp(log_g[:, None] * (chunk - 1.0 - i[None, :]))[:, :, None]
    g_chunk = jnp.exp(log_g * chunk)[:, None, None]
    return dmat, xi, zeta, g_chunk


def _rope_tables(pos):
    half = RET_DK // 2
    freq = jnp.power(ROPE_BASE, -jnp.arange(half, dtype=F32) / half)
    ang = pos.astype(F32)[:, None] * freq[None, :]
    return jnp.cos(ang), jnp.sin(ang)


def _retention_prompt(z, ret_norm, batch, seq):
    c = RET_CHUNK
    rows = _pick_tile(seq, RET_STEP_CHUNKS * c)
    nc = seq // rows
    dmat, xi, zeta, g_chunk = _decay_tables(c)
    cos, sin = _rope_tables(jnp.arange(seq))
    hb = RET_DK

    def zspec(col0):
        return pl.BlockSpec((rows, hb), lambda b, h, t, col0=col0: (b * nc + t, col0 // hb + h))

    per_head = lambda shape: pl.BlockSpec((1,) + shape, lambda b, h, t: (h, 0, 0))
    return pl.pallas_call(
        _ret_prompt_kernel,
        grid=(batch, RET_HEADS, nc),
        in_specs=[zspec(COL_RQ), zspec(COL_RK), zspec(COL_RV), zspec(COL_RG),
                  pl.BlockSpec((rows, hb // 2), lambda b, h, t: (t, 0)),
                  pl.BlockSpec((rows, hb // 2), lambda b, h, t: (t, 0)),
                  per_head((c, c)), per_head((c, 1)), per_head((c, 1)), per_head((1, 1)),
                  pl.BlockSpec((1, hb), lambda b, h, t: (0, h))],
        out_specs=[pl.BlockSpec((rows, hb), lambda b, h, t: (b * nc + t, h)),
                   pl.BlockSpec((1, 1, RET_DK, RET_DV), lambda b, h, t: (b, h, 0, 0))],
        out_shape=[jax.ShapeDtypeStruct((batch * seq, RET_W), BF16),
                   jax.ShapeDtypeStruct((batch, RET_HEADS, RET_DK, RET_DV), F32)],
        compiler_params=_cparams(3),
        name="retention_prompt",
    )(z, z, z, z, cos, sin, dmat, xi, zeta, g_chunk, ret_norm.reshape(1, RET_W))


def _column_of(row):
    n = row.shape[1]
    eye = _iota((n, n), 0) == _iota((n, n), 1)
    return jnp.sum(jnp.where(eye, jnp.broadcast_to(row, (n, n)), 0.0), axis=1, keepdims=True)


def _ret_sample_kernel(q_ref, k_ref, v_ref, rg_ref, cos_ref, sin_ref, gam_ref, gn_ref, s_ref, a_ref, so_ref):
    cos, sin = cos_ref[...], sin_ref[...]
    outs = []
    for h in range(RET_HEADS):
        cols = slice(h * RET_DK, (h + 1) * RET_DK)
        q = _rope_rows(q_ref[0][:, cols], cos, sin)
        k = _rope_rows(k_ref[0][:, cols], cos, sin) * (RET_DK ** -0.5)
        v = v_ref[0][:, cols]
        state = s_ref[0, h]
        gamma = gam_ref[h]
        qk = jnp.sum(q * k, axis=-1, keepdims=True)
        o = qk * v + jnp.sum(_column_of(q) * state, axis=0, keepdims=True) * gamma
        so_ref[0, h] = state * gamma + _column_of(k) * v
        outs.append(_head_norm_gate(o, gn_ref[:, cols], rg_ref[0][:, cols]))
    a_ref[0] = jnp.concatenate(outs, axis=1).astype(BF16)


def _retention_sample(z3, state, ret_norm, pos):
    nb = z3.shape[0]
    cos, sin = _rope_tables(jnp.full((1,), pos))
    gamma = jnp.exp(jnp.log1p(-jnp.exp2(-5.0 - jnp.arange(RET_HEADS, dtype=F32))))[:, None, None]
    hb = RET_DK
    assert RET_HEADS * RET_DK == RET_W

    def zspec(col0):
        return pl.BlockSpec((1, 1, RET_W), lambda b, col0=col0: (b, 0, col0 // RET_W))

    st_spec = pl.BlockSpec((1, RET_HEADS, RET_DK, RET_DV), lambda b: (b, 0, 0, 0))
    return pl.pallas_call(
        _ret_sample_kernel,
        grid=(nb,),
        in_specs=[zspec(COL_RQ), zspec(COL_RK), zspec(COL_RV), zspec(COL_RG),
                  pl.BlockSpec((1, hb // 2), lambda b: (0, 0)),
                  pl.BlockSpec((1, hb // 2), lambda b: (0, 0)),
                  pl.BlockSpec((RET_HEADS, 1, 1), lambda b: (0, 0, 0)),
                  pl.BlockSpec((1, RET_W), lambda b: (0, 0)),
                  st_spec],
        out_specs=[pl.BlockSpec((1, 1, RET_W), lambda b: (b, 0, 0)), st_spec],
        out_shape=[jax.ShapeDtypeStruct((nb, 1, RET_W), BF16),
                   jax.ShapeDtypeStruct(state.shape, F32)],
        compiler_params=_cparams(1),
        name="retention_sample",
    )(z3, z3, z3, z3, cos, sin, gamma, ret_norm.reshape(1, RET_W), state)


def _half_rows(ref_slice_fn, n_half):
    return jnp.concatenate([ref_slice_fn(p) for p in range(CMP_STRIDE)], axis=1)


def _cmp_stage1_dense_kernel(x_ref, w_ref, o_ref):
    nh = o_ref.shape[3]
    x = _half_rows(lambda p: x_ref[pl.ds(p, nh, stride=CMP_STRIDE), :], nh).astype(BF16)
    o_ref[0, 0, 0] = _dot(x, w_ref[0])


def _cmp_stage1_dense(z, w1ab, batch, seq):
    nh = seq // CMP_STRIDE
    return pl.pallas_call(
        _cmp_stage1_dense_kernel,
        grid=(batch, 2, NSA_KV_HEADS),
        in_specs=[pl.BlockSpec((seq, NSA_DK), lambda b, kv, h: (b, COL_KVC // NSA_DK + kv * NSA_KV_HEADS + h)),
                  pl.BlockSpec((1, CMP_STRIDE * NSA_DK, 2 * NSA_DK), lambda b, kv, h: (kv, 0, 0))],
        out_specs=pl.BlockSpec((1, 1, 1, nh, 2 * NSA_DK), lambda b, kv, h: (b, kv, h, 0, 0)),
        out_shape=jax.ShapeDtypeStruct((batch, 2, NSA_KV_HEADS, nh, 2 * NSA_DK), F32),
        compiler_params=_cparams(3),
        name="cmp_stage1_dense",
    )(z, w1ab)


PAGES_PER_STEP = 16


def _cmp_stage1_paged_kernel(pt_ref, *refs):
    pages, (w_ref, o_ref) = refs[:PAGES_PER_STEP], refs[PAGES_PER_STEP:]
    hp = PAGE_SIZE // CMP_STRIDE
    top = _iota((2 * NSA_KV_HEADS, NSA_DK), 0) < NSA_KV_HEADS
    cols = [[], []]
    for p in range(CMP_STRIDE):
        tiles = [[], []]
        for pg in pages:
            xp = pg[0, :, p]
            for n in range(0, hp, 2):
                a, b = xp[n], xp[n + 1]
                tiles[0].append(jnp.where(top, a, pltpu.roll(b, NSA_KV_HEADS, 0)))
                tiles[1].append(jnp.where(top, pltpu.roll(a, NSA_KV_HEADS, 0), b))
        for kv in range(2):
            cols[kv].append(jnp.concatenate(tiles[kv], axis=0))
    for kv in range(2):
        x = jnp.concatenate(cols[kv], axis=1).astype(BF16)
        o_ref[0, kv, 0] = _dot(x, w_ref[kv])


def _cmp_stage1_paged(cache, page_table, w1ab):
    nb, n_pages = page_table.shape
    hp = PAGE_SIZE // CMP_STRIDE
    steps = n_pages // PAGES_PER_STEP
    rows = PAGES_PER_STEP * hp * NSA_KV_HEADS

    def page_spec(j):
        return pl.BlockSpec((1, hp, CMP_STRIDE, 2 * NSA_KV_HEADS, NSA_DK),
                            lambda b, s, pt, j=j: (pt[b, s * PAGES_PER_STEP + j], 0, 0, 0, 0))

    wspec = pl.BlockSpec((2, CMP_STRIDE * NSA_DK, 2 * NSA_DK), lambda b, s, pt: (0, 0, 0))
    return pl.pallas_call(
        _cmp_stage1_paged_kernel,
        grid_spec=pltpu.PrefetchScalarGridSpec(
            num_scalar_prefetch=1,
            grid=(nb, steps),
            in_specs=[page_spec(j) for j in range(PAGES_PER_STEP)] + [wspec],
            out_specs=pl.BlockSpec((1, 2, 1, rows, 2 * NSA_DK), lambda b, s, pt: (b, 0, 0, s, 0))),
        out_shape=jax.ShapeDtypeStruct((nb, 2, 1, steps * rows, 2 * NSA_DK), F32),
        compiler_params=_cparams(2),
        name="cmp_stage1_paged",
    )(page_table, *([cache] * PAGES_PER_STEP), w1ab)


def _cmp_stage2_kernel(ac_ref, pos_ref, w1a_ref, w1b_ref, w2_ref, o_ref, *, shift):
    ac = ac_ref[0, 0, 0]
    nh = ac.shape[0]
    pos = pos_ref[0].astype(BF16)
    kw = CMP_STRIDE * NSA_DK
    pe = _dot(pos[:, :kw], w1a_ref[0]) + _dot(pos[:, kw:], w1b_ref[0])
    pre = ac[:, :NSA_DK] + pltpu.roll(ac[:, NSA_DK:], nh - shift, 0) + pe[0:1]
    gelu = 0.5 * pre * (1.0 + jnp.tanh(math.sqrt(2.0 / math.pi) * (pre + 0.044715 * (pre * pre * pre))))
    o_ref[0, 0, 0] = _dot(gelu.astype(BF16), w2_ref[0]).astype(BF16)


def _cmp_stage2(ac, pos8, w1a, w1b, w2, shift):
    nb, _, groups, nh, _ = ac.shape
    kw = CMP_STRIDE * NSA_DK
    return pl.pallas_call(
        functools.partial(_cmp_stage2_kernel, shift=shift),
        grid=(nb, 2, groups),
        in_specs=[pl.BlockSpec((1, 1, 1, nh, 2 * NSA_DK), lambda b, kv, h: (b, kv, h, 0, 0)),
                  pl.BlockSpec((1, 8, 2 * kw), lambda b, kv, h: (kv, 0, 0)),
                  pl.BlockSpec((1, kw, NSA_DK), lambda b, kv, h: (kv, 0, 0)),
                  pl.BlockSpec((1, kw, NSA_DK), lambda b, kv, h: (kv, 0, 0)),
                  pl.BlockSpec((1, NSA_DK, NSA_DK), lambda b, kv, h: (kv, 0, 0))],
        out_specs=pl.BlockSpec((1, 1, 1, nh, NSA_DK), lambda b, kv, h: (b, kv, h, 0, 0)),
        out_shape=jax.ShapeDtypeStruct((nb, 2, groups, nh, NSA_DK), BF16),
        compiler_params=_cparams(3),
        name="cmp_stage2",
    )(ac, pos8, w1a, w1b, w2)


def _rel_bucket(dist):
    n = jnp.maximum(dist, 0)
    nf = jnp.maximum(n, 1).astype(F32)
    scale = (REL_BUCKETS - REL_MAX_EXACT) / math.log(REL_MAX_DIST / REL_MAX_EXACT)
    large = REL_MAX_EXACT + (jnp.log(nf / REL_MAX_EXACT) * scale).astype(jnp.int32)
    large = jnp.minimum(large, REL_BUCKETS - 1)
    return jnp.where(n < REL_MAX_EXACT, n, large)


def _bias_by_dist(rel_table, n):
    tab = rel_table[_rel_bucket(jnp.arange(n))]
    return tab.T.reshape(NSA_KV_HEADS, NSA_GROUP, n)


def _pad_group_rows(t, axis):
    first = lax.slice_in_dim(t, 0, 1, axis=axis)
    return jnp.concatenate([t] + [first] * (SROWS - NSA_GROUP), axis=axis)


def _flash_init(m_ref, l_ref, acc_ref):
    m_ref[...] = jnp.full(m_ref.shape, NEG_INF, F32)
    l_ref[...] = jnp.zeros(l_ref.shape, F32)
    acc_ref[...] = jnp.zeros(acc_ref.shape, F32)


def _flash_step(s, v, m_ref, l_ref, acc_ref):
    m_old = m_ref[...]
    m_new = jnp.maximum(m_old, jnp.max(s, axis=1, keepdims=True))
    alpha = jnp.exp(m_old - m_new)
    p = jnp.exp(s - m_new)
    l_ref[...] = alpha * l_ref[...] + jnp.sum(p, axis=1, keepdims=True)
    acc_ref[...] = alpha * acc_ref[...] + _dot(p.astype(BF16), v)
    m_ref[...] = m_new


def _flash_result(l_ref, acc_ref):
    return acc_ref[...] / jnp.maximum(l_ref[...], 1e-30)


def _select_blocks(imp_t, q0, ns):
    shape = imp_t.shape
    blk = _iota(shape, 0)
    qpos = q0 + _iota(shape, 1)
    cur = qpos >> 6
    valid = blk * SEL_BLOCK <= qpos
    forced = (blk == 0) | (blk == cur) | (blk == cur - 1)
    imp_t = jnp.where(valid, jnp.where(forced, FORCE_SCORE, imp_t), NEG_INF)
    rank = jnp.zeros(shape, F32)
    for other in range(ns):
        row = imp_t[other:other + 1, :]
        ahead = (row > imp_t) | ((row == imp_t) & (blk > other))
        rank = rank + jnp.where(ahead, 1.0, 0.0)
    return jnp.where((rank < SEL_TOPK) & valid, 1.0, 0.0)


def _nsa_prompt_kernel(q_ref, ks_ref, vs_ref, kw_ref, vw_ref, kc_ref, vct_ref, wd_ref, cfar_ref, basec_ref,
                       ovt_ref, ng_ref, nsl_ref, o_ref, m_ref, l_ref, acc_ref, vst_ref, vwt_ref, tz_ref, *, ns):
    i = pl.program_id(2)
    q0 = i * TQ
    qall = q_ref[...] * ((NSA_DK ** -0.5) * LOG2E)
    qt = jnp.concatenate([qall[:, g * NSA_DK:(g + 1) * NSA_DK].T for g in range(NSA_GROUP)], axis=1).astype(BF16)
    c_loc = _iota((TK, NSA_ROWS), 0)
    r_loc = _iota((TK, NSA_ROWS), 1) & (TQ - 1)

    @pl.when(i == 0)
    def _():
        for kt in range(vst_ref.shape[0]):
            vst_ref[kt] = vs_ref[kt * TK:(kt + 1) * TK, :].T.astype(BF16)
            vwt_ref[kt] = vw_ref[kt * TK:(kt + 1) * TK, :].T.astype(BF16)
        below = _iota((TK, TQ), 1) >= _iota((TK, TQ), 0)
        for g in range(NSA_GROUP):
            lo = pltpu.roll(jnp.broadcast_to(wd_ref[0, 2 * g:2 * g + 1, :], (TK, TQ)), 0, 1, stride=1, stride_axis=0)
            hi = pltpu.roll(jnp.broadcast_to(wd_ref[0, 2 * g + 1:2 * g + 2, :], (TK, TQ)), 0, 1, stride=1, stride_axis=0)
            tz_ref[0, :, g * TQ:(g + 1) * TQ] = lo
            tz_ref[1, :, g * TQ:(g + 1) * TQ] = jnp.where(below, hi, lo)

    ncp = kc_ref.shape[3]
    shift = (TQ // CMP_STRIDE) * i
    bias_c = basec_ref[0, pl.ds(pl.multiple_of(ncp - shift, TQ // CMP_STRIDE), ncp), :]
    s = _dot(kc_ref[0, 0, 0], qt) + bias_c
    m = jnp.max(s, axis=0, keepdims=True)
    e = jnp.exp2(s - m)
    inv = jnp.where(m > 0.5 * NEG_INF, 1.0 / jnp.maximum(jnp.sum(e, axis=0, keepdims=True), 1e-30), 0.0)
    p = e * inv
    o_cmp = _dot(vct_ref[0, 0], p.astype(BF16))

    psum = p[:, 0:TQ] + p[:, TQ:2 * TQ] + p[:, 2 * TQ:3 * TQ] + p[:, 3 * TQ:4 * TQ]
    hi = psum.astype(BF16)
    lo = (psum - hi.astype(F32)).astype(BF16)
    ovt = ovt_ref[...]
    imp_t = _dot(ovt, hi) + _dot(ovt, lo)
    ns8 = -(-ns // 8) * 8
    sel_t = _select_blocks(imp_t[:ns8], q0, ns)
    assert ns8 + 2 <= LANE
    sel_neg = jnp.concatenate([jnp.where(sel_t > 0.5, 0.0, NEG_INF)] * NSA_GROUP, axis=1)
    cfar = cfar_ref[0]
    cfar_hi = cfar.astype(BF16).astype(F32)
    sub8 = _iota((8, NSA_ROWS), 0)
    cfar8 = jnp.where(sub8 == 0, cfar_hi, jnp.where(sub8 == 1, cfar - cfar_hi, 0.0))
    pad_rows = jnp.zeros((LANE - ns8 - 8, NSA_ROWS), F32)
    qx_far = jnp.concatenate([qt, jnp.concatenate([sel_neg, cfar8, pad_rows], axis=0).astype(BF16)], axis=0)
    qx_near = jnp.concatenate([qt, jnp.concatenate([sel_neg, jnp.zeros_like(cfar8), pad_rows], axis=0).astype(BF16)],
                              axis=0)
    lane_k = _iota((TK, LANE), 1)
    ones_k = (lane_k == ns8) | (lane_k == ns8 + 1)

    def flash_step(scs, vts):
        m_old = m_ref[...]
        m_new = m_old
        for sc in scs:
            m_new = jnp.maximum(m_new, jnp.max(sc, axis=0, keepdims=True))
        alpha = jnp.exp2(m_old - m_new)
        l_new = alpha * l_ref[...]
        acc = alpha * acc_ref[...]
        for sc, vt in zip(scs, vts):
            pt = jnp.exp2(sc - m_new)
            l_new = l_new + jnp.sum(pt, axis=0, keepdims=True)
            acc = acc + _dot(vt, pt.astype(BF16))
        l_ref[...] = l_new
        acc_ref[...] = acc
        m_ref[...] = m_new

    def sel_scores(kt, bias, causal):
        k = ks_ref[pl.ds(pl.multiple_of(kt * TK, TK), TK), :].astype(BF16)
        blk_of_key = (TK // SEL_BLOCK) * kt + (_iota((TK, LANE), 0) >> 6)
        extra = jnp.where((lane_k == blk_of_key) | ones_k, 1.0, 0.0).astype(BF16)
        kx = jnp.concatenate([k, extra], axis=1)
        sc = _dot(kx, qx_far) if bias is None else _dot(kx, qx_near) + bias
        if causal:
            sc = jnp.where(c_loc <= r_loc, sc, NEG_INF)
        return sc

    _flash_init(m_ref, l_ref, acc_ref)

    @pl.when(i == 0)
    def _():
        flash_step([sel_scores(i, tz_ref[0], True)], [vst_ref[i]])

    @pl.when(i >= 1)
    def _():
        flash_step([sel_scores(i, tz_ref[0], True), sel_scores(i - 1, tz_ref[1], False)],
                   [vst_ref[i], vst_ref[i - 1]])

    def far_tiles(kt_first, count):
        kts = [kt_first - t for t in range(count)]
        flash_step([sel_scores(kt, None, False) for kt in kts], [vst_ref[kt] for kt in kts])

    def far_group(j, carry):
        far_tiles(i - 2 - FAR_GROUP * j, FAR_GROUP)
        return carry

    n_far = jnp.maximum(i - 1, 0)
    lax.fori_loop(0, n_far // FAR_GROUP, far_group, 0)
    for rest in range(1, FAR_GROUP):
        pl.when(n_far % FAR_GROUP == rest)(functools.partial(far_tiles, rest - 1, rest))

    o_sel = _flash_result(l_ref, acc_ref)

    def win_scores(off):
        kt = i - off
        k = kw_ref[pl.ds(pl.multiple_of(kt * TK, TK), TK), :].astype(BF16)
        if off < 2:
            sc = _dot(k, qt) + tz_ref[off]
        else:
            sc = _dot(jnp.concatenate([k, jnp.where(ones_k, 1.0, 0.0).astype(BF16)], axis=1), qx_far)
        if off == 0:
            sc = jnp.where(c_loc <= r_loc, sc, NEG_INF)
        if off * TK == WINDOW:
            sc = jnp.where(c_loc > r_loc, sc, NEG_INF)
        return sc

    def win_tiles(n_tiles):
        flash_step([win_scores(off) for off in range(n_tiles)], [vwt_ref[i - off] for off in range(n_tiles)])

    _flash_init(m_ref, l_ref, acc_ref)
    n_win = WINDOW // TK + 1
    for n_tiles in range(1, n_win + 1):
        cond = (i == n_tiles - 1) if n_tiles < n_win else (i >= n_win - 1)
        pl.when(cond)(functools.partial(win_tiles, n_tiles))
    o_win = _flash_result(l_ref, acc_ref)

    gates_t = _sigmoid(ng_ref[...]).T
    nsl = nsl_ref[...]
    outs = []
    for g in range(NSA_GROUP):
        cols = slice(g * TQ, (g + 1) * TQ)
        o_t = (gates_t[3 * g:3 * g + 1] * o_cmp[:, cols] + gates_t[3 * g + 1:3 * g + 2] * o_sel[:, cols]
               + gates_t[3 * g + 2:3 * g + 3] * o_win[:, cols])
        x = nsl[:, g * NSA_DK:(g + 1) * NSA_DK]
        outs.append(o_t.T * (x * _sigmoid(x)))
    o_ref[...] = jnp.concatenate(outs, axis=1).astype(BF16)


def _lanes_by_head(t):
    hk, g, rows, tq = t.shape
    return t.transpose(0, 2, 1, 3).reshape(hk, rows, g * tq)


def _nsa_prompt(z, zb, kcvc, bias_d, batch, seq):
    nq = seq // TQ
    ns = seq // SEL_BLOCK
    ncp = kcvc.shape[3]
    nsp = LANE
    wn = TQ // CMP_STRIDE
    assert TQ == TK and ns <= nsp and ncp >= seq // CMP_STRIDE and ncp % LANE == 0 and ncp > wn
    gw = NSA_GROUP * NSA_DK
    n_dist = bias_d.shape[-1]
    assert n_dist >= 2 * TK + TQ
    bias_d = bias_d * LOG2E

    tz = bias_d[..., :2 * TQ].reshape(NSA_KV_HEADS, 2 * NSA_GROUP, TQ)
    far = jnp.broadcast_to(bias_d[..., REL_MAX_DIST][:, :, None, None], (NSA_KV_HEADS, NSA_GROUP, 1, TQ))
    cfar = _lanes_by_head(far)
    half = n_dist // 2
    start = CMP_STRIDE * wn - (CMP_BLOCK - 1)
    assert start + TQ <= half and 2 * wn * CMP_STRIDE - start <= half
    w_ext = jnp.concatenate([bias_d[..., :half], jnp.full(bias_d.shape[:-1] + (n_dist - half,), NEG_INF, F32)], -1)
    near = jnp.tile(w_ext, (1, 1, 2 * wn + 1))[..., :2 * wn * (n_dist - CMP_STRIDE)]
    near = near.reshape(NSA_KV_HEADS, NSA_GROUP, 2 * wn, n_dist - CMP_STRIDE)[..., start:start + TQ]
    basec = jnp.concatenate([jnp.broadcast_to(far, (NSA_KV_HEADS, NSA_GROUP, ncp - wn, TQ)), near,
                             jnp.full((NSA_KV_HEADS, NSA_GROUP, ncp - wn, TQ), NEG_INF, F32)], axis=2)
    basec = _lanes_by_head(basec)
    sblk = jnp.arange(nsp)[:, None]
    nblk = jnp.arange(ncp)[None, :]
    ovt = ((nblk >= 4 * sblk - 1) & (nblk <= 4 * sblk + 3)).astype(BF16)

    vct = kcvc[:, 1].transpose(0, 1, 3, 2)

    def kvspec(col0, which):
        return pl.BlockSpec((seq, NSA_DK), lambda b, h, i: (b, col0 // NSA_DK + which * NSA_KV_HEADS + h))

    vt_scratch = pltpu.VMEM((seq // TK, NSA_DK, TK), BF16)
    return pl.pallas_call(
        functools.partial(_nsa_prompt_kernel, ns=ns),
        grid=(batch, NSA_KV_HEADS, nq),
        in_specs=[pl.BlockSpec((TQ, gw), lambda b, h, i: (b * nq + i, COL_NQ // gw + h)),
                  kvspec(COL_KVS, 0), kvspec(COL_KVS, 1), kvspec(COL_KVW, 0), kvspec(COL_KVW, 1),
                  pl.BlockSpec((1, 1, 1, ncp, NSA_DK), lambda b, h, i: (b, 0, h, 0, 0)),
                  pl.BlockSpec((1, 1, NSA_DK, ncp), lambda b, h, i: (b, h, 0, 0)),
                  pl.BlockSpec((1, 2 * NSA_GROUP, TQ), lambda b, h, i: (h, 0, 0)),
                  pl.BlockSpec((1, 1, NSA_ROWS), lambda b, h, i: (h, 0, 0)),
                  pl.BlockSpec((1, 2 * ncp, NSA_ROWS), lambda b, h, i: (h, 0, 0)),
                  pl.BlockSpec((nsp, ncp), lambda b, h, i: (0, 0)),
                  pl.BlockSpec((TQ, LANE), lambda b, h, i: (b * nq + i, COL_NG // LANE + h)),
                  pl.BlockSpec((TQ, gw), lambda b, h, i: (b * nq + i, COL_NSL // gw + h))],
        out_specs=pl.BlockSpec((TQ, gw), lambda b, h, i: (b * nq + i, h)),
        out_shape=jax.ShapeDtypeStruct((batch * seq, NSA_W), BF16),
        scratch_shapes=[pltpu.VMEM((1, NSA_ROWS), F32), pltpu.VMEM((1, NSA_ROWS), F32),
                        pltpu.VMEM((NSA_DK, NSA_ROWS), F32), vt_scratch, vt_scratch,
                        pltpu.VMEM((2, TK, NSA_ROWS), F32)],
        compiler_params=_cparams(3),
        name="nsa_prompt",
    )(z, z, z, z, z, kcvc, vct, tz, cfar, basec, ovt, zb, zb)


SROWS = 8
SEL_PER_STEP = 2


def _stack_group_q(q_row):
    heads = [q_row[:, g * NSA_DK:(g + 1) * NSA_DK] for g in range(NSA_GROUP)]
    return jnp.concatenate(heads + [heads[0]] * (SROWS - NSA_GROUP), axis=0)


def _nsa_sample_cmp_kernel(q_ref, kc_ref, vc_ref, bias_ref, ov_ref, o_ref, idx_ref, *, ns):
    scale = NSA_DK ** -0.5
    qs = _stack_group_q(q_ref[0]).astype(BF16)
    s = _nt(qs, kc_ref[0, 0]) * scale + bias_ref[0]
    m = jnp.max(s, axis=1, keepdims=True)
    e = jnp.exp(s - m)
    inv = jnp.where(m > 0.5 * NEG_INF, 1.0 / jnp.maximum(jnp.sum(e, axis=1, keepdims=True), 1e-30), 0.0)
    p = e * inv
    o_ref[0, 0] = _dot(p.astype(BF16), vc_ref[0, 0])
    psum = jnp.broadcast_to(jnp.sum(p[0:NSA_GROUP], axis=0, keepdims=True), p.shape)
    hi = psum.astype(BF16)
    lo = (psum - hi.astype(F32)).astype(BF16)
    imp = (_dot(hi, ov_ref[...]) + _dot(lo, ov_ref[...]))[0:1]
    nsp = imp.shape[1]
    blk_r = _iota((1, nsp), 1)
    cur = ns - 1
    forced = (blk_r == 0) | (blk_r == cur) | (blk_r == cur - 1)
    imp = jnp.where(blk_r < ns, jnp.where(forced, FORCE_SCORE, imp), 2.0 * NEG_INF)
    imp_c = _column_of(imp)
    i_r = _iota((nsp, nsp), 1)
    j_c = _iota((nsp, nsp), 0)
    ahead = (imp > imp_c) | ((imp == imp_c) & (i_r < j_c))
    rank_c = jnp.sum(jnp.where(ahead, 1.0, 0.0), axis=1, keepdims=True)
    slot = _iota((nsp, LANE), 1).astype(F32)
    picks = jnp.where(rank_c == slot, _iota((nsp, LANE), 0).astype(F32), 0.0)
    idx_ref[0, 0] = jnp.broadcast_to(jnp.sum(picks, axis=0, keepdims=True), (SROWS, LANE)).astype(jnp.int32)


def _nsa_sample_cmp(z3, kcvc, bias_c, ov, ns):
    nb = z3.shape[0]
    ncp = kcvc.shape[2]
    nsp = ov.shape[1]
    gw = NSA_GROUP * NSA_DK

    def cspec(which):
        return pl.BlockSpec((1, 1, ncp, NSA_DK), lambda b, h: (b, which, 0, h))

    return pl.pallas_call(
        functools.partial(_nsa_sample_cmp_kernel, ns=ns),
        grid=(nb, NSA_KV_HEADS),
        in_specs=[pl.BlockSpec((1, 1, gw), lambda b, h: (b, 0, COL_NQ // gw + h)),
                  cspec(0), cspec(1),
                  pl.BlockSpec((1, SROWS, ncp), lambda b, h: (h, 0, 0)),
                  pl.BlockSpec((ncp, nsp), lambda b, h: (0, 0))],
        out_specs=[pl.BlockSpec((1, 1, SROWS, NSA_DK), lambda b, h: (b, h, 0, 0)),
                   pl.BlockSpec((1, 1, SROWS, LANE), lambda b, h: (b, h, 0, 0))],
        out_shape=[jax.ShapeDtypeStruct((nb, NSA_KV_HEADS, SROWS, NSA_DK), F32),
                   jax.ShapeDtypeStruct((nb, NSA_KV_HEADS, SROWS, LANE), jnp.int32)],
        compiler_params=_cparams(2),
        name="nsa_sample_cmp",
    )(z3, kcvc, kcvc, bias_c, ov)


def _nsa_sample_sel_kernel(pt_ref, idx_ref, q_ref, *refs, ns):
    n_blk = NSA_KV_HEADS * SEL_PER_STEP
    blocks, (new_ref,), biases = refs[:n_blk], refs[n_blk:n_blk + 1], refs[n_blk + 1:2 * n_blk + 1]
    o_ref, m_ref, l_ref, acc_ref = refs[2 * n_blk + 1:]
    b, t = pl.program_id(0), pl.program_id(1)
    gw = NSA_GROUP * NSA_DK
    rows_kv = 2 * NSA_KV_HEADS
    width = SEL_BLOCK * rows_kv

    @pl.when(t == 0)
    def _():
        _flash_init(m_ref, l_ref, acc_ref)

    new_rows = jnp.concatenate([new_ref[0]] * SEL_BLOCK, axis=0)
    xs, scs = [], []
    for h in range(NSA_KV_HEADS):
        qs = _stack_group_q(q_ref[0][:, h * gw:(h + 1) * gw]).astype(BF16)
        row = []
        for u in range(SEL_PER_STEP):
            r = h * SEL_PER_STEP + u
            is_new = idx_ref[b, h, t * SEL_PER_STEP + u] == ns - 1
            x = jnp.where(is_new, new_rows, blocks[r][0].reshape(width, NSA_DK)).astype(BF16)
            xs.append(x)
            row.append(_nt(qs, x) * (NSA_DK ** -0.5) + biases[r][0, 0])
        scs.append(jnp.concatenate(row, axis=1))
    sc = jnp.concatenate(scs, axis=0)
    m_old = m_ref[...]
    m_new = jnp.maximum(m_old, jnp.max(sc, axis=1, keepdims=True))
    alpha = jnp.exp(m_old - m_new)
    p = jnp.exp(sc - m_new)
    l_ref[...] = alpha * l_ref[...] + jnp.sum(p, axis=1, keepdims=True)
    pv = pltpu.roll(p, NSA_KV_HEADS, 1).astype(BF16)
    acc = alpha * acc_ref[...]
    upd = []
    for h in range(NSA_KV_HEADS):
        ph = pv[h * SROWS:(h + 1) * SROWS]
        upd.append(sum(_dot(ph[:, u * width:(u + 1) * width], xs[h * SEL_PER_STEP + u]) for u in range(SEL_PER_STEP)))
    acc_ref[...] = acc + jnp.concatenate(upd, axis=0)
    m_ref[...] = m_new

    @pl.when(t == pl.num_programs(1) - 1)
    def _():
        o_ref[0] = (acc_ref[...] / jnp.maximum(l_ref[...], 1e-30)).reshape(NSA_KV_HEADS, SROWS, NSA_DK)


def _nsa_sample_sel(z3, cache, kv_new, page_table, idx, bias_sel, ns):
    nb, n_pages = page_table.shape
    n_sel = idx.shape[2]
    halves = PAGE_SIZE // SEL_BLOCK
    rows_kv = 2 * NSA_KV_HEADS

    per_step = SEL_PER_STEP
    assert n_sel % per_step == 0

    def blockspec(h, u):
        def index(b, t, pt, ix):
            blk = ix[b, h, t * per_step + u]
            return (pt[b, jnp.minimum(blk // halves, n_pages - 1)], blk % halves, 0, 0)
        return pl.BlockSpec((1, SEL_BLOCK, rows_kv, NSA_DK), index)

    def biasspec(h, u):
        return pl.BlockSpec((1, 1, SROWS, SEL_BLOCK * rows_kv),
                            lambda b, t, pt, ix: (h, ix[b, h, t * per_step + u], 0, 0))

    slots = [(h, u) for h in range(NSA_KV_HEADS) for u in range(per_step)]
    return pl.pallas_call(
        functools.partial(_nsa_sample_sel_kernel, ns=ns),
        grid_spec=pltpu.PrefetchScalarGridSpec(
            num_scalar_prefetch=2,
            grid=(nb, n_sel // per_step),
            in_specs=[pl.BlockSpec((1, 1, NSA_W), lambda b, t, pt, ix: (b, 0, COL_NQ // NSA_W))]
                     + [blockspec(h, u) for h, u in slots]
                     + [pl.BlockSpec((1, rows_kv, NSA_DK), lambda b, t, pt, ix: (b, 0, 0))]
                     + [biasspec(h, u) for h, u in slots],
            out_specs=pl.BlockSpec((1, NSA_KV_HEADS, SROWS, NSA_DK), lambda b, t, pt, ix: (b, 0, 0, 0)),
            scratch_shapes=[pltpu.VMEM((NSA_KV_HEADS * SROWS, 1), F32), pltpu.VMEM((NSA_KV_HEADS * SROWS, 1), F32),
                            pltpu.VMEM((NSA_KV_HEADS * SROWS, NSA_DK), F32)]),
        out_shape=jax.ShapeDtypeStruct((nb, NSA_KV_HEADS, SROWS, NSA_DK), F32),
        compiler_params=_cparams(2),
        name="nsa_sample_sel",
    )(page_table, idx, z3, *([cache] * len(slots)), kv_new, *([bias_sel] * len(slots)))


def _nsa_sample_win_kernel(q_ref, k_ref, v_ref, kn_ref, vn_ref, bias_ref, bnew_ref, o_ref):
    scale = NSA_DK ** -0.5
    q = _stack_group_q(q_ref[0])
    s_buf = _nt(q.astype(BF16), k_ref[0].astype(BF16)) * scale + bias_ref[0]
    s_new = jnp.sum(q * kn_ref[0], axis=1, keepdims=True) * scale + bnew_ref[0]
    m = jnp.maximum(jnp.max(s_buf, axis=1, keepdims=True), s_new)
    p_buf = jnp.exp(s_buf - m)
    p_new = jnp.exp(s_new - m)
    l = jnp.sum(p_buf, axis=1, keepdims=True) + p_new
    acc = _dot(p_buf.astype(BF16), v_ref[0].astype(BF16)) + p_new * vn_ref[0]
    o_ref[0, 0] = acc / jnp.maximum(l, 1e-30)


def _nsa_sample_win(z3, win_buf, bias_win, bias_new):
    nb, nbuf, _ = win_buf.shape
    gw = NSA_GROUP * NSA_DK

    def bufspec(which):
        return pl.BlockSpec((1, nbuf, NSA_DK), lambda b, h: (b, 0, which * NSA_KV_HEADS + h))

    def newspec(which):
        return pl.BlockSpec((1, 1, NSA_DK), lambda b, h: (b, 0, COL_KVW // NSA_DK + which * NSA_KV_HEADS + h))

    return pl.pallas_call(
        _nsa_sample_win_kernel,
        grid=(nb, NSA_KV_HEADS),
        in_specs=[pl.BlockSpec((1, 1, gw), lambda b, h: (b, 0, COL_NQ // gw + h)),
                  bufspec(0), bufspec(1), newspec(0), newspec(1),
                  pl.BlockSpec((1, SROWS, nbuf), lambda b, h: (h, 0, 0)),
                  pl.BlockSpec((1, SROWS, 1), lambda b, h: (h, 0, 0))],
        out_specs=pl.BlockSpec((1, 1, SROWS, NSA_DK), lambda b, h: (b, h, 0, 0)),
        out_shape=jax.ShapeDtypeStruct((nb, NSA_KV_HEADS, SROWS, NSA_DK), F32),
        compiler_params=_cparams(2),
        name="nsa_sample_win",
    )(z3, win_buf, win_buf, z3, z3, bias_win, bias_new)


def _nsa_sample_gate_kernel(oc_ref, os_ref, ow_ref, ng_ref, nsl_ref, o_ref):
    gates = _sigmoid(ng_ref[0])
    nsl = nsl_ref[0]
    outs = []
    for g in range(NSA_GROUP):
        o = (gates[:, 3 * g:3 * g + 1] * oc_ref[0, 0, g:g + 1] + gates[:, 3 * g + 1:3 * g + 2] * os_ref[0, 0, g:g + 1]
             + gates[:, 3 * g + 2:3 * g + 3] * ow_ref[0, 0, g:g + 1])
        x = nsl[:, g * NSA_DK:(g + 1) * NSA_DK]
        outs.append(o * (x * _sigmoid(x)))
    o_ref[0] = jnp.concatenate(outs, axis=1).astype(BF16)


def _nsa_sample_gate(o_cmp, o_sel, o_win, z3b):
    nb = z3b.shape[0]
    gw = NSA_GROUP * NSA_DK
    ospec = pl.BlockSpec((1, 1, SROWS, NSA_DK), lambda b, h: (b, h, 0, 0))
    return pl.pallas_call(
        _nsa_sample_gate_kernel,
        grid=(nb, NSA_KV_HEADS),
        in_specs=[ospec, ospec, ospec,
                  pl.BlockSpec((1, 1, LANE), lambda b, h: (b, 0, COL_NG // LANE + h)),
                  pl.BlockSpec((1, 1, gw), lambda b, h: (b, 0, COL_NSL // gw + h))],
        out_specs=pl.BlockSpec((1, 1, gw), lambda b, h: (b, 0, h)),
        out_shape=jax.ShapeDtypeStruct((nb, 1, NSA_W), BF16),
        compiler_params=_cparams(2),
        name="nsa_sample_gate",
    )(o_cmp, o_sel, o_win, z3b, z3b)


def _mem_heads(q, kv):
    outs = []
    for h in range(MEM_HEADS):
        k = kv[:, h * MEM_DH:(h + 1) * MEM_DH].astype(BF16)
        v = kv[:, MEM_W + h * MEM_DH:MEM_W + (h + 1) * MEM_DH].astype(BF16)
        s = _nt(q[:, h * MEM_DH:(h + 1) * MEM_DH].astype(BF16), k) * (MEM_DH ** -0.5)
        e = jnp.exp(s - jnp.max(s, axis=1, keepdims=True))
        p = e / jnp.sum(e, axis=1, keepdims=True)
        outs.append(_dot(p.astype(BF16), v))
    return jnp.concatenate(outs, axis=1)


def _mem_prompt_kernel(q_ref, kv_ref, o_ref):
    o_ref[...] = _mem_heads(q_ref[:, :MEM_W], kv_ref[...]).astype(BF16)


def _mem_prompt(z, mem_kv, batch, seq, tq):
    nq = seq // tq
    n_mem = mem_kv.shape[0] // batch
    return pl.pallas_call(
        _mem_prompt_kernel,
        grid=(batch, nq),
        in_specs=[pl.BlockSpec((tq, MQ_BLOCK), lambda b, i: (b * nq + i, COL_MQ // MQ_BLOCK)),
                  pl.BlockSpec((n_mem, 2 * MEM_W), lambda b, i: (b, 0))],
        out_specs=pl.BlockSpec((tq, MEM_W), lambda b, i: (b * nq + i, 0)),
        out_shape=jax.ShapeDtypeStruct((batch * seq, MEM_W), BF16),
        compiler_params=_cparams(2),
        name="mem_prompt",
    )(z, mem_kv)


def _mem_sample_kernel(q_ref, kv_ref, o_ref):
    q = jnp.broadcast_to(q_ref[0][:, :MEM_W], (SROWS, MEM_W))
    o_ref[0] = _mem_heads(q, kv_ref[0])[0:1].astype(BF16)


def _mem_sample(z3, mem_kv):
    nb, n_mem, _ = mem_kv.shape
    return pl.pallas_call(
        _mem_sample_kernel,
        grid=(nb,),
        in_specs=[pl.BlockSpec((1, 1, MQ_BLOCK), lambda b: (b, 0, COL_MQ // MQ_BLOCK)),
                  pl.BlockSpec((1, n_mem, 2 * MEM_W), lambda b: (b, 0, 0))],
        out_specs=pl.BlockSpec((1, 1, MEM_W), lambda b: (b, 0, 0)),
        out_shape=jax.ShapeDtypeStruct((nb, 1, MEM_W), BF16),
        compiler_params=_cparams(1),
        name="mem_sample",
    )(z3, mem_kv)


def _merge_kernel(ar_ref, an_ref, am_ref, wr_ref, wn_ref, wm_ref, g0_ref, g1_ref, g2_ref, o_ref):
    merged = (_sigmoid(g0_ref[...]) * _dot(ar_ref[...], wr_ref[...])
              + _sigmoid(g1_ref[...]) * _dot(an_ref[...], wn_ref[...])
              + _sigmoid(g2_ref[...]) * _dot(am_ref[...], wm_ref[...]))
    o_ref[...] = merged.astype(BF16)


def _merge(a_ret, a_nsa, a_mem, w_ret, w_nsa, w_mem, z, tm, tn):
    m = a_ret.shape[0]
    nt = D_MODEL // tn

    def aspec(width):
        return pl.BlockSpec((tm, width), lambda i, j: (i, 0))

    def wspec(width):
        return pl.BlockSpec((width, tn), lambda i, j: (0, j))

    def gspec(branch):
        return pl.BlockSpec((tm, tn), lambda i, j: (i, COL_MG // tn + branch * nt + j))

    return pl.pallas_call(
        _merge_kernel,
        grid=(m // tm, nt),
        in_specs=[aspec(RET_W), aspec(NSA_W), aspec(MEM_W), wspec(RET_W), wspec(NSA_W), wspec(MEM_W),
                  gspec(0), gspec(1), gspec(2)],
        out_specs=pl.BlockSpec((tm, tn), lambda i, j: (i, j)),
        out_shape=jax.ShapeDtypeStruct((m, D_MODEL), BF16),
        compiler_params=_cparams(2),
        name="merge",
    )(a_ret, a_nsa, a_mem, w_ret, w_nsa, w_mem, z, z, z)


def _out_kernel(a_ref, w_ref, x_ref, g_ref, o_ref):
    out = _dot(a_ref[...], w_ref[...])
    y = out * lax.rsqrt(jnp.mean(out * out, axis=-1, keepdims=True) + EPS)
    o_ref[...] = x_ref[...] + y * g_ref[...]


def _out_proj(merged, w_out, x, norm_post, tm):
    m = merged.shape[0]
    return pl.pallas_call(
        _out_kernel,
        grid=(m // tm,),
        in_specs=[pl.BlockSpec((tm, D_MODEL), lambda i: (i, 0)),
                  pl.BlockSpec((D_MODEL, D_MODEL), lambda i: (0, 0)),
                  pl.BlockSpec((tm, D_MODEL), lambda i: (i, 0)),
                  pl.BlockSpec((1, D_MODEL), lambda i: (0, 0))],
        out_specs=pl.BlockSpec((tm, D_MODEL), lambda i: (i, 0)),
        out_shape=jax.ShapeDtypeStruct((m, D_MODEL), F32),
        compiler_params=_cparams(1),
        name="out_proj",
    )(merged, w_out, x, norm_post.reshape(1, D_MODEL))


def _layout_w_ng(w_t):
    per_group = N_BRANCHES * NSA_GROUP
    ng = w_t[PROJ_A:PROJ_A + N_BRANCHES * NSA_HEADS].reshape(NSA_KV_HEADS, per_group, D_MODEL)
    return jnp.pad(ng, ((0, 0), (0, LANE - per_group), (0, 0))).reshape(NG_SLOT, D_MODEL)


def _pick_tile(m, cap):
    t = min(m, cap)
    while m % t:
        t //= 2
    return t


def kernel(x_prompt, x_sample, cache_cmp_kv, cache_sel_kv, cache_win_kv, state_ret, cache_mem_kv, page_table,
           mem_prompt, rel_table, norm_pre, norm_post, norm_mem, w_in, ret_norm, w_ret_up, cmp_pos, w_cmp1,
           w_cmp2, w_nsa_up, w_mem_kv, w_mem_up, w_out):
    batch, seq, _ = x_prompt.shape
    nb = x_sample.shape[0]
    assert x_sample.shape[1] == 1 and norm_pre.shape[0] == 1
    assert seq % TQ == 0 and seq >= WINDOW
    n_pool = cache_cmp_kv.shape[1]
    n_pages = page_table.shape[1]
    past = n_pages * PAGE_SIZE
    n_mem = mem_prompt.shape[1]
    assert n_pages % PAGES_PER_STEP == 0 and cache_win_kv.shape[2] == WINDOW

    w_a = w_in[0].T
    w_ng = _layout_w_ng(w_a)
    kw = CMP_STRIDE * NSA_DK
    w1 = w_cmp1[0].reshape(2, CMP_BLOCK * NSA_DK, NSA_DK).astype(BF16)
    w1a, w1b = w1[:, :kw], w1[:, kw:]
    w1ab = jnp.concatenate([w1a, w1b], axis=2)
    w2 = w_cmp2[0].astype(BF16)
    pos8 = jnp.pad(cmp_pos[0].reshape(2, 1, CMP_BLOCK * NSA_DK), ((0, 0), (0, 7), (0, 0)))
    w_ret = w_ret_up[0].astype(BF16)
    w_nsa = w_nsa_up[0].astype(BF16)
    w_mem = w_mem_up[0].astype(BF16)
    w_o = w_out[0].astype(BF16)

    m_p = batch * seq
    xp = x_prompt.reshape(m_p, D_MODEL)
    hp = _rmsnorm(xp, norm_pre[0], _pick_tile(m_p, 512))
    z, kvc_rows, kvs_rows, kvw_rows = _proj_kv(hp, w_a, _pick_tile(seq, WINDOW), seq)
    zb = _proj_tail(hp, w_a, w_ng, _pick_tile(m_p, 1024))

    a_ret, ret_state_p = _retention_prompt(z, ret_norm[0], batch, seq)

    ncp = max(LANE, -(-(seq // CMP_STRIDE) // LANE) * LANE)
    ac = _cmp_stage1_dense(z, w1ab, batch, seq)
    kcvc = _cmp_stage2(ac, pos8, w1a, w1b, w2, 1)
    if ncp > kcvc.shape[3]:
        kcvc = jnp.pad(kcvc, ((0, 0), (0, 0), (0, 0), (0, ncp - kcvc.shape[3]), (0, 0)))
    bias_d = _bias_by_dist(rel_table, BIAS_DISTS)
    a_nsa = _nsa_prompt(z, zb, kcvc, bias_d, batch, seq)

    hm = _rmsnorm(mem_prompt.reshape(batch * n_mem, D_MODEL), norm_mem[0], _pick_tile(batch * n_mem, 512))
    mem_kv_p = _proj(hm, w_mem_kv[0], _pick_tile(batch * n_mem, 512), PROJ_TN)
    a_mem = _mem_prompt(zb, mem_kv_p, batch, seq, _pick_tile(seq, 512))

    merged = _merge(a_ret, a_nsa, a_mem, w_ret, w_nsa, w_mem, zb, _pick_tile(m_p, 512), 512)
    y_p = _out_proj(merged, w_o, xp, norm_post[0], _pick_tile(m_p, 256)).reshape(batch, seq, D_MODEL)

    kv_shape = (1, batch, seq, 2, NSA_KV_HEADS, NSA_DK)
    new_cmp_p = kvc_rows.reshape(kv_shape)
    new_sel_p = kvs_rows.reshape(kv_shape)
    new_win_p = kvw_rows.reshape(1, batch, WINDOW, 2, NSA_KV_HEADS, NSA_DK)
    new_ret_p = ret_state_p[None]
    new_mem_p = mem_kv_p.reshape(1, batch, n_mem, 2, MEM_HEADS, MEM_DH)

    xs = x_sample.reshape(nb, D_MODEL)
    hs = _rmsnorm(xs, norm_pre[0], nb)
    zs = _proj(hs, w_a, nb, PROJ_TN, PROJ_A, transposed=True)
    zsb = _proj_tail(hs, w_a, w_ng, nb)
    z3 = zs.reshape(nb, 1, PROJ_A)
    z3b = zsb.reshape(nb, 1, PROJ_B)

    a_ret_s, ret_state_s = _retention_sample(z3, state_ret[0], ret_norm[0], past)

    cache_c = cache_cmp_kv[0].reshape(n_pool, PAGE_SIZE // CMP_STRIDE, CMP_STRIDE, 2 * NSA_KV_HEADS, NSA_DK)
    cache_s = cache_sel_kv[0].reshape(n_pool, PAGE_SIZE, 2 * NSA_KV_HEADS, NSA_DK)
    ac_s = _cmp_stage1_paged(cache_c, page_table, w1ab)
    kcvc_s = _cmp_stage2(ac_s, pos8, w1a, w1b, w2, NSA_KV_HEADS)
    ncs = past // CMP_STRIDE
    kcvc_s = kcvc_s.reshape(nb, 2, ncs, KV_W)
    ns_s = past // SEL_BLOCK + 1
    nsp_s = -(-ns_s // LANE) * LANE
    assert past >= WINDOW and past >= REL_MAX_DIST and BIAS_DISTS > WINDOW
    far_s = bias_d[..., REL_MAX_DIST:REL_MAX_DIST + 1]
    hg = (NSA_KV_HEADS, NSA_GROUP)
    n_valid = (past - (CMP_BLOCK - 1)) // CMP_STRIDE + 1
    strided = bias_d[..., (past - (CMP_BLOCK - 1)) % CMP_STRIDE::CMP_STRIDE]
    n_tab = strided.shape[-1]
    assert n_valid >= n_tab and n_tab * CMP_STRIDE > REL_MAX_DIST + CMP_STRIDE and ncs >= n_valid
    bias_cs = jnp.concatenate([jnp.broadcast_to(far_s, hg + (n_valid - n_tab,)), strided[..., ::-1],
                               jnp.full(hg + (ncs - n_valid,), NEG_INF, F32)], axis=-1)
    bias_cs = _pad_group_rows(bias_cs, 1)
    nblk = jnp.arange(ncs)[:, None]
    sblk = jnp.arange(nsp_s)[None, :]
    ov_s = ((nblk >= 4 * sblk - 1) & (nblk <= 4 * sblk + 3)).astype(BF16)
    o_cmp_s, idx_s = _nsa_sample_cmp(z3, kcvc_s, bias_cs, ov_s, ns_s)
    n_sel = min(SEL_TOPK, ns_s)
    idx = idx_s[:, :, 0, :n_sel]

    hg = (NSA_KV_HEADS, NSA_GROUP)
    n_key = ns_s * SEL_BLOCK
    bias_sel = jnp.concatenate([jnp.broadcast_to(far_s, hg + (past + 1 - REL_MAX_DIST,)),
                                bias_d[..., :REL_MAX_DIST][..., ::-1],
                                jnp.full(hg + (n_key - past - 1,), NEG_INF, F32)], axis=-1)
    bias_sel = _pad_group_rows(bias_sel.reshape(hg + (ns_s, SEL_BLOCK)).transpose(0, 2, 1, 3), 2)
    own_k = jnp.arange(2 * NSA_KV_HEADS)[None, :] == jnp.arange(NSA_KV_HEADS)[:, None]
    bias_sel = jnp.where(own_k[:, None, None, None, :], bias_sel[..., None], NEG_INF)
    bias_sel = bias_sel.reshape(NSA_KV_HEADS, ns_s, SROWS, SEL_BLOCK * 2 * NSA_KV_HEADS)
    kvs_new = zs[:, COL_KVS:COL_KVS + 2 * KV_W].reshape(nb, 2 * NSA_KV_HEADS, NSA_DK)
    o_sel_s = _nsa_sample_sel(z3, cache_s, kvs_new, page_table, idx, bias_sel, ns_s)

    win_buf = cache_win_kv[0].reshape(nb, WINDOW, 2 * KV_W)
    bias_w = jnp.concatenate([jnp.full(hg + (1,), NEG_INF, F32), bias_d[..., 1:WINDOW][..., ::-1]], axis=-1)
    o_win_s = _nsa_sample_win(z3, win_buf, _pad_group_rows(bias_w, 1), _pad_group_rows(bias_d[..., 0:1], 1))
    a_nsa_s = _nsa_sample_gate(o_cmp_s, o_sel_s, o_win_s, z3b)

    mem_kv_s = cache_mem_kv[0].reshape(nb, n_mem, 2 * MEM_W)
    a_mem_s = _mem_sample(z3b, mem_kv_s)

    merged_s = _merge(a_ret_s.reshape(nb, RET_W), a_nsa_s.reshape(nb, NSA_W), a_mem_s.reshape(nb, MEM_W),
                      w_ret, w_nsa, w_mem, zsb, nb, 512)
    y_s = _out_proj(merged_s, w_o, xs, norm_post[0], nb).reshape(nb, 1, D_MODEL)

    kvs_shape = (1, nb, 1, 2, NSA_KV_HEADS, NSA_DK)
    new_cmp_s = zs[:, COL_KVC:COL_KVC + 2 * KV_W].reshape(kvs_shape)
    new_sel_s = zs[:, COL_KVS:COL_KVS + 2 * KV_W].reshape(kvs_shape)
    kvw_s = zs[:, COL_KVW:COL_KVW + 2 * KV_W].reshape(nb, 1, 2, NSA_KV_HEADS, NSA_DK)
    new_win_s = jnp.concatenate([cache_win_kv[0][:, 1:], kvw_s], axis=1)[None]
    new_ret_s = ret_state_s[None]

    return (y_p, y_s, new_cmp_p, new_sel_p, new_win_p, new_ret_p, new_mem_p,
            new_cmp_s, new_sel_s, new_win_s, new_ret_s)
```

```python
import functools
import math

import jax
import jax.numpy as jnp
from jax import lax
from jax.experimental import pallas as pl
from jax.experimental.pallas import tpu as pltpu

F32 = jnp.float32
BF16 = jnp.bfloat16

D_MODEL = 2048
PAGE_SIZE = 128
RET_HEADS = 8
RET_DK = 256
RET_DV = 256
RET_CHUNK = 128
ROPE_BASE = 10000.0
NSA_HEADS = 16
NSA_KV_HEADS = 4
NSA_GROUP = NSA_HEADS // NSA_KV_HEADS
NSA_DK = 128
CMP_BLOCK = 32
CMP_STRIDE = 16
SEL_BLOCK = 64
SEL_TOPK = 16
WINDOW = 512
MEM_HEADS = 4
MEM_DH = 384
REL_BUCKETS = 32
REL_MAX_EXACT = 16
REL_MAX_DIST = 128
N_BRANCHES = 3
EPS = 1e-6
NEG_INF = -1e30
FORCE_SCORE = 1e4

RET_W = RET_HEADS * RET_DV
NSA_W = NSA_HEADS * NSA_DK
KV_W = NSA_KV_HEADS * NSA_DK
MEM_W = MEM_HEADS * MEM_DH

COL_RQ = 0
COL_RK = COL_RQ + RET_HEADS * RET_DK
COL_RV = COL_RK + RET_HEADS * RET_DK
COL_RG = COL_RV + RET_W
COL_NQ = COL_RG + RET_W
COL_KVC = COL_NQ + NSA_W
COL_KVS = COL_KVC + 2 * KV_W
COL_KVW = COL_KVS + 2 * KV_W
PROJ_A = COL_KVW + 2 * KV_W
COL_NSL = 0
COL_MG = COL_NSL + NSA_W
COL_MQ = COL_MG + N_BRANCHES * D_MODEL
COL_NG = COL_MQ + MEM_W
NG_SLOT = NSA_KV_HEADS * 128
PROJ_B = COL_NG + NG_SLOT
MQ_BLOCK = MEM_W + NG_SLOT

LOG2E = math.log2(math.e)
LANE = 128
TQ = 256
TK = 256
NSA_ROWS = NSA_GROUP * TQ
BIAS_DISTS = 1024
FAR_GROUP = 4
RET_STEP_CHUNKS = 8
PROJ_TN = 1024
VMEM_LIMIT = 56 * 1024 * 1024


def _cparams(n_axes):
    return pltpu.CompilerParams(dimension_semantics=("arbitrary",) * n_axes, vmem_limit_bytes=VMEM_LIMIT)


def _nt(a, b):
    return lax.dot_general(a, b, (((1,), (1,)), ((), ())), preferred_element_type=F32)


def _dot(a, b):
    return jnp.dot(a, b, preferred_element_type=F32)


def _sigmoid(x):
    return 1.0 / (1.0 + jnp.exp(-x))


def _iota(shape, dim):
    return lax.broadcasted_iota(jnp.int32, shape, dim)


def _rmsnorm_kernel(x_ref, g_ref, h_ref):
    x = x_ref[...]
    ms = jnp.mean(x * x, axis=-1, keepdims=True)
    h_ref[...] = ((x * lax.rsqrt(ms + EPS)) * g_ref[...]).astype(BF16)


def _rmsnorm(x, g, tm):
    m, k = x.shape
    return pl.pallas_call(
        _rmsnorm_kernel,
        grid=(m // tm,),
        in_specs=[pl.BlockSpec((tm, k), lambda i: (i, 0)), pl.BlockSpec((1, k), lambda i: (0, 0))],
        out_specs=pl.BlockSpec((tm, k), lambda i: (i, 0)),
        out_shape=jax.ShapeDtypeStruct((m, k), BF16),
        compiler_params=_cparams(1),
        name="rmsnorm",
    )(x, g.reshape(1, k))


def _proj_kernel(h_ref, w_ref, o_ref, wb_ref, *, transposed):
    @pl.when(pl.program_id(1) == 0)
    def _():
        if transposed:
            _store_transposed(wb_ref, w_ref, wb_ref.shape[1])
        else:
            wb_ref[...] = w_ref[...].astype(BF16)

    o_ref[...] = _dot(h_ref[...], wb_ref[...])


def _proj(h, w, tm, tn, n=None, transposed=False):
    m, k = h.shape
    n = w.shape[0 if transposed else 1] if n is None else n
    wspec = pl.BlockSpec((tn, k), lambda j, i: (j, 0)) if transposed else pl.BlockSpec((k, tn), lambda j, i: (0, j))
    return pl.pallas_call(
        functools.partial(_proj_kernel, transposed=transposed),
        grid=(n // tn, m // tm),
        in_specs=[pl.BlockSpec((tm, k), lambda j, i: (i, 0)), wspec],
        out_specs=pl.BlockSpec((tm, tn), lambda j, i: (i, j)),
        out_shape=jax.ShapeDtypeStruct((m, n), F32),
        scratch_shapes=[pltpu.VMEM((k, tn), BF16)],
        compiler_params=_cparams(2),
        name="proj",
    )(h, w)


def _proj_kv_kernel(h_ref, w_ref, o_ref, oc_ref, os_ref, ow_ref, wb_ref, *, tiles_per_batch):
    j, i = pl.program_id(0), pl.program_id(1)
    n_tiles = pl.num_programs(0)

    @pl.when(i == 0)
    def _():
        _store_transposed(wb_ref, w_ref, wb_ref.shape[1])

    res = _dot(h_ref[...], wb_ref[...])
    o_ref[...] = res
    tm, tn = res.shape
    rows_kv = tn // NSA_DK

    @pl.when(j == n_tiles - 3)
    def _():
        oc_ref[...] = res.reshape(tm * rows_kv, NSA_DK)

    @pl.when(j == n_tiles - 2)
    def _():
        os_ref[...] = res.reshape(tm * rows_kv, NSA_DK)

    @pl.when((j == n_tiles - 1) & (i % tiles_per_batch == tiles_per_batch - 1))
    def _():
        ow_ref[...] = res[tm - WINDOW:, :].reshape(WINDOW * rows_kv, NSA_DK)


def _proj_kv(h, w_t, tm, seq):
    m, k = h.shape
    tn = 2 * KV_W
    assert PROJ_A % tn == 0 and COL_KVC == PROJ_A - 3 * tn and seq % tm == 0 and tm >= WINDOW
    n_j, n_i = PROJ_A // tn, m // tm
    tiles_per_batch = seq // tm
    rows_kv = tn // NSA_DK

    def kv_rows(tile):
        return lambda j, i: (jnp.where(j < tile, 0, jnp.where(j == tile, i, n_i - 1)), 0)

    return pl.pallas_call(
        functools.partial(_proj_kv_kernel, tiles_per_batch=tiles_per_batch),
        grid=(n_j, n_i),
        in_specs=[pl.BlockSpec((tm, k), lambda j, i: (i, 0)),
                  pl.BlockSpec((tn, k), lambda j, i: (j, 0))],
        out_specs=[pl.BlockSpec((tm, tn), lambda j, i: (i, j)),
                   pl.BlockSpec((tm * rows_kv, NSA_DK), kv_rows(n_j - 3)),
                   pl.BlockSpec((tm * rows_kv, NSA_DK), kv_rows(n_j - 2)),
                   pl.BlockSpec((WINDOW * rows_kv, NSA_DK),
                                lambda j, i: (jnp.where(j < n_j - 1, 0, i // tiles_per_batch), 0))],
        out_shape=[jax.ShapeDtypeStruct((m, PROJ_A), F32),
                   jax.ShapeDtypeStruct((m * rows_kv, NSA_DK), F32),
                   jax.ShapeDtypeStruct((m * rows_kv, NSA_DK), F32),
                   jax.ShapeDtypeStruct((m // seq * WINDOW * rows_kv, NSA_DK), F32)],
        scratch_shapes=[pltpu.VMEM((k, tn), BF16)],
        compiler_params=_cparams(2),
        name="proj_kv",
    )(h, w_t)


TAIL_TN = 1024


def _store_transposed(dst_ref, src_ref, rows, chunk=256):
    for r0 in range(0, rows, chunk):
        dst_ref[:, r0:r0 + chunk] = src_ref[r0:r0 + chunk, :].T.astype(BF16)


def _proj_tail_kernel(h_ref, w_ref, wng_ref, o_ref, wb_ref, *, last_rows):
    tn = wb_ref.shape[1]
    is_last = pl.program_id(0) == pl.num_programs(0) - 1

    @pl.when((pl.program_id(1) == 0) & jnp.logical_not(is_last))
    def _():
        _store_transposed(wb_ref, w_ref, tn)

    @pl.when((pl.program_id(1) == 0) & is_last)
    def _():
        _store_transposed(wb_ref, w_ref, last_rows)
        wb_ref[:, last_rows:] = wng_ref[...].T.astype(BF16)

    o_ref[...] = _dot(h_ref[...], wb_ref[...])


def _proj_tail(h, w_t, w_ng, tm):
    m, k = h.shape
    tn = TAIL_TN
    row_ng = PROJ_A
    row_nsl = row_ng + N_BRANCHES * NSA_HEADS
    row_mq = row_nsl + NSA_W
    row_mg = row_mq + MEM_W
    assert row_mg + N_BRANCHES * D_MODEL == w_t.shape[0] and w_ng.shape == (NG_SLOT, k)
    assert COL_NSL == 0 and COL_MG % tn == 0 and COL_MQ % tn == 0 and COL_NG + NG_SLOT == PROJ_B == COL_MQ + 2 * tn
    assert row_mq % 16 == 0 and row_nsl % 16 == 0 and row_mg % 16 == 0
    t_mg, t_mq = COL_MG // tn, COL_MQ // tn

    def w_row(j, i):
        row = jnp.where(j < t_mg, row_nsl + j * tn,
                        jnp.where(j < t_mq, row_mg + (j - t_mg) * tn, row_mq + (j - t_mq) * tn))
        return (pl.multiple_of(row, 16), 0)

    return pl.pallas_call(
        functools.partial(_proj_tail_kernel, last_rows=COL_NG - COL_MQ - tn),
        grid=(PROJ_B // tn, m // tm),
        in_specs=[pl.BlockSpec((tm, k), lambda j, i: (i, 0)),
                  pl.BlockSpec((pl.Element(tn), pl.Element(k)), w_row),
                  pl.BlockSpec((NG_SLOT, k), lambda j, i: (0, 0))],
        out_specs=pl.BlockSpec((tm, tn), lambda j, i: (i, j)),
        out_shape=jax.ShapeDtypeStruct((m, PROJ_B), F32),
        scratch_shapes=[pltpu.VMEM((k, tn), BF16)],
        compiler_params=_cparams(2),
        name="proj_tail",
    )(h, w_t, w_ng)


def _rope_rows(x, cos, sin):
    half = x.shape[-1] // 2
    x1, x2 = x[:, :half], x[:, half:]
    return jnp.concatenate([x1 * cos - x2 * sin, x1 * sin + x2 * cos], axis=-1)


def _head_norm_gate(o, gnorm, rg):
    oc = o - jnp.mean(o, axis=-1, keepdims=True)
    y = oc * lax.rsqrt(jnp.mean(oc * oc, axis=-1, keepdims=True) + EPS) * gnorm
    return y * (rg * _sigmoid(rg))


def _ret_prompt_kernel(q_ref, k_ref, v_ref, rg_ref, cos_ref, sin_ref, dmat_ref, xi_ref, zeta_ref, gc_ref,
                       gn_ref, a_ref, s_ref):
    @pl.when(pl.program_id(2) == 0)
    def _():
        s_ref[...] = jnp.zeros_like(s_ref)

    c = RET_CHUNK
    for t in range(q_ref.shape[0] // c):
        rows = slice(t * c, (t + 1) * c)
        cos, sin = cos_ref[rows, :], sin_ref[rows, :]
        q = _rope_rows(q_ref[rows, :], cos, sin)
        k = _rope_rows(k_ref[rows, :], cos, sin) * (RET_DK ** -0.5)
        qb, vb = q.astype(BF16), v_ref[rows, :].astype(BF16)
        state = s_ref[0, 0]
        inner = _nt(qb, k.astype(BF16)) * dmat_ref[0]
        o = _dot(inner.astype(BF16), vb) + _dot(qb, state.astype(BF16)) * xi_ref[0]
        kz_t = (k * zeta_ref[0]).T.astype(BF16)
        s_ref[0, 0] = state * gc_ref[0] + _dot(kz_t, vb)
        a_ref[rows, :] = _head_norm_gate(o, gn_ref[...], rg_ref[rows, :]).astype(BF16)


def _decay_tables(chunk):
    log_g = jnp.log1p(-jnp.exp2(-5.0 - jnp.arange(RET_HEADS, dtype=F32)))
    i = jnp.arange(chunk, dtype=F32)
    diff = i[:, None] - i[None, :]
    dmat = jnp.where(diff >= 0, jnp.exp(log_g[:, None, None] * jnp.maximum(diff, 0.0)), 0.0)
    xi = jnp.exp(log_g[:, None] * (i[None, :] + 1.0))[:, :, None]
    zeta = jnp.exp(log_g[:, None] * (chunk - 1.0 - i[None, :]))[:, :, None]
    g_chunk = jnp.exp(log_g * chunk)[:, None, None]
    return dmat, xi, zeta, g_chunk


def _rope_tables(pos):
    half = RET_DK // 2
    freq = jnp.power(ROPE_BASE, -jnp.arange(half, dtype=F32) / half)
    ang = pos.astype(F32)[:, None] * freq[None, :]
    return jnp.cos(ang), jnp.sin(ang)


def _retention_prompt(z, ret_norm, batch, seq):
    c = RET_CHUNK
    rows = _pick_tile(seq, RET_STEP_CHUNKS * c)
    nc = seq // rows
    dmat, xi, zeta, g_chunk = _decay_tables(c)
    cos, sin = _rope_tables(jnp.arange(seq))
    hb = RET_DK

    def zspec(col0):
        return pl.BlockSpec((rows, hb), lambda b, h, t, col0=col0: (b * nc + t, col0 // hb + h))

    per_head = lambda shape: pl.BlockSpec((1,) + shape, lambda b, h, t: (h, 0, 0))
    return pl.pallas_call(
        _ret_prompt_kernel,
        grid=(batch, RET_HEADS, nc),
        in_specs=[zspec(COL_RQ), zspec(COL_RK), zspec(COL_RV), zspec(COL_RG),
                  pl.BlockSpec((rows, hb // 2), lambda b, h, t: (t, 0)),
                  pl.BlockSpec((rows, hb // 2), lambda b, h, t: (t, 0)),
                  per_head((c, c)), per_head((c, 1)), per_head((c, 1)), per_head((1, 1)),
                  pl.BlockSpec((1, hb), lambda b, h, t: (0, h))],
        out_specs=[pl.BlockSpec((rows, hb), lambda b, h, t: (b * nc + t, h)),
                   pl.BlockSpec((1, 1, RET_DK, RET_DV), lambda b, h, t: (b, h, 0, 0))],
        out_shape=[jax.ShapeDtypeStruct((batch * seq, RET_W), BF16),
                   jax.ShapeDtypeStruct((batch, RET_HEADS, RET_DK, RET_DV), F32)],
        compiler_params=_cparams(3),
        name="retention_prompt",
    )(z, z, z, z, cos, sin, dmat, xi, zeta, g_chunk, ret_norm.reshape(1, RET_W))


def _column_of(row):
    n = row.shape[1]
    eye = _iota((n, n), 0) == _iota((n, n), 1)
    return jnp.sum(jnp.where(eye, jnp.broadcast_to(row, (n, n)), 0.0), axis=1, keepdims=True)


def _ret_sample_kernel(q_ref, k_ref, v_ref, rg_ref, cos_ref, sin_ref, gam_ref, gn_ref, s_ref, a_ref, so_ref):
    cos, sin = cos_ref[...], sin_ref[...]
    outs = []
    for h in range(RET_HEADS):
        cols = slice(h * RET_DK, (h + 1) * RET_DK)
        q = _rope_rows(q_ref[0][:, cols], cos, sin)
        k = _rope_rows(k_ref[0][:, cols], cos, sin) * (RET_DK ** -0.5)
        v = v_ref[0][:, cols]
        state = s_ref[0, h]
        gamma = gam_ref[h]
        qk = jnp.sum(q * k, axis=-1, keepdims=True)
        o = qk * v + jnp.sum(_column_of(q) * state, axis=0, keepdims=True) * gamma
        so_ref[0, h] = state * gamma + _column_of(k) * v
        outs.append(_head_norm_gate(o, gn_ref[:, cols], rg_ref[0][:, cols]))
    a_ref[0] = jnp.concatenate(outs, axis=1).astype(BF16)


def _retention_sample(z3, state, ret_norm, pos):
    nb = z3.shape[0]
    cos, sin = _rope_tables(jnp.full((1,), pos))
    gamma = jnp.exp(jnp.log1p(-jnp.exp2(-5.0 - jnp.arange(RET_HEADS, dtype=F32))))[:, None, None]
    hb = RET_DK
    assert RET_HEADS * RET_DK == RET_W

    def zspec(col0):
        return pl.BlockSpec((1, 1, RET_W), lambda b, col0=col0: (b, 0, col0 // RET_W))

    st_spec = pl.BlockSpec((1, RET_HEADS, RET_DK, RET_DV), lambda b: (b, 0, 0, 0))
    return pl.pallas_call(
        _ret_sample_kernel,
        grid=(nb,),
        in_specs=[zspec(COL_RQ), zspec(COL_RK), zspec(COL_RV), zspec(COL_RG),
                  pl.BlockSpec((1, hb // 2), lambda b: (0, 0)),
                  pl.BlockSpec((1, hb // 2), lambda b: (0, 0)),
                  pl.BlockSpec((RET_HEADS, 1, 1), lambda b: (0, 0, 0)),
                  pl.BlockSpec((1, RET_W), lambda b: (0, 0)),
                  st_spec],
        out_specs=[pl.BlockSpec((1, 1, RET_W), lambda b: (b, 0, 0)), st_spec],
        out_shape=[jax.ShapeDtypeStruct((nb, 1, RET_W), BF16),
                   jax.ShapeDtypeStruct(state.shape, F32)],
        compiler_params=_cparams(1),
        name="retention_sample",
    )(z3, z3, z3, z3, cos, sin, gamma, ret_norm.reshape(1, RET_W), state)


def _half_rows(ref_slice_fn, n_half):
    return jnp.concatenate([ref_slice_fn(p) for p in range(CMP_STRIDE)], axis=1)


def _cmp_stage1_dense_kernel(x_ref, w_ref, o_ref):
    nh = o_ref.shape[3]
    x = _half_rows(lambda p: x_ref[pl.ds(p, nh, stride=CMP_STRIDE), :], nh).astype(BF16)
    o_ref[0, 0, 0] = _dot(x, w_ref[0])


def _cmp_stage1_dense(z, w1ab, batch, seq):
    nh = seq // CMP_STRIDE
    return pl.pallas_call(
        _cmp_stage1_dense_kernel,
        grid=(batch, 2, NSA_KV_HEADS),
        in_specs=[pl.BlockSpec((seq, NSA_DK), lambda b, kv, h: (b, COL_KVC // NSA_DK + kv * NSA_KV_HEADS + h)),
                  pl.BlockSpec((1, CMP_STRIDE * NSA_DK, 2 * NSA_DK), lambda b, kv, h: (kv, 0, 0))],
        out_specs=pl.BlockSpec((1, 1, 1, nh, 2 * NSA_DK), lambda b, kv, h: (b, kv, h, 0, 0)),
        out_shape=jax.ShapeDtypeStruct((batch, 2, NSA_KV_HEADS, nh, 2 * NSA_DK), F32),
        compiler_params=_cparams(3),
        name="cmp_stage1_dense",
    )(z, w1ab)


PAGES_PER_STEP = 16


def _cmp_stage1_paged_kernel(pt_ref, *refs):
    pages, (w_ref, o_ref) = refs[:PAGES_PER_STEP], refs[PAGES_PER_STEP:]
    hp = PAGE_SIZE // CMP_STRIDE
    top = _iota((2 * NSA_KV_HEADS, NSA_DK), 0) < NSA_KV_HEADS
    cols = [[], []]
    for p in range(CMP_STRIDE):
        tiles = [[], []]
        for pg in pages:
            xp = pg[0, :, p]
            for n in range(0, hp, 2):
                a, b = xp[n], xp[n + 1]
                tiles[0].append(jnp.where(top, a, pltpu.roll(b, NSA_KV_HEADS, 0)))
                tiles[1].append(jnp.where(top, pltpu.roll(a, NSA_KV_HEADS, 0), b))
        for kv in range(2):
            cols[kv].append(jnp.concatenate(tiles[kv], axis=0))
    for kv in range(2):
        x = jnp.concatenate(cols[kv], axis=1).astype(BF16)
        o_ref[0, kv, 0] = _dot(x, w_ref[kv])


def _cmp_stage1_paged(cache, page_table, w1ab):
    nb, n_pages = page_table.shape
    hp = PAGE_SIZE // CMP_STRIDE
    steps = n_pages // PAGES_PER_STEP
    rows = PAGES_PER_STEP * hp * NSA_KV_HEADS

    def page_spec(j):
        return pl.BlockSpec((1, hp, CMP_STRIDE, 2 * NSA_KV_HEADS, NSA_DK),
                            lambda b, s, pt, j=j: (pt[b, s * PAGES_PER_STEP + j], 0, 0, 0, 0))

    wspec = pl.BlockSpec((2, CMP_STRIDE * NSA_DK, 2 * NSA_DK), lambda b, s, pt: (0, 0, 0))
    return pl.pallas_call(
        _cmp_stage1_paged_kernel,
        grid_spec=pltpu.PrefetchScalarGridSpec(
            num_scalar_prefetch=1,
            grid=(nb, steps),
            in_specs=[page_spec(j) for j in range(PAGES_PER_STEP)] + [wspec],
            out_specs=pl.BlockSpec((1, 2, 1, rows, 2 * NSA_DK), lambda b, s, pt: (b, 0, 0, s, 0))),
        out_shape=jax.ShapeDtypeStruct((nb, 2, 1, steps * rows, 2 * NSA_DK), F32),
        compiler_params=_cparams(2),
        name="cmp_stage1_paged",
    )(page_table, *([cache] * PAGES_PER_STEP), w1ab)


def _cmp_stage2_kernel(ac_ref, pos_ref, w1a_ref, w1b_ref, w2_ref, o_ref, *, shift):
    ac = ac_ref[0, 0, 0]
    nh = ac.shape[0]
    pos = pos_ref[0].astype(BF16)
    kw = CMP_STRIDE * NSA_DK
    pe = _dot(pos[:, :kw], w1a_ref[0]) + _dot(pos[:, kw:], w1b_ref[0])
    pre = ac[:, :NSA_DK] + pltpu.roll(ac[:, NSA_DK:], nh - shift, 0) + pe[0:1]
    gelu = 0.5 * pre * (1.0 + jnp.tanh(math.sqrt(2.0 / math.pi) * (pre + 0.044715 * (pre * pre * pre))))
    o_ref[0, 0, 0] = _dot(gelu.astype(BF16), w2_ref[0]).astype(BF16)


def _cmp_stage2(ac, pos8, w1a, w1b, w2, shift):
    nb, _, groups, nh, _ = ac.shape
    kw = CMP_STRIDE * NSA_DK
    return pl.pallas_call(
        functools.partial(_cmp_stage2_kernel, shift=shift),
        grid=(nb, 2, groups),
        in_specs=[pl.BlockSpec((1, 1, 1, nh, 2 * NSA_DK), lambda b, kv, h: (b, kv, h, 0, 0)),
                  pl.BlockSpec((1, 8, 2 * kw), lambda b, kv, h: (kv, 0, 0)),
                  pl.BlockSpec((1, kw, NSA_DK), lambda b, kv, h: (kv, 0, 0)),
                  pl.BlockSpec((1, kw, NSA_DK), lambda b, kv, h: (kv, 0, 0)),
                  pl.BlockSpec((1, NSA_DK, NSA_DK), lambda b, kv, h: (kv, 0, 0))],
        out_specs=pl.BlockSpec((1, 1, 1, nh, NSA_DK), lambda b, kv, h: (b, kv, h, 0, 0)),
        out_shape=jax.ShapeDtypeStruct((nb, 2, groups, nh, NSA_DK), BF16),
        compiler_params=_cparams(3),
        name="cmp_stage2",
    )(ac, pos8, w1a, w1b, w2)


def _rel_bucket(dist):
    n = jnp.maximum(dist, 0)
    nf = jnp.maximum(n, 1).astype(F32)
    scale = (REL_BUCKETS - REL_MAX_EXACT) / math.log(REL_MAX_DIST / REL_MAX_EXACT)
    large = REL_MAX_EXACT + (jnp.log(nf / REL_MAX_EXACT) * scale).astype(jnp.int32)
    large = jnp.minimum(large, REL_BUCKETS - 1)
    return jnp.where(n < REL_MAX_EXACT, n, large)


def _bias_by_dist(rel_table, n):
    tab = rel_table[_rel_bucket(jnp.arange(n))]
    return tab.T.reshape(NSA_KV_HEADS, NSA_GROUP, n)


def _pad_group_rows(t, axis):
    first = lax.slice_in_dim(t, 0, 1, axis=axis)
    return jnp.concatenate([t] + [first] * (SROWS - NSA_GROUP), axis=axis)


def _flash_init(m_ref, l_ref, acc_ref):
    m_ref[...] = jnp.full(m_ref.shape, NEG_INF, F32)
    l_ref[...] = jnp.zeros(l_ref.shape, F32)
    acc_ref[...] = jnp.zeros(acc_ref.shape, F32)


def _flash_step(s, v, m_ref, l_ref, acc_ref):
    m_old = m_ref[...]
    m_new = jnp.maximum(m_old, jnp.max(s, axis=1, keepdims=True))
    alpha = jnp.exp(m_old - m_new)
    p = jnp.exp(s - m_new)
    l_ref[...] = alpha * l_ref[...] + jnp.sum(p, axis=1, keepdims=True)
    acc_ref[...] = alpha * acc_ref[...] + _dot(p.astype(BF16), v)
    m_ref[...] = m_new


def _flash_result(l_ref, acc_ref):
    return acc_ref[...] / jnp.maximum(l_ref[...], 1e-30)


def _select_blocks(imp_t, q0, ns):
    shape = imp_t.shape
    blk = _iota(shape, 0)
    qpos = q0 + _iota(shape, 1)
    cur = qpos >> 6
    valid = blk * SEL_BLOCK <= qpos
    forced = (blk == 0) | (blk == cur) | (blk == cur - 1)
    imp_t = jnp.where(valid, jnp.where(forced, FORCE_SCORE, imp_t), NEG_INF)
    rank = jnp.zeros(shape, F32)
    for other in range(ns):
        row = imp_t[other:other + 1, :]
        ahead = (row > imp_t) | ((row == imp_t) & (blk > other))
        rank = rank + jnp.where(ahead, 1.0, 0.0)
    return jnp.where((rank < SEL_TOPK) & valid, 1.0, 0.0)


def _nsa_prompt_kernel(q_ref, ks_ref, vs_ref, kw_ref, vw_ref, kc_ref, vct_ref, wd_ref, cfar_ref, basec_ref,
                       ovt_ref, ng_ref, nsl_ref, o_ref, m_ref, l_ref, acc_ref, vst_ref, vwt_ref, tz_ref, ow_ref, *, ns):
    i = pl.program_id(2)
    q0 = i * TQ
    qall = q_ref[...] * ((NSA_DK ** -0.5) * LOG2E)
    qt = jnp.concatenate([qall[:, g * NSA_DK:(g + 1) * NSA_DK].T for g in range(NSA_GROUP)], axis=1).astype(BF16)
    c_loc = _iota((TK, NSA_ROWS), 0)
    r_loc = _iota((TK, NSA_ROWS), 1) & (TQ - 1)

    @pl.when(i == 0)
    def _():
        for kt in range(vst_ref.shape[0]):
            vst_ref[kt] = vs_ref[kt * TK:(kt + 1) * TK, :].T.astype(BF16)
            vwt_ref[kt] = vw_ref[kt * TK:(kt + 1) * TK, :].T.astype(BF16)
        below = _iota((TK, TQ), 1) >= _iota((TK, TQ), 0)
        for g in range(NSA_GROUP):
            lo = pltpu.roll(jnp.broadcast_to(wd_ref[0, 2 * g:2 * g + 1, :], (TK, TQ)), 0, 1, stride=1, stride_axis=0)
            hi = pltpu.roll(jnp.broadcast_to(wd_ref[0, 2 * g + 1:2 * g + 2, :], (TK, TQ)), 0, 1, stride=1, stride_axis=0)
            tz_ref[0, :, g * TQ:(g + 1) * TQ] = lo
            tz_ref[1, :, g * TQ:(g + 1) * TQ] = jnp.where(below, hi, lo)

    ncp = kc_ref.shape[3]
    shift = (TQ // CMP_STRIDE) * i
    bias_c = basec_ref[0, pl.ds(pl.multiple_of(ncp - shift, TQ // CMP_STRIDE), ncp), :]
    s = _dot(kc_ref[0, 0, 0], qt) + bias_c
    m = jnp.max(s, axis=0, keepdims=True)
    e = jnp.exp2(s - m)
    inv = jnp.where(m > 0.5 * NEG_INF, 1.0 / jnp.maximum(jnp.sum(e, axis=0, keepdims=True), 1e-30), 0.0)
    p = e * inv
    o_cmp = _dot(vct_ref[0, 0], p.astype(BF16))

    psum = p[:, 0:TQ] + p[:, TQ:2 * TQ] + p[:, 2 * TQ:3 * TQ] + p[:, 3 * TQ:4 * TQ]
    hi = psum.astype(BF16)
    lo = (psum - hi.astype(F32)).astype(BF16)
    ovt = ovt_ref[...]
    imp_t = _dot(ovt, hi) + _dot(ovt, lo)
    ns8 = -(-ns // 8) * 8
    sel_t = _select_blocks(imp_t[:ns8], q0, ns)
    assert ns8 + 2 <= LANE
    sel_neg = jnp.concatenate([jnp.where(sel_t > 0.5, 0.0, NEG_INF)] * NSA_GROUP, axis=1)
    cfar = cfar_ref[0]
    cfar_hi = cfar.astype(BF16).astype(F32)
    sub8 = _iota((8, NSA_ROWS), 0)
    cfar8 = jnp.where(sub8 == 0, cfar_hi, jnp.where(sub8 == 1, cfar - cfar_hi, 0.0))
    pad_rows = jnp.zeros((LANE - ns8 - 8, NSA_ROWS), F32)
    qx_far = jnp.concatenate([qt, jnp.concatenate([sel_neg, cfar8, pad_rows], axis=0).astype(BF16)], axis=0)
    qx_near = jnp.concatenate([qt, jnp.concatenate([sel_neg, jnp.zeros_like(cfar8), pad_rows], axis=0).astype(BF16)],
                              axis=0)
    lane_k = _iota((TK, LANE), 1)
    ones_k = (lane_k == ns8) | (lane_k == ns8 + 1)

    def flash_step(scs, vts):
        m_old = m_ref[...]
        m_new = m_old
        for sc in scs:
            m_new = jnp.maximum(m_new, jnp.max(sc, axis=0, keepdims=True))
        alpha = jnp.exp2(m_old - m_new)
        l_new = alpha * l_ref[...]
        acc = alpha * acc_ref[...]
        for sc, vt in zip(scs, vts):
            pt = jnp.exp2(sc - m_new)
            l_new = l_new + jnp.sum(pt, axis=0, keepdims=True)
            acc = acc + _dot(vt, pt.astype(BF16))
        l_ref[...] = l_new
        acc_ref[...] = acc
        m_ref[...] = m_new

    def sel_scores(kt, bias, causal):
        k = ks_ref[pl.ds(pl.multiple_of(kt * TK, TK), TK), :].astype(BF16)
        blk_of_key = (TK // SEL_BLOCK) * kt + (_iota((TK, LANE), 0) >> 6)
        extra = jnp.where((lane_k == blk_of_key) | ones_k, 1.0, 0.0).astype(BF16)
        kx = jnp.concatenate([k, extra], axis=1)
        sc = _dot(kx, qx_far) if bias is None else _dot(kx, qx_near) + bias
        if causal:
            sc = jnp.where(c_loc <= r_loc, sc, NEG_INF)
        return sc

    def softmax_tiles(scs, vts):
        m_new = jnp.max(scs[0], axis=0, keepdims=True)
        for sc in scs[1:]:
            m_new = jnp.maximum(m_new, jnp.max(sc, axis=0, keepdims=True))
        l_new, acc = None, None
        for sc, vt in zip(scs, vts):
            pt = jnp.exp2(sc - m_new)
            l_t, acc_t = jnp.sum(pt, axis=0, keepdims=True), _dot(vt, pt.astype(BF16))
            l_new, acc = (l_t, acc_t) if l_new is None else (l_new + l_t, acc + acc_t)
        return m_new, l_new, acc

    kt_prev = jnp.maximum(i - 1, 0)
    prev_scores = jnp.where(i >= 1, sel_scores(kt_prev, tz_ref[1], False), NEG_INF)
    m_ref[...], l_ref[...], acc_ref[...] = softmax_tiles([sel_scores(i, tz_ref[0], True), prev_scores],
                                                         [vst_ref[i], vst_ref[kt_prev]])

    def win_scores(off):
        kt = jnp.maximum(i - off, 0)
        k = kw_ref[pl.ds(pl.multiple_of(kt * TK, TK), TK), :].astype(BF16)
        if off < 2:
            sc = _dot(k, qt) + tz_ref[off]
        else:
            sc = _dot(jnp.concatenate([k, jnp.where(ones_k, 1.0, 0.0).astype(BF16)], axis=1), qx_far)
        if off == 0:
            return jnp.where(c_loc <= r_loc, sc, NEG_INF)
        keep = (c_loc > r_loc) & (i >= off) if off * TK == WINDOW else (i >= off)
        return jnp.where(keep, sc, NEG_INF)

    n_win = WINDOW // TK + 1
    _, l_win, acc_win = softmax_tiles([win_scores(off) for off in range(n_win)],
                                      [vwt_ref[jnp.maximum(i - off, 0)] for off in range(n_win)])
    ow_ref[...] = acc_win / jnp.maximum(l_win, 1e-30)

    def far_tiles(kt_first, count):
        kts = [kt_first - t for t in range(count)]
        flash_step([sel_scores(kt, None, False) for kt in kts], [vst_ref[kt] for kt in kts])

    def far_group(j, carry):
        far_tiles(i - 2 - FAR_GROUP * j, FAR_GROUP)
        return carry

    n_far = jnp.maximum(i - 1, 0)
    lax.fori_loop(0, n_far // FAR_GROUP, far_group, 0)
    for rest in range(1, FAR_GROUP):
        pl.when(n_far % FAR_GROUP == rest)(functools.partial(far_tiles, rest - 1, rest))

    o_sel = _flash_result(l_ref, acc_ref)
    o_win = ow_ref[...]

    gates_t = _sigmoid(ng_ref[...]).T
    nsl = nsl_ref[...]
    outs = []
    for g in range(NSA_GROUP):
        cols = slice(g * TQ, (g + 1) * TQ)
        o_t = (gates_t[3 * g:3 * g + 1] * o_cmp[:, cols] + gates_t[3 * g + 1:3 * g + 2] * o_sel[:, cols]
               + gates_t[3 * g + 2:3 * g + 3] * o_win[:, cols])
        x = nsl[:, g * NSA_DK:(g + 1) * NSA_DK]
        outs.append(o_t.T * (x * _sigmoid(x)))
    o_ref[...] = jnp.concatenate(outs, axis=1).astype(BF16)


def _lanes_by_head(t):
    hk, g, rows, tq = t.shape
    return t.transpose(0, 2, 1, 3).reshape(hk, rows, g * tq)


def _nsa_prompt(z, zb, kcvc, bias_d, batch, seq):
    nq = seq // TQ
    ns = seq // SEL_BLOCK
    ncp = kcvc.shape[3]
    nsp = LANE
    wn = TQ // CMP_STRIDE
    assert TQ == TK and ns <= nsp and ncp >= seq // CMP_STRIDE and ncp % LANE == 0 and ncp > wn
    gw = NSA_GROUP * NSA_DK
    n_dist = bias_d.shape[-1]
    assert n_dist >= 2 * TK + TQ
    bias_d = bias_d * LOG2E

    tz = bias_d[..., :2 * TQ].reshape(NSA_KV_HEADS, 2 * NSA_GROUP, TQ)
    far = jnp.broadcast_to(bias_d[..., REL_MAX_DIST][:, :, None, None], (NSA_KV_HEADS, NSA_GROUP, 1, TQ))
    cfar = _lanes_by_head(far)
    half = n_dist // 2
    start = CMP_STRIDE * wn - (CMP_BLOCK - 1)
    assert start + TQ <= half and 2 * wn * CMP_STRIDE - start <= half
    w_ext = jnp.concatenate([bias_d[..., :half], jnp.full(bias_d.shape[:-1] + (n_dist - half,), NEG_INF, F32)], -1)
    near = jnp.tile(w_ext, (1, 1, 2 * wn + 1))[..., :2 * wn * (n_dist - CMP_STRIDE)]
    near = near.reshape(NSA_KV_HEADS, NSA_GROUP, 2 * wn, n_dist - CMP_STRIDE)[..., start:start + TQ]
    basec = jnp.concatenate([jnp.broadcast_to(far, (NSA_KV_HEADS, NSA_GROUP, ncp - wn, TQ)), near,
                             jnp.full((NSA_KV_HEADS, NSA_GROUP, ncp - wn, TQ), NEG_INF, F32)], axis=2)
    basec = _lanes_by_head(basec)
    sblk = jnp.arange(nsp)[:, None]
    nblk = jnp.arange(ncp)[None, :]
    ovt = ((nblk >= 4 * sblk - 1) & (nblk <= 4 * sblk + 3)).astype(BF16)

    vct = kcvc[:, 1].transpose(0, 1, 3, 2)

    def kvspec(col0, which):
        return pl.BlockSpec((seq, NSA_DK), lambda b, h, i: (b, col0 // NSA_DK + which * NSA_KV_HEADS + h))

    vt_scratch = pltpu.VMEM((seq // TK, NSA_DK, TK), BF16)
    return pl.pallas_call(
        functools.partial(_nsa_prompt_kernel, ns=ns),
        grid=(batch, NSA_KV_HEADS, nq),
        in_specs=[pl.BlockSpec((TQ, gw), lambda b, h, i: (b * nq + i, COL_NQ // gw + h)),
                  kvspec(COL_KVS, 0), kvspec(COL_KVS, 1), kvspec(COL_KVW, 0), kvspec(COL_KVW, 1),
                  pl.BlockSpec((1, 1, 1, ncp, NSA_DK), lambda b, h, i: (b, 0, h, 0, 0)),
                  pl.BlockSpec((1, 1, NSA_DK, ncp), lambda b, h, i: (b, h, 0, 0)),
                  pl.BlockSpec((1, 2 * NSA_GROUP, TQ), lambda b, h, i: (h, 0, 0)),
                  pl.BlockSpec((1, 1, NSA_ROWS), lambda b, h, i: (h, 0, 0)),
                  pl.BlockSpec((1, 2 * ncp, NSA_ROWS), lambda b, h, i: (h, 0, 0)),
                  pl.BlockSpec((nsp, ncp), lambda b, h, i: (0, 0)),
                  pl.BlockSpec((TQ, LANE), lambda b, h, i: (b * nq + i, COL_NG // LANE + h)),
                  pl.BlockSpec((TQ, gw), lambda b, h, i: (b * nq + i, COL_NSL // gw + h))],
        out_specs=pl.BlockSpec((TQ, gw), lambda b, h, i: (b * nq + i, h)),
        out_shape=jax.ShapeDtypeStruct((batch * seq, NSA_W), BF16),
        scratch_shapes=[pltpu.VMEM((1, NSA_ROWS), F32), pltpu.VMEM((1, NSA_ROWS), F32),
                        pltpu.VMEM((NSA_DK, NSA_ROWS), F32), vt_scratch, vt_scratch,
                        pltpu.VMEM((2, TK, NSA_ROWS), F32), pltpu.VMEM((NSA_DK, NSA_ROWS), F32)],
        compiler_params=_cparams(3),
        name="nsa_prompt",
    )(z, z, z, z, z, kcvc, vct, tz, cfar, basec, ovt, zb, zb)


SROWS = 8
SEL_PER_STEP = 2


def _stack_group_q(q_row):
    heads = [q_row[:, g * NSA_DK:(g + 1) * NSA_DK] for g in range(NSA_GROUP)]
    return jnp.concatenate(heads + [heads[0]] * (SROWS - NSA_GROUP), axis=0)


def _nsa_sample_cmp_kernel(q_ref, kc_ref, vc_ref, bias_ref, ov_ref, o_ref, idx_ref, *, ns):
    scale = NSA_DK ** -0.5
    qs = _stack_group_q(q_ref[0]).astype(BF16)
    s = _nt(qs, kc_ref[0, 0]) * scale + bias_ref[0]
    m = jnp.max(s, axis=1, keepdims=True)
    e = jnp.exp(s - m)
    inv = jnp.where(m > 0.5 * NEG_INF, 1.0 / jnp.maximum(jnp.sum(e, axis=1, keepdims=True), 1e-30), 0.0)
    p = e * inv
    o_ref[0, 0] = _dot(p.astype(BF16), vc_ref[0, 0])
    psum = jnp.broadcast_to(jnp.sum(p[0:NSA_GROUP], axis=0, keepdims=True), p.shape)
    hi = psum.astype(BF16)
    lo = (psum - hi.astype(F32)).astype(BF16)
    imp = (_dot(hi, ov_ref[...]) + _dot(lo, ov_ref[...]))[0:1]
    nsp = imp.shape[1]
    blk_r = _iota((1, nsp), 1)
    cur = ns - 1
    forced = (blk_r == 0) | (blk_r == cur) | (blk_r == cur - 1)
    imp = jnp.where(blk_r < ns, jnp.where(forced, FORCE_SCORE, imp), 2.0 * NEG_INF)
    imp_c = _column_of(imp)
    i_r = _iota((nsp, nsp), 1)
    j_c = _iota((nsp, nsp), 0)
    ahead = (imp > imp_c) | ((imp == imp_c) & (i_r < j_c))
    rank_c = jnp.sum(jnp.where(ahead, 1.0, 0.0), axis=1, keepdims=True)
    slot = _iota((nsp, LANE), 1).astype(F32)
    picks = jnp.where(rank_c == slot, _iota((nsp, LANE), 0).astype(F32), 0.0)
    idx_ref[0, 0] = jnp.broadcast_to(jnp.sum(picks, axis=0, keepdims=True), (SROWS, LANE)).astype(jnp.int32)


def _nsa_sample_cmp(z3, kcvc, bias_c, ov, ns):
    nb = z3.shape[0]
    ncp = kcvc.shape[2]
    nsp = ov.shape[1]
    gw = NSA_GROUP * NSA_DK

    def cspec(which):
        return pl.BlockSpec((1, 1, ncp, NSA_DK), lambda b, h: (b, which, 0, h))

    return pl.pallas_call(
        functools.partial(_nsa_sample_cmp_kernel, ns=ns),
        grid=(nb, NSA_KV_HEADS),
        in_specs=[pl.BlockSpec((1, 1, gw), lambda b, h: (b, 0, COL_NQ // gw + h)),
                  cspec(0), cspec(1),
                  pl.BlockSpec((1, SROWS, ncp), lambda b, h: (h, 0, 0)),
                  pl.BlockSpec((ncp, nsp), lambda b, h: (0, 0))],
        out_specs=[pl.BlockSpec((1, 1, SROWS, NSA_DK), lambda b, h: (b, h, 0, 0)),
                   pl.BlockSpec((1, 1, SROWS, LANE), lambda b, h: (b, h, 0, 0))],
        out_shape=[jax.ShapeDtypeStruct((nb, NSA_KV_HEADS, SROWS, NSA_DK), F32),
                   jax.ShapeDtypeStruct((nb, NSA_KV_HEADS, SROWS, LANE), jnp.int32)],
        compiler_params=_cparams(2),
        name="nsa_sample_cmp",
    )(z3, kcvc, kcvc, bias_c, ov)


def _nsa_sample_sel_kernel(pt_ref, idx_ref, q_ref, *refs, ns):
    n_blk = NSA_KV_HEADS * SEL_PER_STEP
    blocks, (new_ref,), biases = refs[:n_blk], refs[n_blk:n_blk + 1], refs[n_blk + 1:2 * n_blk + 1]
    o_ref, m_ref, l_ref, acc_ref = refs[2 * n_blk + 1:]
    b, t = pl.program_id(0), pl.program_id(1)
    gw = NSA_GROUP * NSA_DK
    rows_kv = 2 * NSA_KV_HEADS
    width = SEL_BLOCK * rows_kv

    @pl.when(t == 0)
    def _():
        _flash_init(m_ref, l_ref, acc_ref)

    new_rows = jnp.concatenate([new_ref[0]] * SEL_BLOCK, axis=0)
    xs, scs = [], []
    for h in range(NSA_KV_HEADS):
        qs = _stack_group_q(q_ref[0][:, h * gw:(h + 1) * gw]).astype(BF16)
        row = []
        for u in range(SEL_PER_STEP):
            r = h * SEL_PER_STEP + u
            is_new = idx_ref[b, h, t * SEL_PER_STEP + u] == ns - 1
            x = jnp.where(is_new, new_rows, blocks[r][0].reshape(width, NSA_DK)).astype(BF16)
            xs.append(x)
            row.append(_nt(qs, x) * (NSA_DK ** -0.5) + biases[r][0, 0])
        scs.append(jnp.concatenate(row, axis=1))
    sc = jnp.concatenate(scs, axis=0)
    m_old = m_ref[...]
    m_new = jnp.maximum(m_old, jnp.max(sc, axis=1, keepdims=True))
    alpha = jnp.exp(m_old - m_new)
    p = jnp.exp(sc - m_new)
    l_ref[...] = alpha * l_ref[...] + jnp.sum(p, axis=1, keepdims=True)
    pv = pltpu.roll(p, NSA_KV_HEADS, 1).astype(BF16)
    acc = alpha * acc_ref[...]
    upd = []
    for h in range(NSA_KV_HEADS):
        ph = pv[h * SROWS:(h + 1) * SROWS]
        upd.append(sum(_dot(ph[:, u * width:(u + 1) * width], xs[h * SEL_PER_STEP + u]) for u in range(SEL_PER_STEP)))
    acc_ref[...] = acc + jnp.concatenate(upd, axis=0)
    m_ref[...] = m_new

    @pl.when(t == pl.num_programs(1) - 1)
    def _():
        o_ref[0] = (acc_ref[...] / jnp.maximum(l_ref[...], 1e-30)).reshape(NSA_KV_HEADS, SROWS, NSA_DK)


def _nsa_sample_sel(z3, cache, kv_new, page_table, idx, bias_sel, ns):
    nb, n_pages = page_table.shape
    n_sel = idx.shape[2]
    halves = PAGE_SIZE // SEL_BLOCK
    rows_kv = 2 * NSA_KV_HEADS

    per_step = SEL_PER_STEP
    assert n_sel % per_step == 0

    def blockspec(h, u):
        def index(b, t, pt, ix):
            blk = ix[b, h, t * per_step + u]
            return (pt[b, jnp.minimum(blk // halves, n_pages - 1)], blk % halves, 0, 0)
        return pl.BlockSpec((1, SEL_BLOCK, rows_kv, NSA_DK), index)

    def biasspec(h, u):
        return pl.BlockSpec((1, 1, SROWS, SEL_BLOCK * rows_kv),
                            lambda b, t, pt, ix: (h, ix[b, h, t * per_step + u], 0, 0))

    slots = [(h, u) for h in range(NSA_KV_HEADS) for u in range(per_step)]
    return pl.pallas_call(
        functools.partial(_nsa_sample_sel_kernel, ns=ns),
        grid_spec=pltpu.PrefetchScalarGridSpec(
            num_scalar_prefetch=2,
            grid=(nb, n_sel // per_step),
            in_specs=[pl.BlockSpec((1, 1, NSA_W), lambda b, t, pt, ix: (b, 0, COL_NQ // NSA_W))]
                     + [blockspec(h, u) for h, u in slots]
                     + [pl.BlockSpec((1, rows_kv, NSA_DK), lambda b, t, pt, ix: (b, 0, 0))]
                     + [biasspec(h, u) for h, u in slots],
            out_specs=pl.BlockSpec((1, NSA_KV_HEADS, SROWS, NSA_DK), lambda b, t, pt, ix: (b, 0, 0, 0)),
            scratch_shapes=[pltpu.VMEM((NSA_KV_HEADS * SROWS, 1), F32), pltpu.VMEM((NSA_KV_HEADS * SROWS, 1), F32),
                            pltpu.VMEM((NSA_KV_HEADS * SROWS, NSA_DK), F32)]),
        out_shape=jax.ShapeDtypeStruct((nb, NSA_KV_HEADS, SROWS, NSA_DK), F32),
        compiler_params=_cparams(2),
        name="nsa_sample_sel",
    )(page_table, idx, z3, *([cache] * len(slots)), kv_new, *([bias_sel] * len(slots)))


def _nsa_sample_win_kernel(q_ref, k_ref, v_ref, kn_ref, vn_ref, bias_ref, bnew_ref, o_ref):
    scale = NSA_DK ** -0.5
    q = _stack_group_q(q_ref[0])
    s_buf = _nt(q.astype(BF16), k_ref[0].astype(BF16)) * scale + bias_ref[0]
    s_new = jnp.sum(q * kn_ref[0], axis=1, keepdims=True) * scale + bnew_ref[0]
    m = jnp.maximum(jnp.max(s_buf, axis=1, keepdims=True), s_new)
    p_buf = jnp.exp(s_buf - m)
    p_new = jnp.exp(s_new - m)
    l = jnp.sum(p_buf, axis=1, keepdims=True) + p_new
    acc = _dot(p_buf.astype(BF16), v_ref[0].astype(BF16)) + p_new * vn_ref[0]
    o_ref[0, 0] = acc / jnp.maximum(l, 1e-30)


def _nsa_sample_win(z3, win_buf, bias_win, bias_new):
    nb, nbuf, _ = win_buf.shape
    gw = NSA_GROUP * NSA_DK

    def bufspec(which):
        return pl.BlockSpec((1, nbuf, NSA_DK), lambda b, h: (b, 0, which * NSA_KV_HEADS + h))

    def newspec(which):
        return pl.BlockSpec((1, 1, NSA_DK), lambda b, h: (b, 0, COL_KVW // NSA_DK + which * NSA_KV_HEADS + h))

    return pl.pallas_call(
        _nsa_sample_win_kernel,
        grid=(nb, NSA_KV_HEADS),
        in_specs=[pl.BlockSpec((1, 1, gw), lambda b, h: (b, 0, COL_NQ // gw + h)),
                  bufspec(0), bufspec(1), newspec(0), newspec(1),
                  pl.BlockSpec((1, SROWS, nbuf), lambda b, h: (h, 0, 0)),
                  pl.BlockSpec((1, SROWS, 1), lambda b, h: (h, 0, 0))],
        out_specs=pl.BlockSpec((1, 1, SROWS, NSA_DK), lambda b, h: (b, h, 0, 0)),
        out_shape=jax.ShapeDtypeStruct((nb, NSA_KV_HEADS, SROWS, NSA_DK), F32),
        compiler_params=_cparams(2),
        name="nsa_sample_win",
    )(z3, win_buf, win_buf, z3, z3, bias_win, bias_new)


def _nsa_sample_gate_kernel(oc_ref, os_ref, ow_ref, ng_ref, nsl_ref, o_ref):
    gates = _sigmoid(ng_ref[0])
    nsl = nsl_ref[0]
    outs = []
    for g in range(NSA_GROUP):
        o = (gates[:, 3 * g:3 * g + 1] * oc_ref[0, 0, g:g + 1] + gates[:, 3 * g + 1:3 * g + 2] * os_ref[0, 0, g:g + 1]
             + gates[:, 3 * g + 2:3 * g + 3] * ow_ref[0, 0, g:g + 1])
        x = nsl[:, g * NSA_DK:(g + 1) * NSA_DK]
        outs.append(o * (x * _sigmoid(x)))
    o_ref[0] = jnp.concatenate(outs, axis=1).astype(BF16)


def _nsa_sample_gate(o_cmp, o_sel, o_win, z3b):
    nb = z3b.shape[0]
    gw = NSA_GROUP * NSA_DK
    ospec = pl.BlockSpec((1, 1, SROWS, NSA_DK), lambda b, h: (b, h, 0, 0))
    return pl.pallas_call(
        _nsa_sample_gate_kernel,
        grid=(nb, NSA_KV_HEADS),
        in_specs=[ospec, ospec, ospec,
                  pl.BlockSpec((1, 1, LANE), lambda b, h: (b, 0, COL_NG // LANE + h)),
                  pl.BlockSpec((1, 1, gw), lambda b, h: (b, 0, COL_NSL // gw + h))],
        out_specs=pl.BlockSpec((1, 1, gw), lambda b, h: (b, 0, h)),
        out_shape=jax.ShapeDtypeStruct((nb, 1, NSA_W), BF16),
        compiler_params=_cparams(2),
        name="nsa_sample_gate",
    )(o_cmp, o_sel, o_win, z3b, z3b)


def _mem_heads(q, kv):
    outs = []
    for h in range(MEM_HEADS):
        k = kv[:, h * MEM_DH:(h + 1) * MEM_DH].astype(BF16)
        v = kv[:, MEM_W + h * MEM_DH:MEM_W + (h + 1) * MEM_DH].astype(BF16)
        s = _nt(q[:, h * MEM_DH:(h + 1) * MEM_DH].astype(BF16), k) * (MEM_DH ** -0.5)
        e = jnp.exp(s - jnp.max(s, axis=1, keepdims=True))
        p = e / jnp.sum(e, axis=1, keepdims=True)
        outs.append(_dot(p.astype(BF16), v))
    return jnp.concatenate(outs, axis=1)


def _mem_prompt_kernel(q_ref, kv_ref, o_ref):
    o_ref[...] = _mem_heads(q_ref[:, :MEM_W], kv_ref[...]).astype(BF16)


def _mem_prompt(z, mem_kv, batch, seq, tq):
    nq = seq // tq
    n_mem = mem_kv.shape[0] // batch
    return pl.pallas_call(
        _mem_prompt_kernel,
        grid=(batch, nq),
        in_specs=[pl.BlockSpec((tq, MQ_BLOCK), lambda b, i: (b * nq + i, COL_MQ // MQ_BLOCK)),
                  pl.BlockSpec((n_mem, 2 * MEM_W), lambda b, i: (b, 0))],
        out_specs=pl.BlockSpec((tq, MEM_W), lambda b, i: (b * nq + i, 0)),
        out_shape=jax.ShapeDtypeStruct((batch * seq, MEM_W), BF16),
        compiler_params=_cparams(2),
        name="mem_prompt",
    )(z, mem_kv)


def _mem_sample_kernel(q_ref, kv_ref, o_ref):
    q = jnp.broadcast_to(q_ref[0][:, :MEM_W], (SROWS, MEM_W))
    o_ref[0] = _mem_heads(q, kv_ref[0])[0:1].astype(BF16)


def _mem_sample(z3, mem_kv):
    nb, n_mem, _ = mem_kv.shape
    return pl.pallas_call(
        _mem_sample_kernel,
        grid=(nb,),
        in_specs=[pl.BlockSpec((1, 1, MQ_BLOCK), lambda b: (b, 0, COL_MQ // MQ_BLOCK)),
                  pl.BlockSpec((1, n_mem, 2 * MEM_W), lambda b: (b, 0, 0))],
        out_specs=pl.BlockSpec((1, 1, MEM_W), lambda b: (b, 0, 0)),
        out_shape=jax.ShapeDtypeStruct((nb, 1, MEM_W), BF16),
        compiler_params=_cparams(1),
        name="mem_sample",
    )(z3, mem_kv)


def _merge_kernel(ar_ref, an_ref, am_ref, wr_ref, wn_ref, wm_ref, g0_ref, g1_ref, g2_ref, o_ref):
    merged = (_sigmoid(g0_ref[...]) * _dot(ar_ref[...], wr_ref[...])
              + _sigmoid(g1_ref[...]) * _dot(an_ref[...], wn_ref[...])
              + _sigmoid(g2_ref[...]) * _dot(am_ref[...], wm_ref[...]))
    o_ref[...] = merged.astype(BF16)


def _merge(a_ret, a_nsa, a_mem, w_ret, w_nsa, w_mem, z, tm, tn):
    m = a_ret.shape[0]
    nt = D_MODEL // tn

    def aspec(width):
        return pl.BlockSpec((tm, width), lambda i, j: (i, 0))

    def wspec(width):
        return pl.BlockSpec((width, tn), lambda i, j: (0, j))

    def gspec(branch):
        return pl.BlockSpec((tm, tn), lambda i, j: (i, COL_MG // tn + branch * nt + j))

    return pl.pallas_call(
        _merge_kernel,
        grid=(m // tm, nt),
        in_specs=[aspec(RET_W), aspec(NSA_W), aspec(MEM_W), wspec(RET_W), wspec(NSA_W), wspec(MEM_W),
                  gspec(0), gspec(1), gspec(2)],
        out_specs=pl.BlockSpec((tm, tn), lambda i, j: (i, j)),
        out_shape=jax.ShapeDtypeStruct((m, D_MODEL), BF16),
        compiler_params=_cparams(2),
        name="merge",
    )(a_ret, a_nsa, a_mem, w_ret, w_nsa, w_mem, z, z, z)


def _out_kernel(a_ref, w_ref, x_ref, g_ref, o_ref):
    out = _dot(a_ref[...], w_ref[...])
    y = out * lax.rsqrt(jnp.mean(out * out, axis=-1, keepdims=True) + EPS)
    o_ref[...] = x_ref[...] + y * g_ref[...]


def _out_proj(merged, w_out, x, norm_post, tm):
    m = merged.shape[0]
    return pl.pallas_call(
        _out_kernel,
        grid=(m // tm,),
        in_specs=[pl.BlockSpec((tm, D_MODEL), lambda i: (i, 0)),
                  pl.BlockSpec((D_MODEL, D_MODEL), lambda i: (0, 0)),
                  pl.BlockSpec((tm, D_MODEL), lambda i: (i, 0)),
                  pl.BlockSpec((1, D_MODEL), lambda i: (0, 0))],
        out_specs=pl.BlockSpec((tm, D_MODEL), lambda i: (i, 0)),
        out_shape=jax.ShapeDtypeStruct((m, D_MODEL), F32),
        compiler_params=_cparams(1),
        name="out_proj",
    )(merged, w_out, x, norm_post.reshape(1, D_MODEL))


def _layout_w_ng(w_t):
    per_group = N_BRANCHES * NSA_GROUP
    ng = w_t[PROJ_A:PROJ_A + N_BRANCHES * NSA_HEADS].reshape(NSA_KV_HEADS, per_group, D_MODEL)
    return jnp.pad(ng, ((0, 0), (0, LANE - per_group), (0, 0))).reshape(NG_SLOT, D_MODEL)


def _pick_tile(m, cap):
    t = min(m, cap)
    while m % t:
        t //= 2
    return t


def kernel(x_prompt, x_sample, cache_cmp_kv, cache_sel_kv, cache_win_kv, state_ret, cache_mem_kv, page_table,
           mem_prompt, rel_table, norm_pre, norm_post, norm_mem, w_in, ret_norm, w_ret_up, cmp_pos, w_cmp1,
           w_cmp2, w_nsa_up, w_mem_kv, w_mem_up, w_out):
    batch, seq, _ = x_prompt.shape
    nb = x_sample.shape[0]
    assert x_sample.shape[1] == 1 and norm_pre.shape[0] == 1
    assert seq % TQ == 0 and seq >= WINDOW
    n_pool = cache_cmp_kv.shape[1]
    n_pages = page_table.shape[1]
    past = n_pages * PAGE_SIZE
    n_mem = mem_prompt.shape[1]
    assert n_pages % PAGES_PER_STEP == 0 and cache_win_kv.shape[2] == WINDOW

    w_a = w_in[0].T
    w_ng = _layout_w_ng(w_a)
    kw = CMP_STRIDE * NSA_DK
    w1 = w_cmp1[0].reshape(2, CMP_BLOCK * NSA_DK, NSA_DK).astype(BF16)
    w1a, w1b = w1[:, :kw], w1[:, kw:]
    w1ab = jnp.concatenate([w1a, w1b], axis=2)
    w2 = w_cmp2[0].astype(BF16)
    pos8 = jnp.pad(cmp_pos[0].reshape(2, 1, CMP_BLOCK * NSA_DK), ((0, 0), (0, 7), (0, 0)))
    w_ret = w_ret_up[0].astype(BF16)
    w_nsa = w_nsa_up[0].astype(BF16)
    w_mem = w_mem_up[0].astype(BF16)
    w_o = w_out[0].astype(BF16)

    m_p = batch * seq
    xp = x_prompt.reshape(m_p, D_MODEL)
    hp = _rmsnorm(xp, norm_pre[0], _pick_tile(m_p, 512))
    z, kvc_rows, kvs_rows, kvw_rows = _proj_kv(hp, w_a, _pick_tile(seq, WINDOW), seq)
    zb = _proj_tail(hp, w_a, w_ng, _pick_tile(m_p, 1024))

    a_ret, ret_state_p = _retention_prompt(z, ret_norm[0], batch, seq)

    ncp = max(LANE, -(-(seq // CMP_STRIDE) // LANE) * LANE)
    ac = _cmp_stage1_dense(z, w1ab, batch, seq)
    kcvc = _cmp_stage2(ac, pos8, w1a, w1b, w2, 1)
    if ncp > kcvc.shape[3]:
        kcvc = jnp.pad(kcvc, ((0, 0), (0, 0), (0, 0), (0, ncp - kcvc.shape[3]), (0, 0)))
    bias_d = _bias_by_dist(rel_table, BIAS_DISTS)
    a_nsa = _nsa_prompt(z, zb, kcvc, bias_d, batch, seq)

    hm = _rmsnorm(mem_prompt.reshape(batch * n_mem, D_MODEL), norm_mem[0], _pick_tile(batch * n_mem, 512))
    mem_kv_p = _proj(hm, w_mem_kv[0], _pick_tile(batch * n_mem, 512), PROJ_TN)
    a_mem = _mem_prompt(zb, mem_kv_p, batch, seq, _pick_tile(seq, 512))

    merged = _merge(a_ret, a_nsa, a_mem, w_ret, w_nsa, w_mem, zb, _pick_tile(m_p, 1024), 512)
    y_p = _out_proj(merged, w_o, xp, norm_post[0], _pick_tile(m_p, 512)).reshape(batch, seq, D_MODEL)

    kv_shape = (1, batch, seq, 2, NSA_KV_HEADS, NSA_DK)
    new_cmp_p = kvc_rows.reshape(kv_shape)
    new_sel_p = kvs_rows.reshape(kv_shape)
    new_win_p = kvw_rows.reshape(1, batch, WINDOW, 2, NSA_KV_HEADS, NSA_DK)
    new_ret_p = ret_state_p[None]
    new_mem_p = mem_kv_p.reshape(1, batch, n_mem, 2, MEM_HEADS, MEM_DH)

    xs = x_sample.reshape(nb, D_MODEL)
    hs = _rmsnorm(xs, norm_pre[0], nb)
    zs = _proj(hs, w_a, nb, PROJ_TN, PROJ_A, transposed=True)
    zsb = _proj_tail(hs, w_a, w_ng, nb)
    z3 = zs.reshape(nb, 1, PROJ_A)
    z3b = zsb.reshape(nb, 1, PROJ_B)

    a_ret_s, ret_state_s = _retention_sample(z3, state_ret[0], ret_norm[0], past)

    cache_c = cache_cmp_kv[0].reshape(n_pool, PAGE_SIZE // CMP_STRIDE, CMP_STRIDE, 2 * NSA_KV_HEADS, NSA_DK)
    cache_s = cache_sel_kv[0].reshape(n_pool, PAGE_SIZE, 2 * NSA_KV_HEADS, NSA_DK)
    ac_s = _cmp_stage1_paged(cache_c, page_table, w1ab)
    kcvc_s = _cmp_stage2(ac_s, pos8, w1a, w1b, w2, NSA_KV_HEADS)
    ncs = past // CMP_STRIDE
    kcvc_s = kcvc_s.reshape(nb, 2, ncs, KV_W)
    ns_s = past // SEL_BLOCK + 1
    nsp_s = -(-ns_s // LANE) * LANE
    assert past >= WINDOW and past >= REL_MAX_DIST and BIAS_DISTS > WINDOW
    far_s = bias_d[..., REL_MAX_DIST:REL_MAX_DIST + 1]
    hg = (NSA_KV_HEADS, NSA_GROUP)
    n_valid = (past - (CMP_BLOCK - 1)) // CMP_STRIDE + 1
    strided = bias_d[..., (past - (CMP_BLOCK - 1)) % CMP_STRIDE::CMP_STRIDE]
    n_tab = strided.shape[-1]
    assert n_valid >= n_tab and n_tab * CMP_STRIDE > REL_MAX_DIST + CMP_STRIDE and ncs >= n_valid
    bias_cs = jnp.concatenate([jnp.broadcast_to(far_s, hg + (n_valid - n_tab,)), strided[..., ::-1],
                               jnp.full(hg + (ncs - n_valid,), NEG_INF, F32)], axis=-1)
    bias_cs = _pad_group_rows(bias_cs, 1)
    nblk = jnp.arange(ncs)[:, None]
    sblk = jnp.arange(nsp_s)[None, :]
    ov_s = ((nblk >= 4 * sblk - 1) & (nblk <= 4 * sblk + 3)).astype(BF16)
    o_cmp_s, idx_s = _nsa_sample_cmp(z3, kcvc_s, bias_cs, ov_s, ns_s)
    n_sel = min(SEL_TOPK, ns_s)
    idx = idx_s[:, :, 0, :n_sel]

    hg = (NSA_KV_HEADS, NSA_GROUP)
    n_key = ns_s * SEL_BLOCK
    bias_sel = jnp.concatenate([jnp.broadcast_to(far_s, hg + (past + 1 - REL_MAX_DIST,)),
                                bias_d[..., :REL_MAX_DIST][..., ::-1],
                                jnp.full(hg + (n_key - past - 1,), NEG_INF, F32)], axis=-1)
    bias_sel = _pad_group_rows(bias_sel.reshape(hg + (ns_s, SEL_BLOCK)).transpose(0, 2, 1, 3), 2)
    own_k = jnp.arange(2 * NSA_KV_HEADS)[None, :] == jnp.arange(NSA_KV_HEADS)[:, None]
    bias_sel = jnp.where(own_k[:, None, None, None, :], bias_sel[..., None], NEG_INF)
    bias_sel = bias_sel.reshape(NSA_KV_HEADS, ns_s, SROWS, SEL_BLOCK * 2 * NSA_KV_HEADS)
    kvs_new = zs[:, COL_KVS:COL_KVS + 2 * KV_W].reshape(nb, 2 * NSA_KV_HEADS, NSA_DK)
    o_sel_s = _nsa_sample_sel(z3, cache_s, kvs_new, page_table, idx, bias_sel, ns_s)

    win_buf = cache_win_kv[0].reshape(nb, WINDOW, 2 * KV_W)
    bias_w = jnp.concatenate([jnp.full(hg + (1,), NEG_INF, F32), bias_d[..., 1:WINDOW][..., ::-1]], axis=-1)
    o_win_s = _nsa_sample_win(z3, win_buf, _pad_group_rows(bias_w, 1), _pad_group_rows(bias_d[..., 0:1], 1))
    a_nsa_s = _nsa_sample_gate(o_cmp_s, o_sel_s, o_win_s, z3b)

    mem_kv_s = cache_mem_kv[0].reshape(nb, n_mem, 2 * MEM_W)
    a_mem_s = _mem_sample(z3b, mem_kv_s)

    merged_s = _merge(a_ret_s.reshape(nb, RET_W), a_nsa_s.reshape(nb, NSA_W), a_mem_s.reshape(nb, MEM_W),
                      w_ret, w_nsa, w_mem, zsb, nb, 512)
    y_s = _out_proj(merged_s, w_o, xs, norm_post[0], nb).reshape(nb, 1, D_MODEL)

    kvs_shape = (1, nb, 1, 2, NSA_KV_HEADS, NSA_DK)
    new_cmp_s = zs[:, COL_KVC:COL_KVC + 2 * KV_W].reshape(kvs_shape)
    new_sel_s = zs[:, COL_KVS:COL_KVS + 2 * KV_W].reshape(kvs_shape)
    kvw_s = zs[:, COL_KVW:COL_KVW + 2 * KV_W].reshape(nb, 1, 2, NSA_KV_HEADS, NSA_DK)
    new_win_s = jnp.concatenate([cache_win_kv[0][:, 1:], kvw_s], axis=1)[None]
    new_ret_s = ret_state_s[None]

    return (y_p, y_s, new_cmp_p, new_sel_p, new_win_p, new_ret_p, new_mem_p,
            new_cmp_s, new_sel_s, new_win_s, new_ret_s)
```

```python
import functools
import math

import jax
import jax.numpy as jnp
from jax import lax
from jax.experimental import pallas as pl
from jax.experimental.pallas import tpu as pltpu

F32 = jnp.float32
BF16 = jnp.bfloat16

D_MODEL = 2048
PAGE_SIZE = 128
RET_HEADS = 8
RET_DK = 256
RET_DV = 256
RET_CHUNK = 128
ROPE_BASE = 10000.0
NSA_HEADS = 16
NSA_KV_HEADS = 4
NSA_GROUP = NSA_HEADS // NSA_KV_HEADS
NSA_DK = 128
CMP_BLOCK = 32
CMP_STRIDE = 16
SEL_BLOCK = 64
SEL_TOPK = 16
WINDOW = 512
MEM_HEADS = 4
MEM_DH = 384
REL_BUCKETS = 32
REL_MAX_EXACT = 16
REL_MAX_DIST = 128
N_BRANCHES = 3
EPS = 1e-6
NEG_INF = -1e30
FORCE_SCORE = 1e4

RET_W = RET_HEADS * RET_DV
NSA_W = NSA_HEADS * NSA_DK
KV_W = NSA_KV_HEADS * NSA_DK
MEM_W = MEM_HEADS * MEM_DH

COL_RQ = 0
COL_RK = COL_RQ + RET_HEADS * RET_DK
COL_RV = COL_RK + RET_HEADS * RET_DK
COL_RG = COL_RV + RET_W
COL_NQ = COL_RG + RET_W
COL_KVC = COL_NQ + NSA_W
COL_KVS = COL_KVC + 2 * KV_W
COL_KVW = COL_KVS + 2 * KV_W
PROJ_A = COL_KVW + 2 * KV_W
COL_NSL = 0
COL_MG = COL_NSL + NSA_W
COL_MQ = COL_MG + N_BRANCHES * D_MODEL
COL_NG = COL_MQ + MEM_W
NG_SLOT = NSA_KV_HEADS * 128
PROJ_B = COL_NG + NG_SLOT
MQ_BLOCK = MEM_W + NG_SLOT

LOG2E = math.log2(math.e)
LANE = 128
TQ = 256
TK = 256
NSA_ROWS = NSA_GROUP * TQ
BIAS_DISTS = 1024
FAR_GROUP = 4
RET_STEP_CHUNKS = 8
PROJ_TN = 1024
VMEM_LIMIT = 56 * 1024 * 1024


def _cparams(n_axes):
    return pltpu.CompilerParams(dimension_semantics=("arbitrary",) * n_axes, vmem_limit_bytes=VMEM_LIMIT)


def _nt(a, b):
    return lax.dot_general(a, b, (((1,), (1,)), ((), ())), preferred_element_type=F32)


def _dot(a, b):
    return jnp.dot(a, b, preferred_element_type=F32)


def _sigmoid(x):
    return 1.0 / (1.0 + jnp.exp(-x))


def _iota(shape, dim):
    return lax.broadcasted_iota(jnp.int32, shape, dim)


def _rmsnorm_kernel(x_ref, g_ref, h_ref):
    x = x_ref[...]
    ms = jnp.mean(x * x, axis=-1, keepdims=True)
    h_ref[...] = ((x * lax.rsqrt(ms + EPS)) * g_ref[...]).astype(BF16)


def _rmsnorm(x, g, tm):
    m, k = x.shape
    return pl.pallas_call(
        _rmsnorm_kernel,
        grid=(m // tm,),
        in_specs=[pl.BlockSpec((tm, k), lambda i: (i, 0)), pl.BlockSpec((1, k), lambda i: (0, 0))],
        out_specs=pl.BlockSpec((tm, k), lambda i: (i, 0)),
        out_shape=jax.ShapeDtypeStruct((m, k), BF16),
        compiler_params=_cparams(1),
        name="rmsnorm",
    )(x, g.reshape(1, k))


def _proj_kernel(h_ref, w_ref, o_ref, wb_ref, *, transposed):
    @pl.when(pl.program_id(1) == 0)
    def _():
        if transposed:
            _store_transposed(wb_ref, w_ref, wb_ref.shape[1])
        else:
            wb_ref[...] = w_ref[...].astype(BF16)

    o_ref[...] = _dot(h_ref[...], wb_ref[...])


def _proj(h, w, tm, tn, n=None, transposed=False):
    m, k = h.shape
    n = w.shape[0 if transposed else 1] if n is None else n
    wspec = pl.BlockSpec((tn, k), lambda j, i: (j, 0)) if transposed else pl.BlockSpec((k, tn), lambda j, i: (0, j))
    return pl.pallas_call(
        functools.partial(_proj_kernel, transposed=transposed),
        grid=(n // tn, m // tm),
        in_specs=[pl.BlockSpec((tm, k), lambda j, i: (i, 0)), wspec],
        out_specs=pl.BlockSpec((tm, tn), lambda j, i: (i, j)),
        out_shape=jax.ShapeDtypeStruct((m, n), F32),
        scratch_shapes=[pltpu.VMEM((k, tn), BF16)],
        compiler_params=_cparams(2),
        name="proj",
    )(h, w)


def _proj_kv_kernel(h_ref, w_ref, o_ref, oc_ref, os_ref, ow_ref, wb_ref, *, tiles_per_batch):
    j, i = pl.program_id(0), pl.program_id(1)
    n_tiles = pl.num_programs(0)

    @pl.when(i == 0)
    def _():
        _store_transposed(wb_ref, w_ref, wb_ref.shape[1])

    res = _dot(h_ref[...], wb_ref[...])
    o_ref[...] = res
    tm, tn = res.shape
    rows_kv = tn // NSA_DK

    @pl.when(j == n_tiles - 3)
    def _():
        oc_ref[...] = res.reshape(tm * rows_kv, NSA_DK)

    @pl.when(j == n_tiles - 2)
    def _():
        os_ref[...] = res.reshape(tm * rows_kv, NSA_DK)

    @pl.when((j == n_tiles - 1) & (i % tiles_per_batch == tiles_per_batch - 1))
    def _():
        ow_ref[...] = res[tm - WINDOW:, :].reshape(WINDOW * rows_kv, NSA_DK)


def _proj_kv(h, w_t, tm, seq):
    m, k = h.shape
    tn = 2 * KV_W
    assert COL_KVC % tn == 0 and COL_KVC == PROJ_A - 3 * tn and seq % tm == 0 and tm >= WINDOW
    n_j, n_i = 3, m // tm
    first = COL_KVC // tn
    tiles_per_batch = seq // tm
    rows_kv = tn // NSA_DK

    def kv_rows(tile):
        return lambda j, i: (jnp.where(j < tile, 0, jnp.where(j == tile, i, n_i - 1)), 0)

    return pl.pallas_call(
        functools.partial(_proj_kv_kernel, tiles_per_batch=tiles_per_batch),
        grid=(n_j, n_i),
        in_specs=[pl.BlockSpec((tm, k), lambda j, i: (i, 0)),
                  pl.BlockSpec((tn, k), lambda j, i: (first + j, 0))],
        out_specs=[pl.BlockSpec((tm, tn), lambda j, i: (i, j)),
                   pl.BlockSpec((tm * rows_kv, NSA_DK), kv_rows(n_j - 3)),
                   pl.BlockSpec((tm * rows_kv, NSA_DK), kv_rows(n_j - 2)),
                   pl.BlockSpec((WINDOW * rows_kv, NSA_DK),
                                lambda j, i: (jnp.where(j < n_j - 1, 0, i // tiles_per_batch), 0))],
        out_shape=[jax.ShapeDtypeStruct((m, n_j * tn), F32),
                   jax.ShapeDtypeStruct((m * rows_kv, NSA_DK), F32),
                   jax.ShapeDtypeStruct((m * rows_kv, NSA_DK), F32),
                   jax.ShapeDtypeStruct((m // seq * WINDOW * rows_kv, NSA_DK), F32)],
        scratch_shapes=[pltpu.VMEM((k, tn), BF16)],
        compiler_params=_cparams(2),
        name="proj_kv",
    )(h, w_t)


TAIL_TN = 1024


def _store_transposed(dst_ref, src_ref, rows, chunk=256):
    for r0 in range(0, rows, chunk):
        dst_ref[:, r0:r0 + chunk] = src_ref[r0:r0 + chunk, :].T.astype(BF16)


def _proj_tail_kernel(h_ref, w_ref, wng_ref, o_ref, wb_ref, *, last_rows):
    tn = wb_ref.shape[1]
    is_last = pl.program_id(0) == pl.num_programs(0) - 1

    @pl.when((pl.program_id(1) == 0) & jnp.logical_not(is_last))
    def _():
        _store_transposed(wb_ref, w_ref, tn)

    @pl.when((pl.program_id(1) == 0) & is_last)
    def _():
        _store_transposed(wb_ref, w_ref, last_rows)
        wb_ref[:, last_rows:] = wng_ref[...].T.astype(BF16)

    o_ref[...] = _dot(h_ref[...], wb_ref[...])


def _proj_tail(h, w_t, w_ng, tm):
    m, k = h.shape
    tn = TAIL_TN
    row_ng = PROJ_A
    row_nsl = row_ng + N_BRANCHES * NSA_HEADS
    row_mq = row_nsl + NSA_W
    row_mg = row_mq + MEM_W
    assert row_mg + N_BRANCHES * D_MODEL == w_t.shape[0] and w_ng.shape == (NG_SLOT, k)
    assert COL_NSL == 0 and COL_MG % tn == 0 and COL_MQ % tn == 0 and COL_NG + NG_SLOT == PROJ_B == COL_MQ + 2 * tn
    assert row_mq % 16 == 0 and row_nsl % 16 == 0 and row_mg % 16 == 0
    t_mg, t_mq = COL_MG // tn, COL_MQ // tn

    def w_row(j, i):
        row = jnp.where(j < t_mg, row_nsl + j * tn,
                        jnp.where(j < t_mq, row_mg + (j - t_mg) * tn, row_mq + (j - t_mq) * tn))
        return (pl.multiple_of(row, 16), 0)

    return pl.pallas_call(
        functools.partial(_proj_tail_kernel, last_rows=COL_NG - COL_MQ - tn),
        grid=(PROJ_B // tn, m // tm),
        in_specs=[pl.BlockSpec((tm, k), lambda j, i: (i, 0)),
                  pl.BlockSpec((pl.Element(tn), pl.Element(k)), w_row),
                  pl.BlockSpec((NG_SLOT, k), lambda j, i: (0, 0))],
        out_specs=pl.BlockSpec((tm, tn), lambda j, i: (i, j)),
        out_shape=jax.ShapeDtypeStruct((m, PROJ_B), F32),
        scratch_shapes=[pltpu.VMEM((k, tn), BF16)],
        compiler_params=_cparams(2),
        name="proj_tail",
    )(h, w_t, w_ng)


def _rope_rows(x, cos, sin):
    half = x.shape[-1] // 2
    x1, x2 = x[:, :half], x[:, half:]
    return jnp.concatenate([x1 * cos - x2 * sin, x1 * sin + x2 * cos], axis=-1)


def _head_norm_gate(o, gnorm, rg):
    oc = o - jnp.mean(o, axis=-1, keepdims=True)
    y = oc * lax.rsqrt(jnp.mean(oc * oc, axis=-1, keepdims=True) + EPS) * gnorm
    return y * (rg * _sigmoid(rg))


def _ret_prompt_kernel(q_ref, k_ref, v_ref, rg_ref, cos_ref, sin_ref, dmat_ref, xi_ref, zeta_ref, gc_ref,
                       gn_ref, a_ref, s_ref):
    @pl.when(pl.program_id(2) == 0)
    def _():
        s_ref[...] = jnp.zeros_like(s_ref)

    c = RET_CHUNK
    for t in range(q_ref.shape[0] // c):
        rows = slice(t * c, (t + 1) * c)
        cos, sin = cos_ref[rows, :], sin_ref[rows, :]
        q = _rope_rows(q_ref[rows, :], cos, sin)
        k = _rope_rows(k_ref[rows, :], cos, sin) * (RET_DK ** -0.5)
        qb, vb = q.astype(BF16), v_ref[rows, :].astype(BF16)
        state = s_ref[0, 0]
        inner = _nt(qb, k.astype(BF16)) * dmat_ref[0]
        o = _dot(inner.astype(BF16), vb) + _dot(qb, state.astype(BF16)) * xi_ref[0]
        kz_t = (k * zeta_ref[0]).T.astype(BF16)
        s_ref[0, 0] = state * gc_ref[0] + _dot(kz_t, vb)
        a_ref[rows, :] = _head_norm_gate(o, gn_ref[...], rg_ref[rows, :]).astype(BF16)


def _decay_tables(chunk):
    log_g = jnp.log1p(-jnp.exp2(-5.0 - jnp.arange(RET_HEADS, dtype=F32)))
    i = jnp.arange(chunk, dtype=F32)
    diff = i[:, None] - i[None, :]
    dmat = jnp.where(diff >= 0, jnp.exp(log_g[:, None, None] * jnp.maximum(diff, 0.0)), 0.0)
    xi = jnp.exp(log_g[:, None] * (i[None, :] + 1.0))[:, :, None]
    zeta = jnp.exp(log_g[:, None] * (chunk - 1.0 - i[None, :]))[:, :, None]
    g_chunk = jnp.exp(log_g * chunk)[:, None, None]
    return dmat, xi, zeta, g_chunk


def _rope_tables(pos):
    half = RET_DK // 2
    freq = jnp.power(ROPE_BASE, -jnp.arange(half, dtype=F32) / half)
    ang = pos.astype(F32)[:, None] * freq[None, :]
    return jnp.cos(ang), jnp.sin(ang)


def _retention_prompt(z, ret_norm, batch, seq):
    c = RET_CHUNK
    rows = _pick_tile(seq, RET_STEP_CHUNKS * c)
    nc = seq // rows
    dmat, xi, zeta, g_chunk = _decay_tables(c)
    cos, sin = _rope_tables(jnp.arange(seq))
    hb = RET_DK

    def zspec(col0):
        return pl.BlockSpec((rows, hb), lambda b, h, t, col0=col0: (b * nc + t, col0 // hb + h))

    per_head = lambda shape: pl.BlockSpec((1,) + shape, lambda b, h, t: (h, 0, 0))
    return pl.pallas_call(
        _ret_prompt_kernel,
        grid=(batch, RET_HEADS, nc),
        in_specs=[zspec(COL_RQ), zspec(COL_RK), zspec(COL_RV), zspec(COL_RG),
                  pl.BlockSpec((rows, hb // 2), lambda b, h, t: (t, 0)),
                  pl.BlockSpec((rows, hb // 2), lambda b, h, t: (t, 0)),
                  per_head((c, c)), per_head((c, 1)), per_head((c, 1)), per_head((1, 1)),
                  pl.BlockSpec((1, hb), lambda b, h, t: (0, h))],
        out_specs=[pl.BlockSpec((rows, hb), lambda b, h, t: (b * nc + t, h)),
                   pl.BlockSpec((1, 1, RET_DK, RET_DV), lambda b, h, t: (b, h, 0, 0))],
        out_shape=[jax.ShapeDtypeStruct((batch * seq, RET_W), BF16),
                   jax.ShapeDtypeStruct((batch, RET_HEADS, RET_DK, RET_DV), F32)],
        compiler_params=_cparams(3),
        name="retention_prompt",
    )(z, z, z, z, cos, sin, dmat, xi, zeta, g_chunk, ret_norm.reshape(1, RET_W))


def _column_of(row):
    n = row.shape[1]
    eye = _iota((n, n), 0) == _iota((n, n), 1)
    return jnp.sum(jnp.where(eye, jnp.broadcast_to(row, (n, n)), 0.0), axis=1, keepdims=True)


def _ret_sample_kernel(q_ref, k_ref, v_ref, rg_ref, cos_ref, sin_ref, gam_ref, gn_ref, s_ref, a_ref, so_ref):
    cos, sin = cos_ref[...], sin_ref[...]
    outs = []
    for h in range(RET_HEADS):
        cols = slice(h * RET_DK, (h + 1) * RET_DK)
        q = _rope_rows(q_ref[0][:, cols], cos, sin)
        k = _rope_rows(k_ref[0][:, cols], cos, sin) * (RET_DK ** -0.5)
        v = v_ref[0][:, cols]
        state = s_ref[0, h]
        gamma = gam_ref[h]
        qk = jnp.sum(q * k, axis=-1, keepdims=True)
        o = qk * v + jnp.sum(_column_of(q) * state, axis=0, keepdims=True) * gamma
        so_ref[0, h] = state * gamma + _column_of(k) * v
        outs.append(_head_norm_gate(o, gn_ref[:, cols], rg_ref[0][:, cols]))
    a_ref[0] = jnp.concatenate(outs, axis=1).astype(BF16)


def _retention_sample(z3, state, ret_norm, pos):
    nb = z3.shape[0]
    cos, sin = _rope_tables(jnp.full((1,), pos))
    gamma = jnp.exp(jnp.log1p(-jnp.exp2(-5.0 - jnp.arange(RET_HEADS, dtype=F32))))[:, None, None]
    hb = RET_DK
    assert RET_HEADS * RET_DK == RET_W

    def zspec(col0):
        return pl.BlockSpec((1, 1, RET_W), lambda b, col0=col0: (b, 0, col0 // RET_W))

    st_spec = pl.BlockSpec((1, RET_HEADS, RET_DK, RET_DV), lambda b: (b, 0, 0, 0))
    return pl.pallas_call(
        _ret_sample_kernel,
        grid=(nb,),
        in_specs=[zspec(COL_RQ), zspec(COL_RK), zspec(COL_RV), zspec(COL_RG),
                  pl.BlockSpec((1, hb // 2), lambda b: (0, 0)),
                  pl.BlockSpec((1, hb // 2), lambda b: (0, 0)),
                  pl.BlockSpec((RET_HEADS, 1, 1), lambda b: (0, 0, 0)),
                  pl.BlockSpec((1, RET_W), lambda b: (0, 0)),
                  st_spec],
        out_specs=[pl.BlockSpec((1, 1, RET_W), lambda b: (b, 0, 0)), st_spec],
        out_shape=[jax.ShapeDtypeStruct((nb, 1, RET_W), BF16),
                   jax.ShapeDtypeStruct(state.shape, F32)],
        compiler_params=_cparams(1),
        name="retention_sample",
    )(z3, z3, z3, z3, cos, sin, gamma, ret_norm.reshape(1, RET_W), state)


def _half_rows(ref_slice_fn, n_half):
    return jnp.concatenate([ref_slice_fn(p) for p in range(CMP_STRIDE)], axis=1)


def _cmp_stage1_dense_kernel(x_ref, w_ref, o_ref):
    nh = o_ref.shape[3]
    x = _half_rows(lambda p: x_ref[pl.ds(p, nh, stride=CMP_STRIDE), :], nh).astype(BF16)
    o_ref[0, 0, 0] = _dot(x, w_ref[0])


def _cmp_stage1_dense(z, w1ab, batch, seq):
    nh = seq // CMP_STRIDE
    return pl.pallas_call(
        _cmp_stage1_dense_kernel,
        grid=(batch, 2, NSA_KV_HEADS),
        in_specs=[pl.BlockSpec((seq, NSA_DK), lambda b, kv, h: (b, kv * NSA_KV_HEADS + h)),
                  pl.BlockSpec((1, CMP_STRIDE * NSA_DK, 2 * NSA_DK), lambda b, kv, h: (kv, 0, 0))],
        out_specs=pl.BlockSpec((1, 1, 1, nh, 2 * NSA_DK), lambda b, kv, h: (b, kv, h, 0, 0)),
        out_shape=jax.ShapeDtypeStruct((batch, 2, NSA_KV_HEADS, nh, 2 * NSA_DK), F32),
        compiler_params=_cparams(3),
        name="cmp_stage1_dense",
    )(z, w1ab)


PAGES_PER_STEP = 16


def _cmp_stage1_paged_kernel(pt_ref, *refs):
    pages, (w_ref, o_ref) = refs[:PAGES_PER_STEP], refs[PAGES_PER_STEP:]
    hp = PAGE_SIZE // CMP_STRIDE
    top = _iota((2 * NSA_KV_HEADS, NSA_DK), 0) < NSA_KV_HEADS
    cols = [[], []]
    for p in range(CMP_STRIDE):
        tiles = [[], []]
        for pg in pages:
            xp = pg[0, :, p]
            for n in range(0, hp, 2):
                a, b = xp[n], xp[n + 1]
                tiles[0].append(jnp.where(top, a, pltpu.roll(b, NSA_KV_HEADS, 0)))
                tiles[1].append(jnp.where(top, pltpu.roll(a, NSA_KV_HEADS, 0), b))
        for kv in range(2):
            cols[kv].append(jnp.concatenate(tiles[kv], axis=0))
    for kv in range(2):
        x = jnp.concatenate(cols[kv], axis=1).astype(BF16)
        o_ref[0, kv, 0] = _dot(x, w_ref[kv])


def _cmp_stage1_paged(cache, page_table, w1ab):
    nb, n_pages = page_table.shape
    hp = PAGE_SIZE // CMP_STRIDE
    steps = n_pages // PAGES_PER_STEP
    rows = PAGES_PER_STEP * hp * NSA_KV_HEADS

    def page_spec(j):
        return pl.BlockSpec((1, hp, CMP_STRIDE, 2 * NSA_KV_HEADS, NSA_DK),
                            lambda b, s, pt, j=j: (pt[b, s * PAGES_PER_STEP + j], 0, 0, 0, 0))

    wspec = pl.BlockSpec((2, CMP_STRIDE * NSA_DK, 2 * NSA_DK), lambda b, s, pt: (0, 0, 0))
    return pl.pallas_call(
        _cmp_stage1_paged_kernel,
        grid_spec=pltpu.PrefetchScalarGridSpec(
            num_scalar_prefetch=1,
            grid=(nb, steps),
            in_specs=[page_spec(j) for j in range(PAGES_PER_STEP)] + [wspec],
            out_specs=pl.BlockSpec((1, 2, 1, rows, 2 * NSA_DK), lambda b, s, pt: (b, 0, 0, s, 0))),
        out_shape=jax.ShapeDtypeStruct((nb, 2, 1, steps * rows, 2 * NSA_DK), F32),
        compiler_params=_cparams(2),
        name="cmp_stage1_paged",
    )(page_table, *([cache] * PAGES_PER_STEP), w1ab)


def _cmp_stage2_kernel(ac_ref, pos_ref, w1a_ref, w1b_ref, w2_ref, o_ref, *, shift):
    ac = ac_ref[0, 0, 0]
    nh = ac.shape[0]
    pos = pos_ref[0].astype(BF16)
    kw = CMP_STRIDE * NSA_DK
    pe = _dot(pos[:, :kw], w1a_ref[0]) + _dot(pos[:, kw:], w1b_ref[0])
    pre = ac[:, :NSA_DK] + pltpu.roll(ac[:, NSA_DK:], nh - shift, 0) + pe[0:1]
    gelu = 0.5 * pre * (1.0 + jnp.tanh(math.sqrt(2.0 / math.pi) * (pre + 0.044715 * (pre * pre * pre))))
    o_ref[0, 0, 0] = _dot(gelu.astype(BF16), w2_ref[0]).astype(BF16)


def _cmp_stage2(ac, pos8, w1a, w1b, w2, shift):
    nb, _, groups, nh, _ = ac.shape
    kw = CMP_STRIDE * NSA_DK
    return pl.pallas_call(
        functools.partial(_cmp_stage2_kernel, shift=shift),
        grid=(nb, 2, groups),
        in_specs=[pl.BlockSpec((1, 1, 1, nh, 2 * NSA_DK), lambda b, kv, h: (b, kv, h, 0, 0)),
                  pl.BlockSpec((1, 8, 2 * kw), lambda b, kv, h: (kv, 0, 0)),
                  pl.BlockSpec((1, kw, NSA_DK), lambda b, kv, h: (kv, 0, 0)),
                  pl.BlockSpec((1, kw, NSA_DK), lambda b, kv, h: (kv, 0, 0)),
                  pl.BlockSpec((1, NSA_DK, NSA_DK), lambda b, kv, h: (kv, 0, 0))],
        out_specs=pl.BlockSpec((1, 1, 1, nh, NSA_DK), lambda b, kv, h: (b, kv, h, 0, 0)),
        out_shape=jax.ShapeDtypeStruct((nb, 2, groups, nh, NSA_DK), BF16),
        compiler_params=_cparams(3),
        name="cmp_stage2",
    )(ac, pos8, w1a, w1b, w2)


def _rel_bucket(dist):
    n = jnp.maximum(dist, 0)
    nf = jnp.maximum(n, 1).astype(F32)
    scale = (REL_BUCKETS - REL_MAX_EXACT) / math.log(REL_MAX_DIST / REL_MAX_EXACT)
    large = REL_MAX_EXACT + (jnp.log(nf / REL_MAX_EXACT) * scale).astype(jnp.int32)
    large = jnp.minimum(large, REL_BUCKETS - 1)
    return jnp.where(n < REL_MAX_EXACT, n, large)


def _bias_by_dist(rel_table, n):
    tab = rel_table[_rel_bucket(jnp.arange(n))]
    return tab.T.reshape(NSA_KV_HEADS, NSA_GROUP, n)


def _pad_group_rows(t, axis):
    first = lax.slice_in_dim(t, 0, 1, axis=axis)
    return jnp.concatenate([t] + [first] * (SROWS - NSA_GROUP), axis=axis)


def _flash_init(m_ref, l_ref, acc_ref):
    m_ref[...] = jnp.full(m_ref.shape, NEG_INF, F32)
    l_ref[...] = jnp.zeros(l_ref.shape, F32)
    acc_ref[...] = jnp.zeros(acc_ref.shape, F32)


def _flash_step(s, v, m_ref, l_ref, acc_ref):
    m_old = m_ref[...]
    m_new = jnp.maximum(m_old, jnp.max(s, axis=1, keepdims=True))
    alpha = jnp.exp(m_old - m_new)
    p = jnp.exp(s - m_new)
    l_ref[...] = alpha * l_ref[...] + jnp.sum(p, axis=1, keepdims=True)
    acc_ref[...] = alpha * acc_ref[...] + _dot(p.astype(BF16), v)
    m_ref[...] = m_new


def _flash_result(l_ref, acc_ref):
    return acc_ref[...] / jnp.maximum(l_ref[...], 1e-30)


def _select_blocks(imp_t, q0, ns):
    shape = imp_t.shape
    blk = _iota(shape, 0)
    qpos = q0 + _iota(shape, 1)
    cur = qpos >> 6
    valid = blk * SEL_BLOCK <= qpos
    forced = (blk == 0) | (blk == cur) | (blk == cur - 1)
    imp_t = jnp.where(valid, jnp.where(forced, FORCE_SCORE, imp_t), NEG_INF)
    rank = jnp.zeros(shape, F32)
    for other in range(ns):
        row = imp_t[other:other + 1, :]
        ahead = (row > imp_t) | ((row == imp_t) & (blk > other))
        rank = rank + jnp.where(ahead, 1.0, 0.0)
    return jnp.where((rank < SEL_TOPK) & valid, 1.0, 0.0)


def _nsa_prompt_kernel(q_ref, ks_ref, vs_ref, kw_ref, vw_ref, kc_ref, vct_ref, wd_ref, cfar_ref, basec_ref,
                       ovt_ref, ng_ref, nsl_ref, o_ref, m_ref, l_ref, acc_ref, vst_ref, vwt_ref, tz_ref, ow_ref, *, ns):
    i = pl.program_id(2)
    q0 = i * TQ
    qall = q_ref[...] * ((NSA_DK ** -0.5) * LOG2E)
    qt = jnp.concatenate([qall[:, g * NSA_DK:(g + 1) * NSA_DK].T for g in range(NSA_GROUP)], axis=1).astype(BF16)
    c_loc = _iota((TK, NSA_ROWS), 0)
    r_loc = _iota((TK, NSA_ROWS), 1) & (TQ - 1)

    @pl.when(i == 0)
    def _():
        for kt in range(vst_ref.shape[0]):
            vst_ref[kt] = vs_ref[kt * TK:(kt + 1) * TK, :].T.astype(BF16)
            vwt_ref[kt] = vw_ref[kt * TK:(kt + 1) * TK, :].T.astype(BF16)
        below = _iota((TK, TQ), 1) >= _iota((TK, TQ), 0)
        for g in range(NSA_GROUP):
            lo = pltpu.roll(jnp.broadcast_to(wd_ref[0, 2 * g:2 * g + 1, :], (TK, TQ)), 0, 1, stride=1, stride_axis=0)
            hi = pltpu.roll(jnp.broadcast_to(wd_ref[0, 2 * g + 1:2 * g + 2, :], (TK, TQ)), 0, 1, stride=1, stride_axis=0)
            tz_ref[0, :, g * TQ:(g + 1) * TQ] = lo
            tz_ref[1, :, g * TQ:(g + 1) * TQ] = jnp.where(below, hi, lo)

    ncp = kc_ref.shape[3]
    shift = (TQ // CMP_STRIDE) * i
    bias_c = basec_ref[0, pl.ds(pl.multiple_of(ncp - shift, TQ // CMP_STRIDE), ncp), :]
    s = _dot(kc_ref[0, 0, 0], qt) + bias_c
    m = jnp.max(s, axis=0, keepdims=True)
    e = jnp.exp2(s - m)
    inv = jnp.where(m > 0.5 * NEG_INF, 1.0 / jnp.maximum(jnp.sum(e, axis=0, keepdims=True), 1e-30), 0.0)
    p = e * inv
    o_cmp = _dot(vct_ref[0, 0], p.astype(BF16))

    psum = p[:, 0:TQ] + p[:, TQ:2 * TQ] + p[:, 2 * TQ:3 * TQ] + p[:, 3 * TQ:4 * TQ]
    hi = psum.astype(BF16)
    lo = (psum - hi.astype(F32)).astype(BF16)
    ovt = ovt_ref[...]
    imp_t = _dot(ovt, hi) + _dot(ovt, lo)
    ns8 = -(-ns // 8) * 8
    sel_t = _select_blocks(imp_t[:ns8], q0, ns)
    assert ns8 + 2 <= LANE
    sel_neg = jnp.concatenate([jnp.where(sel_t > 0.5, 0.0, NEG_INF)] * NSA_GROUP, axis=1)
    cfar = cfar_ref[0]
    cfar_hi = cfar.astype(BF16).astype(F32)
    sub8 = _iota((8, NSA_ROWS), 0)
    cfar8 = jnp.where(sub8 == 0, cfar_hi, jnp.where(sub8 == 1, cfar - cfar_hi, 0.0))
    pad_rows = jnp.zeros((LANE - ns8 - 8, NSA_ROWS), F32)
    qx_far = jnp.concatenate([qt, jnp.concatenate([sel_neg, cfar8, pad_rows], axis=0).astype(BF16)], axis=0)
    qx_near = jnp.concatenate([qt, jnp.concatenate([sel_neg, jnp.zeros_like(cfar8), pad_rows], axis=0).astype(BF16)],
                              axis=0)
    lane_k = _iota((TK, LANE), 1)
    ones_k = (lane_k == ns8) | (lane_k == ns8 + 1)

    def flash_step(scs, vts):
        m_old = m_ref[...]
        m_new = m_old
        for sc in scs:
            m_new = jnp.maximum(m_new, jnp.max(sc, axis=0, keepdims=True))
        alpha = jnp.exp2(m_old - m_new)
        l_new = alpha * l_ref[...]
        acc = alpha * acc_ref[...]
        for sc, vt in zip(scs, vts):
            pt = jnp.exp2(sc - m_new)
            l_new = l_new + jnp.sum(pt, axis=0, keepdims=True)
            acc = acc + _dot(vt, pt.astype(BF16))
        l_ref[...] = l_new
        acc_ref[...] = acc
        m_ref[...] = m_new

    def sel_scores(kt, bias, causal):
        k = ks_ref[pl.ds(pl.multiple_of(kt * TK, TK), TK), :].astype(BF16)
        blk_of_key = (TK // SEL_BLOCK) * kt + (_iota((TK, LANE), 0) >> 6)
        extra = jnp.where((lane_k == blk_of_key) | ones_k, 1.0, 0.0).astype(BF16)
        kx = jnp.concatenate([k, extra], axis=1)
        sc = _dot(kx, qx_far) if bias is None else _dot(kx, qx_near) + bias
        if causal:
            sc = jnp.where(c_loc <= r_loc, sc, NEG_INF)
        return sc

    def softmax_tiles(scs, vts):
        m_new = jnp.max(scs[0], axis=0, keepdims=True)
        for sc in scs[1:]:
            m_new = jnp.maximum(m_new, jnp.max(sc, axis=0, keepdims=True))
        l_new, acc = None, None
        for sc, vt in zip(scs, vts):
            pt = jnp.exp2(sc - m_new)
            l_t, acc_t = jnp.sum(pt, axis=0, keepdims=True), _dot(vt, pt.astype(BF16))
            l_new, acc = (l_t, acc_t) if l_new is None else (l_new + l_t, acc + acc_t)
        return m_new, l_new, acc

    kt_prev = jnp.maximum(i - 1, 0)
    prev_scores = jnp.where(i >= 1, sel_scores(kt_prev, tz_ref[1], False), NEG_INF)
    m_ref[...], l_ref[...], acc_ref[...] = softmax_tiles([sel_scores(i, tz_ref[0], True), prev_scores],
                                                         [vst_ref[i], vst_ref[kt_prev]])

    def win_scores(off):
        kt = jnp.maximum(i - off, 0)
        k = kw_ref[pl.ds(pl.multiple_of(kt * TK, TK), TK), :].astype(BF16)
        if off < 2:
            sc = _dot(k, qt) + tz_ref[off]
        else:
            sc = _dot(jnp.concatenate([k, jnp.where(ones_k, 1.0, 0.0).astype(BF16)], axis=1), qx_far)
        if off == 0:
            return jnp.where(c_loc <= r_loc, sc, NEG_INF)
        keep = (c_loc > r_loc) & (i >= off) if off * TK == WINDOW else (i >= off)
        return jnp.where(keep, sc, NEG_INF)

    n_win = WINDOW // TK + 1
    _, l_win, acc_win = softmax_tiles([win_scores(off) for off in range(n_win)],
                                      [vwt_ref[jnp.maximum(i - off, 0)] for off in range(n_win)])
    ow_ref[...] = acc_win / jnp.maximum(l_win, 1e-30)

    def far_tiles(kt_first, count):
        kts = [kt_first - t for t in range(count)]
        flash_step([sel_scores(kt, None, False) for kt in kts], [vst_ref[kt] for kt in kts])

    def far_group(j, carry):
        far_tiles(i - 2 - FAR_GROUP * j, FAR_GROUP)
        return carry

    n_far = jnp.maximum(i - 1, 0)
    lax.fori_loop(0, n_far // FAR_GROUP, far_group, 0)
    for rest in range(1, FAR_GROUP):
        pl.when(n_far % FAR_GROUP == rest)(functools.partial(far_tiles, rest - 1, rest))

    o_sel = _flash_result(l_ref, acc_ref)
    o_win = ow_ref[...]

    gates_t = _sigmoid(ng_ref[...]).T
    nsl = nsl_ref[...]
    outs = []
    for g in range(NSA_GROUP):
        cols = slice(g * TQ, (g + 1) * TQ)
        o_t = (gates_t[3 * g:3 * g + 1] * o_cmp[:, cols] + gates_t[3 * g + 1:3 * g + 2] * o_sel[:, cols]
               + gates_t[3 * g + 2:3 * g + 3] * o_win[:, cols])
        x = nsl[:, g * NSA_DK:(g + 1) * NSA_DK]
        outs.append(o_t.T * (x * _sigmoid(x)))
    o_ref[...] = jnp.concatenate(outs, axis=1).astype(BF16)


def _lanes_by_head(t):
    hk, g, rows, tq = t.shape
    return t.transpose(0, 2, 1, 3).reshape(hk, rows, g * tq)


def _nsa_prompt(z, zkv, zb, kcvc, bias_d, batch, seq):
    nq = seq // TQ
    ns = seq // SEL_BLOCK
    ncp = kcvc.shape[3]
    nsp = LANE
    wn = TQ // CMP_STRIDE
    assert TQ == TK and ns <= nsp and ncp >= seq // CMP_STRIDE and ncp % LANE == 0 and ncp > wn
    gw = NSA_GROUP * NSA_DK
    n_dist = bias_d.shape[-1]
    assert n_dist >= 2 * TK + TQ
    bias_d = bias_d * LOG2E

    tz = bias_d[..., :2 * TQ].reshape(NSA_KV_HEADS, 2 * NSA_GROUP, TQ)
    far = jnp.broadcast_to(bias_d[..., REL_MAX_DIST][:, :, None, None], (NSA_KV_HEADS, NSA_GROUP, 1, TQ))
    cfar = _lanes_by_head(far)
    half = n_dist // 2
    start = CMP_STRIDE * wn - (CMP_BLOCK - 1)
    assert start + TQ <= half and 2 * wn * CMP_STRIDE - start <= half
    w_ext = jnp.concatenate([bias_d[..., :half], jnp.full(bias_d.shape[:-1] + (n_dist - half,), NEG_INF, F32)], -1)
    near = jnp.tile(w_ext, (1, 1, 2 * wn + 1))[..., :2 * wn * (n_dist - CMP_STRIDE)]
    near = near.reshape(NSA_KV_HEADS, NSA_GROUP, 2 * wn, n_dist - CMP_STRIDE)[..., start:start + TQ]
    basec = jnp.concatenate([jnp.broadcast_to(far, (NSA_KV_HEADS, NSA_GROUP, ncp - wn, TQ)), near,
                             jnp.full((NSA_KV_HEADS, NSA_GROUP, ncp - wn, TQ), NEG_INF, F32)], axis=2)
    basec = _lanes_by_head(basec)
    sblk = jnp.arange(nsp)[:, None]
    nblk = jnp.arange(ncp)[None, :]
    ovt = ((nblk >= 4 * sblk - 1) & (nblk <= 4 * sblk + 3)).astype(BF16)

    vct = kcvc[:, 1].transpose(0, 1, 3, 2)

    def kvspec(col0, which):
        return pl.BlockSpec((seq, NSA_DK),
                            lambda b, h, i: (b, (col0 - COL_KVC) // NSA_DK + which * NSA_KV_HEADS + h))

    vt_scratch = pltpu.VMEM((seq // TK, NSA_DK, TK), BF16)
    return pl.pallas_call(
        functools.partial(_nsa_prompt_kernel, ns=ns),
        grid=(batch, NSA_KV_HEADS, nq),
        in_specs=[pl.BlockSpec((TQ, gw), lambda b, h, i: (b * nq + i, COL_NQ // gw + h)),
                  kvspec(COL_KVS, 0), kvspec(COL_KVS, 1), kvspec(COL_KVW, 0), kvspec(COL_KVW, 1),
                  pl.BlockSpec((1, 1, 1, ncp, NSA_DK), lambda b, h, i: (b, 0, h, 0, 0)),
                  pl.BlockSpec((1, 1, NSA_DK, ncp), lambda b, h, i: (b, h, 0, 0)),
                  pl.BlockSpec((1, 2 * NSA_GROUP, TQ), lambda b, h, i: (h, 0, 0)),
                  pl.BlockSpec((1, 1, NSA_ROWS), lambda b, h, i: (h, 0, 0)),
                  pl.BlockSpec((1, 2 * ncp, NSA_ROWS), lambda b, h, i: (h, 0, 0)),
                  pl.BlockSpec((nsp, ncp), lambda b, h, i: (0, 0)),
                  pl.BlockSpec((TQ, LANE), lambda b, h, i: (b * nq + i, COL_NG // LANE + h)),
                  pl.BlockSpec((TQ, gw), lambda b, h, i: (b * nq + i, COL_NSL // gw + h))],
        out_specs=pl.BlockSpec((TQ, gw), lambda b, h, i: (b * nq + i, h)),
        out_shape=jax.ShapeDtypeStruct((batch * seq, NSA_W), BF16),
        scratch_shapes=[pltpu.VMEM((1, NSA_ROWS), F32), pltpu.VMEM((1, NSA_ROWS), F32),
                        pltpu.VMEM((NSA_DK, NSA_ROWS), F32), vt_scratch, vt_scratch,
                        pltpu.VMEM((2, TK, NSA_ROWS), F32), pltpu.VMEM((NSA_DK, NSA_ROWS), F32)],
        compiler_params=_cparams(3),
        name="nsa_prompt",
    )(z, zkv, zkv, zkv, zkv, kcvc, vct, tz, cfar, basec, ovt, zb, zb)


SROWS = 8
SEL_PER_STEP = 2


def _stack_group_q(q_row):
    heads = [q_row[:, g * NSA_DK:(g + 1) * NSA_DK] for g in range(NSA_GROUP)]
    return jnp.concatenate(heads + [heads[0]] * (SROWS - NSA_GROUP), axis=0)


def _nsa_sample_cmp_kernel(q_ref, kc_ref, vc_ref, bias_ref, ov_ref, o_ref, idx_ref, *, ns):
    scale = NSA_DK ** -0.5
    qs = _stack_group_q(q_ref[0]).astype(BF16)
    s = _nt(qs, kc_ref[0, 0]) * scale + bias_ref[0]
    m = jnp.max(s, axis=1, keepdims=True)
    e = jnp.exp(s - m)
    inv = jnp.where(m > 0.5 * NEG_INF, 1.0 / jnp.maximum(jnp.sum(e, axis=1, keepdims=True), 1e-30), 0.0)
    p = e * inv
    o_ref[0, 0] = _dot(p.astype(BF16), vc_ref[0, 0])
    psum = jnp.broadcast_to(jnp.sum(p[0:NSA_GROUP], axis=0, keepdims=True), p.shape)
    hi = psum.astype(BF16)
    lo = (psum - hi.astype(F32)).astype(BF16)
    imp = (_dot(hi, ov_ref[...]) + _dot(lo, ov_ref[...]))[0:1]
    nsp = imp.shape[1]
    blk_r = _iota((1, nsp), 1)
    cur = ns - 1
    forced = (blk_r == 0) | (blk_r == cur) | (blk_r == cur - 1)
    imp = jnp.where(blk_r < ns, jnp.where(forced, FORCE_SCORE, imp), 2.0 * NEG_INF)
    imp_c = _column_of(imp)
    i_r = _iota((nsp, nsp), 1)
    j_c = _iota((nsp, nsp), 0)
    ahead = (imp > imp_c) | ((imp == imp_c) & (i_r < j_c))
    rank_c = jnp.sum(jnp.where(ahead, 1.0, 0.0), axis=1, keepdims=True)
    slot = _iota((nsp, LANE), 1).astype(F32)
    picks = jnp.where(rank_c == slot, _iota((nsp, LANE), 0).astype(F32), 0.0)
    idx_ref[0, 0] = jnp.broadcast_to(jnp.sum(picks, axis=0, keepdims=True), (SROWS, LANE)).astype(jnp.int32)


def _nsa_sample_cmp(z3, kcvc, bias_c, ov, ns):
    nb = z3.shape[0]
    ncp = kcvc.shape[2]
    nsp = ov.shape[1]
    gw = NSA_GROUP * NSA_DK

    def cspec(which):
        return pl.BlockSpec((1, 1, ncp, NSA_DK), lambda b, h: (b, which, 0, h))

    return pl.pallas_call(
        functools.partial(_nsa_sample_cmp_kernel, ns=ns),
        grid=(nb, NSA_KV_HEADS),
        in_specs=[pl.BlockSpec((1, 1, gw), lambda b, h: (b, 0, COL_NQ // gw + h)),
                  cspec(0), cspec(1),
                  pl.BlockSpec((1, SROWS, ncp), lambda b, h: (h, 0, 0)),
                  pl.BlockSpec((ncp, nsp), lambda b, h: (0, 0))],
        out_specs=[pl.BlockSpec((1, 1, SROWS, NSA_DK), lambda b, h: (b, h, 0, 0)),
                   pl.BlockSpec((1, 1, SROWS, LANE), lambda b, h: (b, h, 0, 0))],
        out_shape=[jax.ShapeDtypeStruct((nb, NSA_KV_HEADS, SROWS, NSA_DK), F32),
                   jax.ShapeDtypeStruct((nb, NSA_KV_HEADS, SROWS, LANE), jnp.int32)],
        compiler_params=_cparams(2),
        name="nsa_sample_cmp",
    )(z3, kcvc, kcvc, bias_c, ov)


def _nsa_sample_sel_kernel(pt_ref, idx_ref, q_ref, *refs, ns):
    n_blk = NSA_KV_HEADS * SEL_PER_STEP
    blocks, (new_ref,), biases = refs[:n_blk], refs[n_blk:n_blk + 1], refs[n_blk + 1:2 * n_blk + 1]
    o_ref, m_ref, l_ref, acc_ref = refs[2 * n_blk + 1:]
    b, t = pl.program_id(0), pl.program_id(1)
    gw = NSA_GROUP * NSA_DK
    rows_kv = 2 * NSA_KV_HEADS
    width = SEL_BLOCK * rows_kv

    @pl.when(t == 0)
    def _():
        _flash_init(m_ref, l_ref, acc_ref)

    new_rows = jnp.concatenate([new_ref[0]] * SEL_BLOCK, axis=0)
    xs, scs = [], []
    for h in range(NSA_KV_HEADS):
        qs = _stack_group_q(q_ref[0][:, h * gw:(h + 1) * gw]).astype(BF16)
        row = []
        for u in range(SEL_PER_STEP):
            r = h * SEL_PER_STEP + u
            is_new = idx_ref[b, h, t * SEL_PER_STEP + u] == ns - 1
            x = jnp.where(is_new, new_rows, blocks[r][0].reshape(width, NSA_DK)).astype(BF16)
            xs.append(x)
            row.append(_nt(qs, x) * (NSA_DK ** -0.5) + biases[r][0, 0])
        scs.append(jnp.concatenate(row, axis=1))
    sc = jnp.concatenate(scs, axis=0)
    m_old = m_ref[...]
    m_new = jnp.maximum(m_old, jnp.max(sc, axis=1, keepdims=True))
    alpha = jnp.exp(m_old - m_new)
    p = jnp.exp(sc - m_new)
    l_ref[...] = alpha * l_ref[...] + jnp.sum(p, axis=1, keepdims=True)
    pv = pltpu.roll(p, NSA_KV_HEADS, 1).astype(BF16)
    acc = alpha * acc_ref[...]
    upd = []
    for h in range(NSA_KV_HEADS):
        ph = pv[h * SROWS:(h + 1) * SROWS]
        upd.append(sum(_dot(ph[:, u * width:(u + 1) * width], xs[h * SEL_PER_STEP + u]) for u in range(SEL_PER_STEP)))
    acc_ref[...] = acc + jnp.concatenate(upd, axis=0)
    m_ref[...] = m_new

    @pl.when(t == pl.num_programs(1) - 1)
    def _():
        o_ref[0] = (acc_ref[...] / jnp.maximum(l_ref[...], 1e-30)).reshape(NSA_KV_HEADS, SROWS, NSA_DK)


def _nsa_sample_sel(z3, cache, kv_new, page_table, idx, bias_sel, ns):
    nb, n_pages = page_table.shape
    n_sel = idx.shape[2]
    halves = PAGE_SIZE // SEL_BLOCK
    rows_kv = 2 * NSA_KV_HEADS

    per_step = SEL_PER_STEP
    assert n_sel % per_step == 0

    def blockspec(h, u):
        def index(b, t, pt, ix):
            blk = ix[b, h, t * per_step + u]
            return (pt[b, jnp.minimum(blk // halves, n_pages - 1)], blk % halves, 0, 0)
        return pl.BlockSpec((1, SEL_BLOCK, rows_kv, NSA_DK), index)

    def biasspec(h, u):
        return pl.BlockSpec((1, 1, SROWS, SEL_BLOCK * rows_kv),
                            lambda b, t, pt, ix: (h, ix[b, h, t * per_step + u], 0, 0))

    slots = [(h, u) for h in range(NSA_KV_HEADS) for u in range(per_step)]
    return pl.pallas_call(
        functools.partial(_nsa_sample_sel_kernel, ns=ns),
        grid_spec=pltpu.PrefetchScalarGridSpec(
            num_scalar_prefetch=2,
            grid=(nb, n_sel // per_step),
            in_specs=[pl.BlockSpec((1, 1, NSA_W), lambda b, t, pt, ix: (b, 0, COL_NQ // NSA_W))]
                     + [blockspec(h, u) for h, u in slots]
                     + [pl.BlockSpec((1, rows_kv, NSA_DK), lambda b, t, pt, ix: (b, 0, 0))]
                     + [biasspec(h, u) for h, u in slots],
            out_specs=pl.BlockSpec((1, NSA_KV_HEADS, SROWS, NSA_DK), lambda b, t, pt, ix: (b, 0, 0, 0)),
            scratch_shapes=[pltpu.VMEM((NSA_KV_HEADS * SROWS, 1), F32), pltpu.VMEM((NSA_KV_HEADS * SROWS, 1), F32),
                            pltpu.VMEM((NSA_KV_HEADS * SROWS, NSA_DK), F32)]),
        out_shape=jax.ShapeDtypeStruct((nb, NSA_KV_HEADS, SROWS, NSA_DK), F32),
        compiler_params=_cparams(2),
        name="nsa_sample_sel",
    )(page_table, idx, z3, *([cache] * len(slots)), kv_new, *([bias_sel] * len(slots)))


def _nsa_sample_win_kernel(q_ref, k_ref, v_ref, kn_ref, vn_ref, bias_ref, bnew_ref, o_ref):
    scale = NSA_DK ** -0.5
    q = _stack_group_q(q_ref[0])
    s_buf = _nt(q.astype(BF16), k_ref[0].astype(BF16)) * scale + bias_ref[0]
    s_new = jnp.sum(q * kn_ref[0], axis=1, keepdims=True) * scale + bnew_ref[0]
    m = jnp.maximum(jnp.max(s_buf, axis=1, keepdims=True), s_new)
    p_buf = jnp.exp(s_buf - m)
    p_new = jnp.exp(s_new - m)
    l = jnp.sum(p_buf, axis=1, keepdims=True) + p_new
    acc = _dot(p_buf.astype(BF16), v_ref[0].astype(BF16)) + p_new * vn_ref[0]
    o_ref[0, 0] = acc / jnp.maximum(l, 1e-30)


def _nsa_sample_win(z3, win_buf, bias_win, bias_new):
    nb, nbuf, _ = win_buf.shape
    gw = NSA_GROUP * NSA_DK

    def bufspec(which):
        return pl.BlockSpec((1, nbuf, NSA_DK), lambda b, h: (b, 0, which * NSA_KV_HEADS + h))

    def newspec(which):
        return pl.BlockSpec((1, 1, NSA_DK), lambda b, h: (b, 0, COL_KVW // NSA_DK + which * NSA_KV_HEADS + h))

    return pl.pallas_call(
        _nsa_sample_win_kernel,
        grid=(nb, NSA_KV_HEADS),
        in_specs=[pl.BlockSpec((1, 1, gw), lambda b, h: (b, 0, COL_NQ // gw + h)),
                  bufspec(0), bufspec(1), newspec(0), newspec(1),
                  pl.BlockSpec((1, SROWS, nbuf), lambda b, h: (h, 0, 0)),
                  pl.BlockSpec((1, SROWS, 1), lambda b, h: (h, 0, 0))],
        out_specs=pl.BlockSpec((1, 1, SROWS, NSA_DK), lambda b, h: (b, h, 0, 0)),
        out_shape=jax.ShapeDtypeStruct((nb, NSA_KV_HEADS, SROWS, NSA_DK), F32),
        compiler_params=_cparams(2),
        name="nsa_sample_win",
    )(z3, win_buf, win_buf, z3, z3, bias_win, bias_new)


def _nsa_sample_gate_kernel(oc_ref, os_ref, ow_ref, ng_ref, nsl_ref, o_ref):
    gates = _sigmoid(ng_ref[0])
    nsl = nsl_ref[0]
    outs = []
    for g in range(NSA_GROUP):
        o = (gates[:, 3 * g:3 * g + 1] * oc_ref[0, 0, g:g + 1] + gates[:, 3 * g + 1:3 * g + 2] * os_ref[0, 0, g:g + 1]
             + gates[:, 3 * g + 2:3 * g + 3] * ow_ref[0, 0, g:g + 1])
        x = nsl[:, g * NSA_DK:(g + 1) * NSA_DK]
        outs.append(o * (x * _sigmoid(x)))
    o_ref[0] = jnp.concatenate(outs, axis=1).astype(BF16)


def _nsa_sample_gate(o_cmp, o_sel, o_win, z3b):
    nb = z3b.shape[0]
    gw = NSA_GROUP * NSA_DK
    ospec = pl.BlockSpec((1, 1, SROWS, NSA_DK), lambda b, h: (b, h, 0, 0))
    return pl.pallas_call(
        _nsa_sample_gate_kernel,
        grid=(nb, NSA_KV_HEADS),
        in_specs=[ospec, ospec, ospec,
                  pl.BlockSpec((1, 1, LANE), lambda b, h: (b, 0, COL_NG // LANE + h)),
                  pl.BlockSpec((1, 1, gw), lambda b, h: (b, 0, COL_NSL // gw + h))],
        out_specs=pl.BlockSpec((1, 1, gw), lambda b, h: (b, 0, h)),
        out_shape=jax.ShapeDtypeStruct((nb, 1, NSA_W), BF16),
        compiler_params=_cparams(2),
        name="nsa_sample_gate",
    )(o_cmp, o_sel, o_win, z3b, z3b)


def _mem_heads(q, kv):
    outs = []
    for h in range(MEM_HEADS):
        k = kv[:, h * MEM_DH:(h + 1) * MEM_DH].astype(BF16)
        v = kv[:, MEM_W + h * MEM_DH:MEM_W + (h + 1) * MEM_DH].astype(BF16)
        s = _nt(q[:, h * MEM_DH:(h + 1) * MEM_DH].astype(BF16), k) * (MEM_DH ** -0.5)
        e = jnp.exp(s - jnp.max(s, axis=1, keepdims=True))
        p = e / jnp.sum(e, axis=1, keepdims=True)
        outs.append(_dot(p.astype(BF16), v))
    return jnp.concatenate(outs, axis=1)


def _mem_prompt_kernel(q_ref, kv_ref, o_ref):
    o_ref[...] = _mem_heads(q_ref[:, :MEM_W], kv_ref[...]).astype(BF16)


def _mem_prompt(z, mem_kv, batch, seq, tq):
    nq = seq // tq
    n_mem = mem_kv.shape[0] // batch
    return pl.pallas_call(
        _mem_prompt_kernel,
        grid=(batch, nq),
        in_specs=[pl.BlockSpec((tq, MQ_BLOCK), lambda b, i: (b * nq + i, COL_MQ // MQ_BLOCK)),
                  pl.BlockSpec((n_mem, 2 * MEM_W), lambda b, i: (b, 0))],
        out_specs=pl.BlockSpec((tq, MEM_W), lambda b, i: (b * nq + i, 0)),
        out_shape=jax.ShapeDtypeStruct((batch * seq, MEM_W), BF16),
        compiler_params=_cparams(2),
        name="mem_prompt",
    )(z, mem_kv)


def _mem_sample_kernel(q_ref, kv_ref, o_ref):
    q = jnp.broadcast_to(q_ref[0][:, :MEM_W], (SROWS, MEM_W))
    o_ref[0] = _mem_heads(q, kv_ref[0])[0:1].astype(BF16)


def _mem_sample(z3, mem_kv):
    nb, n_mem, _ = mem_kv.shape
    return pl.pallas_call(
        _mem_sample_kernel,
        grid=(nb,),
        in_specs=[pl.BlockSpec((1, 1, MQ_BLOCK), lambda b: (b, 0, COL_MQ // MQ_BLOCK)),
                  pl.BlockSpec((1, n_mem, 2 * MEM_W), lambda b: (b, 0, 0))],
        out_specs=pl.BlockSpec((1, 1, MEM_W), lambda b: (b, 0, 0)),
        out_shape=jax.ShapeDtypeStruct((nb, 1, MEM_W), BF16),
        compiler_params=_cparams(1),
        name="mem_sample",
    )(z3, mem_kv)


def _merge_kernel(ar_ref, an_ref, am_ref, wr_ref, wn_ref, wm_ref, g0_ref, g1_ref, g2_ref, o_ref):
    merged = (_sigmoid(g0_ref[...]) * _dot(ar_ref[...], wr_ref[...])
              + _sigmoid(g1_ref[...]) * _dot(an_ref[...], wn_ref[...])
              + _sigmoid(g2_ref[...]) * _dot(am_ref[...], wm_ref[...]))
    o_ref[...] = merged.astype(BF16)


def _merge(a_ret, a_nsa, a_mem, w_ret, w_nsa, w_mem, z, tm, tn):
    m = a_ret.shape[0]
    nt = D_MODEL // tn

    def aspec(width):
        return pl.BlockSpec((tm, width), lambda i, j: (i, 0))

    def wspec(width):
        return pl.BlockSpec((width, tn), lambda i, j: (0, j))

    def gspec(branch):
        return pl.BlockSpec((tm, tn), lambda i, j: (i, COL_MG // tn + branch * nt + j))

    return pl.pallas_call(
        _merge_kernel,
        grid=(m // tm, nt),
        in_specs=[aspec(RET_W), aspec(NSA_W), aspec(MEM_W), wspec(RET_W), wspec(NSA_W), wspec(MEM_W),
                  gspec(0), gspec(1), gspec(2)],
        out_specs=pl.BlockSpec((tm, tn), lambda i, j: (i, j)),
        out_shape=jax.ShapeDtypeStruct((m, D_MODEL), BF16),
        compiler_params=_cparams(2),
        name="merge",
    )(a_ret, a_nsa, a_mem, w_ret, w_nsa, w_mem, z, z, z)


def _out_kernel(a_ref, w_ref, x_ref, g_ref, o_ref):
    out = _dot(a_ref[...], w_ref[...])
    y = out * lax.rsqrt(jnp.mean(out * out, axis=-1, keepdims=True) + EPS)
    o_ref[...] = x_ref[...] + y * g_ref[...]


def _out_proj(merged, w_out, x, norm_post, tm):
    m = merged.shape[0]
    return pl.pallas_call(
        _out_kernel,
        grid=(m // tm,),
        in_specs=[pl.BlockSpec((tm, D_MODEL), lambda i: (i, 0)),
                  pl.BlockSpec((D_MODEL, D_MODEL), lambda i: (0, 0)),
                  pl.BlockSpec((tm, D_MODEL), lambda i: (i, 0)),
                  pl.BlockSpec((1, D_MODEL), lambda i: (0, 0))],
        out_specs=pl.BlockSpec((tm, D_MODEL), lambda i: (i, 0)),
        out_shape=jax.ShapeDtypeStruct((m, D_MODEL), F32),
        compiler_params=_cparams(1),
        name="out_proj",
    )(merged, w_out, x, norm_post.reshape(1, D_MODEL))


def _layout_w_ng(w_t):
    per_group = N_BRANCHES * NSA_GROUP
    ng = w_t[PROJ_A:PROJ_A + N_BRANCHES * NSA_HEADS].reshape(NSA_KV_HEADS, per_group, D_MODEL)
    return jnp.pad(ng, ((0, 0), (0, LANE - per_group), (0, 0))).reshape(NG_SLOT, D_MODEL)


def _pick_tile(m, cap):
    t = min(m, cap)
    while m % t:
        t //= 2
    return t


def kernel(x_prompt, x_sample, cache_cmp_kv, cache_sel_kv, cache_win_kv, state_ret, cache_mem_kv, page_table,
           mem_prompt, rel_table, norm_pre, norm_post, norm_mem, w_in, ret_norm, w_ret_up, cmp_pos, w_cmp1,
           w_cmp2, w_nsa_up, w_mem_kv, w_mem_up, w_out):
    batch, seq, _ = x_prompt.shape
    nb = x_sample.shape[0]
    assert x_sample.shape[1] == 1 and norm_pre.shape[0] == 1
    assert seq % TQ == 0 and seq >= WINDOW
    n_pool = cache_cmp_kv.shape[1]
    n_pages = page_table.shape[1]
    past = n_pages * PAGE_SIZE
    n_mem = mem_prompt.shape[1]
    assert n_pages % PAGES_PER_STEP == 0 and cache_win_kv.shape[2] == WINDOW

    w_a = w_in[0].T
    w_ng = _layout_w_ng(w_a)
    kw = CMP_STRIDE * NSA_DK
    w1 = w_cmp1[0].reshape(2, CMP_BLOCK * NSA_DK, NSA_DK).astype(BF16)
    w1a, w1b = w1[:, :kw], w1[:, kw:]
    w1ab = jnp.concatenate([w1a, w1b], axis=2)
    w2 = w_cmp2[0].astype(BF16)
    pos8 = jnp.pad(cmp_pos[0].reshape(2, 1, CMP_BLOCK * NSA_DK), ((0, 0), (0, 7), (0, 0)))
    w_ret = w_ret_up[0].astype(BF16)
    w_nsa = w_nsa_up[0].astype(BF16)
    w_mem = w_mem_up[0].astype(BF16)
    w_o = w_out[0].astype(BF16)

    m_p = batch * seq
    xp = x_prompt.reshape(m_p, D_MODEL)
    hp = _rmsnorm(xp, norm_pre[0], _pick_tile(m_p, 512))
    z = _proj(hp, w_a, _pick_tile(m_p, 1024), PROJ_TN, COL_KVC, transposed=True)
    zkv, kvc_rows, kvs_rows, kvw_rows = _proj_kv(hp, w_a, _pick_tile(seq, WINDOW), seq)
    zb = _proj_tail(hp, w_a, w_ng, _pick_tile(m_p, 1024))

    a_ret, ret_state_p = _retention_prompt(z, ret_norm[0], batch, seq)

    ncp = max(LANE, -(-(seq // CMP_STRIDE) // LANE) * LANE)
    ac = _cmp_stage1_dense(zkv, w1ab, batch, seq)
    kcvc = _cmp_stage2(ac, pos8, w1a, w1b, w2, 1)
    if ncp > kcvc.shape[3]:
        kcvc = jnp.pad(kcvc, ((0, 0), (0, 0), (0, 0), (0, ncp - kcvc.shape[3]), (0, 0)))
    bias_d = _bias_by_dist(rel_table, BIAS_DISTS)
    a_nsa = _nsa_prompt(z, zkv, zb, kcvc, bias_d, batch, seq)

    hm = _rmsnorm(mem_prompt.reshape(batch * n_mem, D_MODEL), norm_mem[0], _pick_tile(batch * n_mem, 512))
    mem_kv_p = _proj(hm, w_mem_kv[0], _pick_tile(batch * n_mem, 512), PROJ_TN)
    a_mem = _mem_prompt(zb, mem_kv_p, batch, seq, _pick_tile(seq, 512))

    merged = _merge(a_ret, a_nsa, a_mem, w_ret, w_nsa, w_mem, zb, _pick_tile(m_p, 1024), 512)
    y_p = _out_proj(merged, w_o, xp, norm_post[0], _pick_tile(m_p, 512)).reshape(batch, seq, D_MODEL)

    kv_shape = (1, batch, seq, 2, NSA_KV_HEADS, NSA_DK)
    new_cmp_p = kvc_rows.reshape(kv_shape)
    new_sel_p = kvs_rows.reshape(kv_shape)
    new_win_p = kvw_rows.reshape(1, batch, WINDOW, 2, NSA_KV_HEADS, NSA_DK)
    new_ret_p = ret_state_p[None]
    new_mem_p = mem_kv_p.reshape(1, batch, n_mem, 2, MEM_HEADS, MEM_DH)

    xs = x_sample.reshape(nb, D_MODEL)
    hs = _rmsnorm(xs, norm_pre[0], nb)
    zs = _proj(hs, w_a, nb, PROJ_TN, PROJ_A, transposed=True)
    zsb = _proj_tail(hs, w_a, w_ng, nb)
    z3 = zs.reshape(nb, 1, PROJ_A)
    z3b = zsb.reshape(nb, 1, PROJ_B)

    a_ret_s, ret_state_s = _retention_sample(z3, state_ret[0], ret_norm[0], past)

    cache_c = cache_cmp_kv[0].reshape(n_pool, PAGE_SIZE // CMP_STRIDE, CMP_STRIDE, 2 * NSA_KV_HEADS, NSA_DK)
    cache_s = cache_sel_kv[0].reshape(n_pool, PAGE_SIZE, 2 * NSA_KV_HEADS, NSA_DK)
    ac_s = _cmp_stage1_paged(cache_c, page_table, w1ab)
    kcvc_s = _cmp_stage2(ac_s, pos8, w1a, w1b, w2, NSA_KV_HEADS)
    ncs = past // CMP_STRIDE
    kcvc_s = kcvc_s.reshape(nb, 2, ncs, KV_W)
    ns_s = past // SEL_BLOCK + 1
    nsp_s = -(-ns_s // LANE) * LANE
    assert past >= WINDOW and past >= REL_MAX_DIST and BIAS_DISTS > WINDOW
    far_s = bias_d[..., REL_MAX_DIST:REL_MAX_DIST + 1]
    hg = (NSA_KV_HEADS, NSA_GROUP)
    n_valid = (past - (CMP_BLOCK - 1)) // CMP_STRIDE + 1
    strided = bias_d[..., (past - (CMP_BLOCK - 1)) % CMP_STRIDE::CMP_STRIDE]
    n_tab = strided.shape[-1]
    assert n_valid >= n_tab and n_tab * CMP_STRIDE > REL_MAX_DIST + CMP_STRIDE and ncs >= n_valid
    bias_cs = jnp.concatenate([jnp.broadcast_to(far_s, hg + (n_valid - n_tab,)), strided[..., ::-1],
                               jnp.full(hg + (ncs - n_valid,), NEG_INF, F32)], axis=-1)
    bias_cs = _pad_group_rows(bias_cs, 1)
    nblk = jnp.arange(ncs)[:, None]
    sblk = jnp.arange(nsp_s)[None, :]
    ov_s = ((nblk >= 4 * sblk - 1) & (nblk <= 4 * sblk + 3)).astype(BF16)
    o_cmp_s, idx_s = _nsa_sample_cmp(z3, kcvc_s, bias_cs, ov_s, ns_s)
    n_sel = min(SEL_TOPK, ns_s)
    idx = idx_s[:, :, 0, :n_sel]

    hg = (NSA_KV_HEADS, NSA_GROUP)
    n_key = ns_s * SEL_BLOCK
    bias_sel = jnp.concatenate([jnp.broadcast_to(far_s, hg + (past + 1 - REL_MAX_DIST,)),
                                bias_d[..., :REL_MAX_DIST][..., ::-1],
                                jnp.full(hg + (n_key - past - 1,), NEG_INF, F32)], axis=-1)
    bias_sel = _pad_group_rows(bias_sel.reshape(hg + (ns_s, SEL_BLOCK)).transpose(0, 2, 1, 3), 2)
    own_k = jnp.arange(2 * NSA_KV_HEADS)[None, :] == jnp.arange(NSA_KV_HEADS)[:, None]
    bias_sel = jnp.where(own_k[:, None, None, None, :], bias_sel[..., None], NEG_INF)
    bias_sel = bias_sel.reshape(NSA_KV_HEADS, ns_s, SROWS, SEL_BLOCK * 2 * NSA_KV_HEADS)
    kvs_new = zs[:, COL_KVS:COL_KVS + 2 * KV_W].reshape(nb, 2 * NSA_KV_HEADS, NSA_DK)
    o_sel_s = _nsa_sample_sel(z3, cache_s, kvs_new, page_table, idx, bias_sel, ns_s)

    win_buf = cache_win_kv[0].reshape(nb, WINDOW, 2 * KV_W)
    bias_w = jnp.concatenate([jnp.full(hg + (1,), NEG_INF, F32), bias_d[..., 1:WINDOW][..., ::-1]], axis=-1)
    o_win_s = _nsa_sample_win(z3, win_buf, _pad_group_rows(bias_w, 1), _pad_group_rows(bias_d[..., 0:1], 1))
    a_nsa_s = _nsa_sample_gate(o_cmp_s, o_sel_s, o_win_s, z3b)

    mem_kv_s = cache_mem_kv[0].reshape(nb, n_mem, 2 * MEM_W)
    a_mem_s = _mem_sample(z3b, mem_kv_s)

    merged_s = _merge(a_ret_s.reshape(nb, RET_W), a_nsa_s.reshape(nb, NSA_W), a_mem_s.reshape(nb, MEM_W),
                      w_ret, w_nsa, w_mem, zsb, nb, 512)
    y_s = _out_proj(merged_s, w_o, xs, norm_post[0], nb).reshape(nb, 1, D_MODEL)

    kvs_shape = (1, nb, 1, 2, NSA_KV_HEADS, NSA_DK)
    new_cmp_s = zs[:, COL_KVC:COL_KVC + 2 * KV_W].reshape(kvs_shape)
    new_sel_s = zs[:, COL_KVS:COL_KVS + 2 * KV_W].reshape(kvs_shape)
    kvw_s = zs[:, COL_KVW:COL_KVW + 2 * KV_W].reshape(nb, 1, 2, NSA_KV_HEADS, NSA_DK)
    new_win_s = jnp.concatenate([cache_win_kv[0][:, 1:], kvw_s], axis=1)[None]
    new_ret_s = ret_state_s[None]

    return (y_p, y_s, new_cmp_p, new_sel_p, new_win_p, new_ret_p, new_mem_p,
            new_cmp_s, new_sel_s, new_win_s, new_ret_s)
```

```python
import functools
import math

import jax
import jax.numpy as jnp
from jax import lax
from jax.experimental import pallas as pl
from jax.experimental.pallas import tpu as pltpu

F32 = jnp.float32
BF16 = jnp.bfloat16

D_MODEL = 2048
PAGE_SIZE = 128
RET_HEADS = 8
RET_DK = 256
RET_DV = 256
RET_CHUNK = 128
ROPE_BASE = 10000.0
NSA_HEADS = 16
NSA_KV_HEADS = 4
NSA_GROUP = NSA_HEADS // NSA_KV_HEADS
NSA_DK = 128
CMP_BLOCK = 32
CMP_STRIDE = 16
SEL_BLOCK = 64
SEL_TOPK = 16
WINDOW = 512
MEM_HEADS = 4
MEM_DH = 384
REL_BUCKETS = 32
REL_MAX_EXACT = 16
REL_MAX_DIST = 128
N_BRANCHES = 3
EPS = 1e-6
NEG_INF = -1e30
FORCE_SCORE = 1e4

RET_W = RET_HEADS * RET_DV
NSA_W = NSA_HEADS * NSA_DK
KV_W = NSA_KV_HEADS * NSA_DK
MEM_W = MEM_HEADS * MEM_DH

COL_RQ = 0
COL_RK = COL_RQ + RET_HEADS * RET_DK
COL_RV = COL_RK + RET_HEADS * RET_DK
COL_RG = COL_RV + RET_W
COL_NQ = COL_RG + RET_W
COL_KVC = COL_NQ + NSA_W
COL_KVS = COL_KVC + 2 * KV_W
COL_KVW = COL_KVS + 2 * KV_W
PROJ_A = COL_KVW + 2 * KV_W
COL_NSL = 0
COL_MG = COL_NSL + NSA_W
COL_MQ = COL_MG + N_BRANCHES * D_MODEL
COL_NG = COL_MQ + MEM_W
NG_SLOT = NSA_KV_HEADS * 128
PROJ_B = COL_NG + NG_SLOT
MQ_BLOCK = MEM_W + NG_SLOT

LOG2E = math.log2(math.e)
LANE = 128
TQ = 256
TK = 256
NSA_ROWS = NSA_GROUP * TQ
BIAS_DISTS = 1024
FAR_GROUP = 4
RET_STEP_CHUNKS = 8
PROJ_TN = 1024
VMEM_LIMIT = 56 * 1024 * 1024


def _cparams(n_axes):
    return pltpu.CompilerParams(dimension_semantics=("arbitrary",) * n_axes, vmem_limit_bytes=VMEM_LIMIT)


def _nt(a, b):
    return lax.dot_general(a, b, (((1,), (1,)), ((), ())), preferred_element_type=F32)


def _dot(a, b):
    return jnp.dot(a, b, preferred_element_type=F32)


def _sigmoid(x):
    return 1.0 / (1.0 + jnp.exp(-x))


def _iota(shape, dim):
    return lax.broadcasted_iota(jnp.int32, shape, dim)


def _rmsnorm_kernel(x_ref, g_ref, h_ref):
    x = x_ref[...]
    ms = jnp.mean(x * x, axis=-1, keepdims=True)
    h_ref[...] = ((x * lax.rsqrt(ms + EPS)) * g_ref[...]).astype(BF16)


def _rmsnorm(x, g, tm):
    m, k = x.shape
    return pl.pallas_call(
        _rmsnorm_kernel,
        grid=(m // tm,),
        in_specs=[pl.BlockSpec((tm, k), lambda i: (i, 0)), pl.BlockSpec((1, k), lambda i: (0, 0))],
        out_specs=pl.BlockSpec((tm, k), lambda i: (i, 0)),
        out_shape=jax.ShapeDtypeStruct((m, k), BF16),
        compiler_params=_cparams(1),
        name="rmsnorm",
    )(x, g.reshape(1, k))


def _proj_kernel(h_ref, w_ref, *refs, transposed):
    o_ref, wb_ref = refs[-3 if len(refs) == 4 else -2], refs[-1]

    @pl.when(pl.program_id(1) == 0)
    def _():
        if transposed:
            _store_transposed(wb_ref, w_ref, wb_ref.shape[1])
        else:
            wb_ref[...] = w_ref[...].astype(BF16)
        if len(refs) == 4:
            refs[2][...] = _dot(refs[0][...], wb_ref[...])

    o_ref[...] = _dot(h_ref[...], wb_ref[...])


def _rider_specs(rider, k, tn):
    if rider is None:
        return [], [], []
    rows = rider.shape[0]
    return ([pl.BlockSpec((rows, k), lambda j, i: (0, 0))], [pl.BlockSpec((rows, tn), lambda j, i: (0, j))], [rows])


def _proj(h, w, tm, tn, n=None, transposed=False, rider=None):
    m, k = h.shape
    n = w.shape[0 if transposed else 1] if n is None else n
    wspec = pl.BlockSpec((tn, k), lambda j, i: (j, 0)) if transposed else pl.BlockSpec((k, tn), lambda j, i: (0, j))
    r_in, r_out, r_rows = _rider_specs(rider, k, tn)
    out = pl.pallas_call(
        functools.partial(_proj_kernel, transposed=transposed),
        grid=(n // tn, m // tm),
        in_specs=[pl.BlockSpec((tm, k), lambda j, i: (i, 0)), wspec] + r_in,
        out_specs=[pl.BlockSpec((tm, tn), lambda j, i: (i, j))] + r_out,
        out_shape=[jax.ShapeDtypeStruct((m, n), F32)] + [jax.ShapeDtypeStruct((r, n), F32) for r in r_rows],
        scratch_shapes=[pltpu.VMEM((k, tn), BF16)],
        compiler_params=_cparams(2),
        name="proj",
    )(h, w, *([] if rider is None else [rider]))
    return out[0] if rider is None else tuple(out)


def _proj_kv_kernel(h_ref, w_ref, hs_ref, o_ref, oc_ref, os_ref, ow_ref, ors_ref, wb_ref, *, tiles_per_batch):
    j, i = pl.program_id(0), pl.program_id(1)
    n_tiles = pl.num_programs(0)

    @pl.when(i == 0)
    def _():
        _store_transposed(wb_ref, w_ref, wb_ref.shape[1])
        ors_ref[...] = _dot(hs_ref[...], wb_ref[...])

    res = _dot(h_ref[...], wb_ref[...])
    o_ref[...] = res
    tm, tn = res.shape
    rows_kv = tn // NSA_DK

    @pl.when(j == n_tiles - 3)
    def _():
        oc_ref[...] = res.reshape(tm * rows_kv, NSA_DK)

    @pl.when(j == n_tiles - 2)
    def _():
        os_ref[...] = res.reshape(tm * rows_kv, NSA_DK)

    @pl.when((j == n_tiles - 1) & (i % tiles_per_batch == tiles_per_batch - 1))
    def _():
        ow_ref[...] = res[tm - WINDOW:, :].reshape(WINDOW * rows_kv, NSA_DK)


def _proj_kv(h, w_t, tm, seq, rider):
    m, k = h.shape
    tn = 2 * KV_W
    assert COL_KVC % tn == 0 and COL_KVC == PROJ_A - 3 * tn and seq % tm == 0 and tm >= WINDOW
    n_j, n_i = 3, m // tm
    first = COL_KVC // tn
    tiles_per_batch = seq // tm
    rows_kv = tn // NSA_DK

    def kv_rows(tile):
        return lambda j, i: (jnp.where(j < tile, 0, jnp.where(j == tile, i, n_i - 1)), 0)

    r_in, r_out, (r_rows,) = _rider_specs(rider, k, tn)
    return pl.pallas_call(
        functools.partial(_proj_kv_kernel, tiles_per_batch=tiles_per_batch),
        grid=(n_j, n_i),
        in_specs=[pl.BlockSpec((tm, k), lambda j, i: (i, 0)),
                  pl.BlockSpec((tn, k), lambda j, i: (first + j, 0))] + r_in,
        out_specs=[pl.BlockSpec((tm, tn), lambda j, i: (i, j)),
                   pl.BlockSpec((tm * rows_kv, NSA_DK), kv_rows(n_j - 3)),
                   pl.BlockSpec((tm * rows_kv, NSA_DK), kv_rows(n_j - 2)),
                   pl.BlockSpec((WINDOW * rows_kv, NSA_DK),
                                lambda j, i: (jnp.where(j < n_j - 1, 0, i // tiles_per_batch), 0))] + r_out,
        out_shape=[jax.ShapeDtypeStruct((m, n_j * tn), F32),
                   jax.ShapeDtypeStruct((m * rows_kv, NSA_DK), F32),
                   jax.ShapeDtypeStruct((m * rows_kv, NSA_DK), F32),
                   jax.ShapeDtypeStruct((m // seq * WINDOW * rows_kv, NSA_DK), F32),
                   jax.ShapeDtypeStruct((r_rows, n_j * tn), F32)],
        scratch_shapes=[pltpu.VMEM((k, tn), BF16)],
        compiler_params=_cparams(2),
        name="proj_kv",
    )(h, w_t, rider)


TAIL_TN = 1024


def _store_transposed(dst_ref, src_ref, rows, chunk=256):
    for r0 in range(0, rows, chunk):
        dst_ref[:, r0:r0 + chunk] = src_ref[r0:r0 + chunk, :].T.astype(BF16)


def _proj_tail_kernel(h_ref, w_ref, wng_ref, hs_ref, o_ref, ors_ref, wb_ref, *, last_rows):
    tn = wb_ref.shape[1]
    is_last = pl.program_id(0) == pl.num_programs(0) - 1

    @pl.when((pl.program_id(1) == 0) & jnp.logical_not(is_last))
    def _():
        _store_transposed(wb_ref, w_ref, tn)
        ors_ref[...] = _dot(hs_ref[...], wb_ref[...])

    @pl.when((pl.program_id(1) == 0) & is_last)
    def _():
        _store_transposed(wb_ref, w_ref, last_rows)
        wb_ref[:, last_rows:] = wng_ref[...].T.astype(BF16)
        ors_ref[...] = _dot(hs_ref[...], wb_ref[...])

    o_ref[...] = _dot(h_ref[...], wb_ref[...])


def _proj_tail(h, w_t, w_ng, tm, rider):
    m, k = h.shape
    tn = TAIL_TN
    row_ng = PROJ_A
    row_nsl = row_ng + N_BRANCHES * NSA_HEADS
    row_mq = row_nsl + NSA_W
    row_mg = row_mq + MEM_W
    assert row_mg + N_BRANCHES * D_MODEL == w_t.shape[0] and w_ng.shape == (NG_SLOT, k)
    assert COL_NSL == 0 and COL_MG % tn == 0 and COL_MQ % tn == 0 and COL_NG + NG_SLOT == PROJ_B == COL_MQ + 2 * tn
    assert row_mq % 16 == 0 and row_nsl % 16 == 0 and row_mg % 16 == 0
    t_mg, t_mq = COL_MG // tn, COL_MQ // tn

    def w_row(j, i):
        row = jnp.where(j < t_mg, row_nsl + j * tn,
                        jnp.where(j < t_mq, row_mg + (j - t_mg) * tn, row_mq + (j - t_mq) * tn))
        return (pl.multiple_of(row, 16), 0)

    r_in, r_out, (r_rows,) = _rider_specs(rider, k, tn)
    return pl.pallas_call(
        functools.partial(_proj_tail_kernel, last_rows=COL_NG - COL_MQ - tn),
        grid=(PROJ_B // tn, m // tm),
        in_specs=[pl.BlockSpec((tm, k), lambda j, i: (i, 0)),
                  pl.BlockSpec((pl.Element(tn), pl.Element(k)), w_row),
                  pl.BlockSpec((NG_SLOT, k), lambda j, i: (0, 0))] + r_in,
        out_specs=[pl.BlockSpec((tm, tn), lambda j, i: (i, j))] + r_out,
        out_shape=[jax.ShapeDtypeStruct((m, PROJ_B), F32), jax.ShapeDtypeStruct((r_rows, PROJ_B), F32)],
        scratch_shapes=[pltpu.VMEM((k, tn), BF16)],
        compiler_params=_cparams(2),
        name="proj_tail",
    )(h, w_t, w_ng, rider)


def _rope_rows(x, cos, sin):
    half = x.shape[-1] // 2
    x1, x2 = x[:, :half], x[:, half:]
    return jnp.concatenate([x1 * cos - x2 * sin, x1 * sin + x2 * cos], axis=-1)


def _head_norm_gate(o, gnorm, rg):
    oc = o - jnp.mean(o, axis=-1, keepdims=True)
    y = oc * lax.rsqrt(jnp.mean(oc * oc, axis=-1, keepdims=True) + EPS) * gnorm
    return y * (rg * _sigmoid(rg))


def _ret_prompt_kernel(q_ref, k_ref, v_ref, rg_ref, cos_ref, sin_ref, dmat_ref, xi_ref, zeta_ref, gc_ref,
                       gn_ref, a_ref, s_ref):
    @pl.when(pl.program_id(2) == 0)
    def _():
        s_ref[...] = jnp.zeros_like(s_ref)

    c = RET_CHUNK
    for t in range(q_ref.shape[0] // c):
        rows = slice(t * c, (t + 1) * c)
        cos, sin = cos_ref[rows, :], sin_ref[rows, :]
        q = _rope_rows(q_ref[rows, :], cos, sin)
        k = _rope_rows(k_ref[rows, :], cos, sin) * (RET_DK ** -0.5)
        qb, vb = q.astype(BF16), v_ref[rows, :].astype(BF16)
        state = s_ref[0, 0]
        inner = _nt(qb, k.astype(BF16)) * dmat_ref[0]
        o = _dot(inner.astype(BF16), vb) + _dot(qb, state.astype(BF16)) * xi_ref[0]
        kz_t = (k * zeta_ref[0]).T.astype(BF16)
        s_ref[0, 0] = state * gc_ref[0] + _dot(kz_t, vb)
        a_ref[rows, :] = _head_norm_gate(o, gn_ref[...], rg_ref[rows, :]).astype(BF16)


def _decay_tables(chunk):
    log_g = jnp.log1p(-jnp.exp2(-5.0 - jnp.arange(RET_HEADS, dtype=F32)))
    i = jnp.arange(chunk, dtype=F32)
    diff = i[:, None] - i[None, :]
    dmat = jnp.where(diff >= 0, jnp.exp(log_g[:, None, None] * jnp.maximum(diff, 0.0)), 0.0)
    xi = jnp.exp(log_g[:, None] * (i[None, :] + 1.0))[:, :, None]
    zeta = jnp.exp(log_g[:, None] * (chunk - 1.0 - i[None, :]))[:, :, None]
    g_chunk = jnp.exp(log_g * chunk)[:, None, None]
    return dmat, xi, zeta, g_chunk


def _rope_tables(pos):
    half = RET_DK // 2
    freq = jnp.power(ROPE_BASE, -jnp.arange(half, dtype=F32) / half)
    ang = pos.astype(F32)[:, None] * freq[None, :]
    return jnp.cos(ang), jnp.sin(ang)


def _retention_prompt(z, ret_norm, batch, seq):
    c = RET_CHUNK
    rows = _pick_tile(seq, RET_STEP_CHUNKS * c)
    nc = seq // rows
    dmat, xi, zeta, g_chunk = _decay_tables(c)
    cos, sin = _rope_tables(jnp.arange(seq))
    hb = RET_DK

    def zspec(col0):
        return pl.BlockSpec((rows, hb), lambda b, h, t, col0=col0: (b * nc + t, col0 // hb + h))

    per_head = lambda shape: pl.BlockSpec((1,) + shape, lambda b, h, t: (h, 0, 0))
    return pl.pallas_call(
        _ret_prompt_kernel,
        grid=(batch, RET_HEADS, nc),
        in_specs=[zspec(COL_RQ), zspec(COL_RK), zspec(COL_RV), zspec(COL_RG),
                  pl.BlockSpec((rows, hb // 2), lambda b, h, t: (t, 0)),
                  pl.BlockSpec((rows, hb // 2), lambda b, h, t: (t, 0)),
                  per_head((c, c)), per_head((c, 1)), per_head((c, 1)), per_head((1, 1)),
                  pl.BlockSpec((1, hb), lambda b, h, t: (0, h))],
        out_specs=[pl.BlockSpec((rows, hb), lambda b, h, t: (b * nc + t, h)),
                   pl.BlockSpec((1, 1, RET_DK, RET_DV), lambda b, h, t: (b, h, 0, 0))],
        out_shape=[jax.ShapeDtypeStruct((batch * seq, RET_W), BF16),
                   jax.ShapeDtypeStruct((batch, RET_HEADS, RET_DK, RET_DV), F32)],
        compiler_params=_cparams(3),
        name="retention_prompt",
    )(z, z, z, z, cos, sin, dmat, xi, zeta, g_chunk, ret_norm.reshape(1, RET_W))


def _column_of(row):
    n = row.shape[1]
    eye = _iota((n, n), 0) == _iota((n, n), 1)
    return jnp.sum(jnp.where(eye, jnp.broadcast_to(row, (n, n)), 0.0), axis=1, keepdims=True)


def _ret_sample_kernel(q_ref, k_ref, v_ref, rg_ref, cos_ref, sin_ref, gam_ref, gn_ref, s_ref, a_ref, so_ref):
    cos, sin = cos_ref[...], sin_ref[...]
    outs = []
    for h in range(RET_HEADS):
        cols = slice(h * RET_DK, (h + 1) * RET_DK)
        q = _rope_rows(q_ref[0][:, cols], cos, sin)
        k = _rope_rows(k_ref[0][:, cols], cos, sin) * (RET_DK ** -0.5)
        v = v_ref[0][:, cols]
        state = s_ref[0, h]
        gamma = gam_ref[h]
        qk = jnp.sum(q * k, axis=-1, keepdims=True)
        o = qk * v + jnp.sum(_column_of(q) * state, axis=0, keepdims=True) * gamma
        so_ref[0, h] = state * gamma + _column_of(k) * v
        outs.append(_head_norm_gate(o, gn_ref[:, cols], rg_ref[0][:, cols]))
    a_ref[0] = jnp.concatenate(outs, axis=1).astype(BF16)


def _retention_sample(z3, state, ret_norm, pos):
    nb = z3.shape[0]
    cos, sin = _rope_tables(jnp.full((1,), pos))
    gamma = jnp.exp(jnp.log1p(-jnp.exp2(-5.0 - jnp.arange(RET_HEADS, dtype=F32))))[:, None, None]
    hb = RET_DK
    assert RET_HEADS * RET_DK == RET_W

    def zspec(col0):
        return pl.BlockSpec((1, 1, RET_W), lambda b, col0=col0: (b, 0, col0 // RET_W))

    st_spec = pl.BlockSpec((1, RET_HEADS, RET_DK, RET_DV), lambda b: (b, 0, 0, 0))
    return pl.pallas_call(
        _ret_sample_kernel,
        grid=(nb,),
        in_specs=[zspec(COL_RQ), zspec(COL_RK), zspec(COL_RV), zspec(COL_RG),
                  pl.BlockSpec((1, hb // 2), lambda b: (0, 0)),
                  pl.BlockSpec((1, hb // 2), lambda b: (0, 0)),
                  pl.BlockSpec((RET_HEADS, 1, 1), lambda b: (0, 0, 0)),
                  pl.BlockSpec((1, RET_W), lambda b: (0, 0)),
                  st_spec],
        out_specs=[pl.BlockSpec((1, 1, RET_W), lambda b: (b, 0, 0)), st_spec],
        out_shape=[jax.ShapeDtypeStruct((nb, 1, RET_W), BF16),
                   jax.ShapeDtypeStruct(state.shape, F32)],
        compiler_params=_cparams(1),
        name="retention_sample",
    )(z3, z3, z3, z3, cos, sin, gamma, ret_norm.reshape(1, RET_W), state)


def _half_rows(ref_slice_fn, n_half):
    return jnp.concatenate([ref_slice_fn(p) for p in range(CMP_STRIDE)], axis=1)


def _cmp_stage1_dense_kernel(x_ref, w_ref, o_ref):
    nh = o_ref.shape[3]
    x = _half_rows(lambda p: x_ref[pl.ds(p, nh, stride=CMP_STRIDE), :], nh).astype(BF16)
    o_ref[0, 0, 0] = _dot(x, w_ref[0])


def _cmp_stage1_dense(z, w1ab, batch, seq):
    nh = seq // CMP_STRIDE
    return pl.pallas_call(
        _cmp_stage1_dense_kernel,
        grid=(batch, 2, NSA_KV_HEADS),
        in_specs=[pl.BlockSpec((seq, NSA_DK), lambda b, kv, h: (b, kv * NSA_KV_HEADS + h)),
                  pl.BlockSpec((1, CMP_STRIDE * NSA_DK, 2 * NSA_DK), lambda b, kv, h: (kv, 0, 0))],
        out_specs=pl.BlockSpec((1, 1, 1, nh, 2 * NSA_DK), lambda b, kv, h: (b, kv, h, 0, 0)),
        out_shape=jax.ShapeDtypeStruct((batch, 2, NSA_KV_HEADS, nh, 2 * NSA_DK), F32),
        compiler_params=_cparams(3),
        name="cmp_stage1_dense",
    )(z, w1ab)


PAGES_PER_STEP = 16


def _cmp_stage1_paged_kernel(pt_ref, *refs):
    pages, (w_ref, o_ref) = refs[:PAGES_PER_STEP], refs[PAGES_PER_STEP:]
    hp = PAGE_SIZE // CMP_STRIDE
    top = _iota((2 * NSA_KV_HEADS, NSA_DK), 0) < NSA_KV_HEADS
    cols = [[], []]
    for p in range(CMP_STRIDE):
        tiles = [[], []]
        for pg in pages:
            xp = pg[0, :, p]
            for n in range(0, hp, 2):
                a, b = xp[n], xp[n + 1]
                tiles[0].append(jnp.where(top, a, pltpu.roll(b, NSA_KV_HEADS, 0)))
                tiles[1].append(jnp.where(top, pltpu.roll(a, NSA_KV_HEADS, 0), b))
        for kv in range(2):
            cols[kv].append(jnp.concatenate(tiles[kv], axis=0))
    for kv in range(2):
        x = jnp.concatenate(cols[kv], axis=1).astype(BF16)
        o_ref[0, kv, 0] = _dot(x, w_ref[kv])


def _cmp_stage1_paged(cache, page_table, w1ab):
    nb, n_pages = page_table.shape
    hp = PAGE_SIZE // CMP_STRIDE
    steps = n_pages // PAGES_PER_STEP
    rows = PAGES_PER_STEP * hp * NSA_KV_HEADS

    def page_spec(j):
        return pl.BlockSpec((1, hp, CMP_STRIDE, 2 * NSA_KV_HEADS, NSA_DK),
                            lambda b, s, pt, j=j: (pt[b, s * PAGES_PER_STEP + j], 0, 0, 0, 0))

    wspec = pl.BlockSpec((2, CMP_STRIDE * NSA_DK, 2 * NSA_DK), lambda b, s, pt: (0, 0, 0))
    return pl.pallas_call(
        _cmp_stage1_paged_kernel,
        grid_spec=pltpu.PrefetchScalarGridSpec(
            num_scalar_prefetch=1,
            grid=(nb, steps),
            in_specs=[page_spec(j) for j in range(PAGES_PER_STEP)] + [wspec],
            out_specs=pl.BlockSpec((1, 2, 1, rows, 2 * NSA_DK), lambda b, s, pt: (b, 0, 0, s, 0))),
        out_shape=jax.ShapeDtypeStruct((nb, 2, 1, steps * rows, 2 * NSA_DK), F32),
        compiler_params=_cparams(2),
        name="cmp_stage1_paged",
    )(page_table, *([cache] * PAGES_PER_STEP), w1ab)


def _cmp_stage2_kernel(ac_ref, pos_ref, w1a_ref, w1b_ref, w2_ref, o_ref, *, shift):
    ac = ac_ref[0, 0, 0]
    nh = ac.shape[0]
    pos = pos_ref[0].astype(BF16)
    kw = CMP_STRIDE * NSA_DK
    pe = _dot(pos[:, :kw], w1a_ref[0]) + _dot(pos[:, kw:], w1b_ref[0])
    pre = ac[:, :NSA_DK] + pltpu.roll(ac[:, NSA_DK:], nh - shift, 0) + pe[0:1]
    gelu = 0.5 * pre * (1.0 + jnp.tanh(math.sqrt(2.0 / math.pi) * (pre + 0.044715 * (pre * pre * pre))))
    o_ref[0, 0, 0] = _dot(gelu.astype(BF16), w2_ref[0]).astype(BF16)


def _cmp_stage2(ac, pos8, w1a, w1b, w2, shift):
    nb, _, groups, nh, _ = ac.shape
    kw = CMP_STRIDE * NSA_DK
    return pl.pallas_call(
        functools.partial(_cmp_stage2_kernel, shift=shift),
        grid=(nb, 2, groups),
        in_specs=[pl.BlockSpec((1, 1, 1, nh, 2 * NSA_DK), lambda b, kv, h: (b, kv, h, 0, 0)),
                  pl.BlockSpec((1, 8, 2 * kw), lambda b, kv, h: (kv, 0, 0)),
                  pl.BlockSpec((1, kw, NSA_DK), lambda b, kv, h: (kv, 0, 0)),
                  pl.BlockSpec((1, kw, NSA_DK), lambda b, kv, h: (kv, 0, 0)),
                  pl.BlockSpec((1, NSA_DK, NSA_DK), lambda b, kv, h: (kv, 0, 0))],
        out_specs=pl.BlockSpec((1, 1, 1, nh, NSA_DK), lambda b, kv, h: (b, kv, h, 0, 0)),
        out_shape=jax.ShapeDtypeStruct((nb, 2, groups, nh, NSA_DK), BF16),
        compiler_params=_cparams(3),
        name="cmp_stage2",
    )(ac, pos8, w1a, w1b, w2)


def _rel_bucket(dist):
    n = jnp.maximum(dist, 0)
    nf = jnp.maximum(n, 1).astype(F32)
    scale = (REL_BUCKETS - REL_MAX_EXACT) / math.log(REL_MAX_DIST / REL_MAX_EXACT)
    large = REL_MAX_EXACT + (jnp.log(nf / REL_MAX_EXACT) * scale).astype(jnp.int32)
    large = jnp.minimum(large, REL_BUCKETS - 1)
    return jnp.where(n < REL_MAX_EXACT, n, large)


def _bias_by_dist(rel_table, n):
    tab = rel_table[_rel_bucket(jnp.arange(n))]
    return tab.T.reshape(NSA_KV_HEADS, NSA_GROUP, n)


def _pad_group_rows(t, axis):
    first = lax.slice_in_dim(t, 0, 1, axis=axis)
    return jnp.concatenate([t] + [first] * (SROWS - NSA_GROUP), axis=axis)


def _flash_init(m_ref, l_ref, acc_ref):
    m_ref[...] = jnp.full(m_ref.shape, NEG_INF, F32)
    l_ref[...] = jnp.zeros(l_ref.shape, F32)
    acc_ref[...] = jnp.zeros(acc_ref.shape, F32)


def _flash_step(s, v, m_ref, l_ref, acc_ref):
    m_old = m_ref[...]
    m_new = jnp.maximum(m_old, jnp.max(s, axis=1, keepdims=True))
    alpha = jnp.exp(m_old - m_new)
    p = jnp.exp(s - m_new)
    l_ref[...] = alpha * l_ref[...] + jnp.sum(p, axis=1, keepdims=True)
    acc_ref[...] = alpha * acc_ref[...] + _dot(p.astype(BF16), v)
    m_ref[...] = m_new


def _flash_result(l_ref, acc_ref):
    return acc_ref[...] / jnp.maximum(l_ref[...], 1e-30)


def _select_blocks(imp_t, q0, ns):
    shape = imp_t.shape
    blk = _iota(shape, 0)
    qpos = q0 + _iota(shape, 1)
    cur = qpos >> 6
    valid = blk * SEL_BLOCK <= qpos
    forced = (blk == 0) | (blk == cur) | (blk == cur - 1)
    imp_t = jnp.where(valid, jnp.where(forced, FORCE_SCORE, imp_t), NEG_INF)
    rank = jnp.zeros(shape, F32)
    for other in range(ns):
        row = imp_t[other:other + 1, :]
        ahead = (row > imp_t) | ((row == imp_t) & (blk > other))
        rank = rank + jnp.where(ahead, 1.0, 0.0)
    return jnp.where((rank < SEL_TOPK) & valid, 1.0, 0.0)


def _nsa_prompt_kernel(q_ref, ks_ref, vs_ref, kw_ref, vw_ref, kc_ref, vct_ref, wd_ref, cfar_ref, basec_ref,
                       ovt_ref, ng_ref, nsl_ref, o_ref, m_ref, l_ref, acc_ref, vst_ref, vwt_ref, tz_ref, ow_ref, *, ns):
    i = pl.program_id(2)
    q0 = i * TQ
    qall = q_ref[...] * ((NSA_DK ** -0.5) * LOG2E)
    qt = jnp.concatenate([qall[:, g * NSA_DK:(g + 1) * NSA_DK].T for g in range(NSA_GROUP)], axis=1).astype(BF16)
    c_loc = _iota((TK, NSA_ROWS), 0)
    r_loc = _iota((TK, NSA_ROWS), 1) & (TQ - 1)

    @pl.when(i == 0)
    def _():
        for kt in range(vst_ref.shape[0]):
            vst_ref[kt] = vs_ref[kt * TK:(kt + 1) * TK, :].T.astype(BF16)
            vwt_ref[kt] = vw_ref[kt * TK:(kt + 1) * TK, :].T.astype(BF16)
        below = _iota((TK, TQ), 1) >= _iota((TK, TQ), 0)
        for g in range(NSA_GROUP):
            lo = pltpu.roll(jnp.broadcast_to(wd_ref[0, 2 * g:2 * g + 1, :], (TK, TQ)), 0, 1, stride=1, stride_axis=0)
            hi = pltpu.roll(jnp.broadcast_to(wd_ref[0, 2 * g + 1:2 * g + 2, :], (TK, TQ)), 0, 1, stride=1, stride_axis=0)
            tz_ref[0, :, g * TQ:(g + 1) * TQ] = lo
            tz_ref[1, :, g * TQ:(g + 1) * TQ] = jnp.where(below, hi, lo)

    ncp = kc_ref.shape[3]
    shift = (TQ // CMP_STRIDE) * i
    bias_c = basec_ref[0, pl.ds(pl.multiple_of(ncp - shift, TQ // CMP_STRIDE), ncp), :]
    s = _dot(kc_ref[0, 0, 0], qt) + bias_c
    m = jnp.max(s, axis=0, keepdims=True)
    e = jnp.exp2(s - m)
    inv = jnp.where(m > 0.5 * NEG_INF, 1.0 / jnp.maximum(jnp.sum(e, axis=0, keepdims=True), 1e-30), 0.0)
    p = e * inv
    o_cmp = _dot(vct_ref[0, 0], p.astype(BF16))

    psum = p[:, 0:TQ] + p[:, TQ:2 * TQ] + p[:, 2 * TQ:3 * TQ] + p[:, 3 * TQ:4 * TQ]
    hi = psum.astype(BF16)
    lo = (psum - hi.astype(F32)).astype(BF16)
    ovt = ovt_ref[...]
    imp_t = _dot(ovt, hi) + _dot(ovt, lo)
    ns8 = -(-ns // 8) * 8
    sel_t = _select_blocks(imp_t[:ns8], q0, ns)
    assert ns8 + 2 <= LANE
    sel_neg = jnp.concatenate([jnp.where(sel_t > 0.5, 0.0, NEG_INF)] * NSA_GROUP, axis=1)
    cfar = cfar_ref[0]
    cfar_hi = cfar.astype(BF16).astype(F32)
    sub8 = _iota((8, NSA_ROWS), 0)
    cfar8 = jnp.where(sub8 == 0, cfar_hi, jnp.where(sub8 == 1, cfar - cfar_hi, 0.0))
    pad_rows = jnp.zeros((LANE - ns8 - 8, NSA_ROWS), F32)
    qx_far = jnp.concatenate([qt, jnp.concatenate([sel_neg, cfar8, pad_rows], axis=0).astype(BF16)], axis=0)
    qx_near = jnp.concatenate([qt, jnp.concatenate([sel_neg, jnp.zeros_like(cfar8), pad_rows], axis=0).astype(BF16)],
                              axis=0)
    lane_k = _iota((TK, LANE), 1)
    ones_k = (lane_k == ns8) | (lane_k == ns8 + 1)

    def flash_step(scs, vts):
        m_old = m_ref[...]
        m_new = m_old
        for sc in scs:
            m_new = jnp.maximum(m_new, jnp.max(sc, axis=0, keepdims=True))
        alpha = jnp.exp2(m_old - m_new)
        l_new = alpha * l_ref[...]
        acc = alpha * acc_ref[...]
        for sc, vt in zip(scs, vts):
            pt = jnp.exp2(sc - m_new)
            l_new = l_new + jnp.sum(pt, axis=0, keepdims=True)
            acc = acc + _dot(vt, pt.astype(BF16))
        l_ref[...] = l_new
        acc_ref[...] = acc
        m_ref[...] = m_new

    def sel_scores(kt, bias, causal):
        k = ks_ref[pl.ds(pl.multiple_of(kt * TK, TK), TK), :].astype(BF16)
        blk_of_key = (TK // SEL_BLOCK) * kt + (_iota((TK, LANE), 0) >> 6)
        extra = jnp.where((lane_k == blk_of_key) | ones_k, 1.0, 0.0).astype(BF16)
        kx = jnp.concatenate([k, extra], axis=1)
        sc = _dot(kx, qx_far) if bias is None else _dot(kx, qx_near) + bias
        if causal:
            sc = jnp.where(c_loc <= r_loc, sc, NEG_INF)
        return sc

    def softmax_tiles(scs, vts):
        m_new = jnp.max(scs[0], axis=0, keepdims=True)
        for sc in scs[1:]:
            m_new = jnp.maximum(m_new, jnp.max(sc, axis=0, keepdims=True))
        l_new, acc = None, None
        for sc, vt in zip(scs, vts):
            pt = jnp.exp2(sc - m_new)
            l_t, acc_t = jnp.sum(pt, axis=0, keepdims=True), _dot(vt, pt.astype(BF16))
            l_new, acc = (l_t, acc_t) if l_new is None else (l_new + l_t, acc + acc_t)
        return m_new, l_new, acc

    kt_prev = jnp.maximum(i - 1, 0)
    prev_scores = jnp.where(i >= 1, sel_scores(kt_prev, tz_ref[1], False), NEG_INF)
    m_ref[...], l_ref[...], acc_ref[...] = softmax_tiles([sel_scores(i, tz_ref[0], True), prev_scores],
                                                         [vst_ref[i], vst_ref[kt_prev]])

    def win_scores(off):
        kt = jnp.maximum(i - off, 0)
        k = kw_ref[pl.ds(pl.multiple_of(kt * TK, TK), TK), :].astype(BF16)
        if off < 2:
            sc = _dot(k, qt) + tz_ref[off]
        else:
            sc = _dot(jnp.concatenate([k, jnp.where(ones_k, 1.0, 0.0).astype(BF16)], axis=1), qx_far)
        if off == 0:
            return jnp.where(c_loc <= r_loc, sc, NEG_INF)
        keep = (c_loc > r_loc) & (i >= off) if off * TK == WINDOW else (i >= off)
        return jnp.where(keep, sc, NEG_INF)

    n_win = WINDOW // TK + 1
    _, l_win, acc_win = softmax_tiles([win_scores(off) for off in range(n_win)],
                                      [vwt_ref[jnp.maximum(i - off, 0)] for off in range(n_win)])
    ow_ref[...] = acc_win / jnp.maximum(l_win, 1e-30)

    def far_tiles(kt_first, count):
        kts = [kt_first - t for t in range(count)]
        flash_step([sel_scores(kt, None, False) for kt in kts], [vst_ref[kt] for kt in kts])

    def far_group(j, carry):
        far_tiles(i - 2 - FAR_GROUP * j, FAR_GROUP)
        return carry

    n_far = jnp.maximum(i - 1, 0)
    lax.fori_loop(0, n_far // FAR_GROUP, far_group, 0)
    for rest in range(1, FAR_GROUP):
        pl.when(n_far % FAR_GROUP == rest)(functools.partial(far_tiles, rest - 1, rest))

    o_sel = _flash_result(l_ref, acc_ref)
    o_win = ow_ref[...]

    gates_t = _sigmoid(ng_ref[...]).T
    nsl = nsl_ref[...]
    outs = []
    for g in range(NSA_GROUP):
        cols = slice(g * TQ, (g + 1) * TQ)
        o_t = (gates_t[3 * g:3 * g + 1] * o_cmp[:, cols] + gates_t[3 * g + 1:3 * g + 2] * o_sel[:, cols]
               + gates_t[3 * g + 2:3 * g + 3] * o_win[:, cols])
        x = nsl[:, g * NSA_DK:(g + 1) * NSA_DK]
        outs.append(o_t.T * (x * _sigmoid(x)))
    o_ref[...] = jnp.concatenate(outs, axis=1).astype(BF16)


def _lanes_by_head(t):
    hk, g, rows, tq = t.shape
    return t.transpose(0, 2, 1, 3).reshape(hk, rows, g * tq)


def _nsa_prompt(z, zkv, zb, kcvc, bias_d, batch, seq):
    nq = seq // TQ
    ns = seq // SEL_BLOCK
    ncp = kcvc.shape[3]
    nsp = LANE
    wn = TQ // CMP_STRIDE
    assert TQ == TK and ns <= nsp and ncp >= seq // CMP_STRIDE and ncp % LANE == 0 and ncp > wn
    gw = NSA_GROUP * NSA_DK
    n_dist = bias_d.shape[-1]
    assert n_dist >= 2 * TK + TQ
    bias_d = bias_d * LOG2E

    tz = bias_d[..., :2 * TQ].reshape(NSA_KV_HEADS, 2 * NSA_GROUP, TQ)
    far = jnp.broadcast_to(bias_d[..., REL_MAX_DIST][:, :, None, None], (NSA_KV_HEADS, NSA_GROUP, 1, TQ))
    cfar = _lanes_by_head(far)
    half = n_dist // 2
    start = CMP_STRIDE * wn - (CMP_BLOCK - 1)
    assert start + TQ <= half and 2 * wn * CMP_STRIDE - start <= half
    w_ext = jnp.concatenate([bias_d[..., :half], jnp.full(bias_d.shape[:-1] + (n_dist - half,), NEG_INF, F32)], -1)
    near = jnp.tile(w_ext, (1, 1, 2 * wn + 1))[..., :2 * wn * (n_dist - CMP_STRIDE)]
    near = near.reshape(NSA_KV_HEADS, NSA_GROUP, 2 * wn, n_dist - CMP_STRIDE)[..., start:start + TQ]
    basec = jnp.concatenate([jnp.broadcast_to(far, (NSA_KV_HEADS, NSA_GROUP, ncp - wn, TQ)), near,
                             jnp.full((NSA_KV_HEADS, NSA_GROUP, ncp - wn, TQ), NEG_INF, F32)], axis=2)
    basec = _lanes_by_head(basec)
    sblk = jnp.arange(nsp)[:, None]
    nblk = jnp.arange(ncp)[None, :]
    ovt = ((nblk >= 4 * sblk - 1) & (nblk <= 4 * sblk + 3)).astype(BF16)

    vct = kcvc[:, 1].transpose(0, 1, 3, 2)

    def kvspec(col0, which):
        return pl.BlockSpec((seq, NSA_DK),
                            lambda b, h, i: (b, (col0 - COL_KVC) // NSA_DK + which * NSA_KV_HEADS + h))

    vt_scratch = pltpu.VMEM((seq // TK, NSA_DK, TK), BF16)
    return pl.pallas_call(
        functools.partial(_nsa_prompt_kernel, ns=ns),
        grid=(batch, NSA_KV_HEADS, nq),
        in_specs=[pl.BlockSpec((TQ, gw), lambda b, h, i: (b * nq + i, COL_NQ // gw + h)),
                  kvspec(COL_KVS, 0), kvspec(COL_KVS, 1), kvspec(COL_KVW, 0), kvspec(COL_KVW, 1),
                  pl.BlockSpec((1, 1, 1, ncp, NSA_DK), lambda b, h, i: (b, 0, h, 0, 0)),
                  pl.BlockSpec((1, 1, NSA_DK, ncp), lambda b, h, i: (b, h, 0, 0)),
                  pl.BlockSpec((1, 2 * NSA_GROUP, TQ), lambda b, h, i: (h, 0, 0)),
                  pl.BlockSpec((1, 1, NSA_ROWS), lambda b, h, i: (h, 0, 0)),
                  pl.BlockSpec((1, 2 * ncp, NSA_ROWS), lambda b, h, i: (h, 0, 0)),
                  pl.BlockSpec((nsp, ncp), lambda b, h, i: (0, 0)),
                  pl.BlockSpec((TQ, LANE), lambda b, h, i: (b * nq + i, COL_NG // LANE + h)),
                  pl.BlockSpec((TQ, gw), lambda b, h, i: (b * nq + i, COL_NSL // gw + h))],
        out_specs=pl.BlockSpec((TQ, gw), lambda b, h, i: (b * nq + i, h)),
        out_shape=jax.ShapeDtypeStruct((batch * seq, NSA_W), BF16),
        scratch_shapes=[pltpu.VMEM((1, NSA_ROWS), F32), pltpu.VMEM((1, NSA_ROWS), F32),
                        pltpu.VMEM((NSA_DK, NSA_ROWS), F32), vt_scratch, vt_scratch,
                        pltpu.VMEM((2, TK, NSA_ROWS), F32), pltpu.VMEM((NSA_DK, NSA_ROWS), F32)],
        compiler_params=_cparams(3),
        name="nsa_prompt",
    )(z, zkv, zkv, zkv, zkv, kcvc, vct, tz, cfar, basec, ovt, zb, zb)


SROWS = 8
SEL_PER_STEP = 4


def _stack_group_q(q_row):
    heads = [q_row[:, g * NSA_DK:(g + 1) * NSA_DK] for g in range(NSA_GROUP)]
    return jnp.concatenate(heads + [heads[0]] * (SROWS - NSA_GROUP), axis=0)


def _nsa_sample_cmp_kernel(q_ref, kc_ref, vc_ref, bias_ref, ov_ref, o_ref, idx_ref, *, ns):
    scale = NSA_DK ** -0.5
    qs = _stack_group_q(q_ref[0]).astype(BF16)
    s = _nt(qs, kc_ref[0, 0]) * scale + bias_ref[0]
    m = jnp.max(s, axis=1, keepdims=True)
    e = jnp.exp(s - m)
    inv = jnp.where(m > 0.5 * NEG_INF, 1.0 / jnp.maximum(jnp.sum(e, axis=1, keepdims=True), 1e-30), 0.0)
    p = e * inv
    o_ref[0, 0] = _dot(p.astype(BF16), vc_ref[0, 0])
    psum = jnp.broadcast_to(jnp.sum(p[0:NSA_GROUP], axis=0, keepdims=True), p.shape)
    hi = psum.astype(BF16)
    lo = (psum - hi.astype(F32)).astype(BF16)
    imp = (_dot(hi, ov_ref[...]) + _dot(lo, ov_ref[...]))[0:1]
    nsp = imp.shape[1]
    blk_r = _iota((1, nsp), 1)
    cur = ns - 1
    forced = (blk_r == 0) | (blk_r == cur) | (blk_r == cur - 1)
    imp = jnp.where(blk_r < ns, jnp.where(forced, FORCE_SCORE, imp), 2.0 * NEG_INF)
    imp_c = _column_of(imp)
    i_r = _iota((nsp, nsp), 1)
    j_c = _iota((nsp, nsp), 0)
    ahead = (imp > imp_c) | ((imp == imp_c) & (i_r < j_c))
    rank_c = jnp.sum(jnp.where(ahead, 1.0, 0.0), axis=1, keepdims=True)
    slot = _iota((nsp, LANE), 1).astype(F32)
    picks = jnp.where(rank_c == slot, _iota((nsp, LANE), 0).astype(F32), 0.0)
    idx_ref[0, 0] = jnp.broadcast_to(jnp.sum(picks, axis=0, keepdims=True), (SROWS, LANE)).astype(jnp.int32)


def _nsa_sample_cmp(z3, kcvc, bias_c, ov, ns):
    nb = z3.shape[0]
    ncp = kcvc.shape[2]
    nsp = ov.shape[1]
    gw = NSA_GROUP * NSA_DK

    def cspec(which):
        return pl.BlockSpec((1, 1, ncp, NSA_DK), lambda b, h: (b, which, 0, h))

    return pl.pallas_call(
        functools.partial(_nsa_sample_cmp_kernel, ns=ns),
        grid=(nb, NSA_KV_HEADS),
        in_specs=[pl.BlockSpec((1, 1, gw), lambda b, h: (b, 0, COL_NQ // gw + h)),
                  cspec(0), cspec(1),
                  pl.BlockSpec((1, SROWS, ncp), lambda b, h: (h, 0, 0)),
                  pl.BlockSpec((ncp, nsp), lambda b, h: (0, 0))],
        out_specs=[pl.BlockSpec((1, 1, SROWS, NSA_DK), lambda b, h: (b, h, 0, 0)),
                   pl.BlockSpec((1, 1, SROWS, LANE), lambda b, h: (b, h, 0, 0))],
        out_shape=[jax.ShapeDtypeStruct((nb, NSA_KV_HEADS, SROWS, NSA_DK), F32),
                   jax.ShapeDtypeStruct((nb, NSA_KV_HEADS, SROWS, LANE), jnp.int32)],
        compiler_params=_cparams(2),
        name="nsa_sample_cmp",
    )(z3, kcvc, kcvc, bias_c, ov)


def _nsa_sample_sel_kernel(pt_ref, idx_ref, q_ref, *refs, ns):
    n_blk = NSA_KV_HEADS * SEL_PER_STEP
    blocks, (new_ref,), biases = refs[:n_blk], refs[n_blk:n_blk + 1], refs[n_blk + 1:2 * n_blk + 1]
    o_ref, m_ref, l_ref, acc_ref = refs[2 * n_blk + 1:]
    b, t = pl.program_id(0), pl.program_id(1)
    gw = NSA_GROUP * NSA_DK
    rows_kv = 2 * NSA_KV_HEADS
    width = SEL_BLOCK * rows_kv

    @pl.when(t == 0)
    def _():
        _flash_init(m_ref, l_ref, acc_ref)

    new_rows = jnp.concatenate([new_ref[0]] * SEL_BLOCK, axis=0)
    xs, scs = [], []
    for h in range(NSA_KV_HEADS):
        qs = _stack_group_q(q_ref[0][:, h * gw:(h + 1) * gw]).astype(BF16)
        row = []
        for u in range(SEL_PER_STEP):
            r = h * SEL_PER_STEP + u
            is_new = idx_ref[b, h, t * SEL_PER_STEP + u] == ns - 1
            x = jnp.where(is_new, new_rows, blocks[r][0].reshape(width, NSA_DK)).astype(BF16)
            xs.append(x)
            row.append(_nt(qs, x) * (NSA_DK ** -0.5) + biases[r][0, 0])
        scs.append(jnp.concatenate(row, axis=1))
    sc = jnp.concatenate(scs, axis=0)
    m_old = m_ref[...]
    m_new = jnp.maximum(m_old, jnp.max(sc, axis=1, keepdims=True))
    alpha = jnp.exp(m_old - m_new)
    p = jnp.exp(sc - m_new)
    l_ref[...] = alpha * l_ref[...] + jnp.sum(p, axis=1, keepdims=True)
    pv = pltpu.roll(p, NSA_KV_HEADS, 1).astype(BF16)
    acc = alpha * acc_ref[...]
    upd = []
    for h in range(NSA_KV_HEADS):
        ph = pv[h * SROWS:(h + 1) * SROWS]
        upd.append(sum(_dot(ph[:, u * width:(u + 1) * width], xs[h * SEL_PER_STEP + u]) for u in range(SEL_PER_STEP)))
    acc_ref[...] = acc + jnp.concatenate(upd, axis=0)
    m_ref[...] = m_new

    @pl.when(t == pl.num_programs(1) - 1)
    def _():
        o_ref[0] = (acc_ref[...] / jnp.maximum(l_ref[...], 1e-30)).reshape(NSA_KV_HEADS, SROWS, NSA_DK)


def _nsa_sample_sel(z3, cache, kv_new, page_table, idx, bias_sel, ns):
    nb, n_pages = page_table.shape
    n_sel = idx.shape[2]
    halves = PAGE_SIZE // SEL_BLOCK
    rows_kv = 2 * NSA_KV_HEADS

    per_step = SEL_PER_STEP
    assert n_sel % per_step == 0

    def blockspec(h, u):
        def index(b, t, pt, ix):
            blk = ix[b, h, t * per_step + u]
            return (pt[b, jnp.minimum(blk // halves, n_pages - 1)], blk % halves, 0, 0)
        return pl.BlockSpec((1, SEL_BLOCK, rows_kv, NSA_DK), index)

    def biasspec(h, u):
        return pl.BlockSpec((1, 1, SROWS, SEL_BLOCK * rows_kv),
                            lambda b, t, pt, ix: (h, ix[b, h, t * per_step + u], 0, 0))

    slots = [(h, u) for h in range(NSA_KV_HEADS) for u in range(per_step)]
    return pl.pallas_call(
        functools.partial(_nsa_sample_sel_kernel, ns=ns),
        grid_spec=pltpu.PrefetchScalarGridSpec(
            num_scalar_prefetch=2,
            grid=(nb, n_sel // per_step),
            in_specs=[pl.BlockSpec((1, 1, NSA_W), lambda b, t, pt, ix: (b, 0, COL_NQ // NSA_W))]
                     + [blockspec(h, u) for h, u in slots]
                     + [pl.BlockSpec((1, rows_kv, NSA_DK), lambda b, t, pt, ix: (b, 0, 0))]
                     + [biasspec(h, u) for h, u in slots],
            out_specs=pl.BlockSpec((1, NSA_KV_HEADS, SROWS, NSA_DK), lambda b, t, pt, ix: (b, 0, 0, 0)),
            scratch_shapes=[pltpu.VMEM((NSA_KV_HEADS * SROWS, 1), F32), pltpu.VMEM((NSA_KV_HEADS * SROWS, 1), F32),
                            pltpu.VMEM((NSA_KV_HEADS * SROWS, NSA_DK), F32)]),
        out_shape=jax.ShapeDtypeStruct((nb, NSA_KV_HEADS, SROWS, NSA_DK), F32),
        compiler_params=_cparams(2),
        name="nsa_sample_sel",
    )(page_table, idx, z3, *([cache] * len(slots)), kv_new, *([bias_sel] * len(slots)))


def _nsa_sample_win_kernel(q_ref, k_ref, v_ref, kn_ref, vn_ref, bias_ref, bnew_ref, o_ref):
    scale = NSA_DK ** -0.5
    q = _stack_group_q(q_ref[0])
    s_buf = _nt(q.astype(BF16), k_ref[0].astype(BF16)) * scale + bias_ref[0]
    s_new = jnp.sum(q * kn_ref[0], axis=1, keepdims=True) * scale + bnew_ref[0]
    m = jnp.maximum(jnp.max(s_buf, axis=1, keepdims=True), s_new)
    p_buf = jnp.exp(s_buf - m)
    p_new = jnp.exp(s_new - m)
    l = jnp.sum(p_buf, axis=1, keepdims=True) + p_new
    acc = _dot(p_buf.astype(BF16), v_ref[0].astype(BF16)) + p_new * vn_ref[0]
    o_ref[0, 0] = acc / jnp.maximum(l, 1e-30)


def _nsa_sample_win(z3, win_buf, bias_win, bias_new):
    nb, nbuf, _ = win_buf.shape
    gw = NSA_GROUP * NSA_DK

    def bufspec(which):
        return pl.BlockSpec((1, nbuf, NSA_DK), lambda b, h: (b, 0, which * NSA_KV_HEADS + h))

    def newspec(which):
        return pl.BlockSpec((1, 1, NSA_DK), lambda b, h: (b, 0, COL_KVW // NSA_DK + which * NSA_KV_HEADS + h))

    return pl.pallas_call(
        _nsa_sample_win_kernel,
        grid=(nb, NSA_KV_HEADS),
        in_specs=[pl.BlockSpec((1, 1, gw), lambda b, h: (b, 0, COL_NQ // gw + h)),
                  bufspec(0), bufspec(1), newspec(0), newspec(1),
                  pl.BlockSpec((1, SROWS, nbuf), lambda b, h: (h, 0, 0)),
                  pl.BlockSpec((1, SROWS, 1), lambda b, h: (h, 0, 0))],
        out_specs=pl.BlockSpec((1, 1, SROWS, NSA_DK), lambda b, h: (b, h, 0, 0)),
        out_shape=jax.ShapeDtypeStruct((nb, NSA_KV_HEADS, SROWS, NSA_DK), F32),
        compiler_params=_cparams(2),
        name="nsa_sample_win",
    )(z3, win_buf, win_buf, z3, z3, bias_win, bias_new)


def _nsa_sample_gate_kernel(oc_ref, os_ref, ow_ref, ng_ref, nsl_ref, o_ref):
    gates = _sigmoid(ng_ref[0])
    nsl = nsl_ref[0]
    outs = []
    for g in range(NSA_GROUP):
        o = (gates[:, 3 * g:3 * g + 1] * oc_ref[0, 0, g:g + 1] + gates[:, 3 * g + 1:3 * g + 2] * os_ref[0, 0, g:g + 1]
             + gates[:, 3 * g + 2:3 * g + 3] * ow_ref[0, 0, g:g + 1])
        x = nsl[:, g * NSA_DK:(g + 1) * NSA_DK]
        outs.append(o * (x * _sigmoid(x)))
    o_ref[0] = jnp.concatenate(outs, axis=1).astype(BF16)


def _nsa_sample_gate(o_cmp, o_sel, o_win, z3b):
    nb = z3b.shape[0]
    gw = NSA_GROUP * NSA_DK
    ospec = pl.BlockSpec((1, 1, SROWS, NSA_DK), lambda b, h: (b, h, 0, 0))
    return pl.pallas_call(
        _nsa_sample_gate_kernel,
        grid=(nb, NSA_KV_HEADS),
        in_specs=[ospec, ospec, ospec,
                  pl.BlockSpec((1, 1, LANE), lambda b, h: (b, 0, COL_NG // LANE + h)),
                  pl.BlockSpec((1, 1, gw), lambda b, h: (b, 0, COL_NSL // gw + h))],
        out_specs=pl.BlockSpec((1, 1, gw), lambda b, h: (b, 0, h)),
        out_shape=jax.ShapeDtypeStruct((nb, 1, NSA_W), BF16),
        compiler_params=_cparams(2),
        name="nsa_sample_gate",
    )(o_cmp, o_sel, o_win, z3b, z3b)


def _mem_heads(q, kv):
    outs = []
    for h in range(MEM_HEADS):
        k = kv[:, h * MEM_DH:(h + 1) * MEM_DH].astype(BF16)
        v = kv[:, MEM_W + h * MEM_DH:MEM_W + (h + 1) * MEM_DH].astype(BF16)
        s = _nt(q[:, h * MEM_DH:(h + 1) * MEM_DH].astype(BF16), k) * (MEM_DH ** -0.5)
        e = jnp.exp(s - jnp.max(s, axis=1, keepdims=True))
        p = e / jnp.sum(e, axis=1, keepdims=True)
        outs.append(_dot(p.astype(BF16), v))
    return jnp.concatenate(outs, axis=1)


def _mem_prompt_kernel(q_ref, kv_ref, o_ref):
    o_ref[...] = _mem_heads(q_ref[:, :MEM_W], kv_ref[...]).astype(BF16)


def _mem_prompt(z, mem_kv, batch, seq, tq):
    nq = seq // tq
    n_mem = mem_kv.shape[0] // batch
    return pl.pallas_call(
        _mem_prompt_kernel,
        grid=(batch, nq),
        in_specs=[pl.BlockSpec((tq, MQ_BLOCK), lambda b, i: (b * nq + i, COL_MQ // MQ_BLOCK)),
                  pl.BlockSpec((n_mem, 2 * MEM_W), lambda b, i: (b, 0))],
        out_specs=pl.BlockSpec((tq, MEM_W), lambda b, i: (b * nq + i, 0)),
        out_shape=jax.ShapeDtypeStruct((batch * seq, MEM_W), BF16),
        compiler_params=_cparams(2),
        name="mem_prompt",
    )(z, mem_kv)


def _mem_sample_kernel(q_ref, kv_ref, o_ref):
    q = jnp.broadcast_to(q_ref[0][:, :MEM_W], (SROWS, MEM_W))
    o_ref[0] = _mem_heads(q, kv_ref[0])[0:1].astype(BF16)


def _mem_sample(z3, mem_kv):
    nb, n_mem, _ = mem_kv.shape
    return pl.pallas_call(
        _mem_sample_kernel,
        grid=(nb,),
        in_specs=[pl.BlockSpec((1, 1, MQ_BLOCK), lambda b: (b, 0, COL_MQ // MQ_BLOCK)),
                  pl.BlockSpec((1, n_mem, 2 * MEM_W), lambda b: (b, 0, 0))],
        out_specs=pl.BlockSpec((1, 1, MEM_W), lambda b: (b, 0, 0)),
        out_shape=jax.ShapeDtypeStruct((nb, 1, MEM_W), BF16),
        compiler_params=_cparams(1),
        name="mem_sample",
    )(z3, mem_kv)


def _merge_kernel(ar_ref, an_ref, am_ref, wr_ref, wn_ref, wm_ref, g0_ref, g1_ref, g2_ref, o_ref):
    merged = (_sigmoid(g0_ref[...]) * _dot(ar_ref[...], wr_ref[...])
              + _sigmoid(g1_ref[...]) * _dot(an_ref[...], wn_ref[...])
              + _sigmoid(g2_ref[...]) * _dot(am_ref[...], wm_ref[...]))
    o_ref[...] = merged.astype(BF16)


def _merge(a_ret, a_nsa, a_mem, w_ret, w_nsa, w_mem, z, tm, tn):
    m = a_ret.shape[0]
    nt = D_MODEL // tn

    def aspec(width):
        return pl.BlockSpec((tm, width), lambda i, j: (i, 0))

    def wspec(width):
        return pl.BlockSpec((width, tn), lambda i, j: (0, j))

    def gspec(branch):
        return pl.BlockSpec((tm, tn), lambda i, j: (i, COL_MG // tn + branch * nt + j))

    return pl.pallas_call(
        _merge_kernel,
        grid=(m // tm, nt),
        in_specs=[aspec(RET_W), aspec(NSA_W), aspec(MEM_W), wspec(RET_W), wspec(NSA_W), wspec(MEM_W),
                  gspec(0), gspec(1), gspec(2)],
        out_specs=pl.BlockSpec((tm, tn), lambda i, j: (i, j)),
        out_shape=jax.ShapeDtypeStruct((m, D_MODEL), BF16),
        compiler_params=_cparams(2),
        name="merge",
    )(a_ret, a_nsa, a_mem, w_ret, w_nsa, w_mem, z, z, z)


def _out_kernel(a_ref, w_ref, x_ref, g_ref, o_ref):
    out = _dot(a_ref[...], w_ref[...])
    y = out * lax.rsqrt(jnp.mean(out * out, axis=-1, keepdims=True) + EPS)
    o_ref[...] = x_ref[...] + y * g_ref[...]


def _out_proj(merged, w_out, x, norm_post, tm):
    m = merged.shape[0]
    return pl.pallas_call(
        _out_kernel,
        grid=(m // tm,),
        in_specs=[pl.BlockSpec((tm, D_MODEL), lambda i: (i, 0)),
                  pl.BlockSpec((D_MODEL, D_MODEL), lambda i: (0, 0)),
                  pl.BlockSpec((tm, D_MODEL), lambda i: (i, 0)),
                  pl.BlockSpec((1, D_MODEL), lambda i: (0, 0))],
        out_specs=pl.BlockSpec((tm, D_MODEL), lambda i: (i, 0)),
        out_shape=jax.ShapeDtypeStruct((m, D_MODEL), F32),
        compiler_params=_cparams(1),
        name="out_proj",
    )(merged, w_out, x, norm_post.reshape(1, D_MODEL))


def _layout_w_ng(w_t):
    per_group = N_BRANCHES * NSA_GROUP
    ng = w_t[PROJ_A:PROJ_A + N_BRANCHES * NSA_HEADS].reshape(NSA_KV_HEADS, per_group, D_MODEL)
    return jnp.pad(ng, ((0, 0), (0, LANE - per_group), (0, 0))).reshape(NG_SLOT, D_MODEL)


def _pick_tile(m, cap):
    t = min(m, cap)
    while m % t:
        t //= 2
    return t


def kernel(x_prompt, x_sample, cache_cmp_kv, cache_sel_kv, cache_win_kv, state_ret, cache_mem_kv, page_table,
           mem_prompt, rel_table, norm_pre, norm_post, norm_mem, w_in, ret_norm, w_ret_up, cmp_pos, w_cmp1,
           w_cmp2, w_nsa_up, w_mem_kv, w_mem_up, w_out):
    batch, seq, _ = x_prompt.shape
    nb = x_sample.shape[0]
    assert x_sample.shape[1] == 1 and norm_pre.shape[0] == 1
    assert seq % TQ == 0 and seq >= WINDOW
    n_pool = cache_cmp_kv.shape[1]
    n_pages = page_table.shape[1]
    past = n_pages * PAGE_SIZE
    n_mem = mem_prompt.shape[1]
    assert n_pages % PAGES_PER_STEP == 0 and cache_win_kv.shape[2] == WINDOW

    w_a = w_in[0].T
    w_ng = _layout_w_ng(w_a)
    kw = CMP_STRIDE * NSA_DK
    w1 = w_cmp1[0].reshape(2, CMP_BLOCK * NSA_DK, NSA_DK).astype(BF16)
    w1a, w1b = w1[:, :kw], w1[:, kw:]
    w1ab = jnp.concatenate([w1a, w1b], axis=2)
    w2 = w_cmp2[0].astype(BF16)
    pos8 = jnp.pad(cmp_pos[0].reshape(2, 1, CMP_BLOCK * NSA_DK), ((0, 0), (0, 7), (0, 0)))
    w_ret = w_ret_up[0].astype(BF16)
    w_nsa = w_nsa_up[0].astype(BF16)
    w_mem = w_mem_up[0].astype(BF16)
    w_o = w_out[0].astype(BF16)

    m_p = batch * seq
    xp = x_prompt.reshape(m_p, D_MODEL)
    hp = _rmsnorm(xp, norm_pre[0], _pick_tile(m_p, 512))
    xs = x_sample.reshape(nb, D_MODEL)
    hs = _rmsnorm(xs, norm_pre[0], nb)
    z, zs_head = _proj(hp, w_a, _pick_tile(m_p, 1024), PROJ_TN, COL_KVC, transposed=True, rider=hs)
    zkv, kvc_rows, kvs_rows, kvw_rows, zs_kv = _proj_kv(hp, w_a, _pick_tile(seq, WINDOW), seq, hs)
    zb, zsb = _proj_tail(hp, w_a, w_ng, _pick_tile(m_p, 1024), hs)

    a_ret, ret_state_p = _retention_prompt(z, ret_norm[0], batch, seq)

    ncp = max(LANE, -(-(seq // CMP_STRIDE) // LANE) * LANE)
    ac = _cmp_stage1_dense(zkv, w1ab, batch, seq)
    kcvc = _cmp_stage2(ac, pos8, w1a, w1b, w2, 1)
    if ncp > kcvc.shape[3]:
        kcvc = jnp.pad(kcvc, ((0, 0), (0, 0), (0, 0), (0, ncp - kcvc.shape[3]), (0, 0)))
    bias_d = _bias_by_dist(rel_table, BIAS_DISTS)
    a_nsa = _nsa_prompt(z, zkv, zb, kcvc, bias_d, batch, seq)

    hm = _rmsnorm(mem_prompt.reshape(batch * n_mem, D_MODEL), norm_mem[0], _pick_tile(batch * n_mem, 512))
    mem_kv_p = _proj(hm, w_mem_kv[0], _pick_tile(batch * n_mem, 512), PROJ_TN)
    a_mem = _mem_prompt(zb, mem_kv_p, batch, seq, _pick_tile(seq, 512))

    merged = _merge(a_ret, a_nsa, a_mem, w_ret, w_nsa, w_mem, zb, _pick_tile(m_p, 1024), 512)
    y_p = _out_proj(merged, w_o, xp, norm_post[0], _pick_tile(m_p, 512)).reshape(batch, seq, D_MODEL)

    kv_shape = (1, batch, seq, 2, NSA_KV_HEADS, NSA_DK)
    new_cmp_p = kvc_rows.reshape(kv_shape)
    new_sel_p = kvs_rows.reshape(kv_shape)
    new_win_p = kvw_rows.reshape(1, batch, WINDOW, 2, NSA_KV_HEADS, NSA_DK)
    new_ret_p = ret_state_p[None]
    new_mem_p = mem_kv_p.reshape(1, batch, n_mem, 2, MEM_HEADS, MEM_DH)

    zs = jnp.concatenate([zs_head, zs_kv], axis=1)
    z3 = zs.reshape(nb, 1, PROJ_A)
    z3b = zsb.reshape(nb, 1, PROJ_B)

    a_ret_s, ret_state_s = _retention_sample(z3, state_ret[0], ret_norm[0], past)

    cache_c = cache_cmp_kv[0].reshape(n_pool, PAGE_SIZE // CMP_STRIDE, CMP_STRIDE, 2 * NSA_KV_HEADS, NSA_DK)
    cache_s = cache_sel_kv[0].reshape(n_pool, PAGE_SIZE, 2 * NSA_KV_HEADS, NSA_DK)
    ac_s = _cmp_stage1_paged(cache_c, page_table, w1ab)
    kcvc_s = _cmp_stage2(ac_s, pos8, w1a, w1b, w2, NSA_KV_HEADS)
    ncs = past // CMP_STRIDE
    kcvc_s = kcvc_s.reshape(nb, 2, ncs, KV_W)
    ns_s = past // SEL_BLOCK + 1
    nsp_s = -(-ns_s // LANE) * LANE
    assert past >= WINDOW and past >= REL_MAX_DIST and BIAS_DISTS > WINDOW
    far_s = bias_d[..., REL_MAX_DIST:REL_MAX_DIST + 1]
    hg = (NSA_KV_HEADS, NSA_GROUP)
    n_valid = (past - (CMP_BLOCK - 1)) // CMP_STRIDE + 1
    strided = bias_d[..., (past - (CMP_BLOCK - 1)) % CMP_STRIDE::CMP_STRIDE]
    n_tab = strided.shape[-1]
    assert n_valid >= n_tab and n_tab * CMP_STRIDE > REL_MAX_DIST + CMP_STRIDE and ncs >= n_valid
    bias_cs = jnp.concatenate([jnp.broadcast_to(far_s, hg + (n_valid - n_tab,)), strided[..., ::-1],
                               jnp.full(hg + (ncs - n_valid,), NEG_INF, F32)], axis=-1)
    bias_cs = _pad_group_rows(bias_cs, 1)
    nblk = jnp.arange(ncs)[:, None]
    sblk = jnp.arange(nsp_s)[None, :]
    ov_s = ((nblk >= 4 * sblk - 1) & (nblk <= 4 * sblk + 3)).astype(BF16)
    o_cmp_s, idx_s = _nsa_sample_cmp(z3, kcvc_s, bias_cs, ov_s, ns_s)
    n_sel = min(SEL_TOPK, ns_s)
    idx = idx_s[:, :, 0, :n_sel]

    hg = (NSA_KV_HEADS, NSA_GROUP)
    n_key = ns_s * SEL_BLOCK
    bias_sel = jnp.concatenate([jnp.broadcast_to(far_s, hg + (past + 1 - REL_MAX_DIST,)),
                                bias_d[..., :REL_MAX_DIST][..., ::-1],
                                jnp.full(hg + (n_key - past - 1,), NEG_INF, F32)], axis=-1)
    bias_sel = _pad_group_rows(bias_sel.reshape(hg + (ns_s, SEL_BLOCK)).transpose(0, 2, 1, 3), 2)
    own_k = jnp.arange(2 * NSA_KV_HEADS)[None, :] == jnp.arange(NSA_KV_HEADS)[:, None]
    bias_sel = jnp.where(own_k[:, None, None, None, :], bias_sel[..., None], NEG_INF)
    bias_sel = bias_sel.reshape(NSA_KV_HEADS, ns_s, SROWS, SEL_BLOCK * 2 * NSA_KV_HEADS)
    kvs_new = zs[:, COL_KVS:COL_KVS + 2 * KV_W].reshape(nb, 2 * NSA_KV_HEADS, NSA_DK)
    o_sel_s = _nsa_sample_sel(z3, cache_s, kvs_new, page_table, idx, bias_sel, ns_s)

    win_buf = cache_win_kv[0].reshape(nb, WINDOW, 2 * KV_W)
    bias_w = jnp.concatenate([jnp.full(hg + (1,), NEG_INF, F32), bias_d[..., 1:WINDOW][..., ::-1]], axis=-1)
    o_win_s = _nsa_sample_win(z3, win_buf, _pad_group_rows(bias_w, 1), _pad_group_rows(bias_d[..., 0:1], 1))
    a_nsa_s = _nsa_sample_gate(o_cmp_s, o_sel_s, o_win_s, z3b)

    mem_kv_s = cache_mem_kv[0].reshape(nb, n_mem, 2 * MEM_W)
    a_mem_s = _mem_sample(z3b, mem_kv_s)

    merged_s = _merge(a_ret_s.reshape(nb, RET_W), a_nsa_s.reshape(nb, NSA_W), a_mem_s.reshape(nb, MEM_W),
                      w_ret, w_nsa, w_mem, zsb, nb, 512)
    y_s = _out_proj(merged_s, w_o, xs, norm_post[0], nb).reshape(nb, 1, D_MODEL)

    kvs_shape = (1, nb, 1, 2, NSA_KV_HEADS, NSA_DK)
    new_cmp_s = zs[:, COL_KVC:COL_KVC + 2 * KV_W].reshape(kvs_shape)
    new_sel_s = zs[:, COL_KVS:COL_KVS + 2 * KV_W].reshape(kvs_shape)
    kvw_s = zs[:, COL_KVW:COL_KVW + 2 * KV_W].reshape(nb, 1, 2, NSA_KV_HEADS, NSA_DK)
    new_win_s = jnp.concatenate([cache_win_kv[0][:, 1:], kvw_s], axis=1)[None]
    new_ret_s = ret_state_s[None]

    return (y_p, y_s, new_cmp_p, new_sel_p, new_win_p, new_ret_p, new_mem_p,
            new_cmp_s, new_sel_s, new_win_s, new_ret_s)
```

```python
import functools
import math

import jax
import jax.numpy as jnp
from jax import lax
from jax.experimental import pallas as pl
from jax.experimental.pallas import tpu as pltpu

F32 = jnp.float32
BF16 = jnp.bfloat16

D_MODEL = 2048
PAGE_SIZE = 128
RET_HEADS = 8
RET_DK = 256
RET_DV = 256
RET_CHUNK = 128
ROPE_BASE = 10000.0
NSA_HEADS = 16
NSA_KV_HEADS = 4
NSA_GROUP = NSA_HEADS // NSA_KV_HEADS
NSA_DK = 128
CMP_BLOCK = 32
CMP_STRIDE = 16
SEL_BLOCK = 64
SEL_TOPK = 16
WINDOW = 512
MEM_HEADS = 4
MEM_DH = 384
REL_BUCKETS = 32
REL_MAX_EXACT = 16
REL_MAX_DIST = 128
N_BRANCHES = 3
EPS = 1e-6
NEG_INF = -1e30
FORCE_SCORE = 1e4

RET_W = RET_HEADS * RET_DV
NSA_W = NSA_HEADS * NSA_DK
KV_W = NSA_KV_HEADS * NSA_DK
MEM_W = MEM_HEADS * MEM_DH

COL_RQ = 0
COL_RK = COL_RQ + RET_HEADS * RET_DK
COL_RV = COL_RK + RET_HEADS * RET_DK
COL_RG = COL_RV + RET_W
COL_NQ = COL_RG + RET_W
COL_KVC = COL_NQ + NSA_W
COL_KVS = COL_KVC + 2 * KV_W
COL_KVW = COL_KVS + 2 * KV_W
PROJ_A = COL_KVW + 2 * KV_W
COL_NSL = 0
COL_MG = COL_NSL + NSA_W
COL_MQ = COL_MG + N_BRANCHES * D_MODEL
COL_NG = COL_MQ + MEM_W
NG_SLOT = NSA_KV_HEADS * 128
PROJ_B = COL_NG + NG_SLOT
MQ_BLOCK = MEM_W + NG_SLOT

LOG2E = math.log2(math.e)
LANE = 128
TQ = 256
TK = 256
NSA_ROWS = NSA_GROUP * TQ
BIAS_DISTS = 1024
FAR_GROUP = 4
RET_STEP_CHUNKS = 8
PROJ_TN = 1024
VMEM_LIMIT = 56 * 1024 * 1024


def _cparams(n_axes):
    return pltpu.CompilerParams(dimension_semantics=("arbitrary",) * n_axes, vmem_limit_bytes=VMEM_LIMIT)


def _nt(a, b):
    return lax.dot_general(a, b, (((1,), (1,)), ((), ())), preferred_element_type=F32)


def _dot(a, b):
    return jnp.dot(a, b, preferred_element_type=F32)


def _sigmoid(x):
    return 1.0 / (1.0 + jnp.exp(-x))


def _iota(shape, dim):
    return lax.broadcasted_iota(jnp.int32, shape, dim)


def _rmsnorm_kernel(x_ref, g_ref, h_ref):
    x = x_ref[...]
    ms = jnp.mean(x * x, axis=-1, keepdims=True)
    h_ref[...] = ((x * lax.rsqrt(ms + EPS)) * g_ref[...]).astype(BF16)


def _rmsnorm(x, g, tm):
    m, k = x.shape
    return pl.pallas_call(
        _rmsnorm_kernel,
        grid=(m // tm,),
        in_specs=[pl.BlockSpec((tm, k), lambda i: (i, 0)), pl.BlockSpec((1, k), lambda i: (0, 0))],
        out_specs=pl.BlockSpec((tm, k), lambda i: (i, 0)),
        out_shape=jax.ShapeDtypeStruct((m, k), BF16),
        compiler_params=_cparams(1),
        name="rmsnorm",
    )(x, g.reshape(1, k))


def _proj_kernel(h_ref, w_ref, *refs, transposed):
    o_ref, wb_ref = refs[-3 if len(refs) == 4 else -2], refs[-1]

    @pl.when(pl.program_id(1) == 0)
    def _():
        if transposed:
            _store_transposed(wb_ref, w_ref, wb_ref.shape[1])
        else:
            wb_ref[...] = w_ref[...].astype(BF16)
        if len(refs) == 4:
            refs[2][...] = _dot(refs[0][...], wb_ref[...])

    o_ref[...] = _dot(h_ref[...], wb_ref[...])


def _rider_specs(rider, k, tn):
    if rider is None:
        return [], [], []
    rows = rider.shape[0]
    return ([pl.BlockSpec((rows, k), lambda j, i: (0, 0))], [pl.BlockSpec((rows, tn), lambda j, i: (0, j))], [rows])


def _proj(h, w, tm, tn, n=None, transposed=False, rider=None):
    m, k = h.shape
    n = w.shape[0 if transposed else 1] if n is None else n
    wspec = pl.BlockSpec((tn, k), lambda j, i: (j, 0)) if transposed else pl.BlockSpec((k, tn), lambda j, i: (0, j))
    r_in, r_out, r_rows = _rider_specs(rider, k, tn)
    out = pl.pallas_call(
        functools.partial(_proj_kernel, transposed=transposed),
        grid=(n // tn, m // tm),
        in_specs=[pl.BlockSpec((tm, k), lambda j, i: (i, 0)), wspec] + r_in,
        out_specs=[pl.BlockSpec((tm, tn), lambda j, i: (i, j))] + r_out,
        out_shape=[jax.ShapeDtypeStruct((m, n), F32)] + [jax.ShapeDtypeStruct((r, n), F32) for r in r_rows],
        scratch_shapes=[pltpu.VMEM((k, tn), BF16)],
        compiler_params=_cparams(2),
        name="proj",
    )(h, w, *([] if rider is None else [rider]))
    return out[0] if rider is None else tuple(out)


def _proj_kv_kernel(h_ref, w_ref, hs_ref, o_ref, oc_ref, os_ref, ow_ref, ors_ref, wb_ref, *, tiles_per_batch):
    j, i = pl.program_id(0), pl.program_id(1)
    n_tiles = pl.num_programs(0)

    @pl.when(i == 0)
    def _():
        _store_transposed(wb_ref, w_ref, wb_ref.shape[1])
        ors_ref[...] = _dot(hs_ref[...], wb_ref[...])

    res = _dot(h_ref[...], wb_ref[...])
    o_ref[...] = res
    tm, tn = res.shape
    rows_kv = tn // NSA_DK

    @pl.when(j == n_tiles - 3)
    def _():
        oc_ref[...] = res.reshape(tm * rows_kv, NSA_DK)

    @pl.when(j == n_tiles - 2)
    def _():
        os_ref[...] = res.reshape(tm * rows_kv, NSA_DK)

    @pl.when((j == n_tiles - 1) & (i % tiles_per_batch == tiles_per_batch - 1))
    def _():
        ow_ref[...] = res[tm - WINDOW:, :].reshape(WINDOW * rows_kv, NSA_DK)


def _proj_kv(h, w_t, tm, seq, rider):
    m, k = h.shape
    tn = 2 * KV_W
    assert COL_KVC % tn == 0 and COL_KVC == PROJ_A - 3 * tn and seq % tm == 0 and tm >= WINDOW
    n_j, n_i = 3, m // tm
    first = COL_KVC // tn
    tiles_per_batch = seq // tm
    rows_kv = tn // NSA_DK

    def kv_rows(tile):
        return lambda j, i: (jnp.where(j < tile, 0, jnp.where(j == tile, i, n_i - 1)), 0)

    r_in, r_out, (r_rows,) = _rider_specs(rider, k, tn)
    return pl.pallas_call(
        functools.partial(_proj_kv_kernel, tiles_per_batch=tiles_per_batch),
        grid=(n_j, n_i),
        in_specs=[pl.BlockSpec((tm, k), lambda j, i: (i, 0)),
                  pl.BlockSpec((tn, k), lambda j, i: (first + j, 0))] + r_in,
        out_specs=[pl.BlockSpec((tm, tn), lambda j, i: (i, j)),
                   pl.BlockSpec((tm * rows_kv, NSA_DK), kv_rows(n_j - 3)),
                   pl.BlockSpec((tm * rows_kv, NSA_DK), kv_rows(n_j - 2)),
                   pl.BlockSpec((WINDOW * rows_kv, NSA_DK),
                                lambda j, i: (jnp.where(j < n_j - 1, 0, i // tiles_per_batch), 0))] + r_out,
        out_shape=[jax.ShapeDtypeStruct((m, n_j * tn), F32),
                   jax.ShapeDtypeStruct((m * rows_kv, NSA_DK), F32),
                   jax.ShapeDtypeStruct((m * rows_kv, NSA_DK), F32),
                   jax.ShapeDtypeStruct((m // seq * WINDOW * rows_kv, NSA_DK), F32),
                   jax.ShapeDtypeStruct((r_rows, n_j * tn), F32)],
        scratch_shapes=[pltpu.VMEM((k, tn), BF16)],
        compiler_params=_cparams(2),
        name="proj_kv",
    )(h, w_t, rider)


TAIL_TN = 1024


def _store_transposed(dst_ref, src_ref, rows, chunk=256):
    for r0 in range(0, rows, chunk):
        dst_ref[:, r0:r0 + chunk] = src_ref[r0:r0 + chunk, :].T.astype(BF16)


def _proj_tail_kernel(h_ref, w_ref, wng_ref, hs_ref, o_ref, ors_ref, wb_ref, *, last_rows):
    tn = wb_ref.shape[1]
    is_last = pl.program_id(0) == pl.num_programs(0) - 1

    @pl.when((pl.program_id(1) == 0) & jnp.logical_not(is_last))
    def _():
        _store_transposed(wb_ref, w_ref, tn)
        ors_ref[...] = _dot(hs_ref[...], wb_ref[...])

    @pl.when((pl.program_id(1) == 0) & is_last)
    def _():
        _store_transposed(wb_ref, w_ref, last_rows)
        wb_ref[:, last_rows:] = wng_ref[...].T.astype(BF16)
        ors_ref[...] = _dot(hs_ref[...], wb_ref[...])

    o_ref[...] = _dot(h_ref[...], wb_ref[...])


def _proj_tail(h, w_t, w_ng, tm, rider):
    m, k = h.shape
    tn = TAIL_TN
    row_ng = PROJ_A
    row_nsl = row_ng + N_BRANCHES * NSA_HEADS
    row_mq = row_nsl + NSA_W
    row_mg = row_mq + MEM_W
    assert row_mg + N_BRANCHES * D_MODEL == w_t.shape[0] and w_ng.shape == (NG_SLOT, k)
    assert COL_NSL == 0 and COL_MG % tn == 0 and COL_MQ % tn == 0 and COL_NG + NG_SLOT == PROJ_B == COL_MQ + 2 * tn
    assert row_mq % 16 == 0 and row_nsl % 16 == 0 and row_mg % 16 == 0
    t_mg, t_mq = COL_MG // tn, COL_MQ // tn

    def w_row(j, i):
        row = jnp.where(j < t_mg, row_nsl + j * tn,
                        jnp.where(j < t_mq, row_mg + (j - t_mg) * tn, row_mq + (j - t_mq) * tn))
        return (pl.multiple_of(row, 16), 0)

    r_in, r_out, (r_rows,) = _rider_specs(rider, k, tn)
    return pl.pallas_call(
        functools.partial(_proj_tail_kernel, last_rows=COL_NG - COL_MQ - tn),
        grid=(PROJ_B // tn, m // tm),
        in_specs=[pl.BlockSpec((tm, k), lambda j, i: (i, 0)),
                  pl.BlockSpec((pl.Element(tn), pl.Element(k)), w_row),
                  pl.BlockSpec((NG_SLOT, k), lambda j, i: (0, 0))] + r_in,
        out_specs=[pl.BlockSpec((tm, tn), lambda j, i: (i, j))] + r_out,
        out_shape=[jax.ShapeDtypeStruct((m, PROJ_B), F32), jax.ShapeDtypeStruct((r_rows, PROJ_B), F32)],
        scratch_shapes=[pltpu.VMEM((k, tn), BF16)],
        compiler_params=_cparams(2),
        name="proj_tail",
    )(h, w_t, w_ng, rider)


def _rope_rows(x, cos, sin):
    half = x.shape[-1] // 2
    x1, x2 = x[:, :half], x[:, half:]
    return jnp.concatenate([x1 * cos - x2 * sin, x1 * sin + x2 * cos], axis=-1)


def _head_norm_gate(o, gnorm, rg):
    oc = o - jnp.mean(o, axis=-1, keepdims=True)
    y = oc * lax.rsqrt(jnp.mean(oc * oc, axis=-1, keepdims=True) + EPS) * gnorm
    return y * (rg * _sigmoid(rg))


def _ret_prompt_kernel(q_ref, k_ref, v_ref, rg_ref, cos_ref, sin_ref, dmat_ref, xi_ref, zeta_ref, gc_ref,
                       gn_ref, a_ref, s_ref):
    @pl.when(pl.program_id(2) == 0)
    def _():
        s_ref[...] = jnp.zeros_like(s_ref)

    c = RET_CHUNK
    for t in range(q_ref.shape[0] // c):
        rows = slice(t * c, (t + 1) * c)
        cos, sin = cos_ref[rows, :], sin_ref[rows, :]
        q = _rope_rows(q_ref[rows, :], cos, sin)
        k = _rope_rows(k_ref[rows, :], cos, sin) * (RET_DK ** -0.5)
        qb, vb = q.astype(BF16), v_ref[rows, :].astype(BF16)
        state = s_ref[0, 0]
        inner = _nt(qb, k.astype(BF16)) * dmat_ref[0]
        o = _dot(inner.astype(BF16), vb) + _dot(qb, state.astype(BF16)) * xi_ref[0]
        kz_t = (k * zeta_ref[0]).T.astype(BF16)
        s_ref[0, 0] = state * gc_ref[0] + _dot(kz_t, vb)
        a_ref[rows, :] = _head_norm_gate(o, gn_ref[...], rg_ref[rows, :]).astype(BF16)


def _decay_tables(chunk):
    log_g = jnp.log1p(-jnp.exp2(-5.0 - jnp.arange(RET_HEADS, dtype=F32)))
    i = jnp.arange(chunk, dtype=F32)
    diff = i[:, None] - i[None, :]
    dmat = jnp.where(diff >= 0, jnp.exp(log_g[:, None, None] * jnp.maximum(diff, 0.0)), 0.0)
    xi = jnp.exp(log_g[:, None] * (i[None, :] + 1.0))[:, :, None]
    zeta = jnp.exp(log_g[:, None] * (chunk - 1.0 - i[None, :]))[:, :, None]
    g_chunk = jnp.exp(log_g * chunk)[:, None, None]
    return dmat, xi, zeta, g_chunk


def _rope_tables(pos):
    half = RET_DK // 2
    freq = jnp.power(ROPE_BASE, -jnp.arange(half, dtype=F32) / half)
    ang = pos.astype(F32)[:, None] * freq[None, :]
    return jnp.cos(ang), jnp.sin(ang)


def _retention_prompt(z, ret_norm, batch, seq):
    c = RET_CHUNK
    rows = _pick_tile(seq, RET_STEP_CHUNKS * c)
    nc = seq // rows
    dmat, xi, zeta, g_chunk = _decay_tables(c)
    cos, sin = _rope_tables(jnp.arange(seq))
    hb = RET_DK

    def zspec(col0):
        return pl.BlockSpec((rows, hb), lambda b, h, t, col0=col0: (b * nc + t, col0 // hb + h))

    per_head = lambda shape: pl.BlockSpec((1,) + shape, lambda b, h, t: (h, 0, 0))
    return pl.pallas_call(
        _ret_prompt_kernel,
        grid=(batch, RET_HEADS, nc),
        in_specs=[zspec(COL_RQ), zspec(COL_RK), zspec(COL_RV), zspec(COL_RG),
                  pl.BlockSpec((rows, hb // 2), lambda b, h, t: (t, 0)),
                  pl.BlockSpec((rows, hb // 2), lambda b, h, t: (t, 0)),
                  per_head((c, c)), per_head((c, 1)), per_head((c, 1)), per_head((1, 1)),
                  pl.BlockSpec((1, hb), lambda b, h, t: (0, h))],
        out_specs=[pl.BlockSpec((rows, hb), lambda b, h, t: (b * nc + t, h)),
                   pl.BlockSpec((1, 1, RET_DK, RET_DV), lambda b, h, t: (b, h, 0, 0))],
        out_shape=[jax.ShapeDtypeStruct((batch * seq, RET_W), BF16),
                   jax.ShapeDtypeStruct((batch, RET_HEADS, RET_DK, RET_DV), F32)],
        compiler_params=_cparams(3),
        name="retention_prompt",
    )(z, z, z, z, cos, sin, dmat, xi, zeta, g_chunk, ret_norm.reshape(1, RET_W))


def _column_of(row):
    n = row.shape[1]
    eye = _iota((n, n), 0) == _iota((n, n), 1)
    return jnp.sum(jnp.where(eye, jnp.broadcast_to(row, (n, n)), 0.0), axis=1, keepdims=True)


def _ret_sample_kernel(q_ref, k_ref, v_ref, rg_ref, cos_ref, sin_ref, gam_ref, gn_ref, s_ref, a_ref, so_ref):
    cos, sin = cos_ref[...], sin_ref[...]
    outs = []
    for h in range(RET_HEADS):
        cols = slice(h * RET_DK, (h + 1) * RET_DK)
        q = _rope_rows(q_ref[0][:, cols], cos, sin)
        k = _rope_rows(k_ref[0][:, cols], cos, sin) * (RET_DK ** -0.5)
        v = v_ref[0][:, cols]
        state = s_ref[0, h]
        gamma = gam_ref[h]
        qk = jnp.sum(q * k, axis=-1, keepdims=True)
        o = qk * v + jnp.sum(_column_of(q) * state, axis=0, keepdims=True) * gamma
        so_ref[0, h] = state * gamma + _column_of(k) * v
        outs.append(_head_norm_gate(o, gn_ref[:, cols], rg_ref[0][:, cols]))
    a_ref[0] = jnp.concatenate(outs, axis=1).astype(BF16)


def _retention_sample(z3, state, ret_norm, pos):
    nb = z3.shape[0]
    cos, sin = _rope_tables(jnp.full((1,), pos))
    gamma = jnp.exp(jnp.log1p(-jnp.exp2(-5.0 - jnp.arange(RET_HEADS, dtype=F32))))[:, None, None]
    hb = RET_DK
    assert RET_HEADS * RET_DK == RET_W

    def zspec(col0):
        return pl.BlockSpec((1, 1, RET_W), lambda b, col0=col0: (b, 0, col0 // RET_W))

    st_spec = pl.BlockSpec((1, RET_HEADS, RET_DK, RET_DV), lambda b: (b, 0, 0, 0))
    return pl.pallas_call(
        _ret_sample_kernel,
        grid=(nb,),
        in_specs=[zspec(COL_RQ), zspec(COL_RK), zspec(COL_RV), zspec(COL_RG),
                  pl.BlockSpec((1, hb // 2), lambda b: (0, 0)),
                  pl.BlockSpec((1, hb // 2), lambda b: (0, 0)),
                  pl.BlockSpec((RET_HEADS, 1, 1), lambda b: (0, 0, 0)),
                  pl.BlockSpec((1, RET_W), lambda b: (0, 0)),
                  st_spec],
        out_specs=[pl.BlockSpec((1, 1, RET_W), lambda b: (b, 0, 0)), st_spec],
        out_shape=[jax.ShapeDtypeStruct((nb, 1, RET_W), BF16),
                   jax.ShapeDtypeStruct(state.shape, F32)],
        compiler_params=_cparams(1),
        name="retention_sample",
    )(z3, z3, z3, z3, cos, sin, gamma, ret_norm.reshape(1, RET_W), state)


def _half_rows(ref_slice_fn, n_half):
    return jnp.concatenate([ref_slice_fn(p) for p in range(CMP_STRIDE)], axis=1)


def _cmp_stage1_dense_kernel(x_ref, w_ref, o_ref):
    nh = o_ref.shape[3]
    x = _half_rows(lambda p: x_ref[pl.ds(p, nh, stride=CMP_STRIDE), :], nh).astype(BF16)
    o_ref[0, 0, 0] = _dot(x, w_ref[0])


def _cmp_stage1_dense(z, w1ab, batch, seq):
    nh = seq // CMP_STRIDE
    return pl.pallas_call(
        _cmp_stage1_dense_kernel,
        grid=(batch, 2, NSA_KV_HEADS),
        in_specs=[pl.BlockSpec((seq, NSA_DK), lambda b, kv, h: (b, kv * NSA_KV_HEADS + h)),
                  pl.BlockSpec((1, CMP_STRIDE * NSA_DK, 2 * NSA_DK), lambda b, kv, h: (kv, 0, 0))],
        out_specs=pl.BlockSpec((1, 1, 1, nh, 2 * NSA_DK), lambda b, kv, h: (b, kv, h, 0, 0)),
        out_shape=jax.ShapeDtypeStruct((batch, 2, NSA_KV_HEADS, nh, 2 * NSA_DK), F32),
        compiler_params=_cparams(3),
        name="cmp_stage1_dense",
    )(z, w1ab)


PAGES_PER_STEP = 16


def _cmp_stage1_paged_kernel(pt_ref, *refs):
    pages, (w_ref, o_ref) = refs[:PAGES_PER_STEP], refs[PAGES_PER_STEP:]
    hp = PAGE_SIZE // CMP_STRIDE
    top = _iota((2 * NSA_KV_HEADS, NSA_DK), 0) < NSA_KV_HEADS
    cols = [[], []]
    for p in range(CMP_STRIDE):
        tiles = [[], []]
        for pg in pages:
            xp = pg[0, :, p]
            for n in range(0, hp, 2):
                a, b = xp[n], xp[n + 1]
                tiles[0].append(jnp.where(top, a, pltpu.roll(b, NSA_KV_HEADS, 0)))
                tiles[1].append(jnp.where(top, pltpu.roll(a, NSA_KV_HEADS, 0), b))
        for kv in range(2):
            cols[kv].append(jnp.concatenate(tiles[kv], axis=0))
    for kv in range(2):
        x = jnp.concatenate(cols[kv], axis=1).astype(BF16)
        o_ref[0, kv, 0] = _dot(x, w_ref[kv])


def _cmp_stage1_paged(cache, page_table, w1ab):
    nb, n_pages = page_table.shape
    hp = PAGE_SIZE // CMP_STRIDE
    steps = n_pages // PAGES_PER_STEP
    rows = PAGES_PER_STEP * hp * NSA_KV_HEADS

    def page_spec(j):
        return pl.BlockSpec((1, hp, CMP_STRIDE, 2 * NSA_KV_HEADS, NSA_DK),
                            lambda b, s, pt, j=j: (pt[b, s * PAGES_PER_STEP + j], 0, 0, 0, 0))

    wspec = pl.BlockSpec((2, CMP_STRIDE * NSA_DK, 2 * NSA_DK), lambda b, s, pt: (0, 0, 0))
    return pl.pallas_call(
        _cmp_stage1_paged_kernel,
        grid_spec=pltpu.PrefetchScalarGridSpec(
            num_scalar_prefetch=1,
            grid=(nb, steps),
            in_specs=[page_spec(j) for j in range(PAGES_PER_STEP)] + [wspec],
            out_specs=pl.BlockSpec((1, 2, 1, rows, 2 * NSA_DK), lambda b, s, pt: (b, 0, 0, s, 0))),
        out_shape=jax.ShapeDtypeStruct((nb, 2, 1, steps * rows, 2 * NSA_DK), F32),
        compiler_params=_cparams(2),
        name="cmp_stage1_paged",
    )(page_table, *([cache] * PAGES_PER_STEP), w1ab)


def _cmp_stage2_kernel(ac_ref, pos_ref, w1a_ref, w1b_ref, w2_ref, o_ref, *, shift):
    ac = ac_ref[0, 0, 0]
    nh = ac.shape[0]
    pos = pos_ref[0].astype(BF16)
    kw = CMP_STRIDE * NSA_DK
    pe = _dot(pos[:, :kw], w1a_ref[0]) + _dot(pos[:, kw:], w1b_ref[0])
    pre = ac[:, :NSA_DK] + pltpu.roll(ac[:, NSA_DK:], nh - shift, 0) + pe[0:1]
    gelu = 0.5 * pre * (1.0 + jnp.tanh(math.sqrt(2.0 / math.pi) * (pre + 0.044715 * (pre * pre * pre))))
    o_ref[0, 0, 0] = _dot(gelu.astype(BF16), w2_ref[0]).astype(BF16)


def _cmp_stage2(ac, pos8, w1a, w1b, w2, shift):
    nb, _, groups, nh, _ = ac.shape
    kw = CMP_STRIDE * NSA_DK
    return pl.pallas_call(
        functools.partial(_cmp_stage2_kernel, shift=shift),
        grid=(nb, 2, groups),
        in_specs=[pl.BlockSpec((1, 1, 1, nh, 2 * NSA_DK), lambda b, kv, h: (b, kv, h, 0, 0)),
                  pl.BlockSpec((1, 8, 2 * kw), lambda b, kv, h: (kv, 0, 0)),
                  pl.BlockSpec((1, kw, NSA_DK), lambda b, kv, h: (kv, 0, 0)),
                  pl.BlockSpec((1, kw, NSA_DK), lambda b, kv, h: (kv, 0, 0)),
                  pl.BlockSpec((1, NSA_DK, NSA_DK), lambda b, kv, h: (kv, 0, 0))],
        out_specs=pl.BlockSpec((1, 1, 1, nh, NSA_DK), lambda b, kv, h: (b, kv, h, 0, 0)),
        out_shape=jax.ShapeDtypeStruct((nb, 2, groups, nh, NSA_DK), BF16),
        compiler_params=_cparams(3),
        name="cmp_stage2",
    )(ac, pos8, w1a, w1b, w2)


def _rel_bucket(dist):
    n = jnp.maximum(dist, 0)
    nf = jnp.maximum(n, 1).astype(F32)
    scale = (REL_BUCKETS - REL_MAX_EXACT) / math.log(REL_MAX_DIST / REL_MAX_EXACT)
    large = REL_MAX_EXACT + (jnp.log(nf / REL_MAX_EXACT) * scale).astype(jnp.int32)
    large = jnp.minimum(large, REL_BUCKETS - 1)
    return jnp.where(n < REL_MAX_EXACT, n, large)


def _bias_by_dist(rel_table, n):
    tab = rel_table[_rel_bucket(jnp.arange(n))]
    return tab.T.reshape(NSA_KV_HEADS, NSA_GROUP, n)


def _pad_group_rows(t, axis):
    first = lax.slice_in_dim(t, 0, 1, axis=axis)
    return jnp.concatenate([t] + [first] * (SROWS - NSA_GROUP), axis=axis)


def _flash_init(m_ref, l_ref, acc_ref):
    m_ref[...] = jnp.full(m_ref.shape, NEG_INF, F32)
    l_ref[...] = jnp.zeros(l_ref.shape, F32)
    acc_ref[...] = jnp.zeros(acc_ref.shape, F32)


def _flash_step(s, v, m_ref, l_ref, acc_ref):
    m_old = m_ref[...]
    m_new = jnp.maximum(m_old, jnp.max(s, axis=1, keepdims=True))
    alpha = jnp.exp(m_old - m_new)
    p = jnp.exp(s - m_new)
    l_ref[...] = alpha * l_ref[...] + jnp.sum(p, axis=1, keepdims=True)
    acc_ref[...] = alpha * acc_ref[...] + _dot(p.astype(BF16), v)
    m_ref[...] = m_new


def _flash_result(l_ref, acc_ref):
    return acc_ref[...] / jnp.maximum(l_ref[...], 1e-30)


def _select_blocks(imp_t, q0, ns):
    shape = imp_t.shape
    blk = _iota(shape, 0)
    qpos = q0 + _iota(shape, 1)
    cur = qpos >> 6
    valid = blk * SEL_BLOCK <= qpos
    forced = (blk == 0) | (blk == cur) | (blk == cur - 1)
    imp_t = jnp.where(valid, jnp.where(forced, FORCE_SCORE, imp_t), NEG_INF)
    rank = jnp.zeros(shape, F32)
    for other in range(ns):
        row = imp_t[other:other + 1, :]
        ahead = (row > imp_t) | ((row == imp_t) & (blk > other))
        rank = rank + jnp.where(ahead, 1.0, 0.0)
    return jnp.where((rank < SEL_TOPK) & valid, 1.0, 0.0)


def _nsa_prompt_kernel(q_ref, ks_ref, vs_ref, kw_ref, vw_ref, kc_ref, vct_ref, wd_ref, cfar_ref, basec_ref,
                       ovt_ref, ng_ref, nsl_ref, o_ref, m_ref, l_ref, acc_ref, vst_ref, vwt_ref, tz_ref, ow_ref, *, ns):
    i = pl.program_id(2)
    q0 = i * TQ
    qall = q_ref[...] * ((NSA_DK ** -0.5) * LOG2E)
    qt = jnp.concatenate([qall[:, g * NSA_DK:(g + 1) * NSA_DK].T for g in range(NSA_GROUP)], axis=1).astype(BF16)
    c_loc = _iota((TK, NSA_ROWS), 0)
    r_loc = _iota((TK, NSA_ROWS), 1) & (TQ - 1)

    @pl.when(i == 0)
    def _():
        for kt in range(vst_ref.shape[0]):
            vst_ref[kt] = vs_ref[kt * TK:(kt + 1) * TK, :].T.astype(BF16)
            vwt_ref[kt] = vw_ref[kt * TK:(kt + 1) * TK, :].T.astype(BF16)
        below = _iota((TK, TQ), 1) >= _iota((TK, TQ), 0)
        for g in range(NSA_GROUP):
            lo = pltpu.roll(jnp.broadcast_to(wd_ref[0, 2 * g:2 * g + 1, :], (TK, TQ)), 0, 1, stride=1, stride_axis=0)
            hi = pltpu.roll(jnp.broadcast_to(wd_ref[0, 2 * g + 1:2 * g + 2, :], (TK, TQ)), 0, 1, stride=1, stride_axis=0)
            tz_ref[0, :, g * TQ:(g + 1) * TQ] = lo
            tz_ref[1, :, g * TQ:(g + 1) * TQ] = jnp.where(below, hi, lo)

    ncp = kc_ref.shape[3]
    shift = (TQ // CMP_STRIDE) * i
    bias_c = basec_ref[0, pl.ds(pl.multiple_of(ncp - shift, TQ // CMP_STRIDE), ncp), :]
    s = _dot(kc_ref[0, 0, 0], qt) + bias_c
    m = jnp.max(s, axis=0, keepdims=True)
    e = jnp.exp2(s - m)
    inv = jnp.where(m > 0.5 * NEG_INF, 1.0 / jnp.maximum(jnp.sum(e, axis=0, keepdims=True), 1e-30), 0.0)
    p = e * inv
    o_cmp = _dot(vct_ref[0, 0], p.astype(BF16))

    psum = p[:, 0:TQ] + p[:, TQ:2 * TQ] + p[:, 2 * TQ:3 * TQ] + p[:, 3 * TQ:4 * TQ]
    hi = psum.astype(BF16)
    lo = (psum - hi.astype(F32)).astype(BF16)
    ovt = ovt_ref[...]
    imp_t = _dot(ovt, hi) + _dot(ovt, lo)
    ns8 = -(-ns // 8) * 8
    sel_t = _select_blocks(imp_t[:ns8], q0, ns)
    assert ns8 + 2 <= LANE
    sel_neg = jnp.concatenate([jnp.where(sel_t > 0.5, 0.0, NEG_INF)] * NSA_GROUP, axis=1)
    cfar = cfar_ref[0]
    cfar_hi = cfar.astype(BF16).astype(F32)
    sub8 = _iota((8, NSA_ROWS), 0)
    cfar8 = jnp.where(sub8 == 0, cfar_hi, jnp.where(sub8 == 1, cfar - cfar_hi, 0.0))
    pad_rows = jnp.zeros((LANE - ns8 - 8, NSA_ROWS), F32)
    qx_far = jnp.concatenate([qt, jnp.concatenate([sel_neg, cfar8, pad_rows], axis=0).astype(BF16)], axis=0)
    qx_near = jnp.concatenate([qt, jnp.concatenate([sel_neg, jnp.zeros_like(cfar8), pad_rows], axis=0).astype(BF16)],
                              axis=0)
    lane_k = _iota((TK, LANE), 1)
    ones_k = (lane_k == ns8) | (lane_k == ns8 + 1)

    def flash_step(scs, vts):
        m_old = m_ref[...]
        m_new = m_old
        for sc in scs:
            m_new = jnp.maximum(m_new, jnp.max(sc, axis=0, keepdims=True))
        alpha = jnp.exp2(m_old - m_new)
        l_new = alpha * l_ref[...]
        acc = alpha * acc_ref[...]
        for sc, vt in zip(scs, vts):
            pt = jnp.exp2(sc - m_new)
            l_new = l_new + jnp.sum(pt, axis=0, keepdims=True)
            acc = acc + _dot(vt, pt.astype(BF16))
        l_ref[...] = l_new
        acc_ref[...] = acc
        m_ref[...] = m_new

    def sel_scores(kt, bias, causal):
        k = ks_ref[pl.ds(pl.multiple_of(kt * TK, TK), TK), :].astype(BF16)
        blk_of_key = (TK // SEL_BLOCK) * kt + (_iota((TK, LANE), 0) >> 6)
        extra = jnp.where((lane_k == blk_of_key) | ones_k, 1.0, 0.0).astype(BF16)
        kx = jnp.concatenate([k, extra], axis=1)
        sc = _dot(kx, qx_far) if bias is None else _dot(kx, qx_near) + bias
        if causal:
            sc = jnp.where(c_loc <= r_loc, sc, NEG_INF)
        return sc

    def softmax_tiles(scs, vts):
        m_new = jnp.max(scs[0], axis=0, keepdims=True)
        for sc in scs[1:]:
            m_new = jnp.maximum(m_new, jnp.max(sc, axis=0, keepdims=True))
        l_new, acc = None, None
        for sc, vt in zip(scs, vts):
            pt = jnp.exp2(sc - m_new)
            l_t, acc_t = jnp.sum(pt, axis=0, keepdims=True), _dot(vt, pt.astype(BF16))
            l_new, acc = (l_t, acc_t) if l_new is None else (l_new + l_t, acc + acc_t)
        return m_new, l_new, acc

    kt_prev = jnp.maximum(i - 1, 0)
    prev_scores = jnp.where(i >= 1, sel_scores(kt_prev, tz_ref[1], False), NEG_INF)
    m_ref[...], l_ref[...], acc_ref[...] = softmax_tiles([sel_scores(i, tz_ref[0], True), prev_scores],
                                                         [vst_ref[i], vst_ref[kt_prev]])

    def win_scores(off):
        kt = jnp.maximum(i - off, 0)
        k = kw_ref[pl.ds(pl.multiple_of(kt * TK, TK), TK), :].astype(BF16)
        if off < 2:
            sc = _dot(k, qt) + tz_ref[off]
        else:
            sc = _dot(jnp.concatenate([k, jnp.where(ones_k, 1.0, 0.0).astype(BF16)], axis=1), qx_far)
        if off == 0:
            return jnp.where(c_loc <= r_loc, sc, NEG_INF)
        keep = (c_loc > r_loc) & (i >= off) if off * TK == WINDOW else (i >= off)
        return jnp.where(keep, sc, NEG_INF)

    n_win = WINDOW // TK + 1
    _, l_win, acc_win = softmax_tiles([win_scores(off) for off in range(n_win)],
                                      [vwt_ref[jnp.maximum(i - off, 0)] for off in range(n_win)])
    ow_ref[...] = acc_win / jnp.maximum(l_win, 1e-30)

    def far_tiles(kt_first, count):
        kts = [kt_first - t for t in range(count)]
        flash_step([sel_scores(kt, None, False) for kt in kts], [vst_ref[kt] for kt in kts])

    def far_group(j, carry):
        far_tiles(i - 2 - FAR_GROUP * j, FAR_GROUP)
        return carry

    n_far = jnp.maximum(i - 1, 0)
    lax.fori_loop(0, n_far // FAR_GROUP, far_group, 0)
    for rest in range(1, FAR_GROUP):
        pl.when(n_far % FAR_GROUP == rest)(functools.partial(far_tiles, rest - 1, rest))

    o_sel = _flash_result(l_ref, acc_ref)
    o_win = ow_ref[...]

    gates_t = _sigmoid(ng_ref[...]).T
    nsl = nsl_ref[...]
    outs = []
    for g in range(NSA_GROUP):
        cols = slice(g * TQ, (g + 1) * TQ)
        o_t = (gates_t[3 * g:3 * g + 1] * o_cmp[:, cols] + gates_t[3 * g + 1:3 * g + 2] * o_sel[:, cols]
               + gates_t[3 * g + 2:3 * g + 3] * o_win[:, cols])
        x = nsl[:, g * NSA_DK:(g + 1) * NSA_DK]
        outs.append(o_t.T * (x * _sigmoid(x)))
    o_ref[...] = jnp.concatenate(outs, axis=1).astype(BF16)


def _lanes_by_head(t):
    hk, g, rows, tq = t.shape
    return t.transpose(0, 2, 1, 3).reshape(hk, rows, g * tq)


def _nsa_prompt(z, zkv, zb, kcvc, bias_d, batch, seq):
    nq = seq // TQ
    ns = seq // SEL_BLOCK
    ncp = kcvc.shape[3]
    nsp = LANE
    wn = TQ // CMP_STRIDE
    assert TQ == TK and ns <= nsp and ncp >= seq // CMP_STRIDE and ncp % LANE == 0 and ncp > wn
    gw = NSA_GROUP * NSA_DK
    n_dist = bias_d.shape[-1]
    assert n_dist >= 2 * TK + TQ
    bias_d = bias_d * LOG2E

    tz = bias_d[..., :2 * TQ].reshape(NSA_KV_HEADS, 2 * NSA_GROUP, TQ)
    far = jnp.broadcast_to(bias_d[..., REL_MAX_DIST][:, :, None, None], (NSA_KV_HEADS, NSA_GROUP, 1, TQ))
    cfar = _lanes_by_head(far)
    half = n_dist // 2
    start = CMP_STRIDE * wn - (CMP_BLOCK - 1)
    assert start + TQ <= half and 2 * wn * CMP_STRIDE - start <= half
    w_ext = jnp.concatenate([bias_d[..., :half], jnp.full(bias_d.shape[:-1] + (n_dist - half,), NEG_INF, F32)], -1)
    near = jnp.tile(w_ext, (1, 1, 2 * wn + 1))[..., :2 * wn * (n_dist - CMP_STRIDE)]
    near = near.reshape(NSA_KV_HEADS, NSA_GROUP, 2 * wn, n_dist - CMP_STRIDE)[..., start:start + TQ]
    basec = jnp.concatenate([jnp.broadcast_to(far, (NSA_KV_HEADS, NSA_GROUP, ncp - wn, TQ)), near,
                             jnp.full((NSA_KV_HEADS, NSA_GROUP, ncp - wn, TQ), NEG_INF, F32)], axis=2)
    basec = _lanes_by_head(basec)
    sblk = jnp.arange(nsp)[:, None]
    nblk = jnp.arange(ncp)[None, :]
    ovt = ((nblk >= 4 * sblk - 1) & (nblk <= 4 * sblk + 3)).astype(BF16)

    vct = kcvc[:, 1].transpose(0, 1, 3, 2)

    def kvspec(col0, which):
        return pl.BlockSpec((seq, NSA_DK),
                            lambda b, h, i: (b, (col0 - COL_KVC) // NSA_DK + which * NSA_KV_HEADS + h))

    vt_scratch = pltpu.VMEM((seq // TK, NSA_DK, TK), BF16)
    return pl.pallas_call(
        functools.partial(_nsa_prompt_kernel, ns=ns),
        grid=(batch, NSA_KV_HEADS, nq),
        in_specs=[pl.BlockSpec((TQ, gw), lambda b, h, i: (b * nq + i, COL_NQ // gw + h)),
                  kvspec(COL_KVS, 0), kvspec(COL_KVS, 1), kvspec(COL_KVW, 0), kvspec(COL_KVW, 1),
                  pl.BlockSpec((1, 1, 1, ncp, NSA_DK), lambda b, h, i: (b, 0, h, 0, 0)),
                  pl.BlockSpec((1, 1, NSA_DK, ncp), lambda b, h, i: (b, h, 0, 0)),
                  pl.BlockSpec((1, 2 * NSA_GROUP, TQ), lambda b, h, i: (h, 0, 0)),
                  pl.BlockSpec((1, 1, NSA_ROWS), lambda b, h, i: (h, 0, 0)),
                  pl.BlockSpec((1, 2 * ncp, NSA_ROWS), lambda b, h, i: (h, 0, 0)),
                  pl.BlockSpec((nsp, ncp), lambda b, h, i: (0, 0)),
                  pl.BlockSpec((TQ, LANE), lambda b, h, i: (b * nq + i, COL_NG // LANE + h)),
                  pl.BlockSpec((TQ, gw), lambda b, h, i: (b * nq + i, COL_NSL // gw + h))],
        out_specs=pl.BlockSpec((TQ, gw), lambda b, h, i: (b * nq + i, h)),
        out_shape=jax.ShapeDtypeStruct((batch * seq, NSA_W), BF16),
        scratch_shapes=[pltpu.VMEM((1, NSA_ROWS), F32), pltpu.VMEM((1, NSA_ROWS), F32),
                        pltpu.VMEM((NSA_DK, NSA_ROWS), F32), vt_scratch, vt_scratch,
                        pltpu.VMEM((2, TK, NSA_ROWS), F32), pltpu.VMEM((NSA_DK, NSA_ROWS), F32)],
        compiler_params=_cparams(3),
        name="nsa_prompt",
    )(z, zkv, zkv, zkv, zkv, kcvc, vct, tz, cfar, basec, ovt, zb, zb)


SROWS = 8
SEL_PER_STEP = 4


def _stack_group_q(q_row):
    heads = [q_row[:, g * NSA_DK:(g + 1) * NSA_DK] for g in range(NSA_GROUP)]
    return jnp.concatenate(heads + [heads[0]] * (SROWS - NSA_GROUP), axis=0)


def _nsa_sample_cmp_kernel(q_ref, kc_ref, vc_ref, bias_ref, ov_ref, o_ref, idx_ref, *, ns):
    scale = NSA_DK ** -0.5
    qs = _stack_group_q(q_ref[0]).astype(BF16)
    s = _nt(qs, kc_ref[0, 0]) * scale + bias_ref[0]
    m = jnp.max(s, axis=1, keepdims=True)
    e = jnp.exp(s - m)
    inv = jnp.where(m > 0.5 * NEG_INF, 1.0 / jnp.maximum(jnp.sum(e, axis=1, keepdims=True), 1e-30), 0.0)
    p = e * inv
    o_ref[0, 0] = _dot(p.astype(BF16), vc_ref[0, 0])
    psum = jnp.broadcast_to(jnp.sum(p[0:NSA_GROUP], axis=0, keepdims=True), p.shape)
    hi = psum.astype(BF16)
    lo = (psum - hi.astype(F32)).astype(BF16)
    imp = (_dot(hi, ov_ref[...]) + _dot(lo, ov_ref[...]))[0:1]
    nsp = imp.shape[1]
    blk_r = _iota((1, nsp), 1)
    cur = ns - 1
    forced = (blk_r == 0) | (blk_r == cur) | (blk_r == cur - 1)
    imp = jnp.where(blk_r < ns, jnp.where(forced, FORCE_SCORE, imp), 2.0 * NEG_INF)
    imp_c = _column_of(imp)
    i_r = _iota((nsp, nsp), 1)
    j_c = _iota((nsp, nsp), 0)
    ahead = (imp > imp_c) | ((imp == imp_c) & (i_r < j_c))
    rank_c = jnp.sum(jnp.where(ahead, 1.0, 0.0), axis=1, keepdims=True)
    slot = _iota((nsp, LANE), 1).astype(F32)
    picks = jnp.where(rank_c == slot, _iota((nsp, LANE), 0).astype(F32), 0.0)
    idx_ref[0, 0] = jnp.broadcast_to(jnp.sum(picks, axis=0, keepdims=True), (SROWS, LANE)).astype(jnp.int32)


def _nsa_sample_cmp(z3, kcvc, bias_c, ov, ns):
    nb = z3.shape[0]
    ncp = kcvc.shape[2]
    nsp = ov.shape[1]
    gw = NSA_GROUP * NSA_DK

    def cspec(which):
        return pl.BlockSpec((1, 1, ncp, NSA_DK), lambda b, h: (b, which, 0, 0))

    return pl.pallas_call(
        functools.partial(_nsa_sample_cmp_kernel, ns=ns),
        grid=(nb, NSA_KV_HEADS),
        in_specs=[pl.BlockSpec((1, 1, gw), lambda b, h: (b, 0, COL_NQ // gw + h)),
                  cspec(0), cspec(1),
                  pl.BlockSpec((1, SROWS, ncp), lambda b, h: (h, 0, 0)),
                  pl.BlockSpec((ncp, nsp), lambda b, h: (0, 0))],
        out_specs=[pl.BlockSpec((1, 1, SROWS, NSA_DK), lambda b, h: (b, h, 0, 0)),
                   pl.BlockSpec((1, 1, SROWS, LANE), lambda b, h: (b, h, 0, 0))],
        out_shape=[jax.ShapeDtypeStruct((nb, NSA_KV_HEADS, SROWS, NSA_DK), F32),
                   jax.ShapeDtypeStruct((nb, NSA_KV_HEADS, SROWS, LANE), jnp.int32)],
        compiler_params=_cparams(2),
        name="nsa_sample_cmp",
    )(z3, kcvc, kcvc, bias_c, ov)


def _nsa_sample_sel_kernel(pt_ref, idx_ref, q_ref, *refs, ns):
    n_blk = NSA_KV_HEADS * SEL_PER_STEP
    blocks, (new_ref,), biases = refs[:n_blk], refs[n_blk:n_blk + 1], refs[n_blk + 1:2 * n_blk + 1]
    o_ref, m_ref, l_ref, acc_ref = refs[2 * n_blk + 1:]
    b, t = pl.program_id(0), pl.program_id(1)
    gw = NSA_GROUP * NSA_DK
    rows_kv = 2 * NSA_KV_HEADS
    width = SEL_BLOCK * rows_kv

    @pl.when(t == 0)
    def _():
        _flash_init(m_ref, l_ref, acc_ref)

    new_rows = jnp.concatenate([new_ref[0]] * SEL_BLOCK, axis=0)
    xs, scs = [], []
    for h in range(NSA_KV_HEADS):
        qs = _stack_group_q(q_ref[0][:, h * gw:(h + 1) * gw]).astype(BF16)
        row = []
        for u in range(SEL_PER_STEP):
            r = h * SEL_PER_STEP + u
            is_new = idx_ref[b, h, t * SEL_PER_STEP + u] == ns - 1
            x = jnp.where(is_new, new_rows, blocks[r][0].reshape(width, NSA_DK)).astype(BF16)
            xs.append(x)
            row.append(_nt(qs, x) * (NSA_DK ** -0.5) + biases[r][0, 0])
        scs.append(jnp.concatenate(row, axis=1))
    sc = jnp.concatenate(scs, axis=0)
    m_old = m_ref[...]
    m_new = jnp.maximum(m_old, jnp.max(sc, axis=1, keepdims=True))
    alpha = jnp.exp(m_old - m_new)
    p = jnp.exp(sc - m_new)
    l_ref[...] = alpha * l_ref[...] + jnp.sum(p, axis=1, keepdims=True)
    pv = pltpu.roll(p, NSA_KV_HEADS, 1).astype(BF16)
    acc = alpha * acc_ref[...]
    upd = []
    for h in range(NSA_KV_HEADS):
        ph = pv[h * SROWS:(h + 1) * SROWS]
        upd.append(sum(_dot(ph[:, u * width:(u + 1) * width], xs[h * SEL_PER_STEP + u]) for u in range(SEL_PER_STEP)))
    acc_ref[...] = acc + jnp.concatenate(upd, axis=0)
    m_ref[...] = m_new

    @pl.when(t == pl.num_programs(1) - 1)
    def _():
        o_ref[0] = (acc_ref[...] / jnp.maximum(l_ref[...], 1e-30)).reshape(NSA_KV_HEADS, SROWS, NSA_DK)


def _nsa_sample_sel(z3, cache, kv_new, page_table, idx, bias_sel, ns):
    nb, n_pages = page_table.shape
    n_sel = idx.shape[2]
    halves = PAGE_SIZE // SEL_BLOCK
    rows_kv = 2 * NSA_KV_HEADS

    per_step = SEL_PER_STEP
    assert n_sel % per_step == 0

    def blockspec(h, u):
        def index(b, t, pt, ix):
            blk = ix[b, h, t * per_step + u]
            return (pt[b, jnp.minimum(blk // halves, n_pages - 1)], blk % halves, 0, 0)
        return pl.BlockSpec((1, SEL_BLOCK, rows_kv, NSA_DK), index)

    def biasspec(h, u):
        return pl.BlockSpec((1, 1, SROWS, SEL_BLOCK * rows_kv),
                            lambda b, t, pt, ix: (h, ix[b, h, t * per_step + u], 0, 0))

    slots = [(h, u) for h in range(NSA_KV_HEADS) for u in range(per_step)]
    return pl.pallas_call(
        functools.partial(_nsa_sample_sel_kernel, ns=ns),
        grid_spec=pltpu.PrefetchScalarGridSpec(
            num_scalar_prefetch=2,
            grid=(nb, n_sel // per_step),
            in_specs=[pl.BlockSpec((1, 1, NSA_W), lambda b, t, pt, ix: (b, 0, COL_NQ // NSA_W))]
                     + [blockspec(h, u) for h, u in slots]
                     + [pl.BlockSpec((1, rows_kv, NSA_DK), lambda b, t, pt, ix: (b, 0, 0))]
                     + [biasspec(h, u) for h, u in slots],
            out_specs=pl.BlockSpec((1, NSA_KV_HEADS, SROWS, NSA_DK), lambda b, t, pt, ix: (b, 0, 0, 0)),
            scratch_shapes=[pltpu.VMEM((NSA_KV_HEADS * SROWS, 1), F32), pltpu.VMEM((NSA_KV_HEADS * SROWS, 1), F32),
                            pltpu.VMEM((NSA_KV_HEADS * SROWS, NSA_DK), F32)]),
        out_shape=jax.ShapeDtypeStruct((nb, NSA_KV_HEADS, SROWS, NSA_DK), F32),
        compiler_params=_cparams(2),
        name="nsa_sample_sel",
    )(page_table, idx, z3, *([cache] * len(slots)), kv_new, *([bias_sel] * len(slots)))


def _nsa_sample_win_kernel(q_ref, buf_ref, new_ref, bias_ref, bnew_ref, o_ref):
    scale = NSA_DK ** -0.5
    gw = NSA_GROUP * NSA_DK
    rows_kv = 2 * NSA_KV_HEADS
    x = buf_ref[0].reshape(buf_ref.shape[1] * rows_kv, NSA_DK).astype(BF16)
    new = new_ref[0]
    for h in range(NSA_KV_HEADS):
        q = _stack_group_q(q_ref[0][:, h * gw:(h + 1) * gw])
        s_buf = _nt(q.astype(BF16), x) * scale + bias_ref[h]
        s_new = jnp.sum(q * new[h:h + 1], axis=1, keepdims=True) * scale + bnew_ref[h]
        m = jnp.maximum(jnp.max(s_buf, axis=1, keepdims=True), s_new)
        p_buf = jnp.exp(s_buf - m)
        p_new = jnp.exp(s_new - m)
        l = jnp.sum(p_buf, axis=1, keepdims=True) + p_new
        acc = _dot(pltpu.roll(p_buf, NSA_KV_HEADS, 1).astype(BF16), x) + p_new * new[NSA_KV_HEADS + h:NSA_KV_HEADS + h + 1]
        o_ref[0, h] = acc / jnp.maximum(l, 1e-30)


def _nsa_sample_win(z3, win_buf, kv_new, bias_win, bias_new):
    nb, nbuf, rows_kv, _ = win_buf.shape
    return pl.pallas_call(
        _nsa_sample_win_kernel,
        grid=(nb,),
        in_specs=[pl.BlockSpec((1, 1, NSA_W), lambda b: (b, 0, COL_NQ // NSA_W)),
                  pl.BlockSpec((1, nbuf, rows_kv, NSA_DK), lambda b: (b, 0, 0, 0)),
                  pl.BlockSpec((1, rows_kv, NSA_DK), lambda b: (b, 0, 0)),
                  pl.BlockSpec((NSA_KV_HEADS, SROWS, nbuf * rows_kv), lambda b: (0, 0, 0)),
                  pl.BlockSpec((NSA_KV_HEADS, SROWS, 1), lambda b: (0, 0, 0))],
        out_specs=pl.BlockSpec((1, NSA_KV_HEADS, SROWS, NSA_DK), lambda b: (b, 0, 0, 0)),
        out_shape=jax.ShapeDtypeStruct((nb, NSA_KV_HEADS, SROWS, NSA_DK), F32),
        compiler_params=_cparams(1),
        name="nsa_sample_win",
    )(z3, win_buf, kv_new, bias_win, bias_new)


def _nsa_sample_gate_kernel(oc_ref, os_ref, ow_ref, ng_ref, nsl_ref, o_ref):
    gates = _sigmoid(ng_ref[0])
    nsl = nsl_ref[0]
    outs = []
    for g in range(NSA_GROUP):
        o = (gates[:, 3 * g:3 * g + 1] * oc_ref[0, 0, g:g + 1] + gates[:, 3 * g + 1:3 * g + 2] * os_ref[0, 0, g:g + 1]
             + gates[:, 3 * g + 2:3 * g + 3] * ow_ref[0, 0, g:g + 1])
        x = nsl[:, g * NSA_DK:(g + 1) * NSA_DK]
        outs.append(o * (x * _sigmoid(x)))
    o_ref[0] = jnp.concatenate(outs, axis=1).astype(BF16)


def _nsa_sample_gate(o_cmp, o_sel, o_win, z3b):
    nb = z3b.shape[0]
    gw = NSA_GROUP * NSA_DK
    ospec = pl.BlockSpec((1, 1, SROWS, NSA_DK), lambda b, h: (b, h, 0, 0))
    return pl.pallas_call(
        _nsa_sample_gate_kernel,
        grid=(nb, NSA_KV_HEADS),
        in_specs=[ospec, ospec, ospec,
                  pl.BlockSpec((1, 1, LANE), lambda b, h: (b, 0, COL_NG // LANE + h)),
                  pl.BlockSpec((1, 1, gw), lambda b, h: (b, 0, COL_NSL // gw + h))],
        out_specs=pl.BlockSpec((1, 1, gw), lambda b, h: (b, 0, h)),
        out_shape=jax.ShapeDtypeStruct((nb, 1, NSA_W), BF16),
        compiler_params=_cparams(2),
        name="nsa_sample_gate",
    )(o_cmp, o_sel, o_win, z3b, z3b)


def _mem_heads(q, kv):
    outs = []
    for h in range(MEM_HEADS):
        k = kv[:, h * MEM_DH:(h + 1) * MEM_DH].astype(BF16)
        v = kv[:, MEM_W + h * MEM_DH:MEM_W + (h + 1) * MEM_DH].astype(BF16)
        s = _nt(q[:, h * MEM_DH:(h + 1) * MEM_DH].astype(BF16), k) * (MEM_DH ** -0.5)
        e = jnp.exp(s - jnp.max(s, axis=1, keepdims=True))
        p = e / jnp.sum(e, axis=1, keepdims=True)
        outs.append(_dot(p.astype(BF16), v))
    return jnp.concatenate(outs, axis=1)


def _mem_prompt_kernel(q_ref, kv_ref, o_ref):
    o_ref[...] = _mem_heads(q_ref[:, :MEM_W], kv_ref[...]).astype(BF16)


def _mem_prompt(z, mem_kv, batch, seq, tq):
    nq = seq // tq
    n_mem = mem_kv.shape[0] // batch
    return pl.pallas_call(
        _mem_prompt_kernel,
        grid=(batch, nq),
        in_specs=[pl.BlockSpec((tq, MQ_BLOCK), lambda b, i: (b * nq + i, COL_MQ // MQ_BLOCK)),
                  pl.BlockSpec((n_mem, 2 * MEM_W), lambda b, i: (b, 0))],
        out_specs=pl.BlockSpec((tq, MEM_W), lambda b, i: (b * nq + i, 0)),
        out_shape=jax.ShapeDtypeStruct((batch * seq, MEM_W), BF16),
        compiler_params=_cparams(2),
        name="mem_prompt",
    )(z, mem_kv)


def _mem_sample_kernel(q_ref, kv_ref, o_ref):
    n_mem, rows_kv, _ = kv_ref.shape[1:]
    x = kv_ref[0].reshape(n_mem * rows_kv, MEM_DH).astype(BF16)
    row_kind = _iota((SROWS, n_mem * rows_kv), 1) & (rows_kv - 1)
    outs = []
    for h in range(MEM_HEADS):
        q = jnp.broadcast_to(q_ref[0][:, h * MEM_DH:(h + 1) * MEM_DH], (SROWS, MEM_DH)).astype(BF16)
        s = jnp.where(row_kind == h, _nt(q, x) * (MEM_DH ** -0.5), NEG_INF)
        e = jnp.exp(s - jnp.max(s, axis=1, keepdims=True))
        p = e / jnp.sum(e, axis=1, keepdims=True)
        outs.append(_dot(pltpu.roll(p, MEM_HEADS, 1).astype(BF16), x)[0:1])
    o_ref[0] = jnp.concatenate(outs, axis=1).astype(BF16)


def _mem_sample(z3, mem_kv):
    nb, n_mem, rows_kv, _ = mem_kv.shape
    assert rows_kv == 2 * MEM_HEADS
    return pl.pallas_call(
        _mem_sample_kernel,
        grid=(nb,),
        in_specs=[pl.BlockSpec((1, 1, MQ_BLOCK), lambda b: (b, 0, COL_MQ // MQ_BLOCK)),
                  pl.BlockSpec((1, n_mem, rows_kv, MEM_DH), lambda b: (b, 0, 0, 0))],
        out_specs=pl.BlockSpec((1, 1, MEM_W), lambda b: (b, 0, 0)),
        out_shape=jax.ShapeDtypeStruct((nb, 1, MEM_W), BF16),
        compiler_params=_cparams(1),
        name="mem_sample",
    )(z3, mem_kv)


def _merge_kernel(ar_ref, an_ref, am_ref, wr_ref, wn_ref, wm_ref, g0_ref, g1_ref, g2_ref, o_ref):
    merged = (_sigmoid(g0_ref[...]) * _dot(ar_ref[...], wr_ref[...])
              + _sigmoid(g1_ref[...]) * _dot(an_ref[...], wn_ref[...])
              + _sigmoid(g2_ref[...]) * _dot(am_ref[...], wm_ref[...]))
    o_ref[...] = merged.astype(BF16)


def _merge(a_ret, a_nsa, a_mem, w_ret, w_nsa, w_mem, z, tm, tn):
    m = a_ret.shape[0]
    nt = D_MODEL // tn

    def aspec(width):
        return pl.BlockSpec((tm, width), lambda i, j: (i, 0))

    def wspec(width):
        return pl.BlockSpec((width, tn), lambda i, j: (0, j))

    def gspec(branch):
        return pl.BlockSpec((tm, tn), lambda i, j: (i, COL_MG // tn + branch * nt + j))

    return pl.pallas_call(
        _merge_kernel,
        grid=(m // tm, nt),
        in_specs=[aspec(RET_W), aspec(NSA_W), aspec(MEM_W), wspec(RET_W), wspec(NSA_W), wspec(MEM_W),
                  gspec(0), gspec(1), gspec(2)],
        out_specs=pl.BlockSpec((tm, tn), lambda i, j: (i, j)),
        out_shape=jax.ShapeDtypeStruct((m, D_MODEL), BF16),
        compiler_params=_cparams(2),
        name="merge",
    )(a_ret, a_nsa, a_mem, w_ret, w_nsa, w_mem, z, z, z)


def _out_kernel(a_ref, w_ref, x_ref, g_ref, o_ref):
    out = _dot(a_ref[...], w_ref[...])
    y = out * lax.rsqrt(jnp.mean(out * out, axis=-1, keepdims=True) + EPS)
    o_ref[...] = x_ref[...] + y * g_ref[...]


def _out_proj(merged, w_out, x, norm_post, tm):
    m = merged.shape[0]
    return pl.pallas_call(
        _out_kernel,
        grid=(m // tm,),
        in_specs=[pl.BlockSpec((tm, D_MODEL), lambda i: (i, 0)),
                  pl.BlockSpec((D_MODEL, D_MODEL), lambda i: (0, 0)),
                  pl.BlockSpec((tm, D_MODEL), lambda i: (i, 0)),
                  pl.BlockSpec((1, D_MODEL), lambda i: (0, 0))],
        out_specs=pl.BlockSpec((tm, D_MODEL), lambda i: (i, 0)),
        out_shape=jax.ShapeDtypeStruct((m, D_MODEL), F32),
        compiler_params=_cparams(1),
        name="out_proj",
    )(merged, w_out, x, norm_post.reshape(1, D_MODEL))


def _layout_w_ng(w_t):
    per_group = N_BRANCHES * NSA_GROUP
    ng = w_t[PROJ_A:PROJ_A + N_BRANCHES * NSA_HEADS].reshape(NSA_KV_HEADS, per_group, D_MODEL)
    return jnp.pad(ng, ((0, 0), (0, LANE - per_group), (0, 0))).reshape(NG_SLOT, D_MODEL)


def _pick_tile(m, cap):
    t = min(m, cap)
    while m % t:
        t //= 2
    return t


def kernel(x_prompt, x_sample, cache_cmp_kv, cache_sel_kv, cache_win_kv, state_ret, cache_mem_kv, page_table,
           mem_prompt, rel_table, norm_pre, norm_post, norm_mem, w_in, ret_norm, w_ret_up, cmp_pos, w_cmp1,
           w_cmp2, w_nsa_up, w_mem_kv, w_mem_up, w_out):
    batch, seq, _ = x_prompt.shape
    nb = x_sample.shape[0]
    assert x_sample.shape[1] == 1 and norm_pre.shape[0] == 1
    assert seq % TQ == 0 and seq >= WINDOW
    n_pool = cache_cmp_kv.shape[1]
    n_pages = page_table.shape[1]
    past = n_pages * PAGE_SIZE
    n_mem = mem_prompt.shape[1]
    assert n_pages % PAGES_PER_STEP == 0 and cache_win_kv.shape[2] == WINDOW

    w_a = w_in[0].T
    w_ng = _layout_w_ng(w_a)
    kw = CMP_STRIDE * NSA_DK
    w1 = w_cmp1[0].reshape(2, CMP_BLOCK * NSA_DK, NSA_DK).astype(BF16)
    w1a, w1b = w1[:, :kw], w1[:, kw:]
    w1ab = jnp.concatenate([w1a, w1b], axis=2)
    w2 = w_cmp2[0].astype(BF16)
    pos8 = jnp.pad(cmp_pos[0].reshape(2, 1, CMP_BLOCK * NSA_DK), ((0, 0), (0, 7), (0, 0)))
    w_ret = w_ret_up[0].astype(BF16)
    w_nsa = w_nsa_up[0].astype(BF16)
    w_mem = w_mem_up[0].astype(BF16)
    w_o = w_out[0].astype(BF16)

    m_p = batch * seq
    xp = x_prompt.reshape(m_p, D_MODEL)
    hp = _rmsnorm(xp, norm_pre[0], _pick_tile(m_p, 512))
    xs = x_sample.reshape(nb, D_MODEL)
    hs = _rmsnorm(xs, norm_pre[0], nb)
    z, zs_head = _proj(hp, w_a, _pick_tile(m_p, 1024), PROJ_TN, COL_KVC, transposed=True, rider=hs)
    zkv, kvc_rows, kvs_rows, kvw_rows, zs_kv = _proj_kv(hp, w_a, _pick_tile(seq, WINDOW), seq, hs)
    zb, zsb = _proj_tail(hp, w_a, w_ng, _pick_tile(m_p, 1024), hs)

    a_ret, ret_state_p = _retention_prompt(z, ret_norm[0], batch, seq)

    ncp = max(LANE, -(-(seq // CMP_STRIDE) // LANE) * LANE)
    ac = _cmp_stage1_dense(zkv, w1ab, batch, seq)
    kcvc = _cmp_stage2(ac, pos8, w1a, w1b, w2, 1)
    if ncp > kcvc.shape[3]:
        kcvc = jnp.pad(kcvc, ((0, 0), (0, 0), (0, 0), (0, ncp - kcvc.shape[3]), (0, 0)))
    bias_d = _bias_by_dist(rel_table, BIAS_DISTS)
    a_nsa = _nsa_prompt(z, zkv, zb, kcvc, bias_d, batch, seq)

    hm = _rmsnorm(mem_prompt.reshape(batch * n_mem, D_MODEL), norm_mem[0], _pick_tile(batch * n_mem, 512))
    mem_kv_p = _proj(hm, w_mem_kv[0], _pick_tile(batch * n_mem, 512), PROJ_TN)
    a_mem = _mem_prompt(zb, mem_kv_p, batch, seq, _pick_tile(seq, 512))

    merged = _merge(a_ret, a_nsa, a_mem, w_ret, w_nsa, w_mem, zb, _pick_tile(m_p, 1024), 512)
    y_p = _out_proj(merged, w_o, xp, norm_post[0], _pick_tile(m_p, 512)).reshape(batch, seq, D_MODEL)

    kv_shape = (1, batch, seq, 2, NSA_KV_HEADS, NSA_DK)
    new_cmp_p = kvc_rows.reshape(kv_shape)
    new_sel_p = kvs_rows.reshape(kv_shape)
    new_win_p = kvw_rows.reshape(1, batch, WINDOW, 2, NSA_KV_HEADS, NSA_DK)
    new_ret_p = ret_state_p[None]
    new_mem_p = mem_kv_p.reshape(1, batch, n_mem, 2, MEM_HEADS, MEM_DH)

    zs = jnp.concatenate([zs_head, zs_kv], axis=1)
    z3 = zs.reshape(nb, 1, PROJ_A)
    z3b = zsb.reshape(nb, 1, PROJ_B)

    a_ret_s, ret_state_s = _retention_sample(z3, state_ret[0], ret_norm[0], past)

    cache_c = cache_cmp_kv[0].reshape(n_pool, PAGE_SIZE // CMP_STRIDE, CMP_STRIDE, 2 * NSA_KV_HEADS, NSA_DK)
    cache_s = cache_sel_kv[0].reshape(n_pool, PAGE_SIZE, 2 * NSA_KV_HEADS, NSA_DK)
    ac_s = _cmp_stage1_paged(cache_c, page_table, w1ab)
    kcvc_s = _cmp_stage2(ac_s, pos8, w1a, w1b, w2, NSA_KV_HEADS)
    ncs = past // CMP_STRIDE
    kcvc_s = kcvc_s.reshape(nb, 2, ncs * NSA_KV_HEADS, NSA_DK)
    ns_s = past // SEL_BLOCK + 1
    nsp_s = -(-ns_s // LANE) * LANE
    assert past >= WINDOW and past >= REL_MAX_DIST and BIAS_DISTS > WINDOW
    far_s = bias_d[..., REL_MAX_DIST:REL_MAX_DIST + 1]
    hg = (NSA_KV_HEADS, NSA_GROUP)
    n_valid = (past - (CMP_BLOCK - 1)) // CMP_STRIDE + 1
    strided = bias_d[..., (past - (CMP_BLOCK - 1)) % CMP_STRIDE::CMP_STRIDE]
    n_tab = strided.shape[-1]
    assert n_valid >= n_tab and n_tab * CMP_STRIDE > REL_MAX_DIST + CMP_STRIDE and ncs >= n_valid
    bias_cs = jnp.concatenate([jnp.broadcast_to(far_s, hg + (n_valid - n_tab,)), strided[..., ::-1],
                               jnp.full(hg + (ncs - n_valid,), NEG_INF, F32)], axis=-1)
    bias_cs = _pad_group_rows(bias_cs, 1)
    nblk = jnp.arange(ncs)[:, None]
    sblk = jnp.arange(nsp_s)[None, :]
    ov_s = ((nblk >= 4 * sblk - 1) & (nblk <= 4 * sblk + 3)).astype(BF16)
    own_head = jnp.arange(NSA_KV_HEADS)[None, :] == jnp.arange(NSA_KV_HEADS)[:, None]
    bias_cs = jnp.where(own_head[:, None, None, :], bias_cs[..., None], NEG_INF)
    bias_cs = bias_cs.reshape(NSA_KV_HEADS, SROWS, ncs * NSA_KV_HEADS)
    ov_s = jnp.repeat(ov_s, NSA_KV_HEADS, axis=0)
    o_cmp_s, idx_s = _nsa_sample_cmp(z3, kcvc_s, bias_cs, ov_s, ns_s)
    n_sel = min(SEL_TOPK, ns_s)
    idx = idx_s[:, :, 0, :n_sel]

    hg = (NSA_KV_HEADS, NSA_GROUP)
    n_key = ns_s * SEL_BLOCK
    bias_sel = jnp.concatenate([jnp.broadcast_to(far_s, hg + (past + 1 - REL_MAX_DIST,)),
                                bias_d[..., :REL_MAX_DIST][..., ::-1],
                                jnp.full(hg + (n_key - past - 1,), NEG_INF, F32)], axis=-1)
    bias_sel = _pad_group_rows(bias_sel.reshape(hg + (ns_s, SEL_BLOCK)).transpose(0, 2, 1, 3), 2)
    own_k = jnp.arange(2 * NSA_KV_HEADS)[None, :] == jnp.arange(NSA_KV_HEADS)[:, None]
    bias_sel = jnp.where(own_k[:, None, None, None, :], bias_sel[..., None], NEG_INF)
    bias_sel = bias_sel.reshape(NSA_KV_HEADS, ns_s, SROWS, SEL_BLOCK * 2 * NSA_KV_HEADS)
    kvs_new = zs[:, COL_KVS:COL_KVS + 2 * KV_W].reshape(nb, 2 * NSA_KV_HEADS, NSA_DK)
    o_sel_s = _nsa_sample_sel(z3, cache_s, kvs_new, page_table, idx, bias_sel, ns_s)

    win_buf = cache_win_kv[0].reshape(nb, WINDOW, 2 * NSA_KV_HEADS, NSA_DK)
    bias_w = jnp.concatenate([jnp.full(hg + (1,), NEG_INF, F32), bias_d[..., 1:WINDOW][..., ::-1]], axis=-1)
    bias_w = jnp.where(own_k[:, None, None, :], _pad_group_rows(bias_w, 1)[..., None], NEG_INF)
    bias_w = bias_w.reshape(NSA_KV_HEADS, SROWS, WINDOW * 2 * NSA_KV_HEADS)
    kvw_new = zs[:, COL_KVW:COL_KVW + 2 * KV_W].reshape(nb, 2 * NSA_KV_HEADS, NSA_DK)
    o_win_s = _nsa_sample_win(z3, win_buf, kvw_new, bias_w, _pad_group_rows(bias_d[..., 0:1], 1))
    a_nsa_s = _nsa_sample_gate(o_cmp_s, o_sel_s, o_win_s, z3b)

    mem_kv_s = cache_mem_kv[0].reshape(nb, n_mem, 2 * MEM_HEADS, MEM_DH)
    a_mem_s = _mem_sample(z3b, mem_kv_s)

    merged_s = _merge(a_ret_s.reshape(nb, RET_W), a_nsa_s.reshape(nb, NSA_W), a_mem_s.reshape(nb, MEM_W),
                      w_ret, w_nsa, w_mem, zsb, nb, 512)
    y_s = _out_proj(merged_s, w_o, xs, norm_post[0], nb).reshape(nb, 1, D_MODEL)

    kvs_shape = (1, nb, 1, 2, NSA_KV_HEADS, NSA_DK)
    new_cmp_s = zs[:, COL_KVC:COL_KVC + 2 * KV_W].reshape(kvs_shape)
    new_sel_s = zs[:, COL_KVS:COL_KVS + 2 * KV_W].reshape(kvs_shape)
    kvw_s = zs[:, COL_KVW:COL_KVW + 2 * KV_W].reshape(nb, 1, 2, NSA_KV_HEADS, NSA_DK)
    new_win_s = jnp.concatenate([cache_win_kv[0][:, 1:], kvw_s], axis=1)[None]
    new_ret_s = ret_state_s[None]

    return (y_p, y_s, new_cmp_p, new_sel_p, new_win_p, new_ret_p, new_mem_p,
            new_cmp_s, new_sel_s, new_win_s, new_ret_s)
```

```python
import functools
import math

import jax
import jax.numpy as jnp
from jax import lax
from jax.experimental import pallas as pl
from jax.experimental.pallas import tpu as pltpu

F32 = jnp.float32
BF16 = jnp.bfloat16

D_MODEL = 2048
PAGE_SIZE = 128
RET_HEADS = 8
RET_DK = 256
RET_DV = 256
RET_CHUNK = 128
ROPE_BASE = 10000.0
NSA_HEADS = 16
NSA_KV_HEADS = 4
NSA_GROUP = NSA_HEADS // NSA_KV_HEADS
NSA_DK = 128
CMP_BLOCK = 32
CMP_STRIDE = 16
SEL_BLOCK = 64
SEL_TOPK = 16
WINDOW = 512
MEM_HEADS = 4
MEM_DH = 384
REL_BUCKETS = 32
REL_MAX_EXACT = 16
REL_MAX_DIST = 128
N_BRANCHES = 3
EPS = 1e-6
NEG_INF = -1e30
FORCE_SCORE = 1e4

RET_W = RET_HEADS * RET_DV
NSA_W = NSA_HEADS * NSA_DK
KV_W = NSA_KV_HEADS * NSA_DK
MEM_W = MEM_HEADS * MEM_DH

COL_RQ = 0
COL_RK = COL_RQ + RET_HEADS * RET_DK
COL_RV = COL_RK + RET_HEADS * RET_DK
COL_RG = COL_RV + RET_W
COL_NQ = COL_RG + RET_W
COL_KVC = COL_NQ + NSA_W
COL_KVS = COL_KVC + 2 * KV_W
COL_KVW = COL_KVS + 2 * KV_W
PROJ_A = COL_KVW + 2 * KV_W
COL_NSL = 0
COL_MG = COL_NSL + NSA_W
COL_MQ = COL_MG + N_BRANCHES * D_MODEL
COL_NG = COL_MQ + MEM_W
NG_SLOT = NSA_KV_HEADS * 128
PROJ_B = COL_NG + NG_SLOT
MQ_BLOCK = MEM_W + NG_SLOT

LOG2E = math.log2(math.e)
LANE = 128
TQ = 256
TK = 256
NSA_ROWS = NSA_GROUP * TQ
BIAS_DISTS = 1024
FAR_GROUP = 4
RET_STEP_CHUNKS = 8
PROJ_TN = 1024
VMEM_LIMIT = 56 * 1024 * 1024


def _cparams(n_axes):
    return pltpu.CompilerParams(dimension_semantics=("arbitrary",) * n_axes, vmem_limit_bytes=VMEM_LIMIT)


def _nt(a, b):
    return lax.dot_general(a, b, (((1,), (1,)), ((), ())), preferred_element_type=F32)


def _dot(a, b):
    return jnp.dot(a, b, preferred_element_type=F32)


def _sigmoid(x):
    return 1.0 / (1.0 + jnp.exp(-x))


def _iota(shape, dim):
    return lax.broadcasted_iota(jnp.int32, shape, dim)


def _rmsnorm_kernel(x_ref, g_ref, h_ref):
    x = x_ref[...]
    ms = jnp.mean(x * x, axis=-1, keepdims=True)
    h_ref[...] = ((x * lax.rsqrt(ms + EPS)) * g_ref[...]).astype(BF16)


def _rmsnorm(x, g, tm):
    m, k = x.shape
    return pl.pallas_call(
        _rmsnorm_kernel,
        grid=(m // tm,),
        in_specs=[pl.BlockSpec((tm, k), lambda i: (i, 0)), pl.BlockSpec((1, k), lambda i: (0, 0))],
        out_specs=pl.BlockSpec((tm, k), lambda i: (i, 0)),
        out_shape=jax.ShapeDtypeStruct((m, k), BF16),
        compiler_params=_cparams(1),
        name="rmsnorm",
    )(x, g.reshape(1, k))


def _proj_kernel(h_ref, w_ref, *refs, transposed):
    o_ref, wb_ref = refs[-3 if len(refs) == 4 else -2], refs[-1]

    @pl.when(pl.program_id(1) == 0)
    def _():
        if transposed:
            _store_transposed(wb_ref, w_ref, wb_ref.shape[1])
        else:
            wb_ref[...] = w_ref[...].astype(BF16)
        if len(refs) == 4:
            refs[2][...] = _dot(refs[0][...], wb_ref[...])

    o_ref[...] = _dot(h_ref[...], wb_ref[...])


def _rider_specs(rider, k, tn):
    if rider is None:
        return [], [], []
    rows = rider.shape[0]
    return ([pl.BlockSpec((rows, k), lambda j, i: (0, 0))], [pl.BlockSpec((rows, tn), lambda j, i: (0, j))], [rows])


def _proj(h, w, tm, tn, n=None, transposed=False, rider=None):
    m, k = h.shape
    n = w.shape[0 if transposed else 1] if n is None else n
    wspec = pl.BlockSpec((tn, k), lambda j, i: (j, 0)) if transposed else pl.BlockSpec((k, tn), lambda j, i: (0, j))
    r_in, r_out, r_rows = _rider_specs(rider, k, tn)
    out = pl.pallas_call(
        functools.partial(_proj_kernel, transposed=transposed),
        grid=(n // tn, m // tm),
        in_specs=[pl.BlockSpec((tm, k), lambda j, i: (i, 0)), wspec] + r_in,
        out_specs=[pl.BlockSpec((tm, tn), lambda j, i: (i, j))] + r_out,
        out_shape=[jax.ShapeDtypeStruct((m, n), F32)] + [jax.ShapeDtypeStruct((r, n), F32) for r in r_rows],
        scratch_shapes=[pltpu.VMEM((k, tn), BF16)],
        compiler_params=_cparams(2),
        name="proj",
    )(h, w, *([] if rider is None else [rider]))
    return out[0] if rider is None else tuple(out)


def _proj_kv_kernel(h_ref, w_ref, hs_ref, o_ref, oc_ref, os_ref, ow_ref, ors_ref, wb_ref, *, tiles_per_batch):
    j, i = pl.program_id(0), pl.program_id(1)
    n_tiles = pl.num_programs(0)

    @pl.when(i == 0)
    def _():
        _store_transposed(wb_ref, w_ref, wb_ref.shape[1])
        ors_ref[...] = _dot(hs_ref[...], wb_ref[...])

    res = _dot(h_ref[...], wb_ref[...])
    o_ref[...] = res
    tm, tn = res.shape
    rows_kv = tn // NSA_DK

    @pl.when(j == n_tiles - 3)
    def _():
        oc_ref[...] = res.reshape(tm * rows_kv, NSA_DK)

    @pl.when(j == n_tiles - 2)
    def _():
        os_ref[...] = res.reshape(tm * rows_kv, NSA_DK)

    @pl.when((j == n_tiles - 1) & (i % tiles_per_batch == tiles_per_batch - 1))
    def _():
        ow_ref[...] = res[tm - WINDOW:, :].reshape(WINDOW * rows_kv, NSA_DK)


def _proj_kv(h, w_t, tm, seq, rider):
    m, k = h.shape
    tn = 2 * KV_W
    assert COL_KVC % tn == 0 and COL_KVC == PROJ_A - 3 * tn and seq % tm == 0 and tm >= WINDOW
    n_j, n_i = 3, m // tm
    first = COL_KVC // tn
    tiles_per_batch = seq // tm
    rows_kv = tn // NSA_DK

    def kv_rows(tile):
        return lambda j, i: (jnp.where(j < tile, 0, jnp.where(j == tile, i, n_i - 1)), 0)

    r_in, r_out, (r_rows,) = _rider_specs(rider, k, tn)
    return pl.pallas_call(
        functools.partial(_proj_kv_kernel, tiles_per_batch=tiles_per_batch),
        grid=(n_j, n_i),
        in_specs=[pl.BlockSpec((tm, k), lambda j, i: (i, 0)),
                  pl.BlockSpec((tn, k), lambda j, i: (first + j, 0))] + r_in,
        out_specs=[pl.BlockSpec((tm, tn), lambda j, i: (i, j)),
                   pl.BlockSpec((tm * rows_kv, NSA_DK), kv_rows(n_j - 3)),
                   pl.BlockSpec((tm * rows_kv, NSA_DK), kv_rows(n_j - 2)),
                   pl.BlockSpec((WINDOW * rows_kv, NSA_DK),
                                lambda j, i: (jnp.where(j < n_j - 1, 0, i // tiles_per_batch), 0))] + r_out,
        out_shape=[jax.ShapeDtypeStruct((m, n_j * tn), F32),
                   jax.ShapeDtypeStruct((m * rows_kv, NSA_DK), F32),
                   jax.ShapeDtypeStruct((m * rows_kv, NSA_DK), F32),
                   jax.ShapeDtypeStruct((m // seq * WINDOW * rows_kv, NSA_DK), F32),
                   jax.ShapeDtypeStruct((r_rows, n_j * tn), F32)],
        scratch_shapes=[pltpu.VMEM((k, tn), BF16)],
        compiler_params=_cparams(2),
        name="proj_kv",
    )(h, w_t, rider)


TAIL_TN = 1024


def _store_transposed(dst_ref, src_ref, rows, chunk=256):
    for r0 in range(0, rows, chunk):
        dst_ref[:, r0:r0 + chunk] = src_ref[r0:r0 + chunk, :].T.astype(BF16)


def _proj_tail_kernel(h_ref, w_ref, wng_ref, hs_ref, o_ref, ors_ref, wb_ref, *, last_rows):
    tn = wb_ref.shape[1]
    is_last = pl.program_id(0) == pl.num_programs(0) - 1

    @pl.when((pl.program_id(1) == 0) & jnp.logical_not(is_last))
    def _():
        _store_transposed(wb_ref, w_ref, tn)
        ors_ref[...] = _dot(hs_ref[...], wb_ref[...])

    @pl.when((pl.program_id(1) == 0) & is_last)
    def _():
        _store_transposed(wb_ref, w_ref, last_rows)
        wb_ref[:, last_rows:] = wng_ref[...].T.astype(BF16)
        ors_ref[...] = _dot(hs_ref[...], wb_ref[...])

    o_ref[...] = _dot(h_ref[...], wb_ref[...])


def _proj_tail(h, w_t, w_ng, tm, rider):
    m, k = h.shape
    tn = TAIL_TN
    row_ng = PROJ_A
    row_nsl = row_ng + N_BRANCHES * NSA_HEADS
    row_mq = row_nsl + NSA_W
    row_mg = row_mq + MEM_W
    assert row_mg + N_BRANCHES * D_MODEL == w_t.shape[0] and w_ng.shape == (NG_SLOT, k)
    assert COL_NSL == 0 and COL_MG % tn == 0 and COL_MQ % tn == 0 and COL_NG + NG_SLOT == PROJ_B == COL_MQ + 2 * tn
    assert row_mq % 16 == 0 and row_nsl % 16 == 0 and row_mg % 16 == 0
    t_mg, t_mq = COL_MG // tn, COL_MQ // tn

    def w_row(j, i):
        row = jnp.where(j < t_mg, row_nsl + j * tn,
                        jnp.where(j < t_mq, row_mg + (j - t_mg) * tn, row_mq + (j - t_mq) * tn))
        return (pl.multiple_of(row, 16), 0)

    r_in, r_out, (r_rows,) = _rider_specs(rider, k, tn)
    return pl.pallas_call(
        functools.partial(_proj_tail_kernel, last_rows=COL_NG - COL_MQ - tn),
        grid=(PROJ_B // tn, m // tm),
        in_specs=[pl.BlockSpec((tm, k), lambda j, i: (i, 0)),
                  pl.BlockSpec((pl.Element(tn), pl.Element(k)), w_row),
                  pl.BlockSpec((NG_SLOT, k), lambda j, i: (0, 0))] + r_in,
        out_specs=[pl.BlockSpec((tm, tn), lambda j, i: (i, j))] + r_out,
        out_shape=[jax.ShapeDtypeStruct((m, PROJ_B), F32), jax.ShapeDtypeStruct((r_rows, PROJ_B), F32)],
        scratch_shapes=[pltpu.VMEM((k, tn), BF16)],
        compiler_params=_cparams(2),
        name="proj_tail",
    )(h, w_t, w_ng, rider)


def _rope_rows(x, cos, sin):
    half = x.shape[-1] // 2
    x1, x2 = x[:, :half], x[:, half:]
    return jnp.concatenate([x1 * cos - x2 * sin, x1 * sin + x2 * cos], axis=-1)


def _head_norm_gate(o, gnorm, rg):
    oc = o - jnp.mean(o, axis=-1, keepdims=True)
    y = oc * lax.rsqrt(jnp.mean(oc * oc, axis=-1, keepdims=True) + EPS) * gnorm
    return y * (rg * _sigmoid(rg))


def _ret_prompt_kernel(q_ref, k_ref, v_ref, rg_ref, cos_ref, sin_ref, dmat_ref, xi_ref, zeta_ref, gc_ref,
                       gn_ref, a_ref, s_ref):
    @pl.when(pl.program_id(2) == 0)
    def _():
        s_ref[...] = jnp.zeros_like(s_ref)

    c = RET_CHUNK
    for t in range(q_ref.shape[0] // c):
        rows = slice(t * c, (t + 1) * c)
        cos, sin = cos_ref[rows, :], sin_ref[rows, :]
        q = _rope_rows(q_ref[rows, :], cos, sin)
        k = _rope_rows(k_ref[rows, :], cos, sin) * (RET_DK ** -0.5)
        qb, vb = q.astype(BF16), v_ref[rows, :].astype(BF16)
        state = s_ref[0, 0]
        inner = _nt(qb, k.astype(BF16)) * dmat_ref[0]
        o = _dot(inner.astype(BF16), vb) + _dot(qb, state.astype(BF16)) * xi_ref[0]
        kz_t = (k * zeta_ref[0]).T.astype(BF16)
        s_ref[0, 0] = state * gc_ref[0] + _dot(kz_t, vb)
        a_ref[rows, :] = _head_norm_gate(o, gn_ref[...], rg_ref[rows, :]).astype(BF16)


def _decay_tables(chunk):
    log_g = jnp.log1p(-jnp.exp2(-5.0 - jnp.arange(RET_HEADS, dtype=F32)))
    i = jnp.arange(chunk, dtype=F32)
    diff = i[:, None] - i[None, :]
    dmat = jnp.where(diff >= 0, jnp.exp(log_g[:, None, None] * jnp.maximum(diff, 0.0)), 0.0)
    xi = jnp.exp(log_g[:, None] * (i[None, :] + 1.0))[:, :, None]
    zeta = jnp.exp(log_g[:, None] * (chunk - 1.0 - i[None, :]))[:, :, None]
    g_chunk = jnp.exp(log_g * chunk)[:, None, None]
    return dmat, xi, zeta, g_chunk


def _rope_tables(pos):
    half = RET_DK // 2
    freq = jnp.power(ROPE_BASE, -jnp.arange(half, dtype=F32) / half)
    ang = pos.astype(F32)[:, None] * freq[None, :]
    return jnp.cos(ang), jnp.sin(ang)


def _retention_prompt(z, ret_norm, batch, seq):
    c = RET_CHUNK
    rows = _pick_tile(seq, RET_STEP_CHUNKS * c)
    nc = seq // rows
    dmat, xi, zeta, g_chunk = _decay_tables(c)
    cos, sin = _rope_tables(jnp.arange(seq))
    hb = RET_DK

    def zspec(col0):
        return pl.BlockSpec((rows, hb), lambda b, h, t, col0=col0: (b * nc + t, col0 // hb + h))

    per_head = lambda shape: pl.BlockSpec((1,) + shape, lambda b, h, t: (h, 0, 0))
    return pl.pallas_call(
        _ret_prompt_kernel,
        grid=(batch, RET_HEADS, nc),
        in_specs=[zspec(COL_RQ), zspec(COL_RK), zspec(COL_RV), zspec(COL_RG),
                  pl.BlockSpec((rows, hb // 2), lambda b, h, t: (t, 0)),
                  pl.BlockSpec((rows, hb // 2), lambda b, h, t: (t, 0)),
                  per_head((c, c)), per_head((c, 1)), per_head((c, 1)), per_head((1, 1)),
                  pl.BlockSpec((1, hb), lambda b, h, t: (0, h))],
        out_specs=[pl.BlockSpec((rows, hb), lambda b, h, t: (b * nc + t, h)),
                   pl.BlockSpec((1, 1, RET_DK, RET_DV), lambda b, h, t: (b, h, 0, 0))],
        out_shape=[jax.ShapeDtypeStruct((batch * seq, RET_W), BF16),
                   jax.ShapeDtypeStruct((batch, RET_HEADS, RET_DK, RET_DV), F32)],
        compiler_params=_cparams(3),
        name="retention_prompt",
    )(z, z, z, z, cos, sin, dmat, xi, zeta, g_chunk, ret_norm.reshape(1, RET_W))


def _column_of(row):
    n = row.shape[1]
    eye = _iota((n, n), 0) == _iota((n, n), 1)
    return jnp.sum(jnp.where(eye, jnp.broadcast_to(row, (n, n)), 0.0), axis=1, keepdims=True)


def _ret_sample_kernel(q_ref, k_ref, v_ref, rg_ref, cos_ref, sin_ref, gam_ref, gn_ref, s_ref, a_ref, so_ref):
    cos, sin = cos_ref[...], sin_ref[...]
    outs = []
    for h in range(RET_HEADS):
        cols = slice(h * RET_DK, (h + 1) * RET_DK)
        q = _rope_rows(q_ref[0][:, cols], cos, sin)
        k = _rope_rows(k_ref[0][:, cols], cos, sin) * (RET_DK ** -0.5)
        v = v_ref[0][:, cols]
        state = s_ref[0, h]
        gamma = gam_ref[h]
        qk = jnp.sum(q * k, axis=-1, keepdims=True)
        o = qk * v + jnp.sum(_column_of(q) * state, axis=0, keepdims=True) * gamma
        so_ref[0, h] = state * gamma + _column_of(k) * v
        outs.append(_head_norm_gate(o, gn_ref[:, cols], rg_ref[0][:, cols]))
    a_ref[0] = jnp.concatenate(outs, axis=1).astype(BF16)


def _retention_sample(z3, state, ret_norm, pos):
    nb = z3.shape[0]
    cos, sin = _rope_tables(jnp.full((1,), pos))
    gamma = jnp.exp(jnp.log1p(-jnp.exp2(-5.0 - jnp.arange(RET_HEADS, dtype=F32))))[:, None, None]
    hb = RET_DK
    assert RET_HEADS * RET_DK == RET_W

    def zspec(col0):
        return pl.BlockSpec((1, 1, RET_W), lambda b, col0=col0: (b, 0, col0 // RET_W))

    st_spec = pl.BlockSpec((1, RET_HEADS, RET_DK, RET_DV), lambda b: (b, 0, 0, 0))
    return pl.pallas_call(
        _ret_sample_kernel,
        grid=(nb,),
        in_specs=[zspec(COL_RQ), zspec(COL_RK), zspec(COL_RV), zspec(COL_RG),
                  pl.BlockSpec((1, hb // 2), lambda b: (0, 0)),
                  pl.BlockSpec((1, hb // 2), lambda b: (0, 0)),
                  pl.BlockSpec((RET_HEADS, 1, 1), lambda b: (0, 0, 0)),
                  pl.BlockSpec((1, RET_W), lambda b: (0, 0)),
                  st_spec],
        out_specs=[pl.BlockSpec((1, 1, RET_W), lambda b: (b, 0, 0)), st_spec],
        out_shape=[jax.ShapeDtypeStruct((nb, 1, RET_W), BF16),
                   jax.ShapeDtypeStruct(state.shape, F32)],
        compiler_params=_cparams(1),
        name="retention_sample",
    )(z3, z3, z3, z3, cos, sin, gamma, ret_norm.reshape(1, RET_W), state)


def _half_rows(ref_slice_fn, n_half):
    return jnp.concatenate([ref_slice_fn(p) for p in range(CMP_STRIDE)], axis=1)


def _cmp_stage1_dense_kernel(x_ref, w_ref, o_ref):
    nh = o_ref.shape[3]
    x = _half_rows(lambda p: x_ref[pl.ds(p, nh, stride=CMP_STRIDE), :], nh).astype(BF16)
    o_ref[0, 0, 0] = _dot(x, w_ref[0])


def _cmp_stage1_dense(z, w1ab, batch, seq):
    nh = seq // CMP_STRIDE
    return pl.pallas_call(
        _cmp_stage1_dense_kernel,
        grid=(batch, 2, NSA_KV_HEADS),
        in_specs=[pl.BlockSpec((seq, NSA_DK), lambda b, kv, h: (b, kv * NSA_KV_HEADS + h)),
                  pl.BlockSpec((1, CMP_STRIDE * NSA_DK, 2 * NSA_DK), lambda b, kv, h: (kv, 0, 0))],
        out_specs=pl.BlockSpec((1, 1, 1, nh, 2 * NSA_DK), lambda b, kv, h: (b, kv, h, 0, 0)),
        out_shape=jax.ShapeDtypeStruct((batch, 2, NSA_KV_HEADS, nh, 2 * NSA_DK), F32),
        compiler_params=_cparams(3),
        name="cmp_stage1_dense",
    )(z, w1ab)


PAGES_PER_STEP = 16


def _cmp_stage1_paged_kernel(pt_ref, *refs):
    pages, (w_ref, o_ref) = refs[:PAGES_PER_STEP], refs[PAGES_PER_STEP:]
    hp = PAGE_SIZE // CMP_STRIDE
    top = _iota((2 * NSA_KV_HEADS, NSA_DK), 0) < NSA_KV_HEADS
    cols = [[], []]
    for p in range(CMP_STRIDE):
        tiles = [[], []]
        for pg in pages:
            xp = pg[0, :, p]
            for n in range(0, hp, 2):
                a, b = xp[n], xp[n + 1]
                tiles[0].append(jnp.where(top, a, pltpu.roll(b, NSA_KV_HEADS, 0)))
                tiles[1].append(jnp.where(top, pltpu.roll(a, NSA_KV_HEADS, 0), b))
        for kv in range(2):
            cols[kv].append(jnp.concatenate(tiles[kv], axis=0))
    for kv in range(2):
        x = jnp.concatenate(cols[kv], axis=1).astype(BF16)
        o_ref[0, kv, 0] = _dot(x, w_ref[kv])


def _cmp_stage1_paged(cache, page_table, w1ab):
    nb, n_pages = page_table.shape
    hp = PAGE_SIZE // CMP_STRIDE
    steps = n_pages // PAGES_PER_STEP
    rows = PAGES_PER_STEP * hp * NSA_KV_HEADS

    def page_spec(j):
        return pl.BlockSpec((1, hp, CMP_STRIDE, 2 * NSA_KV_HEADS, NSA_DK),
                            lambda b, s, pt, j=j: (pt[b, s * PAGES_PER_STEP + j], 0, 0, 0, 0))

    wspec = pl.BlockSpec((2, CMP_STRIDE * NSA_DK, 2 * NSA_DK), lambda b, s, pt: (0, 0, 0))
    return pl.pallas_call(
        _cmp_stage1_paged_kernel,
        grid_spec=pltpu.PrefetchScalarGridSpec(
            num_scalar_prefetch=1,
            grid=(nb, steps),
            in_specs=[page_spec(j) for j in range(PAGES_PER_STEP)] + [wspec],
            out_specs=pl.BlockSpec((1, 2, 1, rows, 2 * NSA_DK), lambda b, s, pt: (b, 0, 0, s, 0))),
        out_shape=jax.ShapeDtypeStruct((nb, 2, 1, steps * rows, 2 * NSA_DK), F32),
        compiler_params=_cparams(2),
        name="cmp_stage1_paged",
    )(page_table, *([cache] * PAGES_PER_STEP), w1ab)


def _cmp_stage2_kernel(ac_ref, pos_ref, w1a_ref, w1b_ref, w2_ref, o_ref, *, shift):
    ac = ac_ref[0, 0, 0]
    nh = ac.shape[0]
    pos = pos_ref[0].astype(BF16)
    kw = CMP_STRIDE * NSA_DK
    pe = _dot(pos[:, :kw], w1a_ref[0]) + _dot(pos[:, kw:], w1b_ref[0])
    pre = ac[:, :NSA_DK] + pltpu.roll(ac[:, NSA_DK:], nh - shift, 0) + pe[0:1]
    gelu = 0.5 * pre * (1.0 + jnp.tanh(math.sqrt(2.0 / math.pi) * (pre + 0.044715 * (pre * pre * pre))))
    o_ref[0, 0, 0] = _dot(gelu.astype(BF16), w2_ref[0]).astype(BF16)


def _cmp_stage2(ac, pos8, w1a, w1b, w2, shift):
    nb, _, groups, nh, _ = ac.shape
    kw = CMP_STRIDE * NSA_DK
    return pl.pallas_call(
        functools.partial(_cmp_stage2_kernel, shift=shift),
        grid=(nb, 2, groups),
        in_specs=[pl.BlockSpec((1, 1, 1, nh, 2 * NSA_DK), lambda b, kv, h: (b, kv, h, 0, 0)),
                  pl.BlockSpec((1, 8, 2 * kw), lambda b, kv, h: (kv, 0, 0)),
                  pl.BlockSpec((1, kw, NSA_DK), lambda b, kv, h: (kv, 0, 0)),
                  pl.BlockSpec((1, kw, NSA_DK), lambda b, kv, h: (kv, 0, 0)),
                  pl.BlockSpec((1, NSA_DK, NSA_DK), lambda b, kv, h: (kv, 0, 0))],
        out_specs=pl.BlockSpec((1, 1, 1, nh, NSA_DK), lambda b, kv, h: (b, kv, h, 0, 0)),
        out_shape=jax.ShapeDtypeStruct((nb, 2, groups, nh, NSA_DK), BF16),
        compiler_params=_cparams(3),
        name="cmp_stage2",
    )(ac, pos8, w1a, w1b, w2)


def _rel_bucket(dist):
    n = jnp.maximum(dist, 0)
    nf = jnp.maximum(n, 1).astype(F32)
    scale = (REL_BUCKETS - REL_MAX_EXACT) / math.log(REL_MAX_DIST / REL_MAX_EXACT)
    large = REL_MAX_EXACT + (jnp.log(nf / REL_MAX_EXACT) * scale).astype(jnp.int32)
    large = jnp.minimum(large, REL_BUCKETS - 1)
    return jnp.where(n < REL_MAX_EXACT, n, large)


def _bias_by_dist(rel_table, n):
    tab = rel_table[_rel_bucket(jnp.arange(n))]
    return tab.T.reshape(NSA_KV_HEADS, NSA_GROUP, n)


def _pad_group_rows(t, axis):
    first = lax.slice_in_dim(t, 0, 1, axis=axis)
    return jnp.concatenate([t] + [first] * (SROWS - NSA_GROUP), axis=axis)


def _flash_init(m_ref, l_ref, acc_ref):
    m_ref[...] = jnp.full(m_ref.shape, NEG_INF, F32)
    l_ref[...] = jnp.zeros(l_ref.shape, F32)
    acc_ref[...] = jnp.zeros(acc_ref.shape, F32)


def _flash_step(s, v, m_ref, l_ref, acc_ref):
    m_old = m_ref[...]
    m_new = jnp.maximum(m_old, jnp.max(s, axis=1, keepdims=True))
    alpha = jnp.exp(m_old - m_new)
    p = jnp.exp(s - m_new)
    l_ref[...] = alpha * l_ref[...] + jnp.sum(p, axis=1, keepdims=True)
    acc_ref[...] = alpha * acc_ref[...] + _dot(p.astype(BF16), v)
    m_ref[...] = m_new


def _flash_result(l_ref, acc_ref):
    return acc_ref[...] / jnp.maximum(l_ref[...], 1e-30)


def _select_blocks(imp_t, q0, ns):
    shape = imp_t.shape
    blk = _iota(shape, 0)
    qpos = q0 + _iota(shape, 1)
    cur = qpos >> 6
    valid = blk * SEL_BLOCK <= qpos
    forced = (blk == 0) | (blk == cur) | (blk == cur - 1)
    imp_t = jnp.where(valid, jnp.where(forced, FORCE_SCORE, imp_t), NEG_INF)
    rank = jnp.zeros(shape, F32)
    for other in range(ns):
        row = imp_t[other:other + 1, :]
        ahead = (row > imp_t) | ((row == imp_t) & (blk > other))
        rank = rank + jnp.where(ahead, 1.0, 0.0)
    return jnp.where((rank < SEL_TOPK) & valid, 1.0, 0.0)


def _nsa_prompt_kernel(q_ref, ks_ref, vs_ref, kw_ref, vw_ref, kc_ref, vct_ref, wd_ref, cfar_ref, basec_ref,
                       ovt_ref, ng_ref, nsl_ref, o_ref, m_ref, l_ref, acc_ref, vst_ref, vwt_ref, tz_ref, ow_ref, *, ns):
    i = pl.program_id(2)
    q0 = i * TQ
    qall = q_ref[...] * ((NSA_DK ** -0.5) * LOG2E)
    qt = jnp.concatenate([qall[:, g * NSA_DK:(g + 1) * NSA_DK].T for g in range(NSA_GROUP)], axis=1).astype(BF16)
    c_loc = _iota((TK, NSA_ROWS), 0)
    r_loc = _iota((TK, NSA_ROWS), 1) & (TQ - 1)

    @pl.when(i == 0)
    def _():
        for kt in range(vst_ref.shape[0]):
            vst_ref[kt] = vs_ref[kt * TK:(kt + 1) * TK, :].T.astype(BF16)
            vwt_ref[kt] = vw_ref[kt * TK:(kt + 1) * TK, :].T.astype(BF16)
        below = _iota((TK, TQ), 1) >= _iota((TK, TQ), 0)
        for g in range(NSA_GROUP):
            lo = pltpu.roll(jnp.broadcast_to(wd_ref[0, 2 * g:2 * g + 1, :], (TK, TQ)), 0, 1, stride=1, stride_axis=0)
            hi = pltpu.roll(jnp.broadcast_to(wd_ref[0, 2 * g + 1:2 * g + 2, :], (TK, TQ)), 0, 1, stride=1, stride_axis=0)
            tz_ref[0, :, g * TQ:(g + 1) * TQ] = lo
            tz_ref[1, :, g * TQ:(g + 1) * TQ] = jnp.where(below, hi, lo)

    ncp = kc_ref.shape[3]
    shift = (TQ // CMP_STRIDE) * i
    bias_c = basec_ref[0, pl.ds(pl.multiple_of(ncp - shift, TQ // CMP_STRIDE), ncp), :]
    s = _dot(kc_ref[0, 0, 0], qt) + bias_c
    m = jnp.max(s, axis=0, keepdims=True)
    e = jnp.exp2(s - m)
    inv = jnp.where(m > 0.5 * NEG_INF, 1.0 / jnp.maximum(jnp.sum(e, axis=0, keepdims=True), 1e-30), 0.0)
    p = e * inv
    o_cmp = _dot(vct_ref[0, 0], p.astype(BF16))

    psum = p[:, 0:TQ] + p[:, TQ:2 * TQ] + p[:, 2 * TQ:3 * TQ] + p[:, 3 * TQ:4 * TQ]
    hi = psum.astype(BF16)
    lo = (psum - hi.astype(F32)).astype(BF16)
    ovt = ovt_ref[...]
    imp_t = _dot(ovt, hi) + _dot(ovt, lo)
    ns8 = -(-ns // 8) * 8
    sel_t = _select_blocks(imp_t[:ns8], q0, ns)
    assert ns8 + 2 <= LANE
    sel_neg = jnp.concatenate([jnp.where(sel_t > 0.5, 0.0, NEG_INF)] * NSA_GROUP, axis=1)
    cfar = cfar_ref[0]
    cfar_hi = cfar.astype(BF16).astype(F32)
    sub8 = _iota((8, NSA_ROWS), 0)
    cfar8 = jnp.where(sub8 == 0, cfar_hi, jnp.where(sub8 == 1, cfar - cfar_hi, 0.0))
    pad_rows = jnp.zeros((LANE - ns8 - 8, NSA_ROWS), F32)
    qx_far = jnp.concatenate([qt, jnp.concatenate([sel_neg, cfar8, pad_rows], axis=0).astype(BF16)], axis=0)
    qx_near = jnp.concatenate([qt, jnp.concatenate([sel_neg, jnp.zeros_like(cfar8), pad_rows], axis=0).astype(BF16)],
                              axis=0)
    lane_k = _iota((TK, LANE), 1)
    ones_k = (lane_k == ns8) | (lane_k == ns8 + 1)

    def flash_step(scs, vts):
        m_old = m_ref[...]
        m_new = m_old
        for sc in scs:
            m_new = jnp.maximum(m_new, jnp.max(sc, axis=0, keepdims=True))
        alpha = jnp.exp2(m_old - m_new)
        l_new = alpha * l_ref[...]
        acc = alpha * acc_ref[...]
        for sc, vt in zip(scs, vts):
            pt = jnp.exp2(sc - m_new)
            l_new = l_new + jnp.sum(pt, axis=0, keepdims=True)
            acc = acc + _dot(vt, pt.astype(BF16))
        l_ref[...] = l_new
        acc_ref[...] = acc
        m_ref[...] = m_new

    def sel_scores(kt, bias, causal):
        k = ks_ref[pl.ds(pl.multiple_of(kt * TK, TK), TK), :].astype(BF16)
        blk_of_key = (TK // SEL_BLOCK) * kt + (_iota((TK, LANE), 0) >> 6)
        extra = jnp.where((lane_k == blk_of_key) | ones_k, 1.0, 0.0).astype(BF16)
        kx = jnp.concatenate([k, extra], axis=1)
        sc = _dot(kx, qx_far) if bias is None else _dot(kx, qx_near) + bias
        if causal:
            sc = jnp.where(c_loc <= r_loc, sc, NEG_INF)
        return sc

    def softmax_tiles(scs, vts):
        m_new = jnp.max(scs[0], axis=0, keepdims=True)
        for sc in scs[1:]:
            m_new = jnp.maximum(m_new, jnp.max(sc, axis=0, keepdims=True))
        l_new, acc = None, None
        for sc, vt in zip(scs, vts):
            pt = jnp.exp2(sc - m_new)
            l_t, acc_t = jnp.sum(pt, axis=0, keepdims=True), _dot(vt, pt.astype(BF16))
            l_new, acc = (l_t, acc_t) if l_new is None else (l_new + l_t, acc + acc_t)
        return m_new, l_new, acc

    kt_prev = jnp.maximum(i - 1, 0)
    prev_scores = jnp.where(i >= 1, sel_scores(kt_prev, tz_ref[1], False), NEG_INF)
    m_ref[...], l_ref[...], acc_ref[...] = softmax_tiles([sel_scores(i, tz_ref[0], True), prev_scores],
                                                         [vst_ref[i], vst_ref[kt_prev]])

    def win_scores(off):
        kt = jnp.maximum(i - off, 0)
        k = kw_ref[pl.ds(pl.multiple_of(kt * TK, TK), TK), :].astype(BF16)
        if off < 2:
            sc = _dot(k, qt) + tz_ref[off]
        else:
            sc = _dot(jnp.concatenate([k, jnp.where(ones_k, 1.0, 0.0).astype(BF16)], axis=1), qx_far)
        if off == 0:
            return jnp.where(c_loc <= r_loc, sc, NEG_INF)
        keep = (c_loc > r_loc) & (i >= off) if off * TK == WINDOW else (i >= off)
        return jnp.where(keep, sc, NEG_INF)

    n_win = WINDOW // TK + 1
    _, l_win, acc_win = softmax_tiles([win_scores(off) for off in range(n_win)],
                                      [vwt_ref[jnp.maximum(i - off, 0)] for off in range(n_win)])
    ow_ref[...] = acc_win / jnp.maximum(l_win, 1e-30)

    def far_tiles(kt_first, count):
        kts = [kt_first - t for t in range(count)]
        flash_step([sel_scores(kt, None, False) for kt in kts], [vst_ref[kt] for kt in kts])

    def far_group(j, carry):
        far_tiles(i - 2 - FAR_GROUP * j, FAR_GROUP)
        return carry

    n_far = jnp.maximum(i - 1, 0)
    lax.fori_loop(0, n_far // FAR_GROUP, far_group, 0)
    for rest in range(1, FAR_GROUP):
        pl.when(n_far % FAR_GROUP == rest)(functools.partial(far_tiles, rest - 1, rest))

    o_sel = _flash_result(l_ref, acc_ref)
    o_win = ow_ref[...]

    gates_t = _sigmoid(ng_ref[...]).T
    nsl = nsl_ref[...]
    outs = []
    for g in range(NSA_GROUP):
        cols = slice(g * TQ, (g + 1) * TQ)
        o_t = (gates_t[3 * g:3 * g + 1] * o_cmp[:, cols] + gates_t[3 * g + 1:3 * g + 2] * o_sel[:, cols]
               + gates_t[3 * g + 2:3 * g + 3] * o_win[:, cols])
        x = nsl[:, g * NSA_DK:(g + 1) * NSA_DK]
        outs.append(o_t.T * (x * _sigmoid(x)))
    o_ref[...] = jnp.concatenate(outs, axis=1).astype(BF16)


def _lanes_by_head(t):
    hk, g, rows, tq = t.shape
    return t.transpose(0, 2, 1, 3).reshape(hk, rows, g * tq)


def _nsa_prompt(z, zkv, zb, kcvc, bias_d, batch, seq):
    nq = seq // TQ
    ns = seq // SEL_BLOCK
    ncp = kcvc.shape[3]
    nsp = LANE
    wn = TQ // CMP_STRIDE
    assert TQ == TK and ns <= nsp and ncp >= seq // CMP_STRIDE and ncp % LANE == 0 and ncp > wn
    gw = NSA_GROUP * NSA_DK
    n_dist = bias_d.shape[-1]
    assert n_dist >= 2 * TK + TQ
    bias_d = bias_d * LOG2E

    tz = bias_d[..., :2 * TQ].reshape(NSA_KV_HEADS, 2 * NSA_GROUP, TQ)
    far = jnp.broadcast_to(bias_d[..., REL_MAX_DIST][:, :, None, None], (NSA_KV_HEADS, NSA_GROUP, 1, TQ))
    cfar = _lanes_by_head(far)
    half = n_dist // 2
    start = CMP_STRIDE * wn - (CMP_BLOCK - 1)
    assert start + TQ <= half and 2 * wn * CMP_STRIDE - start <= half
    w_ext = jnp.concatenate([bias_d[..., :half], jnp.full(bias_d.shape[:-1] + (n_dist - half,), NEG_INF, F32)], -1)
    near = jnp.tile(w_ext, (1, 1, 2 * wn + 1))[..., :2 * wn * (n_dist - CMP_STRIDE)]
    near = near.reshape(NSA_KV_HEADS, NSA_GROUP, 2 * wn, n_dist - CMP_STRIDE)[..., start:start + TQ]
    basec = jnp.concatenate([jnp.broadcast_to(far, (NSA_KV_HEADS, NSA_GROUP, ncp - wn, TQ)), near,
                             jnp.full((NSA_KV_HEADS, NSA_GROUP, ncp - wn, TQ), NEG_INF, F32)], axis=2)
    basec = _lanes_by_head(basec)
    sblk = jnp.arange(nsp)[:, None]
    nblk = jnp.arange(ncp)[None, :]
    ovt = ((nblk >= 4 * sblk - 1) & (nblk <= 4 * sblk + 3)).astype(BF16)

    vct = kcvc[:, 1].transpose(0, 1, 3, 2)

    def kvspec(col0, which):
        return pl.BlockSpec((seq, NSA_DK),
                            lambda b, h, i: (b, (col0 - COL_KVC) // NSA_DK + which * NSA_KV_HEADS + h))

    vt_scratch = pltpu.VMEM((seq // TK, NSA_DK, TK), BF16)
    return pl.pallas_call(
        functools.partial(_nsa_prompt_kernel, ns=ns),
        grid=(batch, NSA_KV_HEADS, nq),
        in_specs=[pl.BlockSpec((TQ, gw), lambda b, h, i: (b * nq + i, COL_NQ // gw + h)),
                  kvspec(COL_KVS, 0), kvspec(COL_KVS, 1), kvspec(COL_KVW, 0), kvspec(COL_KVW, 1),
                  pl.BlockSpec((1, 1, 1, ncp, NSA_DK), lambda b, h, i: (b, 0, h, 0, 0)),
                  pl.BlockSpec((1, 1, NSA_DK, ncp), lambda b, h, i: (b, h, 0, 0)),
                  pl.BlockSpec((1, 2 * NSA_GROUP, TQ), lambda b, h, i: (h, 0, 0)),
                  pl.BlockSpec((1, 1, NSA_ROWS), lambda b, h, i: (h, 0, 0)),
                  pl.BlockSpec((1, 2 * ncp, NSA_ROWS), lambda b, h, i: (h, 0, 0)),
                  pl.BlockSpec((nsp, ncp), lambda b, h, i: (0, 0)),
                  pl.BlockSpec((TQ, LANE), lambda b, h, i: (b * nq + i, COL_NG // LANE + h)),
                  pl.BlockSpec((TQ, gw), lambda b, h, i: (b * nq + i, COL_NSL // gw + h))],
        out_specs=pl.BlockSpec((TQ, gw), lambda b, h, i: (b * nq + i, h)),
        out_shape=jax.ShapeDtypeStruct((batch * seq, NSA_W), BF16),
        scratch_shapes=[pltpu.VMEM((1, NSA_ROWS), F32), pltpu.VMEM((1, NSA_ROWS), F32),
                        pltpu.VMEM((NSA_DK, NSA_ROWS), F32), vt_scratch, vt_scratch,
                        pltpu.VMEM((2, TK, NSA_ROWS), F32), pltpu.VMEM((NSA_DK, NSA_ROWS), F32)],
        compiler_params=_cparams(3),
        name="nsa_prompt",
    )(z, zkv, zkv, zkv, zkv, kcvc, vct, tz, cfar, basec, ovt, zb, zb)


SROWS = 8
SEL_PER_STEP = 4


def _stack_group_q(q_row):
    heads = [q_row[:, g * NSA_DK:(g + 1) * NSA_DK] for g in range(NSA_GROUP)]
    return jnp.concatenate(heads + [heads[0]] * (SROWS - NSA_GROUP), axis=0)


def _nsa_sample_cmp_kernel(q_ref, kc_ref, vc_ref, bias_ref, ov_ref, o_ref, idx_ref, *, ns):
    scale = NSA_DK ** -0.5
    qs = _stack_group_q(q_ref[0]).astype(BF16)
    s = _nt(qs, kc_ref[0, 0]) * scale + bias_ref[0]
    m = jnp.max(s, axis=1, keepdims=True)
    e = jnp.exp(s - m)
    inv = jnp.where(m > 0.5 * NEG_INF, 1.0 / jnp.maximum(jnp.sum(e, axis=1, keepdims=True), 1e-30), 0.0)
    p = e * inv
    o_ref[0, 0] = _dot(p.astype(BF16), vc_ref[0, 0])
    psum = jnp.broadcast_to(jnp.sum(p[0:NSA_GROUP], axis=0, keepdims=True), p.shape)
    hi = psum.astype(BF16)
    lo = (psum - hi.astype(F32)).astype(BF16)
    imp = (_dot(hi, ov_ref[...]) + _dot(lo, ov_ref[...]))[0:1]
    nsp = imp.shape[1]
    blk_r = _iota((1, nsp), 1)
    cur = ns - 1
    forced = (blk_r == 0) | (blk_r == cur) | (blk_r == cur - 1)
    imp = jnp.where(blk_r < ns, jnp.where(forced, FORCE_SCORE, imp), 2.0 * NEG_INF)
    imp_c = _column_of(imp)
    i_r = _iota((nsp, nsp), 1)
    j_c = _iota((nsp, nsp), 0)
    ahead = (imp > imp_c) | ((imp == imp_c) & (i_r < j_c))
    rank_c = jnp.sum(jnp.where(ahead, 1.0, 0.0), axis=1, keepdims=True)
    slot = _iota((nsp, LANE), 1).astype(F32)
    picks = jnp.where(rank_c == slot, _iota((nsp, LANE), 0).astype(F32), 0.0)
    idx_ref[0, 0] = jnp.broadcast_to(jnp.sum(picks, axis=0, keepdims=True), (SROWS, LANE)).astype(jnp.int32)


def _nsa_sample_cmp(z3, kcvc, bias_c, ov, ns):
    nb = z3.shape[0]
    ncp = kcvc.shape[2]
    nsp = ov.shape[1]
    gw = NSA_GROUP * NSA_DK

    def cspec(which):
        return pl.BlockSpec((1, 1, ncp, NSA_DK), lambda b, h: (b, which, 0, h))

    return pl.pallas_call(
        functools.partial(_nsa_sample_cmp_kernel, ns=ns),
        grid=(nb, NSA_KV_HEADS),
        in_specs=[pl.BlockSpec((1, 1, gw), lambda b, h: (b, 0, COL_NQ // gw + h)),
                  cspec(0), cspec(1),
                  pl.BlockSpec((1, SROWS, ncp), lambda b, h: (h, 0, 0)),
                  pl.BlockSpec((ncp, nsp), lambda b, h: (0, 0))],
        out_specs=[pl.BlockSpec((1, 1, SROWS, NSA_DK), lambda b, h: (b, h, 0, 0)),
                   pl.BlockSpec((1, 1, SROWS, LANE), lambda b, h: (b, h, 0, 0))],
        out_shape=[jax.ShapeDtypeStruct((nb, NSA_KV_HEADS, SROWS, NSA_DK), F32),
                   jax.ShapeDtypeStruct((nb, NSA_KV_HEADS, SROWS, LANE), jnp.int32)],
        compiler_params=_cparams(2),
        name="nsa_sample_cmp",
    )(z3, kcvc, kcvc, bias_c, ov)


def _nsa_sample_sel_kernel(pt_ref, idx_ref, q_ref, *refs, ns):
    n_blk = NSA_KV_HEADS * SEL_PER_STEP
    blocks, (new_ref,), biases = refs[:n_blk], refs[n_blk:n_blk + 1], refs[n_blk + 1:2 * n_blk + 1]
    oc_ref, ow_ref, ng_ref, nsl_ref, o_ref, m_ref, l_ref, acc_ref = refs[2 * n_blk + 1:]
    b, t = pl.program_id(0), pl.program_id(1)
    gw = NSA_GROUP * NSA_DK
    rows_kv = 2 * NSA_KV_HEADS
    width = SEL_BLOCK * rows_kv

    @pl.when(t == 0)
    def _():
        _flash_init(m_ref, l_ref, acc_ref)

    new_rows = jnp.concatenate([new_ref[0]] * SEL_BLOCK, axis=0)
    xs, scs = [], []
    for h in range(NSA_KV_HEADS):
        qs = _stack_group_q(q_ref[0][:, h * gw:(h + 1) * gw]).astype(BF16)
        row = []
        for u in range(SEL_PER_STEP):
            r = h * SEL_PER_STEP + u
            is_new = idx_ref[b, h, t * SEL_PER_STEP + u] == ns - 1
            x = jnp.where(is_new, new_rows, blocks[r][0].reshape(width, NSA_DK)).astype(BF16)
            xs.append(x)
            row.append(_nt(qs, x) * (NSA_DK ** -0.5) + biases[r][0, 0])
        scs.append(jnp.concatenate(row, axis=1))
    sc = jnp.concatenate(scs, axis=0)
    m_old = m_ref[...]
    m_new = jnp.maximum(m_old, jnp.max(sc, axis=1, keepdims=True))
    alpha = jnp.exp(m_old - m_new)
    p = jnp.exp(sc - m_new)
    l_ref[...] = alpha * l_ref[...] + jnp.sum(p, axis=1, keepdims=True)
    pv = pltpu.roll(p, NSA_KV_HEADS, 1).astype(BF16)
    acc = alpha * acc_ref[...]
    upd = []
    for h in range(NSA_KV_HEADS):
        ph = pv[h * SROWS:(h + 1) * SROWS]
        upd.append(sum(_dot(ph[:, u * width:(u + 1) * width], xs[h * SEL_PER_STEP + u]) for u in range(SEL_PER_STEP)))
    acc_ref[...] = acc + jnp.concatenate(upd, axis=0)
    m_ref[...] = m_new

    @pl.when(t == pl.num_programs(1) - 1)
    def _():
        o_sel = acc_ref[...] / jnp.maximum(l_ref[...], 1e-30)
        gates = _sigmoid(ng_ref[0])
        nsl = nsl_ref[0]
        outs = []
        for h in range(NSA_KV_HEADS):
            for g in range(NSA_GROUP):
                c = h * LANE + N_BRANCHES * g
                r = h * SROWS + g
                o = (gates[:, c:c + 1] * oc_ref[0, h, g:g + 1] + gates[:, c + 1:c + 2] * o_sel[r:r + 1]
                     + gates[:, c + 2:c + 3] * ow_ref[0, h, g:g + 1])
                x = nsl[:, (h * NSA_GROUP + g) * NSA_DK:(h * NSA_GROUP + g + 1) * NSA_DK]
                outs.append(o * (x * _sigmoid(x)))
        o_ref[0] = jnp.concatenate(outs, axis=1).astype(BF16)


def _nsa_sample_sel(z3, cache, kv_new, page_table, idx, bias_sel, ns, o_cmp, o_win, z3b):
    nb, n_pages = page_table.shape
    branch_spec = pl.BlockSpec((1, NSA_KV_HEADS, SROWS, NSA_DK), lambda b, t, pt, ix: (b, 0, 0, 0))
    n_sel = idx.shape[2]
    halves = PAGE_SIZE // SEL_BLOCK
    rows_kv = 2 * NSA_KV_HEADS

    per_step = SEL_PER_STEP
    assert n_sel % per_step == 0

    def blockspec(h, u):
        def index(b, t, pt, ix):
            blk = ix[b, h, t * per_step + u]
            return (pt[b, jnp.minimum(blk // halves, n_pages - 1)], blk % halves, 0, 0)
        return pl.BlockSpec((1, SEL_BLOCK, rows_kv, NSA_DK), index)

    def biasspec(h, u):
        return pl.BlockSpec((1, 1, SROWS, SEL_BLOCK * rows_kv),
                            lambda b, t, pt, ix: (h, ix[b, h, t * per_step + u], 0, 0))

    slots = [(h, u) for h in range(NSA_KV_HEADS) for u in range(per_step)]
    return pl.pallas_call(
        functools.partial(_nsa_sample_sel_kernel, ns=ns),
        grid_spec=pltpu.PrefetchScalarGridSpec(
            num_scalar_prefetch=2,
            grid=(nb, n_sel // per_step),
            in_specs=[pl.BlockSpec((1, 1, NSA_W), lambda b, t, pt, ix: (b, 0, COL_NQ // NSA_W))]
                     + [blockspec(h, u) for h, u in slots]
                     + [pl.BlockSpec((1, rows_kv, NSA_DK), lambda b, t, pt, ix: (b, 0, 0))]
                     + [biasspec(h, u) for h, u in slots]
                     + [branch_spec, branch_spec,
                        pl.BlockSpec((1, 1, NG_SLOT), lambda b, t, pt, ix: (b, 0, COL_NG // NG_SLOT)),
                        pl.BlockSpec((1, 1, NSA_W), lambda b, t, pt, ix: (b, 0, COL_NSL // NSA_W))],
            out_specs=pl.BlockSpec((1, 1, NSA_W), lambda b, t, pt, ix: (b, 0, 0)),
            scratch_shapes=[pltpu.VMEM((NSA_KV_HEADS * SROWS, 1), F32), pltpu.VMEM((NSA_KV_HEADS * SROWS, 1), F32),
                            pltpu.VMEM((NSA_KV_HEADS * SROWS, NSA_DK), F32)]),
        out_shape=jax.ShapeDtypeStruct((nb, 1, NSA_W), BF16),
        compiler_params=_cparams(2),
        name="nsa_sample_sel",
    )(page_table, idx, z3, *([cache] * len(slots)), kv_new, *([bias_sel] * len(slots)), o_cmp, o_win, z3b, z3b)


def _nsa_sample_win_kernel(q_ref, buf_ref, new_ref, bias_ref, bnew_ref, o_ref):
    scale = NSA_DK ** -0.5
    gw = NSA_GROUP * NSA_DK
    rows_kv = 2 * NSA_KV_HEADS
    x = buf_ref[0].reshape(buf_ref.shape[1] * rows_kv, NSA_DK).astype(BF16)
    new = new_ref[0]
    for h in range(NSA_KV_HEADS):
        q = _stack_group_q(q_ref[0][:, h * gw:(h + 1) * gw])
        s_buf = _nt(q.astype(BF16), x) * scale + bias_ref[h]
        s_new = jnp.sum(q * new[h:h + 1], axis=1, keepdims=True) * scale + bnew_ref[h]
        m = jnp.maximum(jnp.max(s_buf, axis=1, keepdims=True), s_new)
        p_buf = jnp.exp(s_buf - m)
        p_new = jnp.exp(s_new - m)
        l = jnp.sum(p_buf, axis=1, keepdims=True) + p_new
        acc = _dot(pltpu.roll(p_buf, NSA_KV_HEADS, 1).astype(BF16), x) + p_new * new[NSA_KV_HEADS + h:NSA_KV_HEADS + h + 1]
        o_ref[0, h] = acc / jnp.maximum(l, 1e-30)


def _nsa_sample_win(z3, win_buf, kv_new, bias_win, bias_new):
    nb, nbuf, rows_kv, _ = win_buf.shape
    return pl.pallas_call(
        _nsa_sample_win_kernel,
        grid=(nb,),
        in_specs=[pl.BlockSpec((1, 1, NSA_W), lambda b: (b, 0, COL_NQ // NSA_W)),
                  pl.BlockSpec((1, nbuf, rows_kv, NSA_DK), lambda b: (b, 0, 0, 0)),
                  pl.BlockSpec((1, rows_kv, NSA_DK), lambda b: (b, 0, 0)),
                  pl.BlockSpec((NSA_KV_HEADS, SROWS, nbuf * rows_kv), lambda b: (0, 0, 0)),
                  pl.BlockSpec((NSA_KV_HEADS, SROWS, 1), lambda b: (0, 0, 0))],
        out_specs=pl.BlockSpec((1, NSA_KV_HEADS, SROWS, NSA_DK), lambda b: (b, 0, 0, 0)),
        out_shape=jax.ShapeDtypeStruct((nb, NSA_KV_HEADS, SROWS, NSA_DK), F32),
        compiler_params=_cparams(1),
        name="nsa_sample_win",
    )(z3, win_buf, kv_new, bias_win, bias_new)


def _mem_heads(q, kv):
    outs = []
    for h in range(MEM_HEADS):
        k = kv[:, h * MEM_DH:(h + 1) * MEM_DH].astype(BF16)
        v = kv[:, MEM_W + h * MEM_DH:MEM_W + (h + 1) * MEM_DH].astype(BF16)
        s = _nt(q[:, h * MEM_DH:(h + 1) * MEM_DH].astype(BF16), k) * (MEM_DH ** -0.5)
        e = jnp.exp(s - jnp.max(s, axis=1, keepdims=True))
        p = e / jnp.sum(e, axis=1, keepdims=True)
        outs.append(_dot(p.astype(BF16), v))
    return jnp.concatenate(outs, axis=1)


def _mem_prompt_kernel(q_ref, kv_ref, o_ref):
    o_ref[...] = _mem_heads(q_ref[:, :MEM_W], kv_ref[...]).astype(BF16)


def _mem_prompt(z, mem_kv, batch, seq, tq):
    nq = seq // tq
    n_mem = mem_kv.shape[0] // batch
    return pl.pallas_call(
        _mem_prompt_kernel,
        grid=(batch, nq),
        in_specs=[pl.BlockSpec((tq, MQ_BLOCK), lambda b, i: (b * nq + i, COL_MQ // MQ_BLOCK)),
                  pl.BlockSpec((n_mem, 2 * MEM_W), lambda b, i: (b, 0))],
        out_specs=pl.BlockSpec((tq, MEM_W), lambda b, i: (b * nq + i, 0)),
        out_shape=jax.ShapeDtypeStruct((batch * seq, MEM_W), BF16),
        compiler_params=_cparams(2),
        name="mem_prompt",
    )(z, mem_kv)


def _mem_sample_kernel(q_ref, kv_ref, o_ref):
    q = jnp.broadcast_to(q_ref[0][:, :MEM_W], (SROWS, MEM_W))
    o_ref[0] = _mem_heads(q, kv_ref[0])[0:1].astype(BF16)


def _mem_sample(z3, mem_kv):
    nb, n_mem, _ = mem_kv.shape
    return pl.pallas_call(
        _mem_sample_kernel,
        grid=(nb,),
        in_specs=[pl.BlockSpec((1, 1, MQ_BLOCK), lambda b: (b, 0, COL_MQ // MQ_BLOCK)),
                  pl.BlockSpec((1, n_mem, 2 * MEM_W), lambda b: (b, 0, 0))],
        out_specs=pl.BlockSpec((1, 1, MEM_W), lambda b: (b, 0, 0)),
        out_shape=jax.ShapeDtypeStruct((nb, 1, MEM_W), BF16),
        compiler_params=_cparams(1),
        name="mem_sample",
    )(z3, mem_kv)


def _merge_kernel(ar_ref, an_ref, am_ref, wr_ref, wn_ref, wm_ref, g0_ref, g1_ref, g2_ref, o_ref):
    merged = (_sigmoid(g0_ref[...]) * _dot(ar_ref[...], wr_ref[...])
              + _sigmoid(g1_ref[...]) * _dot(an_ref[...], wn_ref[...])
              + _sigmoid(g2_ref[...]) * _dot(am_ref[...], wm_ref[...]))
    o_ref[...] = merged.astype(BF16)


def _merge(a_ret, a_nsa, a_mem, w_ret, w_nsa, w_mem, z, tm, tn):
    m = a_ret.shape[0]
    nt = D_MODEL // tn

    def aspec(width):
        return pl.BlockSpec((tm, width), lambda i, j: (i, 0))

    def wspec(width):
        return pl.BlockSpec((width, tn), lambda i, j: (0, j))

    def gspec(branch):
        return pl.BlockSpec((tm, tn), lambda i, j: (i, COL_MG // tn + branch * nt + j))

    return pl.pallas_call(
        _merge_kernel,
        grid=(m // tm, nt),
        in_specs=[aspec(RET_W), aspec(NSA_W), aspec(MEM_W), wspec(RET_W), wspec(NSA_W), wspec(MEM_W),
                  gspec(0), gspec(1), gspec(2)],
        out_specs=pl.BlockSpec((tm, tn), lambda i, j: (i, j)),
        out_shape=jax.ShapeDtypeStruct((m, D_MODEL), BF16),
        compiler_params=_cparams(2),
        name="merge",
    )(a_ret, a_nsa, a_mem, w_ret, w_nsa, w_mem, z, z, z)


def _out_kernel(a_ref, w_ref, x_ref, g_ref, o_ref):
    out = _dot(a_ref[...], w_ref[...])
    y = out * lax.rsqrt(jnp.mean(out * out, axis=-1, keepdims=True) + EPS)
    o_ref[...] = x_ref[...] + y * g_ref[...]


def _out_proj(merged, w_out, x, norm_post, tm):
    m = merged.shape[0]
    return pl.pallas_call(
        _out_kernel,
        grid=(m // tm,),
        in_specs=[pl.BlockSpec((tm, D_MODEL), lambda i: (i, 0)),
                  pl.BlockSpec((D_MODEL, D_MODEL), lambda i: (0, 0)),
                  pl.BlockSpec((tm, D_MODEL), lambda i: (i, 0)),
                  pl.BlockSpec((1, D_MODEL), lambda i: (0, 0))],
        out_specs=pl.BlockSpec((tm, D_MODEL), lambda i: (i, 0)),
        out_shape=jax.ShapeDtypeStruct((m, D_MODEL), F32),
        compiler_params=_cparams(1),
        name="out_proj",
    )(merged, w_out, x, norm_post.reshape(1, D_MODEL))


def _layout_w_ng(w_t):
    per_group = N_BRANCHES * NSA_GROUP
    ng = w_t[PROJ_A:PROJ_A + N_BRANCHES * NSA_HEADS].reshape(NSA_KV_HEADS, per_group, D_MODEL)
    return jnp.pad(ng, ((0, 0), (0, LANE - per_group), (0, 0))).reshape(NG_SLOT, D_MODEL)


def _pick_tile(m, cap):
    t = min(m, cap)
    while m % t:
        t //= 2
    return t


def kernel(x_prompt, x_sample, cache_cmp_kv, cache_sel_kv, cache_win_kv, state_ret, cache_mem_kv, page_table,
           mem_prompt, rel_table, norm_pre, norm_post, norm_mem, w_in, ret_norm, w_ret_up, cmp_pos, w_cmp1,
           w_cmp2, w_nsa_up, w_mem_kv, w_mem_up, w_out):
    batch, seq, _ = x_prompt.shape
    nb = x_sample.shape[0]
    assert x_sample.shape[1] == 1 and norm_pre.shape[0] == 1
    assert seq % TQ == 0 and seq >= WINDOW
    n_pool = cache_cmp_kv.shape[1]
    n_pages = page_table.shape[1]
    past = n_pages * PAGE_SIZE
    n_mem = mem_prompt.shape[1]
    assert n_pages % PAGES_PER_STEP == 0 and cache_win_kv.shape[2] == WINDOW

    w_a = w_in[0].T
    w_ng = _layout_w_ng(w_a)
    kw = CMP_STRIDE * NSA_DK
    w1 = w_cmp1[0].reshape(2, CMP_BLOCK * NSA_DK, NSA_DK).astype(BF16)
    w1a, w1b = w1[:, :kw], w1[:, kw:]
    w1ab = jnp.concatenate([w1a, w1b], axis=2)
    w2 = w_cmp2[0].astype(BF16)
    pos8 = jnp.pad(cmp_pos[0].reshape(2, 1, CMP_BLOCK * NSA_DK), ((0, 0), (0, 7), (0, 0)))
    w_ret = w_ret_up[0].astype(BF16)
    w_nsa = w_nsa_up[0].astype(BF16)
    w_mem = w_mem_up[0].astype(BF16)
    w_o = w_out[0].astype(BF16)

    m_p = batch * seq
    xp = x_prompt.reshape(m_p, D_MODEL)
    hp = _rmsnorm(xp, norm_pre[0], _pick_tile(m_p, 512))
    xs = x_sample.reshape(nb, D_MODEL)
    hs = _rmsnorm(xs, norm_pre[0], nb)
    z, zs_head = _proj(hp, w_a, _pick_tile(m_p, 1024), PROJ_TN, COL_KVC, transposed=True, rider=hs)
    zkv, kvc_rows, kvs_rows, kvw_rows, zs_kv = _proj_kv(hp, w_a, _pick_tile(seq, WINDOW), seq, hs)
    zb, zsb = _proj_tail(hp, w_a, w_ng, _pick_tile(m_p, 1024), hs)

    a_ret, ret_state_p = _retention_prompt(z, ret_norm[0], batch, seq)

    ncp = max(LANE, -(-(seq // CMP_STRIDE) // LANE) * LANE)
    ac = _cmp_stage1_dense(zkv, w1ab, batch, seq)
    kcvc = _cmp_stage2(ac, pos8, w1a, w1b, w2, 1)
    if ncp > kcvc.shape[3]:
        kcvc = jnp.pad(kcvc, ((0, 0), (0, 0), (0, 0), (0, ncp - kcvc.shape[3]), (0, 0)))
    bias_d = _bias_by_dist(rel_table, BIAS_DISTS)
    a_nsa = _nsa_prompt(z, zkv, zb, kcvc, bias_d, batch, seq)

    hm = _rmsnorm(mem_prompt.reshape(batch * n_mem, D_MODEL), norm_mem[0], _pick_tile(batch * n_mem, 512))
    mem_kv_p = _proj(hm, w_mem_kv[0], _pick_tile(batch * n_mem, 512), PROJ_TN)
    a_mem = _mem_prompt(zb, mem_kv_p, batch, seq, _pick_tile(seq, 512))

    merged = _merge(a_ret, a_nsa, a_mem, w_ret, w_nsa, w_mem, zb, _pick_tile(m_p, 1024), 512)
    y_p = _out_proj(merged, w_o, xp, norm_post[0], _pick_tile(m_p, 512)).reshape(batch, seq, D_MODEL)

    kv_shape = (1, batch, seq, 2, NSA_KV_HEADS, NSA_DK)
    new_cmp_p = kvc_rows.reshape(kv_shape)
    new_sel_p = kvs_rows.reshape(kv_shape)
    new_win_p = kvw_rows.reshape(1, batch, WINDOW, 2, NSA_KV_HEADS, NSA_DK)
    new_ret_p = ret_state_p[None]
    new_mem_p = mem_kv_p.reshape(1, batch, n_mem, 2, MEM_HEADS, MEM_DH)

    zs = jnp.concatenate([zs_head, zs_kv], axis=1)
    z3 = zs.reshape(nb, 1, PROJ_A)
    z3b = zsb.reshape(nb, 1, PROJ_B)

    a_ret_s, ret_state_s = _retention_sample(z3, state_ret[0], ret_norm[0], past)

    cache_c = cache_cmp_kv[0].reshape(n_pool, PAGE_SIZE // CMP_STRIDE, CMP_STRIDE, 2 * NSA_KV_HEADS, NSA_DK)
    cache_s = cache_sel_kv[0].reshape(n_pool, PAGE_SIZE, 2 * NSA_KV_HEADS, NSA_DK)
    ac_s = _cmp_stage1_paged(cache_c, page_table, w1ab)
    kcvc_s = _cmp_stage2(ac_s, pos8, w1a, w1b, w2, NSA_KV_HEADS)
    ncs = past // CMP_STRIDE
    kcvc_s = kcvc_s.reshape(nb, 2, ncs, KV_W)
    ns_s = past // SEL_BLOCK + 1
    nsp_s = -(-ns_s // LANE) * LANE
    assert past >= WINDOW and past >= REL_MAX_DIST and BIAS_DISTS > WINDOW
    far_s = bias_d[..., REL_MAX_DIST:REL_MAX_DIST + 1]
    hg = (NSA_KV_HEADS, NSA_GROUP)
    n_valid = (past - (CMP_BLOCK - 1)) // CMP_STRIDE + 1
    strided = bias_d[..., (past - (CMP_BLOCK - 1)) % CMP_STRIDE::CMP_STRIDE]
    n_tab = strided.shape[-1]
    assert n_valid >= n_tab and n_tab * CMP_STRIDE > REL_MAX_DIST + CMP_STRIDE and ncs >= n_valid
    bias_cs = jnp.concatenate([jnp.broadcast_to(far_s, hg + (n_valid - n_tab,)), strided[..., ::-1],
                               jnp.full(hg + (ncs - n_valid,), NEG_INF, F32)], axis=-1)
    bias_cs = _pad_group_rows(bias_cs, 1)
    nblk = jnp.arange(ncs)[:, None]
    sblk = jnp.arange(nsp_s)[None, :]
    ov_s = ((nblk >= 4 * sblk - 1) & (nblk <= 4 * sblk + 3)).astype(BF16)
    o_cmp_s, idx_s = _nsa_sample_cmp(z3, kcvc_s, bias_cs, ov_s, ns_s)
    n_sel = min(SEL_TOPK, ns_s)
    idx = idx_s[:, :, 0, :n_sel]

    hg = (NSA_KV_HEADS, NSA_GROUP)
    n_key = ns_s * SEL_BLOCK
    bias_sel = jnp.concatenate([jnp.broadcast_to(far_s, hg + (past + 1 - REL_MAX_DIST,)),
                                bias_d[..., :REL_MAX_DIST][..., ::-1],
                                jnp.full(hg + (n_key - past - 1,), NEG_INF, F32)], axis=-1)
    bias_sel = _pad_group_rows(bias_sel.reshape(hg + (ns_s, SEL_BLOCK)).transpose(0, 2, 1, 3), 2)
    own_k = jnp.arange(2 * NSA_KV_HEADS)[None, :] == jnp.arange(NSA_KV_HEADS)[:, None]
    bias_sel = jnp.where(own_k[:, None, None, None, :], bias_sel[..., None], NEG_INF)
    bias_sel = bias_sel.reshape(NSA_KV_HEADS, ns_s, SROWS, SEL_BLOCK * 2 * NSA_KV_HEADS)
    kvs_new = zs[:, COL_KVS:COL_KVS + 2 * KV_W].reshape(nb, 2 * NSA_KV_HEADS, NSA_DK)

    win_buf = cache_win_kv[0].reshape(nb, WINDOW, 2 * NSA_KV_HEADS, NSA_DK)
    bias_w = jnp.concatenate([jnp.full(hg + (1,), NEG_INF, F32), bias_d[..., 1:WINDOW][..., ::-1]], axis=-1)
    bias_w = jnp.where(own_k[:, None, None, :], _pad_group_rows(bias_w, 1)[..., None], NEG_INF)
    bias_w = bias_w.reshape(NSA_KV_HEADS, SROWS, WINDOW * 2 * NSA_KV_HEADS)
    kvw_new = zs[:, COL_KVW:COL_KVW + 2 * KV_W].reshape(nb, 2 * NSA_KV_HEADS, NSA_DK)
    o_win_s = _nsa_sample_win(z3, win_buf, kvw_new, bias_w, _pad_group_rows(bias_d[..., 0:1], 1))
    a_nsa_s = _nsa_sample_sel(z3, cache_s, kvs_new, page_table, idx, bias_sel, ns_s, o_cmp_s, o_win_s, z3b)

    mem_kv_s = cache_mem_kv[0].reshape(nb, n_mem, 2 * MEM_W)
    a_mem_s = _mem_sample(z3b, mem_kv_s)

    merged_s = _merge(a_ret_s.reshape(nb, RET_W), a_nsa_s.reshape(nb, NSA_W), a_mem_s.reshape(nb, MEM_W),
                      w_ret, w_nsa, w_mem, zsb, nb, 512)
    y_s = _out_proj(merged_s, w_o, xs, norm_post[0], nb).reshape(nb, 1, D_MODEL)

    kvs_shape = (1, nb, 1, 2, NSA_KV_HEADS, NSA_DK)
    new_cmp_s = zs[:, COL_KVC:COL_KVC + 2 * KV_W].reshape(kvs_shape)
    new_sel_s = zs[:, COL_KVS:COL_KVS + 2 * KV_W].reshape(kvs_shape)
    kvw_s = zs[:, COL_KVW:COL_KVW + 2 * KV_W].reshape(nb, 1, 2, NSA_KV_HEADS, NSA_DK)
    new_win_s = jnp.concatenate([cache_win_kv[0][:, 1:], kvw_s], axis=1)[None]
    new_ret_s = ret_state_s[None]

    return (y_p, y_s, new_cmp_p, new_sel_p, new_win_p, new_ret_p, new_mem_p,
            new_cmp_s, new_sel_s, new_win_s, new_ret_s)
```

```python
import functools
import math

import jax
import jax.numpy as jnp
from jax import lax
from jax.experimental import pallas as pl
from jax.experimental.pallas import tpu as pltpu

F32 = jnp.float32
BF16 = jnp.bfloat16

D_MODEL = 2048
PAGE_SIZE = 128
RET_HEADS = 8
RET_DK = 256
RET_DV = 256
RET_CHUNK = 128
ROPE_BASE = 10000.0
NSA_HEADS = 16
NSA_KV_HEADS = 4
NSA_GROUP = NSA_HEADS // NSA_KV_HEADS
NSA_DK = 128
CMP_BLOCK = 32
CMP_STRIDE = 16
SEL_BLOCK = 64
SEL_TOPK = 16
WINDOW = 512
MEM_HEADS = 4
MEM_DH = 384
REL_BUCKETS = 32
REL_MAX_EXACT = 16
REL_MAX_DIST = 128
N_BRANCHES = 3
EPS = 1e-6
NEG_INF = -1e30
FORCE_SCORE = 1e4

RET_W = RET_HEADS * RET_DV
NSA_W = NSA_HEADS * NSA_DK
KV_W = NSA_KV_HEADS * NSA_DK
MEM_W = MEM_HEADS * MEM_DH

COL_RQ = 0
COL_RK = COL_RQ + RET_HEADS * RET_DK
COL_RV = COL_RK + RET_HEADS * RET_DK
COL_RG = COL_RV + RET_W
COL_NQ = COL_RG + RET_W
COL_KVC = COL_NQ + NSA_W
COL_KVS = COL_KVC + 2 * KV_W
COL_KVW = COL_KVS + 2 * KV_W
PROJ_A = COL_KVW + 2 * KV_W
COL_NSL = 0
COL_MG = COL_NSL + NSA_W
COL_MQ = COL_MG + N_BRANCHES * D_MODEL
COL_NG = COL_MQ + MEM_W
NG_SLOT = NSA_KV_HEADS * 128
PROJ_B = COL_NG + NG_SLOT
MQ_BLOCK = MEM_W + NG_SLOT

LOG2E = math.log2(math.e)
LANE = 128
TQ = 256
TK = 256
NSA_ROWS = NSA_GROUP * TQ
BIAS_DISTS = 1024
FAR_GROUP = 4
RET_STEP_CHUNKS = 16
PROJ_TN = 1024
VMEM_LIMIT = 56 * 1024 * 1024


def _cparams(n_axes):
    return pltpu.CompilerParams(dimension_semantics=("arbitrary",) * n_axes, vmem_limit_bytes=VMEM_LIMIT)


def _nt(a, b):
    return lax.dot_general(a, b, (((1,), (1,)), ((), ())), preferred_element_type=F32)


def _dot(a, b):
    return jnp.dot(a, b, preferred_element_type=F32)


def _sigmoid(x):
    return 1.0 / (1.0 + jnp.exp(-x))


def _iota(shape, dim):
    return lax.broadcasted_iota(jnp.int32, shape, dim)


def _rmsnorm_kernel(x_ref, g_ref, h_ref):
    x = x_ref[...]
    ms = jnp.mean(x * x, axis=-1, keepdims=True)
    h_ref[...] = ((x * lax.rsqrt(ms + EPS)) * g_ref[...]).astype(BF16)


def _rmsnorm(x, g, tm):
    m, k = x.shape
    return pl.pallas_call(
        _rmsnorm_kernel,
        grid=(m // tm,),
        in_specs=[pl.BlockSpec((tm, k), lambda i: (i, 0)), pl.BlockSpec((1, k), lambda i: (0, 0))],
        out_specs=pl.BlockSpec((tm, k), lambda i: (i, 0)),
        out_shape=jax.ShapeDtypeStruct((m, k), BF16),
        compiler_params=_cparams(1),
        name="rmsnorm",
    )(x, g.reshape(1, k))


def _proj_kernel(h_ref, w_ref, *refs, transposed):
    o_ref, wb_ref = refs[-3 if len(refs) == 4 else -2], refs[-1]

    @pl.when(pl.program_id(1) == 0)
    def _():
        if transposed:
            _store_transposed(wb_ref, w_ref, wb_ref.shape[1])
        else:
            wb_ref[...] = w_ref[...].astype(BF16)
        if len(refs) == 4:
            refs[2][...] = _dot(refs[0][...], wb_ref[...])

    o_ref[...] = _dot(h_ref[...], wb_ref[...])


def _rider_specs(rider, k, tn):
    if rider is None:
        return [], [], []
    rows = rider.shape[0]
    return ([pl.BlockSpec((rows, k), lambda j, i: (0, 0))], [pl.BlockSpec((rows, tn), lambda j, i: (0, j))], [rows])


def _proj(h, w, tm, tn, n=None, transposed=False, rider=None):
    m, k = h.shape
    n = w.shape[0 if transposed else 1] if n is None else n
    wspec = pl.BlockSpec((tn, k), lambda j, i: (j, 0)) if transposed else pl.BlockSpec((k, tn), lambda j, i: (0, j))
    r_in, r_out, r_rows = _rider_specs(rider, k, tn)
    out = pl.pallas_call(
        functools.partial(_proj_kernel, transposed=transposed),
        grid=(n // tn, m // tm),
        in_specs=[pl.BlockSpec((tm, k), lambda j, i: (i, 0)), wspec] + r_in,
        out_specs=[pl.BlockSpec((tm, tn), lambda j, i: (i, j))] + r_out,
        out_shape=[jax.ShapeDtypeStruct((m, n), F32)] + [jax.ShapeDtypeStruct((r, n), F32) for r in r_rows],
        scratch_shapes=[pltpu.VMEM((k, tn), BF16)],
        compiler_params=_cparams(2),
        name="proj",
    )(h, w, *([] if rider is None else [rider]))
    return out[0] if rider is None else tuple(out)


def _proj_kv_kernel(h_ref, w_ref, hs_ref, o_ref, oc_ref, os_ref, ow_ref, ors_ref, wb_ref, *, tiles_per_batch):
    j, i = pl.program_id(0), pl.program_id(1)
    n_tiles = pl.num_programs(0)

    @pl.when(i == 0)
    def _():
        _store_transposed(wb_ref, w_ref, wb_ref.shape[1])
        ors_ref[...] = _dot(hs_ref[...], wb_ref[...])

    res = _dot(h_ref[...], wb_ref[...])
    o_ref[...] = res
    tm, tn = res.shape
    rows_kv = tn // NSA_DK

    @pl.when(j == n_tiles - 3)
    def _():
        oc_ref[...] = res.reshape(tm * rows_kv, NSA_DK)

    @pl.when(j == n_tiles - 2)
    def _():
        os_ref[...] = res.reshape(tm * rows_kv, NSA_DK)

    @pl.when((j == n_tiles - 1) & (i % tiles_per_batch == tiles_per_batch - 1))
    def _():
        ow_ref[...] = res[tm - WINDOW:, :].reshape(WINDOW * rows_kv, NSA_DK)


def _proj_kv(h, w_t, tm, seq, rider):
    m, k = h.shape
    tn = 2 * KV_W
    assert COL_KVC % tn == 0 and COL_KVC == PROJ_A - 3 * tn and seq % tm == 0 and tm >= WINDOW
    n_j, n_i = 3, m // tm
    first = COL_KVC // tn
    tiles_per_batch = seq // tm
    rows_kv = tn // NSA_DK

    def kv_rows(tile):
        return lambda j, i: (jnp.where(j < tile, 0, jnp.where(j == tile, i, n_i - 1)), 0)

    r_in, r_out, (r_rows,) = _rider_specs(rider, k, tn)
    return pl.pallas_call(
        functools.partial(_proj_kv_kernel, tiles_per_batch=tiles_per_batch),
        grid=(n_j, n_i),
        in_specs=[pl.BlockSpec((tm, k), lambda j, i: (i, 0)),
                  pl.BlockSpec((tn, k), lambda j, i: (first + j, 0))] + r_in,
        out_specs=[pl.BlockSpec((tm, tn), lambda j, i: (i, j)),
                   pl.BlockSpec((tm * rows_kv, NSA_DK), kv_rows(n_j - 3)),
                   pl.BlockSpec((tm * rows_kv, NSA_DK), kv_rows(n_j - 2)),
                   pl.BlockSpec((WINDOW * rows_kv, NSA_DK),
                                lambda j, i: (jnp.where(j < n_j - 1, 0, i // tiles_per_batch), 0))] + r_out,
        out_shape=[jax.ShapeDtypeStruct((m, n_j * tn), F32),
                   jax.ShapeDtypeStruct((m * rows_kv, NSA_DK), F32),
                   jax.ShapeDtypeStruct((m * rows_kv, NSA_DK), F32),
                   jax.ShapeDtypeStruct((m // seq * WINDOW * rows_kv, NSA_DK), F32),
                   jax.ShapeDtypeStruct((r_rows, n_j * tn), F32)],
        scratch_shapes=[pltpu.VMEM((k, tn), BF16)],
        compiler_params=_cparams(2),
        name="proj_kv",
    )(h, w_t, rider)


TAIL_TN = 1024


def _store_transposed(dst_ref, src_ref, rows, chunk=256):
    for r0 in range(0, rows, chunk):
        dst_ref[:, r0:r0 + chunk] = src_ref[r0:r0 + chunk, :].T.astype(BF16)


def _proj_tail_kernel(h_ref, w_ref, wng_ref, hs_ref, o_ref, ors_ref, wb_ref, *, last_rows):
    tn = wb_ref.shape[1]
    is_last = pl.program_id(0) == pl.num_programs(0) - 1

    @pl.when((pl.program_id(1) == 0) & jnp.logical_not(is_last))
    def _():
        _store_transposed(wb_ref, w_ref, tn)
        ors_ref[...] = _dot(hs_ref[...], wb_ref[...])

    @pl.when((pl.program_id(1) == 0) & is_last)
    def _():
        _store_transposed(wb_ref, w_ref, last_rows)
        wb_ref[:, last_rows:] = wng_ref[...].T.astype(BF16)
        ors_ref[...] = _dot(hs_ref[...], wb_ref[...])

    o_ref[...] = _dot(h_ref[...], wb_ref[...])


def _proj_tail(h, w_t, w_ng, tm, rider):
    m, k = h.shape
    tn = TAIL_TN
    row_ng = PROJ_A
    row_nsl = row_ng + N_BRANCHES * NSA_HEADS
    row_mq = row_nsl + NSA_W
    row_mg = row_mq + MEM_W
    assert row_mg + N_BRANCHES * D_MODEL == w_t.shape[0] and w_ng.shape == (NG_SLOT, k)
    assert COL_NSL == 0 and COL_MG % tn == 0 and COL_MQ % tn == 0 and COL_NG + NG_SLOT == PROJ_B == COL_MQ + 2 * tn
    assert row_mq % 16 == 0 and row_nsl % 16 == 0 and row_mg % 16 == 0
    t_mg, t_mq = COL_MG // tn, COL_MQ // tn

    def w_row(j, i):
        row = jnp.where(j < t_mg, row_nsl + j * tn,
                        jnp.where(j < t_mq, row_mg + (j - t_mg) * tn, row_mq + (j - t_mq) * tn))
        return (pl.multiple_of(row, 16), 0)

    r_in, r_out, (r_rows,) = _rider_specs(rider, k, tn)
    return pl.pallas_call(
        functools.partial(_proj_tail_kernel, last_rows=COL_NG - COL_MQ - tn),
        grid=(PROJ_B // tn, m // tm),
        in_specs=[pl.BlockSpec((tm, k), lambda j, i: (i, 0)),
                  pl.BlockSpec((pl.Element(tn), pl.Element(k)), w_row),
                  pl.BlockSpec((NG_SLOT, k), lambda j, i: (0, 0))] + r_in,
        out_specs=[pl.BlockSpec((tm, tn), lambda j, i: (i, j))] + r_out,
        out_shape=[jax.ShapeDtypeStruct((m, PROJ_B), F32), jax.ShapeDtypeStruct((r_rows, PROJ_B), F32)],
        scratch_shapes=[pltpu.VMEM((k, tn), BF16)],
        compiler_params=_cparams(2),
        name="proj_tail",
    )(h, w_t, w_ng, rider)


def _rope_rows(x, cos, sin):
    half = x.shape[-1] // 2
    x1, x2 = x[:, :half], x[:, half:]
    return jnp.concatenate([x1 * cos - x2 * sin, x1 * sin + x2 * cos], axis=-1)


def _head_norm_gate(o, gnorm, rg):
    oc = o - jnp.mean(o, axis=-1, keepdims=True)
    y = oc * lax.rsqrt(jnp.mean(oc * oc, axis=-1, keepdims=True) + EPS) * gnorm
    return y * (rg * _sigmoid(rg))


def _ret_prompt_kernel(q_ref, k_ref, v_ref, rg_ref, cos_ref, sin_ref, dmat_ref, xi_ref, zeta_ref, gc_ref,
                       gn_ref, a_ref, s_ref):
    @pl.when(pl.program_id(2) == 0)
    def _():
        s_ref[...] = jnp.zeros_like(s_ref)

    c = RET_CHUNK
    for t in range(q_ref.shape[0] // c):
        rows = slice(t * c, (t + 1) * c)
        cos, sin = cos_ref[rows, :], sin_ref[rows, :]
        q = _rope_rows(q_ref[rows, :], cos, sin)
        k = _rope_rows(k_ref[rows, :], cos, sin) * (RET_DK ** -0.5)
        qb, vb = q.astype(BF16), v_ref[rows, :].astype(BF16)
        state = s_ref[0, 0]
        inner = _nt(qb, k.astype(BF16)) * dmat_ref[0]
        o = _dot(inner.astype(BF16), vb) + _dot(qb, state.astype(BF16)) * xi_ref[0]
        kz_t = (k * zeta_ref[0]).T.astype(BF16)
        s_ref[0, 0] = state * gc_ref[0] + _dot(kz_t, vb)
        a_ref[rows, :] = _head_norm_gate(o, gn_ref[...], rg_ref[rows, :]).astype(BF16)


def _decay_tables(chunk):
    log_g = jnp.log1p(-jnp.exp2(-5.0 - jnp.arange(RET_HEADS, dtype=F32)))
    i = jnp.arange(chunk, dtype=F32)
    diff = i[:, None] - i[None, :]
    dmat = jnp.where(diff >= 0, jnp.exp(log_g[:, None, None] * jnp.maximum(diff, 0.0)), 0.0)
    xi = jnp.exp(log_g[:, None] * (i[None, :] + 1.0))[:, :, None]
    zeta = jnp.exp(log_g[:, None] * (chunk - 1.0 - i[None, :]))[:, :, None]
    g_chunk = jnp.exp(log_g * chunk)[:, None, None]
    return dmat, xi, zeta, g_chunk


def _rope_tables(pos):
    half = RET_DK // 2
    freq = jnp.power(ROPE_BASE, -jnp.arange(half, dtype=F32) / half)
    ang = pos.astype(F32)[:, None] * freq[None, :]
    return jnp.cos(ang), jnp.sin(ang)


def _retention_prompt(z, ret_norm, batch, seq):
    c = RET_CHUNK
    rows = _pick_tile(seq, RET_STEP_CHUNKS * c)
    nc = seq // rows
    dmat, xi, zeta, g_chunk = _decay_tables(c)
    cos, sin = _rope_tables(jnp.arange(seq))
    hb = RET_DK

    def zspec(col0):
        return pl.BlockSpec((rows, hb), lambda b, h, t, col0=col0: (b * nc + t, col0 // hb + h))

    per_head = lambda shape: pl.BlockSpec((1,) + shape, lambda b, h, t: (h, 0, 0))
    return pl.pallas_call(
        _ret_prompt_kernel,
        grid=(batch, RET_HEADS, nc),
        in_specs=[zspec(COL_RQ), zspec(COL_RK), zspec(COL_RV), zspec(COL_RG),
                  pl.BlockSpec((rows, hb // 2), lambda b, h, t: (t, 0)),
                  pl.BlockSpec((rows, hb // 2), lambda b, h, t: (t, 0)),
                  per_head((c, c)), per_head((c, 1)), per_head((c, 1)), per_head((1, 1)),
                  pl.BlockSpec((1, hb), lambda b, h, t: (0, h))],
        out_specs=[pl.BlockSpec((rows, hb), lambda b, h, t: (b * nc + t, h)),
                   pl.BlockSpec((1, 1, RET_DK, RET_DV), lambda b, h, t: (b, h, 0, 0))],
        out_shape=[jax.ShapeDtypeStruct((batch * seq, RET_W), BF16),
                   jax.ShapeDtypeStruct((batch, RET_HEADS, RET_DK, RET_DV), F32)],
        compiler_params=_cparams(3),
        name="retention_prompt",
    )(z, z, z, z, cos, sin, dmat, xi, zeta, g_chunk, ret_norm.reshape(1, RET_W))


def _column_of(row):
    n = row.shape[1]
    eye = _iota((n, n), 0) == _iota((n, n), 1)
    return jnp.sum(jnp.where(eye, jnp.broadcast_to(row, (n, n)), 0.0), axis=1, keepdims=True)


def _ret_sample_kernel(q_ref, k_ref, v_ref, rg_ref, cos_ref, sin_ref, gam_ref, gn_ref, s_ref, a_ref, so_ref):
    cos, sin = cos_ref[...], sin_ref[...]
    outs = []
    for h in range(RET_HEADS):
        cols = slice(h * RET_DK, (h + 1) * RET_DK)
        q = _rope_rows(q_ref[0][:, cols], cos, sin)
        k = _rope_rows(k_ref[0][:, cols], cos, sin) * (RET_DK ** -0.5)
        v = v_ref[0][:, cols]
        state = s_ref[0, h]
        gamma = gam_ref[h]
        qk = jnp.sum(q * k, axis=-1, keepdims=True)
        o = qk * v + jnp.sum(_column_of(q) * state, axis=0, keepdims=True) * gamma
        so_ref[0, h] = state * gamma + _column_of(k) * v
        outs.append(_head_norm_gate(o, gn_ref[:, cols], rg_ref[0][:, cols]))
    a_ref[0] = jnp.concatenate(outs, axis=1).astype(BF16)


def _retention_sample(z3, state, ret_norm, pos):
    nb = z3.shape[0]
    cos, sin = _rope_tables(jnp.full((1,), pos))
    gamma = jnp.exp(jnp.log1p(-jnp.exp2(-5.0 - jnp.arange(RET_HEADS, dtype=F32))))[:, None, None]
    hb = RET_DK
    assert RET_HEADS * RET_DK == RET_W

    def zspec(col0):
        return pl.BlockSpec((1, 1, RET_W), lambda b, col0=col0: (b, 0, col0 // RET_W))

    st_spec = pl.BlockSpec((1, RET_HEADS, RET_DK, RET_DV), lambda b: (b, 0, 0, 0))
    return pl.pallas_call(
        _ret_sample_kernel,
        grid=(nb,),
        in_specs=[zspec(COL_RQ), zspec(COL_RK), zspec(COL_RV), zspec(COL_RG),
                  pl.BlockSpec((1, hb // 2), lambda b: (0, 0)),
                  pl.BlockSpec((1, hb // 2), lambda b: (0, 0)),
                  pl.BlockSpec((RET_HEADS, 1, 1), lambda b: (0, 0, 0)),
                  pl.BlockSpec((1, RET_W), lambda b: (0, 0)),
                  st_spec],
        out_specs=[pl.BlockSpec((1, 1, RET_W), lambda b: (b, 0, 0)), st_spec],
        out_shape=[jax.ShapeDtypeStruct((nb, 1, RET_W), BF16),
                   jax.ShapeDtypeStruct(state.shape, F32)],
        compiler_params=_cparams(1),
        name="retention_sample",
    )(z3, z3, z3, z3, cos, sin, gamma, ret_norm.reshape(1, RET_W), state)


def _half_rows(ref_slice_fn, n_half):
    return jnp.concatenate([ref_slice_fn(p) for p in range(CMP_STRIDE)], axis=1)


def _cmp_stage1_dense_kernel(x_ref, w_ref, o_ref):
    nh = o_ref.shape[3]
    x = _half_rows(lambda p: x_ref[pl.ds(p, nh, stride=CMP_STRIDE), :], nh).astype(BF16)
    o_ref[0, 0, 0] = _dot(x, w_ref[0])


def _cmp_stage1_dense(z, w1ab, batch, seq):
    nh = seq // CMP_STRIDE
    return pl.pallas_call(
        _cmp_stage1_dense_kernel,
        grid=(batch, 2, NSA_KV_HEADS),
        in_specs=[pl.BlockSpec((seq, NSA_DK), lambda b, kv, h: (b, kv * NSA_KV_HEADS + h)),
                  pl.BlockSpec((1, CMP_STRIDE * NSA_DK, 2 * NSA_DK), lambda b, kv, h: (kv, 0, 0))],
        out_specs=pl.BlockSpec((1, 1, 1, nh, 2 * NSA_DK), lambda b, kv, h: (b, kv, h, 0, 0)),
        out_shape=jax.ShapeDtypeStruct((batch, 2, NSA_KV_HEADS, nh, 2 * NSA_DK), F32),
        compiler_params=_cparams(3),
        name="cmp_stage1_dense",
    )(z, w1ab)


PAGES_PER_STEP = 16


def _cmp_stage1_paged_kernel(pt_ref, *refs):
    pages, (w_ref, o_ref) = refs[:PAGES_PER_STEP], refs[PAGES_PER_STEP:]
    hp = PAGE_SIZE // CMP_STRIDE
    top = _iota((2 * NSA_KV_HEADS, NSA_DK), 0) < NSA_KV_HEADS
    cols = [[], []]
    for p in range(CMP_STRIDE):
        tiles = [[], []]
        for pg in pages:
            xp = pg[0, :, p]
            for n in range(0, hp, 2):
                a, b = xp[n], xp[n + 1]
                tiles[0].append(jnp.where(top, a, pltpu.roll(b, NSA_KV_HEADS, 0)))
                tiles[1].append(jnp.where(top, pltpu.roll(a, NSA_KV_HEADS, 0), b))
        for kv in range(2):
            cols[kv].append(jnp.concatenate(tiles[kv], axis=0))
    for kv in range(2):
        x = jnp.concatenate(cols[kv], axis=1).astype(BF16)
        o_ref[0, kv, 0] = _dot(x, w_ref[kv])


def _cmp_stage1_paged(cache, page_table, w1ab):
    nb, n_pages = page_table.shape
    hp = PAGE_SIZE // CMP_STRIDE
    steps = n_pages // PAGES_PER_STEP
    rows = PAGES_PER_STEP * hp * NSA_KV_HEADS

    def page_spec(j):
        return pl.BlockSpec((1, hp, CMP_STRIDE, 2 * NSA_KV_HEADS, NSA_DK),
                            lambda b, s, pt, j=j: (pt[b, s * PAGES_PER_STEP + j], 0, 0, 0, 0))

    wspec = pl.BlockSpec((2, CMP_STRIDE * NSA_DK, 2 * NSA_DK), lambda b, s, pt: (0, 0, 0))
    return pl.pallas_call(
        _cmp_stage1_paged_kernel,
        grid_spec=pltpu.PrefetchScalarGridSpec(
            num_scalar_prefetch=1,
            grid=(nb, steps),
            in_specs=[page_spec(j) for j in range(PAGES_PER_STEP)] + [wspec],
            out_specs=pl.BlockSpec((1, 2, 1, rows, 2 * NSA_DK), lambda b, s, pt: (b, 0, 0, s, 0))),
        out_shape=jax.ShapeDtypeStruct((nb, 2, 1, steps * rows, 2 * NSA_DK), F32),
        compiler_params=_cparams(2),
        name="cmp_stage1_paged",
    )(page_table, *([cache] * PAGES_PER_STEP), w1ab)


def _cmp_stage2_kernel(ac_ref, pos_ref, w1a_ref, w1b_ref, w2_ref, o_ref, *, shift):
    ac = ac_ref[0, 0, 0]
    nh = ac.shape[0]
    pos = pos_ref[0].astype(BF16)
    kw = CMP_STRIDE * NSA_DK
    pe = _dot(pos[:, :kw], w1a_ref[0]) + _dot(pos[:, kw:], w1b_ref[0])
    pre = ac[:, :NSA_DK] + pltpu.roll(ac[:, NSA_DK:], nh - shift, 0) + pe[0:1]
    gelu = 0.5 * pre * (1.0 + jnp.tanh(math.sqrt(2.0 / math.pi) * (pre + 0.044715 * (pre * pre * pre))))
    o_ref[0, 0, 0] = _dot(gelu.astype(BF16), w2_ref[0]).astype(BF16)


def _cmp_stage2(ac, pos8, w1a, w1b, w2, shift):
    nb, _, groups, nh, _ = ac.shape
    kw = CMP_STRIDE * NSA_DK
    return pl.pallas_call(
        functools.partial(_cmp_stage2_kernel, shift=shift),
        grid=(nb, 2, groups),
        in_specs=[pl.BlockSpec((1, 1, 1, nh, 2 * NSA_DK), lambda b, kv, h: (b, kv, h, 0, 0)),
                  pl.BlockSpec((1, 8, 2 * kw), lambda b, kv, h: (kv, 0, 0)),
                  pl.BlockSpec((1, kw, NSA_DK), lambda b, kv, h: (kv, 0, 0)),
                  pl.BlockSpec((1, kw, NSA_DK), lambda b, kv, h: (kv, 0, 0)),
                  pl.BlockSpec((1, NSA_DK, NSA_DK), lambda b, kv, h: (kv, 0, 0))],
        out_specs=pl.BlockSpec((1, 1, 1, nh, NSA_DK), lambda b, kv, h: (b, kv, h, 0, 0)),
        out_shape=jax.ShapeDtypeStruct((nb, 2, groups, nh, NSA_DK), BF16),
        compiler_params=_cparams(3),
        name="cmp_stage2",
    )(ac, pos8, w1a, w1b, w2)


def _rel_bucket(dist):
    n = jnp.maximum(dist, 0)
    nf = jnp.maximum(n, 1).astype(F32)
    scale = (REL_BUCKETS - REL_MAX_EXACT) / math.log(REL_MAX_DIST / REL_MAX_EXACT)
    large = REL_MAX_EXACT + (jnp.log(nf / REL_MAX_EXACT) * scale).astype(jnp.int32)
    large = jnp.minimum(large, REL_BUCKETS - 1)
    return jnp.where(n < REL_MAX_EXACT, n, large)


def _bias_by_dist(rel_table, n):
    tab = rel_table[_rel_bucket(jnp.arange(n))]
    return tab.T.reshape(NSA_KV_HEADS, NSA_GROUP, n)


def _pad_group_rows(t, axis):
    first = lax.slice_in_dim(t, 0, 1, axis=axis)
    return jnp.concatenate([t] + [first] * (SROWS - NSA_GROUP), axis=axis)


def _flash_init(m_ref, l_ref, acc_ref):
    m_ref[...] = jnp.full(m_ref.shape, NEG_INF, F32)
    l_ref[...] = jnp.zeros(l_ref.shape, F32)
    acc_ref[...] = jnp.zeros(acc_ref.shape, F32)


def _flash_step(s, v, m_ref, l_ref, acc_ref):
    m_old = m_ref[...]
    m_new = jnp.maximum(m_old, jnp.max(s, axis=1, keepdims=True))
    alpha = jnp.exp(m_old - m_new)
    p = jnp.exp(s - m_new)
    l_ref[...] = alpha * l_ref[...] + jnp.sum(p, axis=1, keepdims=True)
    acc_ref[...] = alpha * acc_ref[...] + _dot(p.astype(BF16), v)
    m_ref[...] = m_new


def _flash_result(l_ref, acc_ref):
    return acc_ref[...] / jnp.maximum(l_ref[...], 1e-30)


def _select_blocks(imp_t, q0, ns):
    shape = imp_t.shape
    blk = _iota(shape, 0)
    qpos = q0 + _iota(shape, 1)
    cur = qpos >> 6
    valid = blk * SEL_BLOCK <= qpos
    forced = (blk == 0) | (blk == cur) | (blk == cur - 1)
    imp_t = jnp.where(valid, jnp.where(forced, FORCE_SCORE, imp_t), NEG_INF)
    rank = jnp.zeros(shape, F32)
    for other in range(ns):
        row = imp_t[other:other + 1, :]
        ahead = (row > imp_t) | ((row == imp_t) & (blk > other))
        rank = rank + jnp.where(ahead, 1.0, 0.0)
    return jnp.where((rank < SEL_TOPK) & valid, 1.0, 0.0)


def _nsa_prompt_kernel(q_ref, ks_ref, vs_ref, kw_ref, vw_ref, kc_ref, vct_ref, wd_ref, cfar_ref, basec_ref,
                       ovt_ref, ng_ref, nsl_ref, o_ref, m_ref, l_ref, acc_ref, vst_ref, vwt_ref, tz_ref, ow_ref, *, ns):
    i = pl.program_id(2)
    q0 = i * TQ
    qall = q_ref[...] * ((NSA_DK ** -0.5) * LOG2E)
    qt = jnp.concatenate([qall[:, g * NSA_DK:(g + 1) * NSA_DK].T for g in range(NSA_GROUP)], axis=1).astype(BF16)
    c_loc = _iota((TK, NSA_ROWS), 0)
    r_loc = _iota((TK, NSA_ROWS), 1) & (TQ - 1)

    @pl.when(i == 0)
    def _():
        for kt in range(vst_ref.shape[0]):
            vst_ref[kt] = vs_ref[kt * TK:(kt + 1) * TK, :].T.astype(BF16)
            vwt_ref[kt] = vw_ref[kt * TK:(kt + 1) * TK, :].T.astype(BF16)
        below = _iota((TK, TQ), 1) >= _iota((TK, TQ), 0)
        for g in range(NSA_GROUP):
            lo = pltpu.roll(jnp.broadcast_to(wd_ref[0, 2 * g:2 * g + 1, :], (TK, TQ)), 0, 1, stride=1, stride_axis=0)
            hi = pltpu.roll(jnp.broadcast_to(wd_ref[0, 2 * g + 1:2 * g + 2, :], (TK, TQ)), 0, 1, stride=1, stride_axis=0)
            tz_ref[0, :, g * TQ:(g + 1) * TQ] = lo
            tz_ref[1, :, g * TQ:(g + 1) * TQ] = jnp.where(below, hi, lo)

    ncp = kc_ref.shape[3]
    shift = (TQ // CMP_STRIDE) * i
    bias_c = basec_ref[0, pl.ds(pl.multiple_of(ncp - shift, TQ // CMP_STRIDE), ncp), :]
    s = _dot(kc_ref[0, 0, 0], qt) + bias_c
    m = jnp.max(s, axis=0, keepdims=True)
    e = jnp.exp2(s - m)
    inv = jnp.where(m > 0.5 * NEG_INF, 1.0 / jnp.maximum(jnp.sum(e, axis=0, keepdims=True), 1e-30), 0.0)
    p = e * inv
    o_cmp = _dot(vct_ref[0, 0], p.astype(BF16))

    psum = p[:, 0:TQ] + p[:, TQ:2 * TQ] + p[:, 2 * TQ:3 * TQ] + p[:, 3 * TQ:4 * TQ]
    hi = psum.astype(BF16)
    lo = (psum - hi.astype(F32)).astype(BF16)
    ovt = ovt_ref[...]
    imp_t = _dot(ovt, hi) + _dot(ovt, lo)
    ns8 = -(-ns // 8) * 8
    sel_t = _select_blocks(imp_t[:ns8], q0, ns)
    assert ns8 + 2 <= LANE
    sel_neg = jnp.concatenate([jnp.where(sel_t > 0.5, 0.0, NEG_INF)] * NSA_GROUP, axis=1)
    cfar = cfar_ref[0]
    cfar_hi = cfar.astype(BF16).astype(F32)
    sub8 = _iota((8, NSA_ROWS), 0)
    cfar8 = jnp.where(sub8 == 0, cfar_hi, jnp.where(sub8 == 1, cfar - cfar_hi, 0.0))
    pad_rows = jnp.zeros((LANE - ns8 - 8, NSA_ROWS), F32)
    qx_far = jnp.concatenate([qt, jnp.concatenate([sel_neg, cfar8, pad_rows], axis=0).astype(BF16)], axis=0)
    qx_near = jnp.concatenate([qt, jnp.concatenate([sel_neg, jnp.zeros_like(cfar8), pad_rows], axis=0).astype(BF16)],
                              axis=0)
    lane_k = _iota((TK, LANE), 1)
    ones_k = (lane_k == ns8) | (lane_k == ns8 + 1)

    def flash_step(scs, vts):
        m_old = m_ref[...]
        m_new = m_old
        for sc in scs:
            m_new = jnp.maximum(m_new, jnp.max(sc, axis=0, keepdims=True))
        alpha = jnp.exp2(m_old - m_new)
        l_new = alpha * l_ref[...]
        acc = alpha * acc_ref[...]
        for sc, vt in zip(scs, vts):
            pt = jnp.exp2(sc - m_new)
            l_new = l_new + jnp.sum(pt, axis=0, keepdims=True)
            acc = acc + _dot(vt, pt.astype(BF16))
        l_ref[...] = l_new
        acc_ref[...] = acc
        m_ref[...] = m_new

    def sel_scores(kt, bias, causal):
        k = ks_ref[pl.ds(pl.multiple_of(kt * TK, TK), TK), :].astype(BF16)
        blk_of_key = (TK // SEL_BLOCK) * kt + (_iota((TK, LANE), 0) >> 6)
        extra = jnp.where((lane_k == blk_of_key) | ones_k, 1.0, 0.0).astype(BF16)
        kx = jnp.concatenate([k, extra], axis=1)
        sc = _dot(kx, qx_far) if bias is None else _dot(kx, qx_near) + bias
        if causal:
            sc = jnp.where(c_loc <= r_loc, sc, NEG_INF)
        return sc

    def softmax_tiles(scs, vts):
        m_new = jnp.max(scs[0], axis=0, keepdims=True)
        for sc in scs[1:]:
            m_new = jnp.maximum(m_new, jnp.max(sc, axis=0, keepdims=True))
        l_new, acc = None, None
        for sc, vt in zip(scs, vts):
            pt = jnp.exp2(sc - m_new)
            l_t, acc_t = jnp.sum(pt, axis=0, keepdims=True), _dot(vt, pt.astype(BF16))
            l_new, acc = (l_t, acc_t) if l_new is None else (l_new + l_t, acc + acc_t)
        return m_new, l_new, acc

    kt_prev = jnp.maximum(i - 1, 0)
    prev_scores = jnp.where(i >= 1, sel_scores(kt_prev, tz_ref[1], False), NEG_INF)
    m_ref[...], l_ref[...], acc_ref[...] = softmax_tiles([sel_scores(i, tz_ref[0], True), prev_scores],
                                                         [vst_ref[i], vst_ref[kt_prev]])

    def win_scores(off):
        kt = jnp.maximum(i - off, 0)
        k = kw_ref[pl.ds(pl.multiple_of(kt * TK, TK), TK), :].astype(BF16)
        if off < 2:
            sc = _dot(k, qt) + tz_ref[off]
        else:
            sc = _dot(jnp.concatenate([k, jnp.where(ones_k, 1.0, 0.0).astype(BF16)], axis=1), qx_far)
        if off == 0:
            return jnp.where(c_loc <= r_loc, sc, NEG_INF)
        keep = (c_loc > r_loc) & (i >= off) if off * TK == WINDOW else (i >= off)
        return jnp.where(keep, sc, NEG_INF)

    n_win = WINDOW // TK + 1
    _, l_win, acc_win = softmax_tiles([win_scores(off) for off in range(n_win)],
                                      [vwt_ref[jnp.maximum(i - off, 0)] for off in range(n_win)])
    ow_ref[...] = acc_win / jnp.maximum(l_win, 1e-30)

    def far_tiles(kt_first, count):
        kts = [kt_first - t for t in range(count)]
        flash_step([sel_scores(kt, None, False) for kt in kts], [vst_ref[kt] for kt in kts])

    def far_group(j, carry):
        far_tiles(i - 2 - FAR_GROUP * j, FAR_GROUP)
        return carry

    n_far = jnp.maximum(i - 1, 0)
    lax.fori_loop(0, n_far // FAR_GROUP, far_group, 0)
    for rest in range(1, FAR_GROUP):
        pl.when(n_far % FAR_GROUP == rest)(functools.partial(far_tiles, rest - 1, rest))

    o_sel = _flash_result(l_ref, acc_ref)
    o_win = ow_ref[...]

    gates_t = _sigmoid(ng_ref[...]).T
    nsl = nsl_ref[...]
    outs = []
    for g in range(NSA_GROUP):
        cols = slice(g * TQ, (g + 1) * TQ)
        o_t = (gates_t[3 * g:3 * g + 1] * o_cmp[:, cols] + gates_t[3 * g + 1:3 * g + 2] * o_sel[:, cols]
               + gates_t[3 * g + 2:3 * g + 3] * o_win[:, cols])
        x = nsl[:, g * NSA_DK:(g + 1) * NSA_DK]
        outs.append(o_t.T * (x * _sigmoid(x)))
    o_ref[...] = jnp.concatenate(outs, axis=1).astype(BF16)


def _lanes_by_head(t):
    hk, g, rows, tq = t.shape
    return t.transpose(0, 2, 1, 3).reshape(hk, rows, g * tq)


def _nsa_prompt(z, zkv, zb, kcvc, bias_d, batch, seq):
    nq = seq // TQ
    ns = seq // SEL_BLOCK
    ncp = kcvc.shape[3]
    nsp = LANE
    wn = TQ // CMP_STRIDE
    assert TQ == TK and ns <= nsp and ncp >= seq // CMP_STRIDE and ncp % LANE == 0 and ncp > wn
    gw = NSA_GROUP * NSA_DK
    n_dist = bias_d.shape[-1]
    assert n_dist >= 2 * TK + TQ
    bias_d = bias_d * LOG2E

    tz = bias_d[..., :2 * TQ].reshape(NSA_KV_HEADS, 2 * NSA_GROUP, TQ)
    far = jnp.broadcast_to(bias_d[..., REL_MAX_DIST][:, :, None, None], (NSA_KV_HEADS, NSA_GROUP, 1, TQ))
    cfar = _lanes_by_head(far)
    half = n_dist // 2
    start = CMP_STRIDE * wn - (CMP_BLOCK - 1)
    assert start + TQ <= half and 2 * wn * CMP_STRIDE - start <= half
    w_ext = jnp.concatenate([bias_d[..., :half], jnp.full(bias_d.shape[:-1] + (n_dist - half,), NEG_INF, F32)], -1)
    near = jnp.tile(w_ext, (1, 1, 2 * wn + 1))[..., :2 * wn * (n_dist - CMP_STRIDE)]
    near = near.reshape(NSA_KV_HEADS, NSA_GROUP, 2 * wn, n_dist - CMP_STRIDE)[..., start:start + TQ]
    basec = jnp.concatenate([jnp.broadcast_to(far, (NSA_KV_HEADS, NSA_GROUP, ncp - wn, TQ)), near,
                             jnp.full((NSA_KV_HEADS, NSA_GROUP, ncp - wn, TQ), NEG_INF, F32)], axis=2)
    basec = _lanes_by_head(basec)
    sblk = jnp.arange(nsp)[:, None]
    nblk = jnp.arange(ncp)[None, :]
    ovt = ((nblk >= 4 * sblk - 1) & (nblk <= 4 * sblk + 3)).astype(BF16)

    vct = kcvc[:, 1].transpose(0, 1, 3, 2)

    def kvspec(col0, which):
        return pl.BlockSpec((seq, NSA_DK),
                            lambda b, h, i: (b, (col0 - COL_KVC) // NSA_DK + which * NSA_KV_HEADS + h))

    vt_scratch = pltpu.VMEM((seq // TK, NSA_DK, TK), BF16)
    return pl.pallas_call(
        functools.partial(_nsa_prompt_kernel, ns=ns),
        grid=(batch, NSA_KV_HEADS, nq),
        in_specs=[pl.BlockSpec((TQ, gw), lambda b, h, i: (b * nq + i, COL_NQ // gw + h)),
                  kvspec(COL_KVS, 0), kvspec(COL_KVS, 1), kvspec(COL_KVW, 0), kvspec(COL_KVW, 1),
                  pl.BlockSpec((1, 1, 1, ncp, NSA_DK), lambda b, h, i: (b, 0, h, 0, 0)),
                  pl.BlockSpec((1, 1, NSA_DK, ncp), lambda b, h, i: (b, h, 0, 0)),
                  pl.BlockSpec((1, 2 * NSA_GROUP, TQ), lambda b, h, i: (h, 0, 0)),
                  pl.BlockSpec((1, 1, NSA_ROWS), lambda b, h, i: (h, 0, 0)),
                  pl.BlockSpec((1, 2 * ncp, NSA_ROWS), lambda b, h, i: (h, 0, 0)),
                  pl.BlockSpec((nsp, ncp), lambda b, h, i: (0, 0)),
                  pl.BlockSpec((TQ, LANE), lambda b, h, i: (b * nq + i, COL_NG // LANE + h)),
                  pl.BlockSpec((TQ, gw), lambda b, h, i: (b * nq + i, COL_NSL // gw + h))],
        out_specs=pl.BlockSpec((TQ, gw), lambda b, h, i: (b * nq + i, h)),
        out_shape=jax.ShapeDtypeStruct((batch * seq, NSA_W), BF16),
        scratch_shapes=[pltpu.VMEM((1, NSA_ROWS), F32), pltpu.VMEM((1, NSA_ROWS), F32),
                        pltpu.VMEM((NSA_DK, NSA_ROWS), F32), vt_scratch, vt_scratch,
                        pltpu.VMEM((2, TK, NSA_ROWS), F32), pltpu.VMEM((NSA_DK, NSA_ROWS), F32)],
        compiler_params=_cparams(3),
        name="nsa_prompt",
    )(z, zkv, zkv, zkv, zkv, kcvc, vct, tz, cfar, basec, ovt, zb, zb)


SROWS = 8
SEL_PER_STEP = 4


def _stack_group_q(q_row):
    heads = [q_row[:, g * NSA_DK:(g + 1) * NSA_DK] for g in range(NSA_GROUP)]
    return jnp.concatenate(heads + [heads[0]] * (SROWS - NSA_GROUP), axis=0)


def _nsa_sample_cmp_kernel(q_ref, kc_ref, vc_ref, bias_ref, ov_ref, o_ref, idx_ref, *, ns):
    scale = NSA_DK ** -0.5
    qs = _stack_group_q(q_ref[0]).astype(BF16)
    s = _nt(qs, kc_ref[0, 0]) * scale + bias_ref[0]
    m = jnp.max(s, axis=1, keepdims=True)
    e = jnp.exp(s - m)
    inv = jnp.where(m > 0.5 * NEG_INF, 1.0 / jnp.maximum(jnp.sum(e, axis=1, keepdims=True), 1e-30), 0.0)
    p = e * inv
    o_ref[0, 0] = _dot(p.astype(BF16), vc_ref[0, 0])
    psum = jnp.broadcast_to(jnp.sum(p[0:NSA_GROUP], axis=0, keepdims=True), p.shape)
    hi = psum.astype(BF16)
    lo = (psum - hi.astype(F32)).astype(BF16)
    imp = (_dot(hi, ov_ref[...]) + _dot(lo, ov_ref[...]))[0:1]
    nsp = imp.shape[1]
    blk_r = _iota((1, nsp), 1)
    cur = ns - 1
    forced = (blk_r == 0) | (blk_r == cur) | (blk_r == cur - 1)
    imp = jnp.where(blk_r < ns, jnp.where(forced, FORCE_SCORE, imp), 2.0 * NEG_INF)
    imp_c = _column_of(imp)
    i_r = _iota((nsp, nsp), 1)
    j_c = _iota((nsp, nsp), 0)
    ahead = (imp > imp_c) | ((imp == imp_c) & (i_r < j_c))
    rank_c = jnp.sum(jnp.where(ahead, 1.0, 0.0), axis=1, keepdims=True)
    slot = _iota((nsp, LANE), 1).astype(F32)
    picks = jnp.where(rank_c == slot, _iota((nsp, LANE), 0).astype(F32), 0.0)
    idx_ref[0, 0] = jnp.broadcast_to(jnp.sum(picks, axis=0, keepdims=True), (SROWS, LANE)).astype(jnp.int32)


def _nsa_sample_cmp(z3, kcvc, bias_c, ov, ns):
    nb = z3.shape[0]
    ncp = kcvc.shape[2]
    nsp = ov.shape[1]
    gw = NSA_GROUP * NSA_DK

    def cspec(which):
        return pl.BlockSpec((1, 1, ncp, NSA_DK), lambda b, h: (b, which, 0, h))

    return pl.pallas_call(
        functools.partial(_nsa_sample_cmp_kernel, ns=ns),
        grid=(nb, NSA_KV_HEADS),
        in_specs=[pl.BlockSpec((1, 1, gw), lambda b, h: (b, 0, COL_NQ // gw + h)),
                  cspec(0), cspec(1),
                  pl.BlockSpec((1, SROWS, ncp), lambda b, h: (h, 0, 0)),
                  pl.BlockSpec((ncp, nsp), lambda b, h: (0, 0))],
        out_specs=[pl.BlockSpec((1, 1, SROWS, NSA_DK), lambda b, h: (b, h, 0, 0)),
                   pl.BlockSpec((1, 1, SROWS, LANE), lambda b, h: (b, h, 0, 0))],
        out_shape=[jax.ShapeDtypeStruct((nb, NSA_KV_HEADS, SROWS, NSA_DK), F32),
                   jax.ShapeDtypeStruct((nb, NSA_KV_HEADS, SROWS, LANE), jnp.int32)],
        compiler_params=_cparams(2),
        name="nsa_sample_cmp",
    )(z3, kcvc, kcvc, bias_c, ov)


def _nsa_sample_sel_kernel(pt_ref, idx_ref, q_ref, *refs, ns):
    n_blk = NSA_KV_HEADS * SEL_PER_STEP
    blocks, (new_ref,), biases = refs[:n_blk], refs[n_blk:n_blk + 1], refs[n_blk + 1:2 * n_blk + 1]
    oc_ref, ow_ref, ng_ref, nsl_ref, o_ref, m_ref, l_ref, acc_ref = refs[2 * n_blk + 1:]
    b, t = pl.program_id(0), pl.program_id(1)
    gw = NSA_GROUP * NSA_DK
    rows_kv = 2 * NSA_KV_HEADS
    width = SEL_BLOCK * rows_kv

    @pl.when(t == 0)
    def _():
        _flash_init(m_ref, l_ref, acc_ref)

    new_rows = jnp.concatenate([new_ref[0]] * SEL_BLOCK, axis=0)
    xs, scs = [], []
    for h in range(NSA_KV_HEADS):
        qs = _stack_group_q(q_ref[0][:, h * gw:(h + 1) * gw]).astype(BF16)
        row = []
        for u in range(SEL_PER_STEP):
            r = h * SEL_PER_STEP + u
            is_new = idx_ref[b, h, t * SEL_PER_STEP + u] == ns - 1
            x = jnp.where(is_new, new_rows, blocks[r][0].reshape(width, NSA_DK)).astype(BF16)
            xs.append(x)
            row.append(_nt(qs, x) * (NSA_DK ** -0.5) + biases[r][0, 0])
        scs.append(jnp.concatenate(row, axis=1))
    sc = jnp.concatenate(scs, axis=0)
    m_old = m_ref[...]
    m_new = jnp.maximum(m_old, jnp.max(sc, axis=1, keepdims=True))
    alpha = jnp.exp(m_old - m_new)
    p = jnp.exp(sc - m_new)
    l_ref[...] = alpha * l_ref[...] + jnp.sum(p, axis=1, keepdims=True)
    pv = pltpu.roll(p, NSA_KV_HEADS, 1).astype(BF16)
    acc = alpha * acc_ref[...]
    upd = []
    for h in range(NSA_KV_HEADS):
        ph = pv[h * SROWS:(h + 1) * SROWS]
        upd.append(sum(_dot(ph[:, u * width:(u + 1) * width], xs[h * SEL_PER_STEP + u]) for u in range(SEL_PER_STEP)))
    acc_ref[...] = acc + jnp.concatenate(upd, axis=0)
    m_ref[...] = m_new

    @pl.when(t == pl.num_programs(1) - 1)
    def _():
        o_sel = acc_ref[...] / jnp.maximum(l_ref[...], 1e-30)
        gates = _sigmoid(ng_ref[0])
        nsl = nsl_ref[0]
        outs = []
        for h in range(NSA_KV_HEADS):
            for g in range(NSA_GROUP):
                c = h * LANE + N_BRANCHES * g
                r = h * SROWS + g
                o = (gates[:, c:c + 1] * oc_ref[0, h, g:g + 1] + gates[:, c + 1:c + 2] * o_sel[r:r + 1]
                     + gates[:, c + 2:c + 3] * ow_ref[0, h, g:g + 1])
                x = nsl[:, (h * NSA_GROUP + g) * NSA_DK:(h * NSA_GROUP + g + 1) * NSA_DK]
                outs.append(o * (x * _sigmoid(x)))
        o_ref[0] = jnp.concatenate(outs, axis=1).astype(BF16)


def _nsa_sample_sel(z3, cache, kv_new, page_table, idx, bias_sel, ns, o_cmp, o_win, z3b):
    nb, n_pages = page_table.shape
    branch_spec = pl.BlockSpec((1, NSA_KV_HEADS, SROWS, NSA_DK), lambda b, t, pt, ix: (b, 0, 0, 0))
    n_sel = idx.shape[2]
    halves = PAGE_SIZE // SEL_BLOCK
    rows_kv = 2 * NSA_KV_HEADS

    per_step = SEL_PER_STEP
    assert n_sel % per_step == 0

    def blockspec(h, u):
        def index(b, t, pt, ix):
            blk = ix[b, h, t * per_step + u]
            return (pt[b, jnp.minimum(blk // halves, n_pages - 1)], blk % halves, 0, 0)
        return pl.BlockSpec((1, SEL_BLOCK, rows_kv, NSA_DK), index)

    def biasspec(h, u):
        return pl.BlockSpec((1, 1, SROWS, SEL_BLOCK * rows_kv),
                            lambda b, t, pt, ix: (h, ix[b, h, t * per_step + u], 0, 0))

    slots = [(h, u) for h in range(NSA_KV_HEADS) for u in range(per_step)]
    return pl.pallas_call(
        functools.partial(_nsa_sample_sel_kernel, ns=ns),
        grid_spec=pltpu.PrefetchScalarGridSpec(
            num_scalar_prefetch=2,
            grid=(nb, n_sel // per_step),
            in_specs=[pl.BlockSpec((1, 1, NSA_W), lambda b, t, pt, ix: (b, 0, COL_NQ // NSA_W))]
                     + [blockspec(h, u) for h, u in slots]
                     + [pl.BlockSpec((1, rows_kv, NSA_DK), lambda b, t, pt, ix: (b, 0, 0))]
                     + [biasspec(h, u) for h, u in slots]
                     + [branch_spec, branch_spec,
                        pl.BlockSpec((1, 1, NG_SLOT), lambda b, t, pt, ix: (b, 0, COL_NG // NG_SLOT)),
                        pl.BlockSpec((1, 1, NSA_W), lambda b, t, pt, ix: (b, 0, COL_NSL // NSA_W))],
            out_specs=pl.BlockSpec((1, 1, NSA_W), lambda b, t, pt, ix: (b, 0, 0)),
            scratch_shapes=[pltpu.VMEM((NSA_KV_HEADS * SROWS, 1), F32), pltpu.VMEM((NSA_KV_HEADS * SROWS, 1), F32),
                            pltpu.VMEM((NSA_KV_HEADS * SROWS, NSA_DK), F32)]),
        out_shape=jax.ShapeDtypeStruct((nb, 1, NSA_W), BF16),
        compiler_params=_cparams(2),
        name="nsa_sample_sel",
    )(page_table, idx, z3, *([cache] * len(slots)), kv_new, *([bias_sel] * len(slots)), o_cmp, o_win, z3b, z3b)


def _nsa_sample_win_kernel(q_ref, buf_ref, new_ref, bias_ref, bnew_ref, o_ref):
    scale = NSA_DK ** -0.5
    gw = NSA_GROUP * NSA_DK
    rows_kv = 2 * NSA_KV_HEADS
    x = buf_ref[0].reshape(buf_ref.shape[1] * rows_kv, NSA_DK).astype(BF16)
    new = new_ref[0]
    for h in range(NSA_KV_HEADS):
        q = _stack_group_q(q_ref[0][:, h * gw:(h + 1) * gw])
        s_buf = _nt(q.astype(BF16), x) * scale + bias_ref[h]
        s_new = jnp.sum(q * new[h:h + 1], axis=1, keepdims=True) * scale + bnew_ref[h]
        m = jnp.maximum(jnp.max(s_buf, axis=1, keepdims=True), s_new)
        p_buf = jnp.exp(s_buf - m)
        p_new = jnp.exp(s_new - m)
        l = jnp.sum(p_buf, axis=1, keepdims=True) + p_new
        acc = _dot(pltpu.roll(p_buf, NSA_KV_HEADS, 1).astype(BF16), x) + p_new * new[NSA_KV_HEADS + h:NSA_KV_HEADS + h + 1]
        o_ref[0, h] = acc / jnp.maximum(l, 1e-30)


def _nsa_sample_win(z3, win_buf, kv_new, bias_win, bias_new):
    nb, nbuf, rows_kv, _ = win_buf.shape
    return pl.pallas_call(
        _nsa_sample_win_kernel,
        grid=(nb,),
        in_specs=[pl.BlockSpec((1, 1, NSA_W), lambda b: (b, 0, COL_NQ // NSA_W)),
                  pl.BlockSpec((1, nbuf, rows_kv, NSA_DK), lambda b: (b, 0, 0, 0)),
                  pl.BlockSpec((1, rows_kv, NSA_DK), lambda b: (b, 0, 0)),
                  pl.BlockSpec((NSA_KV_HEADS, SROWS, nbuf * rows_kv), lambda b: (0, 0, 0)),
                  pl.BlockSpec((NSA_KV_HEADS, SROWS, 1), lambda b: (0, 0, 0))],
        out_specs=pl.BlockSpec((1, NSA_KV_HEADS, SROWS, NSA_DK), lambda b: (b, 0, 0, 0)),
        out_shape=jax.ShapeDtypeStruct((nb, NSA_KV_HEADS, SROWS, NSA_DK), F32),
        compiler_params=_cparams(1),
        name="nsa_sample_win",
    )(z3, win_buf, kv_new, bias_win, bias_new)


def _mem_heads(q, kv):
    outs = []
    for h in range(MEM_HEADS):
        k = kv[:, h * MEM_DH:(h + 1) * MEM_DH].astype(BF16)
        v = kv[:, MEM_W + h * MEM_DH:MEM_W + (h + 1) * MEM_DH].astype(BF16)
        s = _nt(q[:, h * MEM_DH:(h + 1) * MEM_DH].astype(BF16), k) * (MEM_DH ** -0.5)
        e = jnp.exp(s - jnp.max(s, axis=1, keepdims=True))
        p = e / jnp.sum(e, axis=1, keepdims=True)
        outs.append(_dot(p.astype(BF16), v))
    return jnp.concatenate(outs, axis=1)


def _mem_prompt_kernel(q_ref, kv_ref, o_ref):
    o_ref[...] = _mem_heads(q_ref[:, :MEM_W], kv_ref[...]).astype(BF16)


def _mem_prompt(z, mem_kv, batch, seq, tq):
    nq = seq // tq
    n_mem = mem_kv.shape[0] // batch
    return pl.pallas_call(
        _mem_prompt_kernel,
        grid=(batch, nq),
        in_specs=[pl.BlockSpec((tq, MQ_BLOCK), lambda b, i: (b * nq + i, COL_MQ // MQ_BLOCK)),
                  pl.BlockSpec((n_mem, 2 * MEM_W), lambda b, i: (b, 0))],
        out_specs=pl.BlockSpec((tq, MEM_W), lambda b, i: (b * nq + i, 0)),
        out_shape=jax.ShapeDtypeStruct((batch * seq, MEM_W), BF16),
        compiler_params=_cparams(2),
        name="mem_prompt",
    )(z, mem_kv)


def _mem_sample_kernel(q_ref, kv_ref, o_ref):
    q = jnp.broadcast_to(q_ref[0][:, :MEM_W], (SROWS, MEM_W))
    o_ref[0] = _mem_heads(q, kv_ref[0])[0:1].astype(BF16)


def _mem_sample(z3, mem_kv):
    nb, n_mem, _ = mem_kv.shape
    return pl.pallas_call(
        _mem_sample_kernel,
        grid=(nb,),
        in_specs=[pl.BlockSpec((1, 1, MQ_BLOCK), lambda b: (b, 0, COL_MQ // MQ_BLOCK)),
                  pl.BlockSpec((1, n_mem, 2 * MEM_W), lambda b: (b, 0, 0))],
        out_specs=pl.BlockSpec((1, 1, MEM_W), lambda b: (b, 0, 0)),
        out_shape=jax.ShapeDtypeStruct((nb, 1, MEM_W), BF16),
        compiler_params=_cparams(1),
        name="mem_sample",
    )(z3, mem_kv)


def _merge_kernel(ar_ref, an_ref, am_ref, wr_ref, wn_ref, wm_ref, g0_ref, g1_ref, g2_ref, o_ref):
    merged = (_sigmoid(g0_ref[...]) * _dot(ar_ref[...], wr_ref[...])
              + _sigmoid(g1_ref[...]) * _dot(an_ref[...], wn_ref[...])
              + _sigmoid(g2_ref[...]) * _dot(am_ref[...], wm_ref[...]))
    o_ref[...] = merged.astype(BF16)


def _merge(a_ret, a_nsa, a_mem, w_ret, w_nsa, w_mem, z, tm, tn):
    m = a_ret.shape[0]
    nt = D_MODEL // tn

    def aspec(width):
        return pl.BlockSpec((tm, width), lambda i, j: (i, 0))

    def wspec(width):
        return pl.BlockSpec((width, tn), lambda i, j: (0, j))

    def gspec(branch):
        return pl.BlockSpec((tm, tn), lambda i, j: (i, COL_MG // tn + branch * nt + j))

    return pl.pallas_call(
        _merge_kernel,
        grid=(m // tm, nt),
        in_specs=[aspec(RET_W), aspec(NSA_W), aspec(MEM_W), wspec(RET_W), wspec(NSA_W), wspec(MEM_W),
                  gspec(0), gspec(1), gspec(2)],
        out_specs=pl.BlockSpec((tm, tn), lambda i, j: (i, j)),
        out_shape=jax.ShapeDtypeStruct((m, D_MODEL), BF16),
        compiler_params=_cparams(2),
        name="merge",
    )(a_ret, a_nsa, a_mem, w_ret, w_nsa, w_mem, z, z, z)


def _out_kernel(a_ref, w_ref, x_ref, g_ref, o_ref):
    out = _dot(a_ref[...], w_ref[...])
    y = out * lax.rsqrt(jnp.mean(out * out, axis=-1, keepdims=True) + EPS)
    o_ref[...] = x_ref[...] + y * g_ref[...]


def _out_proj(merged, w_out, x, norm_post, tm):
    m = merged.shape[0]
    return pl.pallas_call(
        _out_kernel,
        grid=(m // tm,),
        in_specs=[pl.BlockSpec((tm, D_MODEL), lambda i: (i, 0)),
                  pl.BlockSpec((D_MODEL, D_MODEL), lambda i: (0, 0)),
                  pl.BlockSpec((tm, D_MODEL), lambda i: (i, 0)),
                  pl.BlockSpec((1, D_MODEL), lambda i: (0, 0))],
        out_specs=pl.BlockSpec((tm, D_MODEL), lambda i: (i, 0)),
        out_shape=jax.ShapeDtypeStruct((m, D_MODEL), F32),
        compiler_params=_cparams(1),
        name="out_proj",
    )(merged, w_out, x, norm_post.reshape(1, D_MODEL))


def _layout_w_ng(w_t):
    per_group = N_BRANCHES * NSA_GROUP
    ng = w_t[PROJ_A:PROJ_A + N_BRANCHES * NSA_HEADS].reshape(NSA_KV_HEADS, per_group, D_MODEL)
    return jnp.pad(ng, ((0, 0), (0, LANE - per_group), (0, 0))).reshape(NG_SLOT, D_MODEL)


def _pick_tile(m, cap):
    t = min(m, cap)
    while m % t:
        t //= 2
    return t


def kernel(x_prompt, x_sample, cache_cmp_kv, cache_sel_kv, cache_win_kv, state_ret, cache_mem_kv, page_table,
           mem_prompt, rel_table, norm_pre, norm_post, norm_mem, w_in, ret_norm, w_ret_up, cmp_pos, w_cmp1,
           w_cmp2, w_nsa_up, w_mem_kv, w_mem_up, w_out):
    batch, seq, _ = x_prompt.shape
    nb = x_sample.shape[0]
    assert x_sample.shape[1] == 1 and norm_pre.shape[0] == 1
    assert seq % TQ == 0 and seq >= WINDOW
    n_pool = cache_cmp_kv.shape[1]
    n_pages = page_table.shape[1]
    past = n_pages * PAGE_SIZE
    n_mem = mem_prompt.shape[1]
    assert n_pages % PAGES_PER_STEP == 0 and cache_win_kv.shape[2] == WINDOW

    w_a = w_in[0].T
    w_ng = _layout_w_ng(w_a)
    kw = CMP_STRIDE * NSA_DK
    w1 = w_cmp1[0].reshape(2, CMP_BLOCK * NSA_DK, NSA_DK).astype(BF16)
    w1a, w1b = w1[:, :kw], w1[:, kw:]
    w1ab = jnp.concatenate([w1a, w1b], axis=2)
    w2 = w_cmp2[0].astype(BF16)
    pos8 = jnp.pad(cmp_pos[0].reshape(2, 1, CMP_BLOCK * NSA_DK), ((0, 0), (0, 7), (0, 0)))
    w_ret = w_ret_up[0].astype(BF16)
    w_nsa = w_nsa_up[0].astype(BF16)
    w_mem = w_mem_up[0].astype(BF16)
    w_o = w_out[0].astype(BF16)

    m_p = batch * seq
    xp = x_prompt.reshape(m_p, D_MODEL)
    hp = _rmsnorm(xp, norm_pre[0], _pick_tile(m_p, 1024))
    xs = x_sample.reshape(nb, D_MODEL)
    hs = _rmsnorm(xs, norm_pre[0], nb)
    z, zs_head = _proj(hp, w_a, _pick_tile(m_p, 1024), PROJ_TN, COL_KVC, transposed=True, rider=hs)
    zkv, kvc_rows, kvs_rows, kvw_rows, zs_kv = _proj_kv(hp, w_a, _pick_tile(seq, WINDOW), seq, hs)
    zb, zsb = _proj_tail(hp, w_a, w_ng, _pick_tile(m_p, 1024), hs)

    a_ret, ret_state_p = _retention_prompt(z, ret_norm[0], batch, seq)

    ncp = max(LANE, -(-(seq // CMP_STRIDE) // LANE) * LANE)
    ac = _cmp_stage1_dense(zkv, w1ab, batch, seq)
    kcvc = _cmp_stage2(ac, pos8, w1a, w1b, w2, 1)
    if ncp > kcvc.shape[3]:
        kcvc = jnp.pad(kcvc, ((0, 0), (0, 0), (0, 0), (0, ncp - kcvc.shape[3]), (0, 0)))
    bias_d = _bias_by_dist(rel_table, BIAS_DISTS)
    a_nsa = _nsa_prompt(z, zkv, zb, kcvc, bias_d, batch, seq)

    hm = _rmsnorm(mem_prompt.reshape(batch * n_mem, D_MODEL), norm_mem[0], _pick_tile(batch * n_mem, 512))
    mem_kv_p = _proj(hm, w_mem_kv[0], _pick_tile(batch * n_mem, 512), PROJ_TN)
    a_mem = _mem_prompt(zb, mem_kv_p, batch, seq, _pick_tile(seq, 512))

    merged = _merge(a_ret, a_nsa, a_mem, w_ret, w_nsa, w_mem, zb, _pick_tile(m_p, 1024), 512)
    y_p = _out_proj(merged, w_o, xp, norm_post[0], _pick_tile(m_p, 512)).reshape(batch, seq, D_MODEL)

    kv_shape = (1, batch, seq, 2, NSA_KV_HEADS, NSA_DK)
    new_cmp_p = kvc_rows.reshape(kv_shape)
    new_sel_p = kvs_rows.reshape(kv_shape)
    new_win_p = kvw_rows.reshape(1, batch, WINDOW, 2, NSA_KV_HEADS, NSA_DK)
    new_ret_p = ret_state_p[None]
    new_mem_p = mem_kv_p.reshape(1, batch, n_mem, 2, MEM_HEADS, MEM_DH)

    zs = jnp.concatenate([zs_head, zs_kv], axis=1)
    z3 = zs.reshape(nb, 1, PROJ_A)
    z3b = zsb.reshape(nb, 1, PROJ_B)

    a_ret_s, ret_state_s = _retention_sample(z3, state_ret[0], ret_norm[0], past)

    cache_c = cache_cmp_kv[0].reshape(n_pool, PAGE_SIZE // CMP_STRIDE, CMP_STRIDE, 2 * NSA_KV_HEADS, NSA_DK)
    cache_s = cache_sel_kv[0].reshape(n_pool, PAGE_SIZE, 2 * NSA_KV_HEADS, NSA_DK)
    ac_s = _cmp_stage1_paged(cache_c, page_table, w1ab)
    kcvc_s = _cmp_stage2(ac_s, pos8, w1a, w1b, w2, NSA_KV_HEADS)
    ncs = past // CMP_STRIDE
    kcvc_s = kcvc_s.reshape(nb, 2, ncs, KV_W)
    ns_s = past // SEL_BLOCK + 1
    nsp_s = -(-ns_s // LANE) * LANE
    assert past >= WINDOW and past >= REL_MAX_DIST and BIAS_DISTS > WINDOW
    far_s = bias_d[..., REL_MAX_DIST:REL_MAX_DIST + 1]
    hg = (NSA_KV_HEADS, NSA_GROUP)
    n_valid = (past - (CMP_BLOCK - 1)) // CMP_STRIDE + 1
    strided = bias_d[..., (past - (CMP_BLOCK - 1)) % CMP_STRIDE::CMP_STRIDE]
    n_tab = strided.shape[-1]
    assert n_valid >= n_tab and n_tab * CMP_STRIDE > REL_MAX_DIST + CMP_STRIDE and ncs >= n_valid
    bias_cs = jnp.concatenate([jnp.broadcast_to(far_s, hg + (n_valid - n_tab,)), strided[..., ::-1],
                               jnp.full(hg + (ncs - n_valid,), NEG_INF, F32)], axis=-1)
    bias_cs = _pad_group_rows(bias_cs, 1)
    nblk = jnp.arange(ncs)[:, None]
    sblk = jnp.arange(nsp_s)[None, :]
    ov_s = ((nblk >= 4 * sblk - 1) & (nblk <= 4 * sblk + 3)).astype(BF16)
    o_cmp_s, idx_s = _nsa_sample_cmp(z3, kcvc_s, bias_cs, ov_s, ns_s)
    n_sel = min(SEL_TOPK, ns_s)
    idx = idx_s[:, :, 0, :n_sel]

    hg = (NSA_KV_HEADS, NSA_GROUP)
    n_key = ns_s * SEL_BLOCK
    bias_sel = jnp.concatenate([jnp.broadcast_to(far_s, hg + (past + 1 - REL_MAX_DIST,)),
                                bias_d[..., :REL_MAX_DIST][..., ::-1],
                                jnp.full(hg + (n_key - past - 1,), NEG_INF, F32)], axis=-1)
    bias_sel = _pad_group_rows(bias_sel.reshape(hg + (ns_s, SEL_BLOCK)).transpose(0, 2, 1, 3), 2)
    own_k = jnp.arange(2 * NSA_KV_HEADS)[None, :] == jnp.arange(NSA_KV_HEADS)[:, None]
    bias_sel = jnp.where(own_k[:, None, None, None, :], bias_sel[..., None], NEG_INF)
    bias_sel = bias_sel.reshape(NSA_KV_HEADS, ns_s, SROWS, SEL_BLOCK * 2 * NSA_KV_HEADS)
    kvs_new = zs[:, COL_KVS:COL_KVS + 2 * KV_W].reshape(nb, 2 * NSA_KV_HEADS, NSA_DK)

    win_buf = cache_win_kv[0].reshape(nb, WINDOW, 2 * NSA_KV_HEADS, NSA_DK)
    bias_w = jnp.concatenate([jnp.full(hg + (1,), NEG_INF, F32), bias_d[..., 1:WINDOW][..., ::-1]], axis=-1)
    bias_w = jnp.where(own_k[:, None, None, :], _pad_group_rows(bias_w, 1)[..., None], NEG_INF)
    bias_w = bias_w.reshape(NSA_KV_HEADS, SROWS, WINDOW * 2 * NSA_KV_HEADS)
    kvw_new = zs[:, COL_KVW:COL_KVW + 2 * KV_W].reshape(nb, 2 * NSA_KV_HEADS, NSA_DK)
    o_win_s = _nsa_sample_win(z3, win_buf, kvw_new, bias_w, _pad_group_rows(bias_d[..., 0:1], 1))
    a_nsa_s = _nsa_sample_sel(z3, cache_s, kvs_new, page_table, idx, bias_sel, ns_s, o_cmp_s, o_win_s, z3b)

    mem_kv_s = cache_mem_kv[0].reshape(nb, n_mem, 2 * MEM_W)
    a_mem_s = _mem_sample(z3b, mem_kv_s)

    merged_s = _merge(a_ret_s.reshape(nb, RET_W), a_nsa_s.reshape(nb, NSA_W), a_mem_s.reshape(nb, MEM_W),
                      w_ret, w_nsa, w_mem, zsb, nb, 512)
    y_s = _out_proj(merged_s, w_o, xs, norm_post[0], nb).reshape(nb, 1, D_MODEL)

    kvs_shape = (1, nb, 1, 2, NSA_KV_HEADS, NSA_DK)
    new_cmp_s = zs[:, COL_KVC:COL_KVC + 2 * KV_W].reshape(kvs_shape)
    new_sel_s = zs[:, COL_KVS:COL_KVS + 2 * KV_W].reshape(kvs_shape)
    kvw_s = zs[:, COL_KVW:COL_KVW + 2 * KV_W].reshape(nb, 1, 2, NSA_KV_HEADS, NSA_DK)
    new_win_s = jnp.concatenate([cache_win_kv[0][:, 1:], kvw_s], axis=1)[None]
    new_ret_s = ret_state_s[None]

    return (y_p, y_s, new_cmp_p, new_sel_p, new_win_p, new_ret_p, new_mem_p,
            new_cmp_s, new_sel_s, new_win_s, new_ret_s)
```

```python
import functools
import math

import jax
import jax.numpy as jnp
from jax import lax
from jax.experimental import pallas as pl
from jax.experimental.pallas import tpu as pltpu

F32 = jnp.float32
BF16 = jnp.bfloat16

D_MODEL = 2048
PAGE_SIZE = 128
RET_HEADS = 8
RET_DK = 256
RET_DV = 256
RET_CHUNK = 128
ROPE_BASE = 10000.0
NSA_HEADS = 16
NSA_KV_HEADS = 4
NSA_GROUP = NSA_HEADS // NSA_KV_HEADS
NSA_DK = 128
CMP_BLOCK = 32
CMP_STRIDE = 16
SEL_BLOCK = 64
SEL_TOPK = 16
WINDOW = 512
MEM_HEADS = 4
MEM_DH = 384
REL_BUCKETS = 32
REL_MAX_EXACT = 16
REL_MAX_DIST = 128
N_BRANCHES = 3
EPS = 1e-6
NEG_INF = -1e30
FORCE_SCORE = 1e4

RET_W = RET_HEADS * RET_DV
NSA_W = NSA_HEADS * NSA_DK
KV_W = NSA_KV_HEADS * NSA_DK
MEM_W = MEM_HEADS * MEM_DH

COL_RQ = 0
COL_RK = COL_RQ + RET_HEADS * RET_DK
COL_RV = COL_RK + RET_HEADS * RET_DK
COL_RG = COL_RV + RET_W
COL_NQ = COL_RG + RET_W
COL_KVC = COL_NQ + NSA_W
COL_KVS = COL_KVC + 2 * KV_W
COL_KVW = COL_KVS + 2 * KV_W
PROJ_A = COL_KVW + 2 * KV_W
COL_NSL = 0
COL_MG = COL_NSL + NSA_W
COL_MQ = COL_MG + N_BRANCHES * D_MODEL
COL_NG = COL_MQ + MEM_W
NG_SLOT = NSA_KV_HEADS * 128
PROJ_B = COL_NG + NG_SLOT
MQ_BLOCK = MEM_W + NG_SLOT

LOG2E = math.log2(math.e)
LANE = 128
TQ = 256
TK = 256
NSA_ROWS = NSA_GROUP * TQ
BIAS_DISTS = 1024
FAR_GROUP = 4
RET_STEP_CHUNKS = 16
PROJ_TN = 1024
VMEM_LIMIT = 56 * 1024 * 1024


def _cparams(n_axes):
    return pltpu.CompilerParams(dimension_semantics=("arbitrary",) * n_axes, vmem_limit_bytes=VMEM_LIMIT)


def _nt(a, b):
    return lax.dot_general(a, b, (((1,), (1,)), ((), ())), preferred_element_type=F32)


def _dot(a, b):
    return jnp.dot(a, b, preferred_element_type=F32)


def _sigmoid(x):
    return 1.0 / (1.0 + jnp.exp(-x))


def _iota(shape, dim):
    return lax.broadcasted_iota(jnp.int32, shape, dim)


def _rmsnorm_kernel(x_ref, g_ref, h_ref):
    x = x_ref[...]
    ms = jnp.mean(x * x, axis=-1, keepdims=True)
    h_ref[...] = ((x * lax.rsqrt(ms + EPS)) * g_ref[...]).astype(BF16)


def _rmsnorm(x, g, tm):
    m, k = x.shape
    return pl.pallas_call(
        _rmsnorm_kernel,
        grid=(m // tm,),
        in_specs=[pl.BlockSpec((tm, k), lambda i: (i, 0)), pl.BlockSpec((1, k), lambda i: (0, 0))],
        out_specs=pl.BlockSpec((tm, k), lambda i: (i, 0)),
        out_shape=jax.ShapeDtypeStruct((m, k), BF16),
        compiler_params=_cparams(1),
        name="rmsnorm",
    )(x, g.reshape(1, k))


def _proj_kernel(h_ref, w_ref, *refs, transposed):
    o_ref, wb_ref = refs[-3 if len(refs) == 4 else -2], refs[-1]

    @pl.when(pl.program_id(1) == 0)
    def _():
        if transposed:
            _store_transposed(wb_ref, w_ref, wb_ref.shape[1])
        else:
            wb_ref[...] = w_ref[...].astype(BF16)
        if len(refs) == 4:
            refs[2][...] = _dot(refs[0][...], wb_ref[...])

    o_ref[...] = _dot(h_ref[...], wb_ref[...])


def _rider_specs(rider, k, tn):
    if rider is None:
        return [], [], []
    rows = rider.shape[0]
    return ([pl.BlockSpec((rows, k), lambda j, i: (0, 0))], [pl.BlockSpec((rows, tn), lambda j, i: (0, j))], [rows])


def _proj(h, w, tm, tn, n=None, transposed=False, rider=None):
    m, k = h.shape
    n = w.shape[0 if transposed else 1] if n is None else n
    wspec = pl.BlockSpec((tn, k), lambda j, i: (j, 0)) if transposed else pl.BlockSpec((k, tn), lambda j, i: (0, j))
    r_in, r_out, r_rows = _rider_specs(rider, k, tn)
    out = pl.pallas_call(
        functools.partial(_proj_kernel, transposed=transposed),
        grid=(n // tn, m // tm),
        in_specs=[pl.BlockSpec((tm, k), lambda j, i: (i, 0)), wspec] + r_in,
        out_specs=[pl.BlockSpec((tm, tn), lambda j, i: (i, j))] + r_out,
        out_shape=[jax.ShapeDtypeStruct((m, n), F32)] + [jax.ShapeDtypeStruct((r, n), F32) for r in r_rows],
        scratch_shapes=[pltpu.VMEM((k, tn), BF16)],
        compiler_params=_cparams(2),
        name="proj",
    )(h, w, *([] if rider is None else [rider]))
    return out[0] if rider is None else tuple(out)


def _proj_kv_kernel(h_ref, w_ref, hs_ref, o_ref, oc_ref, os_ref, ow_ref, ors_ref, wb_ref, *, tiles_per_batch):
    j, i = pl.program_id(0), pl.program_id(1)
    n_tiles = pl.num_programs(0)

    @pl.when(i == 0)
    def _():
        _store_transposed(wb_ref, w_ref, wb_ref.shape[1])
        ors_ref[...] = _dot(hs_ref[...], wb_ref[...])

    res = _dot(h_ref[...], wb_ref[...])
    o_ref[...] = res
    tm, tn = res.shape
    rows_kv = tn // NSA_DK

    @pl.when(j == n_tiles - 3)
    def _():
        oc_ref[...] = res.reshape(tm * rows_kv, NSA_DK)

    @pl.when(j == n_tiles - 2)
    def _():
        os_ref[...] = res.reshape(tm * rows_kv, NSA_DK)

    @pl.when((j == n_tiles - 1) & (i % tiles_per_batch == tiles_per_batch - 1))
    def _():
        ow_ref[...] = res[tm - WINDOW:, :].reshape(WINDOW * rows_kv, NSA_DK)


def _proj_kv(h, w_t, tm, seq, rider):
    m, k = h.shape
    tn = 2 * KV_W
    assert COL_KVC % tn == 0 and COL_KVC == PROJ_A - 3 * tn and seq % tm == 0 and tm >= WINDOW
    n_j, n_i = 3, m // tm
    first = COL_KVC // tn
    tiles_per_batch = seq // tm
    rows_kv = tn // NSA_DK

    def kv_rows(tile):
        return lambda j, i: (jnp.where(j < tile, 0, jnp.where(j == tile, i, n_i - 1)), 0)

    r_in, r_out, (r_rows,) = _rider_specs(rider, k, tn)
    return pl.pallas_call(
        functools.partial(_proj_kv_kernel, tiles_per_batch=tiles_per_batch),
        grid=(n_j, n_i),
        in_specs=[pl.BlockSpec((tm, k), lambda j, i: (i, 0)),
                  pl.BlockSpec((tn, k), lambda j, i: (first + j, 0))] + r_in,
        out_specs=[pl.BlockSpec((tm, tn), lambda j, i: (i, j)),
                   pl.BlockSpec((tm * rows_kv, NSA_DK), kv_rows(n_j - 3)),
                   pl.BlockSpec((tm * rows_kv, NSA_DK), kv_rows(n_j - 2)),
                   pl.BlockSpec((WINDOW * rows_kv, NSA_DK),
                                lambda j, i: (jnp.where(j < n_j - 1, 0, i // tiles_per_batch), 0))] + r_out,
        out_shape=[jax.ShapeDtypeStruct((m, n_j * tn), F32),
                   jax.ShapeDtypeStruct((m * rows_kv, NSA_DK), F32),
                   jax.ShapeDtypeStruct((m * rows_kv, NSA_DK), F32),
                   jax.ShapeDtypeStruct((m // seq * WINDOW * rows_kv, NSA_DK), F32),
                   jax.ShapeDtypeStruct((r_rows, n_j * tn), F32)],
        scratch_shapes=[pltpu.VMEM((k, tn), BF16)],
        compiler_params=_cparams(2),
        name="proj_kv",
    )(h, w_t, rider)


TAIL_TN = 1024


def _store_transposed(dst_ref, src_ref, rows, chunk=256):
    for r0 in range(0, rows, chunk):
        dst_ref[:, r0:r0 + chunk] = src_ref[r0:r0 + chunk, :].T.astype(BF16)


def _proj_tail_kernel(h_ref, w_ref, wng_ref, hs_ref, o_ref, ors_ref, wb_ref, *, last_rows):
    tn = wb_ref.shape[1]
    is_last = pl.program_id(0) == pl.num_programs(0) - 1

    @pl.when((pl.program_id(1) == 0) & jnp.logical_not(is_last))
    def _():
        _store_transposed(wb_ref, w_ref, tn)
        ors_ref[...] = _dot(hs_ref[...], wb_ref[...])

    @pl.when((pl.program_id(1) == 0) & is_last)
    def _():
        _store_transposed(wb_ref, w_ref, last_rows)
        wb_ref[:, last_rows:] = wng_ref[...].T.astype(BF16)
        ors_ref[...] = _dot(hs_ref[...], wb_ref[...])

    o_ref[...] = _dot(h_ref[...], wb_ref[...])


def _proj_tail(h, w_t, w_ng, tm, rider):
    m, k = h.shape
    tn = TAIL_TN
    row_ng = PROJ_A
    row_nsl = row_ng + N_BRANCHES * NSA_HEADS
    row_mq = row_nsl + NSA_W
    row_mg = row_mq + MEM_W
    assert row_mg + N_BRANCHES * D_MODEL == w_t.shape[0] and w_ng.shape == (NG_SLOT, k)
    assert COL_NSL == 0 and COL_MG % tn == 0 and COL_MQ % tn == 0 and COL_NG + NG_SLOT == PROJ_B == COL_MQ + 2 * tn
    assert row_mq % 16 == 0 and row_nsl % 16 == 0 and row_mg % 16 == 0
    t_mg, t_mq = COL_MG // tn, COL_MQ // tn

    def w_row(j, i):
        row = jnp.where(j < t_mg, row_nsl + j * tn,
                        jnp.where(j < t_mq, row_mg + (j - t_mg) * tn, row_mq + (j - t_mq) * tn))
        return (pl.multiple_of(row, 16), 0)

    r_in, r_out, (r_rows,) = _rider_specs(rider, k, tn)
    return pl.pallas_call(
        functools.partial(_proj_tail_kernel, last_rows=COL_NG - COL_MQ - tn),
        grid=(PROJ_B // tn, m // tm),
        in_specs=[pl.BlockSpec((tm, k), lambda j, i: (i, 0)),
                  pl.BlockSpec((pl.Element(tn), pl.Element(k)), w_row),
                  pl.BlockSpec((NG_SLOT, k), lambda j, i: (0, 0))] + r_in,
        out_specs=[pl.BlockSpec((tm, tn), lambda j, i: (i, j))] + r_out,
        out_shape=[jax.ShapeDtypeStruct((m, PROJ_B), F32), jax.ShapeDtypeStruct((r_rows, PROJ_B), F32)],
        scratch_shapes=[pltpu.VMEM((k, tn), BF16)],
        compiler_params=_cparams(2),
        name="proj_tail",
    )(h, w_t, w_ng, rider)


def _rope_rows(x, cos, sin):
    half = x.shape[-1] // 2
    x1, x2 = x[:, :half], x[:, half:]
    return jnp.concatenate([x1 * cos - x2 * sin, x1 * sin + x2 * cos], axis=-1)


def _head_norm_gate(o, gnorm, rg):
    oc = o - jnp.mean(o, axis=-1, keepdims=True)
    y = oc * lax.rsqrt(jnp.mean(oc * oc, axis=-1, keepdims=True) + EPS) * gnorm
    return y * (rg * _sigmoid(rg))


def _ret_prompt_kernel(q_ref, k_ref, v_ref, rg_ref, cos_ref, sin_ref, dmat_ref, xi_ref, zeta_ref, gc_ref,
                       gn_ref, a_ref, s_ref):
    @pl.when(pl.program_id(2) == 0)
    def _():
        s_ref[...] = jnp.zeros_like(s_ref)

    c = RET_CHUNK
    for t in range(q_ref.shape[0] // c):
        rows = slice(t * c, (t + 1) * c)
        cos, sin = cos_ref[rows, :], sin_ref[rows, :]
        q = _rope_rows(q_ref[rows, :], cos, sin)
        k = _rope_rows(k_ref[rows, :], cos, sin) * (RET_DK ** -0.5)
        qb, vb = q.astype(BF16), v_ref[rows, :].astype(BF16)
        state = s_ref[0, 0]
        inner = _nt(qb, k.astype(BF16)) * dmat_ref[0]
        o = _dot(inner.astype(BF16), vb) + _dot(qb, state.astype(BF16)) * xi_ref[0]
        kz_t = (k * zeta_ref[0]).T.astype(BF16)
        s_ref[0, 0] = state * gc_ref[0] + _dot(kz_t, vb)
        a_ref[rows, :] = _head_norm_gate(o, gn_ref[...], rg_ref[rows, :]).astype(BF16)


def _decay_tables(chunk):
    log_g = jnp.log1p(-jnp.exp2(-5.0 - jnp.arange(RET_HEADS, dtype=F32)))
    i = jnp.arange(chunk, dtype=F32)
    diff = i[:, None] - i[None, :]
    dmat = jnp.where(diff >= 0, jnp.exp(log_g[:, None, None] * jnp.maximum(diff, 0.0)), 0.0)
    xi = jnp.exp(log_g[:, None] * (i[None, :] + 1.0))[:, :, None]
    zeta = jnp.exp(log_g[:, None] * (chunk - 1.0 - i[None, :]))[:, :, None]
    g_chunk = jnp.exp(log_g * chunk)[:, None, None]
    return dmat, xi, zeta, g_chunk


def _rope_tables(pos):
    half = RET_DK // 2
    freq = jnp.power(ROPE_BASE, -jnp.arange(half, dtype=F32) / half)
    ang = pos.astype(F32)[:, None] * freq[None, :]
    return jnp.cos(ang), jnp.sin(ang)


def _retention_prompt(z, ret_norm, batch, seq):
    c = RET_CHUNK
    rows = _pick_tile(seq, RET_STEP_CHUNKS * c)
    nc = seq // rows
    dmat, xi, zeta, g_chunk = _decay_tables(c)
    cos, sin = _rope_tables(jnp.arange(seq))
    hb = RET_DK

    def zspec(col0):
        return pl.BlockSpec((rows, hb), lambda b, h, t, col0=col0: (b * nc + t, col0 // hb + h))

    per_head = lambda shape: pl.BlockSpec((1,) + shape, lambda b, h, t: (h, 0, 0))
    return pl.pallas_call(
        _ret_prompt_kernel,
        grid=(batch, RET_HEADS, nc),
        in_specs=[zspec(COL_RQ), zspec(COL_RK), zspec(COL_RV), zspec(COL_RG),
                  pl.BlockSpec((rows, hb // 2), lambda b, h, t: (t, 0)),
                  pl.BlockSpec((rows, hb // 2), lambda b, h, t: (t, 0)),
                  per_head((c, c)), per_head((c, 1)), per_head((c, 1)), per_head((1, 1)),
                  pl.BlockSpec((1, hb), lambda b, h, t: (0, h))],
        out_specs=[pl.BlockSpec((rows, hb), lambda b, h, t: (b * nc + t, h)),
                   pl.BlockSpec((1, 1, RET_DK, RET_DV), lambda b, h, t: (b, h, 0, 0))],
        out_shape=[jax.ShapeDtypeStruct((batch * seq, RET_W), BF16),
                   jax.ShapeDtypeStruct((batch, RET_HEADS, RET_DK, RET_DV), F32)],
        compiler_params=_cparams(3),
        name="retention_prompt",
    )(z, z, z, z, cos, sin, dmat, xi, zeta, g_chunk, ret_norm.reshape(1, RET_W))


def _column_of(row):
    n = row.shape[1]
    eye = _iota((n, n), 0) == _iota((n, n), 1)
    return jnp.sum(jnp.where(eye, jnp.broadcast_to(row, (n, n)), 0.0), axis=1, keepdims=True)


def _ret_sample_kernel(q_ref, k_ref, v_ref, rg_ref, cos_ref, sin_ref, gam_ref, gn_ref, s_ref, a_ref, so_ref):
    cos, sin = cos_ref[...], sin_ref[...]
    outs = []
    for h in range(RET_HEADS):
        cols = slice(h * RET_DK, (h + 1) * RET_DK)
        q = _rope_rows(q_ref[0][:, cols], cos, sin)
        k = _rope_rows(k_ref[0][:, cols], cos, sin) * (RET_DK ** -0.5)
        v = v_ref[0][:, cols]
        state = s_ref[0, h]
        gamma = gam_ref[h]
        qk = jnp.sum(q * k, axis=-1, keepdims=True)
        o = qk * v + jnp.sum(_column_of(q) * state, axis=0, keepdims=True) * gamma
        so_ref[0, h] = state * gamma + _column_of(k) * v
        outs.append(_head_norm_gate(o, gn_ref[:, cols], rg_ref[0][:, cols]))
    a_ref[0] = jnp.concatenate(outs, axis=1).astype(BF16)


def _retention_sample(z3, state, ret_norm, pos):
    nb = z3.shape[0]
    cos, sin = _rope_tables(jnp.full((1,), pos))
    gamma = jnp.exp(jnp.log1p(-jnp.exp2(-5.0 - jnp.arange(RET_HEADS, dtype=F32))))[:, None, None]
    hb = RET_DK
    assert RET_HEADS * RET_DK == RET_W

    def zspec(col0):
        return pl.BlockSpec((1, 1, RET_W), lambda b, col0=col0: (b, 0, col0 // RET_W))

    st_spec = pl.BlockSpec((1, RET_HEADS, RET_DK, RET_DV), lambda b: (b, 0, 0, 0))
    return pl.pallas_call(
        _ret_sample_kernel,
        grid=(nb,),
        in_specs=[zspec(COL_RQ), zspec(COL_RK), zspec(COL_RV), zspec(COL_RG),
                  pl.BlockSpec((1, hb // 2), lambda b: (0, 0)),
                  pl.BlockSpec((1, hb // 2), lambda b: (0, 0)),
                  pl.BlockSpec((RET_HEADS, 1, 1), lambda b: (0, 0, 0)),
                  pl.BlockSpec((1, RET_W), lambda b: (0, 0)),
                  st_spec],
        out_specs=[pl.BlockSpec((1, 1, RET_W), lambda b: (b, 0, 0)), st_spec],
        out_shape=[jax.ShapeDtypeStruct((nb, 1, RET_W), BF16),
                   jax.ShapeDtypeStruct(state.shape, F32)],
        compiler_params=_cparams(1),
        name="retention_sample",
    )(z3, z3, z3, z3, cos, sin, gamma, ret_norm.reshape(1, RET_W), state)


def _half_rows(ref_slice_fn, n_half):
    return jnp.concatenate([ref_slice_fn(p) for p in range(CMP_STRIDE)], axis=1)


def _cmp_stage1_dense_kernel(x_ref, w_ref, o_ref):
    nh = o_ref.shape[3]
    x = _half_rows(lambda p: x_ref[pl.ds(p, nh, stride=CMP_STRIDE), :], nh).astype(BF16)
    o_ref[0, 0, 0] = _dot(x, w_ref[0])


def _cmp_stage1_dense(z, w1ab, batch, seq):
    nh = seq // CMP_STRIDE
    return pl.pallas_call(
        _cmp_stage1_dense_kernel,
        grid=(batch, 2, NSA_KV_HEADS),
        in_specs=[pl.BlockSpec((seq, NSA_DK), lambda b, kv, h: (b, kv * NSA_KV_HEADS + h)),
                  pl.BlockSpec((1, CMP_STRIDE * NSA_DK, 2 * NSA_DK), lambda b, kv, h: (kv, 0, 0))],
        out_specs=pl.BlockSpec((1, 1, 1, nh, 2 * NSA_DK), lambda b, kv, h: (b, kv, h, 0, 0)),
        out_shape=jax.ShapeDtypeStruct((batch, 2, NSA_KV_HEADS, nh, 2 * NSA_DK), F32),
        compiler_params=_cparams(3),
        name="cmp_stage1_dense",
    )(z, w1ab)


PAGES_PER_STEP = 16


def _cmp_stage1_paged_kernel(pt_ref, *refs):
    pages, (w_ref, o_ref) = refs[:PAGES_PER_STEP], refs[PAGES_PER_STEP:]
    hp = PAGE_SIZE // CMP_STRIDE
    top = _iota((2 * NSA_KV_HEADS, NSA_DK), 0) < NSA_KV_HEADS
    cols = [[], []]
    for p in range(CMP_STRIDE):
        tiles = [[], []]
        for pg in pages:
            xp = pg[0, :, p]
            for n in range(0, hp, 2):
                a, b = xp[n], xp[n + 1]
                tiles[0].append(jnp.where(top, a, pltpu.roll(b, NSA_KV_HEADS, 0)))
                tiles[1].append(jnp.where(top, pltpu.roll(a, NSA_KV_HEADS, 0), b))
        for kv in range(2):
            cols[kv].append(jnp.concatenate(tiles[kv], axis=0))
    for kv in range(2):
        x = jnp.concatenate(cols[kv], axis=1).astype(BF16)
        o_ref[0, kv, 0] = _dot(x, w_ref[kv])


def _cmp_stage1_paged(cache, page_table, w1ab):
    nb, n_pages = page_table.shape
    hp = PAGE_SIZE // CMP_STRIDE
    steps = n_pages // PAGES_PER_STEP
    rows = PAGES_PER_STEP * hp * NSA_KV_HEADS

    def page_spec(j):
        return pl.BlockSpec((1, hp, CMP_STRIDE, 2 * NSA_KV_HEADS, NSA_DK),
                            lambda b, s, pt, j=j: (pt[b, s * PAGES_PER_STEP + j], 0, 0, 0, 0))

    wspec = pl.BlockSpec((2, CMP_STRIDE * NSA_DK, 2 * NSA_DK), lambda b, s, pt: (0, 0, 0))
    return pl.pallas_call(
        _cmp_stage1_paged_kernel,
        grid_spec=pltpu.PrefetchScalarGridSpec(
            num_scalar_prefetch=1,
            grid=(nb, steps),
            in_specs=[page_spec(j) for j in range(PAGES_PER_STEP)] + [wspec],
            out_specs=pl.BlockSpec((1, 2, 1, rows, 2 * NSA_DK), lambda b, s, pt: (b, 0, 0, s, 0))),
        out_shape=jax.ShapeDtypeStruct((nb, 2, 1, steps * rows, 2 * NSA_DK), F32),
        compiler_params=_cparams(2),
        name="cmp_stage1_paged",
    )(page_table, *([cache] * PAGES_PER_STEP), w1ab)


def _cmp_stage2_kernel(ac_ref, pos_ref, w1a_ref, w1b_ref, w2_ref, o_ref, *, shift):
    ac = ac_ref[0, 0, 0]
    nh = ac.shape[0]
    pos = pos_ref[0].astype(BF16)
    kw = CMP_STRIDE * NSA_DK
    pe = _dot(pos[:, :kw], w1a_ref[0]) + _dot(pos[:, kw:], w1b_ref[0])
    pre = ac[:, :NSA_DK] + pltpu.roll(ac[:, NSA_DK:], nh - shift, 0) + pe[0:1]
    gelu = 0.5 * pre * (1.0 + jnp.tanh(math.sqrt(2.0 / math.pi) * (pre + 0.044715 * (pre * pre * pre))))
    o_ref[0, 0, 0] = _dot(gelu.astype(BF16), w2_ref[0]).astype(BF16)


def _cmp_stage2(ac, pos8, w1a, w1b, w2, shift):
    nb, _, groups, nh, _ = ac.shape
    kw = CMP_STRIDE * NSA_DK
    return pl.pallas_call(
        functools.partial(_cmp_stage2_kernel, shift=shift),
        grid=(nb, 2, groups),
        in_specs=[pl.BlockSpec((1, 1, 1, nh, 2 * NSA_DK), lambda b, kv, h: (b, kv, h, 0, 0)),
                  pl.BlockSpec((1, 8, 2 * kw), lambda b, kv, h: (kv, 0, 0)),
                  pl.BlockSpec((1, kw, NSA_DK), lambda b, kv, h: (kv, 0, 0)),
                  pl.BlockSpec((1, kw, NSA_DK), lambda b, kv, h: (kv, 0, 0)),
                  pl.BlockSpec((1, NSA_DK, NSA_DK), lambda b, kv, h: (kv, 0, 0))],
        out_specs=pl.BlockSpec((1, 1, 1, nh, NSA_DK), lambda b, kv, h: (b, kv, h, 0, 0)),
        out_shape=jax.ShapeDtypeStruct((nb, 2, groups, nh, NSA_DK), BF16),
        compiler_params=_cparams(3),
        name="cmp_stage2",
    )(ac, pos8, w1a, w1b, w2)


def _rel_bucket(dist):
    n = jnp.maximum(dist, 0)
    nf = jnp.maximum(n, 1).astype(F32)
    scale = (REL_BUCKETS - REL_MAX_EXACT) / math.log(REL_MAX_DIST / REL_MAX_EXACT)
    large = REL_MAX_EXACT + (jnp.log(nf / REL_MAX_EXACT) * scale).astype(jnp.int32)
    large = jnp.minimum(large, REL_BUCKETS - 1)
    return jnp.where(n < REL_MAX_EXACT, n, large)


def _bias_by_dist(rel_table, n):
    tab = rel_table[_rel_bucket(jnp.arange(n))]
    return tab.T.reshape(NSA_KV_HEADS, NSA_GROUP, n)


def _pad_group_rows(t, axis):
    first = lax.slice_in_dim(t, 0, 1, axis=axis)
    return jnp.concatenate([t] + [first] * (SROWS - NSA_GROUP), axis=axis)


def _flash_init(m_ref, l_ref, acc_ref):
    m_ref[...] = jnp.full(m_ref.shape, NEG_INF, F32)
    l_ref[...] = jnp.zeros(l_ref.shape, F32)
    acc_ref[...] = jnp.zeros(acc_ref.shape, F32)


def _flash_result(l_ref, acc_ref):
    return acc_ref[...] / jnp.maximum(l_ref[...], 1e-30)


def _select_blocks(imp_t, q0, ns):
    shape = imp_t.shape
    blk = _iota(shape, 0)
    qpos = q0 + _iota(shape, 1)
    cur = qpos >> 6
    valid = blk * SEL_BLOCK <= qpos
    forced = (blk == 0) | (blk == cur) | (blk == cur - 1)
    imp_t = jnp.where(valid, jnp.where(forced, FORCE_SCORE, imp_t), NEG_INF)
    rank = jnp.zeros(shape, F32)
    for other in range(ns):
        row = imp_t[other:other + 1, :]
        ahead = (row > imp_t) | ((row == imp_t) & (blk > other))
        rank = rank + jnp.where(ahead, 1.0, 0.0)
    return jnp.where((rank < SEL_TOPK) & valid, 1.0, 0.0)


def _nsa_prompt_kernel(q_ref, ks_ref, vs_ref, kw_ref, vw_ref, kc_ref, vct_ref, wd_ref, cfar_ref, basec_ref,
                       ovt_ref, ng_ref, nsl_ref, o_ref, m_ref, l_ref, acc_ref, vst_ref, vwt_ref, tz_ref, ow_ref, *, ns):
    i = pl.program_id(2)
    q0 = i * TQ
    qall = q_ref[...] * ((NSA_DK ** -0.5) * LOG2E)
    qt = jnp.concatenate([qall[:, g * NSA_DK:(g + 1) * NSA_DK].T for g in range(NSA_GROUP)], axis=1).astype(BF16)
    c_loc = _iota((TK, NSA_ROWS), 0)
    r_loc = _iota((TK, NSA_ROWS), 1) & (TQ - 1)

    @pl.when(i == 0)
    def _():
        for kt in range(vst_ref.shape[0]):
            vst_ref[kt] = vs_ref[kt * TK:(kt + 1) * TK, :].T.astype(BF16)
            vwt_ref[kt] = vw_ref[kt * TK:(kt + 1) * TK, :].T.astype(BF16)
        below = _iota((TK, TQ), 1) >= _iota((TK, TQ), 0)
        for g in range(NSA_GROUP):
            lo = pltpu.roll(jnp.broadcast_to(wd_ref[0, 2 * g:2 * g + 1, :], (TK, TQ)), 0, 1, stride=1, stride_axis=0)
            hi = pltpu.roll(jnp.broadcast_to(wd_ref[0, 2 * g + 1:2 * g + 2, :], (TK, TQ)), 0, 1, stride=1, stride_axis=0)
            tz_ref[0, :, g * TQ:(g + 1) * TQ] = lo
            tz_ref[1, :, g * TQ:(g + 1) * TQ] = jnp.where(below, hi, lo)

    ncp = kc_ref.shape[3]
    shift = (TQ // CMP_STRIDE) * i
    bias_c = basec_ref[0, pl.ds(pl.multiple_of(ncp - shift, TQ // CMP_STRIDE), ncp), :]
    s = _dot(kc_ref[0, 0, 0], qt) + bias_c
    m = jnp.max(s, axis=0, keepdims=True)
    e = jnp.exp2(s - m)
    inv = jnp.where(m > 0.5 * NEG_INF, 1.0 / jnp.maximum(jnp.sum(e, axis=0, keepdims=True), 1e-30), 0.0)
    p = e * inv
    o_cmp = _dot(vct_ref[0, 0], p.astype(BF16))

    psum = p[:, 0:TQ] + p[:, TQ:2 * TQ] + p[:, 2 * TQ:3 * TQ] + p[:, 3 * TQ:4 * TQ]
    hi = psum.astype(BF16)
    lo = (psum - hi.astype(F32)).astype(BF16)
    ovt = ovt_ref[...]
    imp_t = _dot(ovt, hi) + _dot(ovt, lo)
    ns8 = -(-ns // 8) * 8
    sel_t = _select_blocks(imp_t[:ns8], q0, ns)
    assert ns8 + 2 <= LANE
    sel_neg = jnp.concatenate([jnp.where(sel_t > 0.5, 0.0, NEG_INF)] * NSA_GROUP, axis=1)
    cfar = cfar_ref[0]
    cfar_hi = cfar.astype(BF16).astype(F32)
    sub8 = _iota((8, NSA_ROWS), 0)
    cfar8 = jnp.where(sub8 == 0, cfar_hi, jnp.where(sub8 == 1, cfar - cfar_hi, 0.0))
    pad_rows = jnp.zeros((LANE - ns8 - 8, NSA_ROWS), F32)
    qx_far = jnp.concatenate([qt, jnp.concatenate([sel_neg, cfar8, pad_rows], axis=0).astype(BF16)], axis=0)
    qx_near = jnp.concatenate([qt, jnp.concatenate([sel_neg, jnp.zeros_like(cfar8), pad_rows], axis=0).astype(BF16)],
                              axis=0)
    lane_k = _iota((TK, LANE), 1)
    ones_k = (lane_k == ns8) | (lane_k == ns8 + 1)

    def flash_step(scs, vts):
        m_old = m_ref[...]
        m_new = m_old
        for sc in scs:
            m_new = jnp.maximum(m_new, jnp.max(sc, axis=0, keepdims=True))
        alpha = jnp.exp2(m_old - m_new)
        l_new = alpha * l_ref[...]
        acc = alpha * acc_ref[...]
        for sc, vt in zip(scs, vts):
            pt = jnp.exp2(sc - m_new)
            l_new = l_new + jnp.sum(pt, axis=0, keepdims=True)
            acc = acc + _dot(vt, pt.astype(BF16))
        l_ref[...] = l_new
        acc_ref[...] = acc
        m_ref[...] = m_new

    def sel_scores(kt, bias, causal):
        k = ks_ref[pl.ds(pl.multiple_of(kt * TK, TK), TK), :].astype(BF16)
        blk_of_key = (TK // SEL_BLOCK) * kt + (_iota((TK, LANE), 0) >> 6)
        extra = jnp.where((lane_k == blk_of_key) | ones_k, 1.0, 0.0).astype(BF16)
        kx = jnp.concatenate([k, extra], axis=1)
        sc = _dot(kx, qx_far) if bias is None else _dot(kx, qx_near) + bias
        if causal:
            sc = jnp.where(c_loc <= r_loc, sc, NEG_INF)
        return sc

    def softmax_tiles(scs, vts):
        m_new = jnp.max(scs[0], axis=0, keepdims=True)
        for sc in scs[1:]:
            m_new = jnp.maximum(m_new, jnp.max(sc, axis=0, keepdims=True))
        l_new, acc = None, None
        for sc, vt in zip(scs, vts):
            pt = jnp.exp2(sc - m_new)
            l_t, acc_t = jnp.sum(pt, axis=0, keepdims=True), _dot(vt, pt.astype(BF16))
            l_new, acc = (l_t, acc_t) if l_new is None else (l_new + l_t, acc + acc_t)
        return m_new, l_new, acc

    kt_prev = jnp.maximum(i - 1, 0)
    prev_scores = jnp.where(i >= 1, sel_scores(kt_prev, tz_ref[1], False), NEG_INF)
    m_ref[...], l_ref[...], acc_ref[...] = softmax_tiles([sel_scores(i, tz_ref[0], True), prev_scores],
                                                         [vst_ref[i], vst_ref[kt_prev]])

    def win_scores(off):
        kt = jnp.maximum(i - off, 0)
        k = kw_ref[pl.ds(pl.multiple_of(kt * TK, TK), TK), :].astype(BF16)
        if off < 2:
            sc = _dot(k, qt) + tz_ref[off]
        else:
            sc = _dot(jnp.concatenate([k, jnp.where(ones_k, 1.0, 0.0).astype(BF16)], axis=1), qx_far)
        if off == 0:
            return jnp.where(c_loc <= r_loc, sc, NEG_INF)
        keep = (c_loc > r_loc) & (i >= off) if off * TK == WINDOW else (i >= off)
        return jnp.where(keep, sc, NEG_INF)

    n_win = WINDOW // TK + 1
    _, l_win, acc_win = softmax_tiles([win_scores(off) for off in range(n_win)],
                                      [vwt_ref[jnp.maximum(i - off, 0)] for off in range(n_win)])
    ow_ref[...] = acc_win / jnp.maximum(l_win, 1e-30)

    def far_tiles(kt_first, count):
        kts = [kt_first - t for t in range(count)]
        flash_step([sel_scores(kt, None, False) for kt in kts], [vst_ref[kt] for kt in kts])

    def far_group(j, carry):
        far_tiles(i - 2 - FAR_GROUP * j, FAR_GROUP)
        return carry

    n_far = jnp.maximum(i - 1, 0)
    lax.fori_loop(0, n_far // FAR_GROUP, far_group, 0)
    for rest in range(1, FAR_GROUP):
        pl.when(n_far % FAR_GROUP == rest)(functools.partial(far_tiles, rest - 1, rest))

    o_sel = _flash_result(l_ref, acc_ref)
    o_win = ow_ref[...]

    gates_t = _sigmoid(ng_ref[...]).T
    nsl = nsl_ref[...]
    outs = []
    for g in range(NSA_GROUP):
        cols = slice(g * TQ, (g + 1) * TQ)
        o_t = (gates_t[3 * g:3 * g + 1] * o_cmp[:, cols] + gates_t[3 * g + 1:3 * g + 2] * o_sel[:, cols]
               + gates_t[3 * g + 2:3 * g + 3] * o_win[:, cols])
        x = nsl[:, g * NSA_DK:(g + 1) * NSA_DK]
        outs.append(o_t.T * (x * _sigmoid(x)))
    o_ref[...] = jnp.concatenate(outs, axis=1).astype(BF16)


def _lanes_by_head(t):
    hk, g, rows, tq = t.shape
    return t.transpose(0, 2, 1, 3).reshape(hk, rows, g * tq)


def _nsa_prompt(z, zkv, zb, kcvc, bias_d, batch, seq):
    nq = seq // TQ
    ns = seq // SEL_BLOCK
    ncp = kcvc.shape[3]
    nsp = LANE
    wn = TQ // CMP_STRIDE
    assert TQ == TK and ns <= nsp and ncp >= seq // CMP_STRIDE and ncp % LANE == 0 and ncp > wn
    gw = NSA_GROUP * NSA_DK
    n_dist = bias_d.shape[-1]
    assert n_dist >= 2 * TK + TQ
    bias_d = bias_d * LOG2E

    tz = bias_d[..., :2 * TQ].reshape(NSA_KV_HEADS, 2 * NSA_GROUP, TQ)
    far = jnp.broadcast_to(bias_d[..., REL_MAX_DIST][:, :, None, None], (NSA_KV_HEADS, NSA_GROUP, 1, TQ))
    cfar = _lanes_by_head(far)
    half = n_dist // 2
    start = CMP_STRIDE * wn - (CMP_BLOCK - 1)
    assert start + TQ <= half and 2 * wn * CMP_STRIDE - start <= half
    w_ext = jnp.concatenate([bias_d[..., :half], jnp.full(bias_d.shape[:-1] + (n_dist - half,), NEG_INF, F32)], -1)
    near = jnp.tile(w_ext, (1, 1, 2 * wn + 1))[..., :2 * wn * (n_dist - CMP_STRIDE)]
    near = near.reshape(NSA_KV_HEADS, NSA_GROUP, 2 * wn, n_dist - CMP_STRIDE)[..., start:start + TQ]
    basec = jnp.concatenate([jnp.broadcast_to(far, (NSA_KV_HEADS, NSA_GROUP, ncp - wn, TQ)), near,
                             jnp.full((NSA_KV_HEADS, NSA_GROUP, ncp - wn, TQ), NEG_INF, F32)], axis=2)
    basec = _lanes_by_head(basec)
    sblk = jnp.arange(nsp)[:, None]
    nblk = jnp.arange(ncp)[None, :]
    ovt = ((nblk >= 4 * sblk - 1) & (nblk <= 4 * sblk + 3)).astype(BF16)

    vct = kcvc[:, 1].transpose(0, 1, 3, 2)

    def kvspec(col0, which):
        return pl.BlockSpec((seq, NSA_DK),
                            lambda b, h, i: (b, (col0 - COL_KVC) // NSA_DK + which * NSA_KV_HEADS + h))

    vt_scratch = pltpu.VMEM((seq // TK, NSA_DK, TK), BF16)
    return pl.pallas_call(
        functools.partial(_nsa_prompt_kernel, ns=ns),
        grid=(batch, NSA_KV_HEADS, nq),
        in_specs=[pl.BlockSpec((TQ, gw), lambda b, h, i: (b * nq + i, COL_NQ // gw + h)),
                  kvspec(COL_KVS, 0), kvspec(COL_KVS, 1), kvspec(COL_KVW, 0), kvspec(COL_KVW, 1),
                  pl.BlockSpec((1, 1, 1, ncp, NSA_DK), lambda b, h, i: (b, 0, h, 0, 0)),
                  pl.BlockSpec((1, 1, NSA_DK, ncp), lambda b, h, i: (b, h, 0, 0)),
                  pl.BlockSpec((1, 2 * NSA_GROUP, TQ), lambda b, h, i: (h, 0, 0)),
                  pl.BlockSpec((1, 1, NSA_ROWS), lambda b, h, i: (h, 0, 0)),
                  pl.BlockSpec((1, 2 * ncp, NSA_ROWS), lambda b, h, i: (h, 0, 0)),
                  pl.BlockSpec((nsp, ncp), lambda b, h, i: (0, 0)),
                  pl.BlockSpec((TQ, LANE), lambda b, h, i: (b * nq + i, COL_NG // LANE + h)),
                  pl.BlockSpec((TQ, gw), lambda b, h, i: (b * nq + i, COL_NSL // gw + h))],
        out_specs=pl.BlockSpec((TQ, gw), lambda b, h, i: (b * nq + i, h)),
        out_shape=jax.ShapeDtypeStruct((batch * seq, NSA_W), BF16),
        scratch_shapes=[pltpu.VMEM((1, NSA_ROWS), F32), pltpu.VMEM((1, NSA_ROWS), F32),
                        pltpu.VMEM((NSA_DK, NSA_ROWS), F32), vt_scratch, vt_scratch,
                        pltpu.VMEM((2, TK, NSA_ROWS), F32), pltpu.VMEM((NSA_DK, NSA_ROWS), F32)],
        compiler_params=_cparams(3),
        name="nsa_prompt",
    )(z, zkv, zkv, zkv, zkv, kcvc, vct, tz, cfar, basec, ovt, zb, zb)


SROWS = 8
SEL_PER_STEP = 8


def _stack_group_q(q_row):
    heads = [q_row[:, g * NSA_DK:(g + 1) * NSA_DK] for g in range(NSA_GROUP)]
    return jnp.concatenate(heads + [heads[0]] * (SROWS - NSA_GROUP), axis=0)


def _nsa_sample_cmp_kernel(q_ref, kc_ref, vc_ref, bias_ref, ov_ref, o_ref, idx_ref, *, ns):
    scale = NSA_DK ** -0.5
    qs = _stack_group_q(q_ref[0]).astype(BF16)
    s = _nt(qs, kc_ref[0, 0]) * scale + bias_ref[0]
    m = jnp.max(s, axis=1, keepdims=True)
    e = jnp.exp(s - m)
    inv = jnp.where(m > 0.5 * NEG_INF, 1.0 / jnp.maximum(jnp.sum(e, axis=1, keepdims=True), 1e-30), 0.0)
    p = e * inv
    o_ref[0, 0] = _dot(p.astype(BF16), vc_ref[0, 0])
    psum = jnp.broadcast_to(jnp.sum(p[0:NSA_GROUP], axis=0, keepdims=True), p.shape)
    hi = psum.astype(BF16)
    lo = (psum - hi.astype(F32)).astype(BF16)
    imp = (_dot(hi, ov_ref[...]) + _dot(lo, ov_ref[...]))[0:1]
    nsp = imp.shape[1]
    blk_r = _iota((1, nsp), 1)
    cur = ns - 1
    forced = (blk_r == 0) | (blk_r == cur) | (blk_r == cur - 1)
    imp = jnp.where(blk_r < ns, jnp.where(forced, FORCE_SCORE, imp), 2.0 * NEG_INF)
    imp_c = _column_of(imp)
    i_r = _iota((nsp, nsp), 1)
    j_c = _iota((nsp, nsp), 0)
    ahead = (imp > imp_c) | ((imp == imp_c) & (i_r < j_c))
    rank_c = jnp.sum(jnp.where(ahead, 1.0, 0.0), axis=1, keepdims=True)
    slot = _iota((nsp, LANE), 1).astype(F32)
    picks = jnp.where(rank_c == slot, _iota((nsp, LANE), 0).astype(F32), 0.0)
    idx_ref[0, 0] = jnp.broadcast_to(jnp.sum(picks, axis=0, keepdims=True), (SROWS, LANE)).astype(jnp.int32)


def _nsa_sample_cmp(z3, kcvc, bias_c, ov, ns):
    nb = z3.shape[0]
    ncp = kcvc.shape[2]
    nsp = ov.shape[1]
    gw = NSA_GROUP * NSA_DK

    def cspec(which):
        return pl.BlockSpec((1, 1, ncp, NSA_DK), lambda b, h: (b, which, 0, h))

    return pl.pallas_call(
        functools.partial(_nsa_sample_cmp_kernel, ns=ns),
        grid=(nb, NSA_KV_HEADS),
        in_specs=[pl.BlockSpec((1, 1, gw), lambda b, h: (b, 0, COL_NQ // gw + h)),
                  cspec(0), cspec(1),
                  pl.BlockSpec((1, SROWS, ncp), lambda b, h: (h, 0, 0)),
                  pl.BlockSpec((ncp, nsp), lambda b, h: (0, 0))],
        out_specs=[pl.BlockSpec((1, 1, SROWS, NSA_DK), lambda b, h: (b, h, 0, 0)),
                   pl.BlockSpec((1, 1, SROWS, LANE), lambda b, h: (b, h, 0, 0))],
        out_shape=[jax.ShapeDtypeStruct((nb, NSA_KV_HEADS, SROWS, NSA_DK), F32),
                   jax.ShapeDtypeStruct((nb, NSA_KV_HEADS, SROWS, LANE), jnp.int32)],
        compiler_params=_cparams(2),
        name="nsa_sample_cmp",
    )(z3, kcvc, kcvc, bias_c, ov)


def _nsa_sample_sel_kernel(pt_ref, idx_ref, q_ref, *refs, ns):
    n_blk = NSA_KV_HEADS * SEL_PER_STEP
    blocks, (new_ref,), biases = refs[:n_blk], refs[n_blk:n_blk + 1], refs[n_blk + 1:2 * n_blk + 1]
    oc_ref, ow_ref, ng_ref, nsl_ref, o_ref, m_ref, l_ref, acc_ref = refs[2 * n_blk + 1:]
    b, t = pl.program_id(0), pl.program_id(1)
    gw = NSA_GROUP * NSA_DK
    rows_kv = 2 * NSA_KV_HEADS
    width = SEL_BLOCK * rows_kv

    @pl.when(t == 0)
    def _():
        _flash_init(m_ref, l_ref, acc_ref)

    new_rows = jnp.concatenate([new_ref[0]] * SEL_BLOCK, axis=0)
    xs, scs = [], []
    for h in range(NSA_KV_HEADS):
        qs = _stack_group_q(q_ref[0][:, h * gw:(h + 1) * gw]).astype(BF16)
        row = []
        for u in range(SEL_PER_STEP):
            r = h * SEL_PER_STEP + u
            is_new = idx_ref[b, h, t * SEL_PER_STEP + u] == ns - 1
            x = jnp.where(is_new, new_rows, blocks[r][0].reshape(width, NSA_DK)).astype(BF16)
            xs.append(x)
            row.append(_nt(qs, x) * (NSA_DK ** -0.5) + biases[r][0, 0])
        scs.append(jnp.concatenate(row, axis=1))
    sc = jnp.concatenate(scs, axis=0)
    m_old = m_ref[...]
    m_new = jnp.maximum(m_old, jnp.max(sc, axis=1, keepdims=True))
    alpha = jnp.exp(m_old - m_new)
    p = jnp.exp(sc - m_new)
    l_ref[...] = alpha * l_ref[...] + jnp.sum(p, axis=1, keepdims=True)
    pv = pltpu.roll(p, NSA_KV_HEADS, 1).astype(BF16)
    acc = alpha * acc_ref[...]
    upd = []
    for h in range(NSA_KV_HEADS):
        ph = pv[h * SROWS:(h + 1) * SROWS]
        upd.append(sum(_dot(ph[:, u * width:(u + 1) * width], xs[h * SEL_PER_STEP + u]) for u in range(SEL_PER_STEP)))
    acc_ref[...] = acc + jnp.concatenate(upd, axis=0)
    m_ref[...] = m_new

    @pl.when(t == pl.num_programs(1) - 1)
    def _():
        o_sel = acc_ref[...] / jnp.maximum(l_ref[...], 1e-30)
        gates = _sigmoid(ng_ref[0])
        nsl = nsl_ref[0]
        outs = []
        for h in range(NSA_KV_HEADS):
            for g in range(NSA_GROUP):
                c = h * LANE + N_BRANCHES * g
                r = h * SROWS + g
                o = (gates[:, c:c + 1] * oc_ref[0, h, g:g + 1] + gates[:, c + 1:c + 2] * o_sel[r:r + 1]
                     + gates[:, c + 2:c + 3] * ow_ref[0, h, g:g + 1])
                x = nsl[:, (h * NSA_GROUP + g) * NSA_DK:(h * NSA_GROUP + g + 1) * NSA_DK]
                outs.append(o * (x * _sigmoid(x)))
        o_ref[0] = jnp.concatenate(outs, axis=1).astype(BF16)


def _nsa_sample_sel(z3, cache, kv_new, page_table, idx, bias_sel, ns, o_cmp, o_win, z3b):
    nb, n_pages = page_table.shape
    branch_spec = pl.BlockSpec((1, NSA_KV_HEADS, SROWS, NSA_DK), lambda b, t, pt, ix: (b, 0, 0, 0))
    n_sel = idx.shape[2]
    halves = PAGE_SIZE // SEL_BLOCK
    rows_kv = 2 * NSA_KV_HEADS

    per_step = SEL_PER_STEP
    assert n_sel % per_step == 0

    def blockspec(h, u):
        def index(b, t, pt, ix):
            blk = ix[b, h, t * per_step + u]
            return (pt[b, jnp.minimum(blk // halves, n_pages - 1)], blk % halves, 0, 0)
        return pl.BlockSpec((1, SEL_BLOCK, rows_kv, NSA_DK), index)

    def biasspec(h, u):
        return pl.BlockSpec((1, 1, SROWS, SEL_BLOCK * rows_kv),
                            lambda b, t, pt, ix: (h, ix[b, h, t * per_step + u], 0, 0))

    slots = [(h, u) for h in range(NSA_KV_HEADS) for u in range(per_step)]
    return pl.pallas_call(
        functools.partial(_nsa_sample_sel_kernel, ns=ns),
        grid_spec=pltpu.PrefetchScalarGridSpec(
            num_scalar_prefetch=2,
            grid=(nb, n_sel // per_step),
            in_specs=[pl.BlockSpec((1, 1, NSA_W), lambda b, t, pt, ix: (b, 0, COL_NQ // NSA_W))]
                     + [blockspec(h, u) for h, u in slots]
                     + [pl.BlockSpec((1, rows_kv, NSA_DK), lambda b, t, pt, ix: (b, 0, 0))]
                     + [biasspec(h, u) for h, u in slots]
                     + [branch_spec, branch_spec,
                        pl.BlockSpec((1, 1, NG_SLOT), lambda b, t, pt, ix: (b, 0, COL_NG // NG_SLOT)),
                        pl.BlockSpec((1, 1, NSA_W), lambda b, t, pt, ix: (b, 0, COL_NSL // NSA_W))],
            out_specs=pl.BlockSpec((1, 1, NSA_W), lambda b, t, pt, ix: (b, 0, 0)),
            scratch_shapes=[pltpu.VMEM((NSA_KV_HEADS * SROWS, 1), F32), pltpu.VMEM((NSA_KV_HEADS * SROWS, 1), F32),
                            pltpu.VMEM((NSA_KV_HEADS * SROWS, NSA_DK), F32)]),
        out_shape=jax.ShapeDtypeStruct((nb, 1, NSA_W), BF16),
        compiler_params=_cparams(2),
        name="nsa_sample_sel",
    )(page_table, idx, z3, *([cache] * len(slots)), kv_new, *([bias_sel] * len(slots)), o_cmp, o_win, z3b, z3b)


def _nsa_sample_win_kernel(q_ref, buf_ref, new_ref, bias_ref, bnew_ref, o_ref):
    scale = NSA_DK ** -0.5
    gw = NSA_GROUP * NSA_DK
    rows_kv = 2 * NSA_KV_HEADS
    x = buf_ref[0].reshape(buf_ref.shape[1] * rows_kv, NSA_DK).astype(BF16)
    new = new_ref[0]
    for h in range(NSA_KV_HEADS):
        q = _stack_group_q(q_ref[0][:, h * gw:(h + 1) * gw])
        s_buf = _nt(q.astype(BF16), x) * scale + bias_ref[h]
        s_new = jnp.sum(q * new[h:h + 1], axis=1, keepdims=True) * scale + bnew_ref[h]
        m = jnp.maximum(jnp.max(s_buf, axis=1, keepdims=True), s_new)
        p_buf = jnp.exp(s_buf - m)
        p_new = jnp.exp(s_new - m)
        l = jnp.sum(p_buf, axis=1, keepdims=True) + p_new
        acc = _dot(pltpu.roll(p_buf, NSA_KV_HEADS, 1).astype(BF16), x) + p_new * new[NSA_KV_HEADS + h:NSA_KV_HEADS + h + 1]
        o_ref[0, h] = acc / jnp.maximum(l, 1e-30)


def _nsa_sample_win(z3, win_buf, kv_new, bias_win, bias_new):
    nb, nbuf, rows_kv, _ = win_buf.shape
    return pl.pallas_call(
        _nsa_sample_win_kernel,
        grid=(nb,),
        in_specs=[pl.BlockSpec((1, 1, NSA_W), lambda b: (b, 0, COL_NQ // NSA_W)),
                  pl.BlockSpec((1, nbuf, rows_kv, NSA_DK), lambda b: (b, 0, 0, 0)),
                  pl.BlockSpec((1, rows_kv, NSA_DK), lambda b: (b, 0, 0)),
                  pl.BlockSpec((NSA_KV_HEADS, SROWS, nbuf * rows_kv), lambda b: (0, 0, 0)),
                  pl.BlockSpec((NSA_KV_HEADS, SROWS, 1), lambda b: (0, 0, 0))],
        out_specs=pl.BlockSpec((1, NSA_KV_HEADS, SROWS, NSA_DK), lambda b: (b, 0, 0, 0)),
        out_shape=jax.ShapeDtypeStruct((nb, NSA_KV_HEADS, SROWS, NSA_DK), F32),
        compiler_params=_cparams(1),
        name="nsa_sample_win",
    )(z3, win_buf, kv_new, bias_win, bias_new)


def _mem_heads(q, kv):
    outs = []
    for h in range(MEM_HEADS):
        k = kv[:, h * MEM_DH:(h + 1) * MEM_DH].astype(BF16)
        v = kv[:, MEM_W + h * MEM_DH:MEM_W + (h + 1) * MEM_DH].astype(BF16)
        s = _nt(q[:, h * MEM_DH:(h + 1) * MEM_DH].astype(BF16), k) * (MEM_DH ** -0.5)
        e = jnp.exp(s - jnp.max(s, axis=1, keepdims=True))
        p = e / jnp.sum(e, axis=1, keepdims=True)
        outs.append(_dot(p.astype(BF16), v))
    return jnp.concatenate(outs, axis=1)


def _mem_prompt_kernel(q_ref, kv_ref, o_ref):
    o_ref[...] = _mem_heads(q_ref[:, :MEM_W], kv_ref[...]).astype(BF16)


def _mem_prompt(z, mem_kv, batch, seq, tq):
    nq = seq // tq
    n_mem = mem_kv.shape[0] // batch
    return pl.pallas_call(
        _mem_prompt_kernel,
        grid=(batch, nq),
        in_specs=[pl.BlockSpec((tq, MQ_BLOCK), lambda b, i: (b * nq + i, COL_MQ // MQ_BLOCK)),
                  pl.BlockSpec((n_mem, 2 * MEM_W), lambda b, i: (b, 0))],
        out_specs=pl.BlockSpec((tq, MEM_W), lambda b, i: (b * nq + i, 0)),
        out_shape=jax.ShapeDtypeStruct((batch * seq, MEM_W), BF16),
        compiler_params=_cparams(2),
        name="mem_prompt",
    )(z, mem_kv)


def _mem_sample_kernel(q_ref, kv_ref, o_ref):
    q = jnp.broadcast_to(q_ref[0][:, :MEM_W], (SROWS, MEM_W))
    o_ref[0] = _mem_heads(q, kv_ref[0])[0:1].astype(BF16)


def _mem_sample(z3, mem_kv):
    nb, n_mem, _ = mem_kv.shape
    return pl.pallas_call(
        _mem_sample_kernel,
        grid=(nb,),
        in_specs=[pl.BlockSpec((1, 1, MQ_BLOCK), lambda b: (b, 0, COL_MQ // MQ_BLOCK)),
                  pl.BlockSpec((1, n_mem, 2 * MEM_W), lambda b: (b, 0, 0))],
        out_specs=pl.BlockSpec((1, 1, MEM_W), lambda b: (b, 0, 0)),
        out_shape=jax.ShapeDtypeStruct((nb, 1, MEM_W), BF16),
        compiler_params=_cparams(1),
        name="mem_sample",
    )(z3, mem_kv)


def _merge_kernel(ar_ref, an_ref, am_ref, wr_ref, wn_ref, wm_ref, g0_ref, g1_ref, g2_ref, o_ref):
    merged = (_sigmoid(g0_ref[...]) * _dot(ar_ref[...], wr_ref[...])
              + _sigmoid(g1_ref[...]) * _dot(an_ref[...], wn_ref[...])
              + _sigmoid(g2_ref[...]) * _dot(am_ref[...], wm_ref[...]))
    o_ref[...] = merged.astype(BF16)


def _merge(a_ret, a_nsa, a_mem, w_ret, w_nsa, w_mem, z, tm, tn):
    m = a_ret.shape[0]
    nt = D_MODEL // tn

    def aspec(width):
        return pl.BlockSpec((tm, width), lambda i, j: (i, 0))

    def wspec(width):
        return pl.BlockSpec((width, tn), lambda i, j: (0, j))

    def gspec(branch):
        return pl.BlockSpec((tm, tn), lambda i, j: (i, COL_MG // tn + branch * nt + j))

    return pl.pallas_call(
        _merge_kernel,
        grid=(m // tm, nt),
        in_specs=[aspec(RET_W), aspec(NSA_W), aspec(MEM_W), wspec(RET_W), wspec(NSA_W), wspec(MEM_W),
                  gspec(0), gspec(1), gspec(2)],
        out_specs=pl.BlockSpec((tm, tn), lambda i, j: (i, j)),
        out_shape=jax.ShapeDtypeStruct((m, D_MODEL), BF16),
        compiler_params=_cparams(2),
        name="merge",
    )(a_ret, a_nsa, a_mem, w_ret, w_nsa, w_mem, z, z, z)


def _out_kernel(a_ref, w_ref, x_ref, g_ref, o_ref):
    out = _dot(a_ref[...], w_ref[...])
    y = out * lax.rsqrt(jnp.mean(out * out, axis=-1, keepdims=True) + EPS)
    o_ref[...] = x_ref[...] + y * g_ref[...]


def _out_proj(merged, w_out, x, norm_post, tm):
    m = merged.shape[0]
    return pl.pallas_call(
        _out_kernel,
        grid=(m // tm,),
        in_specs=[pl.BlockSpec((tm, D_MODEL), lambda i: (i, 0)),
                  pl.BlockSpec((D_MODEL, D_MODEL), lambda i: (0, 0)),
                  pl.BlockSpec((tm, D_MODEL), lambda i: (i, 0)),
                  pl.BlockSpec((1, D_MODEL), lambda i: (0, 0))],
        out_specs=pl.BlockSpec((tm, D_MODEL), lambda i: (i, 0)),
        out_shape=jax.ShapeDtypeStruct((m, D_MODEL), F32),
        compiler_params=_cparams(1),
        name="out_proj",
    )(merged, w_out, x, norm_post.reshape(1, D_MODEL))


def _layout_w_ng(w_t):
    per_group = N_BRANCHES * NSA_GROUP
    ng = w_t[PROJ_A:PROJ_A + N_BRANCHES * NSA_HEADS].reshape(NSA_KV_HEADS, per_group, D_MODEL)
    return jnp.pad(ng, ((0, 0), (0, LANE - per_group), (0, 0))).reshape(NG_SLOT, D_MODEL)


def _pick_tile(m, cap):
    t = min(m, cap)
    while m % t:
        t //= 2
    return t


def kernel(x_prompt, x_sample, cache_cmp_kv, cache_sel_kv, cache_win_kv, state_ret, cache_mem_kv, page_table,
           mem_prompt, rel_table, norm_pre, norm_post, norm_mem, w_in, ret_norm, w_ret_up, cmp_pos, w_cmp1,
           w_cmp2, w_nsa_up, w_mem_kv, w_mem_up, w_out):
    batch, seq, _ = x_prompt.shape
    nb = x_sample.shape[0]
    assert x_sample.shape[1] == 1 and norm_pre.shape[0] == 1
    assert seq % TQ == 0 and seq >= WINDOW
    n_pool = cache_cmp_kv.shape[1]
    n_pages = page_table.shape[1]
    past = n_pages * PAGE_SIZE
    n_mem = mem_prompt.shape[1]
    assert n_pages % PAGES_PER_STEP == 0 and cache_win_kv.shape[2] == WINDOW

    w_a = w_in[0].T
    w_ng = _layout_w_ng(w_a)
    kw = CMP_STRIDE * NSA_DK
    w1 = w_cmp1[0].reshape(2, CMP_BLOCK * NSA_DK, NSA_DK).astype(BF16)
    w1a, w1b = w1[:, :kw], w1[:, kw:]
    w1ab = jnp.concatenate([w1a, w1b], axis=2)
    w2 = w_cmp2[0].astype(BF16)
    pos8 = jnp.pad(cmp_pos[0].reshape(2, 1, CMP_BLOCK * NSA_DK), ((0, 0), (0, 7), (0, 0)))
    w_ret = w_ret_up[0].astype(BF16)
    w_nsa = w_nsa_up[0].astype(BF16)
    w_mem = w_mem_up[0].astype(BF16)
    w_o = w_out[0].astype(BF16)

    m_p = batch * seq
    xp = x_prompt.reshape(m_p, D_MODEL)
    hp = _rmsnorm(xp, norm_pre[0], _pick_tile(m_p, 1024))
    xs = x_sample.reshape(nb, D_MODEL)
    hs = _rmsnorm(xs, norm_pre[0], nb)
    z, zs_head = _proj(hp, w_a, _pick_tile(m_p, 1024), PROJ_TN, COL_KVC, transposed=True, rider=hs)
    zkv, kvc_rows, kvs_rows, kvw_rows, zs_kv = _proj_kv(hp, w_a, _pick_tile(seq, WINDOW), seq, hs)
    zb, zsb = _proj_tail(hp, w_a, w_ng, _pick_tile(m_p, 1024), hs)

    a_ret, ret_state_p = _retention_prompt(z, ret_norm[0], batch, seq)

    ncp = max(LANE, -(-(seq // CMP_STRIDE) // LANE) * LANE)
    ac = _cmp_stage1_dense(zkv, w1ab, batch, seq)
    kcvc = _cmp_stage2(ac, pos8, w1a, w1b, w2, 1)
    if ncp > kcvc.shape[3]:
        kcvc = jnp.pad(kcvc, ((0, 0), (0, 0), (0, 0), (0, ncp - kcvc.shape[3]), (0, 0)))
    bias_d = _bias_by_dist(rel_table, BIAS_DISTS)
    a_nsa = _nsa_prompt(z, zkv, zb, kcvc, bias_d, batch, seq)

    hm = _rmsnorm(mem_prompt.reshape(batch * n_mem, D_MODEL), norm_mem[0], _pick_tile(batch * n_mem, 512))
    mem_kv_p = _proj(hm, w_mem_kv[0], _pick_tile(batch * n_mem, 512), PROJ_TN)
    a_mem = _mem_prompt(zb, mem_kv_p, batch, seq, _pick_tile(seq, 512))

    merged = _merge(a_ret, a_nsa, a_mem, w_ret, w_nsa, w_mem, zb, _pick_tile(m_p, 1024), 512)
    y_p = _out_proj(merged, w_o, xp, norm_post[0], _pick_tile(m_p, 512)).reshape(batch, seq, D_MODEL)

    kv_shape = (1, batch, seq, 2, NSA_KV_HEADS, NSA_DK)
    new_cmp_p = kvc_rows.reshape(kv_shape)
    new_sel_p = kvs_rows.reshape(kv_shape)
    new_win_p = kvw_rows.reshape(1, batch, WINDOW, 2, NSA_KV_HEADS, NSA_DK)
    new_ret_p = ret_state_p[None]
    new_mem_p = mem_kv_p.reshape(1, batch, n_mem, 2, MEM_HEADS, MEM_DH)

    zs = jnp.concatenate([zs_head, zs_kv], axis=1)
    z3 = zs.reshape(nb, 1, PROJ_A)
    z3b = zsb.reshape(nb, 1, PROJ_B)

    a_ret_s, ret_state_s = _retention_sample(z3, state_ret[0], ret_norm[0], past)

    cache_c = cache_cmp_kv[0].reshape(n_pool, PAGE_SIZE // CMP_STRIDE, CMP_STRIDE, 2 * NSA_KV_HEADS, NSA_DK)
    cache_s = cache_sel_kv[0].reshape(n_pool, PAGE_SIZE, 2 * NSA_KV_HEADS, NSA_DK)
    ac_s = _cmp_stage1_paged(cache_c, page_table, w1ab)
    kcvc_s = _cmp_stage2(ac_s, pos8, w1a, w1b, w2, NSA_KV_HEADS)
    ncs = past // CMP_STRIDE
    kcvc_s = kcvc_s.reshape(nb, 2, ncs, KV_W)
    ns_s = past // SEL_BLOCK + 1
    nsp_s = -(-ns_s // LANE) * LANE
    assert past >= WINDOW and past >= REL_MAX_DIST and BIAS_DISTS > WINDOW
    far_s = bias_d[..., REL_MAX_DIST:REL_MAX_DIST + 1]
    hg = (NSA_KV_HEADS, NSA_GROUP)
    n_valid = (past - (CMP_BLOCK - 1)) // CMP_STRIDE + 1
    strided = bias_d[..., (past - (CMP_BLOCK - 1)) % CMP_STRIDE::CMP_STRIDE]
    n_tab = strided.shape[-1]
    assert n_valid >= n_tab and n_tab * CMP_STRIDE > REL_MAX_DIST + CMP_STRIDE and ncs >= n_valid
    bias_cs = jnp.concatenate([jnp.broadcast_to(far_s, hg + (n_valid - n_tab,)), strided[..., ::-1],
                               jnp.full(hg + (ncs - n_valid,), NEG_INF, F32)], axis=-1)
    bias_cs = _pad_group_rows(bias_cs, 1)
    nblk = jnp.arange(ncs)[:, None]
    sblk = jnp.arange(nsp_s)[None, :]
    ov_s = ((nblk >= 4 * sblk - 1) & (nblk <= 4 * sblk + 3)).astype(BF16)
    o_cmp_s, idx_s = _nsa_sample_cmp(z3, kcvc_s, bias_cs, ov_s, ns_s)
    n_sel = min(SEL_TOPK, ns_s)
    idx = idx_s[:, :, 0, :n_sel]

    hg = (NSA_KV_HEADS, NSA_GROUP)
    n_key = ns_s * SEL_BLOCK
    bias_sel = jnp.concatenate([jnp.broadcast_to(far_s, hg + (past + 1 - REL_MAX_DIST,)),
                                bias_d[..., :REL_MAX_DIST][..., ::-1],
                                jnp.full(hg + (n_key - past - 1,), NEG_INF, F32)], axis=-1)
    bias_sel = _pad_group_rows(bias_sel.reshape(hg + (ns_s, SEL_BLOCK)).transpose(0, 2, 1, 3), 2)
    own_k = jnp.arange(2 * NSA_KV_HEADS)[None, :] == jnp.arange(NSA_KV_HEADS)[:, None]
    bias_sel = jnp.where(own_k[:, None, None, None, :], bias_sel[..., None], NEG_INF)
    bias_sel = bias_sel.reshape(NSA_KV_HEADS, ns_s, SROWS, SEL_BLOCK * 2 * NSA_KV_HEADS)
    kvs_new = zs[:, COL_KVS:COL_KVS + 2 * KV_W].reshape(nb, 2 * NSA_KV_HEADS, NSA_DK)

    win_buf = cache_win_kv[0].reshape(nb, WINDOW, 2 * NSA_KV_HEADS, NSA_DK)
    bias_w = jnp.concatenate([jnp.full(hg + (1,), NEG_INF, F32), bias_d[..., 1:WINDOW][..., ::-1]], axis=-1)
    bias_w = jnp.where(own_k[:, None, None, :], _pad_group_rows(bias_w, 1)[..., None], NEG_INF)
    bias_w = bias_w.reshape(NSA_KV_HEADS, SROWS, WINDOW * 2 * NSA_KV_HEADS)
    kvw_new = zs[:, COL_KVW:COL_KVW + 2 * KV_W].reshape(nb, 2 * NSA_KV_HEADS, NSA_DK)
    o_win_s = _nsa_sample_win(z3, win_buf, kvw_new, bias_w, _pad_group_rows(bias_d[..., 0:1], 1))
    a_nsa_s = _nsa_sample_sel(z3, cache_s, kvs_new, page_table, idx, bias_sel, ns_s, o_cmp_s, o_win_s, z3b)

    mem_kv_s = cache_mem_kv[0].reshape(nb, n_mem, 2 * MEM_W)
    a_mem_s = _mem_sample(z3b, mem_kv_s)

    merged_s = _merge(a_ret_s.reshape(nb, RET_W), a_nsa_s.reshape(nb, NSA_W), a_mem_s.reshape(nb, MEM_W),
                      w_ret, w_nsa, w_mem, zsb, nb, 512)
    y_s = _out_proj(merged_s, w_o, xs, norm_post[0], nb).reshape(nb, 1, D_MODEL)

    kvs_shape = (1, nb, 1, 2, NSA_KV_HEADS, NSA_DK)
    new_cmp_s = zs[:, COL_KVC:COL_KVC + 2 * KV_W].reshape(kvs_shape)
    new_sel_s = zs[:, COL_KVS:COL_KVS + 2 * KV_W].reshape(kvs_shape)
    kvw_s = zs[:, COL_KVW:COL_KVW + 2 * KV_W].reshape(nb, 1, 2, NSA_KV_HEADS, NSA_DK)
    new_win_s = jnp.concatenate([cache_win_kv[0][:, 1:], kvw_s], axis=1)[None]
    new_ret_s = ret_state_s[None]

    return (y_p, y_s, new_cmp_p, new_sel_p, new_win_p, new_ret_p, new_mem_p,
            new_cmp_s, new_sel_s, new_win_s, new_ret_s)
```
